```python
import math
import jax, jax.numpy as jnp
from jax import lax
import numpy as np


D_MODEL = 1024
BATCH = 32
SEQ = 2048
DEPTH = 1

SSD_HEAD_DIM = 64
SSD_WIDTH = D_MODEL
SSD_HEADS = SSD_WIDTH // SSD_HEAD_DIM
SSD_GROUPS = 4
SSD_STATE = 128
SSD_CONV = 5
SSD_CHUNK = 128
XBC_WIDTH = SSD_WIDTH + 2 * SSD_GROUPS * SSD_STATE
S5_WIDTH = D_MODEL // 2
S5_GROUP_CH = 16
S5_GROUPS = S5_WIDTH // S5_GROUP_CH
S5_STATE = 64
MIX_WIDTH = SSD_WIDTH + S5_WIDTH
IN_PROJ_WIDTH = SSD_WIDTH + XBC_WIDTH + 2 * SSD_HEADS + S5_WIDTH
D_FF = 256 * ((8 * D_MODEL // 3 + 255) // 256)
FFN_CONV = 3
EPS = 1e-6

kernel_name = 'hybrid_ssd_s5_encoder_block'


def rms_norm(x, w):
    xf = x.astype(jnp.float32)
    xf = xf * lax.rsqrt(jnp.mean(xf * xf, axis=-1, keepdims=True) + EPS)
    return (xf * w.astype(jnp.float32)).astype(x.dtype)


def dwconv(x, w, b):
    k = w.shape[0]
    y = lax.conv_general_dilated(
        x, w[:, None, :].astype(x.dtype), window_strides=(1,),
        padding=[(k // 2, k // 2)], dimension_numbers=('NWC', 'WIO', 'NWC'),
        feature_group_count=x.shape[-1])
    return y + b.astype(x.dtype)


def ssd_scan(xs, dt, a_log, bm, cm):
    bsz, l, h, p = xs.shape
    g, n = bm.shape[2], bm.shape[3]
    r = h // g
    nc = l // SSD_CHUNK
    a = -jnp.exp(a_log.astype(jnp.float32))
    xdt = (xs.astype(jnp.float32) * dt[..., None]).reshape(bsz, nc, SSD_CHUNK, g, r, p)
    bc = bm.astype(jnp.float32).reshape(bsz, nc, SSD_CHUNK, g, n)
    cc = cm.astype(jnp.float32).reshape(bsz, nc, SSD_CHUNK, g, n)
    a_cs = jnp.cumsum((dt * a).reshape(bsz, nc, SSD_CHUNK, g, r), axis=2)
    diff = a_cs[:, :, :, None] - a_cs[:, :, None]
    mask = jnp.tril(jnp.ones((SSD_CHUNK, SSD_CHUNK), dtype=bool))[None, None, :, :, None, None]
    seg = jnp.exp(jnp.where(mask, diff, -jnp.inf))
    scores = jnp.einsum('bcqgn,bcsgn->bcqsg', cc, bc)
    y_diag = jnp.einsum('bcqsgr,bcsgrp->bcqgrp', scores[..., None] * seg, xdt)
    decay_states = jnp.exp(a_cs[:, :, -1:] - a_cs)
    states = jnp.einsum('bcsgn,bcsgr,bcsgrp->bcgrpn', bc, decay_states, xdt)
    chunk_decay = jnp.exp(a_cs[:, :, -1])

    def step(carry, inp):
        dec, st = inp
        return dec[..., None, None] * carry + st, carry

    init = jnp.zeros((bsz, g, r, p, n), jnp.float32)
    _, prev = lax.scan(step, init, (jnp.moveaxis(chunk_decay, 1, 0), jnp.moveaxis(states, 1, 0)))
    prev = jnp.moveaxis(prev, 0, 1)
    y_off = jnp.einsum('bcqgn,bcgrpn,bcqgr->bcqgrp', cc, prev, jnp.exp(a_cs))
    return (y_diag + y_off).reshape(bsz, l, h, p)


def ssd_mixer(z, xbc, dt_raw, conv_w, conv_b, dt_bias_f, dt_bias_b, a_log_f, a_log_b, d, norm_w):
    bsz, l, _ = z.shape
    xbc = jax.nn.silu(dwconv(xbc, conv_w, conv_b))
    xs, bm, cm = jnp.split(xbc, [SSD_WIDTH, SSD_WIDTH + SSD_GROUPS * SSD_STATE], axis=-1)
    xs = xs.reshape(bsz, l, SSD_HEADS, SSD_HEAD_DIM)
    bm = bm.reshape(bsz, l, SSD_GROUPS, SSD_STATE)
    cm = cm.reshape(bsz, l, SSD_GROUPS, SSD_STATE)
    dt_raw = dt_raw.astype(jnp.float32)
    dtf = jax.nn.softplus(dt_raw[..., :SSD_HEADS] + dt_bias_f.astype(jnp.float32))
    dtb = jax.nn.softplus(dt_raw[..., SSD_HEADS:] + dt_bias_b.astype(jnp.float32))
    y_f = ssd_scan(xs, dtf, a_log_f, bm, cm)
    y_b = jnp.flip(ssd_scan(jnp.flip(xs, 1), jnp.flip(dtb, 1), a_log_b,
                            jnp.flip(bm, 1), jnp.flip(cm, 1)), 1)
    y = y_f + y_b + d.astype(jnp.float32)[:, None] * xs.astype(jnp.float32)
    y = y.reshape(bsz, l, SSD_WIDTH) * jax.nn.silu(z.astype(jnp.float32))
    return rms_norm(y, norm_w).astype(z.dtype)


def _complex_combine(left, right):
    a1r, a1i, b1r, b1i = left
    a2r, a2i, b2r, b2i = right
    return (a2r * a1r - a2i * a1i,
            a2r * a1i + a2i * a1r,
            a2r * b1r - a2i * b1i + b2r,
            a2r * b1i + a2i * b1r + b2i)


def s5_scan(u, lam_re, lam_im, log_step, b_re, b_im, c_re, c_im, reverse):
    lam_re = lam_re.astype(jnp.float32)
    lam_im = lam_im.astype(jnp.float32)
    step = jnp.exp(log_step.astype(jnp.float32))[:, None]
    mag = jnp.exp(lam_re * step)
    ar = mag * jnp.cos(lam_im * step)
    ai = mag * jnp.sin(lam_im * step)
    den = lam_re * lam_re + lam_im * lam_im
    cr = ((ar - 1.0) * lam_re + ai * lam_im) / den
    ci = (ai * lam_re - (ar - 1.0) * lam_im) / den
    b_re = b_re.astype(jnp.float32)
    b_im = b_im.astype(jnp.float32)
    bbr = cr[..., None] * b_re - ci[..., None] * b_im
    bbi = cr[..., None] * b_im + ci[..., None] * b_re
    bur = jnp.einsum('blgc,gpc->lbgp', u, bbr)
    bui = jnp.einsum('blgc,gpc->lbgp', u, bbi)
    seq_len = u.shape[1]
    a_r = jnp.broadcast_to(ar[None, None], (seq_len, 1) + ar.shape)
    a_i = jnp.broadcast_to(ai[None, None], (seq_len, 1) + ai.shape)
    _, _, hr, hi = lax.associative_scan(_complex_combine, (a_r, a_i, bur, bui),
                                        reverse=reverse, axis=0)
    return (jnp.einsum('lbgp,gcp->blgc', hr, c_re.astype(jnp.float32))
            - jnp.einsum('lbgp,gcp->blgc', hi, c_im.astype(jnp.float32)))


def s5_mixer(u, lam_re_f, lam_im_f, log_step_f, lam_re_b, lam_im_b, log_step_b,
             b_re, b_im, c_re_f, c_im_f, c_re_b, c_im_b, d, glu_w, glu_b, norm_w):
    bsz, l, _ = u.shape
    uf = u.astype(jnp.float32).reshape(bsz, l, S5_GROUPS, S5_GROUP_CH)
    y = (s5_scan(uf, lam_re_f, lam_im_f, log_step_f, b_re, b_im, c_re_f, c_im_f, False)
         + s5_scan(uf, lam_re_b, lam_im_b, log_step_b, b_re, b_im, c_re_b, c_im_b, True)
         + d.astype(jnp.float32).reshape(S5_GROUPS, S5_GROUP_CH) * uf)
    g = jax.nn.gelu(y)
    gl = jnp.einsum('blgc,gcd->blgd', g, glu_w.astype(jnp.float32)) + glu_b.astype(jnp.float32)
    out = gl[..., :S5_GROUP_CH] * jax.nn.sigmoid(gl[..., S5_GROUP_CH:])
    return rms_norm(out.reshape(bsz, l, S5_WIDTH), norm_w).astype(u.dtype)


def conv_ffn(h, w_up, conv_w, conv_b, w_down):
    up = dwconv(h @ w_up, conv_w, conv_b)
    val, gate = jnp.split(up, 2, axis=-1)
    return (jax.nn.silu(gate) * val) @ w_down


def _fwd_setup_inputs(seed: int = 0) -> dict:
    key = jax.random.key(seed)
    ks = iter(jax.random.split(key, 48))
    f32 = jnp.float32

    def nrm(shape, scale):
        return scale * jax.random.normal(next(ks), shape, f32)

    def unif(shape, lo, hi):
        return jax.random.uniform(next(ks), shape, f32, minval=lo, maxval=hi)

    x = jax.random.normal(next(ks), (BATCH, SEQ, D_MODEL), f32)
    norm_mix_w = 1.0 + nrm((DEPTH, D_MODEL), 0.02)
    w_in = nrm((DEPTH, D_MODEL, IN_PROJ_WIDTH), D_MODEL ** -0.5)
    ssd_conv_w = nrm((DEPTH, SSD_CONV, XBC_WIDTH), SSD_CONV ** -0.5)
    ssd_conv_b = nrm((DEPTH, XBC_WIDTH), 0.02)
    dt0_f = jnp.exp(unif((DEPTH, SSD_HEADS), math.log(1e-3), math.log(1e-1)))
    ssd_dt_bias_fwd = dt0_f + jnp.log(-jnp.expm1(-dt0_f))
    dt0_b = jnp.exp(unif((DEPTH, SSD_HEADS), math.log(1e-3), math.log(1e-1)))
    ssd_dt_bias_bwd = dt0_b + jnp.log(-jnp.expm1(-dt0_b))
    ssd_a_log_fwd = jnp.log(unif((DEPTH, SSD_HEADS), 1.0, 16.0))
    ssd_a_log_bwd = jnp.log(unif((DEPTH, SSD_HEADS), 1.0, 16.0))
    ssd_d = 1.0 + nrm((DEPTH, SSD_HEADS), 0.1)
    ssd_norm_w = 1.0 + nrm((DEPTH, SSD_WIDTH), 0.02)
    n_idx = math.pi * jnp.arange(S5_STATE, dtype=f32)
    s5_lambda_re_fwd = -0.5 + nrm((DEPTH, S5_GROUPS, S5_STATE), 0.01)
    s5_lambda_im_fwd = n_idx + nrm((DEPTH, S5_GROUPS, S5_STATE), 0.01)
    s5_log_step_fwd = unif((DEPTH, S5_GROUPS), math.log(1e-3), math.log(1e-1))
    s5_lambda_re_bwd = -0.5 + nrm((DEPTH, S5_GROUPS, S5_STATE), 0.01)
    s5_lambda_im_bwd = n_idx + nrm((DEPTH, S5_GROUPS, S5_STATE), 0.01)
    s5_log_step_bwd = unif((DEPTH, S5_GROUPS), math.log(1e-3), math.log(1e-1))
    s5_b_re = nrm((DEPTH, S5_GROUPS, S5_STATE, S5_GROUP_CH), (2 * S5_GROUP_CH) ** -0.5)
    s5_b_im = nrm((DEPTH, S5_GROUPS, S5_STATE, S5_GROUP_CH), (2 * S5_GROUP_CH) ** -0.5)
    s5_c_re_fwd = nrm((DEPTH, S5_GROUPS, S5_GROUP_CH, S5_STATE), (2 * S5_STATE) ** -0.5)
    s5_c_im_fwd = nrm((DEPTH, S5_GROUPS, S5_GROUP_CH, S5_STATE), (2 * S5_STATE) ** -0.5)
    s5_c_re_bwd = nrm((DEPTH, S5_GROUPS, S5_GROUP_CH, S5_STATE), (2 * S5_STATE) ** -0.5)
    s5_c_im_bwd = nrm((DEPTH, S5_GROUPS, S5_GROUP_CH, S5_STATE), (2 * S5_STATE) ** -0.5)
    s5_d = nrm((DEPTH, S5_WIDTH), 0.5)
    s5_glu_w = nrm((DEPTH, S5_GROUPS, S5_GROUP_CH, 2 * S5_GROUP_CH), S5_GROUP_CH ** -0.5)
    s5_glu_b = nrm((DEPTH, S5_GROUPS, 2 * S5_GROUP_CH), 0.02)
    s5_norm_w = 1.0 + nrm((DEPTH, S5_WIDTH), 0.02)
    w_out = nrm((DEPTH, MIX_WIDTH, D_MODEL), MIX_WIDTH ** -0.5)
    norm_ffn_w = 1.0 + nrm((DEPTH, D_MODEL), 0.02)
    ffn_w_up = nrm((DEPTH, D_MODEL, 2 * D_FF), D_MODEL ** -0.5)
    ffn_conv_w = nrm((DEPTH, FFN_CONV, 2 * D_FF), FFN_CONV ** -0.5)
    ffn_conv_b = nrm((DEPTH, 2 * D_FF), 0.02)
    ffn_w_down = nrm((DEPTH, D_FF, D_MODEL), D_FF ** -0.5)
    norm_final_w = 1.0 + nrm((D_MODEL,), 0.02)
    return {
        'x': x, 'norm_mix_w': norm_mix_w, 'w_in': w_in,
        'ssd_conv_w': ssd_conv_w, 'ssd_conv_b': ssd_conv_b,
        'ssd_dt_bias_fwd': ssd_dt_bias_fwd, 'ssd_dt_bias_bwd': ssd_dt_bias_bwd,
        'ssd_a_log_fwd': ssd_a_log_fwd, 'ssd_a_log_bwd': ssd_a_log_bwd,
        'ssd_d': ssd_d, 'ssd_norm_w': ssd_norm_w,
        's5_lambda_re_fwd': s5_lambda_re_fwd, 's5_lambda_im_fwd': s5_lambda_im_fwd,
        's5_log_step_fwd': s5_log_step_fwd,
        's5_lambda_re_bwd': s5_lambda_re_bwd, 's5_lambda_im_bwd': s5_lambda_im_bwd,
        's5_log_step_bwd': s5_log_step_bwd,
        's5_b_re': s5_b_re, 's5_b_im': s5_b_im,
        's5_c_re_fwd': s5_c_re_fwd, 's5_c_im_fwd': s5_c_im_fwd,
        's5_c_re_bwd': s5_c_re_bwd, 's5_c_im_bwd': s5_c_im_bwd,
        's5_d': s5_d, 's5_glu_w': s5_glu_w, 's5_glu_b': s5_glu_b, 's5_norm_w': s5_norm_w,
        'w_out': w_out, 'norm_ffn_w': norm_ffn_w,
        'ffn_w_up': ffn_w_up, 'ffn_conv_w': ffn_conv_w, 'ffn_conv_b': ffn_conv_b,
        'ffn_w_down': ffn_w_down, 'norm_final_w': norm_final_w,
    }


def _fwd_reference(x, norm_mix_w, w_in, ssd_conv_w, ssd_conv_b, ssd_dt_bias_fwd, ssd_dt_bias_bwd,
              ssd_a_log_fwd, ssd_a_log_bwd, ssd_d, ssd_norm_w,
              s5_lambda_re_fwd, s5_lambda_im_fwd, s5_log_step_fwd,
              s5_lambda_re_bwd, s5_lambda_im_bwd, s5_log_step_bwd,
              s5_b_re, s5_b_im, s5_c_re_fwd, s5_c_im_fwd, s5_c_re_bwd, s5_c_im_bwd,
              s5_d, s5_glu_w, s5_glu_b, s5_norm_w, w_out, norm_ffn_w,
              ffn_w_up, ffn_conv_w, ffn_conv_b, ffn_w_down, norm_final_w):
    h = x
    cuts = [SSD_WIDTH, SSD_WIDTH + XBC_WIDTH, SSD_WIDTH + XBC_WIDTH + 2 * SSD_HEADS]
    for i in range(DEPTH):
        hn = rms_norm(h, norm_mix_w[i])
        proj = hn @ w_in[i]
        z, xbc, dt_raw, u = jnp.split(proj, cuts, axis=-1)
        y_ssd = ssd_mixer(z, xbc, dt_raw, ssd_conv_w[i], ssd_conv_b[i],
                          ssd_dt_bias_fwd[i], ssd_dt_bias_bwd[i],
                          ssd_a_log_fwd[i], ssd_a_log_bwd[i], ssd_d[i], ssd_norm_w[i])
        y_s5 = s5_mixer(u, s5_lambda_re_fwd[i], s5_lambda_im_fwd[i], s5_log_step_fwd[i],
                        s5_lambda_re_bwd[i], s5_lambda_im_bwd[i], s5_log_step_bwd[i],
                        s5_b_re[i], s5_b_im[i], s5_c_re_fwd[i], s5_c_im_fwd[i],
                        s5_c_re_bwd[i], s5_c_im_bwd[i], s5_d[i], s5_glu_w[i], s5_glu_b[i],
                        s5_norm_w[i])
        h = h + jnp.concatenate([y_ssd, y_s5], axis=-1) @ w_out[i]
        h = h + conv_ffn(rms_norm(h, norm_ffn_w[i]), ffn_w_up[i], ffn_conv_w[i],
                         ffn_conv_b[i], ffn_w_down[i])
    return rms_norm(h, norm_final_w)


import jax as _jax
import jax.numpy as _jnp

TWIN_FORMAT = 'train_step'
FWD_PARAMS = ['x', 'norm_mix_w', 'w_in', 'ssd_conv_w', 'ssd_conv_b', 'ssd_dt_bias_fwd', 'ssd_dt_bias_bwd', 'ssd_a_log_fwd', 'ssd_a_log_bwd', 'ssd_d', 'ssd_norm_w', 's5_lambda_re_fwd', 's5_lambda_im_fwd', 's5_log_step_fwd', 's5_lambda_re_bwd', 's5_lambda_im_bwd', 's5_log_step_bwd', 's5_b_re', 's5_b_im', 's5_c_re_fwd', 's5_c_im_fwd', 's5_c_re_bwd', 's5_c_im_bwd', 's5_d', 's5_glu_w', 's5_glu_b', 's5_norm_w', 'w_out', 'norm_ffn_w', 'ffn_w_up', 'ffn_conv_w', 'ffn_conv_b', 'ffn_w_down', 'norm_final_w']
TWIN_WEIGHTS = ['norm_mix_w', 'w_in', 'ssd_conv_w', 'ssd_conv_b', 'ssd_dt_bias_fwd', 'ssd_dt_bias_bwd', 'ssd_a_log_fwd', 'ssd_a_log_bwd', 'ssd_d', 'ssd_norm_w', 's5_lambda_re_fwd', 's5_lambda_im_fwd', 's5_log_step_fwd', 's5_lambda_re_bwd', 's5_lambda_im_bwd', 's5_log_step_bwd', 's5_b_re', 's5_b_im', 's5_c_re_fwd', 's5_c_im_fwd', 's5_c_re_bwd', 's5_c_im_bwd', 's5_d', 's5_glu_w', 's5_glu_b', 's5_norm_w', 'w_out', 'norm_ffn_w', 'ffn_w_up', 'ffn_conv_w', 'ffn_conv_b', 'ffn_w_down', 'norm_final_w']
TWIN_DIFF_INPUT = 'x'
TWIN_INPUTS = ['x', 'norm_mix_w', 'w_in', 'ssd_conv_w', 'ssd_conv_b', 'ssd_dt_bias_fwd', 'ssd_dt_bias_bwd', 'ssd_a_log_fwd', 'ssd_a_log_bwd', 'ssd_d', 'ssd_norm_w', 's5_lambda_re_fwd', 's5_lambda_im_fwd', 's5_log_step_fwd', 's5_lambda_re_bwd', 's5_lambda_im_bwd', 's5_log_step_bwd', 's5_b_re', 's5_b_im', 's5_c_re_fwd', 's5_c_im_fwd', 's5_c_re_bwd', 's5_c_im_bwd', 's5_d', 's5_glu_w', 's5_glu_b', 's5_norm_w', 'w_out', 'norm_ffn_w', 'ffn_w_up', 'ffn_conv_w', 'ffn_conv_b', 'ffn_w_down', 'norm_final_w', 'loss_target', 'm_norm_mix_w', 'm_w_in', 'm_ssd_conv_w', 'm_ssd_conv_b', 'm_ssd_dt_bias_fwd', 'm_ssd_dt_bias_bwd', 'm_ssd_a_log_fwd', 'm_ssd_a_log_bwd', 'm_ssd_d', 'm_ssd_norm_w', 'm_s5_lambda_re_fwd', 'm_s5_lambda_im_fwd', 'm_s5_log_step_fwd', 'm_s5_lambda_re_bwd', 'm_s5_lambda_im_bwd', 'm_s5_log_step_bwd', 'm_s5_b_re', 'm_s5_b_im', 'm_s5_c_re_fwd', 'm_s5_c_im_fwd', 'm_s5_c_re_bwd', 'm_s5_c_im_bwd', 'm_s5_d', 'm_s5_glu_w', 'm_s5_glu_b', 'm_s5_norm_w', 'm_w_out', 'm_norm_ffn_w', 'm_ffn_w_up', 'm_ffn_conv_w', 'm_ffn_conv_b', 'm_ffn_w_down', 'm_norm_final_w', 'v_norm_mix_w', 'v_w_in', 'v_ssd_conv_w', 'v_ssd_conv_b', 'v_ssd_dt_bias_fwd', 'v_ssd_dt_bias_bwd', 'v_ssd_a_log_fwd', 'v_ssd_a_log_bwd', 'v_ssd_d', 'v_ssd_norm_w', 'v_s5_lambda_re_fwd', 'v_s5_lambda_im_fwd', 'v_s5_log_step_fwd', 'v_s5_lambda_re_bwd', 'v_s5_lambda_im_bwd', 'v_s5_log_step_bwd', 'v_s5_b_re', 'v_s5_b_im', 'v_s5_c_re_fwd', 'v_s5_c_im_fwd', 'v_s5_c_re_bwd', 'v_s5_c_im_bwd', 'v_s5_d', 'v_s5_glu_w', 'v_s5_glu_b', 'v_s5_norm_w', 'v_w_out', 'v_norm_ffn_w', 'v_ffn_w_up', 'v_ffn_conv_w', 'v_ffn_conv_b', 'v_ffn_w_down', 'v_norm_final_w']
TWIN_OUTPUTS = ['loss', 'grad_x', 'grad_norm_mix_w', 'grad_w_in', 'grad_ssd_conv_w', 'grad_ssd_conv_b', 'grad_ssd_dt_bias_fwd', 'grad_ssd_dt_bias_bwd', 'grad_ssd_a_log_fwd', 'grad_ssd_a_log_bwd', 'grad_ssd_d', 'grad_ssd_norm_w', 'grad_s5_lambda_re_fwd', 'grad_s5_lambda_im_fwd', 'grad_s5_log_step_fwd', 'grad_s5_lambda_re_bwd', 'grad_s5_lambda_im_bwd', 'grad_s5_log_step_bwd', 'grad_s5_b_re', 'grad_s5_b_im', 'grad_s5_c_re_fwd', 'grad_s5_c_im_fwd', 'grad_s5_c_re_bwd', 'grad_s5_c_im_bwd', 'grad_s5_d', 'grad_s5_glu_w', 'grad_s5_glu_b', 'grad_s5_norm_w', 'grad_w_out', 'grad_norm_ffn_w', 'grad_ffn_w_up', 'grad_ffn_conv_w', 'grad_ffn_conv_b', 'grad_ffn_w_down', 'grad_norm_final_w', 'delta_norm_mix_w', 'delta_w_in', 'delta_ssd_conv_w', 'delta_ssd_conv_b', 'delta_ssd_dt_bias_fwd', 'delta_ssd_dt_bias_bwd', 'delta_ssd_a_log_fwd', 'delta_ssd_a_log_bwd', 'delta_ssd_d', 'delta_ssd_norm_w', 'delta_s5_lambda_re_fwd', 'delta_s5_lambda_im_fwd', 'delta_s5_log_step_fwd', 'delta_s5_lambda_re_bwd', 'delta_s5_lambda_im_bwd', 'delta_s5_log_step_bwd', 'delta_s5_b_re', 'delta_s5_b_im', 'delta_s5_c_re_fwd', 'delta_s5_c_im_fwd', 'delta_s5_c_re_bwd', 'delta_s5_c_im_bwd', 'delta_s5_d', 'delta_s5_glu_w', 'delta_s5_glu_b', 'delta_s5_norm_w', 'delta_w_out', 'delta_norm_ffn_w', 'delta_ffn_w_up', 'delta_ffn_conv_w', 'delta_ffn_conv_b', 'delta_ffn_w_down', 'delta_norm_final_w', 'new_m_norm_mix_w', 'new_m_w_in', 'new_m_ssd_conv_w', 'new_m_ssd_conv_b', 'new_m_ssd_dt_bias_fwd', 'new_m_ssd_dt_bias_bwd', 'new_m_ssd_a_log_fwd', 'new_m_ssd_a_log_bwd', 'new_m_ssd_d', 'new_m_ssd_norm_w', 'new_m_s5_lambda_re_fwd', 'new_m_s5_lambda_im_fwd', 'new_m_s5_log_step_fwd', 'new_m_s5_lambda_re_bwd', 'new_m_s5_lambda_im_bwd', 'new_m_s5_log_step_bwd', 'new_m_s5_b_re', 'new_m_s5_b_im', 'new_m_s5_c_re_fwd', 'new_m_s5_c_im_fwd', 'new_m_s5_c_re_bwd', 'new_m_s5_c_im_bwd', 'new_m_s5_d', 'new_m_s5_glu_w', 'new_m_s5_glu_b', 'new_m_s5_norm_w', 'new_m_w_out', 'new_m_norm_ffn_w', 'new_m_ffn_w_up', 'new_m_ffn_conv_w', 'new_m_ffn_conv_b', 'new_m_ffn_w_down', 'new_m_norm_final_w', 'new_v_norm_mix_w', 'new_v_w_in', 'new_v_ssd_conv_w', 'new_v_ssd_conv_b', 'new_v_ssd_dt_bias_fwd', 'new_v_ssd_dt_bias_bwd', 'new_v_ssd_a_log_fwd', 'new_v_ssd_a_log_bwd', 'new_v_ssd_d', 'new_v_ssd_norm_w', 'new_v_s5_lambda_re_fwd', 'new_v_s5_lambda_im_fwd', 'new_v_s5_log_step_fwd', 'new_v_s5_lambda_re_bwd', 'new_v_s5_lambda_im_bwd', 'new_v_s5_log_step_bwd', 'new_v_s5_b_re', 'new_v_s5_b_im', 'new_v_s5_c_re_fwd', 'new_v_s5_c_im_fwd', 'new_v_s5_c_re_bwd', 'new_v_s5_c_im_bwd', 'new_v_s5_d', 'new_v_s5_glu_w', 'new_v_s5_glu_b', 'new_v_s5_norm_w', 'new_v_w_out', 'new_v_norm_ffn_w', 'new_v_ffn_w_up', 'new_v_ffn_conv_w', 'new_v_ffn_conv_b', 'new_v_ffn_w_down', 'new_v_norm_final_w']
TWIN_LEAF_KINDS = {'loss': 'loss', 'grad_x': 'grad_x', 'grad_norm_mix_w': 'grad_w', 'grad_w_in': 'grad_w', 'grad_ssd_conv_w': 'grad_w', 'grad_ssd_conv_b': 'grad_w', 'grad_ssd_dt_bias_fwd': 'grad_w', 'grad_ssd_dt_bias_bwd': 'grad_w', 'grad_ssd_a_log_fwd': 'grad_w', 'grad_ssd_a_log_bwd': 'grad_w', 'grad_ssd_d': 'grad_w', 'grad_ssd_norm_w': 'grad_w', 'grad_s5_lambda_re_fwd': 'grad_w', 'grad_s5_lambda_im_fwd': 'grad_w', 'grad_s5_log_step_fwd': 'grad_w', 'grad_s5_lambda_re_bwd': 'grad_w', 'grad_s5_lambda_im_bwd': 'grad_w', 'grad_s5_log_step_bwd': 'grad_w', 'grad_s5_b_re': 'grad_w', 'grad_s5_b_im': 'grad_w', 'grad_s5_c_re_fwd': 'grad_w', 'grad_s5_c_im_fwd': 'grad_w', 'grad_s5_c_re_bwd': 'grad_w', 'grad_s5_c_im_bwd': 'grad_w', 'grad_s5_d': 'grad_w', 'grad_s5_glu_w': 'grad_w', 'grad_s5_glu_b': 'grad_w', 'grad_s5_norm_w': 'grad_w', 'grad_w_out': 'grad_w', 'grad_norm_ffn_w': 'grad_w', 'grad_ffn_w_up': 'grad_w', 'grad_ffn_conv_w': 'grad_w', 'grad_ffn_conv_b': 'grad_w', 'grad_ffn_w_down': 'grad_w', 'grad_norm_final_w': 'grad_w', 'delta_norm_mix_w': 'delta_w', 'delta_w_in': 'delta_w', 'delta_ssd_conv_w': 'delta_w', 'delta_ssd_conv_b': 'delta_w', 'delta_ssd_dt_bias_fwd': 'delta_w', 'delta_ssd_dt_bias_bwd': 'delta_w', 'delta_ssd_a_log_fwd': 'delta_w', 'delta_ssd_a_log_bwd': 'delta_w', 'delta_ssd_d': 'delta_w', 'delta_ssd_norm_w': 'delta_w', 'delta_s5_lambda_re_fwd': 'delta_w', 'delta_s5_lambda_im_fwd': 'delta_w', 'delta_s5_log_step_fwd': 'delta_w', 'delta_s5_lambda_re_bwd': 'delta_w', 'delta_s5_lambda_im_bwd': 'delta_w', 'delta_s5_log_step_bwd': 'delta_w', 'delta_s5_b_re': 'delta_w', 'delta_s5_b_im': 'delta_w', 'delta_s5_c_re_fwd': 'delta_w', 'delta_s5_c_im_fwd': 'delta_w', 'delta_s5_c_re_bwd': 'delta_w', 'delta_s5_c_im_bwd': 'delta_w', 'delta_s5_d': 'delta_w', 'delta_s5_glu_w': 'delta_w', 'delta_s5_glu_b': 'delta_w', 'delta_s5_norm_w': 'delta_w', 'delta_w_out': 'delta_w', 'delta_norm_ffn_w': 'delta_w', 'delta_ffn_w_up': 'delta_w', 'delta_ffn_conv_w': 'delta_w', 'delta_ffn_conv_b': 'delta_w', 'delta_ffn_w_down': 'delta_w', 'delta_norm_final_w': 'delta_w', 'new_m_norm_mix_w': 'new_m', 'new_m_w_in': 'new_m', 'new_m_ssd_conv_w': 'new_m', 'new_m_ssd_conv_b': 'new_m', 'new_m_ssd_dt_bias_fwd': 'new_m', 'new_m_ssd_dt_bias_bwd': 'new_m', 'new_m_ssd_a_log_fwd': 'new_m', 'new_m_ssd_a_log_bwd': 'new_m', 'new_m_ssd_d': 'new_m', 'new_m_ssd_norm_w': 'new_m', 'new_m_s5_lambda_re_fwd': 'new_m', 'new_m_s5_lambda_im_fwd': 'new_m', 'new_m_s5_log_step_fwd': 'new_m', 'new_m_s5_lambda_re_bwd': 'new_m', 'new_m_s5_lambda_im_bwd': 'new_m', 'new_m_s5_log_step_bwd': 'new_m', 'new_m_s5_b_re': 'new_m', 'new_m_s5_b_im': 'new_m', 'new_m_s5_c_re_fwd': 'new_m', 'new_m_s5_c_im_fwd': 'new_m', 'new_m_s5_c_re_bwd': 'new_m', 'new_m_s5_c_im_bwd': 'new_m', 'new_m_s5_d': 'new_m', 'new_m_s5_glu_w': 'new_m', 'new_m_s5_glu_b': 'new_m', 'new_m_s5_norm_w': 'new_m', 'new_m_w_out': 'new_m', 'new_m_norm_ffn_w': 'new_m', 'new_m_ffn_w_up': 'new_m', 'new_m_ffn_conv_w': 'new_m', 'new_m_ffn_conv_b': 'new_m', 'new_m_ffn_w_down': 'new_m', 'new_m_norm_final_w': 'new_m', 'new_v_norm_mix_w': 'new_v', 'new_v_w_in': 'new_v', 'new_v_ssd_conv_w': 'new_v', 'new_v_ssd_conv_b': 'new_v', 'new_v_ssd_dt_bias_fwd': 'new_v', 'new_v_ssd_dt_bias_bwd': 'new_v', 'new_v_ssd_a_log_fwd': 'new_v', 'new_v_ssd_a_log_bwd': 'new_v', 'new_v_ssd_d': 'new_v', 'new_v_ssd_norm_w': 'new_v', 'new_v_s5_lambda_re_fwd': 'new_v', 'new_v_s5_lambda_im_fwd': 'new_v', 'new_v_s5_log_step_fwd': 'new_v', 'new_v_s5_lambda_re_bwd': 'new_v', 'new_v_s5_lambda_im_bwd': 'new_v', 'new_v_s5_log_step_bwd': 'new_v', 'new_v_s5_b_re': 'new_v', 'new_v_s5_b_im': 'new_v', 'new_v_s5_c_re_fwd': 'new_v', 'new_v_s5_c_im_fwd': 'new_v', 'new_v_s5_c_re_bwd': 'new_v', 'new_v_s5_c_im_bwd': 'new_v', 'new_v_s5_d': 'new_v', 'new_v_s5_glu_w': 'new_v', 'new_v_s5_glu_b': 'new_v', 'new_v_s5_norm_w': 'new_v', 'new_v_w_out': 'new_v', 'new_v_norm_ffn_w': 'new_v', 'new_v_ffn_w_up': 'new_v', 'new_v_ffn_conv_w': 'new_v', 'new_v_ffn_conv_b': 'new_v', 'new_v_ffn_w_down': 'new_v', 'new_v_norm_final_w': 'new_v'}


def _forward(args):
    return _fwd_reference(*[args[k] for k in FWD_PARAMS])


def _output_shape():
    out = _jax.eval_shape(lambda: _forward(_fwd_setup_inputs(0)))
    return out.shape, out.dtype

N_MICROBATCH = 1
ADAM_LR = 0.001
ADAM_B1 = 0.9
ADAM_B2 = 0.999
ADAM_EPS = 1e-08
ADAM_WD = 0.01
ADAM_STEP = 10
PER_EXAMPLE_BATCH_AXIS = {'x': 0, 'loss_target': 0}
SHARED_INPUTS = []
_WEIGHT_DTYPES = {'norm_mix_w': _jnp.float32, 'w_in': _jnp.float32, 'ssd_conv_w': _jnp.float32, 'ssd_conv_b': _jnp.float32, 'ssd_dt_bias_fwd': _jnp.float32, 'ssd_dt_bias_bwd': _jnp.float32, 'ssd_a_log_fwd': _jnp.float32, 'ssd_a_log_bwd': _jnp.float32, 'ssd_d': _jnp.float32, 'ssd_norm_w': _jnp.float32, 's5_lambda_re_fwd': _jnp.float32, 's5_lambda_im_fwd': _jnp.float32, 's5_log_step_fwd': _jnp.float32, 's5_lambda_re_bwd': _jnp.float32, 's5_lambda_im_bwd': _jnp.float32, 's5_log_step_bwd': _jnp.float32, 's5_b_re': _jnp.float32, 's5_b_im': _jnp.float32, 's5_c_re_fwd': _jnp.float32, 's5_c_im_fwd': _jnp.float32, 's5_c_re_bwd': _jnp.float32, 's5_c_im_bwd': _jnp.float32, 's5_d': _jnp.float32, 's5_glu_w': _jnp.float32, 's5_glu_b': _jnp.float32, 's5_norm_w': _jnp.float32, 'w_out': _jnp.float32, 'norm_ffn_w': _jnp.float32, 'ffn_w_up': _jnp.float32, 'ffn_conv_w': _jnp.float32, 'ffn_conv_b': _jnp.float32, 'ffn_w_down': _jnp.float32, 'norm_final_w': _jnp.float32}
MOMENT_SCALE = {'norm_mix_w': 2.606890e-01, 'w_in': 1.405088e-01, 'ssd_conv_w': 1.190555e-01, 'ssd_conv_b': 1.972832e-01, 'ssd_dt_bias_fwd': 2.201917e-01, 'ssd_dt_bias_bwd': 2.282245e-01, 'ssd_a_log_fwd': 5.749111e-01, 'ssd_a_log_bwd': 3.322031e-01, 'ssd_d': 4.713771e-01, 'ssd_norm_w': 1.710856e-01, 's5_lambda_re_fwd': 2.101231e-02, 's5_lambda_im_fwd': 1.822306e-02, 's5_log_step_fwd': 1.888141e+01, 's5_lambda_re_bwd': 1.627335e-02, 's5_lambda_im_bwd': 1.635117e-02, 's5_log_step_bwd': 1.579196e+01, 's5_b_re': 1.563731e-02, 's5_b_im': 1.515701e-02, 's5_c_re_fwd': 2.202777e-02, 's5_c_im_fwd': 2.183160e-02, 's5_c_re_bwd': 2.257965e-02, 's5_c_im_bwd': 2.081209e-02, 's5_d': 3.328659e-01, 's5_glu_w': 1.112014e-01, 's5_glu_b': 3.829436e-01, 's5_norm_w': 1.682629e-01, 'w_out': 1.945669e-01, 'norm_ffn_w': 1.493745e-01, 'ffn_w_up': 6.096743e-02, 'ffn_conv_w': 6.063833e-02, 'ffn_conv_b': 6.064828e-02, 'ffn_w_down': 9.924350e-02, 'norm_final_w': 6.399652e+01}


def _to_microbatches(a, axis):
    t = _jnp.moveaxis(a, axis, 0)
    t = t.reshape((N_MICROBATCH, t.shape[0] // N_MICROBATCH) + t.shape[1:])
    return _jnp.moveaxis(t, 1, axis + 1)


def setup_inputs(seed: int = 0) -> dict:
    inp = _fwd_setup_inputs(seed)
    key = _jax.random.fold_in(_jax.random.key(seed), 7919)
    shape, _ = _output_shape()
    out = dict(inp)
    out["loss_target"] = _jax.random.normal(_jax.random.fold_in(key, 0), shape, _jnp.float32)
    for i, name in enumerate(TWIN_WEIGHTS):
        w = inp[name].astype(_jnp.float32)
        if MOMENT_SCALE is None:
            s = _jnp.sqrt(_jnp.mean(_jnp.square(w)) + 1e-30)
        else:
            s = MOMENT_SCALE[name]
        km, kv = _jax.random.split(_jax.random.fold_in(key, i + 1))
        out[name] = w
        out["m_" + name] = s * _jax.random.normal(km, w.shape, _jnp.float32)
        out["v_" + name] = (s * s) * _jax.random.uniform(kv, w.shape, _jnp.float32, 0.5, 1.5)
    if N_MICROBATCH > 1:
        for name, axis in PER_EXAMPLE_BATCH_AXIS.items():
            out[name] = _to_microbatches(out[name], axis)
    return {'x': out['x'], 'norm_mix_w': out['norm_mix_w'], 'w_in': out['w_in'], 'ssd_conv_w': out['ssd_conv_w'], 'ssd_conv_b': out['ssd_conv_b'], 'ssd_dt_bias_fwd': out['ssd_dt_bias_fwd'], 'ssd_dt_bias_bwd': out['ssd_dt_bias_bwd'], 'ssd_a_log_fwd': out['ssd_a_log_fwd'], 'ssd_a_log_bwd': out['ssd_a_log_bwd'], 'ssd_d': out['ssd_d'], 'ssd_norm_w': out['ssd_norm_w'], 's5_lambda_re_fwd': out['s5_lambda_re_fwd'], 's5_lambda_im_fwd': out['s5_lambda_im_fwd'], 's5_log_step_fwd': out['s5_log_step_fwd'], 's5_lambda_re_bwd': out['s5_lambda_re_bwd'], 's5_lambda_im_bwd': out['s5_lambda_im_bwd'], 's5_log_step_bwd': out['s5_log_step_bwd'], 's5_b_re': out['s5_b_re'], 's5_b_im': out['s5_b_im'], 's5_c_re_fwd': out['s5_c_re_fwd'], 's5_c_im_fwd': out['s5_c_im_fwd'], 's5_c_re_bwd': out['s5_c_re_bwd'], 's5_c_im_bwd': out['s5_c_im_bwd'], 's5_d': out['s5_d'], 's5_glu_w': out['s5_glu_w'], 's5_glu_b': out['s5_glu_b'], 's5_norm_w': out['s5_norm_w'], 'w_out': out['w_out'], 'norm_ffn_w': out['norm_ffn_w'], 'ffn_w_up': out['ffn_w_up'], 'ffn_conv_w': out['ffn_conv_w'], 'ffn_conv_b': out['ffn_conv_b'], 'ffn_w_down': out['ffn_w_down'], 'norm_final_w': out['norm_final_w'], 'loss_target': out['loss_target'], 'm_norm_mix_w': out['m_norm_mix_w'], 'm_w_in': out['m_w_in'], 'm_ssd_conv_w': out['m_ssd_conv_w'], 'm_ssd_conv_b': out['m_ssd_conv_b'], 'm_ssd_dt_bias_fwd': out['m_ssd_dt_bias_fwd'], 'm_ssd_dt_bias_bwd': out['m_ssd_dt_bias_bwd'], 'm_ssd_a_log_fwd': out['m_ssd_a_log_fwd'], 'm_ssd_a_log_bwd': out['m_ssd_a_log_bwd'], 'm_ssd_d': out['m_ssd_d'], 'm_ssd_norm_w': out['m_ssd_norm_w'], 'm_s5_lambda_re_fwd': out['m_s5_lambda_re_fwd'], 'm_s5_lambda_im_fwd': out['m_s5_lambda_im_fwd'], 'm_s5_log_step_fwd': out['m_s5_log_step_fwd'], 'm_s5_lambda_re_bwd': out['m_s5_lambda_re_bwd'], 'm_s5_lambda_im_bwd': out['m_s5_lambda_im_bwd'], 'm_s5_log_step_bwd': out['m_s5_log_step_bwd'], 'm_s5_b_re': out['m_s5_b_re'], 'm_s5_b_im': out['m_s5_b_im'], 'm_s5_c_re_fwd': out['m_s5_c_re_fwd'], 'm_s5_c_im_fwd': out['m_s5_c_im_fwd'], 'm_s5_c_re_bwd': out['m_s5_c_re_bwd'], 'm_s5_c_im_bwd': out['m_s5_c_im_bwd'], 'm_s5_d': out['m_s5_d'], 'm_s5_glu_w': out['m_s5_glu_w'], 'm_s5_glu_b': out['m_s5_glu_b'], 'm_s5_norm_w': out['m_s5_norm_w'], 'm_w_out': out['m_w_out'], 'm_norm_ffn_w': out['m_norm_ffn_w'], 'm_ffn_w_up': out['m_ffn_w_up'], 'm_ffn_conv_w': out['m_ffn_conv_w'], 'm_ffn_conv_b': out['m_ffn_conv_b'], 'm_ffn_w_down': out['m_ffn_w_down'], 'm_norm_final_w': out['m_norm_final_w'], 'v_norm_mix_w': out['v_norm_mix_w'], 'v_w_in': out['v_w_in'], 'v_ssd_conv_w': out['v_ssd_conv_w'], 'v_ssd_conv_b': out['v_ssd_conv_b'], 'v_ssd_dt_bias_fwd': out['v_ssd_dt_bias_fwd'], 'v_ssd_dt_bias_bwd': out['v_ssd_dt_bias_bwd'], 'v_ssd_a_log_fwd': out['v_ssd_a_log_fwd'], 'v_ssd_a_log_bwd': out['v_ssd_a_log_bwd'], 'v_ssd_d': out['v_ssd_d'], 'v_ssd_norm_w': out['v_ssd_norm_w'], 'v_s5_lambda_re_fwd': out['v_s5_lambda_re_fwd'], 'v_s5_lambda_im_fwd': out['v_s5_lambda_im_fwd'], 'v_s5_log_step_fwd': out['v_s5_log_step_fwd'], 'v_s5_lambda_re_bwd': out['v_s5_lambda_re_bwd'], 'v_s5_lambda_im_bwd': out['v_s5_lambda_im_bwd'], 'v_s5_log_step_bwd': out['v_s5_log_step_bwd'], 'v_s5_b_re': out['v_s5_b_re'], 'v_s5_b_im': out['v_s5_b_im'], 'v_s5_c_re_fwd': out['v_s5_c_re_fwd'], 'v_s5_c_im_fwd': out['v_s5_c_im_fwd'], 'v_s5_c_re_bwd': out['v_s5_c_re_bwd'], 'v_s5_c_im_bwd': out['v_s5_c_im_bwd'], 'v_s5_d': out['v_s5_d'], 'v_s5_glu_w': out['v_s5_glu_w'], 'v_s5_glu_b': out['v_s5_glu_b'], 'v_s5_norm_w': out['v_s5_norm_w'], 'v_w_out': out['v_w_out'], 'v_norm_ffn_w': out['v_norm_ffn_w'], 'v_ffn_w_up': out['v_ffn_w_up'], 'v_ffn_conv_w': out['v_ffn_conv_w'], 'v_ffn_conv_b': out['v_ffn_conv_b'], 'v_ffn_w_down': out['v_ffn_w_down'], 'v_norm_final_w': out['v_norm_final_w']}


def _loss(weights, diff, rest, loss_target):
    with _jax.named_scope("forward"):
        args = {**rest, TWIN_DIFF_INPUT: diff, **{k: w.astype(_WEIGHT_DTYPES[k]) for k, w in weights.items()}}
        y = _forward(args)
    with _jax.named_scope("loss_head"):
        err = _jnp.square(y.astype(_jnp.float32) - loss_target)
        return 0.5 * _jnp.sum(_jnp.mean(err, axis=-1)) if err.ndim else 0.5 * err


def _adamw(w, g, m, v):
    m = ADAM_B1 * m + (1.0 - ADAM_B1) * g
    v = ADAM_B2 * v + (1.0 - ADAM_B2) * _jnp.square(g)
    m_hat = m / (1.0 - ADAM_B1 ** ADAM_STEP)
    v_hat = v / (1.0 - ADAM_B2 ** ADAM_STEP)
    delta = -ADAM_LR * (m_hat / (_jnp.sqrt(v_hat) + ADAM_EPS) + ADAM_WD * w)
    return delta, m, v


def reference(x, norm_mix_w, w_in, ssd_conv_w, ssd_conv_b, ssd_dt_bias_fwd, ssd_dt_bias_bwd, ssd_a_log_fwd, ssd_a_log_bwd, ssd_d, ssd_norm_w, s5_lambda_re_fwd, s5_lambda_im_fwd, s5_log_step_fwd, s5_lambda_re_bwd, s5_lambda_im_bwd, s5_log_step_bwd, s5_b_re, s5_b_im, s5_c_re_fwd, s5_c_im_fwd, s5_c_re_bwd, s5_c_im_bwd, s5_d, s5_glu_w, s5_glu_b, s5_norm_w, w_out, norm_ffn_w, ffn_w_up, ffn_conv_w, ffn_conv_b, ffn_w_down, norm_final_w, loss_target, m_norm_mix_w, m_w_in, m_ssd_conv_w, m_ssd_conv_b, m_ssd_dt_bias_fwd, m_ssd_dt_bias_bwd, m_ssd_a_log_fwd, m_ssd_a_log_bwd, m_ssd_d, m_ssd_norm_w, m_s5_lambda_re_fwd, m_s5_lambda_im_fwd, m_s5_log_step_fwd, m_s5_lambda_re_bwd, m_s5_lambda_im_bwd, m_s5_log_step_bwd, m_s5_b_re, m_s5_b_im, m_s5_c_re_fwd, m_s5_c_im_fwd, m_s5_c_re_bwd, m_s5_c_im_bwd, m_s5_d, m_s5_glu_w, m_s5_glu_b, m_s5_norm_w, m_w_out, m_norm_ffn_w, m_ffn_w_up, m_ffn_conv_w, m_ffn_conv_b, m_ffn_w_down, m_norm_final_w, v_norm_mix_w, v_w_in, v_ssd_conv_w, v_ssd_conv_b, v_ssd_dt_bias_fwd, v_ssd_dt_bias_bwd, v_ssd_a_log_fwd, v_ssd_a_log_bwd, v_ssd_d, v_ssd_norm_w, v_s5_lambda_re_fwd, v_s5_lambda_im_fwd, v_s5_log_step_fwd, v_s5_lambda_re_bwd, v_s5_lambda_im_bwd, v_s5_log_step_bwd, v_s5_b_re, v_s5_b_im, v_s5_c_re_fwd, v_s5_c_im_fwd, v_s5_c_re_bwd, v_s5_c_im_bwd, v_s5_d, v_s5_glu_w, v_s5_glu_b, v_s5_norm_w, v_w_out, v_norm_ffn_w, v_ffn_w_up, v_ffn_conv_w, v_ffn_conv_b, v_ffn_w_down, v_norm_final_w):
    given = dict(x=x, norm_mix_w=norm_mix_w, w_in=w_in, ssd_conv_w=ssd_conv_w, ssd_conv_b=ssd_conv_b, ssd_dt_bias_fwd=ssd_dt_bias_fwd, ssd_dt_bias_bwd=ssd_dt_bias_bwd, ssd_a_log_fwd=ssd_a_log_fwd, ssd_a_log_bwd=ssd_a_log_bwd, ssd_d=ssd_d, ssd_norm_w=ssd_norm_w, s5_lambda_re_fwd=s5_lambda_re_fwd, s5_lambda_im_fwd=s5_lambda_im_fwd, s5_log_step_fwd=s5_log_step_fwd, s5_lambda_re_bwd=s5_lambda_re_bwd, s5_lambda_im_bwd=s5_lambda_im_bwd, s5_log_step_bwd=s5_log_step_bwd, s5_b_re=s5_b_re, s5_b_im=s5_b_im, s5_c_re_fwd=s5_c_re_fwd, s5_c_im_fwd=s5_c_im_fwd, s5_c_re_bwd=s5_c_re_bwd, s5_c_im_bwd=s5_c_im_bwd, s5_d=s5_d, s5_glu_w=s5_glu_w, s5_glu_b=s5_glu_b, s5_norm_w=s5_norm_w, w_out=w_out, norm_ffn_w=norm_ffn_w, ffn_w_up=ffn_w_up, ffn_conv_w=ffn_conv_w, ffn_conv_b=ffn_conv_b, ffn_w_down=ffn_w_down, norm_final_w=norm_final_w, loss_target=loss_target, m_norm_mix_w=m_norm_mix_w, m_w_in=m_w_in, m_ssd_conv_w=m_ssd_conv_w, m_ssd_conv_b=m_ssd_conv_b, m_ssd_dt_bias_fwd=m_ssd_dt_bias_fwd, m_ssd_dt_bias_bwd=m_ssd_dt_bias_bwd, m_ssd_a_log_fwd=m_ssd_a_log_fwd, m_ssd_a_log_bwd=m_ssd_a_log_bwd, m_ssd_d=m_ssd_d, m_ssd_norm_w=m_ssd_norm_w, m_s5_lambda_re_fwd=m_s5_lambda_re_fwd, m_s5_lambda_im_fwd=m_s5_lambda_im_fwd, m_s5_log_step_fwd=m_s5_log_step_fwd, m_s5_lambda_re_bwd=m_s5_lambda_re_bwd, m_s5_lambda_im_bwd=m_s5_lambda_im_bwd, m_s5_log_step_bwd=m_s5_log_step_bwd, m_s5_b_re=m_s5_b_re, m_s5_b_im=m_s5_b_im, m_s5_c_re_fwd=m_s5_c_re_fwd, m_s5_c_im_fwd=m_s5_c_im_fwd, m_s5_c_re_bwd=m_s5_c_re_bwd, m_s5_c_im_bwd=m_s5_c_im_bwd, m_s5_d=m_s5_d, m_s5_glu_w=m_s5_glu_w, m_s5_glu_b=m_s5_glu_b, m_s5_norm_w=m_s5_norm_w, m_w_out=m_w_out, m_norm_ffn_w=m_norm_ffn_w, m_ffn_w_up=m_ffn_w_up, m_ffn_conv_w=m_ffn_conv_w, m_ffn_conv_b=m_ffn_conv_b, m_ffn_w_down=m_ffn_w_down, m_norm_final_w=m_norm_final_w, v_norm_mix_w=v_norm_mix_w, v_w_in=v_w_in, v_ssd_conv_w=v_ssd_conv_w, v_ssd_conv_b=v_ssd_conv_b, v_ssd_dt_bias_fwd=v_ssd_dt_bias_fwd, v_ssd_dt_bias_bwd=v_ssd_dt_bias_bwd, v_ssd_a_log_fwd=v_ssd_a_log_fwd, v_ssd_a_log_bwd=v_ssd_a_log_bwd, v_ssd_d=v_ssd_d, v_ssd_norm_w=v_ssd_norm_w, v_s5_lambda_re_fwd=v_s5_lambda_re_fwd, v_s5_lambda_im_fwd=v_s5_lambda_im_fwd, v_s5_log_step_fwd=v_s5_log_step_fwd, v_s5_lambda_re_bwd=v_s5_lambda_re_bwd, v_s5_lambda_im_bwd=v_s5_lambda_im_bwd, v_s5_log_step_bwd=v_s5_log_step_bwd, v_s5_b_re=v_s5_b_re, v_s5_b_im=v_s5_b_im, v_s5_c_re_fwd=v_s5_c_re_fwd, v_s5_c_im_fwd=v_s5_c_im_fwd, v_s5_c_re_bwd=v_s5_c_re_bwd, v_s5_c_im_bwd=v_s5_c_im_bwd, v_s5_d=v_s5_d, v_s5_glu_w=v_s5_glu_w, v_s5_glu_b=v_s5_glu_b, v_s5_norm_w=v_s5_norm_w, v_w_out=v_w_out, v_norm_ffn_w=v_norm_ffn_w, v_ffn_w_up=v_ffn_w_up, v_ffn_conv_w=v_ffn_conv_w, v_ffn_conv_b=v_ffn_conv_b, v_ffn_w_down=v_ffn_w_down, v_norm_final_w=v_norm_final_w)
    weights = {n: given[n] for n in TWIN_WEIGHTS}
    shared = {n: given[n] for n in SHARED_INPUTS}
    per_example = {n: given[n] for n in ['x']}
    grad_fn = _jax.value_and_grad(_loss, argnums=(0, 1))

    def one_microbatch(ex, loss_target):
        ex = dict(ex)
        diff = ex.pop(TWIN_DIFF_INPUT)
        return grad_fn(weights, diff, {**shared, **ex}, loss_target)

    if N_MICROBATCH == 1:
        loss, (grad_w, grad_x) = one_microbatch(per_example, given["loss_target"])
    else:
        def body(carry, xs):
            loss_sum, grad_sum = carry
            l_k, (gw_k, gx_k) = one_microbatch(xs[0], xs[1])
            with _jax.named_scope("update"):
                return (loss_sum + l_k, _jax.tree.map(_jnp.add, grad_sum, gw_k)), gx_k

        init = (_jnp.zeros((), _jnp.float32), _jax.tree.map(_jnp.zeros_like, weights))
        (loss, grad_w), grad_x = _jax.lax.scan(body, init, (per_example, given["loss_target"]))
    with _jax.named_scope("update"):
        delta_w, new_m, new_v = {}, {}, {}
        for n in TWIN_WEIGHTS:
            delta_w[n], new_m[n], new_v[n] = _adamw(weights[n], grad_w[n], given["m_" + n], given["v_" + n])
    return (loss, grad_x, *[grad_w[n] for n in TWIN_WEIGHTS], *[delta_w[n] for n in TWIN_WEIGHTS],
            *[new_m[n] for n in TWIN_WEIGHTS], *[new_v[n] for n in TWIN_WEIGHTS])
```

```python
import functools
import math

import jax
import jax.numpy as jnp
from jax import lax
from jax.experimental import pallas as pl
from jax.experimental.pallas import tpu as pltpu

F32 = jnp.float32
BF16 = jnp.bfloat16
HI = lax.Precision.HIGHEST
SDS = jax.ShapeDtypeStruct
MESH = pl.DeviceIdType.MESH

NN = (((1,), (0,)), ((), ()))
NT = (((1,), (1,)), ((), ()))
TN = (((0,), (0,)), ((), ()))

EPS = 1e-6
SSD_HEADS = 16
SSD_HEAD_DIM = 64
SSD_GROUPS = 4
SSD_STATE = 128
SSD_CHUNK = 128
SSD_CONV = 5
S5_GROUPS = 32
S5_CH = 16
S5_STATE = 64
S5_T = 16
LANES = 128
ADAM_LR, ADAM_B1, ADAM_B2, ADAM_EPS, ADAM_WD, ADAM_STEP = 0.001, 0.9, 0.999, 1e-08, 0.01, 10
V7X_VMEM_BYTES = 64 * 1024 * 1024
VMEM_LIMIT = V7X_VMEM_BYTES - 8 * 1024 * 1024


def _cp(sem, vmem=None):
    return pltpu.CompilerParams(dimension_semantics=sem, vmem_limit_bytes=vmem)


def _dot(a, b, dims=NN, precision=None):
    return lax.dot_general(a, b, dims, precision=precision, preferred_element_type=F32)


def _rms(x, w):
    return x * lax.rsqrt(jnp.mean(x * x, axis=-1, keepdims=True) + EPS) * w


def _sigmoid(x):
    return 1.0 / (1.0 + jnp.exp(-x))


def _softplus(x):
    return jnp.maximum(x, 0.0) + jnp.log1p(jnp.exp(-jnp.abs(x)))


@functools.partial(jax.custom_vjp, nondiff_argnums=(1, 2))
def _shift(x, k, seg):
    n = x.shape[0]
    r = lax.broadcasted_iota(jnp.int32, x.shape, 0) % seg
    y = pltpu.roll(x, k % n, 0)
    ok = (r >= k) if k > 0 else (r < seg + k)
    return jnp.where(ok, y, 0.0)


def _shift_fwd(x, k, seg):
    return _shift(x, k, seg), None


def _shift_bwd(k, seg, _, g):
    return (_shift(g, -k, seg),)


_shift.defvjp(_shift_fwd, _shift_bwd)


@jax.custom_vjp
def _swap(z):
    return pltpu.roll(z, LANES // 2, 1)


_swap.defvjp(lambda z: (_swap(z), None), lambda _, g: (_swap(g),))


def _norm_matmul(x, nw, ws, tm, name):
    n, d = x.shape
    k = len(ws)

    def body(x_ref, nw_ref, *refs):
        hn = _rms(x_ref[...], nw_ref[...]).astype(BF16)
        refs[k][...] = hn
        for w_ref, o_ref in zip(refs[:k], refs[k + 1:]):
            o_ref[...] = _dot(hn, w_ref[...])

    row = lambda i: (i, 0)
    fix = lambda i: (0, 0)
    return pl.pallas_call(
        body, name=name, grid=(n // tm,),
        in_specs=[pl.BlockSpec((tm, d), row), pl.BlockSpec((1, d), fix)] + [pl.BlockSpec(w.shape, fix) for w in ws],
        out_specs=[pl.BlockSpec((tm, d), row)] + [pl.BlockSpec((tm, w.shape[1]), row) for w in ws],
        out_shape=[SDS((n, d), BF16)] + [SDS((n, w.shape[1]), F32) for w in ws],
        compiler_params=_cp(("arbitrary",), VMEM_LIMIT),
    )(x, nw, *ws)


def _matmul_res(a, w, res, tm, name):
    n, kd = a.shape
    m = w.shape[1]

    def body(a_ref, w_ref, r_ref, o_ref):
        o_ref[...] = r_ref[...] + _dot(a_ref[...], w_ref[...])

    return pl.pallas_call(
        body, name=name, grid=(n // tm,),
        in_specs=[pl.BlockSpec((tm, kd), lambda i: (i, 0)), pl.BlockSpec((kd, m), lambda i: (0, 0)),
                  pl.BlockSpec((tm, m), lambda i: (i, 0))],
        out_specs=pl.BlockSpec((tm, m), lambda i: (i, 0)),
        out_shape=SDS((n, m), F32),
        compiler_params=_cp(("arbitrary",), VMEM_LIMIT),
    )(a, w, res)


def _matmul_nt(gs, ws, tm, name):
    n = gs[0].shape[0]
    kd = ws[0].shape[0]
    cnt = len(gs)

    def body(*refs):
        acc = None
        for g_ref, w_ref in zip(refs[:cnt], refs[cnt:2 * cnt]):
            t = _dot(g_ref[...].astype(BF16), w_ref[...], NT)
            acc = t if acc is None else acc + t
        refs[2 * cnt][...] = acc

    return pl.pallas_call(
        body, name=name, grid=(n // tm,),
        in_specs=[pl.BlockSpec((tm, g.shape[1]), lambda i: (i, 0)) for g in gs]
        + [pl.BlockSpec(w.shape, lambda i: (0, 0)) for w in ws],
        out_specs=pl.BlockSpec((tm, kd), lambda i: (i, 0)),
        out_shape=SDS((n, kd), F32),
        compiler_params=_cp(("arbitrary",), VMEM_LIMIT),
    )(*gs, *ws)


def _matmul_tn(a, g, tm, cb, name):
    n, kd = a.shape
    m = g.shape[1]

    def body(a_ref, g_ref, o_ref):
        t = _dot(a_ref[...], g_ref[...].astype(BF16), TN)

        @pl.when(pl.program_id(1) == 0)
        def _():
            o_ref[...] = t

        @pl.when(pl.program_id(1) != 0)
        def _():
            o_ref[...] += t

    return pl.pallas_call(
        body, name=name, grid=(m // cb, n // tm),
        in_specs=[pl.BlockSpec((tm, kd), lambda j, i: (i, 0)), pl.BlockSpec((tm, cb), lambda j, i: (i, j))],
        out_specs=pl.BlockSpec((kd, cb), lambda j, i: (0, j)),
        out_shape=SDS((kd, m), F32),
        compiler_params=_cp(("arbitrary", "arbitrary"), VMEM_LIMIT),
    )(a, g)


def _dwconv(x, w, b):
    kw = w.shape[0]
    acc = b
    for k in range(kw):
        acc = acc + w[k:k + 1, :] * _shift(x, kw // 2 - k, x.shape[0])
    return acc


def _conv_silu_fn(x, w, b):
    y = _dwconv(x, w, b)
    return y * _sigmoid(y)


def _conv_glu_fn(v, g, wv, wg, bv, bg):
    cv = _dwconv(v, wv, bv)
    cg = _dwconv(g, wg, bg)
    return cg * _sigmoid(cg) * cv


def _conv_silu(x, w, b, bl, cb, name):
    n, c = x.shape
    sl = n // bl
    kw = w.shape[0]

    def body(x_ref, w_ref, b_ref, o_ref):
        o_ref[...] = _conv_silu_fn(x_ref[...], w_ref[...], b_ref[...])

    return pl.pallas_call(
        body, name=name, grid=(bl, c // cb),
        in_specs=[pl.BlockSpec((sl, cb), lambda s, j: (s, j)), pl.BlockSpec((kw, cb), lambda s, j: (0, j)),
                  pl.BlockSpec((1, cb), lambda s, j: (0, j))],
        out_specs=pl.BlockSpec((sl, cb), lambda s, j: (s, j)),
        out_shape=SDS((n, c), F32),
        compiler_params=_cp(("arbitrary", "arbitrary"), VMEM_LIMIT),
    )(x, w, b)


def _conv_silu_bwd(x, w, b, dys, bl, cb, name):
    n, c = x.shape
    sl = n // bl
    kw = w.shape[0]
    cnt = len(dys)

    def body(x_ref, w_ref, b_ref, *refs):
        dy = refs[0][...]
        for r in refs[1:cnt]:
            dy = dy + r[...]
        dx_ref, dw_ref, db_ref = refs[cnt:]
        _, vjp = jax.vjp(_conv_silu_fn, x_ref[...], w_ref[...], b_ref[...])
        dx, dw, db = vjp(dy)
        dx_ref[...] = dx

        @pl.when(pl.program_id(1) == 0)
        def _():
            dw_ref[...] = dw
            db_ref[...] = db

        @pl.when(pl.program_id(1) != 0)
        def _():
            dw_ref[...] += dw
            db_ref[...] += db

    dy_specs = []
    for arr, lead in dys:
        if lead is None:
            dy_specs.append(pl.BlockSpec((sl, cb), lambda j, s: (s, j)))
        else:
            dy_specs.append(pl.BlockSpec((None, sl, cb), functools.partial(lambda j, s, lead: (lead, s, j), lead=lead)))
    return pl.pallas_call(
        body, name=name, grid=(c // cb, bl),
        in_specs=[pl.BlockSpec((sl, cb), lambda j, s: (s, j)), pl.BlockSpec((kw, cb), lambda j, s: (0, j)),
                  pl.BlockSpec((1, cb), lambda j, s: (0, j))] + dy_specs,
        out_specs=[pl.BlockSpec((sl, cb), lambda j, s: (s, j)), pl.BlockSpec((kw, cb), lambda j, s: (0, j)),
                   pl.BlockSpec((1, cb), lambda j, s: (0, j))],
        out_shape=[SDS((n, c), F32), SDS((kw, c), F32), SDS((1, c), F32)],
        compiler_params=_cp(("arbitrary", "arbitrary"), VMEM_LIMIT),
    )(x, w, b, *[a for a, _ in dys])


def _conv_glu(v, g, wv, wg, bv, bg, bl, cb, name):
    n, c = v.shape
    sl = n // bl
    kw = wv.shape[0]

    def body(v_ref, g_ref, wv_ref, wg_ref, bv_ref, bg_ref, o_ref):
        o_ref[...] = _conv_glu_fn(v_ref[...], g_ref[...], wv_ref[...], wg_ref[...], bv_ref[...], bg_ref[...]).astype(BF16)

    big = pl.BlockSpec((sl, cb), lambda s, j: (s, j))
    wsp = pl.BlockSpec((kw, cb), lambda s, j: (0, j))
    bsp = pl.BlockSpec((1, cb), lambda s, j: (0, j))
    return pl.pallas_call(
        body, name=name, grid=(bl, c // cb),
        in_specs=[big, big, wsp, wsp, bsp, bsp], out_specs=big, out_shape=SDS((n, c), BF16),
        compiler_params=_cp(("arbitrary", "arbitrary"), VMEM_LIMIT),
    )(v, g, wv, wg, bv, bg)


def _conv_glu_bwd(v, g, wv, wg, bv, bg, dact, bl, cb, name):
    n, c = v.shape
    sl = n // bl
    kw = wv.shape[0]

    def body(v_ref, g_ref, wv_ref, wg_ref, bv_ref, bg_ref, da_ref, dv_ref, dg_ref, dwv_ref, dwg_ref, dbv_ref, dbg_ref):
        _, vjp = jax.vjp(_conv_glu_fn, v_ref[...], g_ref[...], wv_ref[...], wg_ref[...], bv_ref[...], bg_ref[...])
        dv, dg, dwv, dwg, dbv, dbg = vjp(da_ref[...])
        dv_ref[...] = dv
        dg_ref[...] = dg

        @pl.when(pl.program_id(1) == 0)
        def _():
            dwv_ref[...] = dwv
            dwg_ref[...] = dwg
            dbv_ref[...] = dbv
            dbg_ref[...] = dbg

        @pl.when(pl.program_id(1) != 0)
        def _():
            dwv_ref[...] += dwv
            dwg_ref[...] += dwg
            dbv_ref[...] += dbv
            dbg_ref[...] += dbg

    big = pl.BlockSpec((sl, cb), lambda j, s: (s, j))
    wsp = pl.BlockSpec((kw, cb), lambda j, s: (0, j))
    bsp = pl.BlockSpec((1, cb), lambda j, s: (0, j))
    return pl.pallas_call(
        body, name=name, grid=(c // cb, bl),
        in_specs=[big, big, wsp, wsp, bsp, bsp, big],
        out_specs=[big, big, wsp, wsp, bsp, bsp],
        out_shape=[SDS((n, c), F32), SDS((n, c), F32), SDS((kw, c), F32), SDS((kw, c), F32), SDS((1, c), F32), SDS((1, c), F32)],
        compiler_params=_cp(("arbitrary", "arbitrary"), VMEM_LIMIT),
    )(v, g, wv, wg, bv, bg, dact)


def _ssd_group_fn(rev, g, xs, dtr, bg, cg, st, alog, dtb):
    q = bg.shape[0]
    lane = lax.broadcasted_iota(jnp.int32, (1, LANES), 1)
    sub = lax.broadcasted_iota(jnp.int32, (LANES, 1), 0)
    r = lax.broadcasted_iota(jnp.int32, (q, q), 0)
    c = lax.broadcasted_iota(jnp.int32, (q, q), 1)
    sgn = 1 - 2 * rev
    mask = (r - c) * sgn >= 0
    tri = mask.astype(F32)
    tri_t = ((c - r) * sgn >= 0).astype(F32)
    dt = _softplus(dtr + dtb)
    dta = dt * (-jnp.exp(alog))
    cs = _dot(tri, dta, NN, HI)
    cs_t = _dot(dta, tri_t, TN, HI)
    tot = jnp.sum(dta, axis=0, keepdims=True)
    scores = _dot(cg, bg, NT)
    ys, outs = [], []
    for j in range(4):
        hl = 16 * rev + 4 * g + j
        ohl = (lane == hl).astype(F32)
        ohs = (sub == hl).astype(F32)
        col = jnp.sum(cs * ohl, axis=1, keepdims=True)
        row = jnp.sum(cs_t * ohs, axis=0, keepdims=True)
        dth = jnp.sum(dt * ohl, axis=1, keepdims=True)
        toth = jnp.sum(tot * ohl, axis=1, keepdims=True)
        seg = jnp.where(mask, jnp.exp(jnp.where(mask, col - row, 0.0)), 0.0)
        xdt = xs[j] * dth
        y = _dot(scores * seg, xdt) + jnp.exp(col) * _dot(cg, st[j], NT)
        new = _dot(xdt * jnp.exp(toth - col), bg, TN)
        ys.append(y)
        outs.append(jnp.exp(toth) * st[j] + new)
    return ys, outs


def _ssd_scan(xs, bm, cm, dtr, alog2, dtb2, bl, name):
    n = xs.shape[0]
    q = SSD_CHUNK
    nc = n // bl // q
    hd, ns = SSD_HEAD_DIM, SSD_STATE

    def body(xs_ref, b_ref, c_ref, dt_ref, al_ref, db_ref, y_ref, sv_ref, st_ref):
        d, i, g = pl.program_id(0), pl.program_id(2), pl.program_id(3)

        @pl.when(i == 0)
        def _():
            for j in range(4):
                st_ref[4 * g + j] = jnp.zeros((hd, ns), F32)

        st = [st_ref[4 * g + j] for j in range(4)]
        for j in range(4):
            sv_ref[j] = st[j]
        xl = [xs_ref[:, hd * j:hd * (j + 1)] for j in range(4)]
        ys, outs = _ssd_group_fn(d, g, xl, dt_ref[...], b_ref[...], c_ref[...], st, al_ref[...], db_ref[...])
        for j in range(4):
            st_ref[4 * g + j] = outs[j]
            y_ref[:, hd * j:hd * (j + 1)] = ys[j]

    def rowblk(d, s, i):
        return s * nc + i + d * (nc - 1 - 2 * i)

    return pl.pallas_call(
        body, name=name, grid=(2, bl, nc, SSD_GROUPS),
        in_specs=[pl.BlockSpec((q, 4 * hd), lambda d, s, i, g: (rowblk(d, s, i), g)),
                  pl.BlockSpec((q, ns), lambda d, s, i, g: (rowblk(d, s, i), g)),
                  pl.BlockSpec((q, ns), lambda d, s, i, g: (rowblk(d, s, i), g)),
                  pl.BlockSpec((q, LANES), lambda d, s, i, g: (rowblk(d, s, i), 0)),
                  pl.BlockSpec((None, 1, LANES), lambda d, s, i, g: (d, 0, 0)),
                  pl.BlockSpec((None, 1, LANES), lambda d, s, i, g: (d, 0, 0))],
        out_specs=[pl.BlockSpec((None, q, 4 * hd), lambda d, s, i, g: (d, rowblk(d, s, i), g)),
                   pl.BlockSpec((None, None, 4, hd, ns), lambda d, s, i, g: (d, rowblk(d, s, i), g, 0, 0))],
        out_shape=[SDS((2, n, SSD_HEADS * hd), F32), SDS((2, n // q, SSD_HEADS, hd, ns), F32)],
        scratch_shapes=[pltpu.VMEM((SSD_HEADS, hd, ns), F32)],
        compiler_params=_cp(("arbitrary",) * 4, VMEM_LIMIT),
    )(xs, bm, cm, dtr, alog2, dtb2)


def _ssd_scan_bwd(xs, bm, cm, dtr, alog2, dtb2, saved, dy, bl, name):
    n = xs.shape[0]
    q = SSD_CHUNK
    nc = n // bl // q
    hd, ns = SSD_HEAD_DIM, SSD_STATE

    def body(xs_ref, b_ref, c_ref, dt_ref, al_ref, db_ref, sv_ref, dy_ref,
             dxs_ref, dbm_ref, dcm_ref, ddt_ref, dal_ref, ddb_ref, ds_ref):
        d, s, i, g = pl.program_id(0), pl.program_id(1), pl.program_id(2), pl.program_id(3)

        @pl.when(i == 0)
        def _():
            for j in range(4):
                ds_ref[4 * g + j] = jnp.zeros((hd, ns), F32)

        xl = [xs_ref[:, hd * j:hd * (j + 1)] for j in range(4)]
        st = [sv_ref[j] for j in range(4)]
        fn = functools.partial(_ssd_group_fn, d, g)
        _, vjp = jax.vjp(fn, xl, dt_ref[...], b_ref[...], c_ref[...], st, al_ref[...], db_ref[...])
        dys = [dy_ref[:, hd * j:hd * (j + 1)] for j in range(4)]
        dso = [ds_ref[4 * g + j] for j in range(4)]
        dxl, ddt, dbg, dcg, dst, dal, ddb = vjp((dys, dso))
        for j in range(4):
            ds_ref[4 * g + j] = dst[j]
            dxs_ref[:, hd * j:hd * (j + 1)] = dxl[j]
        dbm_ref[...] = dbg
        dcm_ref[...] = dcg

        @pl.when(g == 0)
        def _():
            ddt_ref[...] = ddt

        @pl.when(g != 0)
        def _():
            ddt_ref[...] += ddt

        first = jnp.logical_and(jnp.logical_and(s == 0, i == 0), g == 0)

        @pl.when(first)
        def _():
            dal_ref[...] = dal
            ddb_ref[...] = ddb

        @pl.when(jnp.logical_not(first))
        def _():
            dal_ref[...] += dal
            ddb_ref[...] += ddb

    def rowblk(d, s, i):
        return s * nc + (nc - 1 - i) + d * (2 * i - (nc - 1))

    return pl.pallas_call(
        body, name=name, grid=(2, bl, nc, SSD_GROUPS),
        in_specs=[pl.BlockSpec((q, 4 * hd), lambda d, s, i, g: (rowblk(d, s, i), g)),
                  pl.BlockSpec((q, ns), lambda d, s, i, g: (rowblk(d, s, i), g)),
                  pl.BlockSpec((q, ns), lambda d, s, i, g: (rowblk(d, s, i), g)),
                  pl.BlockSpec((q, LANES), lambda d, s, i, g: (rowblk(d, s, i), 0)),
                  pl.BlockSpec((None, 1, LANES), lambda d, s, i, g: (d, 0, 0)),
                  pl.BlockSpec((None, 1, LANES), lambda d, s, i, g: (d, 0, 0)),
                  pl.BlockSpec((None, None, 4, hd, ns), lambda d, s, i, g: (d, rowblk(d, s, i), g, 0, 0)),
                  pl.BlockSpec((q, 4 * hd), lambda d, s, i, g: (rowblk(d, s, i), g))],
        out_specs=[pl.BlockSpec((None, q, 4 * hd), lambda d, s, i, g: (d, rowblk(d, s, i), g)),
                   pl.BlockSpec((None, q, ns), lambda d, s, i, g: (d, rowblk(d, s, i), g)),
                   pl.BlockSpec((None, q, ns), lambda d, s, i, g: (d, rowblk(d, s, i), g)),
                   pl.BlockSpec((None, q, LANES), lambda d, s, i, g: (d, rowblk(d, s, i), 0)),
                   pl.BlockSpec((None, 1, LANES), lambda d, s, i, g: (d, 0, 0)),
                   pl.BlockSpec((None, 1, LANES), lambda d, s, i, g: (d, 0, 0))],
        out_shape=[SDS((2, n, SSD_HEADS * hd), F32), SDS((2, n, SSD_GROUPS * ns), F32), SDS((2, n, SSD_GROUPS * ns), F32),
                   SDS((2, n, LANES), F32), SDS((2, 1, LANES), F32), SDS((2, 1, LANES), F32)],
        scratch_shapes=[pltpu.VMEM((SSD_HEADS, hd, ns), F32)],
        compiler_params=_cp(("arbitrary",) * 4, VMEM_LIMIT),
    )(xs, bm, cm, dtr, alog2, dtb2, saved, dy)


def _s5_consts():
    t, ch, p = S5_T, S5_CH, S5_STATE
    lane = lax.broadcasted_iota(jnp.int32, (1, 2 * p), 1)
    pr = lax.broadcasted_iota(jnp.int32, (p, 2 * p), 0)
    pc = lax.broadcasted_iota(jnp.int32, (p, 2 * p), 1)
    cr = lax.broadcasted_iota(jnp.int32, (ch, t * ch), 0)
    cc = lax.broadcasted_iota(jnp.int32, (ch, t * ch), 1)
    rr = lax.broadcasted_iota(jnp.int32, (t * ch, t * ch), 0)
    rc = lax.broadcasted_iota(jnp.int32, (t * ch, t * ch), 1)
    return dict(
        left=lane < p,
        sg=jnp.where(lane < p, -1.0, 1.0).astype(F32),
        dup=(pc % p == pr).astype(F32),
        dup_l=(pc == pr).astype(F32),
        dup_r=(pc == pr + p).astype(F32),
        rep=(cc % ch == cr).astype(F32),
        dq=rc // ch - rr // ch,
    )


def _tile_rows(a, times):
    return jnp.concatenate([a] * times, axis=0)


def _s5_mats(k, rev, lr, li, ls, bre, bim, cre, cim):
    t = S5_T
    step = jnp.exp(ls)
    lr2 = jnp.sum(lr * k["dup"], axis=0, keepdims=True)
    li2 = jnp.sum(li * k["dup"], axis=0, keepdims=True)

    def erow(d):
        ang = (d * step) * li2
        return jnp.exp((d * step) * lr2) * jnp.where(k["left"], jnp.cos(ang), jnp.sin(ang))

    es = [erow(d) for d in range(t + 1)]
    mag = jnp.exp(step * lr)
    ar, ai = mag * jnp.cos(step * li), mag * jnp.sin(step * li)
    den = lr * lr + li * li
    zr = ((ar - 1.0) * lr + ai * li) / den
    zi = (ai * lr - (ar - 1.0) * li) / den
    bbr = zr * bre - zi * bim
    bbi = zr * bim + zi * bre
    bt1 = _dot(bbr, k["dup"], TN, HI)
    bt2 = _dot(bbi, k["dup"], TN, HI)
    bst = _dot(bbr, k["dup_l"], TN, HI) - _dot(bbi, k["dup_r"], TN, HI)
    c1 = _dot(cre, k["dup"], NN, HI)
    c2 = _dot(cim, k["dup"], NN, HI)
    sg = k["sg"]
    ce = [e * c1 + sg * _swap(e) * c2 for e in es]
    toep = None
    for d in range(t):
        kt = _dot(bst, ce[d], NT, HI)
        tile = _tile_rows(_dot(kt, k["rep"], NN, HI), t)
        term = jnp.where(k["dq"] == (-d if rev else d), tile, 0.0)
        toep = term if toep is None else toep + term
    w_out = jnp.concatenate([ce[(t - qq) if rev else (qq + 1)] * (-sg) for qq in range(t)], axis=0)
    w_st = jnp.concatenate(
        [(lambda e: e * bt1 + sg * _swap(e) * bt2)(es[s if rev else (t - 1 - s)]) for s in range(t)], axis=0)
    return toep, w_out, w_st, es[t]


def _cmul_row(k, e, z):
    es = _swap(e)
    return z * jnp.where(k["left"], e, es) + k["sg"] * _swap(z) * jnp.where(k["left"], es, e)


def _s5_dir(k, rev, nck, x, mats):
    toep, w_out, w_st, a_t = mats
    acc = _dot(x, w_st)
    e = a_t
    kk = 1
    sign = -1 if rev else 1
    while kk < nck:
        acc = acc + _cmul_row(k, e, _shift(acc, sign * kk, nck))
        e = _cmul_row(k, e, e)
        kk *= 2
    prev = _shift(acc, sign, nck)
    return _dot(x, toep) + _dot(prev, w_out, NT)


def _s5_group_fn(nck, x, pf, pb, bre, bim, dcol, wv, wg, bv, bg):
    k = _s5_consts()
    t = S5_T
    y = x * jnp.sum(dcol * k["rep"], axis=0, keepdims=True)
    for rev, (lr, li, ls, cre, cim) in ((False, pf), (True, pb)):
        y = y + _s5_dir(k, rev, nck, x, _s5_mats(k, rev, lr, li, ls, bre, bim, cre, cim))
    gy = jax.nn.gelu(y)
    same = k["dq"] == 0
    kv = jnp.where(same, _tile_rows(_dot(wv, k["rep"], NN, HI), t), 0.0)
    kg = jnp.where(same, _tile_rows(_dot(wg, k["rep"], NN, HI), t), 0.0)
    val = _dot(gy, kv) + jnp.sum(bv * k["rep"], axis=0, keepdims=True)
    gate = _dot(gy, kg) + jnp.sum(bg * k["rep"], axis=0, keepdims=True)
    return val * _sigmoid(gate)


def _s5_specs(r):
    p, ch = S5_STATE, S5_CH
    g3 = lambda i: (i, 0, 0)
    col = pl.BlockSpec((None, p, 1), g3)
    one = pl.BlockSpec((None, 1, 1), g3)
    cmat = pl.BlockSpec((None, ch, p), g3)
    bmat = pl.BlockSpec((None, p, ch), g3)
    ccol = pl.BlockSpec((None, ch, 1), g3)
    sq = pl.BlockSpec((None, ch, ch), g3)
    xs = pl.BlockSpec((None, r, S5_T * ch), g3)
    specs = [xs, col, col, one, cmat, cmat, col, col, one, cmat, cmat, bmat, bmat, ccol, sq, sq, ccol, ccol]
    return specs


def _s5_unpack(vals):
    x = vals[0]
    pf = tuple(vals[1:6])
    pb = tuple(vals[6:11])
    bre, bim, dcol, wv, wg, bv, bg = vals[11:18]
    return x, pf, pb, bre, bim, dcol, wv, wg, bv, bg


def _s5_fwd(args, nck, name):
    x = args[0]
    ng, r, w = x.shape

    def body(*refs):
        vals = [ref[...] for ref in refs[:18]]
        refs[18][...] = _s5_group_fn(nck, *_s5_unpack(vals))

    specs = _s5_specs(r)
    return pl.pallas_call(
        body, name=name, grid=(ng,), in_specs=specs, out_specs=specs[0], out_shape=SDS(x.shape, F32),
        compiler_params=_cp(("arbitrary",), VMEM_LIMIT),
    )(*args)


def _s5_bwd(args, dy, nck, name):
    x = args[0]
    ng, r, w = x.shape

    def body(*refs):
        vals = [ref[...] for ref in refs[:18]]
        _, vjp = jax.vjp(lambda *v: _s5_group_fn(nck, *_s5_unpack(v)), *vals)
        grads = vjp(refs[18][...])
        for o_ref, gval in zip(refs[19:], grads):
            o_ref[...] = gval

    specs = _s5_specs(r)
    return pl.pallas_call(
        body, name=name, grid=(ng,), in_specs=specs + [specs[0]], out_specs=specs,
        out_shape=[SDS(a.shape, F32) for a in args],
        compiler_params=_cp(("arbitrary",), VMEM_LIMIT),
    )(*args, dy)


def _mix_fn(yf, yb, xs, z, s5o, dvec, nw_ssd, nw_s5):
    hr = lax.broadcasted_iota(jnp.int32, (LANES, SSD_HEADS * SSD_HEAD_DIM), 0)
    hc = lax.broadcasted_iota(jnp.int32, (LANES, SSD_HEADS * SSD_HEAD_DIM), 1)
    expand = (hc // SSD_HEAD_DIM == hr).astype(F32)
    dch = jnp.sum(dvec * expand, axis=0, keepdims=True)
    y = (yf + yb + dch * xs) * (z * _sigmoid(z))
    return _rms(y, nw_ssd), _rms(s5o, nw_s5)


def _mix(y2, xs, z, s5o, dvec, nw_ssd, nw_s5, tm, name):
    n, c1 = xs.shape
    c2 = s5o.shape[1]

    def body(yf_ref, yb_ref, xs_ref, z_ref, s_ref, d_ref, n1_ref, n2_ref, o_ref):
        o1, o2 = _mix_fn(yf_ref[...], yb_ref[...], xs_ref[...], z_ref[...], s_ref[...], d_ref[...], n1_ref[...], n2_ref[...])
        o_ref[:, :c1] = o1.astype(BF16)
        o_ref[:, c1:] = o2.astype(BF16)

    row = lambda i: (i, 0)
    fix = lambda i: (0, 0)
    return pl.pallas_call(
        body, name=name, grid=(n // tm,),
        in_specs=[pl.BlockSpec((None, tm, c1), lambda i: (0, i, 0)), pl.BlockSpec((None, tm, c1), lambda i: (1, i, 0)),
                  pl.BlockSpec((tm, c1), row), pl.BlockSpec((tm, c1), row), pl.BlockSpec((tm, c2), row),
                  pl.BlockSpec((LANES, 1), fix), pl.BlockSpec((1, c1), fix), pl.BlockSpec((1, c2), fix)],
        out_specs=pl.BlockSpec((tm, c1 + c2), row), out_shape=SDS((n, c1 + c2), BF16),
        compiler_params=_cp(("arbitrary",), VMEM_LIMIT),
    )(y2, y2, xs, z, s5o, dvec, nw_ssd, nw_s5)


def _acc_rows(refs, vals, first):
    @pl.when(first)
    def _():
        for ref, v in zip(refs, vals):
            ref[...] = v

    @pl.when(jnp.logical_not(first))
    def _():
        for ref, v in zip(refs, vals):
            ref[...] += v


def _mix_bwd(y2, xs, z, s5o, dvec, nw_ssd, nw_s5, dmix, tm, name):
    n, c1 = xs.shape
    c2 = s5o.shape[1]

    def body(yf_ref, yb_ref, xs_ref, z_ref, s_ref, d_ref, n1_ref, n2_ref, dm_ref,
             dy_ref, dxs_ref, dz_ref, ds_ref, dd_ref, dn1_ref, dn2_ref):
        _, vjp = jax.vjp(_mix_fn, yf_ref[...], yb_ref[...], xs_ref[...], z_ref[...], s_ref[...], d_ref[...], n1_ref[...], n2_ref[...])
        dyf, _, dxs, dz, ds, dd, dn1, dn2 = vjp((dm_ref[:, :c1], dm_ref[:, c1:]))
        dy_ref[...] = dyf
        dxs_ref[...] = dxs
        dz_ref[...] = dz
        ds_ref[...] = ds
        _acc_rows((dd_ref, dn1_ref, dn2_ref), (dd, dn1, dn2), pl.program_id(0) == 0)

    row = lambda i: (i, 0)
    fix = lambda i: (0, 0)
    return pl.pallas_call(
        body, name=name, grid=(n // tm,),
        in_specs=[pl.BlockSpec((None, tm, c1), lambda i: (0, i, 0)), pl.BlockSpec((None, tm, c1), lambda i: (1, i, 0)),
                  pl.BlockSpec((tm, c1), row), pl.BlockSpec((tm, c1), row), pl.BlockSpec((tm, c2), row),
                  pl.BlockSpec((LANES, 1), fix), pl.BlockSpec((1, c1), fix), pl.BlockSpec((1, c2), fix),
                  pl.BlockSpec((tm, c1 + c2), row)],
        out_specs=[pl.BlockSpec((tm, c1), row), pl.BlockSpec((tm, c1), row), pl.BlockSpec((tm, c1), row), pl.BlockSpec((tm, c2), row),
                   pl.BlockSpec((LANES, 1), fix), pl.BlockSpec((1, c1), fix), pl.BlockSpec((1, c2), fix)],
        out_shape=[SDS((n, c1), F32), SDS((n, c1), F32), SDS((n, c1), F32), SDS((n, c2), F32),
                   SDS((LANES, 1), F32), SDS((1, c1), F32), SDS((1, c2), F32)],
        compiler_params=_cp(("arbitrary",), VMEM_LIMIT),
    )(y2, y2, xs, z, s5o, dvec, nw_ssd, nw_s5, dmix)


def _final_loss(h2, nw, tgt, tm, name):
    n, d = h2.shape

    def loss_fn(h, w, t):
        e = _rms(h, w) - t
        return (0.5 / d) * jnp.sum(e * e)

    def body(h_ref, w_ref, t_ref, l_ref, dh_ref, dw_ref):
        loss, (dh, dw) = jax.value_and_grad(loss_fn, argnums=(0, 1))(h_ref[...], w_ref[...], t_ref[...])
        dh_ref[...] = dh
        _acc_rows((l_ref, dw_ref), (jnp.full((1, LANES), loss, F32), dw), pl.program_id(0) == 0)

    row = lambda i: (i, 0)
    fix = lambda i: (0, 0)
    return pl.pallas_call(
        body, name=name, grid=(n // tm,),
        in_specs=[pl.BlockSpec((tm, d), row), pl.BlockSpec((1, d), fix), pl.BlockSpec((tm, d), row)],
        out_specs=[pl.BlockSpec((1, LANES), fix), pl.BlockSpec((tm, d), row), pl.BlockSpec((1, d), fix)],
        out_shape=[SDS((1, LANES), F32), SDS((n, d), F32), SDS((1, d), F32)],
        compiler_params=_cp(("arbitrary",), VMEM_LIMIT),
    )(h2, nw, tgt)


def _norm_bwd(x, nw, dhn, dres, tm, name):
    n, d = x.shape

    def body(x_ref, w_ref, g_ref, r_ref, dx_ref, dw_ref):
        _, vjp = jax.vjp(_rms, x_ref[...], w_ref[...])
        dx, dw = vjp(g_ref[...])
        dx_ref[...] = r_ref[...] + dx
        _acc_rows((dw_ref,), (dw,), pl.program_id(0) == 0)

    row = lambda i: (i, 0)
    fix = lambda i: (0, 0)
    return pl.pallas_call(
        body, name=name, grid=(n // tm,),
        in_specs=[pl.BlockSpec((tm, d), row), pl.BlockSpec((1, d), fix), pl.BlockSpec((tm, d), row), pl.BlockSpec((tm, d), row)],
        out_specs=[pl.BlockSpec((tm, d), row), pl.BlockSpec((1, d), fix)],
        out_shape=[SDS((n, d), F32), SDS((1, d), F32)],
        compiler_params=_cp(("arbitrary",), VMEM_LIMIT),
    )(x, nw, dhn, dres)


def _row_tile(n, cap=512):
    for t in range(min(cap, n) // 8 * 8, 7, -8):
        if n % t == 0:
            return t
    return n


def _sum_lead(a, name):
    kk, n, c = a.shape
    tm = _row_tile(n)

    def body(a_ref, o_ref):
        acc = a_ref[0]
        for i in range(1, kk):
            acc = acc + a_ref[i]
        o_ref[...] = acc

    return pl.pallas_call(
        body, name=name, grid=(n // tm,),
        in_specs=[pl.BlockSpec((kk, tm, c), lambda i: (0, i, 0))],
        out_specs=pl.BlockSpec((tm, c), lambda i: (i, 0)), out_shape=SDS((n, c), F32),
        compiler_params=_cp(("arbitrary",), VMEM_LIMIT),
    )(a)


def _adamw(w, g, m, v, name):
    n, c = w.shape
    tm = _row_tile(n)

    def body(w_ref, g_ref, m_ref, v_ref, d_ref, nm_ref, nv_ref):
        gv = g_ref[...]
        mn = ADAM_B1 * m_ref[...] + (1.0 - ADAM_B1) * gv
        vn = ADAM_B2 * v_ref[...] + (1.0 - ADAM_B2) * jnp.square(gv)
        m_hat = mn / (1.0 - ADAM_B1 ** ADAM_STEP)
        v_hat = vn / (1.0 - ADAM_B2 ** ADAM_STEP)
        d_ref[...] = -ADAM_LR * (m_hat / (jnp.sqrt(v_hat) + ADAM_EPS) + ADAM_WD * w_ref[...])
        nm_ref[...] = mn
        nv_ref[...] = vn

    spec = pl.BlockSpec((tm, c), lambda i: (i, 0))
    return pl.pallas_call(
        body, name=name, grid=(n // tm,), in_specs=[spec] * 4, out_specs=[spec] * 3,
        out_shape=[SDS((n, c), F32)] * 3, compiler_params=_cp(("arbitrary",), VMEM_LIMIT),
    )(w, g, m, v)


ANY = pl.BlockSpec(memory_space=pl.ANY)


def _me():
    return lax.axis_index("x"), lax.axis_index("y"), lax.axis_index("c")


def _gather_xy(shards, name):
    cnt = len(shards)

    def body(*refs):
        src, dst = refs[:cnt], refs[cnt:2 * cnt]
        send, recv, loc = refs[2 * cnt:]
        x, y, c = _me()
        mine = 2 * x + y
        chips = [(1 - x, y), (x, 1 - y), (1 - x, 1 - y)]
        copies = []
        for a in range(cnt):
            own = pltpu.make_async_copy(src[a], dst[a].at[mine], loc.at[a])
            own.start()
            copies.append(own)
        sends = []
        for a in range(cnt):
            for j, (px, py) in enumerate(chips):
                cp = pltpu.make_async_remote_copy(src[a], dst[a].at[mine], send.at[3 * a + j], recv.at[3 * a + j],
                                                  device_id=(px, py, c), device_id_type=MESH)
                cp.start()
                sends.append(cp)
        for a in range(cnt):
            for j, (px, py) in enumerate(chips):
                pltpu.make_async_remote_copy(src[a], dst[a].at[2 * px + py], send.at[3 * a + j], recv.at[3 * a + j],
                                             device_id=(px, py, c), device_id_type=MESH).wait_recv()
        for cp in sends:
            cp.wait_send()
        for cp in copies:
            cp.wait()

    return pl.pallas_call(
        body, name=name, in_specs=[ANY] * cnt, out_specs=[ANY] * cnt,
        out_shape=[SDS((4,) + s.shape, s.dtype) for s in shards],
        scratch_shapes=[pltpu.SemaphoreType.DMA((3 * cnt,)), pltpu.SemaphoreType.DMA((3 * cnt,)), pltpu.SemaphoreType.DMA((cnt,))],
    )(*shards)


def _swap_sibling(parts, name):
    cnt = len(parts)

    def body(*refs):
        src, dst = refs[:cnt], refs[cnt:2 * cnt]
        send, recv = refs[2 * cnt:]
        x, y, c = _me()
        cps = []
        for a in range(cnt):
            cp = pltpu.make_async_remote_copy(src[a].at[1 - c], dst[a], send.at[a], recv.at[a],
                                              device_id=(x, y, 1 - c), device_id_type=MESH)
            cp.start()
            cps.append(cp)
        for cp in cps:
            cp.wait()

    return pl.pallas_call(
        body, name=name, in_specs=[ANY] * cnt, out_specs=[ANY] * cnt,
        out_shape=[SDS(p.shape[1:], p.dtype) for p in parts],
        scratch_shapes=[pltpu.SemaphoreType.DMA((cnt,)), pltpu.SemaphoreType.DMA((cnt,))],
    )(*parts)


def _scatter_xy(parts, name):
    cnt = len(parts)

    def body(*refs):
        src, dst = refs[:cnt], refs[cnt:2 * cnt]
        send, recv, loc = refs[2 * cnt:]
        x, y, c = _me()
        mine = 2 * x + y
        chips = [(1 - x, y), (x, 1 - y), (1 - x, 1 - y)]
        local = []
        for a in range(cnt):
            cp = pltpu.make_async_copy(src[a].at[mine], dst[a].at[mine], loc.at[a])
            cp.start()
            local.append(cp)
        sends = []
        for a in range(cnt):
            for j, (px, py) in enumerate(chips):
                cp = pltpu.make_async_remote_copy(src[a].at[2 * px + py], dst[a].at[mine], send.at[3 * a + j], recv.at[3 * a + j],
                                                  device_id=(px, py, c), device_id_type=MESH)
                cp.start()
                sends.append(cp)
        for a in range(cnt):
            for j, (px, py) in enumerate(chips):
                pltpu.make_async_remote_copy(src[a].at[mine], dst[a].at[2 * px + py], send.at[3 * a + j], recv.at[3 * a + j],
                                             device_id=(px, py, c), device_id_type=MESH).wait_recv()
        for cp in sends:
            cp.wait_send()
        for cp in local:
            cp.wait()

    return pl.pallas_call(
        body, name=name, in_specs=[ANY] * cnt, out_specs=[ANY] * cnt,
        out_shape=[SDS(p.shape, p.dtype) for p in parts],
        scratch_shapes=[pltpu.SemaphoreType.DMA((3 * cnt,)), pltpu.SemaphoreType.DMA((3 * cnt,)), pltpu.SemaphoreType.DMA((cnt,))],
    )(*parts)


def _join_sibling(halves, name):
    cnt = len(halves)

    def body(*refs):
        src, dst = refs[:cnt], refs[cnt:2 * cnt]
        send, recv, loc = refs[2 * cnt:]
        x, y, c = _me()
        cps = []
        for a in range(cnt):
            own = pltpu.make_async_copy(src[a], dst[a].at[c], loc.at[a])
            own.start()
            cp = pltpu.make_async_remote_copy(src[a], dst[a].at[c], send.at[a], recv.at[a],
                                              device_id=(x, y, 1 - c), device_id_type=MESH)
            cp.start()
            cps.append((own, cp))
        for a in range(cnt):
            pltpu.make_async_remote_copy(src[a], dst[a].at[1 - c], send.at[a], recv.at[a],
                                         device_id=(x, y, 1 - c), device_id_type=MESH).wait_recv()
        for own, cp in cps:
            cp.wait_send()
            own.wait()

    return pl.pallas_call(
        body, name=name, in_specs=[ANY] * cnt, out_specs=[ANY] * cnt,
        out_shape=[SDS((2,) + h.shape, h.dtype) for h in halves],
        scratch_shapes=[pltpu.SemaphoreType.DMA((cnt,)), pltpu.SemaphoreType.DMA((cnt,)), pltpu.SemaphoreType.DMA((cnt,))],
    )(*halves)


def _bcast_all(buf, name):
    def body(src, dst, send, recv, loc):
        x, y, c = _me()
        mine = 4 * x + 2 * y + c
        own = pltpu.make_async_copy(src, dst.at[mine], loc)
        own.start()
        sends = []
        for k in range(1, 8):
            px, py, pc = x ^ (k >> 2), y ^ ((k >> 1) & 1), c ^ (k & 1)
            cp = pltpu.make_async_remote_copy(src, dst.at[mine], send.at[k - 1], recv.at[k - 1],
                                              device_id=(px, py, pc), device_id_type=MESH)
            cp.start()
            sends.append(cp)
        for k in range(1, 8):
            px, py, pc = x ^ (k >> 2), y ^ ((k >> 1) & 1), c ^ (k & 1)
            pltpu.make_async_remote_copy(src, dst.at[4 * px + 2 * py + pc], send.at[k - 1], recv.at[k - 1],
                                         device_id=(px, py, pc), device_id_type=MESH).wait_recv()
        for cp in sends:
            cp.wait_send()
        own.wait()

    return pl.pallas_call(
        body, name=name, in_specs=[ANY], out_specs=ANY, out_shape=SDS((8,) + buf.shape, buf.dtype),
        scratch_shapes=[pltpu.SemaphoreType.DMA((7,)), pltpu.SemaphoreType.DMA((7,)), pltpu.SemaphoreType.DMA(())],
    )(buf)


def _add2(a, b, name):
    shp = a.shape
    a2, b2 = a.reshape(-1, shp[-1]), b.reshape(-1, shp[-1])
    n, c = a2.shape
    tm = _row_tile(n, 256)

    def body(a_ref, b_ref, o_ref):
        o_ref[...] = a_ref[...] + b_ref[...]

    spec = pl.BlockSpec((tm, c), lambda i: (i, 0))
    return pl.pallas_call(body, name=name, grid=(n // tm,), in_specs=[spec, spec], out_specs=spec,
                          out_shape=SDS((n, c), F32), compiler_params=_cp(("arbitrary",), VMEM_LIMIT))(a2, b2).reshape(shp)


def _x_layout(u, bl):
    n, c = u.shape
    nck = n // bl // S5_T
    return u.reshape(bl, nck, S5_T, S5_GROUPS, S5_CH).transpose(3, 0, 1, 2, 4).reshape(S5_GROUPS, bl * nck, S5_T * S5_CH)


def _token_layout(xg, bl):
    ng, r, w = xg.shape
    nck = r // bl
    return xg.reshape(ng, bl, nck, S5_T, S5_CH).transpose(1, 2, 3, 0, 4).reshape(bl * nck * S5_T, ng * S5_CH)


def _pad_lanes(a, lanes=LANES):
    return jnp.pad(a, ((0, 0), (0, lanes - a.shape[1])))


def _local_step(x, tgt, p, bl):
    n, d = x.shape
    sw = SSD_HEADS * SSD_HEAD_DIM
    gn = SSD_GROUPS * SSD_STATE
    tm = min(n, 512)
    tm_ffn = min(n, 256)
    nck = n // bl // S5_T
    dff = p["w_down"].shape[0]
    s5w = S5_GROUPS * S5_CH

    w_in = p["w_in"]
    o1, o2, o3, o4 = sw, sw + sw, sw + sw + gn, sw + sw + 2 * gn
    w_z, w_xs, w_b, w_c = w_in[:, :o1], w_in[:, o1:o2], w_in[:, o2:o3], w_in[:, o3:o4]
    w_dt = _pad_lanes(w_in[:, o4:o4 + 2 * SSD_HEADS])
    w_u = w_in[:, o4 + 2 * SSD_HEADS:]
    in_ws = [w_z, w_xs, w_b, w_c, w_dt, w_u]
    cw, cb_ = p["ssd_conv_w"], p["ssd_conv_b"]
    conv_parts = [(cw[:, :sw], cb_[:, :sw]), (cw[:, sw:sw + gn], cb_[:, sw:sw + gn]), (cw[:, sw + gn:], cb_[:, sw + gn:])]
    zeros16 = jnp.zeros((1, SSD_HEADS), F32)

    def two_dirs(f, b):
        return jnp.stack([_pad_lanes(jnp.concatenate([f, zeros16], 1)), _pad_lanes(jnp.concatenate([zeros16, b], 1))])

    alog2 = two_dirs(p["ssd_a_log_fwd"], p["ssd_a_log_bwd"])
    dtb2 = two_dirs(p["ssd_dt_bias_fwd"], p["ssd_dt_bias_bwd"])
    dvec = _pad_lanes(p["ssd_d"]).reshape(LANES, 1)

    hn, z, xs_pre, b_pre, c_pre, dtr, u = _norm_matmul(x, p["norm_mix_w"], in_ws, tm, "in_proj")
    pres = [xs_pre, b_pre, c_pre]
    acts = [_conv_silu(pre, w, b, bl, min(256, pre.shape[1]), f"ssd_conv_{i}") for i, (pre, (w, b)) in enumerate(zip(pres, conv_parts))]
    xs_a, b_a, c_a = acts
    y2, saved = _ssd_scan(xs_a, b_a, c_a, dtr, alog2, dtb2, bl, "ssd_scan")

    def col(a):
        return a.reshape(a.shape + (1,))

    s5_params = [
        col(p["s5_lambda_re_fwd"]), col(p["s5_lambda_im_fwd"]), p["s5_log_step_fwd"].reshape(S5_GROUPS, 1, 1), p["s5_c_re_fwd"], p["s5_c_im_fwd"],
        col(p["s5_lambda_re_bwd"]), col(p["s5_lambda_im_bwd"]), p["s5_log_step_bwd"].reshape(S5_GROUPS, 1, 1), p["s5_c_re_bwd"], p["s5_c_im_bwd"],
        p["s5_b_re"], p["s5_b_im"], col(p["s5_d"].reshape(S5_GROUPS, S5_CH)),
        p["s5_glu_w"][:, :, :S5_CH], p["s5_glu_w"][:, :, S5_CH:], col(p["s5_glu_b"][:, :S5_CH]), col(p["s5_glu_b"][:, S5_CH:]),
    ]
    s5_args = [_x_layout(u, bl)] + s5_params
    s5o = _token_layout(_s5_fwd(s5_args, nck, "s5_fwd"), bl)
    ymix = _mix(y2, xs_a, z, s5o, dvec, p["ssd_norm_w"], p["s5_norm_w"], tm, "mix")
    h1 = _matmul_res(ymix, p["w_out"], x, tm, "out_proj")
    w_up = p["w_up"]
    hn2, up_v, up_g = _norm_matmul(h1, p["norm_ffn_w"], [w_up[:, :dff], w_up[:, dff:]], tm_ffn, "ffn_up")
    fw, fb = p["ffn_conv_w"], p["ffn_conv_b"]
    act = _conv_glu(up_v, up_g, fw[:, :dff], fw[:, dff:], fb[:, :dff], fb[:, dff:], bl, 256, "ffn_conv")
    h2 = _matmul_res(act, p["w_down"], h1, tm, "ffn_down")
    loss, dh2, g_nfw = _final_loss(h2, p["norm_final_w"].reshape(1, d), tgt, tm, "final_loss")

    g = {"norm_final_w": g_nfw.reshape(d)}
    g["w_down"] = _matmul_tn(act, dh2, tm, 512, "ffn_down_dw")
    dact = _matmul_nt([dh2], [p["w_down"]], tm, "ffn_down_dx")
    dup_v, dup_g, dwv, dwg, dbv, dbg = _conv_glu_bwd(up_v, up_g, fw[:, :dff], fw[:, dff:], fb[:, :dff], fb[:, dff:], dact, bl, 256, "ffn_conv_bwd")
    g["ffn_conv_w"] = jnp.concatenate([dwv, dwg], 1)
    g["ffn_conv_b"] = jnp.concatenate([dbv, dbg], 1)
    g["w_up"] = jnp.concatenate([_matmul_tn(hn2, dup_v, tm, 256, "ffn_up_dw_v"), _matmul_tn(hn2, dup_g, tm, 256, "ffn_up_dw_g")], 1)
    dhn2 = _matmul_nt([dup_v, dup_g], [w_up[:, :dff], w_up[:, dff:]], tm_ffn, "ffn_up_dx")
    dh1, g["norm_ffn_w"] = _norm_bwd(h1, p["norm_ffn_w"], dhn2, dh2, tm, "ffn_norm_bwd")
    g["w_out"] = _matmul_tn(ymix, dh1, tm, 512, "out_proj_dw")
    dmix = _matmul_nt([dh1], [p["w_out"]], tm, "out_proj_dx")
    dyssd, dxs_gate, dz, ds5o, g_d, g["ssd_norm_w"], g["s5_norm_w"] = _mix_bwd(
        y2, xs_a, z, s5o, dvec, p["ssd_norm_w"], p["s5_norm_w"], dmix, tm, "mix_bwd")
    g["ssd_d"] = g_d[:SSD_HEADS].reshape(1, SSD_HEADS)
    s5g = _s5_bwd(s5_args, _x_layout(ds5o, bl), nck, "s5_bwd")
    du = _token_layout(s5g[0], bl)
    (g["s5_lambda_re_fwd"], g["s5_lambda_im_fwd"], g["s5_log_step_fwd"], g["s5_c_re_fwd"], g["s5_c_im_fwd"],
     g["s5_lambda_re_bwd"], g["s5_lambda_im_bwd"], g["s5_log_step_bwd"], g["s5_c_re_bwd"], g["s5_c_im_bwd"],
     g["s5_b_re"], g["s5_b_im"], g_s5d, g_wv, g_wg, g_bv, g_bg) = s5g[1:]
    for k_ in ("s5_lambda_re_fwd", "s5_lambda_im_fwd", "s5_lambda_re_bwd", "s5_lambda_im_bwd"):
        g[k_] = g[k_].reshape(S5_GROUPS, S5_STATE)
    for k_ in ("s5_log_step_fwd", "s5_log_step_bwd"):
        g[k_] = g[k_].reshape(S5_GROUPS)
    g["s5_d"] = g_s5d.reshape(1, s5w)
    g["s5_glu_w"] = jnp.concatenate([g_wv, g_wg], 2)
    g["s5_glu_b"] = jnp.concatenate([g_bv.reshape(S5_GROUPS, S5_CH), g_bg.reshape(S5_GROUPS, S5_CH)], 1)
    dxs2, dbm2, dcm2, ddt2, dal2, ddb2 = _ssd_scan_bwd(xs_a, b_a, c_a, dtr, alog2, dtb2, saved, dyssd, bl, "ssd_scan_bwd")
    g["ssd_a_log_fwd"], g["ssd_a_log_bwd"] = dal2[0, :, :SSD_HEADS], dal2[1, :, SSD_HEADS:2 * SSD_HEADS]
    g["ssd_dt_bias_fwd"], g["ssd_dt_bias_bwd"] = ddb2[0, :, :SSD_HEADS], ddb2[1, :, SSD_HEADS:2 * SSD_HEADS]
    cots = [[(dxs2, 0), (dxs2, 1), (dxs_gate, None)], [(dbm2, 0), (dbm2, 1)], [(dcm2, 0), (dcm2, 1)]]
    dpres, dcw, dcb = [], [], []
    for i, (pre, (w, b), cot) in enumerate(zip(pres, conv_parts, cots)):
        dp, dw_, db_ = _conv_silu_bwd(pre, w, b, cot, bl, min(256, pre.shape[1]), f"ssd_conv_bwd_{i}")
        dpres.append(dp)
        dcw.append(dw_)
        dcb.append(db_)
    g["ssd_conv_w"] = jnp.concatenate(dcw, 1)
    g["ssd_conv_b"] = jnp.concatenate(dcb, 1)
    ddtr = _sum_lead(ddt2, "ssd_ddt_sum")
    dprojs = [dz, dpres[0], dpres[1], dpres[2], ddtr, du]
    dws = [_matmul_tn(hn, dpj, tm, min(512, dpj.shape[1]), f"in_proj_dw_{i}") for i, dpj in enumerate(dprojs)]
    dws[4] = dws[4][:, :2 * SSD_HEADS]
    g["w_in"] = jnp.concatenate(dws, 1)
    dhn = _matmul_nt(dprojs, in_ws, tm, "in_proj_dx")
    grad_x, g["norm_mix_w"] = _norm_bwd(x, p["norm_mix_w"], dhn, dh1, tm, "mix_norm_bwd")
    return loss, grad_x, g


_WEIGHTS = ['norm_mix_w', 'w_in', 'ssd_conv_w', 'ssd_conv_b', 'ssd_dt_bias_fwd', 'ssd_dt_bias_bwd', 'ssd_a_log_fwd', 'ssd_a_log_bwd',
            'ssd_d', 'ssd_norm_w', 's5_lambda_re_fwd', 's5_lambda_im_fwd', 's5_log_step_fwd', 's5_lambda_re_bwd', 's5_lambda_im_bwd',
            's5_log_step_bwd', 's5_b_re', 's5_b_im', 's5_c_re_fwd', 's5_c_im_fwd', 's5_c_re_bwd', 's5_c_im_bwd', 's5_d', 's5_glu_w',
            's5_glu_b', 's5_norm_w', 'w_out', 'norm_ffn_w', 'ffn_w_up', 'ffn_conv_w', 'ffn_conv_b', 'ffn_w_down', 'norm_final_w']
_BIG = ('w_in', 'w_out', 'ffn_w_up', 'ffn_w_down')
_CONV = ('ssd_conv_w', 'ffn_conv_w')


def _pack(arrs):
    flat = jnp.concatenate([a.reshape(-1) for a in arrs])
    rows = -(-flat.shape[0] // (8 * LANES)) * 8
    return jnp.pad(flat, (0, rows * LANES - flat.shape[0])).reshape(rows, LANES)


def _unpack(buf, shapes):
    flat = buf.reshape(-1)
    out, off = [], 0
    for shp in shapes:
        size = math.prod(shp)
        out.append(flat[off:off + size].reshape(shp))
        off += size
    return out


def kernel(x, norm_mix_w, w_in, ssd_conv_w, ssd_conv_b, ssd_dt_bias_fwd, ssd_dt_bias_bwd, ssd_a_log_fwd, ssd_a_log_bwd, ssd_d, ssd_norm_w, s5_lambda_re_fwd, s5_lambda_im_fwd, s5_log_step_fwd, s5_lambda_re_bwd, s5_lambda_im_bwd, s5_log_step_bwd, s5_b_re, s5_b_im, s5_c_re_fwd, s5_c_im_fwd, s5_c_re_bwd, s5_c_im_bwd, s5_d, s5_glu_w, s5_glu_b, s5_norm_w, w_out, norm_ffn_w, ffn_w_up, ffn_conv_w, ffn_conv_b, ffn_w_down, norm_final_w, loss_target, m_norm_mix_w, m_w_in, m_ssd_conv_w, m_ssd_conv_b, m_ssd_dt_bias_fwd, m_ssd_dt_bias_bwd, m_ssd_a_log_fwd, m_ssd_a_log_bwd, m_ssd_d, m_ssd_norm_w, m_s5_lambda_re_fwd, m_s5_lambda_im_fwd, m_s5_log_step_fwd, m_s5_lambda_re_bwd, m_s5_lambda_im_bwd, m_s5_log_step_bwd, m_s5_b_re, m_s5_b_im, m_s5_c_re_fwd, m_s5_c_im_fwd, m_s5_c_re_bwd, m_s5_c_im_bwd, m_s5_d, m_s5_glu_w, m_s5_glu_b, m_s5_norm_w, m_w_out, m_norm_ffn_w, m_ffn_w_up, m_ffn_conv_w, m_ffn_conv_b, m_ffn_w_down, m_norm_final_w, v_norm_mix_w, v_w_in, v_ssd_conv_w, v_ssd_conv_b, v_ssd_dt_bias_fwd, v_ssd_dt_bias_bwd, v_ssd_a_log_fwd, v_ssd_a_log_bwd, v_ssd_d, v_ssd_norm_w, v_s5_lambda_re_fwd, v_s5_lambda_im_fwd, v_s5_log_step_fwd, v_s5_lambda_re_bwd, v_s5_lambda_im_bwd, v_s5_log_step_bwd, v_s5_b_re, v_s5_b_im, v_s5_c_re_fwd, v_s5_c_im_fwd, v_s5_c_re_bwd, v_s5_c_im_bwd, v_s5_d, v_s5_glu_w, v_s5_glu_b, v_s5_norm_w, v_w_out, v_norm_ffn_w, v_ffn_w_up, v_ffn_conv_w, v_ffn_conv_b, v_ffn_w_down, v_norm_final_w):
    args = dict(locals())
    w = {k_: args[k_] for k_ in _WEIGHTS}
    m = {k_: args["m_" + k_] for k_ in _WEIGHTS}
    v = {k_: args["v_" + k_] for k_ in _WEIGHTS}
    bl, sl, d = x.shape
    chip = 2 * lax.axis_index("x") + lax.axis_index("y")
    core = lax.axis_index("c")

    shards = [w[k_][0].astype(BF16) for k_ in _BIG] + [w[k_][0] for k_ in _CONV]
    g_in, g_out, g_up, g_down, g_scw, g_fcw = _gather_xy(shards, "gather_weights")

    def cols(a):
        return jnp.moveaxis(a, 0, 1).reshape(a.shape[1], 4 * a.shape[2])

    p = {k_: (w[k_][0] if w[k_].ndim >= 3 else w[k_]) for k_ in _WEIGHTS if k_ not in _BIG + _CONV}
    p["w_in"], p["w_up"] = cols(g_in), cols(g_up)
    p["w_out"], p["w_down"] = g_out.reshape(-1, g_out.shape[2]), g_down.reshape(-1, g_down.shape[2])
    p["ssd_conv_w"], p["ffn_conv_w"] = cols(g_scw), cols(g_fcw)

    loss, grad_x, g = _local_step(x.reshape(bl * sl, d), loss_target.reshape(bl * sl, d), p, bl)
    g["ffn_w_up"], g["ffn_w_down"] = g.pop("w_up"), g.pop("w_down")

    def owner_major(a, k_):
        r, c = w[k_].shape[1:]
        if a.shape[0] == r:
            a = jnp.moveaxis(a.reshape(r, 4, c), 1, 0)
        else:
            a = a.reshape(4, r, c)
        return a.reshape(4, 2, r // 2, c)

    parts = [jnp.moveaxis(owner_major(g[k_], k_), 1, 0) for k_ in _BIG]
    got = _swap_sibling(parts, "reduce_sibling")
    mine = [lax.dynamic_index_in_dim(pt, core, 0, keepdims=False) for pt in parts]
    chip_sums = [_add2(a, b, f"reduce_add_{i}") for i, (a, b) in enumerate(zip(mine, got))]
    from_chips = _scatter_xy(chip_sums, "reduce_chips")
    halves = [_sum_lead(a.reshape(4, -1, a.shape[-1]), f"reduce_sum_{i}") for i, a in enumerate(from_chips)]
    joined = _join_sibling(halves, "reduce_join")
    big_grad = {k_: j.reshape((1,) + w[k_].shape[1:]) for k_, j in zip(_BIG, joined)}

    small = [k_ for k_ in _WEIGHTS if k_ not in _BIG]
    small_full_shapes = [g[k_].shape for k_ in small]
    buf = _pack([g[k_] for k_ in small] + [loss[0, :1]])
    tot = _sum_lead(_bcast_all(buf, "reduce_small"), "reduce_small_sum")
    unp = _unpack(tot, small_full_shapes + [(1,)])
    small_grad = dict(zip(small, unp[:-1]))
    loss_out = unp[-1].reshape(())
    for k_ in _CONV:
        cshard = w[k_].shape[2]
        small_grad[k_] = lax.dynamic_slice_in_dim(small_grad[k_], chip * cshard, cshard, 1)

    grads, deltas, new_m, new_v = {}, {}, {}, {}
    for k_ in _BIG:
        shp = w[k_].shape
        grads[k_] = big_grad[k_]
        dl, nm, nv = _adamw(w[k_][0], big_grad[k_][0], m[k_][0], v[k_][0], f"adamw_{k_}")
        deltas[k_], new_m[k_], new_v[k_] = dl.reshape(shp), nm.reshape(shp), nv.reshape(shp)
    sw_ = _pack([w[k_] for k_ in small])
    sg_ = _pack([small_grad[k_] for k_ in small])
    sm_ = _pack([m[k_] for k_ in small])
    sv_ = _pack([v[k_] for k_ in small])
    dl, nm, nv = _adamw(sw_, sg_, sm_, sv_, "adamw_small")
    shapes = [w[k_].shape for k_ in small]
    for k_, a, b, c_ in zip(small, _unpack(dl, shapes), _unpack(nm, shapes), _unpack(nv, shapes)):
        deltas[k_], new_m[k_], new_v[k_] = a, b, c_
        grads[k_] = small_grad[k_].reshape(w[k_].shape)
    return (loss_out, grad_x.reshape(bl, sl, d), *[grads[k_] for k_ in _WEIGHTS], *[deltas[k_] for k_ in _WEIGHTS],
            *[new_m[k_] for k_ in _WEIGHTS], *[new_v[k_] for k_ in _WEIGHTS])
```

```python
import functools
import math

import jax
import jax.numpy as jnp
from jax import lax
from jax.experimental import pallas as pl
from jax.experimental.pallas import tpu as pltpu

F32 = jnp.float32
BF16 = jnp.bfloat16
HI = lax.Precision.HIGHEST
SDS = jax.ShapeDtypeStruct
MESH = pl.DeviceIdType.MESH

NN = (((1,), (0,)), ((), ()))
NT = (((1,), (1,)), ((), ()))
TN = (((0,), (0,)), ((), ()))

EPS = 1e-6
SSD_HEADS = 16
SSD_HEAD_DIM = 64
SSD_GROUPS = 4
SSD_STATE = 128
SSD_CHUNK = 128
SSD_CONV = 5
S5_GROUPS = 32
S5_CH = 16
S5_STATE = 64
S5_T = 16
LANES = 128
ADAM_LR, ADAM_B1, ADAM_B2, ADAM_EPS, ADAM_WD, ADAM_STEP = 0.001, 0.9, 0.999, 1e-08, 0.01, 10
V7X_VMEM_BYTES = 64 * 1024 * 1024
VMEM_LIMIT = V7X_VMEM_BYTES - 8 * 1024 * 1024


def _cp(sem, vmem=None):
    return pltpu.CompilerParams(dimension_semantics=sem, vmem_limit_bytes=vmem)


def _dot(a, b, dims=NN, precision=None):
    return lax.dot_general(a, b, dims, precision=precision, preferred_element_type=F32)


def _rms(x, w):
    return x * lax.rsqrt(jnp.mean(x * x, axis=-1, keepdims=True) + EPS) * w


def _sigmoid(x):
    return 1.0 / (1.0 + jnp.exp(-x))


def _softplus(x):
    return jnp.maximum(x, 0.0) + jnp.log1p(jnp.exp(-jnp.abs(x)))


@functools.partial(jax.custom_vjp, nondiff_argnums=(1, 2))
def _shift(x, k, seg):
    n = x.shape[0]
    r = lax.broadcasted_iota(jnp.int32, x.shape, 0) % seg
    y = pltpu.roll(x, k % n, 0)
    ok = (r >= k) if k > 0 else (r < seg + k)
    return jnp.where(ok, y, 0.0)


def _shift_fwd(x, k, seg):
    return _shift(x, k, seg), None


def _shift_bwd(k, seg, _, g):
    return (_shift(g, -k, seg),)


_shift.defvjp(_shift_fwd, _shift_bwd)


@jax.custom_vjp
def _swap(z):
    return pltpu.roll(z, LANES // 2, 1)


_swap.defvjp(lambda z: (_swap(z), None), lambda _, g: (_swap(g),))


def _norm_matmul(x, nw, ws, tm, name):
    n, d = x.shape
    k = len(ws)

    def body(x_ref, nw_ref, *refs):
        hn = _rms(x_ref[...], nw_ref[...]).astype(BF16)
        refs[k][...] = hn
        for w_ref, o_ref in zip(refs[:k], refs[k + 1:]):
            o_ref[...] = _dot(hn, w_ref[...])

    row = lambda i: (i, 0)
    fix = lambda i: (0, 0)
    return pl.pallas_call(
        body, name=name, grid=(n // tm,),
        in_specs=[pl.BlockSpec((tm, d), row), pl.BlockSpec((1, d), fix)] + [pl.BlockSpec(w.shape, fix) for w in ws],
        out_specs=[pl.BlockSpec((tm, d), row)] + [pl.BlockSpec((tm, w.shape[1]), row) for w in ws],
        out_shape=[SDS((n, d), BF16)] + [SDS((n, w.shape[1]), F32) for w in ws],
        compiler_params=_cp(("arbitrary",), VMEM_LIMIT),
    )(x, nw, *ws)


def _matmul_res(a, w, res, tm, name):
    n, kd = a.shape
    m = w.shape[1]

    def body(a_ref, w_ref, r_ref, o_ref):
        o_ref[...] = r_ref[...] + _dot(a_ref[...], w_ref[...])

    return pl.pallas_call(
        body, name=name, grid=(n // tm,),
        in_specs=[pl.BlockSpec((tm, kd), lambda i: (i, 0)), pl.BlockSpec((kd, m), lambda i: (0, 0)),
                  pl.BlockSpec((tm, m), lambda i: (i, 0))],
        out_specs=pl.BlockSpec((tm, m), lambda i: (i, 0)),
        out_shape=SDS((n, m), F32),
        compiler_params=_cp(("arbitrary",), VMEM_LIMIT),
    )(a, w, res)


def _matmul_nt(gs, ws, tm, name):
    n = gs[0].shape[0]
    kd = ws[0].shape[0]
    cnt = len(gs)

    def body(*refs):
        acc = None
        for g_ref, w_ref in zip(refs[:cnt], refs[cnt:2 * cnt]):
            t = _dot(g_ref[...].astype(BF16), w_ref[...], NT)
            acc = t if acc is None else acc + t
        refs[2 * cnt][...] = acc

    return pl.pallas_call(
        body, name=name, grid=(n // tm,),
        in_specs=[pl.BlockSpec((tm, g.shape[1]), lambda i: (i, 0)) for g in gs]
        + [pl.BlockSpec(w.shape, lambda i: (0, 0)) for w in ws],
        out_specs=pl.BlockSpec((tm, kd), lambda i: (i, 0)),
        out_shape=SDS((n, kd), F32),
        compiler_params=_cp(("arbitrary",), VMEM_LIMIT),
    )(*gs, *ws)


def _matmul_tn(a, g, tm, cb, name):
    n, kd = a.shape
    m = g.shape[1]

    def body(a_ref, g_ref, o_ref):
        t = _dot(a_ref[...], g_ref[...].astype(BF16), TN)

        @pl.when(pl.program_id(1) == 0)
        def _():
            o_ref[...] = t

        @pl.when(pl.program_id(1) != 0)
        def _():
            o_ref[...] += t

    return pl.pallas_call(
        body, name=name, grid=(m // cb, n // tm),
        in_specs=[pl.BlockSpec((tm, kd), lambda j, i: (i, 0)), pl.BlockSpec((tm, cb), lambda j, i: (i, j))],
        out_specs=pl.BlockSpec((kd, cb), lambda j, i: (0, j)),
        out_shape=SDS((kd, m), F32),
        compiler_params=_cp(("arbitrary", "arbitrary"), VMEM_LIMIT),
    )(a, g)


def _dwconv(x, w, b):
    kw = w.shape[0]
    acc = b
    for k in range(kw):
        acc = acc + w[k:k + 1, :] * _shift(x, kw // 2 - k, x.shape[0])
    return acc


def _conv_silu_fn(x, w, b):
    y = _dwconv(x, w, b)
    return y * _sigmoid(y)


def _conv_glu_fn(v, g, wv, wg, bv, bg):
    cv = _dwconv(v, wv, bv)
    cg = _dwconv(g, wg, bg)
    return cg * _sigmoid(cg) * cv


def _conv_silu(x, w, b, bl, cb, name):
    n, c = x.shape
    sl = n // bl
    kw = w.shape[0]

    def body(x_ref, w_ref, b_ref, o_ref):
        o_ref[...] = _conv_silu_fn(x_ref[...], w_ref[...], b_ref[...])

    return pl.pallas_call(
        body, name=name, grid=(bl, c // cb),
        in_specs=[pl.BlockSpec((sl, cb), lambda s, j: (s, j)), pl.BlockSpec((kw, cb), lambda s, j: (0, j)),
                  pl.BlockSpec((1, cb), lambda s, j: (0, j))],
        out_specs=pl.BlockSpec((sl, cb), lambda s, j: (s, j)),
        out_shape=SDS((n, c), F32),
        compiler_params=_cp(("arbitrary", "arbitrary"), VMEM_LIMIT),
    )(x, w, b)


def _conv_silu_bwd(x, w, b, dys, bl, cb, name):
    n, c = x.shape
    sl = n // bl
    kw = w.shape[0]
    cnt = len(dys)

    def body(x_ref, w_ref, b_ref, *refs):
        dy = refs[0][...]
        for r in refs[1:cnt]:
            dy = dy + r[...]
        dx_ref, dw_ref, db_ref = refs[cnt:]
        _, vjp = jax.vjp(_conv_silu_fn, x_ref[...], w_ref[...], b_ref[...])
        dx, dw, db = vjp(dy)
        dx_ref[...] = dx

        @pl.when(pl.program_id(1) == 0)
        def _():
            dw_ref[...] = dw
            db_ref[...] = db

        @pl.when(pl.program_id(1) != 0)
        def _():
            dw_ref[...] += dw
            db_ref[...] += db

    dy_specs = []
    for arr, lead in dys:
        if lead is None:
            dy_specs.append(pl.BlockSpec((sl, cb), lambda j, s: (s, j)))
        else:
            dy_specs.append(pl.BlockSpec((None, sl, cb), functools.partial(lambda j, s, lead: (lead, s, j), lead=lead)))
    return pl.pallas_call(
        body, name=name, grid=(c // cb, bl),
        in_specs=[pl.BlockSpec((sl, cb), lambda j, s: (s, j)), pl.BlockSpec((kw, cb), lambda j, s: (0, j)),
                  pl.BlockSpec((1, cb), lambda j, s: (0, j))] + dy_specs,
        out_specs=[pl.BlockSpec((sl, cb), lambda j, s: (s, j)), pl.BlockSpec((kw, cb), lambda j, s: (0, j)),
                   pl.BlockSpec((1, cb), lambda j, s: (0, j))],
        out_shape=[SDS((n, c), F32), SDS((kw, c), F32), SDS((1, c), F32)],
        compiler_params=_cp(("arbitrary", "arbitrary"), VMEM_LIMIT),
    )(x, w, b, *[a for a, _ in dys])


def _conv_glu(v, g, wv, wg, bv, bg, bl, cb, name):
    n, c = v.shape
    sl = n // bl
    kw = wv.shape[0]

    def body(v_ref, g_ref, wv_ref, wg_ref, bv_ref, bg_ref, o_ref):
        o_ref[...] = _conv_glu_fn(v_ref[...], g_ref[...], wv_ref[...], wg_ref[...], bv_ref[...], bg_ref[...]).astype(BF16)

    big = pl.BlockSpec((sl, cb), lambda s, j: (s, j))
    wsp = pl.BlockSpec((kw, cb), lambda s, j: (0, j))
    bsp = pl.BlockSpec((1, cb), lambda s, j: (0, j))
    return pl.pallas_call(
        body, name=name, grid=(bl, c // cb),
        in_specs=[big, big, wsp, wsp, bsp, bsp], out_specs=big, out_shape=SDS((n, c), BF16),
        compiler_params=_cp(("arbitrary", "arbitrary"), VMEM_LIMIT),
    )(v, g, wv, wg, bv, bg)


def _conv_glu_bwd(v, g, wv, wg, bv, bg, dact, bl, cb, name):
    n, c = v.shape
    sl = n // bl
    kw = wv.shape[0]

    def body(v_ref, g_ref, wv_ref, wg_ref, bv_ref, bg_ref, da_ref, dv_ref, dg_ref, dwv_ref, dwg_ref, dbv_ref, dbg_ref):
        _, vjp = jax.vjp(_conv_glu_fn, v_ref[...], g_ref[...], wv_ref[...], wg_ref[...], bv_ref[...], bg_ref[...])
        dv, dg, dwv, dwg, dbv, dbg = vjp(da_ref[...])
        dv_ref[...] = dv
        dg_ref[...] = dg

        @pl.when(pl.program_id(1) == 0)
        def _():
            dwv_ref[...] = dwv
            dwg_ref[...] = dwg
            dbv_ref[...] = dbv
            dbg_ref[...] = dbg

        @pl.when(pl.program_id(1) != 0)
        def _():
            dwv_ref[...] += dwv
            dwg_ref[...] += dwg
            dbv_ref[...] += dbv
            dbg_ref[...] += dbg

    big = pl.BlockSpec((sl, cb), lambda j, s: (s, j))
    wsp = pl.BlockSpec((kw, cb), lambda j, s: (0, j))
    bsp = pl.BlockSpec((1, cb), lambda j, s: (0, j))
    return pl.pallas_call(
        body, name=name, grid=(c // cb, bl),
        in_specs=[big, big, wsp, wsp, bsp, bsp, big],
        out_specs=[big, big, wsp, wsp, bsp, bsp],
        out_shape=[SDS((n, c), F32), SDS((n, c), F32), SDS((kw, c), F32), SDS((kw, c), F32), SDS((1, c), F32), SDS((1, c), F32)],
        compiler_params=_cp(("arbitrary", "arbitrary"), VMEM_LIMIT),
    )(v, g, wv, wg, bv, bg, dact)


def _ssd_chunk_fn(rev, xs, dtr, bms, cms, st, alog, dtb):
    q = dtr.shape[0]
    lane = lax.broadcasted_iota(jnp.int32, (1, LANES), 1)
    sub = lax.broadcasted_iota(jnp.int32, (LANES, 1), 0)
    r = lax.broadcasted_iota(jnp.int32, (q, q), 0)
    c = lax.broadcasted_iota(jnp.int32, (q, q), 1)
    sgn = 1 - 2 * rev
    mask = (r - c) * sgn >= 0
    tri = mask.astype(F32)
    tri_t = ((c - r) * sgn >= 0).astype(F32)
    dt = _softplus(dtr + dtb)
    dta = dt * (-jnp.exp(alog))
    cs = _dot(tri, dta, NN, HI)
    cs_t = _dot(dta, tri_t, TN, HI)
    tot = jnp.sum(dta, axis=0, keepdims=True)
    ys, outs = [], []
    for h in range(SSD_HEADS):
        g = h // (SSD_HEADS // SSD_GROUPS)
        bg, cg = bms[g], cms[g]
        if h % (SSD_HEADS // SSD_GROUPS) == 0:
            scores = _dot(cg, bg, NT)
        hl = 16 * rev + h
        ohl = (lane == hl).astype(F32)
        ohs = (sub == hl).astype(F32)
        col = jnp.sum(cs * ohl, axis=1, keepdims=True)
        row = jnp.sum(cs_t * ohs, axis=0, keepdims=True)
        dth = jnp.sum(dt * ohl, axis=1, keepdims=True)
        toth = jnp.sum(tot * ohl, axis=1, keepdims=True)
        seg = jnp.where(mask, jnp.exp(jnp.where(mask, col - row, 0.0)), 0.0)
        xdt = xs[h] * dth
        y = _dot(scores * seg, xdt) + jnp.exp(col) * _dot(cg, st[h], NT)
        new = _dot(xdt * jnp.exp(toth - col), bg, TN)
        ys.append(y)
        outs.append(jnp.exp(toth) * st[h] + new)
    return ys, outs


def _ssd_scan(xs, bm, cm, dtr, alog2, dtb2, bl, name):
    n = xs.shape[0]
    q = SSD_CHUNK
    nc = n // bl // q
    hd, ns = SSD_HEAD_DIM, SSD_STATE

    def body(xs_ref, b_ref, c_ref, dt_ref, al_ref, db_ref, y_ref, sv_ref, st_ref):
        d, i = pl.program_id(0), pl.program_id(2)

        @pl.when(i == 0)
        def _():
            st_ref[...] = jnp.zeros(st_ref.shape, F32)

        st = [st_ref[h] for h in range(SSD_HEADS)]
        sv_ref[...] = st_ref[...]
        xl = [xs_ref[:, hd * h:hd * (h + 1)] for h in range(SSD_HEADS)]
        bms = [b_ref[:, ns * g:ns * (g + 1)] for g in range(SSD_GROUPS)]
        cms = [c_ref[:, ns * g:ns * (g + 1)] for g in range(SSD_GROUPS)]
        ys, outs = _ssd_chunk_fn(d, xl, dt_ref[...], bms, cms, st, al_ref[...], db_ref[...])
        for h in range(SSD_HEADS):
            st_ref[h] = outs[h]
            y_ref[:, hd * h:hd * (h + 1)] = ys[h]

    def rowblk(d, s, i):
        return s * nc + i + d * (nc - 1 - 2 * i)

    return pl.pallas_call(
        body, name=name, grid=(2, bl, nc),
        in_specs=[pl.BlockSpec((q, SSD_HEADS * hd), lambda d, s, i: (rowblk(d, s, i), 0)),
                  pl.BlockSpec((q, SSD_GROUPS * ns), lambda d, s, i: (rowblk(d, s, i), 0)),
                  pl.BlockSpec((q, SSD_GROUPS * ns), lambda d, s, i: (rowblk(d, s, i), 0)),
                  pl.BlockSpec((q, LANES), lambda d, s, i: (rowblk(d, s, i), 0)),
                  pl.BlockSpec((None, 1, LANES), lambda d, s, i: (d, 0, 0)),
                  pl.BlockSpec((None, 1, LANES), lambda d, s, i: (d, 0, 0))],
        out_specs=[pl.BlockSpec((None, q, SSD_HEADS * hd), lambda d, s, i: (d, rowblk(d, s, i), 0)),
                   pl.BlockSpec((None, None, SSD_HEADS, hd, ns), lambda d, s, i: (d, rowblk(d, s, i), 0, 0, 0))],
        out_shape=[SDS((2, n, SSD_HEADS * hd), F32), SDS((2, n // q, SSD_HEADS, hd, ns), F32)],
        scratch_shapes=[pltpu.VMEM((SSD_HEADS, hd, ns), F32)],
        compiler_params=_cp(("arbitrary",) * 3, VMEM_LIMIT),
    )(xs, bm, cm, dtr, alog2, dtb2)


def _ssd_scan_bwd(xs, bm, cm, dtr, alog2, dtb2, saved, dy, bl, name):
    n = xs.shape[0]
    q = SSD_CHUNK
    nc = n // bl // q
    hd, ns = SSD_HEAD_DIM, SSD_STATE

    def body(xs_ref, b_ref, c_ref, dt_ref, al_ref, db_ref, sv_ref, dy_ref,
             dxs_ref, dbm_ref, dcm_ref, ddt_ref, dal_ref, ddb_ref, ds_ref):
        d, s, i = pl.program_id(0), pl.program_id(1), pl.program_id(2)

        @pl.when(i == 0)
        def _():
            ds_ref[...] = jnp.zeros(ds_ref.shape, F32)

        xl = [xs_ref[:, hd * h:hd * (h + 1)] for h in range(SSD_HEADS)]
        bms = [b_ref[:, ns * g:ns * (g + 1)] for g in range(SSD_GROUPS)]
        cms = [c_ref[:, ns * g:ns * (g + 1)] for g in range(SSD_GROUPS)]
        st = [sv_ref[h] for h in range(SSD_HEADS)]
        fn = functools.partial(_ssd_chunk_fn, d)
        _, vjp = jax.vjp(fn, xl, dt_ref[...], bms, cms, st, al_ref[...], db_ref[...])
        dys = [dy_ref[:, hd * h:hd * (h + 1)] for h in range(SSD_HEADS)]
        dso = [ds_ref[h] for h in range(SSD_HEADS)]
        dxl, ddt, dbg, dcg, dst, dal, ddb = vjp((dys, dso))
        for h in range(SSD_HEADS):
            ds_ref[h] = dst[h]
            dxs_ref[:, hd * h:hd * (h + 1)] = dxl[h]
        for g in range(SSD_GROUPS):
            dbm_ref[:, ns * g:ns * (g + 1)] = dbg[g]
            dcm_ref[:, ns * g:ns * (g + 1)] = dcg[g]
        ddt_ref[...] = ddt
        _acc_rows((dal_ref, ddb_ref), (dal, ddb), jnp.logical_and(s == 0, i == 0))

    def rowblk(d, s, i):
        return s * nc + (nc - 1 - i) + d * (2 * i - (nc - 1))

    row = lambda d, s, i: (rowblk(d, s, i), 0)
    drow = lambda d, s, i: (d, rowblk(d, s, i), 0)
    dfix = lambda d, s, i: (d, 0, 0)
    return pl.pallas_call(
        body, name=name, grid=(2, bl, nc),
        in_specs=[pl.BlockSpec((q, SSD_HEADS * hd), row), pl.BlockSpec((q, SSD_GROUPS * ns), row),
                  pl.BlockSpec((q, SSD_GROUPS * ns), row), pl.BlockSpec((q, LANES), row),
                  pl.BlockSpec((None, 1, LANES), dfix), pl.BlockSpec((None, 1, LANES), dfix),
                  pl.BlockSpec((None, None, SSD_HEADS, hd, ns), lambda d, s, i: (d, rowblk(d, s, i), 0, 0, 0)),
                  pl.BlockSpec((q, SSD_HEADS * hd), row)],
        out_specs=[pl.BlockSpec((None, q, SSD_HEADS * hd), drow), pl.BlockSpec((None, q, SSD_GROUPS * ns), drow),
                   pl.BlockSpec((None, q, SSD_GROUPS * ns), drow), pl.BlockSpec((None, q, LANES), drow),
                   pl.BlockSpec((None, 1, LANES), dfix), pl.BlockSpec((None, 1, LANES), dfix)],
        out_shape=[SDS((2, n, SSD_HEADS * hd), F32), SDS((2, n, SSD_GROUPS * ns), F32), SDS((2, n, SSD_GROUPS * ns), F32),
                   SDS((2, n, LANES), F32), SDS((2, 1, LANES), F32), SDS((2, 1, LANES), F32)],
        scratch_shapes=[pltpu.VMEM((SSD_HEADS, hd, ns), F32)],
        compiler_params=_cp(("arbitrary",) * 3, VMEM_LIMIT),
    )(xs, bm, cm, dtr, alog2, dtb2, saved, dy)


def _s5_consts():
    t, ch, p = S5_T, S5_CH, S5_STATE
    lane = lax.broadcasted_iota(jnp.int32, (1, 2 * p), 1)
    pr = lax.broadcasted_iota(jnp.int32, (p, 2 * p), 0)
    pc = lax.broadcasted_iota(jnp.int32, (p, 2 * p), 1)
    cr = lax.broadcasted_iota(jnp.int32, (ch, t * ch), 0)
    cc = lax.broadcasted_iota(jnp.int32, (ch, t * ch), 1)
    rr = lax.broadcasted_iota(jnp.int32, (t * ch, t * ch), 0)
    rc = lax.broadcasted_iota(jnp.int32, (t * ch, t * ch), 1)
    return dict(
        left=lane < p,
        sg=jnp.where(lane < p, -1.0, 1.0).astype(F32),
        dup=(pc % p == pr).astype(F32),
        dup_l=(pc == pr).astype(F32),
        dup_r=(pc == pr + p).astype(F32),
        rep=(cc % ch == cr).astype(F32),
        dq=rc // ch - rr // ch,
    )


def _tile_rows(a, times):
    return jnp.concatenate([a] * times, axis=0)


def _s5_mats(k, rev, lr, li, ls, bre, bim, cre, cim):
    t = S5_T
    step = jnp.exp(ls)
    lr2 = jnp.sum(lr * k["dup"], axis=0, keepdims=True)
    li2 = jnp.sum(li * k["dup"], axis=0, keepdims=True)

    def erow(d):
        ang = (d * step) * li2
        return jnp.exp((d * step) * lr2) * jnp.where(k["left"], jnp.cos(ang), jnp.sin(ang))

    es = [erow(d) for d in range(t + 1)]
    mag = jnp.exp(step * lr)
    ar, ai = mag * jnp.cos(step * li), mag * jnp.sin(step * li)
    den = lr * lr + li * li
    zr = ((ar - 1.0) * lr + ai * li) / den
    zi = (ai * lr - (ar - 1.0) * li) / den
    bbr = zr * bre - zi * bim
    bbi = zr * bim + zi * bre
    bt1 = _dot(bbr, k["dup"], TN, HI)
    bt2 = _dot(bbi, k["dup"], TN, HI)
    bst = _dot(bbr, k["dup_l"], TN, HI) - _dot(bbi, k["dup_r"], TN, HI)
    c1 = _dot(cre, k["dup"], NN, HI)
    c2 = _dot(cim, k["dup"], NN, HI)
    sg = k["sg"]
    ce = [e * c1 + sg * _swap(e) * c2 for e in es]
    toep = None
    for d in range(t):
        kt = _dot(bst, ce[d], NT, HI)
        tile = _tile_rows(_dot(kt, k["rep"], NN, HI), t)
        term = jnp.where(k["dq"] == (-d if rev else d), tile, 0.0)
        toep = term if toep is None else toep + term
    w_out = jnp.concatenate([ce[(t - qq) if rev else (qq + 1)] * (-sg) for qq in range(t)], axis=0)
    w_st = jnp.concatenate(
        [(lambda e: e * bt1 + sg * _swap(e) * bt2)(es[s if rev else (t - 1 - s)]) for s in range(t)], axis=0)
    return toep, w_out, w_st, es[t]


def _cmul_row(k, e, z):
    es = _swap(e)
    return z * jnp.where(k["left"], e, es) + k["sg"] * _swap(z) * jnp.where(k["left"], es, e)


def _s5_dir(k, rev, nck, x, mats):
    toep, w_out, w_st, a_t = mats
    acc = _dot(x, w_st)
    e = a_t
    kk = 1
    sign = -1 if rev else 1
    while kk < nck:
        acc = acc + _cmul_row(k, e, _shift(acc, sign * kk, nck))
        e = _cmul_row(k, e, e)
        kk *= 2
    prev = _shift(acc, sign, nck)
    return _dot(x, toep) + _dot(prev, w_out, NT)


def _s5_group_fn(nck, x, pf, pb, bre, bim, dcol, wv, wg, bv, bg):
    k = _s5_consts()
    t = S5_T
    y = x * jnp.sum(dcol * k["rep"], axis=0, keepdims=True)
    for rev, (lr, li, ls, cre, cim) in ((False, pf), (True, pb)):
        y = y + _s5_dir(k, rev, nck, x, _s5_mats(k, rev, lr, li, ls, bre, bim, cre, cim))
    gy = jax.nn.gelu(y)
    same = k["dq"] == 0
    kv = jnp.where(same, _tile_rows(_dot(wv, k["rep"], NN, HI), t), 0.0)
    kg = jnp.where(same, _tile_rows(_dot(wg, k["rep"], NN, HI), t), 0.0)
    val = _dot(gy, kv) + jnp.sum(bv * k["rep"], axis=0, keepdims=True)
    gate = _dot(gy, kg) + jnp.sum(bg * k["rep"], axis=0, keepdims=True)
    return val * _sigmoid(gate)


def _s5_specs(r):
    p, ch = S5_STATE, S5_CH
    g3 = lambda i: (i, 0, 0)
    col = pl.BlockSpec((None, p, 1), g3)
    one = pl.BlockSpec((None, 1, 1), g3)
    cmat = pl.BlockSpec((None, ch, p), g3)
    bmat = pl.BlockSpec((None, p, ch), g3)
    ccol = pl.BlockSpec((None, ch, 1), g3)
    sq = pl.BlockSpec((None, ch, ch), g3)
    xs = pl.BlockSpec((None, r, S5_T * ch), g3)
    specs = [xs, col, col, one, cmat, cmat, col, col, one, cmat, cmat, bmat, bmat, ccol, sq, sq, ccol, ccol]
    return specs


def _s5_unpack(vals):
    x = vals[0]
    pf = tuple(vals[1:6])
    pb = tuple(vals[6:11])
    bre, bim, dcol, wv, wg, bv, bg = vals[11:18]
    return x, pf, pb, bre, bim, dcol, wv, wg, bv, bg


def _s5_fwd(args, nck, name):
    x = args[0]
    ng, r, w = x.shape

    def body(*refs):
        vals = [ref[...] for ref in refs[:18]]
        refs[18][...] = _s5_group_fn(nck, *_s5_unpack(vals))

    specs = _s5_specs(r)
    return pl.pallas_call(
        body, name=name, grid=(ng,), in_specs=specs, out_specs=specs[0], out_shape=SDS(x.shape, F32),
        compiler_params=_cp(("arbitrary",), VMEM_LIMIT),
    )(*args)


def _s5_bwd(args, dy, nck, name):
    x = args[0]
    ng, r, w = x.shape

    def body(*refs):
        vals = [ref[...] for ref in refs[:18]]
        _, vjp = jax.vjp(lambda *v: _s5_group_fn(nck, *_s5_unpack(v)), *vals)
        grads = vjp(refs[18][...])
        for o_ref, gval in zip(refs[19:], grads):
            o_ref[...] = gval

    specs = _s5_specs(r)
    return pl.pallas_call(
        body, name=name, grid=(ng,), in_specs=specs + [specs[0]], out_specs=specs,
        out_shape=[SDS(a.shape, F32) for a in args],
        compiler_params=_cp(("arbitrary",), VMEM_LIMIT),
    )(*args, dy)


def _mix_fn(yf, yb, xs, z, s5o, dvec, nw_ssd, nw_s5):
    hr = lax.broadcasted_iota(jnp.int32, (LANES, SSD_HEADS * SSD_HEAD_DIM), 0)
    hc = lax.broadcasted_iota(jnp.int32, (LANES, SSD_HEADS * SSD_HEAD_DIM), 1)
    expand = (hc // SSD_HEAD_DIM == hr).astype(F32)
    dch = jnp.sum(dvec * expand, axis=0, keepdims=True)
    y = (yf + yb + dch * xs) * (z * _sigmoid(z))
    return _rms(y, nw_ssd), _rms(s5o, nw_s5)


def _mix(y2, xs, z, s5o, dvec, nw_ssd, nw_s5, tm, name):
    n, c1 = xs.shape
    c2 = s5o.shape[1]

    def body(yf_ref, yb_ref, xs_ref, z_ref, s_ref, d_ref, n1_ref, n2_ref, o_ref):
        o1, o2 = _mix_fn(yf_ref[...], yb_ref[...], xs_ref[...], z_ref[...], s_ref[...], d_ref[...], n1_ref[...], n2_ref[...])
        o_ref[:, :c1] = o1.astype(BF16)
        o_ref[:, c1:] = o2.astype(BF16)

    row = lambda i: (i, 0)
    fix = lambda i: (0, 0)
    return pl.pallas_call(
        body, name=name, grid=(n // tm,),
        in_specs=[pl.BlockSpec((None, tm, c1), lambda i: (0, i, 0)), pl.BlockSpec((None, tm, c1), lambda i: (1, i, 0)),
                  pl.BlockSpec((tm, c1), row), pl.BlockSpec((tm, c1), row), pl.BlockSpec((tm, c2), row),
                  pl.BlockSpec((LANES, 1), fix), pl.BlockSpec((1, c1), fix), pl.BlockSpec((1, c2), fix)],
        out_specs=pl.BlockSpec((tm, c1 + c2), row), out_shape=SDS((n, c1 + c2), BF16),
        compiler_params=_cp(("arbitrary",), VMEM_LIMIT),
    )(y2, y2, xs, z, s5o, dvec, nw_ssd, nw_s5)


def _acc_rows(refs, vals, first):
    @pl.when(first)
    def _():
        for ref, v in zip(refs, vals):
            ref[...] = v

    @pl.when(jnp.logical_not(first))
    def _():
        for ref, v in zip(refs, vals):
            ref[...] += v


def _mix_bwd(y2, xs, z, s5o, dvec, nw_ssd, nw_s5, dmix, tm, name):
    n, c1 = xs.shape
    c2 = s5o.shape[1]

    def body(yf_ref, yb_ref, xs_ref, z_ref, s_ref, d_ref, n1_ref, n2_ref, dm_ref,
             dy_ref, dxs_ref, dz_ref, ds_ref, dd_ref, dn1_ref, dn2_ref):
        _, vjp = jax.vjp(_mix_fn, yf_ref[...], yb_ref[...], xs_ref[...], z_ref[...], s_ref[...], d_ref[...], n1_ref[...], n2_ref[...])
        dyf, _, dxs, dz, ds, dd, dn1, dn2 = vjp((dm_ref[:, :c1], dm_ref[:, c1:]))
        dy_ref[...] = dyf
        dxs_ref[...] = dxs
        dz_ref[...] = dz
        ds_ref[...] = ds
        _acc_rows((dd_ref, dn1_ref, dn2_ref), (dd, dn1, dn2), pl.program_id(0) == 0)

    row = lambda i: (i, 0)
    fix = lambda i: (0, 0)
    return pl.pallas_call(
        body, name=name, grid=(n // tm,),
        in_specs=[pl.BlockSpec((None, tm, c1), lambda i: (0, i, 0)), pl.BlockSpec((None, tm, c1), lambda i: (1, i, 0)),
                  pl.BlockSpec((tm, c1), row), pl.BlockSpec((tm, c1), row), pl.BlockSpec((tm, c2), row),
                  pl.BlockSpec((LANES, 1), fix), pl.BlockSpec((1, c1), fix), pl.BlockSpec((1, c2), fix),
                  pl.BlockSpec((tm, c1 + c2), row)],
        out_specs=[pl.BlockSpec((tm, c1), row), pl.BlockSpec((tm, c1), row), pl.BlockSpec((tm, c1), row), pl.BlockSpec((tm, c2), row),
                   pl.BlockSpec((LANES, 1), fix), pl.BlockSpec((1, c1), fix), pl.BlockSpec((1, c2), fix)],
        out_shape=[SDS((n, c1), F32), SDS((n, c1), F32), SDS((n, c1), F32), SDS((n, c2), F32),
                   SDS((LANES, 1), F32), SDS((1, c1), F32), SDS((1, c2), F32)],
        compiler_params=_cp(("arbitrary",), VMEM_LIMIT),
    )(y2, y2, xs, z, s5o, dvec, nw_ssd, nw_s5, dmix)


def _final_loss(h2, nw, tgt, tm, name):
    n, d = h2.shape

    def loss_fn(h, w, t):
        e = _rms(h, w) - t
        return (0.5 / d) * jnp.sum(e * e)

    def body(h_ref, w_ref, t_ref, l_ref, dh_ref, dw_ref):
        loss, (dh, dw) = jax.value_and_grad(loss_fn, argnums=(0, 1))(h_ref[...], w_ref[...], t_ref[...])
        dh_ref[...] = dh
        _acc_rows((l_ref, dw_ref), (jnp.full((1, LANES), loss, F32), dw), pl.program_id(0) == 0)

    row = lambda i: (i, 0)
    fix = lambda i: (0, 0)
    return pl.pallas_call(
        body, name=name, grid=(n // tm,),
        in_specs=[pl.BlockSpec((tm, d), row), pl.BlockSpec((1, d), fix), pl.BlockSpec((tm, d), row)],
        out_specs=[pl.BlockSpec((1, LANES), fix), pl.BlockSpec((tm, d), row), pl.BlockSpec((1, d), fix)],
        out_shape=[SDS((1, LANES), F32), SDS((n, d), F32), SDS((1, d), F32)],
        compiler_params=_cp(("arbitrary",), VMEM_LIMIT),
    )(h2, nw, tgt)


def _norm_bwd(x, nw, dhn, dres, tm, name):
    n, d = x.shape

    def body(x_ref, w_ref, g_ref, r_ref, dx_ref, dw_ref):
        _, vjp = jax.vjp(_rms, x_ref[...], w_ref[...])
        dx, dw = vjp(g_ref[...])
        dx_ref[...] = r_ref[...] + dx
        _acc_rows((dw_ref,), (dw,), pl.program_id(0) == 0)

    row = lambda i: (i, 0)
    fix = lambda i: (0, 0)
    return pl.pallas_call(
        body, name=name, grid=(n // tm,),
        in_specs=[pl.BlockSpec((tm, d), row), pl.BlockSpec((1, d), fix), pl.BlockSpec((tm, d), row), pl.BlockSpec((tm, d), row)],
        out_specs=[pl.BlockSpec((tm, d), row), pl.BlockSpec((1, d), fix)],
        out_shape=[SDS((n, d), F32), SDS((1, d), F32)],
        compiler_params=_cp(("arbitrary",), VMEM_LIMIT),
    )(x, nw, dhn, dres)


def _row_tile(n, cap=512):
    for t in range(min(cap, n) // 8 * 8, 7, -8):
        if n % t == 0:
            return t
    return n


def _sum_lead(a, name):
    kk, n, c = a.shape
    tm = _row_tile(n)

    def body(a_ref, o_ref):
        acc = a_ref[0].astype(F32)
        for i in range(1, kk):
            acc = acc + a_ref[i].astype(F32)
        o_ref[...] = acc

    return pl.pallas_call(
        body, name=name, grid=(n // tm,),
        in_specs=[pl.BlockSpec((kk, tm, c), lambda i: (0, i, 0))],
        out_specs=pl.BlockSpec((tm, c), lambda i: (i, 0)), out_shape=SDS((n, c), F32),
        compiler_params=_cp(("arbitrary",), VMEM_LIMIT),
    )(a)


def _adamw(w, g, m, v, name):
    n, c = w.shape
    tm = _row_tile(n)

    def body(w_ref, g_ref, m_ref, v_ref, d_ref, nm_ref, nv_ref):
        gv = g_ref[...]
        mn = ADAM_B1 * m_ref[...] + (1.0 - ADAM_B1) * gv
        vn = ADAM_B2 * v_ref[...] + (1.0 - ADAM_B2) * jnp.square(gv)
        m_hat = mn / (1.0 - ADAM_B1 ** ADAM_STEP)
        v_hat = vn / (1.0 - ADAM_B2 ** ADAM_STEP)
        d_ref[...] = -ADAM_LR * (m_hat / (jnp.sqrt(v_hat) + ADAM_EPS) + ADAM_WD * w_ref[...])
        nm_ref[...] = mn
        nv_ref[...] = vn

    spec = pl.BlockSpec((tm, c), lambda i: (i, 0))
    return pl.pallas_call(
        body, name=name, grid=(n // tm,), in_specs=[spec] * 4, out_specs=[spec] * 3,
        out_shape=[SDS((n, c), F32)] * 3, compiler_params=_cp(("arbitrary",), VMEM_LIMIT),
    )(w, g, m, v)


ANY = pl.BlockSpec(memory_space=pl.ANY)


def _me():
    return lax.axis_index("x"), lax.axis_index("y"), lax.axis_index("c")


def _gather_xy(split, whole, name):
    ns, cnt = len(split), len(split) + len(whole)

    def body(*refs):
        src, dst = refs[:cnt], refs[cnt:2 * cnt]
        send, recv = refs[2 * cnt:]
        x, y, c = _me()
        mine = 2 * x + y
        chips = [(1 - x, y), (x, 1 - y), (1 - x, 1 - y)]

        def ici(a, j, slot):
            px, py = chips[j]
            if a < ns:
                s_ref, d_ref = src[a].at[c], dst[a].at[slot].at[c]
            else:
                s_ref, d_ref = src[a], dst[a].at[slot]
            return pltpu.make_async_remote_copy(s_ref, d_ref, send.at[3 * a + j], recv.at[3 * a + j],
                                                device_id=(px, py, c), device_id_type=MESH)

        def d2d(a, j, half):
            px, py = chips[j]
            ref = dst[a].at[2 * px + py].at[half]
            return pltpu.make_async_remote_copy(ref, ref, send.at[3 * cnt + 3 * a + j], recv.at[3 * cnt + 3 * a + j],
                                                device_id=(x, y, 1 - c), device_id_type=MESH)

        started = []
        for a in range(cnt):
            for j in range(3):
                cp = ici(a, j, mine)
                cp.start()
                started.append(cp)
        for a in range(cnt):
            for j, (px, py) in enumerate(chips):
                ici(a, j, 2 * px + py).wait_recv()
                if a < ns:
                    cp = d2d(a, j, c)
                    cp.start()
                    started.append(cp)
        for a in range(ns):
            for j in range(3):
                d2d(a, j, 1 - c).wait_recv()
        for cp in started:
            cp.wait_send()

    nsem = 3 * cnt + 3 * ns
    return pl.pallas_call(
        body, name=name, in_specs=[ANY] * cnt, out_specs=[ANY] * cnt,
        out_shape=[SDS((4,) + s.shape, s.dtype) for s in split + whole],
        scratch_shapes=[pltpu.SemaphoreType.DMA((nsem,)), pltpu.SemaphoreType.DMA((nsem,))],
    )(*split, *whole)


def _swap_sibling(parts, pick, name):
    cnt = len(parts)

    def body(*refs):
        src, dst = refs[:cnt], refs[cnt:2 * cnt]
        send, recv = refs[2 * cnt:]
        x, y, c = _me()
        cps = []
        for a in range(cnt):
            cp = pltpu.make_async_remote_copy(src[a].at[1 - c] if pick else src[a], dst[a], send.at[a], recv.at[a],
                                              device_id=(x, y, 1 - c), device_id_type=MESH)
            cp.start()
            cps.append(cp)
        for cp in cps:
            cp.wait()

    return pl.pallas_call(
        body, name=name, in_specs=[ANY] * cnt, out_specs=[ANY] * cnt,
        out_shape=[SDS(p.shape[1:] if pick else p.shape, p.dtype) for p in parts],
        scratch_shapes=[pltpu.SemaphoreType.DMA((cnt,)), pltpu.SemaphoreType.DMA((cnt,))],
    )(*parts)


def _scatter_xy(parts, name):
    cnt = len(parts)

    def body(*refs):
        src, dst = refs[:cnt], refs[cnt:2 * cnt]
        send, recv, loc = refs[2 * cnt:]
        x, y, c = _me()
        mine = 2 * x + y
        chips = [(1 - x, y), (x, 1 - y), (1 - x, 1 - y)]
        local = []
        for a in range(cnt):
            cp = pltpu.make_async_copy(src[a].at[mine], dst[a].at[mine], loc.at[a])
            cp.start()
            local.append(cp)
        sends = []
        for a in range(cnt):
            for j, (px, py) in enumerate(chips):
                cp = pltpu.make_async_remote_copy(src[a].at[2 * px + py], dst[a].at[mine], send.at[3 * a + j], recv.at[3 * a + j],
                                                  device_id=(px, py, c), device_id_type=MESH)
                cp.start()
                sends.append(cp)
        for a in range(cnt):
            for j, (px, py) in enumerate(chips):
                pltpu.make_async_remote_copy(src[a].at[mine], dst[a].at[2 * px + py], send.at[3 * a + j], recv.at[3 * a + j],
                                             device_id=(px, py, c), device_id_type=MESH).wait_recv()
        for cp in sends:
            cp.wait_send()
        for cp in local:
            cp.wait()

    return pl.pallas_call(
        body, name=name, in_specs=[ANY] * cnt, out_specs=[ANY] * cnt,
        out_shape=[SDS(p.shape, p.dtype) for p in parts],
        scratch_shapes=[pltpu.SemaphoreType.DMA((3 * cnt,)), pltpu.SemaphoreType.DMA((3 * cnt,)), pltpu.SemaphoreType.DMA((cnt,))],
    )(*parts)


def _bcast_all(buf, name):
    def body(src, dst, send, recv, loc):
        x, y, c = _me()
        mine = 4 * x + 2 * y + c
        own = pltpu.make_async_copy(src, dst.at[mine], loc)
        own.start()
        sends = []
        for k in range(1, 8):
            px, py, pc = x ^ (k >> 2), y ^ ((k >> 1) & 1), c ^ (k & 1)
            cp = pltpu.make_async_remote_copy(src, dst.at[mine], send.at[k - 1], recv.at[k - 1],
                                              device_id=(px, py, pc), device_id_type=MESH)
            cp.start()
            sends.append(cp)
        for k in range(1, 8):
            px, py, pc = x ^ (k >> 2), y ^ ((k >> 1) & 1), c ^ (k & 1)
            pltpu.make_async_remote_copy(src, dst.at[4 * px + 2 * py + pc], send.at[k - 1], recv.at[k - 1],
                                         device_id=(px, py, pc), device_id_type=MESH).wait_recv()
        for cp in sends:
            cp.wait_send()
        own.wait()

    return pl.pallas_call(
        body, name=name, in_specs=[ANY], out_specs=ANY, out_shape=SDS((8,) + buf.shape, buf.dtype),
        scratch_shapes=[pltpu.SemaphoreType.DMA((7,)), pltpu.SemaphoreType.DMA((7,)), pltpu.SemaphoreType.DMA(())],
    )(buf)


def _add2(a, b, dtype, name):
    shp = a.shape
    a2, b2 = a.reshape(-1, shp[-1]), b.reshape(-1, shp[-1])
    n, c = a2.shape
    tm = _row_tile(n, 256)

    def body(a_ref, b_ref, o_ref):
        o_ref[...] = (a_ref[...] + b_ref[...]).astype(dtype)

    spec = pl.BlockSpec((tm, c), lambda i: (i, 0))
    return pl.pallas_call(body, name=name, grid=(n // tm,), in_specs=[spec, spec], out_specs=spec,
                          out_shape=SDS((n, c), dtype), compiler_params=_cp(("arbitrary",), VMEM_LIMIT))(a2, b2).reshape(shp)


def _x_layout(u, bl):
    n, c = u.shape
    nck = n // bl // S5_T
    return u.reshape(bl, nck, S5_T, S5_GROUPS, S5_CH).transpose(3, 0, 1, 2, 4).reshape(S5_GROUPS, bl * nck, S5_T * S5_CH)


def _token_layout(xg, bl):
    ng, r, w = xg.shape
    nck = r // bl
    return xg.reshape(ng, bl, nck, S5_T, S5_CH).transpose(1, 2, 3, 0, 4).reshape(bl * nck * S5_T, ng * S5_CH)


def _pad_lanes(a, lanes=LANES):
    return jnp.pad(a, ((0, 0), (0, lanes - a.shape[1])))


def _local_step(x, tgt, p, bl):
    n, d = x.shape
    sw = SSD_HEADS * SSD_HEAD_DIM
    gn = SSD_GROUPS * SSD_STATE
    tm = min(n, 512)
    tm_ffn = min(n, 256)
    nck = n // bl // S5_T
    dff = p["w_down"].shape[0]
    s5w = S5_GROUPS * S5_CH

    w_in = p["w_in"]
    o1, o2, o3, o4 = sw, sw + sw, sw + sw + gn, sw + sw + 2 * gn
    w_z, w_xs, w_b, w_c = w_in[:, :o1], w_in[:, o1:o2], w_in[:, o2:o3], w_in[:, o3:o4]
    w_dt = _pad_lanes(w_in[:, o4:o4 + 2 * SSD_HEADS])
    w_u = w_in[:, o4 + 2 * SSD_HEADS:]
    in_ws = [w_z, w_xs, w_b, w_c, w_dt, w_u]
    cw, cb_ = p["ssd_conv_w"], p["ssd_conv_b"]
    conv_parts = [(cw[:, :sw], cb_[:, :sw]), (cw[:, sw:sw + gn], cb_[:, sw:sw + gn]), (cw[:, sw + gn:], cb_[:, sw + gn:])]
    zeros16 = jnp.zeros((1, SSD_HEADS), F32)

    def two_dirs(f, b):
        return jnp.stack([_pad_lanes(jnp.concatenate([f, zeros16], 1)), _pad_lanes(jnp.concatenate([zeros16, b], 1))])

    alog2 = two_dirs(p["ssd_a_log_fwd"], p["ssd_a_log_bwd"])
    dtb2 = two_dirs(p["ssd_dt_bias_fwd"], p["ssd_dt_bias_bwd"])
    dvec = _pad_lanes(p["ssd_d"]).reshape(LANES, 1)

    hn, z, xs_pre, b_pre, c_pre, dtr, u = _norm_matmul(x, p["norm_mix_w"], in_ws, tm, "in_proj")
    pres = [xs_pre, b_pre, c_pre]
    acts = [_conv_silu(pre, w, b, bl, min(256, pre.shape[1]), f"ssd_conv_{i}") for i, (pre, (w, b)) in enumerate(zip(pres, conv_parts))]
    xs_a, b_a, c_a = acts
    y2, saved = _ssd_scan(xs_a, b_a, c_a, dtr, alog2, dtb2, bl, "ssd_scan")

    def col(a):
        return a.reshape(a.shape + (1,))

    s5_params = [
        col(p["s5_lambda_re_fwd"]), col(p["s5_lambda_im_fwd"]), p["s5_log_step_fwd"].reshape(S5_GROUPS, 1, 1), p["s5_c_re_fwd"], p["s5_c_im_fwd"],
        col(p["s5_lambda_re_bwd"]), col(p["s5_lambda_im_bwd"]), p["s5_log_step_bwd"].reshape(S5_GROUPS, 1, 1), p["s5_c_re_bwd"], p["s5_c_im_bwd"],
        p["s5_b_re"], p["s5_b_im"], col(p["s5_d"].reshape(S5_GROUPS, S5_CH)),
        p["s5_glu_w"][:, :, :S5_CH], p["s5_glu_w"][:, :, S5_CH:], col(p["s5_glu_b"][:, :S5_CH]), col(p["s5_glu_b"][:, S5_CH:]),
    ]
    s5_args = [_x_layout(u, bl)] + s5_params
    s5o = _token_layout(_s5_fwd(s5_args, nck, "s5_fwd"), bl)
    ymix = _mix(y2, xs_a, z, s5o, dvec, p["ssd_norm_w"], p["s5_norm_w"], tm, "mix")
    h1 = _matmul_res(ymix, p["w_out"], x, tm, "out_proj")
    w_up = p["w_up"]
    hn2, up_v, up_g = _norm_matmul(h1, p["norm_ffn_w"], [w_up[:, :dff], w_up[:, dff:]], tm_ffn, "ffn_up")
    fw, fb = p["ffn_conv_w"], p["ffn_conv_b"]
    act = _conv_glu(up_v, up_g, fw[:, :dff], fw[:, dff:], fb[:, :dff], fb[:, dff:], bl, 256, "ffn_conv")
    h2 = _matmul_res(act, p["w_down"], h1, tm, "ffn_down")
    loss, dh2, g_nfw = _final_loss(h2, p["norm_final_w"].reshape(1, d), tgt, tm, "final_loss")

    g = {"norm_final_w": g_nfw.reshape(d)}
    g["w_down"] = _matmul_tn(act, dh2, tm, d, "ffn_down_dw")
    dact = _matmul_nt([dh2], [p["w_down"]], tm, "ffn_down_dx")
    dup_v, dup_g, dwv, dwg, dbv, dbg = _conv_glu_bwd(up_v, up_g, fw[:, :dff], fw[:, dff:], fb[:, :dff], fb[:, dff:], dact, bl, 256, "ffn_conv_bwd")
    g["ffn_conv_w"] = jnp.concatenate([dwv, dwg], 1)
    g["ffn_conv_b"] = jnp.concatenate([dbv, dbg], 1)
    g["w_up"] = jnp.concatenate([_matmul_tn(hn2, dup_v, tm, dff // 2, "ffn_up_dw_v"), _matmul_tn(hn2, dup_g, tm, dff // 2, "ffn_up_dw_g")], 1)
    dhn2 = _matmul_nt([dup_v, dup_g], [w_up[:, :dff], w_up[:, dff:]], tm_ffn, "ffn_up_dx")
    dh1, g["norm_ffn_w"] = _norm_bwd(h1, p["norm_ffn_w"], dhn2, dh2, tm, "ffn_norm_bwd")
    g["w_out"] = _matmul_tn(ymix, dh1, tm, d, "out_proj_dw")
    dmix = _matmul_nt([dh1], [p["w_out"]], tm, "out_proj_dx")
    dyssd, dxs_gate, dz, ds5o, g_d, g["ssd_norm_w"], g["s5_norm_w"] = _mix_bwd(
        y2, xs_a, z, s5o, dvec, p["ssd_norm_w"], p["s5_norm_w"], dmix, tm, "mix_bwd")
    g["ssd_d"] = g_d[:SSD_HEADS].reshape(1, SSD_HEADS)
    s5g = _s5_bwd(s5_args, _x_layout(ds5o, bl), nck, "s5_bwd")
    du = _token_layout(s5g[0], bl)
    (g["s5_lambda_re_fwd"], g["s5_lambda_im_fwd"], g["s5_log_step_fwd"], g["s5_c_re_fwd"], g["s5_c_im_fwd"],
     g["s5_lambda_re_bwd"], g["s5_lambda_im_bwd"], g["s5_log_step_bwd"], g["s5_c_re_bwd"], g["s5_c_im_bwd"],
     g["s5_b_re"], g["s5_b_im"], g_s5d, g_wv, g_wg, g_bv, g_bg) = s5g[1:]
    for k_ in ("s5_lambda_re_fwd", "s5_lambda_im_fwd", "s5_lambda_re_bwd", "s5_lambda_im_bwd"):
        g[k_] = g[k_].reshape(S5_GROUPS, S5_STATE)
    for k_ in ("s5_log_step_fwd", "s5_log_step_bwd"):
        g[k_] = g[k_].reshape(S5_GROUPS)
    g["s5_d"] = g_s5d.reshape(1, s5w)
    g["s5_glu_w"] = jnp.concatenate([g_wv, g_wg], 2)
    g["s5_glu_b"] = jnp.concatenate([g_bv.reshape(S5_GROUPS, S5_CH), g_bg.reshape(S5_GROUPS, S5_CH)], 1)
    dxs2, dbm2, dcm2, ddt2, dal2, ddb2 = _ssd_scan_bwd(xs_a, b_a, c_a, dtr, alog2, dtb2, saved, dyssd, bl, "ssd_scan_bwd")
    g["ssd_a_log_fwd"], g["ssd_a_log_bwd"] = dal2[0, :, :SSD_HEADS], dal2[1, :, SSD_HEADS:2 * SSD_HEADS]
    g["ssd_dt_bias_fwd"], g["ssd_dt_bias_bwd"] = ddb2[0, :, :SSD_HEADS], ddb2[1, :, SSD_HEADS:2 * SSD_HEADS]
    cots = [[(dxs2, 0), (dxs2, 1), (dxs_gate, None)], [(dbm2, 0), (dbm2, 1)], [(dcm2, 0), (dcm2, 1)]]
    dpres, dcw, dcb = [], [], []
    for i, (pre, (w, b), cot) in enumerate(zip(pres, conv_parts, cots)):
        dp, dw_, db_ = _conv_silu_bwd(pre, w, b, cot, bl, min(256, pre.shape[1]), f"ssd_conv_bwd_{i}")
        dpres.append(dp)
        dcw.append(dw_)
        dcb.append(db_)
    g["ssd_conv_w"] = jnp.concatenate(dcw, 1)
    g["ssd_conv_b"] = jnp.concatenate(dcb, 1)
    ddtr = _sum_lead(ddt2, "ssd_ddt_sum")
    dprojs = [dz, dpres[0], dpres[1], dpres[2], ddtr, du]
    dws = [_matmul_tn(hn, dpj, tm, dpj.shape[1], f"in_proj_dw_{i}") for i, dpj in enumerate(dprojs)]
    dws[4] = dws[4][:, :2 * SSD_HEADS]
    g["w_in"] = jnp.concatenate(dws, 1)
    dhn = _matmul_nt(dprojs, in_ws, tm, "in_proj_dx")
    grad_x, g["norm_mix_w"] = _norm_bwd(x, p["norm_mix_w"], dhn, dh1, tm, "mix_norm_bwd")
    return loss, grad_x, g


_WEIGHTS = ['norm_mix_w', 'w_in', 'ssd_conv_w', 'ssd_conv_b', 'ssd_dt_bias_fwd', 'ssd_dt_bias_bwd', 'ssd_a_log_fwd', 'ssd_a_log_bwd',
            'ssd_d', 'ssd_norm_w', 's5_lambda_re_fwd', 's5_lambda_im_fwd', 's5_log_step_fwd', 's5_lambda_re_bwd', 's5_lambda_im_bwd',
            's5_log_step_bwd', 's5_b_re', 's5_b_im', 's5_c_re_fwd', 's5_c_im_fwd', 's5_c_re_bwd', 's5_c_im_bwd', 's5_d', 's5_glu_w',
            's5_glu_b', 's5_norm_w', 'w_out', 'norm_ffn_w', 'ffn_w_up', 'ffn_conv_w', 'ffn_conv_b', 'ffn_w_down', 'norm_final_w']
_BIG = ('w_in', 'w_out', 'ffn_w_up', 'ffn_w_down')
_CONV = ('ssd_conv_w', 'ffn_conv_w')


def _pack(arrs):
    flat = jnp.concatenate([a.reshape(-1) for a in arrs])
    rows = -(-flat.shape[0] // (64 * LANES)) * 64
    return jnp.pad(flat, (0, rows * LANES - flat.shape[0])).reshape(rows, LANES)


def _unpack(buf, shapes):
    flat = buf.reshape(-1)
    out, off = [], 0
    for shp in shapes:
        size = math.prod(shp)
        out.append(flat[off:off + size].reshape(shp))
        off += size
    return out


def kernel(x, norm_mix_w, w_in, ssd_conv_w, ssd_conv_b, ssd_dt_bias_fwd, ssd_dt_bias_bwd, ssd_a_log_fwd, ssd_a_log_bwd, ssd_d, ssd_norm_w, s5_lambda_re_fwd, s5_lambda_im_fwd, s5_log_step_fwd, s5_lambda_re_bwd, s5_lambda_im_bwd, s5_log_step_bwd, s5_b_re, s5_b_im, s5_c_re_fwd, s5_c_im_fwd, s5_c_re_bwd, s5_c_im_bwd, s5_d, s5_glu_w, s5_glu_b, s5_norm_w, w_out, norm_ffn_w, ffn_w_up, ffn_conv_w, ffn_conv_b, ffn_w_down, norm_final_w, loss_target, m_norm_mix_w, m_w_in, m_ssd_conv_w, m_ssd_conv_b, m_ssd_dt_bias_fwd, m_ssd_dt_bias_bwd, m_ssd_a_log_fwd, m_ssd_a_log_bwd, m_ssd_d, m_ssd_norm_w, m_s5_lambda_re_fwd, m_s5_lambda_im_fwd, m_s5_log_step_fwd, m_s5_lambda_re_bwd, m_s5_lambda_im_bwd, m_s5_log_step_bwd, m_s5_b_re, m_s5_b_im, m_s5_c_re_fwd, m_s5_c_im_fwd, m_s5_c_re_bwd, m_s5_c_im_bwd, m_s5_d, m_s5_glu_w, m_s5_glu_b, m_s5_norm_w, m_w_out, m_norm_ffn_w, m_ffn_w_up, m_ffn_conv_w, m_ffn_conv_b, m_ffn_w_down, m_norm_final_w, v_norm_mix_w, v_w_in, v_ssd_conv_w, v_ssd_conv_b, v_ssd_dt_bias_fwd, v_ssd_dt_bias_bwd, v_ssd_a_log_fwd, v_ssd_a_log_bwd, v_ssd_d, v_ssd_norm_w, v_s5_lambda_re_fwd, v_s5_lambda_im_fwd, v_s5_log_step_fwd, v_s5_lambda_re_bwd, v_s5_lambda_im_bwd, v_s5_log_step_bwd, v_s5_b_re, v_s5_b_im, v_s5_c_re_fwd, v_s5_c_im_fwd, v_s5_c_re_bwd, v_s5_c_im_bwd, v_s5_d, v_s5_glu_w, v_s5_glu_b, v_s5_norm_w, v_w_out, v_norm_ffn_w, v_ffn_w_up, v_ffn_conv_w, v_ffn_conv_b, v_ffn_w_down, v_norm_final_w):
    args = dict(locals())
    w = {k_: args[k_] for k_ in _WEIGHTS}
    m = {k_: args["m_" + k_] for k_ in _WEIGHTS}
    v = {k_: args["v_" + k_] for k_ in _WEIGHTS}
    bl, sl, d = x.shape
    chip = 2 * lax.axis_index("x") + lax.axis_index("y")
    core = lax.axis_index("c")

    shards = [w[k_][0].astype(BF16) for k_ in _BIG] + [w[k_][0] for k_ in _CONV]
    split = [s.reshape(2, s.shape[0] // 2, s.shape[1]) for s in shards[:len(_BIG)]]
    gathered = _gather_xy(split, shards[len(_BIG):], "gather_weights")
    g_in, g_out, g_up, g_down, g_scw, g_fcw = [
        lax.dynamic_update_index_in_dim(got.reshape((4,) + s.shape), s, chip, 0) for got, s in zip(gathered, shards)]

    def cols(a):
        return jnp.moveaxis(a, 0, 1).reshape(a.shape[1], 4 * a.shape[2])

    p = {k_: (w[k_][0] if w[k_].ndim >= 3 else w[k_]) for k_ in _WEIGHTS if k_ not in _BIG + _CONV}
    p["w_in"], p["w_up"] = cols(g_in), cols(g_up)
    p["w_out"], p["w_down"] = g_out.reshape(-1, g_out.shape[2]), g_down.reshape(-1, g_down.shape[2])
    p["ssd_conv_w"], p["ffn_conv_w"] = cols(g_scw), cols(g_fcw)

    loss, grad_x, g = _local_step(x.reshape(bl * sl, d), loss_target.reshape(bl * sl, d), p, bl)
    g["ffn_w_up"], g["ffn_w_down"] = g.pop("w_up"), g.pop("w_down")

    def owner_major(a, k_):
        r, c = w[k_].shape[1:]
        if a.shape[0] == r:
            a = jnp.moveaxis(a.reshape(r, 4, c), 1, 0)
        else:
            a = a.reshape(4, r, c)
        return a.reshape(4, 2, r // 2, c)

    small = [k_ for k_ in _WEIGHTS if k_ not in _BIG]
    small_full_shapes = [g[k_].shape for k_ in small]
    buf = _pack([g[k_] for k_ in small] + [loss[0, :1]])
    parts = [jnp.moveaxis(owner_major(g[k_], k_), 1, 0) for k_ in _BIG]
    parts.append(jnp.moveaxis(buf.reshape(4, 2, -1, LANES), 1, 0))
    got = _swap_sibling(parts, True, "reduce_sibling")
    mine = [lax.dynamic_index_in_dim(pt, core, 0, keepdims=False) for pt in parts]
    chip_sums = [_add2(a, b, BF16 if i < len(_BIG) else F32, f"reduce_add_{i}") for i, (a, b) in enumerate(zip(mine, got))]
    from_chips = _scatter_xy(chip_sums, "reduce_chips")
    halves = [_sum_lead(a.reshape(4, -1, a.shape[-1]), f"reduce_sum_{i}") for i, a in enumerate(from_chips)]
    other = _swap_sibling(halves[:-1], False, "reduce_join")
    big_grad = {}
    for k_, own_half, sib_half in zip(_BIG, halves, other):
        south = core == 0
        full = jnp.stack([jnp.where(south, own_half, sib_half), jnp.where(south, sib_half, own_half)])
        big_grad[k_] = full.reshape((1,) + w[k_].shape[1:])
    tot = _bcast_all(halves[-1], "reduce_small").reshape(buf.shape)
    unp = _unpack(tot, small_full_shapes + [(1,)])
    small_grad = dict(zip(small, unp[:-1]))
    loss_out = unp[-1].reshape(())
    for k_ in _CONV:
        cshard = w[k_].shape[2]
        small_grad[k_] = lax.dynamic_slice_in_dim(small_grad[k_], chip * cshard, cshard, 1)

    grads, deltas, new_m, new_v = {}, {}, {}, {}
    for k_ in _BIG:
        shp = w[k_].shape
        grads[k_] = big_grad[k_]
        dl, nm, nv = _adamw(w[k_][0], big_grad[k_][0], m[k_][0], v[k_][0], f"adamw_{k_}")
        deltas[k_], new_m[k_], new_v[k_] = dl.reshape(shp), nm.reshape(shp), nv.reshape(shp)
    sw_ = _pack([w[k_] for k_ in small])
    sg_ = _pack([small_grad[k_] for k_ in small])
    sm_ = _pack([m[k_] for k_ in small])
    sv_ = _pack([v[k_] for k_ in small])
    dl, nm, nv = _adamw(sw_, sg_, sm_, sv_, "adamw_small")
    shapes = [w[k_].shape for k_ in small]
    for k_, a, b, c_ in zip(small, _unpack(dl, shapes), _unpack(nm, shapes), _unpack(nv, shapes)):
        deltas[k_], new_m[k_], new_v[k_] = a, b, c_
        grads[k_] = small_grad[k_].reshape(w[k_].shape)
    return (loss_out, grad_x.reshape(bl, sl, d), *[grads[k_] for k_ in _WEIGHTS], *[deltas[k_] for k_ in _WEIGHTS],
            *[new_m[k_] for k_ in _WEIGHTS], *[new_v[k_] for k_ in _WEIGHTS])
```

```python
import functools
import math

import jax
import jax.numpy as jnp
from jax import lax
from jax.experimental import pallas as pl
from jax.experimental.pallas import tpu as pltpu

F32 = jnp.float32
BF16 = jnp.bfloat16
HI = lax.Precision.HIGHEST
SDS = jax.ShapeDtypeStruct
MESH = pl.DeviceIdType.MESH

NN = (((1,), (0,)), ((), ()))
NT = (((1,), (1,)), ((), ()))
TN = (((0,), (0,)), ((), ()))

EPS = 1e-6
SSD_HEADS = 16
SSD_HEAD_DIM = 64
SSD_GROUPS = 4
SSD_STATE = 128
SSD_CHUNK = 128
SSD_CONV = 5
S5_GROUPS = 32
S5_CH = 16
S5_STATE = 64
S5_T = 16
LANES = 128
ADAM_LR, ADAM_B1, ADAM_B2, ADAM_EPS, ADAM_WD, ADAM_STEP = 0.001, 0.9, 0.999, 1e-08, 0.01, 10
V7X_VMEM_BYTES = 64 * 1024 * 1024
VMEM_LIMIT = V7X_VMEM_BYTES - 8 * 1024 * 1024


def _cp(sem, vmem=None):
    return pltpu.CompilerParams(dimension_semantics=sem, vmem_limit_bytes=vmem)


def _dot(a, b, dims=NN, precision=None):
    return lax.dot_general(a, b, dims, precision=precision, preferred_element_type=F32)


def _rms(x, w):
    return x * lax.rsqrt(jnp.mean(x * x, axis=-1, keepdims=True) + EPS) * w


def _sigmoid(x):
    return 1.0 / (1.0 + jnp.exp(-x))


def _softplus(x):
    return jnp.maximum(x, 0.0) + jnp.log1p(jnp.exp(-jnp.abs(x)))


@functools.partial(jax.custom_vjp, nondiff_argnums=(1, 2))
def _shift(x, k, seg):
    n = x.shape[0]
    r = lax.broadcasted_iota(jnp.int32, x.shape, 0) % seg
    y = pltpu.roll(x, k % n, 0)
    ok = (r >= k) if k > 0 else (r < seg + k)
    return jnp.where(ok, y, 0.0)


def _shift_fwd(x, k, seg):
    return _shift(x, k, seg), None


def _shift_bwd(k, seg, _, g):
    return (_shift(g, -k, seg),)


_shift.defvjp(_shift_fwd, _shift_bwd)


@functools.partial(jax.custom_vjp, nondiff_argnums=(1,))
def _lane_shift(x, k):
    if k == 0:
        return x
    n = x.shape[1]
    lane = lax.broadcasted_iota(jnp.int32, x.shape, 1)
    ok = (lane >= k) if k > 0 else (lane < n + k)
    return jnp.where(ok, pltpu.roll(x, k % n, 1), 0.0)


_lane_shift.defvjp(lambda x, k: (_lane_shift(x, k), None), lambda k, _, g: (_lane_shift(g, -k),))


@jax.custom_vjp
def _swap(z):
    return pltpu.roll(z, LANES // 2, 1)


_swap.defvjp(lambda z: (_swap(z), None), lambda _, g: (_swap(g),))


def _norm_matmul(x, nw, ws, tm, name):
    n, d = x.shape
    k = len(ws)

    def body(x_ref, nw_ref, *refs):
        hn = _rms(x_ref[...], nw_ref[...]).astype(BF16)
        refs[k][...] = hn
        for w_ref, o_ref in zip(refs[:k], refs[k + 1:]):
            o_ref[...] = _dot(hn, w_ref[...])

    row = lambda i: (i, 0)
    fix = lambda i: (0, 0)
    return pl.pallas_call(
        body, name=name, grid=(n // tm,),
        in_specs=[pl.BlockSpec((tm, d), row), pl.BlockSpec((1, d), fix)] + [pl.BlockSpec(w.shape, fix) for w in ws],
        out_specs=[pl.BlockSpec((tm, d), row)] + [pl.BlockSpec((tm, w.shape[1]), row) for w in ws],
        out_shape=[SDS((n, d), BF16)] + [SDS((n, w.shape[1]), F32) for w in ws],
        compiler_params=_cp(("arbitrary",), VMEM_LIMIT),
    )(x, nw, *ws)


def _matmul_res(a, w, res, tm, name):
    n, kd = a.shape
    m = w.shape[1]

    def body(a_ref, w_ref, r_ref, o_ref):
        o_ref[...] = r_ref[...] + _dot(a_ref[...], w_ref[...])

    return pl.pallas_call(
        body, name=name, grid=(n // tm,),
        in_specs=[pl.BlockSpec((tm, kd), lambda i: (i, 0)), pl.BlockSpec((kd, m), lambda i: (0, 0)),
                  pl.BlockSpec((tm, m), lambda i: (i, 0))],
        out_specs=pl.BlockSpec((tm, m), lambda i: (i, 0)),
        out_shape=SDS((n, m), F32),
        compiler_params=_cp(("arbitrary",), VMEM_LIMIT),
    )(a, w, res)


def _matmul_nt(gs, ws, tm, name):
    n = gs[0].shape[0]
    kd = ws[0].shape[0]
    cnt = len(gs)

    def body(*refs):
        acc = None
        for g_ref, w_ref in zip(refs[:cnt], refs[cnt:2 * cnt]):
            t = _dot(g_ref[...].astype(BF16), w_ref[...], NT)
            acc = t if acc is None else acc + t
        refs[2 * cnt][...] = acc

    return pl.pallas_call(
        body, name=name, grid=(n // tm,),
        in_specs=[pl.BlockSpec((tm, g.shape[1]), lambda i: (i, 0)) for g in gs]
        + [pl.BlockSpec(w.shape, lambda i: (0, 0)) for w in ws],
        out_specs=pl.BlockSpec((tm, kd), lambda i: (i, 0)),
        out_shape=SDS((n, kd), F32),
        compiler_params=_cp(("arbitrary",), VMEM_LIMIT),
    )(*gs, *ws)


def _matmul_tn(a, g, tm, cb, name):
    n, kd = a.shape
    m = g.shape[1]

    def body(a_ref, g_ref, o_ref):
        t = _dot(a_ref[...], g_ref[...].astype(BF16), TN)

        @pl.when(pl.program_id(1) == 0)
        def _():
            o_ref[...] = t

        @pl.when(pl.program_id(1) != 0)
        def _():
            o_ref[...] += t

    return pl.pallas_call(
        body, name=name, grid=(m // cb, n // tm),
        in_specs=[pl.BlockSpec((tm, kd), lambda j, i: (i, 0)), pl.BlockSpec((tm, cb), lambda j, i: (i, j))],
        out_specs=pl.BlockSpec((kd, cb), lambda j, i: (0, j)),
        out_shape=SDS((kd, m), F32),
        compiler_params=_cp(("arbitrary", "arbitrary"), VMEM_LIMIT),
    )(a, g)


def _dwconv(x, w, b):
    kw = w.shape[0]
    acc = b
    for k in range(kw):
        acc = acc + w[k:k + 1, :] * _shift(x, kw // 2 - k, x.shape[0])
    return acc


def _conv_silu_fn(x, w, b):
    y = _dwconv(x, w, b)
    return y * _sigmoid(y)


def _conv_glu_fn(v, g, wv, wg, bv, bg):
    cv = _dwconv(v, wv, bv)
    cg = _dwconv(g, wg, bg)
    return cg * _sigmoid(cg) * cv


def _conv_silu(x, w, b, bl, cb, name):
    n, c = x.shape
    sl = n // bl
    kw = w.shape[0]

    def body(x_ref, w_ref, b_ref, o_ref):
        o_ref[...] = _conv_silu_fn(x_ref[...], w_ref[...], b_ref[...])

    return pl.pallas_call(
        body, name=name, grid=(bl, c // cb),
        in_specs=[pl.BlockSpec((sl, cb), lambda s, j: (s, j)), pl.BlockSpec((kw, cb), lambda s, j: (0, j)),
                  pl.BlockSpec((1, cb), lambda s, j: (0, j))],
        out_specs=pl.BlockSpec((sl, cb), lambda s, j: (s, j)),
        out_shape=SDS((n, c), F32),
        compiler_params=_cp(("arbitrary", "arbitrary"), VMEM_LIMIT),
    )(x, w, b)


def _conv_silu_bwd(x, w, b, dys, bl, cb, name):
    n, c = x.shape
    sl = n // bl
    kw = w.shape[0]
    cnt = len(dys)

    def body(x_ref, w_ref, b_ref, *refs):
        dy = refs[0][...]
        for r in refs[1:cnt]:
            dy = dy + r[...]
        dx_ref, dw_ref, db_ref = refs[cnt:]
        _, vjp = jax.vjp(_conv_silu_fn, x_ref[...], w_ref[...], b_ref[...])
        dx, dw, db = vjp(dy)
        dx_ref[...] = dx.astype(BF16)

        @pl.when(pl.program_id(1) == 0)
        def _():
            dw_ref[...] = dw
            db_ref[...] = db

        @pl.when(pl.program_id(1) != 0)
        def _():
            dw_ref[...] += dw
            db_ref[...] += db

    dy_specs = []
    for arr, lead in dys:
        if lead is None:
            dy_specs.append(pl.BlockSpec((sl, cb), lambda j, s: (s, j)))
        else:
            dy_specs.append(pl.BlockSpec((None, sl, cb), functools.partial(lambda j, s, lead: (lead, s, j), lead=lead)))
    return pl.pallas_call(
        body, name=name, grid=(c // cb, bl),
        in_specs=[pl.BlockSpec((sl, cb), lambda j, s: (s, j)), pl.BlockSpec((kw, cb), lambda j, s: (0, j)),
                  pl.BlockSpec((1, cb), lambda j, s: (0, j))] + dy_specs,
        out_specs=[pl.BlockSpec((sl, cb), lambda j, s: (s, j)), pl.BlockSpec((kw, cb), lambda j, s: (0, j)),
                   pl.BlockSpec((1, cb), lambda j, s: (0, j))],
        out_shape=[SDS((n, c), BF16), SDS((kw, c), F32), SDS((1, c), F32)],
        compiler_params=_cp(("arbitrary", "arbitrary"), VMEM_LIMIT),
    )(x, w, b, *[a for a, _ in dys])


def _conv_glu(v, g, wv, wg, bv, bg, bl, cb, name):
    n, c = v.shape
    sl = n // bl
    kw = wv.shape[0]

    def body(v_ref, g_ref, wv_ref, wg_ref, bv_ref, bg_ref, o_ref):
        o_ref[...] = _conv_glu_fn(v_ref[...], g_ref[...], wv_ref[...], wg_ref[...], bv_ref[...], bg_ref[...]).astype(BF16)

    big = pl.BlockSpec((sl, cb), lambda s, j: (s, j))
    wsp = pl.BlockSpec((kw, cb), lambda s, j: (0, j))
    bsp = pl.BlockSpec((1, cb), lambda s, j: (0, j))
    return pl.pallas_call(
        body, name=name, grid=(bl, c // cb),
        in_specs=[big, big, wsp, wsp, bsp, bsp], out_specs=big, out_shape=SDS((n, c), BF16),
        compiler_params=_cp(("arbitrary", "arbitrary"), VMEM_LIMIT),
    )(v, g, wv, wg, bv, bg)


def _conv_glu_bwd(v, g, wv, wg, bv, bg, dact, bl, cb, name):
    n, c = v.shape
    sl = n // bl
    kw = wv.shape[0]

    def body(v_ref, g_ref, wv_ref, wg_ref, bv_ref, bg_ref, da_ref, dv_ref, dg_ref, dwv_ref, dwg_ref, dbv_ref, dbg_ref):
        _, vjp = jax.vjp(_conv_glu_fn, v_ref[...], g_ref[...], wv_ref[...], wg_ref[...], bv_ref[...], bg_ref[...])
        dv, dg, dwv, dwg, dbv, dbg = vjp(da_ref[...])
        dv_ref[...] = dv.astype(BF16)
        dg_ref[...] = dg.astype(BF16)

        @pl.when(pl.program_id(1) == 0)
        def _():
            dwv_ref[...] = dwv
            dwg_ref[...] = dwg
            dbv_ref[...] = dbv
            dbg_ref[...] = dbg

        @pl.when(pl.program_id(1) != 0)
        def _():
            dwv_ref[...] += dwv
            dwg_ref[...] += dwg
            dbv_ref[...] += dbv
            dbg_ref[...] += dbg

    big = pl.BlockSpec((sl, cb), lambda j, s: (s, j))
    wsp = pl.BlockSpec((kw, cb), lambda j, s: (0, j))
    bsp = pl.BlockSpec((1, cb), lambda j, s: (0, j))
    return pl.pallas_call(
        body, name=name, grid=(c // cb, bl),
        in_specs=[big, big, wsp, wsp, bsp, bsp, big],
        out_specs=[big, big, wsp, wsp, bsp, bsp],
        out_shape=[SDS((n, c), BF16), SDS((n, c), BF16), SDS((kw, c), F32), SDS((kw, c), F32), SDS((1, c), F32), SDS((1, c), F32)],
        compiler_params=_cp(("arbitrary", "arbitrary"), VMEM_LIMIT),
    )(v, g, wv, wg, bv, bg, dact)


_DIMS_T = {NN: (NT, TN, False, False), NT: (NN, TN, False, True), TN: (NT, NN, True, False)}


@functools.partial(jax.custom_vjp, nondiff_argnums=(2,))
def _bdot(a, b, dims):
    return _dot(a.astype(BF16), b.astype(BF16), dims)


def _bdot_fwd(a, b, dims):
    return _bdot(a, b, dims), (a, b)


def _bdot_bwd(dims, res, g):
    a, b = res
    da_dims, db_dims, a_swapped, b_swapped = _DIMS_T[dims]
    da = _bdot(b, g, da_dims) if a_swapped else _bdot(g, b, da_dims)
    db = _bdot(g, a, db_dims) if b_swapped else _bdot(a, g, db_dims)
    return da, db


_bdot.defvjp(_bdot_fwd, _bdot_bwd)


def _ssd_chunk_fn(rev, xs, dtr, bms, cms, st, alog, dtb):
    q = dtr.shape[0]
    r = lax.broadcasted_iota(jnp.int32, (q, q), 0)
    c = lax.broadcasted_iota(jnp.int32, (q, q), 1)
    sgn = 1 - 2 * rev
    mask = (r - c) * sgn >= 0
    tri = mask.astype(F32)
    tri_t = ((c - r) * sgn >= 0).astype(F32)
    dt = _softplus(dtr + dtb)
    dta = dt * (-jnp.exp(alog))
    cs = _dot(tri, dta, NN, HI)
    cs_t = _dot(dta, tri_t, TN, HI)
    tot = jnp.sum(dta, axis=0, keepdims=True)
    ys, outs = [], []
    for h in range(SSD_HEADS):
        g = h // (SSD_HEADS // SSD_GROUPS)
        bg, cg = bms[g], cms[g]
        if h % (SSD_HEADS // SSD_GROUPS) == 0:
            scores = _bdot(cg, bg, NT)
        col = cs[:, h:h + 1]
        row = cs_t[h:h + 1, :]
        dth = dt[:, h:h + 1]
        toth = tot[:, h:h + 1]
        seg = jnp.where(mask, jnp.exp(jnp.where(mask, col - row, 0.0)), 0.0)
        xdt = xs[h] * dth
        y = _bdot(scores * seg, xdt, NN) + jnp.exp(col) * _bdot(cg, st[h], NT)
        new = _bdot(xdt * jnp.exp(toth - col), bg, TN)
        ys.append(y)
        outs.append(jnp.exp(toth) * st[h] + new)
    return ys, outs


def _ssd_scan(xs, bm, cm, dtr, alog2, dtb2, bl, name):
    n = xs.shape[0]
    q = SSD_CHUNK
    nc = n // bl // q
    hd, ns = SSD_HEAD_DIM, SSD_STATE

    def body(xs_ref, b_ref, c_ref, dt_ref, al_ref, db_ref, y_ref, sv_ref, st_ref):
        d, i = pl.program_id(0), pl.program_id(2)

        @pl.when(i == 0)
        def _():
            st_ref[...] = jnp.zeros(st_ref.shape, F32)

        st = [st_ref[h] for h in range(SSD_HEADS)]
        sv_ref[...] = st_ref[...]
        xl = [xs_ref[:, hd * h:hd * (h + 1)] for h in range(SSD_HEADS)]
        bms = [b_ref[:, ns * g:ns * (g + 1)] for g in range(SSD_GROUPS)]
        cms = [c_ref[:, ns * g:ns * (g + 1)] for g in range(SSD_GROUPS)]
        ys, outs = _ssd_chunk_fn(d, xl, dt_ref[...], bms, cms, st, al_ref[...], db_ref[...])
        for h in range(SSD_HEADS):
            st_ref[h] = outs[h]
            y_ref[:, hd * h:hd * (h + 1)] = ys[h]

    def rowblk(d, s, i):
        return s * nc + i + d * (nc - 1 - 2 * i)

    return pl.pallas_call(
        body, name=name, grid=(2, bl, nc),
        in_specs=[pl.BlockSpec((q, SSD_HEADS * hd), lambda d, s, i: (rowblk(d, s, i), 0)),
                  pl.BlockSpec((q, SSD_GROUPS * ns), lambda d, s, i: (rowblk(d, s, i), 0)),
                  pl.BlockSpec((q, SSD_GROUPS * ns), lambda d, s, i: (rowblk(d, s, i), 0)),
                  pl.BlockSpec((None, q, LANES), lambda d, s, i: (d, rowblk(d, s, i), 0)),
                  pl.BlockSpec((None, 1, LANES), lambda d, s, i: (d, 0, 0)),
                  pl.BlockSpec((None, 1, LANES), lambda d, s, i: (d, 0, 0))],
        out_specs=[pl.BlockSpec((None, q, SSD_HEADS * hd), lambda d, s, i: (d, rowblk(d, s, i), 0)),
                   pl.BlockSpec((None, None, SSD_HEADS, hd, ns), lambda d, s, i: (d, rowblk(d, s, i), 0, 0, 0))],
        out_shape=[SDS((2, n, SSD_HEADS * hd), F32), SDS((2, n // q, SSD_HEADS, hd, ns), F32)],
        scratch_shapes=[pltpu.VMEM((SSD_HEADS, hd, ns), F32)],
        compiler_params=_cp(("arbitrary",) * 3, VMEM_LIMIT),
    )(xs, bm, cm, dtr, alog2, dtb2)


def _ssd_scan_bwd(xs, bm, cm, dtr, alog2, dtb2, saved, dy, bl, name):
    n = xs.shape[0]
    q = SSD_CHUNK
    nc = n // bl // q
    hd, ns = SSD_HEAD_DIM, SSD_STATE

    def body(xs_ref, b_ref, c_ref, dt_ref, al_ref, db_ref, sv_ref, dy_ref,
             dxs_ref, dbm_ref, dcm_ref, ddt_ref, dal_ref, ddb_ref, ds_ref):
        d, s, i = pl.program_id(0), pl.program_id(1), pl.program_id(2)

        @pl.when(i == 0)
        def _():
            ds_ref[...] = jnp.zeros(ds_ref.shape, F32)

        xl = [xs_ref[:, hd * h:hd * (h + 1)] for h in range(SSD_HEADS)]
        bms = [b_ref[:, ns * g:ns * (g + 1)] for g in range(SSD_GROUPS)]
        cms = [c_ref[:, ns * g:ns * (g + 1)] for g in range(SSD_GROUPS)]
        st = [sv_ref[h] for h in range(SSD_HEADS)]
        fn = functools.partial(_ssd_chunk_fn, d)
        _, vjp = jax.vjp(fn, xl, dt_ref[...], bms, cms, st, al_ref[...], db_ref[...])
        dys = [dy_ref[:, hd * h:hd * (h + 1)] for h in range(SSD_HEADS)]
        dso = [ds_ref[h] for h in range(SSD_HEADS)]
        dxl, ddt, dbg, dcg, dst, dal, ddb = vjp((dys, dso))
        for h in range(SSD_HEADS):
            ds_ref[h] = dst[h]
            dxs_ref[:, hd * h:hd * (h + 1)] = dxl[h]
        for g in range(SSD_GROUPS):
            dbm_ref[:, ns * g:ns * (g + 1)] = dbg[g]
            dcm_ref[:, ns * g:ns * (g + 1)] = dcg[g]
        ddt_ref[...] = ddt
        _acc_rows((dal_ref, ddb_ref), (dal, ddb), jnp.logical_and(s == 0, i == 0))

    def rowblk(d, s, i):
        return s * nc + (nc - 1 - i) + d * (2 * i - (nc - 1))

    row = lambda d, s, i: (rowblk(d, s, i), 0)
    drow = lambda d, s, i: (d, rowblk(d, s, i), 0)
    dfix = lambda d, s, i: (d, 0, 0)
    return pl.pallas_call(
        body, name=name, grid=(2, bl, nc),
        in_specs=[pl.BlockSpec((q, SSD_HEADS * hd), row), pl.BlockSpec((q, SSD_GROUPS * ns), row),
                  pl.BlockSpec((q, SSD_GROUPS * ns), row), pl.BlockSpec((None, q, LANES), drow),
                  pl.BlockSpec((None, 1, LANES), dfix), pl.BlockSpec((None, 1, LANES), dfix),
                  pl.BlockSpec((None, None, SSD_HEADS, hd, ns), lambda d, s, i: (d, rowblk(d, s, i), 0, 0, 0)),
                  pl.BlockSpec((q, SSD_HEADS * hd), row)],
        out_specs=[pl.BlockSpec((None, q, SSD_HEADS * hd), drow), pl.BlockSpec((None, q, SSD_GROUPS * ns), drow),
                   pl.BlockSpec((None, q, SSD_GROUPS * ns), drow), pl.BlockSpec((None, q, LANES), drow),
                   pl.BlockSpec((None, 1, LANES), dfix), pl.BlockSpec((None, 1, LANES), dfix)],
        out_shape=[SDS((2, n, SSD_HEADS * hd), F32), SDS((2, n, SSD_GROUPS * ns), F32), SDS((2, n, SSD_GROUPS * ns), F32),
                   SDS((2, n, LANES), F32), SDS((2, 1, LANES), F32), SDS((2, 1, LANES), F32)],
        scratch_shapes=[pltpu.VMEM((SSD_HEADS, hd, ns), F32)],
        compiler_params=_cp(("arbitrary",) * 3, VMEM_LIMIT),
    )(xs, bm, cm, dtr, alog2, dtb2, saved, dy)


def _s5_consts():
    t, ch, p = S5_T, S5_CH, S5_STATE
    lane = lax.broadcasted_iota(jnp.int32, (1, 2 * p), 1)
    pr = lax.broadcasted_iota(jnp.int32, (p, 2 * p), 0)
    pc = lax.broadcasted_iota(jnp.int32, (p, 2 * p), 1)
    cr = lax.broadcasted_iota(jnp.int32, (ch, t * ch), 0)
    cc = lax.broadcasted_iota(jnp.int32, (ch, t * ch), 1)
    return dict(
        left=lane < p,
        sg=jnp.where(lane < p, -1.0, 1.0).astype(F32),
        dup=(pc % p == pr).astype(F32),
        dup_l=(pc == pr).astype(F32),
        dup_r=(pc == pr + p).astype(F32),
        rep=(cc % ch == cr).astype(F32),
        rep0=(cc == cr).astype(F32),
    )


def _s5_mats(k, rev, lr, li, ls, bre, bim, cre, cim):
    t = S5_T
    step = jnp.exp(ls)
    lr2 = jnp.sum(lr * k["dup"], axis=0, keepdims=True)
    li2 = jnp.sum(li * k["dup"], axis=0, keepdims=True)

    def erow(d):
        ang = (d * step) * li2
        return jnp.exp((d * step) * lr2) * jnp.where(k["left"], jnp.cos(ang), jnp.sin(ang))

    es = [erow(d) for d in range(t + 1)]
    mag = jnp.exp(step * lr)
    ar, ai = mag * jnp.cos(step * li), mag * jnp.sin(step * li)
    den = lr * lr + li * li
    zr = ((ar - 1.0) * lr + ai * li) / den
    zi = (ai * lr - (ar - 1.0) * li) / den
    bbr = zr * bre - zi * bim
    bbi = zr * bim + zi * bre
    bt1 = _dot(bbr, k["dup"], TN, HI)
    bt2 = _dot(bbi, k["dup"], TN, HI)
    bst = _dot(bbr, k["dup_l"], TN, HI) - _dot(bbi, k["dup_r"], TN, HI)
    c1 = _dot(cre, k["dup"], NN, HI)
    c2 = _dot(cim, k["dup"], NN, HI)
    sg = k["sg"]
    ce = [e * c1 + sg * _swap(e) * c2 for e in es]
    lags = range(t - 1, -1, -1) if rev else range(t)
    kt = _dot(bst, jnp.concatenate([ce[d] for d in lags], axis=0), NT, HI)
    toep = jnp.concatenate([_lane_shift(kt, -S5_CH * (t - 1 - s) if rev else S5_CH * s) for s in range(t)], axis=0)
    w_out =jnp.concatenate([ce[(t - qq) if rev else (qq + 1)] * (-sg) for qq in range(t)], axis=0)
    w_st = jnp.concatenate(
        [(lambda e: e * bt1 + sg * _swap(e) * bt2)(es[s if rev else (t - 1 - s)]) for s in range(t)], axis=0)
    return toep, w_out, w_st, es[t]


def _cmul_row(k, e, z):
    es = _swap(e)
    return z * jnp.where(k["left"], e, es) + k["sg"] * _swap(z) * jnp.where(k["left"], es, e)


def _s5_dir(k, rev, nck, x, mats):
    toep, w_out, w_st, a_t = mats
    acc = _dot(x, w_st)
    e = a_t
    kk = 1
    sign = -1 if rev else 1
    while kk < nck:
        acc = acc + _cmul_row(k, e, _shift(acc, sign * kk, nck))
        e = _cmul_row(k, e, e)
        kk *= 2
    prev = _shift(acc, sign, nck)
    return _dot(x, toep) + _dot(prev, w_out, NT)


def _s5_group_fn(nck, x, pf, pb, bre, bim, dcol, wv, wg, bv, bg):
    k = _s5_consts()
    t = S5_T
    y = x * jnp.sum(dcol * k["rep"], axis=0, keepdims=True)
    for rev, (lr, li, ls, cre, cim) in ((False, pf), (True, pb)):
        y = y + _s5_dir(k, rev, nck, x, _s5_mats(k, rev, lr, li, ls, bre, bim, cre, cim))
    gy = jax.nn.gelu(y)
    def kron_eye(w16):
        wide = _dot(w16, k["rep0"], NN, HI)
        return jnp.concatenate([_lane_shift(wide, S5_CH * qq) for qq in range(t)], axis=0)

    kv, kg = kron_eye(wv), kron_eye(wg)
    val =_dot(gy, kv) + jnp.sum(bv * k["rep"], axis=0, keepdims=True)
    gate = _dot(gy, kg) + jnp.sum(bg * k["rep"], axis=0, keepdims=True)
    return val * _sigmoid(gate)


def _s5_specs(r):
    p, ch = S5_STATE, S5_CH
    g3 = lambda i: (i, 0, 0)
    col = pl.BlockSpec((None, p, 1), g3)
    one = pl.BlockSpec((None, 1, 1), g3)
    cmat = pl.BlockSpec((None, ch, p), g3)
    bmat = pl.BlockSpec((None, p, ch), g3)
    ccol = pl.BlockSpec((None, ch, 1), g3)
    sq = pl.BlockSpec((None, ch, ch), g3)
    xs = pl.BlockSpec((None, r, S5_T * ch), g3)
    specs = [xs, col, col, one, cmat, cmat, col, col, one, cmat, cmat, bmat, bmat, ccol, sq, sq, ccol, ccol]
    return specs


def _s5_unpack(vals):
    x = vals[0]
    pf = tuple(vals[1:6])
    pb = tuple(vals[6:11])
    bre, bim, dcol, wv, wg, bv, bg = vals[11:18]
    return x, pf, pb, bre, bim, dcol, wv, wg, bv, bg


def _s5_fwd(args, nck, name):
    x = args[0]
    ng, r, w = x.shape

    def body(*refs):
        vals = [ref[...] for ref in refs[:18]]
        refs[18][...] = _s5_group_fn(nck, *_s5_unpack(vals))

    specs = _s5_specs(r)
    return pl.pallas_call(
        body, name=name, grid=(ng,), in_specs=specs, out_specs=specs[0], out_shape=SDS(x.shape, F32),
        compiler_params=_cp(("arbitrary",), VMEM_LIMIT),
    )(*args)


def _s5_bwd(args, dy, nck, name):
    x = args[0]
    ng, r, w = x.shape

    def body(*refs):
        vals = [ref[...] for ref in refs[:18]]
        _, vjp = jax.vjp(lambda *v: _s5_group_fn(nck, *_s5_unpack(v)), *vals)
        grads = vjp(refs[18][...])
        for o_ref, gval in zip(refs[19:], grads):
            o_ref[...] = gval.astype(o_ref.dtype)

    specs = _s5_specs(r)
    return pl.pallas_call(
        body, name=name, grid=(ng,), in_specs=specs + [specs[0]], out_specs=specs,
        out_shape=[SDS(x.shape, BF16)] + [SDS(a.shape, F32) for a in args[1:]],
        compiler_params=_cp(("arbitrary",), VMEM_LIMIT),
    )(*args, dy)


def _mix_fn(yf, yb, xs, z, s5o, dvec, nw_ssd, nw_s5):
    hr = lax.broadcasted_iota(jnp.int32, (LANES, SSD_HEADS * SSD_HEAD_DIM), 0)
    hc = lax.broadcasted_iota(jnp.int32, (LANES, SSD_HEADS * SSD_HEAD_DIM), 1)
    expand = (hc // SSD_HEAD_DIM == hr).astype(F32)
    dch = jnp.sum(dvec * expand, axis=0, keepdims=True)
    y = (yf + yb + dch * xs) * (z * _sigmoid(z))
    return _rms(y, nw_ssd), _rms(s5o, nw_s5)


def _mix(y2, xs, z, s5o, dvec, nw_ssd, nw_s5, tm, name):
    n, c1 = xs.shape
    c2 = s5o.shape[1]

    def body(yf_ref, yb_ref, xs_ref, z_ref, s_ref, d_ref, n1_ref, n2_ref, o_ref):
        o1, o2 = _mix_fn(yf_ref[...], yb_ref[...], xs_ref[...], z_ref[...], s_ref[...], d_ref[...], n1_ref[...], n2_ref[...])
        o_ref[:, :c1] = o1.astype(BF16)
        o_ref[:, c1:] = o2.astype(BF16)

    row = lambda i: (i, 0)
    fix = lambda i: (0, 0)
    return pl.pallas_call(
        body, name=name, grid=(n // tm,),
        in_specs=[pl.BlockSpec((None, tm, c1), lambda i: (0, i, 0)), pl.BlockSpec((None, tm, c1), lambda i: (1, i, 0)),
                  pl.BlockSpec((tm, c1), row), pl.BlockSpec((tm, c1), row), pl.BlockSpec((tm, c2), row),
                  pl.BlockSpec((LANES, 1), fix), pl.BlockSpec((1, c1), fix), pl.BlockSpec((1, c2), fix)],
        out_specs=pl.BlockSpec((tm, c1 + c2), row), out_shape=SDS((n, c1 + c2), BF16),
        compiler_params=_cp(("arbitrary",), VMEM_LIMIT),
    )(y2, y2, xs, z, s5o, dvec, nw_ssd, nw_s5)


def _acc_rows(refs, vals, first):
    @pl.when(first)
    def _():
        for ref, v in zip(refs, vals):
            ref[...] = v

    @pl.when(jnp.logical_not(first))
    def _():
        for ref, v in zip(refs, vals):
            ref[...] += v


def _mix_bwd(y2, xs, z, s5o, dvec, nw_ssd, nw_s5, dmix, tm, name):
    n, c1 = xs.shape
    c2 = s5o.shape[1]

    def body(yf_ref, yb_ref, xs_ref, z_ref, s_ref, d_ref, n1_ref, n2_ref, dm_ref,
             dy_ref, dxs_ref, dz_ref, ds_ref, dd_ref, dn1_ref, dn2_ref):
        _, vjp = jax.vjp(_mix_fn, yf_ref[...], yb_ref[...], xs_ref[...], z_ref[...], s_ref[...], d_ref[...], n1_ref[...], n2_ref[...])
        dyf, _, dxs, dz, ds, dd, dn1, dn2 = vjp((dm_ref[:, :c1], dm_ref[:, c1:]))
        dy_ref[...] = dyf
        dxs_ref[...] = dxs
        dz_ref[...] = dz.astype(BF16)
        ds_ref[...] = ds
        _acc_rows((dd_ref, dn1_ref, dn2_ref), (dd, dn1, dn2), pl.program_id(0) == 0)

    row = lambda i: (i, 0)
    fix = lambda i: (0, 0)
    return pl.pallas_call(
        body, name=name, grid=(n // tm,),
        in_specs=[pl.BlockSpec((None, tm, c1), lambda i: (0, i, 0)), pl.BlockSpec((None, tm, c1), lambda i: (1, i, 0)),
                  pl.BlockSpec((tm, c1), row), pl.BlockSpec((tm, c1), row), pl.BlockSpec((tm, c2), row),
                  pl.BlockSpec((LANES, 1), fix), pl.BlockSpec((1, c1), fix), pl.BlockSpec((1, c2), fix),
                  pl.BlockSpec((tm, c1 + c2), row)],
        out_specs=[pl.BlockSpec((tm, c1), row), pl.BlockSpec((tm, c1), row), pl.BlockSpec((tm, c1), row), pl.BlockSpec((tm, c2), row),
                   pl.BlockSpec((LANES, 1), fix), pl.BlockSpec((1, c1), fix), pl.BlockSpec((1, c2), fix)],
        out_shape=[SDS((n, c1), F32), SDS((n, c1), F32), SDS((n, c1), BF16), SDS((n, c2), F32),
                   SDS((LANES, 1), F32), SDS((1, c1), F32), SDS((1, c2), F32)],
        compiler_params=_cp(("arbitrary",), VMEM_LIMIT),
    )(y2, y2, xs, z, s5o, dvec, nw_ssd, nw_s5, dmix)


def _final_loss(h2, nw, tgt, tm, name):
    n, d = h2.shape

    def loss_fn(h, w, t):
        e = _rms(h, w) - t
        return (0.5 / d) * jnp.sum(e * e)

    def body(h_ref, w_ref, t_ref, l_ref, dh_ref, dw_ref):
        loss, (dh, dw) = jax.value_and_grad(loss_fn, argnums=(0, 1))(h_ref[...], w_ref[...], t_ref[...])
        dh_ref[...] = dh
        _acc_rows((l_ref, dw_ref), (jnp.full((1, LANES), loss, F32), dw), pl.program_id(0) == 0)

    row = lambda i: (i, 0)
    fix = lambda i: (0, 0)
    return pl.pallas_call(
        body, name=name, grid=(n // tm,),
        in_specs=[pl.BlockSpec((tm, d), row), pl.BlockSpec((1, d), fix), pl.BlockSpec((tm, d), row)],
        out_specs=[pl.BlockSpec((1, LANES), fix), pl.BlockSpec((tm, d), row), pl.BlockSpec((1, d), fix)],
        out_shape=[SDS((1, LANES), F32), SDS((n, d), F32), SDS((1, d), F32)],
        compiler_params=_cp(("arbitrary",), VMEM_LIMIT),
    )(h2, nw, tgt)


def _norm_bwd(x, nw, dhn, dres, tm, name):
    n, d = x.shape

    def body(x_ref, w_ref, g_ref, r_ref, dx_ref, dw_ref):
        _, vjp = jax.vjp(_rms, x_ref[...], w_ref[...])
        dx, dw = vjp(g_ref[...])
        dx_ref[...] = r_ref[...] + dx
        _acc_rows((dw_ref,), (dw,), pl.program_id(0) == 0)

    row = lambda i: (i, 0)
    fix = lambda i: (0, 0)
    return pl.pallas_call(
        body, name=name, grid=(n // tm,),
        in_specs=[pl.BlockSpec((tm, d), row), pl.BlockSpec((1, d), fix), pl.BlockSpec((tm, d), row), pl.BlockSpec((tm, d), row)],
        out_specs=[pl.BlockSpec((tm, d), row), pl.BlockSpec((1, d), fix)],
        out_shape=[SDS((n, d), F32), SDS((1, d), F32)],
        compiler_params=_cp(("arbitrary",), VMEM_LIMIT),
    )(x, nw, dhn, dres)


def _row_tile(n, cap=512):
    for t in range(min(cap, n) // 8 * 8, 7, -8):
        if n % t == 0:
            return t
    return n


def _sum_lead(a, name):
    kk, n, c = a.shape
    tm = _row_tile(n)

    def body(a_ref, o_ref):
        acc = a_ref[0].astype(F32)
        for i in range(1, kk):
            acc = acc + a_ref[i].astype(F32)
        o_ref[...] = acc

    return pl.pallas_call(
        body, name=name, grid=(n // tm,),
        in_specs=[pl.BlockSpec((kk, tm, c), lambda i: (0, i, 0))],
        out_specs=pl.BlockSpec((tm, c), lambda i: (i, 0)), out_shape=SDS((n, c), F32),
        compiler_params=_cp(("arbitrary",), VMEM_LIMIT),
    )(a)


def _adamw(w, g, m, v, name):
    n, c = w.shape
    tm = _row_tile(n)

    def body(w_ref, g_ref, m_ref, v_ref, d_ref, nm_ref, nv_ref):
        gv = g_ref[...]
        mn = ADAM_B1 * m_ref[...] + (1.0 - ADAM_B1) * gv
        vn = ADAM_B2 * v_ref[...] + (1.0 - ADAM_B2) * jnp.square(gv)
        m_hat = mn / (1.0 - ADAM_B1 ** ADAM_STEP)
        v_hat = vn / (1.0 - ADAM_B2 ** ADAM_STEP)
        d_ref[...] = -ADAM_LR * (m_hat / (jnp.sqrt(v_hat) + ADAM_EPS) + ADAM_WD * w_ref[...])
        nm_ref[...] = mn
        nv_ref[...] = vn

    spec = pl.BlockSpec((tm, c), lambda i: (i, 0))
    return pl.pallas_call(
        body, name=name, grid=(n // tm,), in_specs=[spec] * 4, out_specs=[spec] * 3,
        out_shape=[SDS((n, c), F32)] * 3, compiler_params=_cp(("arbitrary",), VMEM_LIMIT),
    )(w, g, m, v)


ANY = pl.BlockSpec(memory_space=pl.ANY)


def _me():
    return lax.axis_index("x"), lax.axis_index("y"), lax.axis_index("c")


def _gather_xy(split, whole, name):
    ns, cnt = len(split), len(split) + len(whole)

    def body(*refs):
        src, dst = refs[:cnt], refs[cnt:2 * cnt]
        send, recv = refs[2 * cnt:]
        x, y, c = _me()
        mine = 2 * x + y
        chips = [(1 - x, y), (x, 1 - y), (1 - x, 1 - y)]

        def ici(a, j, slot):
            px, py = chips[j]
            if a < ns:
                s_ref, d_ref = src[a].at[c], dst[a].at[slot].at[c]
            else:
                s_ref, d_ref = src[a], dst[a].at[slot]
            return pltpu.make_async_remote_copy(s_ref, d_ref, send.at[3 * a + j], recv.at[3 * a + j],
                                                device_id=(px, py, c), device_id_type=MESH)

        def d2d(a, j, half):
            px, py = chips[j]
            ref = dst[a].at[2 * px + py].at[half]
            return pltpu.make_async_remote_copy(ref, ref, send.at[3 * cnt + 3 * a + j], recv.at[3 * cnt + 3 * a + j],
                                                device_id=(x, y, 1 - c), device_id_type=MESH)

        def own(a):
            return pltpu.make_async_remote_copy(src[a], dst[a].at[mine], send.at[nsem - cnt + a], recv.at[nsem - cnt + a],
                                                device_id=(x, y, 1 - c), device_id_type=MESH)

        started = []
        for a in range(cnt):
            cp = own(a)
            cp.start()
            started.append(cp)
            for j in range(3):
                cp = ici(a, j, mine)
                cp.start()
                started.append(cp)
        for a in range(cnt):
            for j, (px, py) in enumerate(chips):
                ici(a, j, 2 * px + py).wait_recv()
                if a < ns:
                    cp = d2d(a, j, c)
                    cp.start()
                    started.append(cp)
        for a in range(ns):
            for j in range(3):
                d2d(a, j, 1 - c).wait_recv()
        for a in range(cnt):
            own(a).wait_recv()
        for cp in started:
            cp.wait_send()

    nsem = 3 * cnt + 3 * ns + cnt
    return pl.pallas_call(
        body, name=name, in_specs=[ANY] * cnt, out_specs=[ANY] * cnt,
        out_shape=[SDS((4,) + s.shape, s.dtype) for s in split + whole],
        scratch_shapes=[pltpu.SemaphoreType.DMA((nsem,)), pltpu.SemaphoreType.DMA((nsem,))],
    )(*split, *whole)


def _swap_sibling(parts, pick, name):
    cnt = len(parts)

    def body(*refs):
        src, dst = refs[:cnt], refs[cnt:2 * cnt]
        send, recv = refs[2 * cnt:]
        x, y, c = _me()
        cps = []
        for a in range(cnt):
            cp = pltpu.make_async_remote_copy(src[a].at[1 - c] if pick else src[a], dst[a], send.at[a], recv.at[a],
                                              device_id=(x, y, 1 - c), device_id_type=MESH)
            cp.start()
            cps.append(cp)
        for cp in cps:
            cp.wait()

    return pl.pallas_call(
        body, name=name, in_specs=[ANY] * cnt, out_specs=[ANY] * cnt,
        out_shape=[SDS(p.shape[1:] if pick else p.shape, p.dtype) for p in parts],
        scratch_shapes=[pltpu.SemaphoreType.DMA((cnt,)), pltpu.SemaphoreType.DMA((cnt,))],
    )(*parts)


def _scatter_xy(parts, name):
    cnt = len(parts)

    def body(*refs):
        src, dst = refs[:cnt], refs[cnt:2 * cnt]
        send, recv, loc = refs[2 * cnt:]
        x, y, c = _me()
        mine = 2 * x + y
        chips = [(1 - x, y), (x, 1 - y), (1 - x, 1 - y)]
        local = []
        for a in range(cnt):
            cp = pltpu.make_async_copy(src[a].at[mine], dst[a].at[mine], loc.at[a])
            cp.start()
            local.append(cp)
        sends = []
        for a in range(cnt):
            for j, (px, py) in enumerate(chips):
                cp = pltpu.make_async_remote_copy(src[a].at[2 * px + py], dst[a].at[mine], send.at[3 * a + j], recv.at[3 * a + j],
                                                  device_id=(px, py, c), device_id_type=MESH)
                cp.start()
                sends.append(cp)
        for a in range(cnt):
            for j, (px, py) in enumerate(chips):
                pltpu.make_async_remote_copy(src[a].at[mine], dst[a].at[2 * px + py], send.at[3 * a + j], recv.at[3 * a + j],
                                             device_id=(px, py, c), device_id_type=MESH).wait_recv()
        for cp in sends:
            cp.wait_send()
        for cp in local:
            cp.wait()

    return pl.pallas_call(
        body, name=name, in_specs=[ANY] * cnt, out_specs=[ANY] * cnt,
        out_shape=[SDS(p.shape, p.dtype) for p in parts],
        scratch_shapes=[pltpu.SemaphoreType.DMA((3 * cnt,)), pltpu.SemaphoreType.DMA((3 * cnt,)), pltpu.SemaphoreType.DMA((cnt,))],
    )(*parts)


def _bcast_all(buf, name):
    def body(src, dst, send, recv, loc):
        x, y, c = _me()
        mine = 4 * x + 2 * y + c
        own = pltpu.make_async_copy(src, dst.at[mine], loc)
        own.start()
        sends = []
        for k in range(1, 8):
            px, py, pc = x ^ (k >> 2), y ^ ((k >> 1) & 1), c ^ (k & 1)
            cp = pltpu.make_async_remote_copy(src, dst.at[mine], send.at[k - 1], recv.at[k - 1],
                                              device_id=(px, py, pc), device_id_type=MESH)
            cp.start()
            sends.append(cp)
        for k in range(1, 8):
            px, py, pc = x ^ (k >> 2), y ^ ((k >> 1) & 1), c ^ (k & 1)
            pltpu.make_async_remote_copy(src, dst.at[4 * px + 2 * py + pc], send.at[k - 1], recv.at[k - 1],
                                         device_id=(px, py, pc), device_id_type=MESH).wait_recv()
        for cp in sends:
            cp.wait_send()
        own.wait()

    return pl.pallas_call(
        body, name=name, in_specs=[ANY], out_specs=ANY, out_shape=SDS((8,) + buf.shape, buf.dtype),
        scratch_shapes=[pltpu.SemaphoreType.DMA((7,)), pltpu.SemaphoreType.DMA((7,)), pltpu.SemaphoreType.DMA(())],
    )(buf)


def _add2(a, b, dtype, name):
    shp = a.shape
    a2, b2 = a.reshape(-1, shp[-1]), b.reshape(-1, shp[-1])
    n, c = a2.shape
    tm = _row_tile(n, 256)

    def body(a_ref, b_ref, o_ref):
        o_ref[...] = (a_ref[...] + b_ref[...]).astype(dtype)

    spec = pl.BlockSpec((tm, c), lambda i: (i, 0))
    return pl.pallas_call(body, name=name, grid=(n // tm,), in_specs=[spec, spec], out_specs=spec,
                          out_shape=SDS((n, c), dtype), compiler_params=_cp(("arbitrary",), VMEM_LIMIT))(a2, b2).reshape(shp)


def _x_layout(u, bl):
    n, c = u.shape
    nck = n // bl // S5_T
    return u.reshape(bl, nck, S5_T, S5_GROUPS, S5_CH).transpose(3, 0, 1, 2, 4).reshape(S5_GROUPS, bl * nck, S5_T * S5_CH)


def _token_layout(xg, bl):
    ng, r, w = xg.shape
    nck = r // bl
    return xg.reshape(ng, bl, nck, S5_T, S5_CH).transpose(1, 2, 3, 0, 4).reshape(bl * nck * S5_T, ng * S5_CH)


def _pad_lanes(a, lanes=LANES):
    return jnp.pad(a, ((0, 0), (0, lanes - a.shape[1])))


def _local_step(x, tgt, p, bl):
    n, d = x.shape
    sw = SSD_HEADS * SSD_HEAD_DIM
    gn = SSD_GROUPS * SSD_STATE
    tm = min(n, 512)
    tm_ffn = min(n, 256)
    nck = n // bl // S5_T
    dff = p["w_down"].shape[0]
    s5w = S5_GROUPS * S5_CH

    w_in = p["w_in"]
    o1, o2, o3, o4 = sw, sw + sw, sw + sw + gn, sw + sw + 2 * gn
    w_z, w_xs, w_b, w_c = w_in[:, :o1], w_in[:, o1:o2], w_in[:, o2:o3], w_in[:, o3:o4]
    w_dt = _pad_lanes(w_in[:, o4:o4 + 2 * SSD_HEADS])
    w_u = w_in[:, o4 + 2 * SSD_HEADS:]
    in_ws = [w_z, w_xs, w_b, w_c, w_dt, w_u]
    cw, cb_ = p["ssd_conv_w"], p["ssd_conv_b"]
    conv_parts = [(cw[:, :sw], cb_[:, :sw]), (cw[:, sw:sw + gn], cb_[:, sw:sw + gn]), (cw[:, sw + gn:], cb_[:, sw + gn:])]
    alog2 = jnp.stack([_pad_lanes(p["ssd_a_log_fwd"]), _pad_lanes(p["ssd_a_log_bwd"])])
    dtb2 = jnp.stack([_pad_lanes(p["ssd_dt_bias_fwd"]), _pad_lanes(p["ssd_dt_bias_bwd"])])
    dvec = _pad_lanes(p["ssd_d"]).reshape(LANES, 1)

    hn, z, xs_pre, b_pre, c_pre, dtr, u = _norm_matmul(x, p["norm_mix_w"], in_ws, tm, "in_proj")
    pres = [xs_pre, b_pre, c_pre]
    acts = [_conv_silu(pre, w, b, bl, min(256, pre.shape[1]), f"ssd_conv_{i}") for i, (pre, (w, b)) in enumerate(zip(pres, conv_parts))]
    xs_a, b_a, c_a = acts
    dtr2 = jnp.stack([dtr, _pad_lanes(dtr[:, SSD_HEADS:2 * SSD_HEADS])])
    y2, saved = _ssd_scan(xs_a, b_a, c_a, dtr2, alog2, dtb2, bl, "ssd_scan")

    def col(a):
        return a.reshape(a.shape + (1,))

    s5_params = [
        col(p["s5_lambda_re_fwd"]), col(p["s5_lambda_im_fwd"]), p["s5_log_step_fwd"].reshape(S5_GROUPS, 1, 1), p["s5_c_re_fwd"], p["s5_c_im_fwd"],
        col(p["s5_lambda_re_bwd"]), col(p["s5_lambda_im_bwd"]), p["s5_log_step_bwd"].reshape(S5_GROUPS, 1, 1), p["s5_c_re_bwd"], p["s5_c_im_bwd"],
        p["s5_b_re"], p["s5_b_im"], col(p["s5_d"].reshape(S5_GROUPS, S5_CH)),
        p["s5_glu_w"][:, :, :S5_CH], p["s5_glu_w"][:, :, S5_CH:], col(p["s5_glu_b"][:, :S5_CH]), col(p["s5_glu_b"][:, S5_CH:]),
    ]
    s5_args = [_x_layout(u, bl)] + s5_params
    s5o = _token_layout(_s5_fwd(s5_args, nck, "s5_fwd"), bl)
    ymix = _mix(y2, xs_a, z, s5o, dvec, p["ssd_norm_w"], p["s5_norm_w"], tm, "mix")
    h1 = _matmul_res(ymix, p["w_out"], x, tm, "out_proj")
    w_up = p["w_up"]
    hn2, up_v, up_g = _norm_matmul(h1, p["norm_ffn_w"], [w_up[:, :dff], w_up[:, dff:]], tm_ffn, "ffn_up")
    fw, fb = p["ffn_conv_w"], p["ffn_conv_b"]
    act = _conv_glu(up_v, up_g, fw[:, :dff], fw[:, dff:], fb[:, :dff], fb[:, dff:], bl, 256, "ffn_conv")
    h2 = _matmul_res(act, p["w_down"], h1, tm, "ffn_down")
    loss, dh2, g_nfw = _final_loss(h2, p["norm_final_w"].reshape(1, d), tgt, tm, "final_loss")

    g = {"norm_final_w": g_nfw.reshape(d)}
    g["w_down"] = _matmul_tn(act, dh2, tm, d, "ffn_down_dw")
    dact = _matmul_nt([dh2], [p["w_down"]], tm, "ffn_down_dx")
    dup_v, dup_g, dwv, dwg, dbv, dbg = _conv_glu_bwd(up_v, up_g, fw[:, :dff], fw[:, dff:], fb[:, :dff], fb[:, dff:], dact, bl, 256, "ffn_conv_bwd")
    g["ffn_conv_w"] = jnp.concatenate([dwv, dwg], 1)
    g["ffn_conv_b"] = jnp.concatenate([dbv, dbg], 1)
    g["w_up"] = jnp.concatenate([_matmul_tn(hn2, dup_v, tm, dff // 2, "ffn_up_dw_v"), _matmul_tn(hn2, dup_g, tm, dff // 2, "ffn_up_dw_g")], 1)
    dhn2 = _matmul_nt([dup_v, dup_g], [w_up[:, :dff], w_up[:, dff:]], tm_ffn, "ffn_up_dx")
    dh1, g["norm_ffn_w"] = _norm_bwd(h1, p["norm_ffn_w"], dhn2, dh2, tm, "ffn_norm_bwd")
    g["w_out"] = _matmul_tn(ymix, dh1, tm, d, "out_proj_dw")
    dmix = _matmul_nt([dh1], [p["w_out"]], tm, "out_proj_dx")
    dyssd, dxs_gate, dz, ds5o, g_d, g["ssd_norm_w"], g["s5_norm_w"] = _mix_bwd(
        y2, xs_a, z, s5o, dvec, p["ssd_norm_w"], p["s5_norm_w"], dmix, tm, "mix_bwd")
    g["ssd_d"] = g_d[:SSD_HEADS].reshape(1, SSD_HEADS)
    s5g = _s5_bwd(s5_args, _x_layout(ds5o, bl), nck, "s5_bwd")
    du = _token_layout(s5g[0], bl)
    (g["s5_lambda_re_fwd"], g["s5_lambda_im_fwd"], g["s5_log_step_fwd"], g["s5_c_re_fwd"], g["s5_c_im_fwd"],
     g["s5_lambda_re_bwd"], g["s5_lambda_im_bwd"], g["s5_log_step_bwd"], g["s5_c_re_bwd"], g["s5_c_im_bwd"],
     g["s5_b_re"], g["s5_b_im"], g_s5d, g_wv, g_wg, g_bv, g_bg) = s5g[1:]
    for k_ in ("s5_lambda_re_fwd", "s5_lambda_im_fwd", "s5_lambda_re_bwd", "s5_lambda_im_bwd"):
        g[k_] = g[k_].reshape(S5_GROUPS, S5_STATE)
    for k_ in ("s5_log_step_fwd", "s5_log_step_bwd"):
        g[k_] = g[k_].reshape(S5_GROUPS)
    g["s5_d"] = g_s5d.reshape(1, s5w)
    g["s5_glu_w"] = jnp.concatenate([g_wv, g_wg], 2)
    g["s5_glu_b"] = jnp.concatenate([g_bv.reshape(S5_GROUPS, S5_CH), g_bg.reshape(S5_GROUPS, S5_CH)], 1)
    dxs2, dbm2, dcm2, ddt2, dal2, ddb2 = _ssd_scan_bwd(xs_a, b_a, c_a, dtr2, alog2, dtb2, saved, dyssd, bl, "ssd_scan_bwd")
    g["ssd_a_log_fwd"], g["ssd_a_log_bwd"] = dal2[0, :, :SSD_HEADS], dal2[1, :, :SSD_HEADS]
    g["ssd_dt_bias_fwd"], g["ssd_dt_bias_bwd"] = ddb2[0, :, :SSD_HEADS], ddb2[1, :, :SSD_HEADS]
    cots = [[(dxs2, 0), (dxs2, 1), (dxs_gate, None)], [(dbm2, 0), (dbm2, 1)], [(dcm2, 0), (dcm2, 1)]]
    dpres, dcw, dcb = [], [], []
    for i, (pre, (w, b), cot) in enumerate(zip(pres, conv_parts, cots)):
        dp, dw_, db_ = _conv_silu_bwd(pre, w, b, cot, bl, min(256, pre.shape[1]), f"ssd_conv_bwd_{i}")
        dpres.append(dp)
        dcw.append(dw_)
        dcb.append(db_)
    g["ssd_conv_w"] = jnp.concatenate(dcw, 1)
    g["ssd_conv_b"] = jnp.concatenate(dcb, 1)
    ddtr = _pad_lanes(jnp.concatenate([ddt2[0][:, :SSD_HEADS], ddt2[1][:, :SSD_HEADS]], 1))
    dprojs = [dz, dpres[0], dpres[1], dpres[2], ddtr, du]
    dws = [_matmul_tn(hn, dpj, tm, dpj.shape[1], f"in_proj_dw_{i}") for i, dpj in enumerate(dprojs)]
    dws[4] = dws[4][:, :2 * SSD_HEADS]
    g["w_in"] = jnp.concatenate(dws, 1)
    dhn = _matmul_nt(dprojs, in_ws, tm, "in_proj_dx")
    grad_x, g["norm_mix_w"] = _norm_bwd(x, p["norm_mix_w"], dhn, dh1, tm, "mix_norm_bwd")
    return loss, grad_x, g


_WEIGHTS = ['norm_mix_w', 'w_in', 'ssd_conv_w', 'ssd_conv_b', 'ssd_dt_bias_fwd', 'ssd_dt_bias_bwd', 'ssd_a_log_fwd', 'ssd_a_log_bwd',
            'ssd_d', 'ssd_norm_w', 's5_lambda_re_fwd', 's5_lambda_im_fwd', 's5_log_step_fwd', 's5_lambda_re_bwd', 's5_lambda_im_bwd',
            's5_log_step_bwd', 's5_b_re', 's5_b_im', 's5_c_re_fwd', 's5_c_im_fwd', 's5_c_re_bwd', 's5_c_im_bwd', 's5_d', 's5_glu_w',
            's5_glu_b', 's5_norm_w', 'w_out', 'norm_ffn_w', 'ffn_w_up', 'ffn_conv_w', 'ffn_conv_b', 'ffn_w_down', 'norm_final_w']
_BIG = ('w_in', 'w_out', 'ffn_w_up', 'ffn_w_down')
_CONV = ('ssd_conv_w', 'ffn_conv_w')


def _pack(arrs):
    flat = jnp.concatenate([a.reshape(-1) for a in arrs])
    rows = -(-flat.shape[0] // (64 * LANES)) * 64
    return jnp.pad(flat, (0, rows * LANES - flat.shape[0])).reshape(rows, LANES)


def _unpack(buf, shapes):
    flat = buf.reshape(-1)
    out, off = [], 0
    for shp in shapes:
        size = math.prod(shp)
        out.append(flat[off:off + size].reshape(shp))
        off += size
    return out


def kernel(x, norm_mix_w, w_in, ssd_conv_w, ssd_conv_b, ssd_dt_bias_fwd, ssd_dt_bias_bwd, ssd_a_log_fwd, ssd_a_log_bwd, ssd_d, ssd_norm_w, s5_lambda_re_fwd, s5_lambda_im_fwd, s5_log_step_fwd, s5_lambda_re_bwd, s5_lambda_im_bwd, s5_log_step_bwd, s5_b_re, s5_b_im, s5_c_re_fwd, s5_c_im_fwd, s5_c_re_bwd, s5_c_im_bwd, s5_d, s5_glu_w, s5_glu_b, s5_norm_w, w_out, norm_ffn_w, ffn_w_up, ffn_conv_w, ffn_conv_b, ffn_w_down, norm_final_w, loss_target, m_norm_mix_w, m_w_in, m_ssd_conv_w, m_ssd_conv_b, m_ssd_dt_bias_fwd, m_ssd_dt_bias_bwd, m_ssd_a_log_fwd, m_ssd_a_log_bwd, m_ssd_d, m_ssd_norm_w, m_s5_lambda_re_fwd, m_s5_lambda_im_fwd, m_s5_log_step_fwd, m_s5_lambda_re_bwd, m_s5_lambda_im_bwd, m_s5_log_step_bwd, m_s5_b_re, m_s5_b_im, m_s5_c_re_fwd, m_s5_c_im_fwd, m_s5_c_re_bwd, m_s5_c_im_bwd, m_s5_d, m_s5_glu_w, m_s5_glu_b, m_s5_norm_w, m_w_out, m_norm_ffn_w, m_ffn_w_up, m_ffn_conv_w, m_ffn_conv_b, m_ffn_w_down, m_norm_final_w, v_norm_mix_w, v_w_in, v_ssd_conv_w, v_ssd_conv_b, v_ssd_dt_bias_fwd, v_ssd_dt_bias_bwd, v_ssd_a_log_fwd, v_ssd_a_log_bwd, v_ssd_d, v_ssd_norm_w, v_s5_lambda_re_fwd, v_s5_lambda_im_fwd, v_s5_log_step_fwd, v_s5_lambda_re_bwd, v_s5_lambda_im_bwd, v_s5_log_step_bwd, v_s5_b_re, v_s5_b_im, v_s5_c_re_fwd, v_s5_c_im_fwd, v_s5_c_re_bwd, v_s5_c_im_bwd, v_s5_d, v_s5_glu_w, v_s5_glu_b, v_s5_norm_w, v_w_out, v_norm_ffn_w, v_ffn_w_up, v_ffn_conv_w, v_ffn_conv_b, v_ffn_w_down, v_norm_final_w):
    args = dict(locals())
    w = {k_: args[k_] for k_ in _WEIGHTS}
    m = {k_: args["m_" + k_] for k_ in _WEIGHTS}
    v = {k_: args["v_" + k_] for k_ in _WEIGHTS}
    bl, sl, d = x.shape
    chip = 2 * lax.axis_index("x") + lax.axis_index("y")
    core = lax.axis_index("c")

    shards = [w[k_][0].astype(BF16) for k_ in _BIG] + [w[k_][0] for k_ in _CONV]
    split = [s.reshape(2, s.shape[0] // 2, s.shape[1]) for s in shards[:len(_BIG)]]
    gathered = _gather_xy(split, shards[len(_BIG):], "gather_weights")
    g_in, g_out, g_up, g_down, g_scw, g_fcw = [got.reshape((4,) + s.shape) for got, s in zip(gathered, shards)]

    def cols(a):
        return jnp.moveaxis(a, 0, 1).reshape(a.shape[1], 4 * a.shape[2])

    p = {k_: (w[k_][0] if w[k_].ndim >= 3 else w[k_]) for k_ in _WEIGHTS if k_ not in _BIG + _CONV}
    p["w_in"], p["w_up"] = cols(g_in), cols(g_up)
    p["w_out"], p["w_down"] = g_out.reshape(-1, g_out.shape[2]), g_down.reshape(-1, g_down.shape[2])
    p["ssd_conv_w"], p["ffn_conv_w"] = cols(g_scw), cols(g_fcw)

    loss, grad_x, g = _local_step(x.reshape(bl * sl, d), loss_target.reshape(bl * sl, d), p, bl)
    g["ffn_w_up"], g["ffn_w_down"] = g.pop("w_up"), g.pop("w_down")

    def owner_major(a, k_):
        r, c = w[k_].shape[1:]
        if a.shape[0] == r:
            a = jnp.moveaxis(a.reshape(r, 4, c), 1, 0)
        else:
            a = a.reshape(4, r, c)
        return a.reshape(4, 2, r // 2, c)

    small = [k_ for k_ in _WEIGHTS if k_ not in _BIG]
    small_full_shapes = [g[k_].shape for k_ in small]
    buf = _pack([g[k_] for k_ in small] + [loss[0, :1]])
    parts = [jnp.moveaxis(owner_major(g[k_], k_), 1, 0) for k_ in _BIG]
    parts.append(jnp.moveaxis(buf.reshape(4, 2, -1, LANES), 1, 0))
    got = _swap_sibling(parts, True, "reduce_sibling")
    mine = [lax.dynamic_index_in_dim(pt, core, 0, keepdims=False) for pt in parts]
    chip_sums = [_add2(a, b, BF16 if i < len(_BIG) else F32, f"reduce_add_{i}") for i, (a, b) in enumerate(zip(mine, got))]
    from_chips = _scatter_xy(chip_sums, "reduce_chips")
    halves = [_sum_lead(a.reshape(4, -1, a.shape[-1]), f"reduce_sum_{i}") for i, a in enumerate(from_chips)]
    other = _swap_sibling(halves[:-1], False, "reduce_join")
    big_grad = {}
    for k_, own_half, sib_half in zip(_BIG, halves, other):
        south = core == 0
        full = jnp.stack([jnp.where(south, own_half, sib_half), jnp.where(south, sib_half, own_half)])
        big_grad[k_] = full.reshape((1,) + w[k_].shape[1:])
    tot = _bcast_all(halves[-1], "reduce_small").reshape(buf.shape)
    unp = _unpack(tot, small_full_shapes + [(1,)])
    small_grad = dict(zip(small, unp[:-1]))
    loss_out = unp[-1].reshape(())
    for k_ in _CONV:
        cshard = w[k_].shape[2]
        small_grad[k_] = lax.dynamic_slice_in_dim(small_grad[k_], chip * cshard, cshard, 1)

    grads, deltas, new_m, new_v = {}, {}, {}, {}
    for k_ in _BIG:
        shp = w[k_].shape
        grads[k_] = big_grad[k_]
        dl, nm, nv = _adamw(w[k_][0], big_grad[k_][0], m[k_][0], v[k_][0], f"adamw_{k_}")
        deltas[k_], new_m[k_], new_v[k_] = dl.reshape(shp), nm.reshape(shp), nv.reshape(shp)
    sw_ = _pack([w[k_] for k_ in small])
    sg_ = _pack([small_grad[k_] for k_ in small])
    sm_ = _pack([m[k_] for k_ in small])
    sv_ = _pack([v[k_] for k_ in small])
    dl, nm, nv = _adamw(sw_, sg_, sm_, sv_, "adamw_small")
    shapes = [w[k_].shape for k_ in small]
    for k_, a, b, c_ in zip(small, _unpack(dl, shapes), _unpack(nm, shapes), _unpack(nv, shapes)):
        deltas[k_], new_m[k_], new_v[k_] = a, b, c_
        grads[k_] = small_grad[k_].reshape(w[k_].shape)
    return (loss_out, grad_x.reshape(bl, sl, d), *[grads[k_] for k_ in _WEIGHTS], *[deltas[k_] for k_ in _WEIGHTS],
            *[new_m[k_] for k_ in _WEIGHTS], *[new_v[k_] for k_ in _WEIGHTS])
```

```python
import functools
import math

import jax
import jax.numpy as jnp
from jax import lax
from jax.experimental import pallas as pl
from jax.experimental.pallas import tpu as pltpu

F32 = jnp.float32
BF16 = jnp.bfloat16
HI = lax.Precision.HIGHEST
SDS = jax.ShapeDtypeStruct
MESH = pl.DeviceIdType.MESH

NN = (((1,), (0,)), ((), ()))
NT = (((1,), (1,)), ((), ()))
TN = (((0,), (0,)), ((), ()))

EPS = 1e-6
SSD_HEADS = 16
SSD_HEAD_DIM = 64
SSD_GROUPS = 4
SSD_STATE = 128
SSD_CHUNK = 128
SSD_CONV = 5
S5_GROUPS = 32
S5_CH = 16
S5_STATE = 64
S5_T = 16
LANES = 128
ADAM_LR, ADAM_B1, ADAM_B2, ADAM_EPS, ADAM_WD, ADAM_STEP = 0.001, 0.9, 0.999, 1e-08, 0.01, 10
V7X_VMEM_BYTES = 64 * 1024 * 1024
VMEM_LIMIT = V7X_VMEM_BYTES - 8 * 1024 * 1024


def _cp(sem, vmem=None):
    return pltpu.CompilerParams(dimension_semantics=sem, vmem_limit_bytes=vmem)


def _dot(a, b, dims=NN, precision=None):
    return lax.dot_general(a, b, dims, precision=precision, preferred_element_type=F32)


def _rms(x, w):
    return x * lax.rsqrt(jnp.mean(x * x, axis=-1, keepdims=True) + EPS) * w


def _sigmoid(x):
    return 1.0 / (1.0 + jnp.exp(-x))


def _softplus(x):
    return jnp.maximum(x, 0.0) + jnp.log1p(jnp.exp(-jnp.abs(x)))


@functools.partial(jax.custom_vjp, nondiff_argnums=(1, 2))
def _shift(x, k, seg):
    n = x.shape[0]
    r = lax.broadcasted_iota(jnp.int32, x.shape, 0) % seg
    y = pltpu.roll(x, k % n, 0)
    ok = (r >= k) if k > 0 else (r < seg + k)
    return jnp.where(ok, y, 0.0)


def _shift_fwd(x, k, seg):
    return _shift(x, k, seg), None


def _shift_bwd(k, seg, _, g):
    return (_shift(g, -k, seg),)


_shift.defvjp(_shift_fwd, _shift_bwd)


@functools.partial(jax.custom_vjp, nondiff_argnums=(1,))
def _lane_shift(x, k):
    if k == 0:
        return x
    n = x.shape[1]
    lane = lax.broadcasted_iota(jnp.int32, x.shape, 1)
    ok = (lane >= k) if k > 0 else (lane < n + k)
    return jnp.where(ok, pltpu.roll(x, k % n, 1), 0.0)


_lane_shift.defvjp(lambda x, k: (_lane_shift(x, k), None), lambda k, _, g: (_lane_shift(g, -k),))


@jax.custom_vjp
def _swap(z):
    return pltpu.roll(z, LANES // 2, 1)


_swap.defvjp(lambda z: (_swap(z), None), lambda _, g: (_swap(g),))


def _norm_matmul(x, nw, ws, tm, name):
    n, d = x.shape
    k = len(ws)

    def body(x_ref, nw_ref, *refs):
        hn = _rms(x_ref[...], nw_ref[...]).astype(BF16)
        refs[k][...] = hn
        for w_ref, o_ref in zip(refs[:k], refs[k + 1:]):
            o_ref[...] = _dot(hn, w_ref[...])

    row = lambda i: (i, 0)
    fix = lambda i: (0, 0)
    return pl.pallas_call(
        body, name=name, grid=(n // tm,),
        in_specs=[pl.BlockSpec((tm, d), row), pl.BlockSpec((1, d), fix)] + [pl.BlockSpec(w.shape, fix) for w in ws],
        out_specs=[pl.BlockSpec((tm, d), row)] + [pl.BlockSpec((tm, w.shape[1]), row) for w in ws],
        out_shape=[SDS((n, d), BF16)] + [SDS((n, w.shape[1]), F32) for w in ws],
        compiler_params=_cp(("arbitrary",), VMEM_LIMIT),
    )(x, nw, *ws)


def _matmul_res(a, w, res, tm, name):
    n, kd = a.shape
    m = w.shape[1]

    def body(a_ref, w_ref, r_ref, o_ref):
        o_ref[...] = r_ref[...] + _dot(a_ref[...], w_ref[...])

    return pl.pallas_call(
        body, name=name, grid=(n // tm,),
        in_specs=[pl.BlockSpec((tm, kd), lambda i: (i, 0)), pl.BlockSpec((kd, m), lambda i: (0, 0)),
                  pl.BlockSpec((tm, m), lambda i: (i, 0))],
        out_specs=pl.BlockSpec((tm, m), lambda i: (i, 0)),
        out_shape=SDS((n, m), F32),
        compiler_params=_cp(("arbitrary",), VMEM_LIMIT),
    )(a, w, res)


def _matmul_nt(gs, ws, tm, name):
    n = gs[0].shape[0]
    kd = ws[0].shape[0]
    cnt = len(gs)

    def body(*refs):
        acc = None
        for g_ref, w_ref in zip(refs[:cnt], refs[cnt:2 * cnt]):
            t = _dot(g_ref[...].astype(BF16), w_ref[...], NT)
            acc = t if acc is None else acc + t
        refs[2 * cnt][...] = acc

    return pl.pallas_call(
        body, name=name, grid=(n // tm,),
        in_specs=[pl.BlockSpec((tm, g.shape[1]), lambda i: (i, 0)) for g in gs]
        + [pl.BlockSpec(w.shape, lambda i: (0, 0)) for w in ws],
        out_specs=pl.BlockSpec((tm, kd), lambda i: (i, 0)),
        out_shape=SDS((n, kd), F32),
        compiler_params=_cp(("arbitrary",), VMEM_LIMIT),
    )(*gs, *ws)


def _matmul_tn(a, g, tm, cb, name):
    n, kd = a.shape
    m = g.shape[1]

    def body(a_ref, g_ref, o_ref):
        t = _dot(a_ref[...], g_ref[...].astype(BF16), TN)

        @pl.when(pl.program_id(1) == 0)
        def _():
            o_ref[...] = t

        @pl.when(pl.program_id(1) != 0)
        def _():
            o_ref[...] += t

    return pl.pallas_call(
        body, name=name, grid=(m // cb, n // tm),
        in_specs=[pl.BlockSpec((tm, kd), lambda j, i: (i, 0)), pl.BlockSpec((tm, cb), lambda j, i: (i, j))],
        out_specs=pl.BlockSpec((kd, cb), lambda j, i: (0, j)),
        out_shape=SDS((kd, m), F32),
        compiler_params=_cp(("arbitrary", "arbitrary"), VMEM_LIMIT),
    )(a, g)


def _dwconv(x, w, b):
    kw = w.shape[0]
    acc = b
    for k in range(kw):
        acc = acc + w[k:k + 1, :] * _shift(x, kw // 2 - k, x.shape[0])
    return acc


def _conv_silu_fn(x, w, b):
    y = _dwconv(x, w, b)
    return y * _sigmoid(y)


def _conv_glu_fn(v, g, wv, wg, bv, bg):
    cv = _dwconv(v, wv, bv)
    cg = _dwconv(g, wg, bg)
    return cg * _sigmoid(cg) * cv


def _conv_silu(x, w, b, bl, cb, name):
    n, c = x.shape
    sl = n // bl
    kw = w.shape[0]

    def body(x_ref, w_ref, b_ref, o_ref):
        o_ref[...] = _conv_silu_fn(x_ref[...], w_ref[...], b_ref[...])

    return pl.pallas_call(
        body, name=name, grid=(bl, c // cb),
        in_specs=[pl.BlockSpec((sl, cb), lambda s, j: (s, j)), pl.BlockSpec((kw, cb), lambda s, j: (0, j)),
                  pl.BlockSpec((1, cb), lambda s, j: (0, j))],
        out_specs=pl.BlockSpec((sl, cb), lambda s, j: (s, j)),
        out_shape=SDS((n, c), F32),
        compiler_params=_cp(("arbitrary", "arbitrary"), VMEM_LIMIT),
    )(x, w, b)


def _conv_silu_bwd(x, w, b, dys, bl, cb, name):
    n, c = x.shape
    sl = n // bl
    kw = w.shape[0]
    cnt = len(dys)

    def body(x_ref, w_ref, b_ref, *refs):
        dy = refs[0][...]
        for r in refs[1:cnt]:
            dy = dy + r[...]
        dx_ref, dw_ref, db_ref = refs[cnt:]
        _, vjp = jax.vjp(_conv_silu_fn, x_ref[...], w_ref[...], b_ref[...])
        dx, dw, db = vjp(dy)
        dx_ref[...] = dx.astype(BF16)

        @pl.when(pl.program_id(1) == 0)
        def _():
            dw_ref[...] = dw
            db_ref[...] = db

        @pl.when(pl.program_id(1) != 0)
        def _():
            dw_ref[...] += dw
            db_ref[...] += db

    dy_specs = []
    for arr, lead in dys:
        if lead is None:
            dy_specs.append(pl.BlockSpec((sl, cb), lambda j, s: (s, j)))
        else:
            dy_specs.append(pl.BlockSpec((None, sl, cb), functools.partial(lambda j, s, lead: (lead, s, j), lead=lead)))
    return pl.pallas_call(
        body, name=name, grid=(c // cb, bl),
        in_specs=[pl.BlockSpec((sl, cb), lambda j, s: (s, j)), pl.BlockSpec((kw, cb), lambda j, s: (0, j)),
                  pl.BlockSpec((1, cb), lambda j, s: (0, j))] + dy_specs,
        out_specs=[pl.BlockSpec((sl, cb), lambda j, s: (s, j)), pl.BlockSpec((kw, cb), lambda j, s: (0, j)),
                   pl.BlockSpec((1, cb), lambda j, s: (0, j))],
        out_shape=[SDS((n, c), BF16), SDS((kw, c), F32), SDS((1, c), F32)],
        compiler_params=_cp(("arbitrary", "arbitrary"), VMEM_LIMIT),
    )(x, w, b, *[a for a, _ in dys])


def _conv_glu(v, g, wv, wg, bv, bg, bl, cb, name):
    n, c = v.shape
    sl = n // bl
    kw = wv.shape[0]

    def body(v_ref, g_ref, wv_ref, wg_ref, bv_ref, bg_ref, o_ref):
        o_ref[...] = _conv_glu_fn(v_ref[...], g_ref[...], wv_ref[...], wg_ref[...], bv_ref[...], bg_ref[...]).astype(BF16)

    big = pl.BlockSpec((sl, cb), lambda s, j: (s, j))
    wsp = pl.BlockSpec((kw, cb), lambda s, j: (0, j))
    bsp = pl.BlockSpec((1, cb), lambda s, j: (0, j))
    return pl.pallas_call(
        body, name=name, grid=(bl, c // cb),
        in_specs=[big, big, wsp, wsp, bsp, bsp], out_specs=big, out_shape=SDS((n, c), BF16),
        compiler_params=_cp(("arbitrary", "arbitrary"), VMEM_LIMIT),
    )(v, g, wv, wg, bv, bg)


def _conv_glu_bwd(v, g, wv, wg, bv, bg, dact, bl, cb, name):
    n, c = v.shape
    sl = n // bl
    kw = wv.shape[0]

    def body(v_ref, g_ref, wv_ref, wg_ref, bv_ref, bg_ref, da_ref, dv_ref, dg_ref, dwv_ref, dwg_ref, dbv_ref, dbg_ref):
        _, vjp = jax.vjp(_conv_glu_fn, v_ref[...], g_ref[...], wv_ref[...], wg_ref[...], bv_ref[...], bg_ref[...])
        dv, dg, dwv, dwg, dbv, dbg = vjp(da_ref[...])
        dv_ref[...] = dv.astype(BF16)
        dg_ref[...] = dg.astype(BF16)

        @pl.when(pl.program_id(1) == 0)
        def _():
            dwv_ref[...] = dwv
            dwg_ref[...] = dwg
            dbv_ref[...] = dbv
            dbg_ref[...] = dbg

        @pl.when(pl.program_id(1) != 0)
        def _():
            dwv_ref[...] += dwv
            dwg_ref[...] += dwg
            dbv_ref[...] += dbv
            dbg_ref[...] += dbg

    big = pl.BlockSpec((sl, cb), lambda j, s: (s, j))
    wsp = pl.BlockSpec((kw, cb), lambda j, s: (0, j))
    bsp = pl.BlockSpec((1, cb), lambda j, s: (0, j))
    return pl.pallas_call(
        body, name=name, grid=(c // cb, bl),
        in_specs=[big, big, wsp, wsp, bsp, bsp, big],
        out_specs=[big, big, wsp, wsp, bsp, bsp],
        out_shape=[SDS((n, c), BF16), SDS((n, c), BF16), SDS((kw, c), F32), SDS((kw, c), F32), SDS((1, c), F32), SDS((1, c), F32)],
        compiler_params=_cp(("arbitrary", "arbitrary"), VMEM_LIMIT),
    )(v, g, wv, wg, bv, bg, dact)


_DIMS_T = {NN: (NT, TN, False, False), NT: (NN, TN, False, True), TN: (NT, NN, True, False)}


@functools.partial(jax.custom_vjp, nondiff_argnums=(2,))
def _bdot(a, b, dims):
    return _dot(a.astype(BF16), b.astype(BF16), dims)


def _bdot_fwd(a, b, dims):
    return _bdot(a, b, dims), (a, b)


def _bdot_bwd(dims, res, g):
    a, b = res
    da_dims, db_dims, a_swapped, b_swapped = _DIMS_T[dims]
    da = _bdot(b, g, da_dims) if a_swapped else _bdot(g, b, da_dims)
    db = _bdot(g, a, db_dims) if b_swapped else _bdot(a, g, db_dims)
    return da, db


_bdot.defvjp(_bdot_fwd, _bdot_bwd)


@functools.partial(jax.custom_vjp, nondiff_argnums=(1,))
def _expand_heads(v, width):
    return _split_dot(v, _head_matrix(width), NN)


def _head_matrix(width):
    hr = lax.broadcasted_iota(jnp.int32, (LANES, SSD_HEADS * width), 0)
    hc = lax.broadcasted_iota(jnp.int32, (LANES, SSD_HEADS * width), 1)
    return (hc // width == hr).astype(BF16)


def _split_dot(v, e, dims):
    hi = v.astype(BF16)
    lo = (v - hi.astype(F32)).astype(BF16)
    return _dot(hi, e, dims) + _dot(lo, e, dims)


_expand_heads.defvjp(lambda v, width: (_expand_heads(v, width), None),
                     lambda width, _, g: (_split_dot(g, _head_matrix(width), NT),))


def _ssd_chunk_fn(rev, xs, dtr, bms, cms, st, alog, dtb):
    q = dtr.shape[0]
    hd, per = SSD_HEAD_DIM, SSD_HEADS // SSD_GROUPS
    gw = per * hd
    r = lax.broadcasted_iota(jnp.int32, (q, q), 0)
    c = lax.broadcasted_iota(jnp.int32, (q, q), 1)
    sgn = 1 - 2 * rev
    tri = ((r - c) * sgn >= 0).astype(F32)
    tri_t = ((c - r) * sgn >= 0).astype(F32)
    r4 = lax.broadcasted_iota(jnp.int32, (q, per * q), 0)
    c4 = lax.broadcasted_iota(jnp.int32, (q, per * q), 1) % q
    mask4 = (r4 - c4) * sgn >= 0
    bdr = lax.broadcasted_iota(jnp.int32, (per * q, gw), 0) // q
    bdc = lax.broadcasted_iota(jnp.int32, (per * q, gw), 1) // hd
    diag = bdr == bdc
    dt = _softplus(dtr + dtb)
    dta = dt * (-jnp.exp(alog))
    cs = _dot(tri, dta, NN, HI)
    cs_t = _dot(dta, tri_t, TN, HI)
    tot = jnp.sum(dta, axis=0, keepdims=True)
    dt_x = _expand_heads(dt, hd)
    in_x = _expand_heads(jnp.exp(cs), hd)
    out_x = _expand_heads(jnp.exp(tot - cs), hd)
    ys, outs = [], []
    for g in range(SSD_GROUPS):
        bg, cg = bms[g], cms[g]
        heads = range(per * g, per * (g + 1))
        lanes = slice(gw * g, gw * (g + 1))
        scores = _bdot(cg, bg, NT)
        col = jnp.concatenate([jnp.broadcast_to(cs[:, h:h + 1], (q, q)) for h in heads], axis=1)
        row = jnp.concatenate([cs_t[h:h + 1, :] for h in heads], axis=1)
        seg = jnp.where(mask4, jnp.exp(jnp.where(mask4, col - row, 0.0)), 0.0)
        mcat = jnp.concatenate([scores] * per, axis=1) * seg
        xdt = xs[g] * dt_x[:, lanes]
        blocks = jnp.where(diag, jnp.concatenate([xdt] * per, axis=0), 0.0)
        y = _bdot(mcat, blocks, NN) + in_x[:, lanes] * _bdot(cg, st[g], NT)
        new = _bdot(xdt * out_x[:, lanes], bg, TN)
        keep = jnp.concatenate([jnp.exp(tot[:, h:h + 1]) * st[g][hd * j:hd * (j + 1), :] for j, h in enumerate(heads)], axis=0)
        ys.append(y)
        outs.append(keep + new)
    return ys, outs


def _ssd_scan(xs, bm, cm, dtr, alog2, dtb2, bl, name):
    n = xs.shape[0]
    q = SSD_CHUNK
    nc = n // bl // q
    hd, ns = SSD_HEAD_DIM, SSD_STATE
    gw = SSD_HEADS // SSD_GROUPS * hd

    def body(xs_ref, b_ref, c_ref, dt_ref, al_ref, db_ref, y_ref, sv_ref, st_ref):
        d, i = pl.program_id(0), pl.program_id(2)

        @pl.when(i == 0)
        def _():
            st_ref[...] = jnp.zeros(st_ref.shape, F32)

        st = [st_ref[gw * g:gw * (g + 1), :] for g in range(SSD_GROUPS)]
        sv_ref[...] = st_ref[...]
        xl = [xs_ref[:, gw * g:gw * (g + 1)] for g in range(SSD_GROUPS)]
        bms = [b_ref[:, ns * g:ns * (g + 1)] for g in range(SSD_GROUPS)]
        cms = [c_ref[:, ns * g:ns * (g + 1)] for g in range(SSD_GROUPS)]
        ys, outs = _ssd_chunk_fn(d, xl, dt_ref[...], bms, cms, st, al_ref[...], db_ref[...])
        for g in range(SSD_GROUPS):
            st_ref[gw * g:gw * (g + 1), :] = outs[g]
            y_ref[:, gw * g:gw * (g + 1)] = ys[g]

    def rowblk(d, s, i):
        return s * nc + i + d * (nc - 1 - 2 * i)

    return pl.pallas_call(
        body, name=name, grid=(2, bl, nc),
        in_specs=[pl.BlockSpec((q, SSD_HEADS * hd), lambda d, s, i: (rowblk(d, s, i), 0)),
                  pl.BlockSpec((q, SSD_GROUPS * ns), lambda d, s, i: (rowblk(d, s, i), 0)),
                  pl.BlockSpec((q, SSD_GROUPS * ns), lambda d, s, i: (rowblk(d, s, i), 0)),
                  pl.BlockSpec((None, q, LANES), lambda d, s, i: (d, rowblk(d, s, i), 0)),
                  pl.BlockSpec((None, 1, LANES), lambda d, s, i: (d, 0, 0)),
                  pl.BlockSpec((None, 1, LANES), lambda d, s, i: (d, 0, 0))],
        out_specs=[pl.BlockSpec((None, q, SSD_HEADS * hd), lambda d, s, i: (d, rowblk(d, s, i), 0)),
                   pl.BlockSpec((None, None, SSD_HEADS * hd, ns), lambda d, s, i: (d, rowblk(d, s, i), 0, 0))],
        out_shape=[SDS((2, n, SSD_HEADS * hd), F32), SDS((2, n // q, SSD_HEADS * hd, ns), F32)],
        scratch_shapes=[pltpu.VMEM((SSD_HEADS * hd, ns), F32)],
        compiler_params=_cp(("arbitrary",) * 3, VMEM_LIMIT),
    )(xs, bm, cm, dtr, alog2, dtb2)


def _ssd_scan_bwd(xs, bm, cm, dtr, alog2, dtb2, saved, dy, bl, name):
    n = xs.shape[0]
    q = SSD_CHUNK
    nc = n // bl // q
    hd, ns = SSD_HEAD_DIM, SSD_STATE
    gw = SSD_HEADS // SSD_GROUPS * hd

    def body(xs_ref, b_ref, c_ref, dt_ref, al_ref, db_ref, sv_ref, dy_ref,
             dxs_ref, dbm_ref, dcm_ref, ddt_ref, dal_ref, ddb_ref, ds_ref):
        d, s, i = pl.program_id(0), pl.program_id(1), pl.program_id(2)

        @pl.when(i == 0)
        def _():
            ds_ref[...] = jnp.zeros(ds_ref.shape, F32)

        xl = [xs_ref[:, gw * g:gw * (g + 1)] for g in range(SSD_GROUPS)]
        bms = [b_ref[:, ns * g:ns * (g + 1)] for g in range(SSD_GROUPS)]
        cms = [c_ref[:, ns * g:ns * (g + 1)] for g in range(SSD_GROUPS)]
        st = [sv_ref[gw * g:gw * (g + 1), :] for g in range(SSD_GROUPS)]
        fn = functools.partial(_ssd_chunk_fn, d)
        _, vjp = jax.vjp(fn, xl, dt_ref[...], bms, cms, st, al_ref[...], db_ref[...])
        dys = [dy_ref[:, gw * g:gw * (g + 1)] for g in range(SSD_GROUPS)]
        dso = [ds_ref[gw * g:gw * (g + 1), :] for g in range(SSD_GROUPS)]
        dxl, ddt, dbg, dcg, dst, dal, ddb = vjp((dys, dso))
        for g in range(SSD_GROUPS):
            ds_ref[gw * g:gw * (g + 1), :] = dst[g]
            dxs_ref[:, gw * g:gw * (g + 1)] = dxl[g]
            dbm_ref[:, ns * g:ns * (g + 1)] = dbg[g]
            dcm_ref[:, ns * g:ns * (g + 1)] = dcg[g]
        ddt_ref[...] = ddt
        _acc_rows((dal_ref, ddb_ref), (dal, ddb), jnp.logical_and(s == 0, i == 0))

    def rowblk(d, s, i):
        return s * nc + (nc - 1 - i) + d * (2 * i - (nc - 1))

    row = lambda d, s, i: (rowblk(d, s, i), 0)
    drow = lambda d, s, i: (d, rowblk(d, s, i), 0)
    dfix = lambda d, s, i: (d, 0, 0)
    return pl.pallas_call(
        body, name=name, grid=(2, bl, nc),
        in_specs=[pl.BlockSpec((q, SSD_HEADS * hd), row), pl.BlockSpec((q, SSD_GROUPS * ns), row),
                  pl.BlockSpec((q, SSD_GROUPS * ns), row), pl.BlockSpec((None, q, LANES), drow),
                  pl.BlockSpec((None, 1, LANES), dfix), pl.BlockSpec((None, 1, LANES), dfix),
                  pl.BlockSpec((None, None, SSD_HEADS * hd, ns), lambda d, s, i: (d, rowblk(d, s, i), 0, 0)),
                  pl.BlockSpec((q, SSD_HEADS * hd), row)],
        out_specs=[pl.BlockSpec((None, q, SSD_HEADS * hd), drow), pl.BlockSpec((None, q, SSD_GROUPS * ns), drow),
                   pl.BlockSpec((None, q, SSD_GROUPS * ns), drow), pl.BlockSpec((None, q, LANES), drow),
                   pl.BlockSpec((None, 1, LANES), dfix), pl.BlockSpec((None, 1, LANES), dfix)],
        out_shape=[SDS((2, n, SSD_HEADS * hd), F32), SDS((2, n, SSD_GROUPS * ns), F32), SDS((2, n, SSD_GROUPS * ns), F32),
                   SDS((2, n, LANES), F32), SDS((2, 1, LANES), F32), SDS((2, 1, LANES), F32)],
        scratch_shapes=[pltpu.VMEM((SSD_HEADS * hd, ns), F32)],
        compiler_params=_cp(("arbitrary",) * 3, VMEM_LIMIT),
    )(xs, bm, cm, dtr, alog2, dtb2, saved, dy)


def _s5_consts():
    t, ch, p = S5_T, S5_CH, S5_STATE
    lane = lax.broadcasted_iota(jnp.int32, (1, 2 * p), 1)
    pr = lax.broadcasted_iota(jnp.int32, (p, 2 * p), 0)
    pc = lax.broadcasted_iota(jnp.int32, (p, 2 * p), 1)
    cr = lax.broadcasted_iota(jnp.int32, (ch, t * ch), 0)
    cc = lax.broadcasted_iota(jnp.int32, (ch, t * ch), 1)
    return dict(
        left=lane < p,
        sg=jnp.where(lane < p, -1.0, 1.0).astype(F32),
        dup=(pc % p == pr).astype(F32),
        dup_l=(pc == pr).astype(F32),
        dup_r=(pc == pr + p).astype(F32),
        rep=(cc % ch == cr).astype(F32),
        rep0=(cc == cr).astype(F32),
    )


def _s5_mats(k, rev, lr, li, ls, bre, bim, cre, cim):
    t = S5_T
    step = jnp.exp(ls)
    lr2 = jnp.sum(lr * k["dup"], axis=0, keepdims=True)
    li2 = jnp.sum(li * k["dup"], axis=0, keepdims=True)

    def erow(d):
        ang = (d * step) * li2
        return jnp.exp((d * step) * lr2) * jnp.where(k["left"], jnp.cos(ang), jnp.sin(ang))

    es = [erow(d) for d in range(t + 1)]
    mag = jnp.exp(step * lr)
    ar, ai = mag * jnp.cos(step * li), mag * jnp.sin(step * li)
    den = lr * lr + li * li
    zr = ((ar - 1.0) * lr + ai * li) / den
    zi = (ai * lr - (ar - 1.0) * li) / den
    bbr = zr * bre - zi * bim
    bbi = zr * bim + zi * bre
    bt1 = _dot(bbr, k["dup"], TN, HI)
    bt2 = _dot(bbi, k["dup"], TN, HI)
    bst = _dot(bbr, k["dup_l"], TN, HI) - _dot(bbi, k["dup_r"], TN, HI)
    c1 = _dot(cre, k["dup"], NN, HI)
    c2 = _dot(cim, k["dup"], NN, HI)
    sg = k["sg"]
    ce = [e * c1 + sg * _swap(e) * c2 for e in es]
    lags = range(t - 1, -1, -1) if rev else range(t)
    kt = _dot(bst, jnp.concatenate([ce[d] for d in lags], axis=0), NT, HI)
    toep = jnp.concatenate([_lane_shift(kt, -S5_CH * (t - 1 - s) if rev else S5_CH * s) for s in range(t)], axis=0)
    w_out =jnp.concatenate([ce[(t - qq) if rev else (qq + 1)] * (-sg) for qq in range(t)], axis=0)
    w_st = jnp.concatenate(
        [(lambda e: e * bt1 + sg * _swap(e) * bt2)(es[s if rev else (t - 1 - s)]) for s in range(t)], axis=0)
    return toep, w_out, w_st, es[t]


def _cmul_row(k, e, z):
    es = _swap(e)
    return z * jnp.where(k["left"], e, es) + k["sg"] * _swap(z) * jnp.where(k["left"], es, e)


def _s5_dir(k, rev, nck, x, mats):
    toep, w_out, w_st, a_t = mats
    acc = _dot(x, w_st)
    e = a_t
    kk = 1
    sign = -1 if rev else 1
    while kk < nck:
        acc = acc + _cmul_row(k, e, _shift(acc, sign * kk, nck))
        e = _cmul_row(k, e, e)
        kk *= 2
    prev = _shift(acc, sign, nck)
    return _dot(x, toep) + _dot(prev, w_out, NT)


def _s5_group_fn(nck, x, pf, pb, bre, bim, dcol, wv, wg, bv, bg):
    k = _s5_consts()
    t = S5_T
    y = x * jnp.sum(dcol * k["rep"], axis=0, keepdims=True)
    for rev, (lr, li, ls, cre, cim) in ((False, pf), (True, pb)):
        y = y + _s5_dir(k, rev, nck, x, _s5_mats(k, rev, lr, li, ls, bre, bim, cre, cim))
    gy = jax.nn.gelu(y)
    def kron_eye(w16):
        wide = _dot(w16, k["rep0"], NN, HI)
        return jnp.concatenate([_lane_shift(wide, S5_CH * qq) for qq in range(t)], axis=0)

    kv, kg = kron_eye(wv), kron_eye(wg)
    val =_dot(gy, kv) + jnp.sum(bv * k["rep"], axis=0, keepdims=True)
    gate = _dot(gy, kg) + jnp.sum(bg * k["rep"], axis=0, keepdims=True)
    return val * _sigmoid(gate)


def _s5_specs(r):
    p, ch = S5_STATE, S5_CH
    g3 = lambda i: (i, 0, 0)
    col = pl.BlockSpec((None, p, 1), g3)
    one = pl.BlockSpec((None, 1, 1), g3)
    cmat = pl.BlockSpec((None, ch, p), g3)
    bmat = pl.BlockSpec((None, p, ch), g3)
    ccol = pl.BlockSpec((None, ch, 1), g3)
    sq = pl.BlockSpec((None, ch, ch), g3)
    xs = pl.BlockSpec((None, r, S5_T * ch), g3)
    specs = [xs, col, col, one, cmat, cmat, col, col, one, cmat, cmat, bmat, bmat, ccol, sq, sq, ccol, ccol]
    return specs


def _s5_unpack(vals):
    x = vals[0]
    pf = tuple(vals[1:6])
    pb = tuple(vals[6:11])
    bre, bim, dcol, wv, wg, bv, bg = vals[11:18]
    return x, pf, pb, bre, bim, dcol, wv, wg, bv, bg


def _s5_fwd(args, nck, name):
    x = args[0]
    ng, r, w = x.shape

    def body(*refs):
        vals = [ref[...] for ref in refs[:18]]
        refs[18][...] = _s5_group_fn(nck, *_s5_unpack(vals))

    specs = _s5_specs(r)
    return pl.pallas_call(
        body, name=name, grid=(ng,), in_specs=specs, out_specs=specs[0], out_shape=SDS(x.shape, F32),
        compiler_params=_cp(("arbitrary",), VMEM_LIMIT),
    )(*args)


def _s5_bwd(args, dy, nck, name):
    x = args[0]
    ng, r, w = x.shape

    def body(*refs):
        vals = [ref[...] for ref in refs[:18]]
        _, vjp = jax.vjp(lambda *v: _s5_group_fn(nck, *_s5_unpack(v)), *vals)
        grads = vjp(refs[18][...])
        for o_ref, gval in zip(refs[19:], grads):
            o_ref[...] = gval.astype(o_ref.dtype)

    specs = _s5_specs(r)
    return pl.pallas_call(
        body, name=name, grid=(ng,), in_specs=specs + [specs[0]], out_specs=specs,
        out_shape=[SDS(x.shape, BF16)] + [SDS(a.shape, F32) for a in args[1:]],
        compiler_params=_cp(("arbitrary",), VMEM_LIMIT),
    )(*args, dy)


def _mix_fn(yf, yb, xs, z, s5o, dvec, nw_ssd, nw_s5):
    hr = lax.broadcasted_iota(jnp.int32, (LANES, SSD_HEADS * SSD_HEAD_DIM), 0)
    hc = lax.broadcasted_iota(jnp.int32, (LANES, SSD_HEADS * SSD_HEAD_DIM), 1)
    expand = (hc // SSD_HEAD_DIM == hr).astype(F32)
    dch = jnp.sum(dvec * expand, axis=0, keepdims=True)
    y = (yf + yb + dch * xs) * (z * _sigmoid(z))
    return _rms(y, nw_ssd), _rms(s5o, nw_s5)


def _mix(y2, xs, z, s5o, dvec, nw_ssd, nw_s5, tm, name):
    n, c1 = xs.shape
    c2 = s5o.shape[1]

    def body(yf_ref, yb_ref, xs_ref, z_ref, s_ref, d_ref, n1_ref, n2_ref, o_ref):
        o1, o2 = _mix_fn(yf_ref[...], yb_ref[...], xs_ref[...], z_ref[...], s_ref[...], d_ref[...], n1_ref[...], n2_ref[...])
        o_ref[:, :c1] = o1.astype(BF16)
        o_ref[:, c1:] = o2.astype(BF16)

    row = lambda i: (i, 0)
    fix = lambda i: (0, 0)
    return pl.pallas_call(
        body, name=name, grid=(n // tm,),
        in_specs=[pl.BlockSpec((None, tm, c1), lambda i: (0, i, 0)), pl.BlockSpec((None, tm, c1), lambda i: (1, i, 0)),
                  pl.BlockSpec((tm, c1), row), pl.BlockSpec((tm, c1), row), pl.BlockSpec((tm, c2), row),
                  pl.BlockSpec((LANES, 1), fix), pl.BlockSpec((1, c1), fix), pl.BlockSpec((1, c2), fix)],
        out_specs=pl.BlockSpec((tm, c1 + c2), row), out_shape=SDS((n, c1 + c2), BF16),
        compiler_params=_cp(("arbitrary",), VMEM_LIMIT),
    )(y2, y2, xs, z, s5o, dvec, nw_ssd, nw_s5)


def _acc_rows(refs, vals, first):
    @pl.when(first)
    def _():
        for ref, v in zip(refs, vals):
            ref[...] = v

    @pl.when(jnp.logical_not(first))
    def _():
        for ref, v in zip(refs, vals):
            ref[...] += v


def _mix_bwd(y2, xs, z, s5o, dvec, nw_ssd, nw_s5, dmix, tm, name):
    n, c1 = xs.shape
    c2 = s5o.shape[1]

    def body(yf_ref, yb_ref, xs_ref, z_ref, s_ref, d_ref, n1_ref, n2_ref, dm_ref,
             dy_ref, dxs_ref, dz_ref, ds_ref, dd_ref, dn1_ref, dn2_ref):
        _, vjp = jax.vjp(_mix_fn, yf_ref[...], yb_ref[...], xs_ref[...], z_ref[...], s_ref[...], d_ref[...], n1_ref[...], n2_ref[...])
        dyf, _, dxs, dz, ds, dd, dn1, dn2 = vjp((dm_ref[:, :c1], dm_ref[:, c1:]))
        dy_ref[...] = dyf
        dxs_ref[...] = dxs
        dz_ref[...] = dz.astype(BF16)
        ds_ref[...] = ds
        _acc_rows((dd_ref, dn1_ref, dn2_ref), (dd, dn1, dn2), pl.program_id(0) == 0)

    row = lambda i: (i, 0)
    fix = lambda i: (0, 0)
    return pl.pallas_call(
        body, name=name, grid=(n // tm,),
        in_specs=[pl.BlockSpec((None, tm, c1), lambda i: (0, i, 0)), pl.BlockSpec((None, tm, c1), lambda i: (1, i, 0)),
                  pl.BlockSpec((tm, c1), row), pl.BlockSpec((tm, c1), row), pl.BlockSpec((tm, c2), row),
                  pl.BlockSpec((LANES, 1), fix), pl.BlockSpec((1, c1), fix), pl.BlockSpec((1, c2), fix),
                  pl.BlockSpec((tm, c1 + c2), row)],
        out_specs=[pl.BlockSpec((tm, c1), row), pl.BlockSpec((tm, c1), row), pl.BlockSpec((tm, c1), row), pl.BlockSpec((tm, c2), row),
                   pl.BlockSpec((LANES, 1), fix), pl.BlockSpec((1, c1), fix), pl.BlockSpec((1, c2), fix)],
        out_shape=[SDS((n, c1), F32), SDS((n, c1), F32), SDS((n, c1), BF16), SDS((n, c2), F32),
                   SDS((LANES, 1), F32), SDS((1, c1), F32), SDS((1, c2), F32)],
        compiler_params=_cp(("arbitrary",), VMEM_LIMIT),
    )(y2, y2, xs, z, s5o, dvec, nw_ssd, nw_s5, dmix)


def _final_loss(h2, nw, tgt, tm, name):
    n, d = h2.shape

    def loss_fn(h, w, t):
        e = _rms(h, w) - t
        return (0.5 / d) * jnp.sum(e * e)

    def body(h_ref, w_ref, t_ref, l_ref, dh_ref, dw_ref):
        loss, (dh, dw) = jax.value_and_grad(loss_fn, argnums=(0, 1))(h_ref[...], w_ref[...], t_ref[...])
        dh_ref[...] = dh
        _acc_rows((l_ref, dw_ref), (jnp.full((1, LANES), loss, F32), dw), pl.program_id(0) == 0)

    row = lambda i: (i, 0)
    fix = lambda i: (0, 0)
    return pl.pallas_call(
        body, name=name, grid=(n // tm,),
        in_specs=[pl.BlockSpec((tm, d), row), pl.BlockSpec((1, d), fix), pl.BlockSpec((tm, d), row)],
        out_specs=[pl.BlockSpec((1, LANES), fix), pl.BlockSpec((tm, d), row), pl.BlockSpec((1, d), fix)],
        out_shape=[SDS((1, LANES), F32), SDS((n, d), F32), SDS((1, d), F32)],
        compiler_params=_cp(("arbitrary",), VMEM_LIMIT),
    )(h2, nw, tgt)


def _norm_bwd(x, nw, dhn, dres, tm, name):
    n, d = x.shape

    def body(x_ref, w_ref, g_ref, r_ref, dx_ref, dw_ref):
        _, vjp = jax.vjp(_rms, x_ref[...], w_ref[...])
        dx, dw = vjp(g_ref[...])
        dx_ref[...] = r_ref[...] + dx
        _acc_rows((dw_ref,), (dw,), pl.program_id(0) == 0)

    row = lambda i: (i, 0)
    fix = lambda i: (0, 0)
    return pl.pallas_call(
        body, name=name, grid=(n // tm,),
        in_specs=[pl.BlockSpec((tm, d), row), pl.BlockSpec((1, d), fix), pl.BlockSpec((tm, d), row), pl.BlockSpec((tm, d), row)],
        out_specs=[pl.BlockSpec((tm, d), row), pl.BlockSpec((1, d), fix)],
        out_shape=[SDS((n, d), F32), SDS((1, d), F32)],
        compiler_params=_cp(("arbitrary",), VMEM_LIMIT),
    )(x, nw, dhn, dres)


def _row_tile(n, cap=512):
    for t in range(min(cap, n) // 8 * 8, 7, -8):
        if n % t == 0:
            return t
    return n


def _sum_lead(a, name):
    kk, n, c = a.shape
    tm = _row_tile(n)

    def body(a_ref, o_ref):
        acc = a_ref[0].astype(F32)
        for i in range(1, kk):
            acc = acc + a_ref[i].astype(F32)
        o_ref[...] = acc

    return pl.pallas_call(
        body, name=name, grid=(n // tm,),
        in_specs=[pl.BlockSpec((kk, tm, c), lambda i: (0, i, 0))],
        out_specs=pl.BlockSpec((tm, c), lambda i: (i, 0)), out_shape=SDS((n, c), F32),
        compiler_params=_cp(("arbitrary",), VMEM_LIMIT),
    )(a)


def _adamw(w, g, m, v, name):
    n, c = w.shape
    tm = _row_tile(n)

    def body(w_ref, g_ref, m_ref, v_ref, d_ref, nm_ref, nv_ref):
        gv = g_ref[...]
        mn = ADAM_B1 * m_ref[...] + (1.0 - ADAM_B1) * gv
        vn = ADAM_B2 * v_ref[...] + (1.0 - ADAM_B2) * jnp.square(gv)
        m_hat = mn / (1.0 - ADAM_B1 ** ADAM_STEP)
        v_hat = vn / (1.0 - ADAM_B2 ** ADAM_STEP)
        d_ref[...] = -ADAM_LR * (m_hat / (jnp.sqrt(v_hat) + ADAM_EPS) + ADAM_WD * w_ref[...])
        nm_ref[...] = mn
        nv_ref[...] = vn

    spec = pl.BlockSpec((tm, c), lambda i: (i, 0))
    return pl.pallas_call(
        body, name=name, grid=(n // tm,), in_specs=[spec] * 4, out_specs=[spec] * 3,
        out_shape=[SDS((n, c), F32)] * 3, compiler_params=_cp(("arbitrary",), VMEM_LIMIT),
    )(w, g, m, v)


ANY = pl.BlockSpec(memory_space=pl.ANY)


def _me():
    return lax.axis_index("x"), lax.axis_index("y"), lax.axis_index("c")


def _gather_xy(split, whole, name):
    ns, cnt = len(split), len(split) + len(whole)

    def body(*refs):
        src, dst = refs[:cnt], refs[cnt:2 * cnt]
        send, recv = refs[2 * cnt:]
        x, y, c = _me()
        mine = 2 * x + y
        chips = [(1 - x, y), (x, 1 - y), (1 - x, 1 - y)]

        def ici(a, j, slot):
            px, py = chips[j]
            if a < ns:
                s_ref, d_ref = src[a].at[c], dst[a].at[slot].at[c]
            else:
                s_ref, d_ref = src[a], dst[a].at[slot]
            return pltpu.make_async_remote_copy(s_ref, d_ref, send.at[3 * a + j], recv.at[3 * a + j],
                                                device_id=(px, py, c), device_id_type=MESH)

        def d2d(a, j, half):
            px, py = chips[j]
            ref = dst[a].at[2 * px + py].at[half]
            return pltpu.make_async_remote_copy(ref, ref, send.at[3 * cnt + 3 * a + j], recv.at[3 * cnt + 3 * a + j],
                                                device_id=(x, y, 1 - c), device_id_type=MESH)

        def own(a):
            return pltpu.make_async_remote_copy(src[a], dst[a].at[mine], send.at[nsem - cnt + a], recv.at[nsem - cnt + a],
                                                device_id=(x, y, 1 - c), device_id_type=MESH)

        started = []
        for a in range(cnt):
            cp = own(a)
            cp.start()
            started.append(cp)
            for j in range(3):
                cp = ici(a, j, mine)
                cp.start()
                started.append(cp)
        for a in range(cnt):
            for j, (px, py) in enumerate(chips):
                ici(a, j, 2 * px + py).wait_recv()
                if a < ns:
                    cp = d2d(a, j, c)
                    cp.start()
                    started.append(cp)
        for a in range(ns):
            for j in range(3):
                d2d(a, j, 1 - c).wait_recv()
        for a in range(cnt):
            own(a).wait_recv()
        for cp in started:
            cp.wait_send()

    nsem = 3 * cnt + 3 * ns + cnt
    return pl.pallas_call(
        body, name=name, in_specs=[ANY] * cnt, out_specs=[ANY] * cnt,
        out_shape=[SDS((4,) + s.shape, s.dtype) for s in split + whole],
        scratch_shapes=[pltpu.SemaphoreType.DMA((nsem,)), pltpu.SemaphoreType.DMA((nsem,))],
    )(*split, *whole)


def _swap_sibling(parts, pick, name):
    cnt = len(parts)

    def body(*refs):
        src, dst = refs[:cnt], refs[cnt:2 * cnt]
        send, recv = refs[2 * cnt:]
        x, y, c = _me()
        cps = []
        for a in range(cnt):
            cp = pltpu.make_async_remote_copy(src[a].at[1 - c] if pick else src[a], dst[a], send.at[a], recv.at[a],
                                              device_id=(x, y, 1 - c), device_id_type=MESH)
            cp.start()
            cps.append(cp)
        for cp in cps:
            cp.wait()

    return pl.pallas_call(
        body, name=name, in_specs=[ANY] * cnt, out_specs=[ANY] * cnt,
        out_shape=[SDS(p.shape[1:] if pick else p.shape, p.dtype) for p in parts],
        scratch_shapes=[pltpu.SemaphoreType.DMA((cnt,)), pltpu.SemaphoreType.DMA((cnt,))],
    )(*parts)


def _scatter_xy(parts, name):
    cnt = len(parts)

    def body(*refs):
        src, dst = refs[:cnt], refs[cnt:2 * cnt]
        send, recv, loc = refs[2 * cnt:]
        x, y, c = _me()
        mine = 2 * x + y
        chips = [(1 - x, y), (x, 1 - y), (1 - x, 1 - y)]
        local = []
        for a in range(cnt):
            cp = pltpu.make_async_copy(src[a].at[mine], dst[a].at[mine], loc.at[a])
            cp.start()
            local.append(cp)
        sends = []
        for a in range(cnt):
            for j, (px, py) in enumerate(chips):
                cp = pltpu.make_async_remote_copy(src[a].at[2 * px + py], dst[a].at[mine], send.at[3 * a + j], recv.at[3 * a + j],
                                                  device_id=(px, py, c), device_id_type=MESH)
                cp.start()
                sends.append(cp)
        for a in range(cnt):
            for j, (px, py) in enumerate(chips):
                pltpu.make_async_remote_copy(src[a].at[mine], dst[a].at[2 * px + py], send.at[3 * a + j], recv.at[3 * a + j],
                                             device_id=(px, py, c), device_id_type=MESH).wait_recv()
        for cp in sends:
            cp.wait_send()
        for cp in local:
            cp.wait()

    return pl.pallas_call(
        body, name=name, in_specs=[ANY] * cnt, out_specs=[ANY] * cnt,
        out_shape=[SDS(p.shape, p.dtype) for p in parts],
        scratch_shapes=[pltpu.SemaphoreType.DMA((3 * cnt,)), pltpu.SemaphoreType.DMA((3 * cnt,)), pltpu.SemaphoreType.DMA((cnt,))],
    )(*parts)


def _bcast_all(buf, name):
    def body(src, dst, send, recv, loc):
        x, y, c = _me()
        mine = 4 * x + 2 * y + c
        own = pltpu.make_async_copy(src, dst.at[mine], loc)
        own.start()
        sends = []
        for k in range(1, 8):
            px, py, pc = x ^ (k >> 2), y ^ ((k >> 1) & 1), c ^ (k & 1)
            cp = pltpu.make_async_remote_copy(src, dst.at[mine], send.at[k - 1], recv.at[k - 1],
                                              device_id=(px, py, pc), device_id_type=MESH)
            cp.start()
            sends.append(cp)
        for k in range(1, 8):
            px, py, pc = x ^ (k >> 2), y ^ ((k >> 1) & 1), c ^ (k & 1)
            pltpu.make_async_remote_copy(src, dst.at[4 * px + 2 * py + pc], send.at[k - 1], recv.at[k - 1],
                                         device_id=(px, py, pc), device_id_type=MESH).wait_recv()
        for cp in sends:
            cp.wait_send()
        own.wait()

    return pl.pallas_call(
        body, name=name, in_specs=[ANY], out_specs=ANY, out_shape=SDS((8,) + buf.shape, buf.dtype),
        scratch_shapes=[pltpu.SemaphoreType.DMA((7,)), pltpu.SemaphoreType.DMA((7,)), pltpu.SemaphoreType.DMA(())],
    )(buf)


def _add2(a, b, dtype, name):
    shp = a.shape
    a2, b2 = a.reshape(-1, shp[-1]), b.reshape(-1, shp[-1])
    n, c = a2.shape
    tm = _row_tile(n, 256)

    def body(a_ref, b_ref, o_ref):
        o_ref[...] = (a_ref[...] + b_ref[...]).astype(dtype)

    spec = pl.BlockSpec((tm, c), lambda i: (i, 0))
    return pl.pallas_call(body, name=name, grid=(n // tm,), in_specs=[spec, spec], out_specs=spec,
                          out_shape=SDS((n, c), dtype), compiler_params=_cp(("arbitrary",), VMEM_LIMIT))(a2, b2).reshape(shp)


def _x_layout(u, bl):
    n, c = u.shape
    nck = n // bl // S5_T
    return u.reshape(bl, nck, S5_T, S5_GROUPS, S5_CH).transpose(3, 0, 1, 2, 4).reshape(S5_GROUPS, bl * nck, S5_T * S5_CH)


def _token_layout(xg, bl):
    ng, r, w = xg.shape
    nck = r // bl
    return xg.reshape(ng, bl, nck, S5_T, S5_CH).transpose(1, 2, 3, 0, 4).reshape(bl * nck * S5_T, ng * S5_CH)


def _pad_lanes(a, lanes=LANES):
    return jnp.pad(a, ((0, 0), (0, lanes - a.shape[1])))


def _local_step(x, tgt, p, bl):
    n, d = x.shape
    sw = SSD_HEADS * SSD_HEAD_DIM
    gn = SSD_GROUPS * SSD_STATE
    tm = min(n, 512)
    tm_ffn = min(n, 256)
    nck = n // bl // S5_T
    dff = p["w_down"].shape[0]
    s5w = S5_GROUPS * S5_CH

    w_in = p["w_in"]
    o1, o2, o3, o4 = sw, sw + sw, sw + sw + gn, sw + sw + 2 * gn
    w_z, w_xs, w_b, w_c = w_in[:, :o1], w_in[:, o1:o2], w_in[:, o2:o3], w_in[:, o3:o4]
    w_dt = _pad_lanes(w_in[:, o4:o4 + 2 * SSD_HEADS])
    w_u = w_in[:, o4 + 2 * SSD_HEADS:]
    in_ws = [w_z, w_xs, w_b, w_c, w_dt, w_u]
    cw, cb_ = p["ssd_conv_w"], p["ssd_conv_b"]
    conv_parts = [(cw[:, :sw], cb_[:, :sw]), (cw[:, sw:sw + gn], cb_[:, sw:sw + gn]), (cw[:, sw + gn:], cb_[:, sw + gn:])]
    alog2 = jnp.stack([_pad_lanes(p["ssd_a_log_fwd"]), _pad_lanes(p["ssd_a_log_bwd"])])
    dtb2 = jnp.stack([_pad_lanes(p["ssd_dt_bias_fwd"]), _pad_lanes(p["ssd_dt_bias_bwd"])])
    dvec = _pad_lanes(p["ssd_d"]).reshape(LANES, 1)

    hn, z, xs_pre, b_pre, c_pre, dtr, u = _norm_matmul(x, p["norm_mix_w"], in_ws, tm, "in_proj")
    pres = [xs_pre, b_pre, c_pre]
    acts = [_conv_silu(pre, w, b, bl, min(256, pre.shape[1]), f"ssd_conv_{i}") for i, (pre, (w, b)) in enumerate(zip(pres, conv_parts))]
    xs_a, b_a, c_a = acts
    dtr2 = jnp.stack([dtr, _pad_lanes(dtr[:, SSD_HEADS:2 * SSD_HEADS])])
    y2, saved = _ssd_scan(xs_a, b_a, c_a, dtr2, alog2, dtb2, bl, "ssd_scan")

    def col(a):
        return a.reshape(a.shape + (1,))

    s5_params = [
        col(p["s5_lambda_re_fwd"]), col(p["s5_lambda_im_fwd"]), p["s5_log_step_fwd"].reshape(S5_GROUPS, 1, 1), p["s5_c_re_fwd"], p["s5_c_im_fwd"],
        col(p["s5_lambda_re_bwd"]), col(p["s5_lambda_im_bwd"]), p["s5_log_step_bwd"].reshape(S5_GROUPS, 1, 1), p["s5_c_re_bwd"], p["s5_c_im_bwd"],
        p["s5_b_re"], p["s5_b_im"], col(p["s5_d"].reshape(S5_GROUPS, S5_CH)),
        p["s5_glu_w"][:, :, :S5_CH], p["s5_glu_w"][:, :, S5_CH:], col(p["s5_glu_b"][:, :S5_CH]), col(p["s5_glu_b"][:, S5_CH:]),
    ]
    s5_args = [_x_layout(u, bl)] + s5_params
    s5o = _token_layout(_s5_fwd(s5_args, nck, "s5_fwd"), bl)
    ymix = _mix(y2, xs_a, z, s5o, dvec, p["ssd_norm_w"], p["s5_norm_w"], tm, "mix")
    h1 = _matmul_res(ymix, p["w_out"], x, tm, "out_proj")
    w_up = p["w_up"]
    hn2, up_v, up_g = _norm_matmul(h1, p["norm_ffn_w"], [w_up[:, :dff], w_up[:, dff:]], tm_ffn, "ffn_up")
    fw, fb = p["ffn_conv_w"], p["ffn_conv_b"]
    act = _conv_glu(up_v, up_g, fw[:, :dff], fw[:, dff:], fb[:, :dff], fb[:, dff:], bl, 256, "ffn_conv")
    h2 = _matmul_res(act, p["w_down"], h1, tm, "ffn_down")
    loss, dh2, g_nfw = _final_loss(h2, p["norm_final_w"].reshape(1, d), tgt, tm, "final_loss")

    g = {"norm_final_w": g_nfw.reshape(d)}
    g["w_down"] = _matmul_tn(act, dh2, tm, d, "ffn_down_dw")
    dact = _matmul_nt([dh2], [p["w_down"]], tm, "ffn_down_dx")
    dup_v, dup_g, dwv, dwg, dbv, dbg = _conv_glu_bwd(up_v, up_g, fw[:, :dff], fw[:, dff:], fb[:, :dff], fb[:, dff:], dact, bl, 256, "ffn_conv_bwd")
    g["ffn_conv_w"] = jnp.concatenate([dwv, dwg], 1)
    g["ffn_conv_b"] = jnp.concatenate([dbv, dbg], 1)
    g["w_up"] = jnp.concatenate([_matmul_tn(hn2, dup_v, tm, dff // 2, "ffn_up_dw_v"), _matmul_tn(hn2, dup_g, tm, dff // 2, "ffn_up_dw_g")], 1)
    dhn2 = _matmul_nt([dup_v, dup_g], [w_up[:, :dff], w_up[:, dff:]], tm_ffn, "ffn_up_dx")
    dh1, g["norm_ffn_w"] = _norm_bwd(h1, p["norm_ffn_w"], dhn2, dh2, tm, "ffn_norm_bwd")
    g["w_out"] = _matmul_tn(ymix, dh1, tm, d, "out_proj_dw")
    dmix = _matmul_nt([dh1], [p["w_out"]], tm, "out_proj_dx")
    dyssd, dxs_gate, dz, ds5o, g_d, g["ssd_norm_w"], g["s5_norm_w"] = _mix_bwd(
        y2, xs_a, z, s5o, dvec, p["ssd_norm_w"], p["s5_norm_w"], dmix, tm, "mix_bwd")
    g["ssd_d"] = g_d[:SSD_HEADS].reshape(1, SSD_HEADS)
    s5g = _s5_bwd(s5_args, _x_layout(ds5o, bl), nck, "s5_bwd")
    du = _token_layout(s5g[0], bl)
    (g["s5_lambda_re_fwd"], g["s5_lambda_im_fwd"], g["s5_log_step_fwd"], g["s5_c_re_fwd"], g["s5_c_im_fwd"],
     g["s5_lambda_re_bwd"], g["s5_lambda_im_bwd"], g["s5_log_step_bwd"], g["s5_c_re_bwd"], g["s5_c_im_bwd"],
     g["s5_b_re"], g["s5_b_im"], g_s5d, g_wv, g_wg, g_bv, g_bg) = s5g[1:]
    for k_ in ("s5_lambda_re_fwd", "s5_lambda_im_fwd", "s5_lambda_re_bwd", "s5_lambda_im_bwd"):
        g[k_] = g[k_].reshape(S5_GROUPS, S5_STATE)
    for k_ in ("s5_log_step_fwd", "s5_log_step_bwd"):
        g[k_] = g[k_].reshape(S5_GROUPS)
    g["s5_d"] = g_s5d.reshape(1, s5w)
    g["s5_glu_w"] = jnp.concatenate([g_wv, g_wg], 2)
    g["s5_glu_b"] = jnp.concatenate([g_bv.reshape(S5_GROUPS, S5_CH), g_bg.reshape(S5_GROUPS, S5_CH)], 1)
    dxs2, dbm2, dcm2, ddt2, dal2, ddb2 = _ssd_scan_bwd(xs_a, b_a, c_a, dtr2, alog2, dtb2, saved, dyssd, bl, "ssd_scan_bwd")
    g["ssd_a_log_fwd"], g["ssd_a_log_bwd"] = dal2[0, :, :SSD_HEADS], dal2[1, :, :SSD_HEADS]
    g["ssd_dt_bias_fwd"], g["ssd_dt_bias_bwd"] = ddb2[0, :, :SSD_HEADS], ddb2[1, :, :SSD_HEADS]
    cots = [[(dxs2, 0), (dxs2, 1), (dxs_gate, None)], [(dbm2, 0), (dbm2, 1)], [(dcm2, 0), (dcm2, 1)]]
    dpres, dcw, dcb = [], [], []
    for i, (pre, (w, b), cot) in enumerate(zip(pres, conv_parts, cots)):
        dp, dw_, db_ = _conv_silu_bwd(pre, w, b, cot, bl, min(256, pre.shape[1]), f"ssd_conv_bwd_{i}")
        dpres.append(dp)
        dcw.append(dw_)
        dcb.append(db_)
    g["ssd_conv_w"] = jnp.concatenate(dcw, 1)
    g["ssd_conv_b"] = jnp.concatenate(dcb, 1)
    ddtr = _pad_lanes(jnp.concatenate([ddt2[0][:, :SSD_HEADS], ddt2[1][:, :SSD_HEADS]], 1))
    dprojs = [dz, dpres[0], dpres[1], dpres[2], ddtr, du]
    dws = [_matmul_tn(hn, dpj, tm, dpj.shape[1], f"in_proj_dw_{i}") for i, dpj in enumerate(dprojs)]
    dws[4] = dws[4][:, :2 * SSD_HEADS]
    g["w_in"] = jnp.concatenate(dws, 1)
    dhn = _matmul_nt(dprojs, in_ws, tm, "in_proj_dx")
    grad_x, g["norm_mix_w"] = _norm_bwd(x, p["norm_mix_w"], dhn, dh1, tm, "mix_norm_bwd")
    return loss, grad_x, g


_WEIGHTS = ['norm_mix_w', 'w_in', 'ssd_conv_w', 'ssd_conv_b', 'ssd_dt_bias_fwd', 'ssd_dt_bias_bwd', 'ssd_a_log_fwd', 'ssd_a_log_bwd',
            'ssd_d', 'ssd_norm_w', 's5_lambda_re_fwd', 's5_lambda_im_fwd', 's5_log_step_fwd', 's5_lambda_re_bwd', 's5_lambda_im_bwd',
            's5_log_step_bwd', 's5_b_re', 's5_b_im', 's5_c_re_fwd', 's5_c_im_fwd', 's5_c_re_bwd', 's5_c_im_bwd', 's5_d', 's5_glu_w',
            's5_glu_b', 's5_norm_w', 'w_out', 'norm_ffn_w', 'ffn_w_up', 'ffn_conv_w', 'ffn_conv_b', 'ffn_w_down', 'norm_final_w']
_BIG = ('w_in', 'w_out', 'ffn_w_up', 'ffn_w_down')
_CONV = ('ssd_conv_w', 'ffn_conv_w')


def _pack(arrs):
    flat = jnp.concatenate([a.reshape(-1) for a in arrs])
    rows = -(-flat.shape[0] // (64 * LANES)) * 64
    return jnp.pad(flat, (0, rows * LANES - flat.shape[0])).reshape(rows, LANES)


def _unpack(buf, shapes):
    flat = buf.reshape(-1)
    out, off = [], 0
    for shp in shapes:
        size = math.prod(shp)
        out.append(flat[off:off + size].reshape(shp))
        off += size
    return out


def kernel(x, norm_mix_w, w_in, ssd_conv_w, ssd_conv_b, ssd_dt_bias_fwd, ssd_dt_bias_bwd, ssd_a_log_fwd, ssd_a_log_bwd, ssd_d, ssd_norm_w, s5_lambda_re_fwd, s5_lambda_im_fwd, s5_log_step_fwd, s5_lambda_re_bwd, s5_lambda_im_bwd, s5_log_step_bwd, s5_b_re, s5_b_im, s5_c_re_fwd, s5_c_im_fwd, s5_c_re_bwd, s5_c_im_bwd, s5_d, s5_glu_w, s5_glu_b, s5_norm_w, w_out, norm_ffn_w, ffn_w_up, ffn_conv_w, ffn_conv_b, ffn_w_down, norm_final_w, loss_target, m_norm_mix_w, m_w_in, m_ssd_conv_w, m_ssd_conv_b, m_ssd_dt_bias_fwd, m_ssd_dt_bias_bwd, m_ssd_a_log_fwd, m_ssd_a_log_bwd, m_ssd_d, m_ssd_norm_w, m_s5_lambda_re_fwd, m_s5_lambda_im_fwd, m_s5_log_step_fwd, m_s5_lambda_re_bwd, m_s5_lambda_im_bwd, m_s5_log_step_bwd, m_s5_b_re, m_s5_b_im, m_s5_c_re_fwd, m_s5_c_im_fwd, m_s5_c_re_bwd, m_s5_c_im_bwd, m_s5_d, m_s5_glu_w, m_s5_glu_b, m_s5_norm_w, m_w_out, m_norm_ffn_w, m_ffn_w_up, m_ffn_conv_w, m_ffn_conv_b, m_ffn_w_down, m_norm_final_w, v_norm_mix_w, v_w_in, v_ssd_conv_w, v_ssd_conv_b, v_ssd_dt_bias_fwd, v_ssd_dt_bias_bwd, v_ssd_a_log_fwd, v_ssd_a_log_bwd, v_ssd_d, v_ssd_norm_w, v_s5_lambda_re_fwd, v_s5_lambda_im_fwd, v_s5_log_step_fwd, v_s5_lambda_re_bwd, v_s5_lambda_im_bwd, v_s5_log_step_bwd, v_s5_b_re, v_s5_b_im, v_s5_c_re_fwd, v_s5_c_im_fwd, v_s5_c_re_bwd, v_s5_c_im_bwd, v_s5_d, v_s5_glu_w, v_s5_glu_b, v_s5_norm_w, v_w_out, v_norm_ffn_w, v_ffn_w_up, v_ffn_conv_w, v_ffn_conv_b, v_ffn_w_down, v_norm_final_w):
    args = dict(locals())
    w = {k_: args[k_] for k_ in _WEIGHTS}
    m = {k_: args["m_" + k_] for k_ in _WEIGHTS}
    v = {k_: args["v_" + k_] for k_ in _WEIGHTS}
    bl, sl, d = x.shape
    chip = 2 * lax.axis_index("x") + lax.axis_index("y")
    core = lax.axis_index("c")

    shards = [w[k_][0].astype(BF16) for k_ in _BIG] + [w[k_][0] for k_ in _CONV]
    split = [s.reshape(2, s.shape[0] // 2, s.shape[1]) for s in shards[:len(_BIG)]]
    gathered = _gather_xy(split, shards[len(_BIG):], "gather_weights")
    g_in, g_out, g_up, g_down, g_scw, g_fcw = [got.reshape((4,) + s.shape) for got, s in zip(gathered, shards)]

    def cols(a):
        return jnp.moveaxis(a, 0, 1).reshape(a.shape[1], 4 * a.shape[2])

    p = {k_: (w[k_][0] if w[k_].ndim >= 3 else w[k_]) for k_ in _WEIGHTS if k_ not in _BIG + _CONV}
    p["w_in"], p["w_up"] = cols(g_in), cols(g_up)
    p["w_out"], p["w_down"] = g_out.reshape(-1, g_out.shape[2]), g_down.reshape(-1, g_down.shape[2])
    p["ssd_conv_w"], p["ffn_conv_w"] = cols(g_scw), cols(g_fcw)

    loss, grad_x, g = _local_step(x.reshape(bl * sl, d), loss_target.reshape(bl * sl, d), p, bl)
    g["ffn_w_up"], g["ffn_w_down"] = g.pop("w_up"), g.pop("w_down")

    def owner_major(a, k_):
        r, c = w[k_].shape[1:]
        if a.shape[0] == r:
            a = jnp.moveaxis(a.reshape(r, 4, c), 1, 0)
        else:
            a = a.reshape(4, r, c)
        return a.reshape(4, 2, r // 2, c)

    small = [k_ for k_ in _WEIGHTS if k_ not in _BIG]
    small_full_shapes = [g[k_].shape for k_ in small]
    buf = _pack([g[k_] for k_ in small] + [loss[0, :1]])
    parts = [jnp.moveaxis(owner_major(g[k_], k_), 1, 0) for k_ in _BIG]
    parts.append(jnp.moveaxis(buf.reshape(4, 2, -1, LANES), 1, 0))
    got = _swap_sibling(parts, True, "reduce_sibling")
    mine = [lax.dynamic_index_in_dim(pt, core, 0, keepdims=False) for pt in parts]
    chip_sums = [_add2(a, b, BF16 if i < len(_BIG) else F32, f"reduce_add_{i}") for i, (a, b) in enumerate(zip(mine, got))]
    from_chips = _scatter_xy(chip_sums, "reduce_chips")
    halves = [_sum_lead(a.reshape(4, -1, a.shape[-1]), f"reduce_sum_{i}") for i, a in enumerate(from_chips)]
    other = _swap_sibling(halves[:-1], False, "reduce_join")
    big_grad = {}
    for k_, own_half, sib_half in zip(_BIG, halves, other):
        south = core == 0
        full = jnp.stack([jnp.where(south, own_half, sib_half), jnp.where(south, sib_half, own_half)])
        big_grad[k_] = full.reshape((1,) + w[k_].shape[1:])
    tot = _bcast_all(halves[-1], "reduce_small").reshape(buf.shape)
    unp = _unpack(tot, small_full_shapes + [(1,)])
    small_grad = dict(zip(small, unp[:-1]))
    loss_out = unp[-1].reshape(())
    for k_ in _CONV:
        cshard = w[k_].shape[2]
        small_grad[k_] = lax.dynamic_slice_in_dim(small_grad[k_], chip * cshard, cshard, 1)

    grads, deltas, new_m, new_v = {}, {}, {}, {}
    for k_ in _BIG:
        shp = w[k_].shape
        grads[k_] = big_grad[k_]
        dl, nm, nv = _adamw(w[k_][0], big_grad[k_][0], m[k_][0], v[k_][0], f"adamw_{k_}")
        deltas[k_], new_m[k_], new_v[k_] = dl.reshape(shp), nm.reshape(shp), nv.reshape(shp)
    sw_ = _pack([w[k_] for k_ in small])
    sg_ = _pack([small_grad[k_] for k_ in small])
    sm_ = _pack([m[k_] for k_ in small])
    sv_ = _pack([v[k_] for k_ in small])
    dl, nm, nv = _adamw(sw_, sg_, sm_, sv_, "adamw_small")
    shapes = [w[k_].shape for k_ in small]
    for k_, a, b, c_ in zip(small, _unpack(dl, shapes), _unpack(nm, shapes), _unpack(nv, shapes)):
        deltas[k_], new_m[k_], new_v[k_] = a, b, c_
        grads[k_] = small_grad[k_].reshape(w[k_].shape)
    return (loss_out, grad_x.reshape(bl, sl, d), *[grads[k_] for k_ in _WEIGHTS], *[deltas[k_] for k_ in _WEIGHTS],
            *[new_m[k_] for k_ in _WEIGHTS], *[new_v[k_] for k_ in _WEIGHTS])
```

```python
import functools
import math

import jax
import jax.numpy as jnp
from jax import lax
from jax.experimental import pallas as pl
from jax.experimental.pallas import tpu as pltpu

F32 = jnp.float32
BF16 = jnp.bfloat16
HI = lax.Precision.HIGHEST
SDS = jax.ShapeDtypeStruct
MESH = pl.DeviceIdType.MESH

NN = (((1,), (0,)), ((), ()))
NT = (((1,), (1,)), ((), ()))
TN = (((0,), (0,)), ((), ()))

EPS = 1e-6
SSD_HEADS = 16
SSD_HEAD_DIM = 64
SSD_GROUPS = 4
SSD_STATE = 128
SSD_CHUNK = 128
SSD_CONV = 5
S5_GROUPS = 32
S5_CH = 16
S5_STATE = 64
S5_T = 16
LANES = 128
ADAM_LR, ADAM_B1, ADAM_B2, ADAM_EPS, ADAM_WD, ADAM_STEP = 0.001, 0.9, 0.999, 1e-08, 0.01, 10
V7X_VMEM_BYTES = 64 * 1024 * 1024
VMEM_LIMIT = V7X_VMEM_BYTES - 8 * 1024 * 1024


def _cp(sem, vmem=None):
    return pltpu.CompilerParams(dimension_semantics=sem, vmem_limit_bytes=vmem)


def _dot(a, b, dims=NN, precision=None):
    return lax.dot_general(a, b, dims, precision=precision, preferred_element_type=F32)


def _rms(x, w):
    return x * lax.rsqrt(jnp.mean(x * x, axis=-1, keepdims=True) + EPS) * w


def _sigmoid(x):
    return 1.0 / (1.0 + jnp.exp(-x))


def _softplus(x):
    return jnp.maximum(x, 0.0) + jnp.log1p(jnp.exp(-jnp.abs(x)))


@functools.partial(jax.custom_vjp, nondiff_argnums=(1, 2))
def _shift(x, k, seg):
    n = x.shape[0]
    r = lax.broadcasted_iota(jnp.int32, x.shape, 0) % seg
    y = pltpu.roll(x, k % n, 0)
    ok = (r >= k) if k > 0 else (r < seg + k)
    return jnp.where(ok, y, 0.0)


def _shift_fwd(x, k, seg):
    return _shift(x, k, seg), None


def _shift_bwd(k, seg, _, g):
    return (_shift(g, -k, seg),)


_shift.defvjp(_shift_fwd, _shift_bwd)


@functools.partial(jax.custom_vjp, nondiff_argnums=(1,))
def _lane_shift(x, k):
    if k == 0:
        return x
    n = x.shape[1]
    lane = lax.broadcasted_iota(jnp.int32, x.shape, 1)
    ok = (lane >= k) if k > 0 else (lane < n + k)
    return jnp.where(ok, pltpu.roll(x, k % n, 1), 0.0)


_lane_shift.defvjp(lambda x, k: (_lane_shift(x, k), None), lambda k, _, g: (_lane_shift(g, -k),))


@jax.custom_vjp
def _swap(z):
    return pltpu.roll(z, LANES // 2, 1)


_swap.defvjp(lambda z: (_swap(z), None), lambda _, g: (_swap(g),))


def _norm_matmul(x, nw, ws, tm, name):
    n, d = x.shape
    k = len(ws)

    def body(x_ref, nw_ref, *refs):
        hn = _rms(x_ref[...], nw_ref[...]).astype(BF16)
        refs[k][...] = hn
        for w_ref, o_ref in zip(refs[:k], refs[k + 1:]):
            o_ref[...] = _dot(hn, w_ref[...])

    row = lambda i: (i, 0)
    fix = lambda i: (0, 0)
    return pl.pallas_call(
        body, name=name, grid=(n // tm,),
        in_specs=[pl.BlockSpec((tm, d), row), pl.BlockSpec((1, d), fix)] + [pl.BlockSpec(w.shape, fix) for w in ws],
        out_specs=[pl.BlockSpec((tm, d), row)] + [pl.BlockSpec((tm, w.shape[1]), row) for w in ws],
        out_shape=[SDS((n, d), BF16)] + [SDS((n, w.shape[1]), F32) for w in ws],
        compiler_params=_cp(("arbitrary",), VMEM_LIMIT),
    )(x, nw, *ws)


def _matmul_res(a, w, res, tm, name):
    n, kd = a.shape
    m = w.shape[1]

    def body(a_ref, w_ref, r_ref, o_ref):
        o_ref[...] = r_ref[...] + _dot(a_ref[...], w_ref[...])

    return pl.pallas_call(
        body, name=name, grid=(n // tm,),
        in_specs=[pl.BlockSpec((tm, kd), lambda i: (i, 0)), pl.BlockSpec((kd, m), lambda i: (0, 0)),
                  pl.BlockSpec((tm, m), lambda i: (i, 0))],
        out_specs=pl.BlockSpec((tm, m), lambda i: (i, 0)),
        out_shape=SDS((n, m), F32),
        compiler_params=_cp(("arbitrary",), VMEM_LIMIT),
    )(a, w, res)


def _matmul_nt(gs, ws, tm, name):
    n = gs[0].shape[0]
    kd = ws[0].shape[0]
    cnt = len(gs)

    def body(*refs):
        acc = None
        for g_ref, w_ref in zip(refs[:cnt], refs[cnt:2 * cnt]):
            t = _dot(g_ref[...].astype(BF16), w_ref[...], NT)
            acc = t if acc is None else acc + t
        refs[2 * cnt][...] = acc

    return pl.pallas_call(
        body, name=name, grid=(n // tm,),
        in_specs=[pl.BlockSpec((tm, g.shape[1]), lambda i: (i, 0)) for g in gs]
        + [pl.BlockSpec(w.shape, lambda i: (0, 0)) for w in ws],
        out_specs=pl.BlockSpec((tm, kd), lambda i: (i, 0)),
        out_shape=SDS((n, kd), F32),
        compiler_params=_cp(("arbitrary",), VMEM_LIMIT),
    )(*gs, *ws)


def _matmul_tn(a, g, tm, cb, name):
    n, kd = a.shape
    m = g.shape[1]

    def body(a_ref, g_ref, o_ref):
        t = _dot(a_ref[...], g_ref[...].astype(BF16), TN)

        @pl.when(pl.program_id(1) == 0)
        def _():
            o_ref[...] = t

        @pl.when(pl.program_id(1) != 0)
        def _():
            o_ref[...] += t

    return pl.pallas_call(
        body, name=name, grid=(m // cb, n // tm),
        in_specs=[pl.BlockSpec((tm, kd), lambda j, i: (i, 0)), pl.BlockSpec((tm, cb), lambda j, i: (i, j))],
        out_specs=pl.BlockSpec((kd, cb), lambda j, i: (0, j)),
        out_shape=SDS((kd, m), F32),
        compiler_params=_cp(("arbitrary", "arbitrary"), VMEM_LIMIT),
    )(a, g)


def _dwconv(x, w, b):
    kw = w.shape[0]
    acc = b
    for k in range(kw):
        acc = acc + w[k:k + 1, :] * _shift(x, kw // 2 - k, x.shape[0])
    return acc


def _conv_silu_fn(x, w, b):
    y = _dwconv(x, w, b)
    return y * _sigmoid(y)


def _conv_glu_fn(v, g, wv, wg, bv, bg):
    cv = _dwconv(v, wv, bv)
    cg = _dwconv(g, wg, bg)
    return cg * _sigmoid(cg) * cv


def _conv_silu(x, w, b, bl, cb, name):
    n, c = x.shape
    sl = n // bl
    kw = w.shape[0]

    def body(x_ref, w_ref, b_ref, o_ref):
        o_ref[...] = _conv_silu_fn(x_ref[...], w_ref[...], b_ref[...])

    return pl.pallas_call(
        body, name=name, grid=(bl, c // cb),
        in_specs=[pl.BlockSpec((sl, cb), lambda s, j: (s, j)), pl.BlockSpec((kw, cb), lambda s, j: (0, j)),
                  pl.BlockSpec((1, cb), lambda s, j: (0, j))],
        out_specs=pl.BlockSpec((sl, cb), lambda s, j: (s, j)),
        out_shape=SDS((n, c), F32),
        compiler_params=_cp(("arbitrary", "arbitrary"), VMEM_LIMIT),
    )(x, w, b)


def _conv_silu_bwd(x, w, b, dys, bl, cb, name):
    n, c = x.shape
    sl = n // bl
    kw = w.shape[0]
    cnt = len(dys)

    def body(x_ref, w_ref, b_ref, *refs):
        dy = refs[0][...]
        for r in refs[1:cnt]:
            dy = dy + r[...]
        dx_ref, dw_ref, db_ref = refs[cnt:]
        _, vjp = jax.vjp(_conv_silu_fn, x_ref[...], w_ref[...], b_ref[...])
        dx, dw, db = vjp(dy)
        dx_ref[...] = dx.astype(BF16)

        @pl.when(pl.program_id(1) == 0)
        def _():
            dw_ref[...] = dw
            db_ref[...] = db

        @pl.when(pl.program_id(1) != 0)
        def _():
            dw_ref[...] += dw
            db_ref[...] += db

    dy_specs = []
    for arr, lead in dys:
        if lead is None:
            dy_specs.append(pl.BlockSpec((sl, cb), lambda j, s: (s, j)))
        else:
            dy_specs.append(pl.BlockSpec((None, sl, cb), functools.partial(lambda j, s, lead: (lead, s, j), lead=lead)))
    return pl.pallas_call(
        body, name=name, grid=(c // cb, bl),
        in_specs=[pl.BlockSpec((sl, cb), lambda j, s: (s, j)), pl.BlockSpec((kw, cb), lambda j, s: (0, j)),
                  pl.BlockSpec((1, cb), lambda j, s: (0, j))] + dy_specs,
        out_specs=[pl.BlockSpec((sl, cb), lambda j, s: (s, j)), pl.BlockSpec((kw, cb), lambda j, s: (0, j)),
                   pl.BlockSpec((1, cb), lambda j, s: (0, j))],
        out_shape=[SDS((n, c), BF16), SDS((kw, c), F32), SDS((1, c), F32)],
        compiler_params=_cp(("arbitrary", "arbitrary"), VMEM_LIMIT),
    )(x, w, b, *[a for a, _ in dys])


def _conv_glu(v, g, wv, wg, bv, bg, bl, cb, name):
    n, c = v.shape
    sl = n // bl
    kw = wv.shape[0]

    def body(v_ref, g_ref, wv_ref, wg_ref, bv_ref, bg_ref, o_ref):
        o_ref[...] = _conv_glu_fn(v_ref[...], g_ref[...], wv_ref[...], wg_ref[...], bv_ref[...], bg_ref[...]).astype(BF16)

    big = pl.BlockSpec((sl, cb), lambda s, j: (s, j))
    wsp = pl.BlockSpec((kw, cb), lambda s, j: (0, j))
    bsp = pl.BlockSpec((1, cb), lambda s, j: (0, j))
    return pl.pallas_call(
        body, name=name, grid=(bl, c // cb),
        in_specs=[big, big, wsp, wsp, bsp, bsp], out_specs=big, out_shape=SDS((n, c), BF16),
        compiler_params=_cp(("arbitrary", "arbitrary"), VMEM_LIMIT),
    )(v, g, wv, wg, bv, bg)


def _conv_glu_bwd(v, g, wv, wg, bv, bg, dact, bl, cb, name):
    n, c = v.shape
    sl = n // bl
    kw = wv.shape[0]

    def body(v_ref, g_ref, wv_ref, wg_ref, bv_ref, bg_ref, da_ref, dv_ref, dg_ref, dwv_ref, dwg_ref, dbv_ref, dbg_ref):
        _, vjp = jax.vjp(_conv_glu_fn, v_ref[...], g_ref[...], wv_ref[...], wg_ref[...], bv_ref[...], bg_ref[...])
        dv, dg, dwv, dwg, dbv, dbg = vjp(da_ref[...])
        dv_ref[...] = dv.astype(BF16)
        dg_ref[...] = dg.astype(BF16)

        @pl.when(pl.program_id(1) == 0)
        def _():
            dwv_ref[...] = dwv
            dwg_ref[...] = dwg
            dbv_ref[...] = dbv
            dbg_ref[...] = dbg

        @pl.when(pl.program_id(1) != 0)
        def _():
            dwv_ref[...] += dwv
            dwg_ref[...] += dwg
            dbv_ref[...] += dbv
            dbg_ref[...] += dbg

    big = pl.BlockSpec((sl, cb), lambda j, s: (s, j))
    wsp = pl.BlockSpec((kw, cb), lambda j, s: (0, j))
    bsp = pl.BlockSpec((1, cb), lambda j, s: (0, j))
    return pl.pallas_call(
        body, name=name, grid=(c // cb, bl),
        in_specs=[big, big, wsp, wsp, bsp, bsp, big],
        out_specs=[big, big, wsp, wsp, bsp, bsp],
        out_shape=[SDS((n, c), BF16), SDS((n, c), BF16), SDS((kw, c), F32), SDS((kw, c), F32), SDS((1, c), F32), SDS((1, c), F32)],
        compiler_params=_cp(("arbitrary", "arbitrary"), VMEM_LIMIT),
    )(v, g, wv, wg, bv, bg, dact)


_DIMS_T = {NN: (NT, TN, False, False), NT: (NN, TN, False, True), TN: (NT, NN, True, False)}


@functools.partial(jax.custom_vjp, nondiff_argnums=(2,))
def _bdot(a, b, dims):
    return _dot(a.astype(BF16), b.astype(BF16), dims)


def _bdot_fwd(a, b, dims):
    return _bdot(a, b, dims), (a, b)


def _bdot_bwd(dims, res, g):
    a, b = res
    da_dims, db_dims, a_swapped, b_swapped = _DIMS_T[dims]
    da = _bdot(b, g, da_dims) if a_swapped else _bdot(g, b, da_dims)
    db = _bdot(g, a, db_dims) if b_swapped else _bdot(a, g, db_dims)
    return da, db


_bdot.defvjp(_bdot_fwd, _bdot_bwd)


@functools.partial(jax.custom_vjp, nondiff_argnums=(1,))
def _expand_heads(v, width):
    return _split_dot(v, _head_matrix(width), NN)


def _head_matrix(width):
    hr = lax.broadcasted_iota(jnp.int32, (LANES, SSD_HEADS * width), 0)
    hc = lax.broadcasted_iota(jnp.int32, (LANES, SSD_HEADS * width), 1)
    return (hc // width == hr).astype(BF16)


def _split_dot(v, e, dims):
    hi = v.astype(BF16)
    lo = (v - hi.astype(F32)).astype(BF16)
    return _dot(hi, e, dims) + _dot(lo, e, dims)


_expand_heads.defvjp(lambda v, width: (_expand_heads(v, width), None),
                     lambda width, _, g: (_split_dot(g, _head_matrix(width), NT),))


def _ssd_chunk_fn(rev, xs, dtr, bms, cms, st, alog, dtb):
    q = dtr.shape[0]
    hd, per = SSD_HEAD_DIM, SSD_HEADS // SSD_GROUPS
    gw = per * hd
    r = lax.broadcasted_iota(jnp.int32, (q, q), 0)
    c = lax.broadcasted_iota(jnp.int32, (q, q), 1)
    sgn = 1 - 2 * rev
    tri = ((r - c) * sgn >= 0).astype(F32)
    tri_t = ((c - r) * sgn >= 0).astype(F32)
    r4 = lax.broadcasted_iota(jnp.int32, (q, per * q), 0)
    c4 = lax.broadcasted_iota(jnp.int32, (q, per * q), 1) % q
    mask4 = (r4 - c4) * sgn >= 0
    bdr = lax.broadcasted_iota(jnp.int32, (per * q, gw), 0) // q
    bdc = lax.broadcasted_iota(jnp.int32, (per * q, gw), 1) // hd
    diag = bdr == bdc
    dt = _softplus(dtr + dtb)
    dta = dt * (-jnp.exp(alog))
    cs = _dot(tri, dta, NN, HI)
    cs_t = _dot(dta, tri_t, TN, HI)
    tot = jnp.sum(dta, axis=0, keepdims=True)
    dt_x = _expand_heads(dt, hd)
    in_x = _expand_heads(jnp.exp(cs), hd)
    out_x = _expand_heads(jnp.exp(tot - cs), hd)
    ys, outs = [], []
    for g in range(SSD_GROUPS):
        bg, cg = bms[g], cms[g]
        heads = range(per * g, per * (g + 1))
        lanes = slice(gw * g, gw * (g + 1))
        scores = _bdot(cg, bg, NT)
        col = jnp.concatenate([jnp.broadcast_to(cs[:, h:h + 1], (q, q)) for h in heads], axis=1)
        row = jnp.concatenate([cs_t[h:h + 1, :] for h in heads], axis=1)
        seg = jnp.where(mask4, jnp.exp(jnp.where(mask4, col - row, 0.0)), 0.0)
        mcat = jnp.concatenate([scores] * per, axis=1) * seg
        xdt = xs[g] * dt_x[:, lanes]
        blocks = jnp.where(diag, jnp.concatenate([xdt] * per, axis=0), 0.0)
        y = _bdot(mcat, blocks, NN) + in_x[:, lanes] * _bdot(cg, st[g], NT)
        new = _bdot(xdt * out_x[:, lanes], bg, TN)
        keep = jnp.concatenate([jnp.exp(tot[:, h:h + 1]) * st[g][hd * j:hd * (j + 1), :] for j, h in enumerate(heads)], axis=0)
        ys.append(y)
        outs.append(keep + new)
    return ys, outs


def _ssd_scan(xs, bm, cm, dtr, alog2, dtb2, bl, name):
    n = xs.shape[0]
    q = SSD_CHUNK
    nc = n // bl // q
    hd, ns = SSD_HEAD_DIM, SSD_STATE
    gw = SSD_HEADS // SSD_GROUPS * hd

    def body(xs_ref, b_ref, c_ref, dt_ref, al_ref, db_ref, y_ref, sv_ref, st_ref):
        d, i = pl.program_id(0), pl.program_id(2)

        @pl.when(i == 0)
        def _():
            st_ref[...] = jnp.zeros(st_ref.shape, F32)

        st = [st_ref[gw * g:gw * (g + 1), :] for g in range(SSD_GROUPS)]
        sv_ref[...] = st_ref[...]
        xl = [xs_ref[:, gw * g:gw * (g + 1)] for g in range(SSD_GROUPS)]
        bms = [b_ref[:, ns * g:ns * (g + 1)] for g in range(SSD_GROUPS)]
        cms = [c_ref[:, ns * g:ns * (g + 1)] for g in range(SSD_GROUPS)]
        ys, outs = _ssd_chunk_fn(d, xl, dt_ref[...], bms, cms, st, al_ref[...], db_ref[...])
        for g in range(SSD_GROUPS):
            st_ref[gw * g:gw * (g + 1), :] = outs[g]
            y_ref[:, gw * g:gw * (g + 1)] = ys[g]

    def rowblk(d, s, i):
        return s * nc + i + d * (nc - 1 - 2 * i)

    return pl.pallas_call(
        body, name=name, grid=(2, bl, nc),
        in_specs=[pl.BlockSpec((q, SSD_HEADS * hd), lambda d, s, i: (rowblk(d, s, i), 0)),
                  pl.BlockSpec((q, SSD_GROUPS * ns), lambda d, s, i: (rowblk(d, s, i), 0)),
                  pl.BlockSpec((q, SSD_GROUPS * ns), lambda d, s, i: (rowblk(d, s, i), 0)),
                  pl.BlockSpec((q, LANES), lambda d, s, i: (rowblk(d, s, i), d)),
                  pl.BlockSpec((None, 1, LANES), lambda d, s, i: (d, 0, 0)),
                  pl.BlockSpec((None, 1, LANES), lambda d, s, i: (d, 0, 0))],
        out_specs=[pl.BlockSpec((None, q, SSD_HEADS * hd), lambda d, s, i: (d, rowblk(d, s, i), 0)),
                   pl.BlockSpec((None, None, SSD_HEADS * hd, ns), lambda d, s, i: (d, rowblk(d, s, i), 0, 0))],
        out_shape=[SDS((2, n, SSD_HEADS * hd), F32), SDS((2, n // q, SSD_HEADS * hd, ns), F32)],
        scratch_shapes=[pltpu.VMEM((SSD_HEADS * hd, ns), F32)],
        compiler_params=_cp(("arbitrary",) * 3, VMEM_LIMIT),
    )(xs, bm, cm, dtr, alog2, dtb2)


def _ssd_scan_bwd(xs, bm, cm, dtr, alog2, dtb2, saved, dy, bl, name):
    n = xs.shape[0]
    q = SSD_CHUNK
    nc = n // bl // q
    hd, ns = SSD_HEAD_DIM, SSD_STATE
    gw = SSD_HEADS // SSD_GROUPS * hd

    def body(xs_ref, b_ref, c_ref, dt_ref, al_ref, db_ref, sv_ref, dy_ref,
             dxs_ref, dbm_ref, dcm_ref, ddt_ref, dal_ref, ddb_ref, ds_ref):
        d, s, i = pl.program_id(0), pl.program_id(1), pl.program_id(2)

        @pl.when(i == 0)
        def _():
            ds_ref[...] = jnp.zeros(ds_ref.shape, F32)

        xl = [xs_ref[:, gw * g:gw * (g + 1)] for g in range(SSD_GROUPS)]
        bms = [b_ref[:, ns * g:ns * (g + 1)] for g in range(SSD_GROUPS)]
        cms = [c_ref[:, ns * g:ns * (g + 1)] for g in range(SSD_GROUPS)]
        st = [sv_ref[gw * g:gw * (g + 1), :] for g in range(SSD_GROUPS)]
        fn = functools.partial(_ssd_chunk_fn, d)
        _, vjp = jax.vjp(fn, xl, dt_ref[...], bms, cms, st, al_ref[...], db_ref[...])
        dys = [dy_ref[:, gw * g:gw * (g + 1)] for g in range(SSD_GROUPS)]
        dso = [ds_ref[gw * g:gw * (g + 1), :] for g in range(SSD_GROUPS)]
        dxl, ddt, dbg, dcg, dst, dal, ddb = vjp((dys, dso))
        for g in range(SSD_GROUPS):
            ds_ref[gw * g:gw * (g + 1), :] = dst[g]
            dxs_ref[:, gw * g:gw * (g + 1)] = dxl[g]
            dbm_ref[:, ns * g:ns * (g + 1)] = dbg[g]
            dcm_ref[:, ns * g:ns * (g + 1)] = dcg[g]
        ddt_ref[...] = ddt
        _acc_rows((dal_ref, ddb_ref), (dal, ddb), jnp.logical_and(s == 0, i == 0))

    def rowblk(d, s, i):
        return s * nc + (nc - 1 - i) + d * (2 * i - (nc - 1))

    row = lambda d, s, i: (rowblk(d, s, i), 0)
    drow = lambda d, s, i: (d, rowblk(d, s, i), 0)
    dfix = lambda d, s, i: (d, 0, 0)
    dcol = lambda d, s, i: (rowblk(d, s, i), d)
    return pl.pallas_call(
        body, name=name, grid=(2, bl, nc),
        in_specs=[pl.BlockSpec((q, SSD_HEADS * hd), row), pl.BlockSpec((q, SSD_GROUPS * ns), row),
                  pl.BlockSpec((q, SSD_GROUPS * ns), row), pl.BlockSpec((q, LANES), dcol),
                  pl.BlockSpec((None, 1, LANES), dfix), pl.BlockSpec((None, 1, LANES), dfix),
                  pl.BlockSpec((None, None, SSD_HEADS * hd, ns), lambda d, s, i: (d, rowblk(d, s, i), 0, 0)),
                  pl.BlockSpec((q, SSD_HEADS * hd), row)],
        out_specs=[pl.BlockSpec((None, q, SSD_HEADS * hd), drow), pl.BlockSpec((None, q, SSD_GROUPS * ns), drow),
                   pl.BlockSpec((None, q, SSD_GROUPS * ns), drow), pl.BlockSpec((q, LANES), dcol),
                   pl.BlockSpec((None, 1, LANES), dfix), pl.BlockSpec((None, 1, LANES), dfix)],
        out_shape=[SDS((2, n, SSD_HEADS * hd), F32), SDS((2, n, SSD_GROUPS * ns), F32), SDS((2, n, SSD_GROUPS * ns), F32),
                   SDS((n, 2 * LANES), F32), SDS((2, 1, LANES), F32), SDS((2, 1, LANES), F32)],
        scratch_shapes=[pltpu.VMEM((SSD_HEADS * hd, ns), F32)],
        compiler_params=_cp(("arbitrary",) * 3, VMEM_LIMIT),
    )(xs, bm, cm, dtr, alog2, dtb2, saved, dy)


def _s5_consts():
    t, ch, p = S5_T, S5_CH, S5_STATE
    lane = lax.broadcasted_iota(jnp.int32, (1, 2 * p), 1)
    pr = lax.broadcasted_iota(jnp.int32, (p, 2 * p), 0)
    pc = lax.broadcasted_iota(jnp.int32, (p, 2 * p), 1)
    cr = lax.broadcasted_iota(jnp.int32, (ch, t * ch), 0)
    cc = lax.broadcasted_iota(jnp.int32, (ch, t * ch), 1)
    return dict(
        left=lane < p,
        sg=jnp.where(lane < p, -1.0, 1.0).astype(F32),
        dup=(pc % p == pr).astype(F32),
        dup_l=(pc == pr).astype(F32),
        dup_r=(pc == pr + p).astype(F32),
        rep=(cc % ch == cr).astype(F32),
        rep0=(cc == cr).astype(F32),
    )


def _s5_mats(k, rev, lr, li, ls, bre, bim, cre, cim):
    t = S5_T
    step = jnp.exp(ls)
    lr2 = jnp.sum(lr * k["dup"], axis=0, keepdims=True)
    li2 = jnp.sum(li * k["dup"], axis=0, keepdims=True)

    def erow(d):
        ang = (d * step) * li2
        return jnp.exp((d * step) * lr2) * jnp.where(k["left"], jnp.cos(ang), jnp.sin(ang))

    es = [erow(d) for d in range(t + 1)]
    mag = jnp.exp(step * lr)
    ar, ai = mag * jnp.cos(step * li), mag * jnp.sin(step * li)
    den = lr * lr + li * li
    zr = ((ar - 1.0) * lr + ai * li) / den
    zi = (ai * lr - (ar - 1.0) * li) / den
    bbr = zr * bre - zi * bim
    bbi = zr * bim + zi * bre
    bt1 = _dot(bbr, k["dup"], TN, HI)
    bt2 = _dot(bbi, k["dup"], TN, HI)
    bst = _dot(bbr, k["dup_l"], TN, HI) - _dot(bbi, k["dup_r"], TN, HI)
    c1 = _dot(cre, k["dup"], NN, HI)
    c2 = _dot(cim, k["dup"], NN, HI)
    sg = k["sg"]
    ce = [e * c1 + sg * _swap(e) * c2 for e in es]
    lags = range(t - 1, -1, -1) if rev else range(t)
    kt = _dot(bst, jnp.concatenate([ce[d] for d in lags], axis=0), NT, HI)
    toep = jnp.concatenate([_lane_shift(kt, -S5_CH * (t - 1 - s) if rev else S5_CH * s) for s in range(t)], axis=0)
    w_out =jnp.concatenate([ce[(t - qq) if rev else (qq + 1)] * (-sg) for qq in range(t)], axis=0)
    w_st = jnp.concatenate(
        [(lambda e: e * bt1 + sg * _swap(e) * bt2)(es[s if rev else (t - 1 - s)]) for s in range(t)], axis=0)
    return toep, w_out, w_st, es[t]


def _cmul_row(k, e, z):
    es = _swap(e)
    return z * jnp.where(k["left"], e, es) + k["sg"] * _swap(z) * jnp.where(k["left"], es, e)


def _s5_dir(k, rev, nck, x, mats):
    toep, w_out, w_st, a_t = mats
    acc = _dot(x, w_st)
    e = a_t
    kk = 1
    sign = -1 if rev else 1
    while kk < nck:
        acc = acc + _cmul_row(k, e, _shift(acc, sign * kk, nck))
        e = _cmul_row(k, e, e)
        kk *= 2
    prev = _shift(acc, sign, nck)
    return _dot(x, toep) + _dot(prev, w_out, NT)


def _s5_group_fn(nck, x, pf, pb, bre, bim, dcol, wv, wg, bv, bg):
    k = _s5_consts()
    t = S5_T
    y = x * jnp.sum(dcol * k["rep"], axis=0, keepdims=True)
    for rev, (lr, li, ls, cre, cim) in ((False, pf), (True, pb)):
        y = y + _s5_dir(k, rev, nck, x, _s5_mats(k, rev, lr, li, ls, bre, bim, cre, cim))
    gy = jax.nn.gelu(y)
    def kron_eye(w16):
        wide = _dot(w16, k["rep0"], NN, HI)
        return jnp.concatenate([_lane_shift(wide, S5_CH * qq) for qq in range(t)], axis=0)

    kv, kg = kron_eye(wv), kron_eye(wg)
    val =_dot(gy, kv) + jnp.sum(bv * k["rep"], axis=0, keepdims=True)
    gate = _dot(gy, kg) + jnp.sum(bg * k["rep"], axis=0, keepdims=True)
    return val * _sigmoid(gate)


def _s5_specs(r):
    p, ch = S5_STATE, S5_CH
    g3 = lambda i: (i, 0, 0)
    col = pl.BlockSpec((None, p, 1), g3)
    one = pl.BlockSpec((None, 1, 1), g3)
    cmat = pl.BlockSpec((None, ch, p), g3)
    bmat = pl.BlockSpec((None, p, ch), g3)
    ccol = pl.BlockSpec((None, ch, 1), g3)
    sq = pl.BlockSpec((None, ch, ch), g3)
    xs = pl.BlockSpec((None, r, S5_T * ch), g3)
    specs = [xs, col, col, one, cmat, cmat, col, col, one, cmat, cmat, bmat, bmat, ccol, sq, sq, ccol, ccol]
    return specs


def _s5_unpack(vals):
    x = vals[0]
    pf = tuple(vals[1:6])
    pb = tuple(vals[6:11])
    bre, bim, dcol, wv, wg, bv, bg = vals[11:18]
    return x, pf, pb, bre, bim, dcol, wv, wg, bv, bg


def _s5_fwd(args, nck, name):
    x = args[0]
    ng, r, w = x.shape

    def body(*refs):
        vals = [ref[...] for ref in refs[:18]]
        refs[18][...] = _s5_group_fn(nck, *_s5_unpack(vals))

    specs = _s5_specs(r)
    return pl.pallas_call(
        body, name=name, grid=(ng,), in_specs=specs, out_specs=specs[0], out_shape=SDS(x.shape, F32),
        compiler_params=_cp(("arbitrary",), VMEM_LIMIT),
    )(*args)


def _s5_bwd(args, dy, nck, name):
    x = args[0]
    ng, r, w = x.shape

    def body(*refs):
        vals = [ref[...] for ref in refs[:18]]
        _, vjp = jax.vjp(lambda *v: _s5_group_fn(nck, *_s5_unpack(v)), *vals)
        grads = vjp(refs[18][...])
        for o_ref, gval in zip(refs[19:], grads):
            o_ref[...] = gval.astype(o_ref.dtype)

    specs = _s5_specs(r)
    return pl.pallas_call(
        body, name=name, grid=(ng,), in_specs=specs + [specs[0]], out_specs=specs,
        out_shape=[SDS(x.shape, BF16)] + [SDS(a.shape, F32) for a in args[1:]],
        compiler_params=_cp(("arbitrary",), VMEM_LIMIT),
    )(*args, dy)


def _mix_fn(yf, yb, xs, z, s5o, dvec, nw_ssd, nw_s5):
    hr = lax.broadcasted_iota(jnp.int32, (LANES, SSD_HEADS * SSD_HEAD_DIM), 0)
    hc = lax.broadcasted_iota(jnp.int32, (LANES, SSD_HEADS * SSD_HEAD_DIM), 1)
    expand = (hc // SSD_HEAD_DIM == hr).astype(F32)
    dch = jnp.sum(dvec * expand, axis=0, keepdims=True)
    y = (yf + yb + dch * xs) * (z * _sigmoid(z))
    return _rms(y, nw_ssd), _rms(s5o, nw_s5)


def _mix(y2, xs, z, s5o, dvec, nw_ssd, nw_s5, tm, name):
    n, c1 = xs.shape
    c2 = s5o.shape[1]

    def body(yf_ref, yb_ref, xs_ref, z_ref, s_ref, d_ref, n1_ref, n2_ref, o_ref):
        o1, o2 = _mix_fn(yf_ref[...], yb_ref[...], xs_ref[...], z_ref[...], s_ref[...], d_ref[...], n1_ref[...], n2_ref[...])
        o_ref[:, :c1] = o1.astype(BF16)
        o_ref[:, c1:] = o2.astype(BF16)

    row = lambda i: (i, 0)
    fix = lambda i: (0, 0)
    return pl.pallas_call(
        body, name=name, grid=(n // tm,),
        in_specs=[pl.BlockSpec((None, tm, c1), lambda i: (0, i, 0)), pl.BlockSpec((None, tm, c1), lambda i: (1, i, 0)),
                  pl.BlockSpec((tm, c1), row), pl.BlockSpec((tm, c1), row), pl.BlockSpec((tm, c2), row),
                  pl.BlockSpec((LANES, 1), fix), pl.BlockSpec((1, c1), fix), pl.BlockSpec((1, c2), fix)],
        out_specs=pl.BlockSpec((tm, c1 + c2), row), out_shape=SDS((n, c1 + c2), BF16),
        compiler_params=_cp(("arbitrary",), VMEM_LIMIT),
    )(y2, y2, xs, z, s5o, dvec, nw_ssd, nw_s5)


def _acc_rows(refs, vals, first):
    @pl.when(first)
    def _():
        for ref, v in zip(refs, vals):
            ref[...] = v

    @pl.when(jnp.logical_not(first))
    def _():
        for ref, v in zip(refs, vals):
            ref[...] += v


def _mix_bwd(y2, xs, z, s5o, dvec, nw_ssd, nw_s5, dmix, tm, name):
    n, c1 = xs.shape
    c2 = s5o.shape[1]

    def body(yf_ref, yb_ref, xs_ref, z_ref, s_ref, d_ref, n1_ref, n2_ref, dm_ref,
             dy_ref, dxs_ref, dz_ref, ds_ref, dd_ref, dn1_ref, dn2_ref):
        _, vjp = jax.vjp(_mix_fn, yf_ref[...], yb_ref[...], xs_ref[...], z_ref[...], s_ref[...], d_ref[...], n1_ref[...], n2_ref[...])
        dyf, _, dxs, dz, ds, dd, dn1, dn2 = vjp((dm_ref[:, :c1], dm_ref[:, c1:]))
        dy_ref[...] = dyf
        dxs_ref[...] = dxs
        dz_ref[...] = dz.astype(BF16)
        ds_ref[...] = ds
        _acc_rows((dd_ref, dn1_ref, dn2_ref), (dd, dn1, dn2), pl.program_id(0) == 0)

    row = lambda i: (i, 0)
    fix = lambda i: (0, 0)
    return pl.pallas_call(
        body, name=name, grid=(n // tm,),
        in_specs=[pl.BlockSpec((None, tm, c1), lambda i: (0, i, 0)), pl.BlockSpec((None, tm, c1), lambda i: (1, i, 0)),
                  pl.BlockSpec((tm, c1), row), pl.BlockSpec((tm, c1), row), pl.BlockSpec((tm, c2), row),
                  pl.BlockSpec((LANES, 1), fix), pl.BlockSpec((1, c1), fix), pl.BlockSpec((1, c2), fix),
                  pl.BlockSpec((tm, c1 + c2), row)],
        out_specs=[pl.BlockSpec((tm, c1), row), pl.BlockSpec((tm, c1), row), pl.BlockSpec((tm, c1), row), pl.BlockSpec((tm, c2), row),
                   pl.BlockSpec((LANES, 1), fix), pl.BlockSpec((1, c1), fix), pl.BlockSpec((1, c2), fix)],
        out_shape=[SDS((n, c1), F32), SDS((n, c1), F32), SDS((n, c1), BF16), SDS((n, c2), F32),
                   SDS((LANES, 1), F32), SDS((1, c1), F32), SDS((1, c2), F32)],
        compiler_params=_cp(("arbitrary",), VMEM_LIMIT),
    )(y2, y2, xs, z, s5o, dvec, nw_ssd, nw_s5, dmix)


def _final_loss(h2, nw, tgt, tm, name):
    n, d = h2.shape

    def loss_fn(h, w, t):
        e = _rms(h, w) - t
        return (0.5 / d) * jnp.sum(e * e)

    def body(h_ref, w_ref, t_ref, l_ref, dh_ref, dw_ref):
        loss, (dh, dw) = jax.value_and_grad(loss_fn, argnums=(0, 1))(h_ref[...], w_ref[...], t_ref[...])
        dh_ref[...] = dh
        _acc_rows((l_ref, dw_ref), (jnp.full((1, LANES), loss, F32), dw), pl.program_id(0) == 0)

    row = lambda i: (i, 0)
    fix = lambda i: (0, 0)
    return pl.pallas_call(
        body, name=name, grid=(n // tm,),
        in_specs=[pl.BlockSpec((tm, d), row), pl.BlockSpec((1, d), fix), pl.BlockSpec((tm, d), row)],
        out_specs=[pl.BlockSpec((1, LANES), fix), pl.BlockSpec((tm, d), row), pl.BlockSpec((1, d), fix)],
        out_shape=[SDS((1, LANES), F32), SDS((n, d), F32), SDS((1, d), F32)],
        compiler_params=_cp(("arbitrary",), VMEM_LIMIT),
    )(h2, nw, tgt)


def _norm_bwd(x, nw, dhn, dres, tm, name):
    n, d = x.shape

    def body(x_ref, w_ref, g_ref, r_ref, dx_ref, dw_ref):
        _, vjp = jax.vjp(_rms, x_ref[...], w_ref[...])
        dx, dw = vjp(g_ref[...])
        dx_ref[...] = r_ref[...] + dx
        _acc_rows((dw_ref,), (dw,), pl.program_id(0) == 0)

    row = lambda i: (i, 0)
    fix = lambda i: (0, 0)
    return pl.pallas_call(
        body, name=name, grid=(n // tm,),
        in_specs=[pl.BlockSpec((tm, d), row), pl.BlockSpec((1, d), fix), pl.BlockSpec((tm, d), row), pl.BlockSpec((tm, d), row)],
        out_specs=[pl.BlockSpec((tm, d), row), pl.BlockSpec((1, d), fix)],
        out_shape=[SDS((n, d), F32), SDS((1, d), F32)],
        compiler_params=_cp(("arbitrary",), VMEM_LIMIT),
    )(x, nw, dhn, dres)


def _row_tile(n, cap=512):
    for t in range(min(cap, n) // 8 * 8, 7, -8):
        if n % t == 0:
            return t
    return n


def _sum_lead(a, name):
    kk, n, c = a.shape
    tm = _row_tile(n)

    def body(a_ref, o_ref):
        acc = a_ref[0].astype(F32)
        for i in range(1, kk):
            acc = acc + a_ref[i].astype(F32)
        o_ref[...] = acc

    return pl.pallas_call(
        body, name=name, grid=(n // tm,),
        in_specs=[pl.BlockSpec((kk, tm, c), lambda i: (0, i, 0))],
        out_specs=pl.BlockSpec((tm, c), lambda i: (i, 0)), out_shape=SDS((n, c), F32),
        compiler_params=_cp(("arbitrary",), VMEM_LIMIT),
    )(a)


def _adamw(w, g, m, v, name):
    n, c = w.shape
    tm = _row_tile(n)

    def body(w_ref, g_ref, m_ref, v_ref, d_ref, nm_ref, nv_ref):
        gv = g_ref[...]
        mn = ADAM_B1 * m_ref[...] + (1.0 - ADAM_B1) * gv
        vn = ADAM_B2 * v_ref[...] + (1.0 - ADAM_B2) * jnp.square(gv)
        m_hat = mn / (1.0 - ADAM_B1 ** ADAM_STEP)
        v_hat = vn / (1.0 - ADAM_B2 ** ADAM_STEP)
        d_ref[...] = -ADAM_LR * (m_hat / (jnp.sqrt(v_hat) + ADAM_EPS) + ADAM_WD * w_ref[...])
        nm_ref[...] = mn
        nv_ref[...] = vn

    spec = pl.BlockSpec((tm, c), lambda i: (i, 0))
    return pl.pallas_call(
        body, name=name, grid=(n // tm,), in_specs=[spec] * 4, out_specs=[spec] * 3,
        out_shape=[SDS((n, c), F32)] * 3, compiler_params=_cp(("arbitrary",), VMEM_LIMIT),
    )(w, g, m, v)


ANY = pl.BlockSpec(memory_space=pl.ANY)


def _me():
    return lax.axis_index("x"), lax.axis_index("y"), lax.axis_index("c")


def _gather_xy(split, whole, name):
    ns, cnt = len(split), len(split) + len(whole)

    def body(*refs):
        src, dst = refs[:cnt], refs[cnt:2 * cnt]
        send, recv = refs[2 * cnt:]
        x, y, c = _me()
        mine = 2 * x + y
        chips = [(1 - x, y), (x, 1 - y), (1 - x, 1 - y)]

        def ici(a, j, slot):
            px, py = chips[j]
            if a < ns:
                s_ref, d_ref = src[a].at[c], dst[a].at[slot].at[c]
            else:
                s_ref, d_ref = src[a], dst[a].at[slot]
            return pltpu.make_async_remote_copy(s_ref, d_ref, send.at[3 * a + j], recv.at[3 * a + j],
                                                device_id=(px, py, c), device_id_type=MESH)

        def d2d(a, j, half):
            px, py = chips[j]
            ref = dst[a].at[2 * px + py].at[half]
            return pltpu.make_async_remote_copy(ref, ref, send.at[3 * cnt + 3 * a + j], recv.at[3 * cnt + 3 * a + j],
                                                device_id=(x, y, 1 - c), device_id_type=MESH)

        def own(a):
            return pltpu.make_async_remote_copy(src[a], dst[a].at[mine], send.at[nsem - cnt + a], recv.at[nsem - cnt + a],
                                                device_id=(x, y, 1 - c), device_id_type=MESH)

        started = []
        for a in range(cnt):
            cp = own(a)
            cp.start()
            started.append(cp)
            for j in range(3):
                cp = ici(a, j, mine)
                cp.start()
                started.append(cp)
        for a in range(cnt):
            for j, (px, py) in enumerate(chips):
                ici(a, j, 2 * px + py).wait_recv()
                if a < ns:
                    cp = d2d(a, j, c)
                    cp.start()
                    started.append(cp)
        for a in range(ns):
            for j in range(3):
                d2d(a, j, 1 - c).wait_recv()
        for a in range(cnt):
            own(a).wait_recv()
        for cp in started:
            cp.wait_send()

    nsem = 3 * cnt + 3 * ns + cnt
    return pl.pallas_call(
        body, name=name, in_specs=[ANY] * cnt, out_specs=[ANY] * cnt,
        out_shape=[SDS((4,) + s.shape, s.dtype) for s in split + whole],
        scratch_shapes=[pltpu.SemaphoreType.DMA((nsem,)), pltpu.SemaphoreType.DMA((nsem,))],
    )(*split, *whole)


def _swap_sibling(parts, pick, name):
    cnt = len(parts)

    def body(*refs):
        src, dst = refs[:cnt], refs[cnt:2 * cnt]
        send, recv = refs[2 * cnt:]
        x, y, c = _me()
        cps = []
        for a in range(cnt):
            cp = pltpu.make_async_remote_copy(src[a].at[1 - c] if pick else src[a], dst[a], send.at[a], recv.at[a],
                                              device_id=(x, y, 1 - c), device_id_type=MESH)
            cp.start()
            cps.append(cp)
        for cp in cps:
            cp.wait()

    return pl.pallas_call(
        body, name=name, in_specs=[ANY] * cnt, out_specs=[ANY] * cnt,
        out_shape=[SDS(p.shape[1:] if pick else p.shape, p.dtype) for p in parts],
        scratch_shapes=[pltpu.SemaphoreType.DMA((cnt,)), pltpu.SemaphoreType.DMA((cnt,))],
    )(*parts)


def _scatter_xy(parts, name):
    cnt = len(parts)

    def body(*refs):
        src, dst = refs[:cnt], refs[cnt:2 * cnt]
        send, recv, loc = refs[2 * cnt:]
        x, y, c = _me()
        mine = 2 * x + y
        chips = [(1 - x, y), (x, 1 - y), (1 - x, 1 - y)]
        local = []
        for a in range(cnt):
            cp = pltpu.make_async_copy(src[a].at[mine], dst[a].at[mine], loc.at[a])
            cp.start()
            local.append(cp)
        sends = []
        for a in range(cnt):
            for j, (px, py) in enumerate(chips):
                cp = pltpu.make_async_remote_copy(src[a].at[2 * px + py], dst[a].at[mine], send.at[3 * a + j], recv.at[3 * a + j],
                                                  device_id=(px, py, c), device_id_type=MESH)
                cp.start()
                sends.append(cp)
        for a in range(cnt):
            for j, (px, py) in enumerate(chips):
                pltpu.make_async_remote_copy(src[a].at[mine], dst[a].at[2 * px + py], send.at[3 * a + j], recv.at[3 * a + j],
                                             device_id=(px, py, c), device_id_type=MESH).wait_recv()
        for cp in sends:
            cp.wait_send()
        for cp in local:
            cp.wait()

    return pl.pallas_call(
        body, name=name, in_specs=[ANY] * cnt, out_specs=[ANY] * cnt,
        out_shape=[SDS(p.shape, p.dtype) for p in parts],
        scratch_shapes=[pltpu.SemaphoreType.DMA((3 * cnt,)), pltpu.SemaphoreType.DMA((3 * cnt,)), pltpu.SemaphoreType.DMA((cnt,))],
    )(*parts)


def _bcast_all(buf, name):
    def body(src, dst, send, recv, loc):
        x, y, c = _me()
        mine = 4 * x + 2 * y + c
        own = pltpu.make_async_copy(src, dst.at[mine], loc)
        own.start()
        sends = []
        for k in range(1, 8):
            px, py, pc = x ^ (k >> 2), y ^ ((k >> 1) & 1), c ^ (k & 1)
            cp = pltpu.make_async_remote_copy(src, dst.at[mine], send.at[k - 1], recv.at[k - 1],
                                              device_id=(px, py, pc), device_id_type=MESH)
            cp.start()
            sends.append(cp)
        for k in range(1, 8):
            px, py, pc = x ^ (k >> 2), y ^ ((k >> 1) & 1), c ^ (k & 1)
            pltpu.make_async_remote_copy(src, dst.at[4 * px + 2 * py + pc], send.at[k - 1], recv.at[k - 1],
                                         device_id=(px, py, pc), device_id_type=MESH).wait_recv()
        for cp in sends:
            cp.wait_send()
        own.wait()

    return pl.pallas_call(
        body, name=name, in_specs=[ANY], out_specs=ANY, out_shape=SDS((8,) + buf.shape, buf.dtype),
        scratch_shapes=[pltpu.SemaphoreType.DMA((7,)), pltpu.SemaphoreType.DMA((7,)), pltpu.SemaphoreType.DMA(())],
    )(buf)


def _add_half(parts, got, core, dtype, name):
    shp = got.shape
    a2, b2 = parts.reshape(2, -1, shp[-1]), got.reshape(-1, shp[-1])
    n, c = b2.shape
    tm = _row_tile(n, 256)

    def body(core_ref, a_ref, b_ref, o_ref):
        o_ref[...] = (a_ref[...] + b_ref[...]).astype(dtype)

    spec = pl.BlockSpec((tm, c), lambda i, core_ref: (i, 0))
    grid_spec = pltpu.PrefetchScalarGridSpec(
        num_scalar_prefetch=1, grid=(n // tm,),
        in_specs=[pl.BlockSpec((None, tm, c), lambda i, core_ref: (core_ref[0], i, 0)), spec], out_specs=spec)
    return pl.pallas_call(body, name=name, grid_spec=grid_spec, out_shape=SDS((n, c), dtype),
                          compiler_params=_cp(("arbitrary",), VMEM_LIMIT))(core.reshape(1), a2, b2).reshape(shp)


def _x_layout(u, name):
    n, c = u.shape
    t, ch = S5_T, S5_CH
    gb = LANES // ch
    rows = min(64, n // t)

    def body(u_ref, o_ref):
        for s in range(t):
            us = u_ref[pl.ds(s, rows, stride=t), :]
            for g in range(gb):
                o_ref[g, :, ch * s:ch * (s + 1)] = us[:, ch * g:ch * (g + 1)]

    return pl.pallas_call(
        body, name=name, grid=(n // (rows * t), c // LANES),
        in_specs=[pl.BlockSpec((rows * t, LANES), lambda i, j: (i, j))],
        out_specs=pl.BlockSpec((gb, rows, t * ch), lambda i, j: (j, i, 0)),
        out_shape=SDS((c // ch, n // t, t * ch), F32),
        compiler_params=_cp(("arbitrary", "arbitrary"), VMEM_LIMIT),
    )(u)


def _token_layout(xg, name):
    ng, r, w = xg.shape
    t, ch = S5_T, S5_CH
    gb = LANES // ch
    rows = min(64, r)

    def body(x_ref, o_ref):
        for s in range(t):
            parts = [x_ref[g, :, ch * s:ch * (s + 1)].astype(F32) for g in range(gb)]
            o_ref[pl.ds(s, rows, stride=t), :] = jnp.concatenate(parts, axis=1)

    return pl.pallas_call(
        body, name=name, grid=(r // rows, ng // gb),
        in_specs=[pl.BlockSpec((gb, rows, w), lambda i, j: (j, i, 0))],
        out_specs=pl.BlockSpec((rows * t, LANES), lambda i, j: (i, j)),
        out_shape=SDS((r * t, ng * ch), F32),
        compiler_params=_cp(("arbitrary", "arbitrary"), VMEM_LIMIT),
    )(xg)


def _pad_lanes(a, lanes=LANES):
    return jnp.pad(a, ((0, 0), (0, lanes - a.shape[1])))


def _local_step(x, tgt, p, bl):
    n, d = x.shape
    sw = SSD_HEADS * SSD_HEAD_DIM
    gn = SSD_GROUPS * SSD_STATE
    tm = min(n, 512)
    tm_ffn = min(n, 256)
    nck = n // bl // S5_T
    dff = p["w_down"].shape[0]
    s5w = S5_GROUPS * S5_CH

    w_in = p["w_in"]
    o1, o2, o3, o4 = sw, sw + sw, sw + sw + gn, sw + sw + 2 * gn
    w_z, w_xs, w_b, w_c = w_in[:, :o1], w_in[:, o1:o2], w_in[:, o2:o3], w_in[:, o3:o4]
    w_dt = jnp.concatenate([_pad_lanes(w_in[:, o4:o4 + SSD_HEADS]), _pad_lanes(w_in[:, o4 + SSD_HEADS:o4 + 2 * SSD_HEADS])], 1)
    w_u = w_in[:, o4 + 2 * SSD_HEADS:]
    in_ws = [w_z, w_xs, w_b, w_c, w_dt, w_u]
    cw, cb_ = p["ssd_conv_w"], p["ssd_conv_b"]
    conv_parts = [(cw[:, :sw], cb_[:, :sw]), (cw[:, sw:sw + gn], cb_[:, sw:sw + gn]), (cw[:, sw + gn:], cb_[:, sw + gn:])]
    alog2 = jnp.stack([_pad_lanes(p["ssd_a_log_fwd"]), _pad_lanes(p["ssd_a_log_bwd"])])
    dtb2 = jnp.stack([_pad_lanes(p["ssd_dt_bias_fwd"]), _pad_lanes(p["ssd_dt_bias_bwd"])])
    dvec = _pad_lanes(p["ssd_d"]).reshape(LANES, 1)

    hn, z, xs_pre, b_pre, c_pre, dtr, u = _norm_matmul(x, p["norm_mix_w"], in_ws, tm, "in_proj")
    pres = [xs_pre, b_pre, c_pre]
    acts = [_conv_silu(pre, w, b, bl, min(256, pre.shape[1]), f"ssd_conv_{i}") for i, (pre, (w, b)) in enumerate(zip(pres, conv_parts))]
    xs_a, b_a, c_a = acts
    y2, saved = _ssd_scan(xs_a, b_a, c_a, dtr, alog2, dtb2, bl, "ssd_scan")

    def col(a):
        return a.reshape(a.shape + (1,))

    s5_params = [
        col(p["s5_lambda_re_fwd"]), col(p["s5_lambda_im_fwd"]), p["s5_log_step_fwd"].reshape(S5_GROUPS, 1, 1), p["s5_c_re_fwd"], p["s5_c_im_fwd"],
        col(p["s5_lambda_re_bwd"]), col(p["s5_lambda_im_bwd"]), p["s5_log_step_bwd"].reshape(S5_GROUPS, 1, 1), p["s5_c_re_bwd"], p["s5_c_im_bwd"],
        p["s5_b_re"], p["s5_b_im"], col(p["s5_d"].reshape(S5_GROUPS, S5_CH)),
        p["s5_glu_w"][:, :, :S5_CH], p["s5_glu_w"][:, :, S5_CH:], col(p["s5_glu_b"][:, :S5_CH]), col(p["s5_glu_b"][:, S5_CH:]),
    ]
    s5_args = [_x_layout(u, "s5_u_blocks")] + s5_params
    s5o = _token_layout(_s5_fwd(s5_args, nck, "s5_fwd"), "s5_y_tokens")
    ymix = _mix(y2, xs_a, z, s5o, dvec, p["ssd_norm_w"], p["s5_norm_w"], tm, "mix")
    h1 = _matmul_res(ymix, p["w_out"], x, tm, "out_proj")
    w_up = p["w_up"]
    hn2, up_v, up_g = _norm_matmul(h1, p["norm_ffn_w"], [w_up[:, :dff], w_up[:, dff:]], tm_ffn, "ffn_up")
    fw, fb = p["ffn_conv_w"], p["ffn_conv_b"]
    act = _conv_glu(up_v, up_g, fw[:, :dff], fw[:, dff:], fb[:, :dff], fb[:, dff:], bl, 256, "ffn_conv")
    h2 = _matmul_res(act, p["w_down"], h1, tm, "ffn_down")
    loss, dh2, g_nfw = _final_loss(h2, p["norm_final_w"].reshape(1, d), tgt, tm, "final_loss")

    g = {"norm_final_w": g_nfw.reshape(d)}
    g["w_down"] = _matmul_tn(act, dh2, tm, d, "ffn_down_dw")
    dact = _matmul_nt([dh2], [p["w_down"]], tm, "ffn_down_dx")
    dup_v, dup_g, dwv, dwg, dbv, dbg = _conv_glu_bwd(up_v, up_g, fw[:, :dff], fw[:, dff:], fb[:, :dff], fb[:, dff:], dact, bl, 256, "ffn_conv_bwd")
    g["ffn_conv_w"] = jnp.concatenate([dwv, dwg], 1)
    g["ffn_conv_b"] = jnp.concatenate([dbv, dbg], 1)
    g["w_up"] = jnp.concatenate([_matmul_tn(hn2, dup_v, tm, dff // 2, "ffn_up_dw_v"), _matmul_tn(hn2, dup_g, tm, dff // 2, "ffn_up_dw_g")], 1)
    dhn2 = _matmul_nt([dup_v, dup_g], [w_up[:, :dff], w_up[:, dff:]], tm_ffn, "ffn_up_dx")
    dh1, g["norm_ffn_w"] = _norm_bwd(h1, p["norm_ffn_w"], dhn2, dh2, tm, "ffn_norm_bwd")
    g["w_out"] = _matmul_tn(ymix, dh1, tm, d, "out_proj_dw")
    dmix = _matmul_nt([dh1], [p["w_out"]], tm, "out_proj_dx")
    dyssd, dxs_gate, dz, ds5o, g_d, g["ssd_norm_w"], g["s5_norm_w"] = _mix_bwd(
        y2, xs_a, z, s5o, dvec, p["ssd_norm_w"], p["s5_norm_w"], dmix, tm, "mix_bwd")
    g["ssd_d"] = g_d[:SSD_HEADS].reshape(1, SSD_HEADS)
    s5g = _s5_bwd(s5_args, _x_layout(ds5o, "s5_dy_blocks"), nck, "s5_bwd")
    du = _token_layout(s5g[0], "s5_du_tokens")
    (g["s5_lambda_re_fwd"], g["s5_lambda_im_fwd"], g["s5_log_step_fwd"], g["s5_c_re_fwd"], g["s5_c_im_fwd"],
     g["s5_lambda_re_bwd"], g["s5_lambda_im_bwd"], g["s5_log_step_bwd"], g["s5_c_re_bwd"], g["s5_c_im_bwd"],
     g["s5_b_re"], g["s5_b_im"], g_s5d, g_wv, g_wg, g_bv, g_bg) = s5g[1:]
    for k_ in ("s5_lambda_re_fwd", "s5_lambda_im_fwd", "s5_lambda_re_bwd", "s5_lambda_im_bwd"):
        g[k_] = g[k_].reshape(S5_GROUPS, S5_STATE)
    for k_ in ("s5_log_step_fwd", "s5_log_step_bwd"):
        g[k_] = g[k_].reshape(S5_GROUPS)
    g["s5_d"] = g_s5d.reshape(1, s5w)
    g["s5_glu_w"] = jnp.concatenate([g_wv, g_wg], 2)
    g["s5_glu_b"] = jnp.concatenate([g_bv.reshape(S5_GROUPS, S5_CH), g_bg.reshape(S5_GROUPS, S5_CH)], 1)
    dxs2, dbm2, dcm2, ddtr, dal2, ddb2 = _ssd_scan_bwd(xs_a, b_a, c_a, dtr, alog2, dtb2, saved, dyssd, bl, "ssd_scan_bwd")
    g["ssd_a_log_fwd"], g["ssd_a_log_bwd"] = dal2[0, :, :SSD_HEADS], dal2[1, :, :SSD_HEADS]
    g["ssd_dt_bias_fwd"], g["ssd_dt_bias_bwd"] = ddb2[0, :, :SSD_HEADS], ddb2[1, :, :SSD_HEADS]
    cots = [[(dxs2, 0), (dxs2, 1), (dxs_gate, None)], [(dbm2, 0), (dbm2, 1)], [(dcm2, 0), (dcm2, 1)]]
    dpres, dcw, dcb = [], [], []
    for i, (pre, (w, b), cot) in enumerate(zip(pres, conv_parts, cots)):
        dp, dw_, db_ = _conv_silu_bwd(pre, w, b, cot, bl, min(256, pre.shape[1]), f"ssd_conv_bwd_{i}")
        dpres.append(dp)
        dcw.append(dw_)
        dcb.append(db_)
    g["ssd_conv_w"] = jnp.concatenate(dcw, 1)
    g["ssd_conv_b"] = jnp.concatenate(dcb, 1)
    dprojs = [dz, dpres[0], dpres[1], dpres[2], ddtr, du]
    dws = [_matmul_tn(hn, dpj, tm, dpj.shape[1], f"in_proj_dw_{i}") for i, dpj in enumerate(dprojs)]
    dws[4] = jnp.concatenate([dws[4][:, :SSD_HEADS], dws[4][:, LANES:LANES + SSD_HEADS]], 1)
    g["w_in"] = jnp.concatenate(dws, 1)
    dhn = _matmul_nt(dprojs, in_ws, tm, "in_proj_dx")
    grad_x, g["norm_mix_w"] = _norm_bwd(x, p["norm_mix_w"], dhn, dh1, tm, "mix_norm_bwd")
    return loss, grad_x, g


_WEIGHTS = ['norm_mix_w', 'w_in', 'ssd_conv_w', 'ssd_conv_b', 'ssd_dt_bias_fwd', 'ssd_dt_bias_bwd', 'ssd_a_log_fwd', 'ssd_a_log_bwd',
            'ssd_d', 'ssd_norm_w', 's5_lambda_re_fwd', 's5_lambda_im_fwd', 's5_log_step_fwd', 's5_lambda_re_bwd', 's5_lambda_im_bwd',
            's5_log_step_bwd', 's5_b_re', 's5_b_im', 's5_c_re_fwd', 's5_c_im_fwd', 's5_c_re_bwd', 's5_c_im_bwd', 's5_d', 's5_glu_w',
            's5_glu_b', 's5_norm_w', 'w_out', 'norm_ffn_w', 'ffn_w_up', 'ffn_conv_w', 'ffn_conv_b', 'ffn_w_down', 'norm_final_w']
_BIG = ('w_in', 'w_out', 'ffn_w_up', 'ffn_w_down')
_CONV = ('ssd_conv_w', 'ffn_conv_w')


def _pack(arrs):
    flat = jnp.concatenate([a.reshape(-1) for a in arrs])
    rows = -(-flat.shape[0] // (64 * LANES)) * 64
    return jnp.pad(flat, (0, rows * LANES - flat.shape[0])).reshape(rows, LANES)


def _unpack(buf, shapes):
    flat = buf.reshape(-1)
    out, off = [], 0
    for shp in shapes:
        size = math.prod(shp)
        out.append(flat[off:off + size].reshape(shp))
        off += size
    return out


def kernel(x, norm_mix_w, w_in, ssd_conv_w, ssd_conv_b, ssd_dt_bias_fwd, ssd_dt_bias_bwd, ssd_a_log_fwd, ssd_a_log_bwd, ssd_d, ssd_norm_w, s5_lambda_re_fwd, s5_lambda_im_fwd, s5_log_step_fwd, s5_lambda_re_bwd, s5_lambda_im_bwd, s5_log_step_bwd, s5_b_re, s5_b_im, s5_c_re_fwd, s5_c_im_fwd, s5_c_re_bwd, s5_c_im_bwd, s5_d, s5_glu_w, s5_glu_b, s5_norm_w, w_out, norm_ffn_w, ffn_w_up, ffn_conv_w, ffn_conv_b, ffn_w_down, norm_final_w, loss_target, m_norm_mix_w, m_w_in, m_ssd_conv_w, m_ssd_conv_b, m_ssd_dt_bias_fwd, m_ssd_dt_bias_bwd, m_ssd_a_log_fwd, m_ssd_a_log_bwd, m_ssd_d, m_ssd_norm_w, m_s5_lambda_re_fwd, m_s5_lambda_im_fwd, m_s5_log_step_fwd, m_s5_lambda_re_bwd, m_s5_lambda_im_bwd, m_s5_log_step_bwd, m_s5_b_re, m_s5_b_im, m_s5_c_re_fwd, m_s5_c_im_fwd, m_s5_c_re_bwd, m_s5_c_im_bwd, m_s5_d, m_s5_glu_w, m_s5_glu_b, m_s5_norm_w, m_w_out, m_norm_ffn_w, m_ffn_w_up, m_ffn_conv_w, m_ffn_conv_b, m_ffn_w_down, m_norm_final_w, v_norm_mix_w, v_w_in, v_ssd_conv_w, v_ssd_conv_b, v_ssd_dt_bias_fwd, v_ssd_dt_bias_bwd, v_ssd_a_log_fwd, v_ssd_a_log_bwd, v_ssd_d, v_ssd_norm_w, v_s5_lambda_re_fwd, v_s5_lambda_im_fwd, v_s5_log_step_fwd, v_s5_lambda_re_bwd, v_s5_lambda_im_bwd, v_s5_log_step_bwd, v_s5_b_re, v_s5_b_im, v_s5_c_re_fwd, v_s5_c_im_fwd, v_s5_c_re_bwd, v_s5_c_im_bwd, v_s5_d, v_s5_glu_w, v_s5_glu_b, v_s5_norm_w, v_w_out, v_norm_ffn_w, v_ffn_w_up, v_ffn_conv_w, v_ffn_conv_b, v_ffn_w_down, v_norm_final_w):
    args = dict(locals())
    w = {k_: args[k_] for k_ in _WEIGHTS}
    m = {k_: args["m_" + k_] for k_ in _WEIGHTS}
    v = {k_: args["v_" + k_] for k_ in _WEIGHTS}
    bl, sl, d = x.shape
    chip = 2 * lax.axis_index("x") + lax.axis_index("y")
    core = lax.axis_index("c")

    shards = [w[k_][0].astype(BF16) for k_ in _BIG] + [w[k_][0] for k_ in _CONV]
    split = [s.reshape(2, s.shape[0] // 2, s.shape[1]) for s in shards[:len(_BIG)]]
    gathered = _gather_xy(split, shards[len(_BIG):], "gather_weights")
    g_in, g_out, g_up, g_down, g_scw, g_fcw = [got.reshape((4,) + s.shape) for got, s in zip(gathered, shards)]

    def cols(a):
        return jnp.moveaxis(a, 0, 1).reshape(a.shape[1], 4 * a.shape[2])

    p = {k_: (w[k_][0] if w[k_].ndim >= 3 else w[k_]) for k_ in _WEIGHTS if k_ not in _BIG + _CONV}
    p["w_in"], p["w_up"] = cols(g_in), cols(g_up)
    p["w_out"], p["w_down"] = g_out.reshape(-1, g_out.shape[2]), g_down.reshape(-1, g_down.shape[2])
    p["ssd_conv_w"], p["ffn_conv_w"] = cols(g_scw), cols(g_fcw)

    loss, grad_x, g = _local_step(x.reshape(bl * sl, d), loss_target.reshape(bl * sl, d), p, bl)
    g["ffn_w_up"], g["ffn_w_down"] = g.pop("w_up"), g.pop("w_down")

    def owner_major(a, k_):
        r, c = w[k_].shape[1:]
        if a.shape[0] == r:
            a = jnp.moveaxis(a.reshape(r, 4, c), 1, 0)
        else:
            a = a.reshape(4, r, c)
        return a.reshape(4, 2, r // 2, c)

    small = [k_ for k_ in _WEIGHTS if k_ not in _BIG]
    small_full_shapes = [g[k_].shape for k_ in small]
    buf = _pack([g[k_] for k_ in small] + [loss[0, :1]])
    parts = [jnp.moveaxis(owner_major(g[k_], k_), 1, 0) for k_ in _BIG]
    parts.append(jnp.moveaxis(buf.reshape(4, 2, -1, LANES), 1, 0))
    got = _swap_sibling(parts, True, "reduce_sibling")
    chip_sums = [_add_half(pt, gt, core, BF16 if i < len(_BIG) else F32, f"reduce_add_{i}") for i, (pt, gt) in enumerate(zip(parts, got))]
    from_chips = _scatter_xy(chip_sums, "reduce_chips")
    halves = [_sum_lead(a.reshape(4, -1, a.shape[-1]), f"reduce_sum_{i}") for i, a in enumerate(from_chips)]
    other = _swap_sibling(halves[:-1], False, "reduce_join")
    big_grad = {}
    for k_, own_half, sib_half in zip(_BIG, halves, other):
        south = core == 0
        full = jnp.stack([jnp.where(south, own_half, sib_half), jnp.where(south, sib_half, own_half)])
        big_grad[k_] = full.reshape((1,) + w[k_].shape[1:])
    tot = _bcast_all(halves[-1], "reduce_small").reshape(buf.shape)
    unp = _unpack(tot, small_full_shapes + [(1,)])
    small_grad = dict(zip(small, unp[:-1]))
    loss_out = unp[-1].reshape(())
    for k_ in _CONV:
        cshard = w[k_].shape[2]
        small_grad[k_] = lax.dynamic_slice_in_dim(small_grad[k_], chip * cshard, cshard, 1)

    grads, deltas, new_m, new_v = {}, {}, {}, {}
    for k_ in _BIG:
        shp = w[k_].shape
        grads[k_] = big_grad[k_]
        dl, nm, nv = _adamw(w[k_][0], big_grad[k_][0], m[k_][0], v[k_][0], f"adamw_{k_}")
        deltas[k_], new_m[k_], new_v[k_] = dl.reshape(shp), nm.reshape(shp), nv.reshape(shp)
    sw_ = _pack([w[k_] for k_ in small])
    sg_ = _pack([small_grad[k_] for k_ in small])
    sm_ = _pack([m[k_] for k_ in small])
    sv_ = _pack([v[k_] for k_ in small])
    dl, nm, nv = _adamw(sw_, sg_, sm_, sv_, "adamw_small")
    shapes = [w[k_].shape for k_ in small]
    for k_, a, b, c_ in zip(small, _unpack(dl, shapes), _unpack(nm, shapes), _unpack(nv, shapes)):
        deltas[k_], new_m[k_], new_v[k_] = a, b, c_
        grads[k_] = small_grad[k_].reshape(w[k_].shape)
    return (loss_out, grad_x.reshape(bl, sl, d), *[grads[k_] for k_ in _WEIGHTS], *[deltas[k_] for k_ in _WEIGHTS],
            *[new_m[k_] for k_ in _WEIGHTS], *[new_v[k_] for k_ in _WEIGHTS])
```

```python
import functools
import math

import jax
import jax.numpy as jnp
from jax import lax
from jax.experimental import pallas as pl
from jax.experimental.pallas import tpu as pltpu
from jax.experimental.pallas import tpu_sc as plsc

F32 = jnp.float32
BF16 = jnp.bfloat16
HI = lax.Precision.HIGHEST
SDS = jax.ShapeDtypeStruct
MESH = pl.DeviceIdType.MESH

NN = (((1,), (0,)), ((), ()))
NT = (((1,), (1,)), ((), ()))
TN = (((0,), (0,)), ((), ()))

EPS = 1e-6
SSD_HEADS = 16
SSD_HEAD_DIM = 64
SSD_GROUPS = 4
SSD_STATE = 128
SSD_CHUNK = 128
SSD_CONV = 5
S5_GROUPS = 32
S5_CH = 16
S5_STATE = 64
S5_T = 16
LANES = 128
ADAM_LR, ADAM_B1, ADAM_B2, ADAM_EPS, ADAM_WD, ADAM_STEP = 0.001, 0.9, 0.999, 1e-08, 0.01, 10
V7X_VMEM_BYTES = 64 * 1024 * 1024
VMEM_LIMIT = V7X_VMEM_BYTES - 8 * 1024 * 1024


def _cp(sem, vmem=None):
    return pltpu.CompilerParams(dimension_semantics=sem, vmem_limit_bytes=vmem)


def _dot(a, b, dims=NN, precision=None):
    return lax.dot_general(a, b, dims, precision=precision, preferred_element_type=F32)


def _rms(x, w):
    return x * lax.rsqrt(jnp.mean(x * x, axis=-1, keepdims=True) + EPS) * w


def _sigmoid(x):
    return 1.0 / (1.0 + jnp.exp(-x))


def _softplus(x):
    return jnp.maximum(x, 0.0) + jnp.log1p(jnp.exp(-jnp.abs(x)))


@functools.partial(jax.custom_vjp, nondiff_argnums=(1, 2))
def _shift(x, k, seg):
    n = x.shape[0]
    r = lax.broadcasted_iota(jnp.int32, x.shape, 0) % seg
    y = pltpu.roll(x, k % n, 0)
    ok = (r >= k) if k > 0 else (r < seg + k)
    return jnp.where(ok, y, 0.0)


def _shift_fwd(x, k, seg):
    return _shift(x, k, seg), None


def _shift_bwd(k, seg, _, g):
    return (_shift(g, -k, seg),)


_shift.defvjp(_shift_fwd, _shift_bwd)


@functools.partial(jax.custom_vjp, nondiff_argnums=(1,))
def _lane_shift(x, k):
    if k == 0:
        return x
    n = x.shape[1]
    lane = lax.broadcasted_iota(jnp.int32, x.shape, 1)
    ok = (lane >= k) if k > 0 else (lane < n + k)
    return jnp.where(ok, pltpu.roll(x, k % n, 1), 0.0)


_lane_shift.defvjp(lambda x, k: (_lane_shift(x, k), None), lambda k, _, g: (_lane_shift(g, -k),))


@jax.custom_vjp
def _swap(z):
    return pltpu.roll(z, LANES // 2, 1)


_swap.defvjp(lambda z: (_swap(z), None), lambda _, g: (_swap(g),))


def _norm_matmul(x, nw, ws, tm, name):
    n, d = x.shape
    k = len(ws)

    def body(x_ref, nw_ref, *refs):
        hn = _rms(x_ref[...], nw_ref[...]).astype(BF16)
        refs[k][...] = hn
        for w_ref, o_ref in zip(refs[:k], refs[k + 1:]):
            o_ref[...] = _dot(hn, w_ref[...])

    row = lambda i: (i, 0)
    fix = lambda i: (0, 0)
    return pl.pallas_call(
        body, name=name, grid=(n // tm,),
        in_specs=[pl.BlockSpec((tm, d), row), pl.BlockSpec((1, d), fix)] + [pl.BlockSpec(w.shape, fix) for w in ws],
        out_specs=[pl.BlockSpec((tm, d), row)] + [pl.BlockSpec((tm, w.shape[1]), row) for w in ws],
        out_shape=[SDS((n, d), BF16)] + [SDS((n, w.shape[1]), F32) for w in ws],
        compiler_params=_cp(("arbitrary",), VMEM_LIMIT),
    )(x, nw, *ws)


def _matmul_res(a, w, res, tm, name):
    n, kd = a.shape
    m = w.shape[1]

    def body(a_ref, w_ref, r_ref, o_ref):
        o_ref[...] = r_ref[...] + _dot(a_ref[...], w_ref[...])

    return pl.pallas_call(
        body, name=name, grid=(n // tm,),
        in_specs=[pl.BlockSpec((tm, kd), lambda i: (i, 0)), pl.BlockSpec((kd, m), lambda i: (0, 0)),
                  pl.BlockSpec((tm, m), lambda i: (i, 0))],
        out_specs=pl.BlockSpec((tm, m), lambda i: (i, 0)),
        out_shape=SDS((n, m), F32),
        compiler_params=_cp(("arbitrary",), VMEM_LIMIT),
    )(a, w, res)


def _matmul_nt(gs, ws, tm, name):
    n = gs[0].shape[0]
    kd = ws[0].shape[0]
    cnt = len(gs)

    def body(*refs):
        acc = None
        for g_ref, w_ref in zip(refs[:cnt], refs[cnt:2 * cnt]):
            t = _dot(g_ref[...].astype(BF16), w_ref[...], NT)
            acc = t if acc is None else acc + t
        refs[2 * cnt][...] = acc

    return pl.pallas_call(
        body, name=name, grid=(n // tm,),
        in_specs=[pl.BlockSpec((tm, g.shape[1]), lambda i: (i, 0)) for g in gs]
        + [pl.BlockSpec(w.shape, lambda i: (0, 0)) for w in ws],
        out_specs=pl.BlockSpec((tm, kd), lambda i: (i, 0)),
        out_shape=SDS((n, kd), F32),
        compiler_params=_cp(("arbitrary",), VMEM_LIMIT),
    )(*gs, *ws)


def _matmul_tn(a, g, tm, cb, name):
    n, kd = a.shape
    m = g.shape[1]

    def body(a_ref, g_ref, o_ref):
        t = _dot(a_ref[...], g_ref[...].astype(BF16), TN)

        @pl.when(pl.program_id(1) == 0)
        def _():
            o_ref[...] = t

        @pl.when(pl.program_id(1) != 0)
        def _():
            o_ref[...] += t

    return pl.pallas_call(
        body, name=name, grid=(m // cb, n // tm),
        in_specs=[pl.BlockSpec((tm, kd), lambda j, i: (i, 0)), pl.BlockSpec((tm, cb), lambda j, i: (i, j))],
        out_specs=pl.BlockSpec((kd, cb), lambda j, i: (0, j)),
        out_shape=SDS((kd, m), F32),
        compiler_params=_cp(("arbitrary", "arbitrary"), VMEM_LIMIT),
    )(a, g)


def _dwconv(x, w, b):
    kw = w.shape[0]
    acc = b
    for k in range(kw):
        acc = acc + w[k:k + 1, :] * _shift(x, kw // 2 - k, x.shape[0])
    return acc


def _conv_silu_fn(x, w, b):
    y = _dwconv(x, w, b)
    return y * _sigmoid(y)


def _conv_glu_fn(v, g, wv, wg, bv, bg):
    cv = _dwconv(v, wv, bv)
    cg = _dwconv(g, wg, bg)
    return cg * _sigmoid(cg) * cv


def _conv_silu(x, w, b, bl, cb, name):
    n, c = x.shape
    sl = n // bl
    kw = w.shape[0]

    def body(x_ref, w_ref, b_ref, o_ref):
        o_ref[...] = _conv_silu_fn(x_ref[...], w_ref[...], b_ref[...])

    return pl.pallas_call(
        body, name=name, grid=(bl, c // cb),
        in_specs=[pl.BlockSpec((sl, cb), lambda s, j: (s, j)), pl.BlockSpec((kw, cb), lambda s, j: (0, j)),
                  pl.BlockSpec((1, cb), lambda s, j: (0, j))],
        out_specs=pl.BlockSpec((sl, cb), lambda s, j: (s, j)),
        out_shape=SDS((n, c), F32),
        compiler_params=_cp(("arbitrary", "arbitrary"), VMEM_LIMIT),
    )(x, w, b)


def _conv_silu_bwd(x, w, b, dys, bl, cb, name):
    n, c = x.shape
    sl = n // bl
    kw = w.shape[0]
    cnt = len(dys)

    def body(x_ref, w_ref, b_ref, *refs):
        dy = refs[0][...]
        for r in refs[1:cnt]:
            dy = dy + r[...]
        dx_ref, dw_ref, db_ref = refs[cnt:]
        _, vjp = jax.vjp(_conv_silu_fn, x_ref[...], w_ref[...], b_ref[...])
        dx, dw, db = vjp(dy)
        dx_ref[...] = dx.astype(BF16)

        @pl.when(pl.program_id(1) == 0)
        def _():
            dw_ref[...] = dw
            db_ref[...] = db

        @pl.when(pl.program_id(1) != 0)
        def _():
            dw_ref[...] += dw
            db_ref[...] += db

    dy_specs = []
    for arr, lead in dys:
        if lead is None:
            dy_specs.append(pl.BlockSpec((sl, cb), lambda j, s: (s, j)))
        else:
            dy_specs.append(pl.BlockSpec((None, sl, cb), functools.partial(lambda j, s, lead: (lead, s, j), lead=lead)))
    return pl.pallas_call(
        body, name=name, grid=(c // cb, bl),
        in_specs=[pl.BlockSpec((sl, cb), lambda j, s: (s, j)), pl.BlockSpec((kw, cb), lambda j, s: (0, j)),
                  pl.BlockSpec((1, cb), lambda j, s: (0, j))] + dy_specs,
        out_specs=[pl.BlockSpec((sl, cb), lambda j, s: (s, j)), pl.BlockSpec((kw, cb), lambda j, s: (0, j)),
                   pl.BlockSpec((1, cb), lambda j, s: (0, j))],
        out_shape=[SDS((n, c), BF16), SDS((kw, c), F32), SDS((1, c), F32)],
        compiler_params=_cp(("arbitrary", "arbitrary"), VMEM_LIMIT),
    )(x, w, b, *[a for a, _ in dys])


def _conv_glu(v, g, wv, wg, bv, bg, bl, cb, name):
    n, c = v.shape
    sl = n // bl
    kw = wv.shape[0]

    def body(v_ref, g_ref, wv_ref, wg_ref, bv_ref, bg_ref, o_ref):
        o_ref[...] = _conv_glu_fn(v_ref[...], g_ref[...], wv_ref[...], wg_ref[...], bv_ref[...], bg_ref[...]).astype(BF16)

    big = pl.BlockSpec((sl, cb), lambda s, j: (s, j))
    wsp = pl.BlockSpec((kw, cb), lambda s, j: (0, j))
    bsp = pl.BlockSpec((1, cb), lambda s, j: (0, j))
    return pl.pallas_call(
        body, name=name, grid=(bl, c // cb),
        in_specs=[big, big, wsp, wsp, bsp, bsp], out_specs=big, out_shape=SDS((n, c), BF16),
        compiler_params=_cp(("arbitrary", "arbitrary"), VMEM_LIMIT),
    )(v, g, wv, wg, bv, bg)


def _conv_glu_bwd(v, g, wv, wg, bv, bg, dact, bl, cb, name):
    n, c = v.shape
    sl = n // bl
    kw = wv.shape[0]

    def body(v_ref, g_ref, wv_ref, wg_ref, bv_ref, bg_ref, da_ref, dv_ref, dg_ref, dwv_ref, dwg_ref, dbv_ref, dbg_ref):
        _, vjp = jax.vjp(_conv_glu_fn, v_ref[...], g_ref[...], wv_ref[...], wg_ref[...], bv_ref[...], bg_ref[...])
        dv, dg, dwv, dwg, dbv, dbg = vjp(da_ref[...])
        dv_ref[...] = dv.astype(BF16)
        dg_ref[...] = dg.astype(BF16)

        @pl.when(pl.program_id(1) == 0)
        def _():
            dwv_ref[...] = dwv
            dwg_ref[...] = dwg
            dbv_ref[...] = dbv
            dbg_ref[...] = dbg

        @pl.when(pl.program_id(1) != 0)
        def _():
            dwv_ref[...] += dwv
            dwg_ref[...] += dwg
            dbv_ref[...] += dbv
            dbg_ref[...] += dbg

    big = pl.BlockSpec((sl, cb), lambda j, s: (s, j))
    wsp = pl.BlockSpec((kw, cb), lambda j, s: (0, j))
    bsp = pl.BlockSpec((1, cb), lambda j, s: (0, j))
    return pl.pallas_call(
        body, name=name, grid=(c // cb, bl),
        in_specs=[big, big, wsp, wsp, bsp, bsp, big],
        out_specs=[big, big, wsp, wsp, bsp, bsp],
        out_shape=[SDS((n, c), BF16), SDS((n, c), BF16), SDS((kw, c), F32), SDS((kw, c), F32), SDS((1, c), F32), SDS((1, c), F32)],
        compiler_params=_cp(("arbitrary", "arbitrary"), VMEM_LIMIT),
    )(v, g, wv, wg, bv, bg, dact)


_DIMS_T = {NN: (NT, TN, False, False), NT: (NN, TN, False, True), TN: (NT, NN, True, False)}


@functools.partial(jax.custom_vjp, nondiff_argnums=(2,))
def _bdot(a, b, dims):
    return _dot(a.astype(BF16), b.astype(BF16), dims)


def _bdot_fwd(a, b, dims):
    return _bdot(a, b, dims), (a, b)


def _bdot_bwd(dims, res, g):
    a, b = res
    da_dims, db_dims, a_swapped, b_swapped = _DIMS_T[dims]
    da = _bdot(b, g, da_dims) if a_swapped else _bdot(g, b, da_dims)
    db = _bdot(g, a, db_dims) if b_swapped else _bdot(a, g, db_dims)
    return da, db


_bdot.defvjp(_bdot_fwd, _bdot_bwd)


@functools.partial(jax.custom_vjp, nondiff_argnums=(1,))
def _expand_heads(v, width):
    return _split_dot(v, _head_matrix(width), NN)


def _head_matrix(width):
    hr = lax.broadcasted_iota(jnp.int32, (LANES, SSD_HEADS * width), 0)
    hc = lax.broadcasted_iota(jnp.int32, (LANES, SSD_HEADS * width), 1)
    return (hc // width == hr).astype(BF16)


def _split_dot(v, e, dims):
    hi = v.astype(BF16)
    lo = (v - hi.astype(F32)).astype(BF16)
    return _dot(hi, e, dims) + _dot(lo, e, dims)


_expand_heads.defvjp(lambda v, width: (_expand_heads(v, width), None),
                     lambda width, _, g: (_split_dot(g, _head_matrix(width), NT),))


def _ssd_chunk_fn(rev, xs, dtr, bms, cms, st, alog, dtb):
    q = dtr.shape[0]
    hd, per = SSD_HEAD_DIM, SSD_HEADS // SSD_GROUPS
    gw = per * hd
    r = lax.broadcasted_iota(jnp.int32, (q, q), 0)
    c = lax.broadcasted_iota(jnp.int32, (q, q), 1)
    sgn = 1 - 2 * rev
    tri = ((r - c) * sgn >= 0).astype(F32)
    tri_t = ((c - r) * sgn >= 0).astype(F32)
    r4 = lax.broadcasted_iota(jnp.int32, (q, per * q), 0)
    c4 = lax.broadcasted_iota(jnp.int32, (q, per * q), 1) % q
    mask4 = (r4 - c4) * sgn >= 0
    bdr = lax.broadcasted_iota(jnp.int32, (per * q, gw), 0) // q
    bdc = lax.broadcasted_iota(jnp.int32, (per * q, gw), 1) // hd
    diag = bdr == bdc
    dt = _softplus(dtr + dtb)
    dta = dt * (-jnp.exp(alog))
    cs = _dot(tri, dta, NN, HI)
    cs_t = _dot(dta, tri_t, TN, HI)
    tot = jnp.sum(dta, axis=0, keepdims=True)
    dt_x = _expand_heads(dt, hd)
    in_x = _expand_heads(jnp.exp(cs), hd)
    out_x = _expand_heads(jnp.exp(tot - cs), hd)
    ys, outs = [], []
    for g in range(SSD_GROUPS):
        bg, cg = bms[g], cms[g]
        heads = range(per * g, per * (g + 1))
        lanes = slice(gw * g, gw * (g + 1))
        scores = _bdot(cg, bg, NT)
        col = jnp.concatenate([jnp.broadcast_to(cs[:, h:h + 1], (q, q)) for h in heads], axis=1)
        row = jnp.concatenate([cs_t[h:h + 1, :] for h in heads], axis=1)
        seg = jnp.where(mask4, jnp.exp(jnp.where(mask4, col - row, 0.0)), 0.0)
        mcat = jnp.concatenate([scores] * per, axis=1) * seg
        xdt = xs[g] * dt_x[:, lanes]
        blocks = jnp.where(diag, jnp.concatenate([xdt] * per, axis=0), 0.0)
        y = _bdot(mcat, blocks, NN) + in_x[:, lanes] * _bdot(cg, st[g], NT)
        new = _bdot(xdt * out_x[:, lanes], bg, TN)
        keep = jnp.concatenate([jnp.exp(tot[:, h:h + 1]) * st[g][hd * j:hd * (j + 1), :] for j, h in enumerate(heads)], axis=0)
        ys.append(y)
        outs.append(keep + new)
    return ys, outs


def _ssd_scan(xs, bm, cm, dtr, alog2, dtb2, bl, name):
    n = xs.shape[0]
    q = SSD_CHUNK
    nc = n // bl // q
    hd, ns = SSD_HEAD_DIM, SSD_STATE
    gw = SSD_HEADS // SSD_GROUPS * hd

    def body(xs_ref, b_ref, c_ref, dt_ref, al_ref, db_ref, y_ref, sv_ref, st_ref):
        d, i = pl.program_id(0), pl.program_id(2)

        @pl.when(i == 0)
        def _():
            st_ref[...] = jnp.zeros(st_ref.shape, F32)

        st = [st_ref[gw * g:gw * (g + 1), :] for g in range(SSD_GROUPS)]
        sv_ref[...] = st_ref[...]
        xl = [xs_ref[:, gw * g:gw * (g + 1)] for g in range(SSD_GROUPS)]
        bms = [b_ref[:, ns * g:ns * (g + 1)] for g in range(SSD_GROUPS)]
        cms = [c_ref[:, ns * g:ns * (g + 1)] for g in range(SSD_GROUPS)]
        ys, outs = _ssd_chunk_fn(d, xl, dt_ref[...], bms, cms, st, al_ref[...], db_ref[...])
        for g in range(SSD_GROUPS):
            st_ref[gw * g:gw * (g + 1), :] = outs[g]
            y_ref[:, gw * g:gw * (g + 1)] = ys[g]

    def rowblk(d, s, i):
        return s * nc + i + d * (nc - 1 - 2 * i)

    return pl.pallas_call(
        body, name=name, grid=(2, bl, nc),
        in_specs=[pl.BlockSpec((q, SSD_HEADS * hd), lambda d, s, i: (rowblk(d, s, i), 0)),
                  pl.BlockSpec((q, SSD_GROUPS * ns), lambda d, s, i: (rowblk(d, s, i), 0)),
                  pl.BlockSpec((q, SSD_GROUPS * ns), lambda d, s, i: (rowblk(d, s, i), 0)),
                  pl.BlockSpec((q, LANES), lambda d, s, i: (rowblk(d, s, i), d)),
                  pl.BlockSpec((None, 1, LANES), lambda d, s, i: (d, 0, 0)),
                  pl.BlockSpec((None, 1, LANES), lambda d, s, i: (d, 0, 0))],
        out_specs=[pl.BlockSpec((None, q, SSD_HEADS * hd), lambda d, s, i: (d, rowblk(d, s, i), 0)),
                   pl.BlockSpec((None, None, SSD_HEADS * hd, ns), lambda d, s, i: (d, rowblk(d, s, i), 0, 0))],
        out_shape=[SDS((2, n, SSD_HEADS * hd), F32), SDS((2, n // q, SSD_HEADS * hd, ns), F32)],
        scratch_shapes=[pltpu.VMEM((SSD_HEADS * hd, ns), F32)],
        compiler_params=_cp(("arbitrary",) * 3, VMEM_LIMIT),
    )(xs, bm, cm, dtr, alog2, dtb2)


def _ssd_scan_bwd(xs, bm, cm, dtr, alog2, dtb2, saved, dy, bl, name):
    n = xs.shape[0]
    q = SSD_CHUNK
    nc = n // bl // q
    hd, ns = SSD_HEAD_DIM, SSD_STATE
    gw = SSD_HEADS // SSD_GROUPS * hd

    def body(xs_ref, b_ref, c_ref, dt_ref, al_ref, db_ref, sv_ref, dy_ref,
             dxs_ref, dbm_ref, dcm_ref, ddt_ref, dal_ref, ddb_ref, ds_ref):
        d, s, i = pl.program_id(0), pl.program_id(1), pl.program_id(2)

        @pl.when(i == 0)
        def _():
            ds_ref[...] = jnp.zeros(ds_ref.shape, F32)

        xl = [xs_ref[:, gw * g:gw * (g + 1)] for g in range(SSD_GROUPS)]
        bms = [b_ref[:, ns * g:ns * (g + 1)] for g in range(SSD_GROUPS)]
        cms = [c_ref[:, ns * g:ns * (g + 1)] for g in range(SSD_GROUPS)]
        st = [sv_ref[gw * g:gw * (g + 1), :] for g in range(SSD_GROUPS)]
        fn = functools.partial(_ssd_chunk_fn, d)
        _, vjp = jax.vjp(fn, xl, dt_ref[...], bms, cms, st, al_ref[...], db_ref[...])
        dys = [dy_ref[:, gw * g:gw * (g + 1)] for g in range(SSD_GROUPS)]
        dso = [ds_ref[gw * g:gw * (g + 1), :] for g in range(SSD_GROUPS)]
        dxl, ddt, dbg, dcg, dst, dal, ddb = vjp((dys, dso))
        for g in range(SSD_GROUPS):
            ds_ref[gw * g:gw * (g + 1), :] = dst[g]
            dxs_ref[:, gw * g:gw * (g + 1)] = dxl[g]
            dbm_ref[:, ns * g:ns * (g + 1)] = dbg[g]
            dcm_ref[:, ns * g:ns * (g + 1)] = dcg[g]
        ddt_ref[...] = ddt
        _acc_rows((dal_ref, ddb_ref), (dal, ddb), jnp.logical_and(s == 0, i == 0))

    def rowblk(d, s, i):
        return s * nc + (nc - 1 - i) + d * (2 * i - (nc - 1))

    row = lambda d, s, i: (rowblk(d, s, i), 0)
    drow = lambda d, s, i: (d, rowblk(d, s, i), 0)
    dfix = lambda d, s, i: (d, 0, 0)
    dcol = lambda d, s, i: (rowblk(d, s, i), d)
    return pl.pallas_call(
        body, name=name, grid=(2, bl, nc),
        in_specs=[pl.BlockSpec((q, SSD_HEADS * hd), row), pl.BlockSpec((q, SSD_GROUPS * ns), row),
                  pl.BlockSpec((q, SSD_GROUPS * ns), row), pl.BlockSpec((q, LANES), dcol),
                  pl.BlockSpec((None, 1, LANES), dfix), pl.BlockSpec((None, 1, LANES), dfix),
                  pl.BlockSpec((None, None, SSD_HEADS * hd, ns), lambda d, s, i: (d, rowblk(d, s, i), 0, 0)),
                  pl.BlockSpec((q, SSD_HEADS * hd), row)],
        out_specs=[pl.BlockSpec((None, q, SSD_HEADS * hd), drow), pl.BlockSpec((None, q, SSD_GROUPS * ns), drow),
                   pl.BlockSpec((None, q, SSD_GROUPS * ns), drow), pl.BlockSpec((q, LANES), dcol),
                   pl.BlockSpec((None, 1, LANES), dfix), pl.BlockSpec((None, 1, LANES), dfix)],
        out_shape=[SDS((2, n, SSD_HEADS * hd), F32), SDS((2, n, SSD_GROUPS * ns), F32), SDS((2, n, SSD_GROUPS * ns), F32),
                   SDS((n, 2 * LANES), F32), SDS((2, 1, LANES), F32), SDS((2, 1, LANES), F32)],
        scratch_shapes=[pltpu.VMEM((SSD_HEADS * hd, ns), F32)],
        compiler_params=_cp(("arbitrary",) * 3, VMEM_LIMIT),
    )(xs, bm, cm, dtr, alog2, dtb2, saved, dy)


def _s5_consts():
    t, ch, p = S5_T, S5_CH, S5_STATE
    lane = lax.broadcasted_iota(jnp.int32, (1, 2 * p), 1)
    pr = lax.broadcasted_iota(jnp.int32, (p, 2 * p), 0)
    pc = lax.broadcasted_iota(jnp.int32, (p, 2 * p), 1)
    cr = lax.broadcasted_iota(jnp.int32, (ch, t * ch), 0)
    cc = lax.broadcasted_iota(jnp.int32, (ch, t * ch), 1)
    return dict(
        left=lane < p,
        sg=jnp.where(lane < p, -1.0, 1.0).astype(F32),
        dup=(pc % p == pr).astype(F32),
        dup_l=(pc == pr).astype(F32),
        dup_r=(pc == pr + p).astype(F32),
        rep=(cc % ch == cr).astype(F32),
        rep0=(cc == cr).astype(F32),
    )


def _s5_mats(k, rev, lr, li, ls, bre, bim, cre, cim):
    t = S5_T
    step = jnp.exp(ls)
    lr2 = jnp.sum(lr * k["dup"], axis=0, keepdims=True)
    li2 = jnp.sum(li * k["dup"], axis=0, keepdims=True)

    def erow(d):
        ang = (d * step) * li2
        return jnp.exp((d * step) * lr2) * jnp.where(k["left"], jnp.cos(ang), jnp.sin(ang))

    es = [erow(d) for d in range(t + 1)]
    mag = jnp.exp(step * lr)
    ar, ai = mag * jnp.cos(step * li), mag * jnp.sin(step * li)
    den = lr * lr + li * li
    zr = ((ar - 1.0) * lr + ai * li) / den
    zi = (ai * lr - (ar - 1.0) * li) / den
    bbr = zr * bre - zi * bim
    bbi = zr * bim + zi * bre
    bt1 = _dot(bbr, k["dup"], TN, HI)
    bt2 = _dot(bbi, k["dup"], TN, HI)
    bst = _dot(bbr, k["dup_l"], TN, HI) - _dot(bbi, k["dup_r"], TN, HI)
    c1 = _dot(cre, k["dup"], NN, HI)
    c2 = _dot(cim, k["dup"], NN, HI)
    sg = k["sg"]
    ce = [e * c1 + sg * _swap(e) * c2 for e in es]
    lags = range(t - 1, -1, -1) if rev else range(t)
    kt = _dot(bst, jnp.concatenate([ce[d] for d in lags], axis=0), NT, HI)
    toep = jnp.concatenate([_lane_shift(kt, -S5_CH * (t - 1 - s) if rev else S5_CH * s) for s in range(t)], axis=0)
    w_out =jnp.concatenate([ce[(t - qq) if rev else (qq + 1)] * (-sg) for qq in range(t)], axis=0)
    w_st = jnp.concatenate(
        [(lambda e: e * bt1 + sg * _swap(e) * bt2)(es[s if rev else (t - 1 - s)]) for s in range(t)], axis=0)
    return toep, w_out, w_st, es[t]


def _cmul_row(k, e, z):
    es = _swap(e)
    return z * jnp.where(k["left"], e, es) + k["sg"] * _swap(z) * jnp.where(k["left"], es, e)


def _s5_dir(k, rev, nck, x, mats):
    toep, w_out, w_st, a_t = mats
    acc = _dot(x, w_st)
    e = a_t
    kk = 1
    sign = -1 if rev else 1
    while kk < nck:
        acc = acc + _cmul_row(k, e, _shift(acc, sign * kk, nck))
        e = _cmul_row(k, e, e)
        kk *= 2
    prev = _shift(acc, sign, nck)
    return _dot(x, toep) + _dot(prev, w_out, NT)


def _s5_group_fn(nck, x, pf, pb, bre, bim, dcol, wv, wg, bv, bg):
    k = _s5_consts()
    t = S5_T
    y = x * jnp.sum(dcol * k["rep"], axis=0, keepdims=True)
    for rev, (lr, li, ls, cre, cim) in ((False, pf), (True, pb)):
        y = y + _s5_dir(k, rev, nck, x, _s5_mats(k, rev, lr, li, ls, bre, bim, cre, cim))
    gy = jax.nn.gelu(y)
    def kron_eye(w16):
        wide = _dot(w16, k["rep0"], NN, HI)
        return jnp.concatenate([_lane_shift(wide, S5_CH * qq) for qq in range(t)], axis=0)

    kv, kg = kron_eye(wv), kron_eye(wg)
    val =_dot(gy, kv) + jnp.sum(bv * k["rep"], axis=0, keepdims=True)
    gate = _dot(gy, kg) + jnp.sum(bg * k["rep"], axis=0, keepdims=True)
    return val * _sigmoid(gate)


def _s5_specs(r):
    p, ch = S5_STATE, S5_CH
    g3 = lambda i: (i, 0, 0)
    col = pl.BlockSpec((None, p, 1), g3)
    one = pl.BlockSpec((None, 1, 1), g3)
    cmat = pl.BlockSpec((None, ch, p), g3)
    bmat = pl.BlockSpec((None, p, ch), g3)
    ccol = pl.BlockSpec((None, ch, 1), g3)
    sq = pl.BlockSpec((None, ch, ch), g3)
    xs = pl.BlockSpec((None, r, S5_T * ch), g3)
    specs = [xs, col, col, one, cmat, cmat, col, col, one, cmat, cmat, bmat, bmat, ccol, sq, sq, ccol, ccol]
    return specs


def _s5_unpack(vals):
    x = vals[0]
    pf = tuple(vals[1:6])
    pb = tuple(vals[6:11])
    bre, bim, dcol, wv, wg, bv, bg = vals[11:18]
    return x, pf, pb, bre, bim, dcol, wv, wg, bv, bg


def _s5_fwd(args, nck, name):
    x = args[0]
    ng, r, w = x.shape

    def body(*refs):
        vals = [ref[...] for ref in refs[:18]]
        refs[18][...] = _s5_group_fn(nck, *_s5_unpack(vals))

    specs = _s5_specs(r)
    return pl.pallas_call(
        body, name=name, grid=(ng,), in_specs=specs, out_specs=specs[0], out_shape=SDS(x.shape, F32),
        compiler_params=_cp(("arbitrary",), VMEM_LIMIT),
    )(*args)


def _s5_bwd(args, dy, nck, name):
    x = args[0]
    ng, r, w = x.shape

    def body(*refs):
        vals = [ref[...] for ref in refs[:18]]
        _, vjp = jax.vjp(lambda *v: _s5_group_fn(nck, *_s5_unpack(v)), *vals)
        grads = vjp(refs[18][...])
        for o_ref, gval in zip(refs[19:], grads):
            o_ref[...] = gval.astype(o_ref.dtype)

    specs = _s5_specs(r)
    return pl.pallas_call(
        body, name=name, grid=(ng,), in_specs=specs + [specs[0]], out_specs=specs,
        out_shape=[SDS(x.shape, BF16)] + [SDS(a.shape, F32) for a in args[1:]],
        compiler_params=_cp(("arbitrary",), VMEM_LIMIT),
    )(*args, dy)


def _mix_fn(yf, yb, xs, z, s5o, dvec, nw_ssd, nw_s5):
    hr = lax.broadcasted_iota(jnp.int32, (LANES, SSD_HEADS * SSD_HEAD_DIM), 0)
    hc = lax.broadcasted_iota(jnp.int32, (LANES, SSD_HEADS * SSD_HEAD_DIM), 1)
    expand = (hc // SSD_HEAD_DIM == hr).astype(F32)
    dch = jnp.sum(dvec * expand, axis=0, keepdims=True)
    y = (yf + yb + dch * xs) * (z * _sigmoid(z))
    return _rms(y, nw_ssd), _rms(s5o, nw_s5)


def _mix(y2, xs, z, s5o, dvec, nw_ssd, nw_s5, tm, name):
    n, c1 = xs.shape
    c2 = s5o.shape[1]

    def body(yf_ref, yb_ref, xs_ref, z_ref, s_ref, d_ref, n1_ref, n2_ref, o_ref):
        o1, o2 = _mix_fn(yf_ref[...], yb_ref[...], xs_ref[...], z_ref[...], s_ref[...], d_ref[...], n1_ref[...], n2_ref[...])
        o_ref[:, :c1] = o1.astype(BF16)
        o_ref[:, c1:] = o2.astype(BF16)

    row = lambda i: (i, 0)
    fix = lambda i: (0, 0)
    return pl.pallas_call(
        body, name=name, grid=(n // tm,),
        in_specs=[pl.BlockSpec((None, tm, c1), lambda i: (0, i, 0)), pl.BlockSpec((None, tm, c1), lambda i: (1, i, 0)),
                  pl.BlockSpec((tm, c1), row), pl.BlockSpec((tm, c1), row), pl.BlockSpec((tm, c2), row),
                  pl.BlockSpec((LANES, 1), fix), pl.BlockSpec((1, c1), fix), pl.BlockSpec((1, c2), fix)],
        out_specs=pl.BlockSpec((tm, c1 + c2), row), out_shape=SDS((n, c1 + c2), BF16),
        compiler_params=_cp(("arbitrary",), VMEM_LIMIT),
    )(y2, y2, xs, z, s5o, dvec, nw_ssd, nw_s5)


def _acc_rows(refs, vals, first):
    @pl.when(first)
    def _():
        for ref, v in zip(refs, vals):
            ref[...] = v

    @pl.when(jnp.logical_not(first))
    def _():
        for ref, v in zip(refs, vals):
            ref[...] += v


def _mix_bwd(y2, xs, z, s5o, dvec, nw_ssd, nw_s5, dmix, tm, name):
    n, c1 = xs.shape
    c2 = s5o.shape[1]

    def body(yf_ref, yb_ref, xs_ref, z_ref, s_ref, d_ref, n1_ref, n2_ref, dm_ref,
             dy_ref, dxs_ref, dz_ref, ds_ref, dd_ref, dn1_ref, dn2_ref):
        _, vjp = jax.vjp(_mix_fn, yf_ref[...], yb_ref[...], xs_ref[...], z_ref[...], s_ref[...], d_ref[...], n1_ref[...], n2_ref[...])
        dyf, _, dxs, dz, ds, dd, dn1, dn2 = vjp((dm_ref[:, :c1], dm_ref[:, c1:]))
        dy_ref[...] = dyf
        dxs_ref[...] = dxs
        dz_ref[...] = dz.astype(BF16)
        ds_ref[...] = ds
        _acc_rows((dd_ref, dn1_ref, dn2_ref), (dd, dn1, dn2), pl.program_id(0) == 0)

    row = lambda i: (i, 0)
    fix = lambda i: (0, 0)
    return pl.pallas_call(
        body, name=name, grid=(n // tm,),
        in_specs=[pl.BlockSpec((None, tm, c1), lambda i: (0, i, 0)), pl.BlockSpec((None, tm, c1), lambda i: (1, i, 0)),
                  pl.BlockSpec((tm, c1), row), pl.BlockSpec((tm, c1), row), pl.BlockSpec((tm, c2), row),
                  pl.BlockSpec((LANES, 1), fix), pl.BlockSpec((1, c1), fix), pl.BlockSpec((1, c2), fix),
                  pl.BlockSpec((tm, c1 + c2), row)],
        out_specs=[pl.BlockSpec((tm, c1), row), pl.BlockSpec((tm, c1), row), pl.BlockSpec((tm, c1), row), pl.BlockSpec((tm, c2), row),
                   pl.BlockSpec((LANES, 1), fix), pl.BlockSpec((1, c1), fix), pl.BlockSpec((1, c2), fix)],
        out_shape=[SDS((n, c1), F32), SDS((n, c1), F32), SDS((n, c1), BF16), SDS((n, c2), F32),
                   SDS((LANES, 1), F32), SDS((1, c1), F32), SDS((1, c2), F32)],
        compiler_params=_cp(("arbitrary",), VMEM_LIMIT),
    )(y2, y2, xs, z, s5o, dvec, nw_ssd, nw_s5, dmix)


def _final_loss(h2, nw, tgt, tm, name):
    n, d = h2.shape

    def loss_fn(h, w, t):
        e = _rms(h, w) - t
        return (0.5 / d) * jnp.sum(e * e)

    def body(h_ref, w_ref, t_ref, l_ref, dh_ref, dw_ref):
        loss, (dh, dw) = jax.value_and_grad(loss_fn, argnums=(0, 1))(h_ref[...], w_ref[...], t_ref[...])
        dh_ref[...] = dh
        _acc_rows((l_ref, dw_ref), (jnp.full((1, LANES), loss, F32), dw), pl.program_id(0) == 0)

    row = lambda i: (i, 0)
    fix = lambda i: (0, 0)
    return pl.pallas_call(
        body, name=name, grid=(n // tm,),
        in_specs=[pl.BlockSpec((tm, d), row), pl.BlockSpec((1, d), fix), pl.BlockSpec((tm, d), row)],
        out_specs=[pl.BlockSpec((1, LANES), fix), pl.BlockSpec((tm, d), row), pl.BlockSpec((1, d), fix)],
        out_shape=[SDS((1, LANES), F32), SDS((n, d), F32), SDS((1, d), F32)],
        compiler_params=_cp(("arbitrary",), VMEM_LIMIT),
    )(h2, nw, tgt)


def _norm_bwd(x, nw, dhn, dres, tm, name):
    n, d = x.shape

    def body(x_ref, w_ref, g_ref, r_ref, dx_ref, dw_ref):
        _, vjp = jax.vjp(_rms, x_ref[...], w_ref[...])
        dx, dw = vjp(g_ref[...])
        dx_ref[...] = r_ref[...] + dx
        _acc_rows((dw_ref,), (dw,), pl.program_id(0) == 0)

    row = lambda i: (i, 0)
    fix = lambda i: (0, 0)
    return pl.pallas_call(
        body, name=name, grid=(n // tm,),
        in_specs=[pl.BlockSpec((tm, d), row), pl.BlockSpec((1, d), fix), pl.BlockSpec((tm, d), row), pl.BlockSpec((tm, d), row)],
        out_specs=[pl.BlockSpec((tm, d), row), pl.BlockSpec((1, d), fix)],
        out_shape=[SDS((n, d), F32), SDS((1, d), F32)],
        compiler_params=_cp(("arbitrary",), VMEM_LIMIT),
    )(x, nw, dhn, dres)


def _row_tile(n, cap=512):
    for t in range(min(cap, n) // 8 * 8, 7, -8):
        if n % t == 0:
            return t
    return n


def _sum_lead(a, name):
    kk, n, c = a.shape
    tm = _row_tile(n)

    def body(a_ref, o_ref):
        acc = a_ref[0].astype(F32)
        for i in range(1, kk):
            acc = acc + a_ref[i].astype(F32)
        o_ref[...] = acc

    return pl.pallas_call(
        body, name=name, grid=(n // tm,),
        in_specs=[pl.BlockSpec((kk, tm, c), lambda i: (0, i, 0))],
        out_specs=pl.BlockSpec((tm, c), lambda i: (i, 0)), out_shape=SDS((n, c), F32),
        compiler_params=_cp(("arbitrary",), VMEM_LIMIT),
    )(a)


def _adamw(w, g, m, v, name):
    n, c = w.shape
    tm = _row_tile(n)

    def body(w_ref, g_ref, m_ref, v_ref, d_ref, nm_ref, nv_ref):
        gv = g_ref[...]
        mn = ADAM_B1 * m_ref[...] + (1.0 - ADAM_B1) * gv
        vn = ADAM_B2 * v_ref[...] + (1.0 - ADAM_B2) * jnp.square(gv)
        m_hat = mn / (1.0 - ADAM_B1 ** ADAM_STEP)
        v_hat = vn / (1.0 - ADAM_B2 ** ADAM_STEP)
        d_ref[...] = -ADAM_LR * (m_hat / (jnp.sqrt(v_hat) + ADAM_EPS) + ADAM_WD * w_ref[...])
        nm_ref[...] = mn
        nv_ref[...] = vn

    spec = pl.BlockSpec((tm, c), lambda i: (i, 0))
    return pl.pallas_call(
        body, name=name, grid=(n // tm,), in_specs=[spec] * 4, out_specs=[spec] * 3,
        out_shape=[SDS((n, c), F32)] * 3, compiler_params=_cp(("arbitrary",), VMEM_LIMIT),
    )(w, g, m, v)


ANY = pl.BlockSpec(memory_space=pl.ANY)


def _me():
    return lax.axis_index("x"), lax.axis_index("y"), lax.axis_index("c")


def _gather_xy(split, whole, name):
    ns, cnt = len(split), len(split) + len(whole)

    def body(*refs):
        src, dst = refs[:cnt], refs[cnt:2 * cnt]
        send, recv = refs[2 * cnt:]
        x, y, c = _me()
        mine = 2 * x + y
        chips = [(1 - x, y), (x, 1 - y), (1 - x, 1 - y)]

        def ici(a, j, slot):
            px, py = chips[j]
            if a < ns:
                s_ref, d_ref = src[a].at[c], dst[a].at[slot].at[c]
            else:
                s_ref, d_ref = src[a], dst[a].at[slot]
            return pltpu.make_async_remote_copy(s_ref, d_ref, send.at[3 * a + j], recv.at[3 * a + j],
                                                device_id=(px, py, c), device_id_type=MESH)

        def d2d(a, j, half):
            px, py = chips[j]
            ref = dst[a].at[2 * px + py].at[half]
            return pltpu.make_async_remote_copy(ref, ref, send.at[3 * cnt + 3 * a + j], recv.at[3 * cnt + 3 * a + j],
                                                device_id=(x, y, 1 - c), device_id_type=MESH)

        def own(a):
            return pltpu.make_async_remote_copy(src[a], dst[a].at[mine], send.at[nsem - cnt + a], recv.at[nsem - cnt + a],
                                                device_id=(x, y, 1 - c), device_id_type=MESH)

        started = []
        for a in range(cnt):
            cp = own(a)
            cp.start()
            started.append(cp)
            for j in range(3):
                cp = ici(a, j, mine)
                cp.start()
                started.append(cp)
        for a in range(cnt):
            for j, (px, py) in enumerate(chips):
                ici(a, j, 2 * px + py).wait_recv()
                if a < ns:
                    cp = d2d(a, j, c)
                    cp.start()
                    started.append(cp)
        for a in range(ns):
            for j in range(3):
                d2d(a, j, 1 - c).wait_recv()
        for a in range(cnt):
            own(a).wait_recv()
        for cp in started:
            cp.wait_send()

    nsem = 3 * cnt + 3 * ns + cnt
    return pl.pallas_call(
        body, name=name, in_specs=[ANY] * cnt, out_specs=[ANY] * cnt,
        out_shape=[SDS((4,) + s.shape, s.dtype) for s in split + whole],
        scratch_shapes=[pltpu.SemaphoreType.DMA((nsem,)), pltpu.SemaphoreType.DMA((nsem,))],
    )(*split, *whole)


GATHER_BEHIND_ID = 1


def _gather_behind(shards, name):
    cnt = len(shards)
    srcs = [jax.new_ref(s, memory_space=pltpu.MemorySpace.HBM) for s in shards]
    dsts = [jax.empty_ref(SDS((4,) + s.shape, s.dtype), memory_space=pltpu.MemorySpace.HBM) for s in shards]

    @pl.kernel(mesh=plsc.ScalarSubcoreMesh(axis_name="sequencer", num_cores=1), name=name,
               scratch_types=(pltpu.SemaphoreType.DMA((4 * cnt,)), pltpu.SemaphoreType.DMA((4 * cnt,))),
               compiler_params=pltpu.CompilerParams(collective_id=GATHER_BEHIND_ID))
    def launch(send, recv):
        x, y, c = _me()
        peers = [(1 - x, y, c), (x, 1 - y, c), (1 - x, 1 - y, c), (x, y, 1 - c)]
        barrier = pltpu.get_barrier_semaphore()
        for peer in peers:
            pl.semaphore_signal(barrier, inc=1, device_id=peer, device_id_type=MESH)
        pl.semaphore_wait(barrier, len(peers))
        mine = 2 * x + y

        def copy(a, j, slot):
            return pltpu.make_async_remote_copy(srcs[a], dsts[a].at[slot], send.at[4 * a + j], recv.at[4 * a + j],
                                                device_id=peers[j], device_id_type=MESH)

        started = []
        for a in range(cnt):
            for j in range(4):
                cp = copy(a, j, mine)
                cp.start()
                started.append(cp)
        for a in range(cnt):
            for j, (px, py, _) in enumerate(peers):
                copy(a, j, 2 * px + py).wait_recv()
        for cp in started:
            cp.wait_send()

    launch()
    return [d[...] for d in dsts]


def _swap_sibling(parts, pick, name):
    cnt = len(parts)

    def body(*refs):
        src, dst = refs[:cnt], refs[cnt:2 * cnt]
        send, recv = refs[2 * cnt:]
        x, y, c = _me()
        cps = []
        for a in range(cnt):
            cp = pltpu.make_async_remote_copy(src[a].at[1 - c] if pick else src[a], dst[a], send.at[a], recv.at[a],
                                              device_id=(x, y, 1 - c), device_id_type=MESH)
            cp.start()
            cps.append(cp)
        for cp in cps:
            cp.wait()

    return pl.pallas_call(
        body, name=name, in_specs=[ANY] * cnt, out_specs=[ANY] * cnt,
        out_shape=[SDS(p.shape[1:] if pick else p.shape, p.dtype) for p in parts],
        scratch_shapes=[pltpu.SemaphoreType.DMA((cnt,)), pltpu.SemaphoreType.DMA((cnt,))],
    )(*parts)


def _scatter_xy(parts, name):
    cnt = len(parts)

    def body(*refs):
        src, dst = refs[:cnt], refs[cnt:2 * cnt]
        send, recv, loc = refs[2 * cnt:]
        x, y, c = _me()
        mine = 2 * x + y
        chips = [(1 - x, y), (x, 1 - y), (1 - x, 1 - y)]
        local = []
        for a in range(cnt):
            cp = pltpu.make_async_copy(src[a].at[mine], dst[a].at[mine], loc.at[a])
            cp.start()
            local.append(cp)
        sends = []
        for a in range(cnt):
            for j, (px, py) in enumerate(chips):
                cp = pltpu.make_async_remote_copy(src[a].at[2 * px + py], dst[a].at[mine], send.at[3 * a + j], recv.at[3 * a + j],
                                                  device_id=(px, py, c), device_id_type=MESH)
                cp.start()
                sends.append(cp)
        for a in range(cnt):
            for j, (px, py) in enumerate(chips):
                pltpu.make_async_remote_copy(src[a].at[mine], dst[a].at[2 * px + py], send.at[3 * a + j], recv.at[3 * a + j],
                                             device_id=(px, py, c), device_id_type=MESH).wait_recv()
        for cp in sends:
            cp.wait_send()
        for cp in local:
            cp.wait()

    return pl.pallas_call(
        body, name=name, in_specs=[ANY] * cnt, out_specs=[ANY] * cnt,
        out_shape=[SDS(p.shape, p.dtype) for p in parts],
        scratch_shapes=[pltpu.SemaphoreType.DMA((3 * cnt,)), pltpu.SemaphoreType.DMA((3 * cnt,)), pltpu.SemaphoreType.DMA((cnt,))],
    )(*parts)


def _bcast_all(buf, name):
    def body(src, dst, send, recv, loc):
        x, y, c = _me()
        mine = 4 * x + 2 * y + c
        own = pltpu.make_async_copy(src, dst.at[mine], loc)
        own.start()
        sends = []
        for k in range(1, 8):
            px, py, pc = x ^ (k >> 2), y ^ ((k >> 1) & 1), c ^ (k & 1)
            cp = pltpu.make_async_remote_copy(src, dst.at[mine], send.at[k - 1], recv.at[k - 1],
                                              device_id=(px, py, pc), device_id_type=MESH)
            cp.start()
            sends.append(cp)
        for k in range(1, 8):
            px, py, pc = x ^ (k >> 2), y ^ ((k >> 1) & 1), c ^ (k & 1)
            pltpu.make_async_remote_copy(src, dst.at[4 * px + 2 * py + pc], send.at[k - 1], recv.at[k - 1],
                                         device_id=(px, py, pc), device_id_type=MESH).wait_recv()
        for cp in sends:
            cp.wait_send()
        own.wait()

    return pl.pallas_call(
        body, name=name, in_specs=[ANY], out_specs=ANY, out_shape=SDS((8,) + buf.shape, buf.dtype),
        scratch_shapes=[pltpu.SemaphoreType.DMA((7,)), pltpu.SemaphoreType.DMA((7,)), pltpu.SemaphoreType.DMA(())],
    )(buf)


def _add_half(parts, got, core, dtype, name):
    shp = got.shape
    a2, b2 = parts.reshape(2, -1, shp[-1]), got.reshape(-1, shp[-1])
    n, c = b2.shape
    tm = _row_tile(n, 256)

    def body(core_ref, a_ref, b_ref, o_ref):
        o_ref[...] = (a_ref[...] + b_ref[...]).astype(dtype)

    spec = pl.BlockSpec((tm, c), lambda i, core_ref: (i, 0))
    grid_spec = pltpu.PrefetchScalarGridSpec(
        num_scalar_prefetch=1, grid=(n // tm,),
        in_specs=[pl.BlockSpec((None, tm, c), lambda i, core_ref: (core_ref[0], i, 0)), spec], out_specs=spec)
    return pl.pallas_call(body, name=name, grid_spec=grid_spec, out_shape=SDS((n, c), dtype),
                          compiler_params=_cp(("arbitrary",), VMEM_LIMIT))(core.reshape(1), a2, b2).reshape(shp)


def _x_layout(u, name):
    n, c = u.shape
    t, ch = S5_T, S5_CH
    gb = LANES // ch
    rows = min(64, n // t)

    def body(u_ref, o_ref):
        for s in range(t):
            us = u_ref[pl.ds(s, rows, stride=t), :]
            for g in range(gb):
                o_ref[g, :, ch * s:ch * (s + 1)] = us[:, ch * g:ch * (g + 1)]

    return pl.pallas_call(
        body, name=name, grid=(n // (rows * t), c // LANES),
        in_specs=[pl.BlockSpec((rows * t, LANES), lambda i, j: (i, j))],
        out_specs=pl.BlockSpec((gb, rows, t * ch), lambda i, j: (j, i, 0)),
        out_shape=SDS((c // ch, n // t, t * ch), F32),
        compiler_params=_cp(("arbitrary", "arbitrary"), VMEM_LIMIT),
    )(u)


def _token_layout(xg, name):
    ng, r, w = xg.shape
    t, ch = S5_T, S5_CH
    gb = LANES // ch
    rows = min(64, r)

    def body(x_ref, o_ref):
        for s in range(t):
            parts = [x_ref[g, :, ch * s:ch * (s + 1)].astype(F32) for g in range(gb)]
            o_ref[pl.ds(s, rows, stride=t), :] = jnp.concatenate(parts, axis=1)

    return pl.pallas_call(
        body, name=name, grid=(r // rows, ng // gb),
        in_specs=[pl.BlockSpec((gb, rows, w), lambda i, j: (j, i, 0))],
        out_specs=pl.BlockSpec((rows * t, LANES), lambda i, j: (i, j)),
        out_shape=SDS((r * t, ng * ch), F32),
        compiler_params=_cp(("arbitrary", "arbitrary"), VMEM_LIMIT),
    )(xg)


def _pad_lanes(a, lanes=LANES):
    return jnp.pad(a, ((0, 0), (0, lanes - a.shape[1])))


def _local_step(x, tgt, p, bl):
    n, d = x.shape
    sw = SSD_HEADS * SSD_HEAD_DIM
    gn = SSD_GROUPS * SSD_STATE
    tm = min(n, 512)
    tm_ffn = min(n, 256)
    nck = n // bl // S5_T
    dff = p["w_down"].shape[0]
    s5w = S5_GROUPS * S5_CH

    w_in = p["w_in"]
    o1, o2, o3, o4 = sw, sw + sw, sw + sw + gn, sw + sw + 2 * gn
    w_z, w_xs, w_b, w_c = w_in[:, :o1], w_in[:, o1:o2], w_in[:, o2:o3], w_in[:, o3:o4]
    w_dt = jnp.concatenate([_pad_lanes(w_in[:, o4:o4 + SSD_HEADS]), _pad_lanes(w_in[:, o4 + SSD_HEADS:o4 + 2 * SSD_HEADS])], 1)
    w_u = w_in[:, o4 + 2 * SSD_HEADS:]
    in_ws = [w_z, w_xs, w_b, w_c, w_dt, w_u]
    cw, cb_ = p["ssd_conv_w"], p["ssd_conv_b"]
    conv_parts = [(cw[:, :sw], cb_[:, :sw]), (cw[:, sw:sw + gn], cb_[:, sw:sw + gn]), (cw[:, sw + gn:], cb_[:, sw + gn:])]
    alog2 = jnp.stack([_pad_lanes(p["ssd_a_log_fwd"]), _pad_lanes(p["ssd_a_log_bwd"])])
    dtb2 = jnp.stack([_pad_lanes(p["ssd_dt_bias_fwd"]), _pad_lanes(p["ssd_dt_bias_bwd"])])
    dvec = _pad_lanes(p["ssd_d"]).reshape(LANES, 1)

    hn, z, xs_pre, b_pre, c_pre, dtr, u = _norm_matmul(x, p["norm_mix_w"], in_ws, tm, "in_proj")
    pres = [xs_pre, b_pre, c_pre]
    acts = [_conv_silu(pre, w, b, bl, min(256, pre.shape[1]), f"ssd_conv_{i}") for i, (pre, (w, b)) in enumerate(zip(pres, conv_parts))]
    xs_a, b_a, c_a = acts
    y2, saved = _ssd_scan(xs_a, b_a, c_a, dtr, alog2, dtb2, bl, "ssd_scan")

    def col(a):
        return a.reshape(a.shape + (1,))

    s5_params = [
        col(p["s5_lambda_re_fwd"]), col(p["s5_lambda_im_fwd"]), p["s5_log_step_fwd"].reshape(S5_GROUPS, 1, 1), p["s5_c_re_fwd"], p["s5_c_im_fwd"],
        col(p["s5_lambda_re_bwd"]), col(p["s5_lambda_im_bwd"]), p["s5_log_step_bwd"].reshape(S5_GROUPS, 1, 1), p["s5_c_re_bwd"], p["s5_c_im_bwd"],
        p["s5_b_re"], p["s5_b_im"], col(p["s5_d"].reshape(S5_GROUPS, S5_CH)),
        p["s5_glu_w"][:, :, :S5_CH], p["s5_glu_w"][:, :, S5_CH:], col(p["s5_glu_b"][:, :S5_CH]), col(p["s5_glu_b"][:, S5_CH:]),
    ]
    s5_args = [_x_layout(u, "s5_u_blocks")] + s5_params
    s5o = _token_layout(_s5_fwd(s5_args, nck, "s5_fwd"), "s5_y_tokens")
    ymix = _mix(y2, xs_a, z, s5o, dvec, p["ssd_norm_w"], p["s5_norm_w"], tm, "mix")
    h1 = _matmul_res(ymix, p["w_out"], x, tm, "out_proj")
    w_up = p["w_up"]
    hn2, up_v, up_g = _norm_matmul(h1, p["norm_ffn_w"], [w_up[:, :dff], w_up[:, dff:]], tm_ffn, "ffn_up")
    fw, fb = p["ffn_conv_w"], p["ffn_conv_b"]
    act = _conv_glu(up_v, up_g, fw[:, :dff], fw[:, dff:], fb[:, :dff], fb[:, dff:], bl, 256, "ffn_conv")
    h2 = _matmul_res(act, p["w_down"], h1, tm, "ffn_down")
    loss, dh2, g_nfw = _final_loss(h2, p["norm_final_w"].reshape(1, d), tgt, tm, "final_loss")

    g = {"norm_final_w": g_nfw.reshape(d)}
    g["w_down"] = _matmul_tn(act, dh2, tm, d, "ffn_down_dw")
    dact = _matmul_nt([dh2], [p["w_down"]], tm, "ffn_down_dx")
    dup_v, dup_g, dwv, dwg, dbv, dbg = _conv_glu_bwd(up_v, up_g, fw[:, :dff], fw[:, dff:], fb[:, :dff], fb[:, dff:], dact, bl, 256, "ffn_conv_bwd")
    g["ffn_conv_w"] = jnp.concatenate([dwv, dwg], 1)
    g["ffn_conv_b"] = jnp.concatenate([dbv, dbg], 1)
    g["w_up"] = jnp.concatenate([_matmul_tn(hn2, dup_v, tm, dff // 2, "ffn_up_dw_v"), _matmul_tn(hn2, dup_g, tm, dff // 2, "ffn_up_dw_g")], 1)
    dhn2 = _matmul_nt([dup_v, dup_g], [w_up[:, :dff], w_up[:, dff:]], tm_ffn, "ffn_up_dx")
    dh1, g["norm_ffn_w"] = _norm_bwd(h1, p["norm_ffn_w"], dhn2, dh2, tm, "ffn_norm_bwd")
    g["w_out"] = _matmul_tn(ymix, dh1, tm, d, "out_proj_dw")
    dmix = _matmul_nt([dh1], [p["w_out"]], tm, "out_proj_dx")
    dyssd, dxs_gate, dz, ds5o, g_d, g["ssd_norm_w"], g["s5_norm_w"] = _mix_bwd(
        y2, xs_a, z, s5o, dvec, p["ssd_norm_w"], p["s5_norm_w"], dmix, tm, "mix_bwd")
    g["ssd_d"] = g_d[:SSD_HEADS].reshape(1, SSD_HEADS)
    s5g = _s5_bwd(s5_args, _x_layout(ds5o, "s5_dy_blocks"), nck, "s5_bwd")
    du = _token_layout(s5g[0], "s5_du_tokens")
    (g["s5_lambda_re_fwd"], g["s5_lambda_im_fwd"], g["s5_log_step_fwd"], g["s5_c_re_fwd"], g["s5_c_im_fwd"],
     g["s5_lambda_re_bwd"], g["s5_lambda_im_bwd"], g["s5_log_step_bwd"], g["s5_c_re_bwd"], g["s5_c_im_bwd"],
     g["s5_b_re"], g["s5_b_im"], g_s5d, g_wv, g_wg, g_bv, g_bg) = s5g[1:]
    for k_ in ("s5_lambda_re_fwd", "s5_lambda_im_fwd", "s5_lambda_re_bwd", "s5_lambda_im_bwd"):
        g[k_] = g[k_].reshape(S5_GROUPS, S5_STATE)
    for k_ in ("s5_log_step_fwd", "s5_log_step_bwd"):
        g[k_] = g[k_].reshape(S5_GROUPS)
    g["s5_d"] = g_s5d.reshape(1, s5w)
    g["s5_glu_w"] = jnp.concatenate([g_wv, g_wg], 2)
    g["s5_glu_b"] = jnp.concatenate([g_bv.reshape(S5_GROUPS, S5_CH), g_bg.reshape(S5_GROUPS, S5_CH)], 1)
    dxs2, dbm2, dcm2, ddtr, dal2, ddb2 = _ssd_scan_bwd(xs_a, b_a, c_a, dtr, alog2, dtb2, saved, dyssd, bl, "ssd_scan_bwd")
    g["ssd_a_log_fwd"], g["ssd_a_log_bwd"] = dal2[0, :, :SSD_HEADS], dal2[1, :, :SSD_HEADS]
    g["ssd_dt_bias_fwd"], g["ssd_dt_bias_bwd"] = ddb2[0, :, :SSD_HEADS], ddb2[1, :, :SSD_HEADS]
    cots = [[(dxs2, 0), (dxs2, 1), (dxs_gate, None)], [(dbm2, 0), (dbm2, 1)], [(dcm2, 0), (dcm2, 1)]]
    dpres, dcw, dcb = [], [], []
    for i, (pre, (w, b), cot) in enumerate(zip(pres, conv_parts, cots)):
        dp, dw_, db_ = _conv_silu_bwd(pre, w, b, cot, bl, min(256, pre.shape[1]), f"ssd_conv_bwd_{i}")
        dpres.append(dp)
        dcw.append(dw_)
        dcb.append(db_)
    g["ssd_conv_w"] = jnp.concatenate(dcw, 1)
    g["ssd_conv_b"] = jnp.concatenate(dcb, 1)
    dprojs = [dz, dpres[0], dpres[1], dpres[2], ddtr, du]
    dws = [_matmul_tn(hn, dpj, tm, dpj.shape[1], f"in_proj_dw_{i}") for i, dpj in enumerate(dprojs)]
    dws[4] = jnp.concatenate([dws[4][:, :SSD_HEADS], dws[4][:, LANES:LANES + SSD_HEADS]], 1)
    g["w_in"] = jnp.concatenate(dws, 1)
    dhn = _matmul_nt(dprojs, in_ws, tm, "in_proj_dx")
    grad_x, g["norm_mix_w"] = _norm_bwd(x, p["norm_mix_w"], dhn, dh1, tm, "mix_norm_bwd")
    return loss, grad_x, g


_WEIGHTS = ['norm_mix_w', 'w_in', 'ssd_conv_w', 'ssd_conv_b', 'ssd_dt_bias_fwd', 'ssd_dt_bias_bwd', 'ssd_a_log_fwd', 'ssd_a_log_bwd',
            'ssd_d', 'ssd_norm_w', 's5_lambda_re_fwd', 's5_lambda_im_fwd', 's5_log_step_fwd', 's5_lambda_re_bwd', 's5_lambda_im_bwd',
            's5_log_step_bwd', 's5_b_re', 's5_b_im', 's5_c_re_fwd', 's5_c_im_fwd', 's5_c_re_bwd', 's5_c_im_bwd', 's5_d', 's5_glu_w',
            's5_glu_b', 's5_norm_w', 'w_out', 'norm_ffn_w', 'ffn_w_up', 'ffn_conv_w', 'ffn_conv_b', 'ffn_w_down', 'norm_final_w']
_BIG = ('w_in', 'w_out', 'ffn_w_up', 'ffn_w_down')
_CONV = ('ssd_conv_w', 'ffn_conv_w')


def _pack(arrs):
    flat = jnp.concatenate([a.reshape(-1) for a in arrs])
    rows = -(-flat.shape[0] // (64 * LANES)) * 64
    return jnp.pad(flat, (0, rows * LANES - flat.shape[0])).reshape(rows, LANES)


def _unpack(buf, shapes):
    flat = buf.reshape(-1)
    out, off = [], 0
    for shp in shapes:
        size = math.prod(shp)
        out.append(flat[off:off + size].reshape(shp))
        off += size
    return out


def kernel(x, norm_mix_w, w_in, ssd_conv_w, ssd_conv_b, ssd_dt_bias_fwd, ssd_dt_bias_bwd, ssd_a_log_fwd, ssd_a_log_bwd, ssd_d, ssd_norm_w, s5_lambda_re_fwd, s5_lambda_im_fwd, s5_log_step_fwd, s5_lambda_re_bwd, s5_lambda_im_bwd, s5_log_step_bwd, s5_b_re, s5_b_im, s5_c_re_fwd, s5_c_im_fwd, s5_c_re_bwd, s5_c_im_bwd, s5_d, s5_glu_w, s5_glu_b, s5_norm_w, w_out, norm_ffn_w, ffn_w_up, ffn_conv_w, ffn_conv_b, ffn_w_down, norm_final_w, loss_target, m_norm_mix_w, m_w_in, m_ssd_conv_w, m_ssd_conv_b, m_ssd_dt_bias_fwd, m_ssd_dt_bias_bwd, m_ssd_a_log_fwd, m_ssd_a_log_bwd, m_ssd_d, m_ssd_norm_w, m_s5_lambda_re_fwd, m_s5_lambda_im_fwd, m_s5_log_step_fwd, m_s5_lambda_re_bwd, m_s5_lambda_im_bwd, m_s5_log_step_bwd, m_s5_b_re, m_s5_b_im, m_s5_c_re_fwd, m_s5_c_im_fwd, m_s5_c_re_bwd, m_s5_c_im_bwd, m_s5_d, m_s5_glu_w, m_s5_glu_b, m_s5_norm_w, m_w_out, m_norm_ffn_w, m_ffn_w_up, m_ffn_conv_w, m_ffn_conv_b, m_ffn_w_down, m_norm_final_w, v_norm_mix_w, v_w_in, v_ssd_conv_w, v_ssd_conv_b, v_ssd_dt_bias_fwd, v_ssd_dt_bias_bwd, v_ssd_a_log_fwd, v_ssd_a_log_bwd, v_ssd_d, v_ssd_norm_w, v_s5_lambda_re_fwd, v_s5_lambda_im_fwd, v_s5_log_step_fwd, v_s5_lambda_re_bwd, v_s5_lambda_im_bwd, v_s5_log_step_bwd, v_s5_b_re, v_s5_b_im, v_s5_c_re_fwd, v_s5_c_im_fwd, v_s5_c_re_bwd, v_s5_c_im_bwd, v_s5_d, v_s5_glu_w, v_s5_glu_b, v_s5_norm_w, v_w_out, v_norm_ffn_w, v_ffn_w_up, v_ffn_conv_w, v_ffn_conv_b, v_ffn_w_down, v_norm_final_w):
    args = dict(locals())
    w = {k_: args[k_] for k_ in _WEIGHTS}
    m = {k_: args["m_" + k_] for k_ in _WEIGHTS}
    v = {k_: args["v_" + k_] for k_ in _WEIGHTS}
    bl, sl, d = x.shape
    chip = 2 * lax.axis_index("x") + lax.axis_index("y")
    core = lax.axis_index("c")

    later = [w[k_][0].astype(BF16) for k_ in _BIG[1:]]
    g_out, g_up, g_down = _gather_behind(later, "gather_rest")
    first = w["w_in"][0].astype(BF16)
    convs = [w[k_][0] for k_ in _CONV]
    g_in, g_scw, g_fcw = _gather_xy([first.reshape(2, first.shape[0] // 2, first.shape[1])], convs, "gather_first")
    g_in = g_in.reshape((4,) + first.shape)

    def cols(a):
        return jnp.moveaxis(a, 0, 1).reshape(a.shape[1], 4 * a.shape[2])

    p = {k_: (w[k_][0] if w[k_].ndim >= 3 else w[k_]) for k_ in _WEIGHTS if k_ not in _BIG + _CONV}
    p["w_in"], p["w_up"] = cols(g_in), cols(g_up)
    p["w_out"], p["w_down"] = g_out.reshape(-1, g_out.shape[2]), g_down.reshape(-1, g_down.shape[2])
    p["ssd_conv_w"], p["ffn_conv_w"] = cols(g_scw), cols(g_fcw)

    loss, grad_x, g = _local_step(x.reshape(bl * sl, d), loss_target.reshape(bl * sl, d), p, bl)
    g["ffn_w_up"], g["ffn_w_down"] = g.pop("w_up"), g.pop("w_down")

    def owner_major(a, k_):
        r, c = w[k_].shape[1:]
        if a.shape[0] == r:
            a = jnp.moveaxis(a.reshape(r, 4, c), 1, 0)
        else:
            a = a.reshape(4, r, c)
        return a.reshape(4, 2, r // 2, c)

    small = [k_ for k_ in _WEIGHTS if k_ not in _BIG]
    small_full_shapes = [g[k_].shape for k_ in small]
    buf = _pack([g[k_] for k_ in small] + [loss[0, :1]])
    parts = [jnp.moveaxis(owner_major(g[k_], k_), 1, 0) for k_ in _BIG]
    parts.append(jnp.moveaxis(buf.reshape(4, 2, -1, LANES), 1, 0))
    got = _swap_sibling(parts, True, "reduce_sibling")
    chip_sums = [_add_half(pt, gt, core, BF16 if i < len(_BIG) else F32, f"reduce_add_{i}") for i, (pt, gt) in enumerate(zip(parts, got))]
    from_chips = _scatter_xy(chip_sums, "reduce_chips")
    halves = [_sum_lead(a.reshape(4, -1, a.shape[-1]), f"reduce_sum_{i}") for i, a in enumerate(from_chips)]
    other = _swap_sibling(halves[:-1], False, "reduce_join")
    big_grad = {}
    for k_, own_half, sib_half in zip(_BIG, halves, other):
        south = core == 0
        full = jnp.stack([jnp.where(south, own_half, sib_half), jnp.where(south, sib_half, own_half)])
        big_grad[k_] = full.reshape((1,) + w[k_].shape[1:])
    tot = _bcast_all(halves[-1], "reduce_small").reshape(buf.shape)
    unp = _unpack(tot, small_full_shapes + [(1,)])
    small_grad = dict(zip(small, unp[:-1]))
    loss_out = unp[-1].reshape(())
    for k_ in _CONV:
        cshard = w[k_].shape[2]
        small_grad[k_] = lax.dynamic_slice_in_dim(small_grad[k_], chip * cshard, cshard, 1)

    grads, deltas, new_m, new_v = {}, {}, {}, {}
    for k_ in _BIG:
        shp = w[k_].shape
        grads[k_] = big_grad[k_]
        dl, nm, nv = _adamw(w[k_][0], big_grad[k_][0], m[k_][0], v[k_][0], f"adamw_{k_}")
        deltas[k_], new_m[k_], new_v[k_] = dl.reshape(shp), nm.reshape(shp), nv.reshape(shp)
    sw_ = _pack([w[k_] for k_ in small])
    sg_ = _pack([small_grad[k_] for k_ in small])
    sm_ = _pack([m[k_] for k_ in small])
    sv_ = _pack([v[k_] for k_ in small])
    dl, nm, nv = _adamw(sw_, sg_, sm_, sv_, "adamw_small")
    shapes = [w[k_].shape for k_ in small]
    for k_, a, b, c_ in zip(small, _unpack(dl, shapes), _unpack(nm, shapes), _unpack(nv, shapes)):
        deltas[k_], new_m[k_], new_v[k_] = a, b, c_
        grads[k_] = small_grad[k_].reshape(w[k_].shape)
    return (loss_out, grad_x.reshape(bl, sl, d), *[grads[k_] for k_ in _WEIGHTS], *[deltas[k_] for k_ in _WEIGHTS],
            *[new_m[k_] for k_ in _WEIGHTS], *[new_v[k_] for k_ in _WEIGHTS])
```

```python
import functools
import math

import jax
import jax.numpy as jnp
from jax import lax
from jax.experimental import pallas as pl
from jax.experimental.pallas import tpu as pltpu

F32 = jnp.float32
BF16 = jnp.bfloat16
HI = lax.Precision.HIGHEST
SDS = jax.ShapeDtypeStruct
MESH = pl.DeviceIdType.MESH

NN = (((1,), (0,)), ((), ()))
NT = (((1,), (1,)), ((), ()))
TN = (((0,), (0,)), ((), ()))

EPS = 1e-6
SSD_HEADS = 16
SSD_HEAD_DIM = 64
SSD_GROUPS = 4
SSD_STATE = 128
SSD_CHUNK = 128
SSD_CONV = 5
S5_GROUPS = 32
S5_CH = 16
S5_STATE = 64
S5_T = 16
LANES = 128
ADAM_LR, ADAM_B1, ADAM_B2, ADAM_EPS, ADAM_WD, ADAM_STEP = 0.001, 0.9, 0.999, 1e-08, 0.01, 10
V7X_VMEM_BYTES = 64 * 1024 * 1024
VMEM_LIMIT = V7X_VMEM_BYTES - 8 * 1024 * 1024


def _cp(sem, vmem=None):
    return pltpu.CompilerParams(dimension_semantics=sem, vmem_limit_bytes=vmem)


def _dot(a, b, dims=NN, precision=None):
    return lax.dot_general(a, b, dims, precision=precision, preferred_element_type=F32)


def _rms(x, w):
    return x * lax.rsqrt(jnp.mean(x * x, axis=-1, keepdims=True) + EPS) * w


def _sigmoid(x):
    return 1.0 / (1.0 + jnp.exp(-x))


def _softplus(x):
    return jnp.maximum(x, 0.0) + jnp.log1p(jnp.exp(-jnp.abs(x)))


@functools.partial(jax.custom_vjp, nondiff_argnums=(1, 2))
def _shift(x, k, seg):
    n = x.shape[0]
    r = lax.broadcasted_iota(jnp.int32, x.shape, 0)
    if seg != n:
        r = r & (seg - 1) if seg & (seg - 1) == 0 else r % seg
    y = pltpu.roll(x, k % n, 0)
    ok = (r >= k) if k > 0 else (r < seg + k)
    return jnp.where(ok, y, 0.0)


def _shift_fwd(x, k, seg):
    return _shift(x, k, seg), None


def _shift_bwd(k, seg, _, g):
    return (_shift(g, -k, seg),)


_shift.defvjp(_shift_fwd, _shift_bwd)


@functools.partial(jax.custom_vjp, nondiff_argnums=(1,))
def _lane_shift(x, k):
    if k == 0:
        return x
    n = x.shape[1]
    lane = lax.broadcasted_iota(jnp.int32, x.shape, 1)
    ok = (lane >= k) if k > 0 else (lane < n + k)
    return jnp.where(ok, pltpu.roll(x, k % n, 1), 0.0)


_lane_shift.defvjp(lambda x, k: (_lane_shift(x, k), None), lambda k, _, g: (_lane_shift(g, -k),))


@jax.custom_vjp
def _swap(z):
    return pltpu.roll(z, LANES // 2, 1)


_swap.defvjp(lambda z: (_swap(z), None), lambda _, g: (_swap(g),))


def _norm_matmul(x, nw, ws, tm, name):
    n, d = x.shape
    k = len(ws)

    def body(x_ref, nw_ref, *refs):
        hn = _rms(x_ref[...], nw_ref[...]).astype(BF16)
        refs[k][...] = hn
        for w_ref, o_ref in zip(refs[:k], refs[k + 1:]):
            o_ref[...] = _dot(hn, w_ref[...])

    row = lambda i: (i, 0)
    fix = lambda i: (0, 0)
    return pl.pallas_call(
        body, name=name, grid=(n // tm,),
        in_specs=[pl.BlockSpec((tm, d), row), pl.BlockSpec((1, d), fix)] + [pl.BlockSpec(w.shape, fix) for w in ws],
        out_specs=[pl.BlockSpec((tm, d), row)] + [pl.BlockSpec((tm, w.shape[1]), row) for w in ws],
        out_shape=[SDS((n, d), BF16)] + [SDS((n, w.shape[1]), F32) for w in ws],
        compiler_params=_cp(("arbitrary",), VMEM_LIMIT),
    )(x, nw, *ws)


def _matmul_res(a, w, res, tm, name):
    n, kd = a.shape
    m = w.shape[1]

    def body(a_ref, w_ref, r_ref, o_ref):
        o_ref[...] = r_ref[...] + _dot(a_ref[...], w_ref[...])

    return pl.pallas_call(
        body, name=name, grid=(n // tm,),
        in_specs=[pl.BlockSpec((tm, kd), lambda i: (i, 0)), pl.BlockSpec((kd, m), lambda i: (0, 0)),
                  pl.BlockSpec((tm, m), lambda i: (i, 0))],
        out_specs=pl.BlockSpec((tm, m), lambda i: (i, 0)),
        out_shape=SDS((n, m), F32),
        compiler_params=_cp(("arbitrary",), VMEM_LIMIT),
    )(a, w, res)


def _matmul_nt(gs, ws, tm, name):
    n = gs[0].shape[0]
    kd = ws[0].shape[0]
    cnt = len(gs)

    def body(*refs):
        acc = None
        for g_ref, w_ref in zip(refs[:cnt], refs[cnt:2 * cnt]):
            t = _dot(g_ref[...].astype(BF16), w_ref[...], NT)
            acc = t if acc is None else acc + t
        refs[2 * cnt][...] = acc

    return pl.pallas_call(
        body, name=name, grid=(n // tm,),
        in_specs=[pl.BlockSpec((tm, g.shape[1]), lambda i: (i, 0)) for g in gs]
        + [pl.BlockSpec(w.shape, lambda i: (0, 0)) for w in ws],
        out_specs=pl.BlockSpec((tm, kd), lambda i: (i, 0)),
        out_shape=SDS((n, kd), F32),
        compiler_params=_cp(("arbitrary",), VMEM_LIMIT),
    )(*gs, *ws)


def _matmul_tn(a, g, tm, cb, name):
    n, kd = a.shape
    m = g.shape[1]

    def body(a_ref, g_ref, o_ref):
        t = _dot(a_ref[...], g_ref[...].astype(BF16), TN)

        @pl.when(pl.program_id(1) == 0)
        def _():
            o_ref[...] = t

        @pl.when(pl.program_id(1) != 0)
        def _():
            o_ref[...] += t

    return pl.pallas_call(
        body, name=name, grid=(m // cb, n // tm),
        in_specs=[pl.BlockSpec((tm, kd), lambda j, i: (i, 0)), pl.BlockSpec((tm, cb), lambda j, i: (i, j))],
        out_specs=pl.BlockSpec((kd, cb), lambda j, i: (0, j)),
        out_shape=SDS((kd, m), F32),
        compiler_params=_cp(("arbitrary", "arbitrary"), VMEM_LIMIT),
    )(a, g)


def _dwconv(x, w, b):
    kw = w.shape[0]
    acc = b
    for k in range(kw):
        acc = acc + w[k:k + 1, :] * _shift(x, kw // 2 - k, x.shape[0])
    return acc


def _conv_silu_fn(x, w, b):
    y = _dwconv(x, w, b)
    return y * _sigmoid(y)


def _conv_glu_fn(v, g, wv, wg, bv, bg):
    cv = _dwconv(v, wv, bv)
    cg = _dwconv(g, wg, bg)
    return cg * _sigmoid(cg) * cv


def _conv_silu(x, w, b, bl, cb, name):
    n, c = x.shape
    sl = n // bl
    kw = w.shape[0]

    def body(x_ref, w_ref, b_ref, o_ref):
        o_ref[...] = _conv_silu_fn(x_ref[...], w_ref[...], b_ref[...])

    return pl.pallas_call(
        body, name=name, grid=(bl, c // cb),
        in_specs=[pl.BlockSpec((sl, cb), lambda s, j: (s, j)), pl.BlockSpec((kw, cb), lambda s, j: (0, j)),
                  pl.BlockSpec((1, cb), lambda s, j: (0, j))],
        out_specs=pl.BlockSpec((sl, cb), lambda s, j: (s, j)),
        out_shape=SDS((n, c), F32),
        compiler_params=_cp(("arbitrary", "arbitrary"), VMEM_LIMIT),
    )(x, w, b)


def _conv_silu_bwd(x, w, b, dys, bl, cb, name):
    n, c = x.shape
    sl = n // bl
    kw = w.shape[0]
    cnt = len(dys)

    def body(x_ref, w_ref, b_ref, *refs):
        dy = refs[0][...]
        for r in refs[1:cnt]:
            dy = dy + r[...]
        dx_ref, dw_ref, db_ref = refs[cnt:]
        _, vjp = jax.vjp(_conv_silu_fn, x_ref[...], w_ref[...], b_ref[...])
        dx, dw, db = vjp(dy)
        dx_ref[...] = dx.astype(BF16)

        @pl.when(pl.program_id(1) == 0)
        def _():
            dw_ref[...] = dw
            db_ref[...] = db

        @pl.when(pl.program_id(1) != 0)
        def _():
            dw_ref[...] += dw
            db_ref[...] += db

    dy_specs = []
    for arr, lead in dys:
        if lead is None:
            dy_specs.append(pl.BlockSpec((sl, cb), lambda j, s: (s, j)))
        else:
            dy_specs.append(pl.BlockSpec((None, sl, cb), functools.partial(lambda j, s, lead: (lead, s, j), lead=lead)))
    return pl.pallas_call(
        body, name=name, grid=(c // cb, bl),
        in_specs=[pl.BlockSpec((sl, cb), lambda j, s: (s, j)), pl.BlockSpec((kw, cb), lambda j, s: (0, j)),
                  pl.BlockSpec((1, cb), lambda j, s: (0, j))] + dy_specs,
        out_specs=[pl.BlockSpec((sl, cb), lambda j, s: (s, j)), pl.BlockSpec((kw, cb), lambda j, s: (0, j)),
                   pl.BlockSpec((1, cb), lambda j, s: (0, j))],
        out_shape=[SDS((n, c), BF16), SDS((kw, c), F32), SDS((1, c), F32)],
        compiler_params=_cp(("arbitrary", "arbitrary"), VMEM_LIMIT),
    )(x, w, b, *[a for a, _ in dys])


def _conv_glu(v, g, wv, wg, bv, bg, bl, cb, name):
    n, c = v.shape
    sl = n // bl
    kw = wv.shape[0]

    def body(v_ref, g_ref, wv_ref, wg_ref, bv_ref, bg_ref, o_ref):
        o_ref[...] = _conv_glu_fn(v_ref[...], g_ref[...], wv_ref[...], wg_ref[...], bv_ref[...], bg_ref[...]).astype(BF16)

    big = pl.BlockSpec((sl, cb), lambda s, j: (s, j))
    wsp = pl.BlockSpec((kw, cb), lambda s, j: (0, j))
    bsp = pl.BlockSpec((1, cb), lambda s, j: (0, j))
    return pl.pallas_call(
        body, name=name, grid=(bl, c // cb),
        in_specs=[big, big, wsp, wsp, bsp, bsp], out_specs=big, out_shape=SDS((n, c), BF16),
        compiler_params=_cp(("arbitrary", "arbitrary"), VMEM_LIMIT),
    )(v, g, wv, wg, bv, bg)


def _conv_glu_bwd(v, g, wv, wg, bv, bg, dact, bl, cb, name):
    n, c = v.shape
    sl = n // bl
    kw = wv.shape[0]

    def body(v_ref, g_ref, wv_ref, wg_ref, bv_ref, bg_ref, da_ref, dv_ref, dg_ref, dwv_ref, dwg_ref, dbv_ref, dbg_ref):
        _, vjp = jax.vjp(_conv_glu_fn, v_ref[...], g_ref[...], wv_ref[...], wg_ref[...], bv_ref[...], bg_ref[...])
        dv, dg, dwv, dwg, dbv, dbg = vjp(da_ref[...])
        dv_ref[...] = dv.astype(BF16)
        dg_ref[...] = dg.astype(BF16)

        @pl.when(pl.program_id(1) == 0)
        def _():
            dwv_ref[...] = dwv
            dwg_ref[...] = dwg
            dbv_ref[...] = dbv
            dbg_ref[...] = dbg

        @pl.when(pl.program_id(1) != 0)
        def _():
            dwv_ref[...] += dwv
            dwg_ref[...] += dwg
            dbv_ref[...] += dbv
            dbg_ref[...] += dbg

    big = pl.BlockSpec((sl, cb), lambda j, s: (s, j))
    wsp = pl.BlockSpec((kw, cb), lambda j, s: (0, j))
    bsp = pl.BlockSpec((1, cb), lambda j, s: (0, j))
    return pl.pallas_call(
        body, name=name, grid=(c // cb, bl),
        in_specs=[big, big, wsp, wsp, bsp, bsp, big],
        out_specs=[big, big, wsp, wsp, bsp, bsp],
        out_shape=[SDS((n, c), BF16), SDS((n, c), BF16), SDS((kw, c), F32), SDS((kw, c), F32), SDS((1, c), F32), SDS((1, c), F32)],
        compiler_params=_cp(("arbitrary", "arbitrary"), VMEM_LIMIT),
    )(v, g, wv, wg, bv, bg, dact)


_DIMS_T = {NN: (NT, TN, False, False), NT: (NN, TN, False, True), TN: (NT, NN, True, False)}


@functools.partial(jax.custom_vjp, nondiff_argnums=(2,))
def _bdot(a, b, dims):
    return _dot(a.astype(BF16), b.astype(BF16), dims)


def _bdot_fwd(a, b, dims):
    return _bdot(a, b, dims), (a, b)


def _bdot_bwd(dims, res, g):
    a, b = res
    da_dims, db_dims, a_swapped, b_swapped = _DIMS_T[dims]
    da = _bdot(b, g, da_dims) if a_swapped else _bdot(g, b, da_dims)
    db = _bdot(g, a, db_dims) if b_swapped else _bdot(a, g, db_dims)
    return da, db


_bdot.defvjp(_bdot_fwd, _bdot_bwd)


@functools.partial(jax.custom_vjp, nondiff_argnums=(1,))
def _expand_heads(v, width):
    return _split_dot(v, _head_matrix(width), NN)


def _head_matrix(width):
    hr = lax.broadcasted_iota(jnp.int32, (LANES, SSD_HEADS * width), 0)
    hc = lax.broadcasted_iota(jnp.int32, (LANES, SSD_HEADS * width), 1)
    return (hc // width == hr).astype(BF16)


def _split_dot(v, e, dims):
    hi = v.astype(BF16)
    lo = (v - hi.astype(F32)).astype(BF16)
    return _dot(hi, e, dims) + _dot(lo, e, dims)


_expand_heads.defvjp(lambda v, width: (_expand_heads(v, width), None),
                     lambda width, _, g: (_split_dot(g, _head_matrix(width), NT),))


def _ssd_chunk_fn(rev, xs, dtr, bms, cms, st, alog, dtb):
    q = dtr.shape[0]
    hd, per = SSD_HEAD_DIM, SSD_HEADS // SSD_GROUPS
    gw = per * hd
    r = lax.broadcasted_iota(jnp.int32, (q, q), 0)
    c = lax.broadcasted_iota(jnp.int32, (q, q), 1)
    sgn = 1 - 2 * rev
    tri = ((r - c) * sgn >= 0).astype(F32)
    tri_t = ((c - r) * sgn >= 0).astype(F32)
    r4 = lax.broadcasted_iota(jnp.int32, (q, per * q), 0)
    c4 = lax.broadcasted_iota(jnp.int32, (q, per * q), 1) % q
    mask4 = (r4 - c4) * sgn >= 0
    bdr = lax.broadcasted_iota(jnp.int32, (per * q, gw), 0) // q
    bdc = lax.broadcasted_iota(jnp.int32, (per * q, gw), 1) // hd
    diag = bdr == bdc
    dt = _softplus(dtr + dtb)
    dta = dt * (-jnp.exp(alog))
    cs = _dot(tri, dta, NN, HI)
    cs_t = _dot(dta, tri_t, TN, HI)
    tot = jnp.sum(dta, axis=0, keepdims=True)
    dt_x = _expand_heads(dt, hd)
    in_x = _expand_heads(jnp.exp(cs), hd)
    out_x = _expand_heads(jnp.exp(tot - cs), hd)
    ys, outs = [], []
    for g in range(SSD_GROUPS):
        bg, cg = bms[g], cms[g]
        heads = range(per * g, per * (g + 1))
        lanes = slice(gw * g, gw * (g + 1))
        scores = _bdot(cg, bg, NT)
        col = jnp.concatenate([jnp.broadcast_to(cs[:, h:h + 1], (q, q)) for h in heads], axis=1)
        row = jnp.concatenate([cs_t[h:h + 1, :] for h in heads], axis=1)
        seg = jnp.where(mask4, jnp.exp(jnp.where(mask4, col - row, 0.0)), 0.0)
        mcat = jnp.concatenate([scores] * per, axis=1) * seg
        xdt = xs[g] * dt_x[:, lanes]
        blocks = jnp.where(diag, jnp.concatenate([xdt] * per, axis=0), 0.0)
        y = _bdot(mcat, blocks, NN) + in_x[:, lanes] * _bdot(cg, st[g], NT)
        new = _bdot(xdt * out_x[:, lanes], bg, TN)
        keep = jnp.concatenate([jnp.exp(tot[:, h:h + 1]) * st[g][hd * j:hd * (j + 1), :] for j, h in enumerate(heads)], axis=0)
        ys.append(y)
        outs.append(keep + new)
    return ys, outs


def _ssd_scan(xs, bm, cm, dtr, alog2, dtb2, bl, name):
    n = xs.shape[0]
    q = SSD_CHUNK
    nc = n // bl // q
    hd, ns = SSD_HEAD_DIM, SSD_STATE
    gw = SSD_HEADS // SSD_GROUPS * hd

    def body(xs_ref, b_ref, c_ref, dt_ref, al_ref, db_ref, y_ref, sv_ref, st_ref):
        d, i = pl.program_id(0), pl.program_id(2)

        @pl.when(i == 0)
        def _():
            st_ref[...] = jnp.zeros(st_ref.shape, F32)

        st = [st_ref[gw * g:gw * (g + 1), :] for g in range(SSD_GROUPS)]
        sv_ref[...] = st_ref[...]
        xl = [xs_ref[:, gw * g:gw * (g + 1)] for g in range(SSD_GROUPS)]
        bms = [b_ref[:, ns * g:ns * (g + 1)] for g in range(SSD_GROUPS)]
        cms = [c_ref[:, ns * g:ns * (g + 1)] for g in range(SSD_GROUPS)]
        ys, outs = _ssd_chunk_fn(d, xl, dt_ref[...], bms, cms, st, al_ref[...], db_ref[...])
        for g in range(SSD_GROUPS):
            st_ref[gw * g:gw * (g + 1), :] = outs[g]
            y_ref[:, gw * g:gw * (g + 1)] = ys[g]

    def rowblk(d, s, i):
        return s * nc + i + d * (nc - 1 - 2 * i)

    return pl.pallas_call(
        body, name=name, grid=(2, bl, nc),
        in_specs=[pl.BlockSpec((q, SSD_HEADS * hd), lambda d, s, i: (rowblk(d, s, i), 0)),
                  pl.BlockSpec((q, SSD_GROUPS * ns), lambda d, s, i: (rowblk(d, s, i), 0)),
                  pl.BlockSpec((q, SSD_GROUPS * ns), lambda d, s, i: (rowblk(d, s, i), 0)),
                  pl.BlockSpec((q, LANES), lambda d, s, i: (rowblk(d, s, i), d)),
                  pl.BlockSpec((None, 1, LANES), lambda d, s, i: (d, 0, 0)),
                  pl.BlockSpec((None, 1, LANES), lambda d, s, i: (d, 0, 0))],
        out_specs=[pl.BlockSpec((None, q, SSD_HEADS * hd), lambda d, s, i: (d, rowblk(d, s, i), 0)),
                   pl.BlockSpec((None, None, SSD_HEADS * hd, ns), lambda d, s, i: (d, rowblk(d, s, i), 0, 0))],
        out_shape=[SDS((2, n, SSD_HEADS * hd), F32), SDS((2, n // q, SSD_HEADS * hd, ns), F32)],
        scratch_shapes=[pltpu.VMEM((SSD_HEADS * hd, ns), F32)],
        compiler_params=_cp(("arbitrary",) * 3, VMEM_LIMIT),
    )(xs, bm, cm, dtr, alog2, dtb2)


def _ssd_scan_bwd(xs, bm, cm, dtr, alog2, dtb2, saved, dy, bl, name):
    n = xs.shape[0]
    q = SSD_CHUNK
    nc = n // bl // q
    hd, ns = SSD_HEAD_DIM, SSD_STATE
    gw = SSD_HEADS // SSD_GROUPS * hd

    def body(xs_ref, b_ref, c_ref, dt_ref, al_ref, db_ref, sv_ref, dy_ref,
             dxs_ref, dbm_ref, dcm_ref, ddt_ref, dal_ref, ddb_ref, ds_ref):
        d, s, i = pl.program_id(0), pl.program_id(1), pl.program_id(2)

        @pl.when(i == 0)
        def _():
            ds_ref[...] = jnp.zeros(ds_ref.shape, F32)

        xl = [xs_ref[:, gw * g:gw * (g + 1)] for g in range(SSD_GROUPS)]
        bms = [b_ref[:, ns * g:ns * (g + 1)] for g in range(SSD_GROUPS)]
        cms = [c_ref[:, ns * g:ns * (g + 1)] for g in range(SSD_GROUPS)]
        st = [sv_ref[gw * g:gw * (g + 1), :] for g in range(SSD_GROUPS)]
        fn = functools.partial(_ssd_chunk_fn, d)
        _, vjp = jax.vjp(fn, xl, dt_ref[...], bms, cms, st, al_ref[...], db_ref[...])
        dys = [dy_ref[:, gw * g:gw * (g + 1)] for g in range(SSD_GROUPS)]
        dso = [ds_ref[gw * g:gw * (g + 1), :] for g in range(SSD_GROUPS)]
        dxl, ddt, dbg, dcg, dst, dal, ddb = vjp((dys, dso))
        for g in range(SSD_GROUPS):
            ds_ref[gw * g:gw * (g + 1), :] = dst[g]
            dxs_ref[:, gw * g:gw * (g + 1)] = dxl[g]
            dbm_ref[:, ns * g:ns * (g + 1)] = dbg[g]
            dcm_ref[:, ns * g:ns * (g + 1)] = dcg[g]
        ddt_ref[...] = ddt
        _acc_rows((dal_ref, ddb_ref), (dal, ddb), jnp.logical_and(s == 0, i == 0))

    def rowblk(d, s, i):
        return s * nc + (nc - 1 - i) + d * (2 * i - (nc - 1))

    row = lambda d, s, i: (rowblk(d, s, i), 0)
    drow = lambda d, s, i: (d, rowblk(d, s, i), 0)
    dfix = lambda d, s, i: (d, 0, 0)
    dcol = lambda d, s, i: (rowblk(d, s, i), d)
    return pl.pallas_call(
        body, name=name, grid=(2, bl, nc),
        in_specs=[pl.BlockSpec((q, SSD_HEADS * hd), row), pl.BlockSpec((q, SSD_GROUPS * ns), row),
                  pl.BlockSpec((q, SSD_GROUPS * ns), row), pl.BlockSpec((q, LANES), dcol),
                  pl.BlockSpec((None, 1, LANES), dfix), pl.BlockSpec((None, 1, LANES), dfix),
                  pl.BlockSpec((None, None, SSD_HEADS * hd, ns), lambda d, s, i: (d, rowblk(d, s, i), 0, 0)),
                  pl.BlockSpec((q, SSD_HEADS * hd), row)],
        out_specs=[pl.BlockSpec((None, q, SSD_HEADS * hd), drow), pl.BlockSpec((None, q, SSD_GROUPS * ns), drow),
                   pl.BlockSpec((None, q, SSD_GROUPS * ns), drow), pl.BlockSpec((q, LANES), dcol),
                   pl.BlockSpec((None, 1, LANES), dfix), pl.BlockSpec((None, 1, LANES), dfix)],
        out_shape=[SDS((2, n, SSD_HEADS * hd), F32), SDS((2, n, SSD_GROUPS * ns), F32), SDS((2, n, SSD_GROUPS * ns), F32),
                   SDS((n, 2 * LANES), F32), SDS((2, 1, LANES), F32), SDS((2, 1, LANES), F32)],
        scratch_shapes=[pltpu.VMEM((SSD_HEADS * hd, ns), F32)],
        compiler_params=_cp(("arbitrary",) * 3, VMEM_LIMIT),
    )(xs, bm, cm, dtr, alog2, dtb2, saved, dy)


def _s5_consts():
    t, ch, p = S5_T, S5_CH, S5_STATE
    lane = lax.broadcasted_iota(jnp.int32, (1, 2 * p), 1)
    pr = lax.broadcasted_iota(jnp.int32, (p, 2 * p), 0)
    pc = lax.broadcasted_iota(jnp.int32, (p, 2 * p), 1)
    cr = lax.broadcasted_iota(jnp.int32, (ch, t * ch), 0)
    cc = lax.broadcasted_iota(jnp.int32, (ch, t * ch), 1)
    return dict(
        left=lane < p,
        sg=jnp.where(lane < p, -1.0, 1.0).astype(F32),
        dup=(pc % p == pr).astype(F32),
        dup_l=(pc == pr).astype(F32),
        dup_r=(pc == pr + p).astype(F32),
        rep=(cc % ch == cr).astype(F32),
        rep0=(cc == cr).astype(F32),
    )


def _s5_mats(k, rev, lr, li, ls, bre, bim, cre, cim):
    t = S5_T
    step = jnp.exp(ls)
    lr2 = jnp.sum(lr * k["dup"], axis=0, keepdims=True)
    li2 = jnp.sum(li * k["dup"], axis=0, keepdims=True)

    def erow(d):
        ang = (d * step) * li2
        return jnp.exp((d * step) * lr2) * jnp.where(k["left"], jnp.cos(ang), jnp.sin(ang))

    es = [erow(d) for d in range(t + 1)]
    mag = jnp.exp(step * lr)
    ar, ai = mag * jnp.cos(step * li), mag * jnp.sin(step * li)
    den = lr * lr + li * li
    zr = ((ar - 1.0) * lr + ai * li) / den
    zi = (ai * lr - (ar - 1.0) * li) / den
    bbr = zr * bre - zi * bim
    bbi = zr * bim + zi * bre
    bt1 = _dot(bbr, k["dup"], TN, HI)
    bt2 = _dot(bbi, k["dup"], TN, HI)
    bst = _dot(bbr, k["dup_l"], TN, HI) - _dot(bbi, k["dup_r"], TN, HI)
    c1 = _dot(cre, k["dup"], NN, HI)
    c2 = _dot(cim, k["dup"], NN, HI)
    sg = k["sg"]
    ce = [e * c1 + sg * _swap(e) * c2 for e in es]
    lags = range(t - 1, -1, -1) if rev else range(t)
    kt = _dot(bst, jnp.concatenate([ce[d] for d in lags], axis=0), NT, HI)
    toep = jnp.concatenate([_lane_shift(kt, -S5_CH * (t - 1 - s) if rev else S5_CH * s) for s in range(t)], axis=0)
    w_out =jnp.concatenate([ce[(t - qq) if rev else (qq + 1)] * (-sg) for qq in range(t)], axis=0)
    w_st = jnp.concatenate(
        [(lambda e: e * bt1 + sg * _swap(e) * bt2)(es[s if rev else (t - 1 - s)]) for s in range(t)], axis=0)
    return toep, w_out, w_st, es[t]


def _cmul_row(k, e, z):
    es = _swap(e)
    return z * jnp.where(k["left"], e, es) + _swap(z) * (k["sg"] * jnp.where(k["left"], es, e))


def _s5_dir(k, rev, nck, x, mats):
    toep, w_out, w_st, a_t = mats
    acc = _dot(x, w_st)
    e = a_t
    kk = 1
    sign = -1 if rev else 1
    while kk < nck:
        acc = acc + _cmul_row(k, e, _shift(acc, sign * kk, nck))
        e = _cmul_row(k, e, e)
        kk *= 2
    prev = _shift(acc, sign, nck)
    return _dot(x, toep) + _dot(prev, w_out, NT)


def _s5_group_fn(nck, x, pf, pb, bre, bim, dcol, wv, wg, bv, bg):
    k = _s5_consts()
    t = S5_T
    y = x * jnp.sum(dcol * k["rep"], axis=0, keepdims=True)
    for rev, (lr, li, ls, cre, cim) in ((False, pf), (True, pb)):
        y = y + _s5_dir(k, rev, nck, x, _s5_mats(k, rev, lr, li, ls, bre, bim, cre, cim))
    gy = jax.nn.gelu(y)
    def kron_eye(w16):
        wide = _dot(w16, k["rep0"], NN, HI)
        return jnp.concatenate([_lane_shift(wide, S5_CH * qq) for qq in range(t)], axis=0)

    kv, kg = kron_eye(wv), kron_eye(wg)
    val =_dot(gy, kv) + jnp.sum(bv * k["rep"], axis=0, keepdims=True)
    gate = _dot(gy, kg) + jnp.sum(bg * k["rep"], axis=0, keepdims=True)
    return val * _sigmoid(gate)


def _s5_specs(r):
    p, ch = S5_STATE, S5_CH
    g3 = lambda i: (i, 0, 0)
    col = pl.BlockSpec((None, p, 1), g3)
    one = pl.BlockSpec((None, 1, 1), g3)
    cmat = pl.BlockSpec((None, ch, p), g3)
    bmat = pl.BlockSpec((None, p, ch), g3)
    ccol = pl.BlockSpec((None, ch, 1), g3)
    sq = pl.BlockSpec((None, ch, ch), g3)
    xs = pl.BlockSpec((None, r, S5_T * ch), g3)
    specs = [xs, col, col, one, cmat, cmat, col, col, one, cmat, cmat, bmat, bmat, ccol, sq, sq, ccol, ccol]
    return specs


def _s5_unpack(vals):
    x = vals[0]
    pf = tuple(vals[1:6])
    pb = tuple(vals[6:11])
    bre, bim, dcol, wv, wg, bv, bg = vals[11:18]
    return x, pf, pb, bre, bim, dcol, wv, wg, bv, bg


def _s5_fwd(args, nck, name):
    x = args[0]
    ng, r, w = x.shape

    def body(*refs):
        vals = [ref[...] for ref in refs[:18]]
        refs[18][...] = _s5_group_fn(nck, *_s5_unpack(vals))

    specs = _s5_specs(r)
    return pl.pallas_call(
        body, name=name, grid=(ng,), in_specs=specs, out_specs=specs[0], out_shape=SDS(x.shape, F32),
        compiler_params=_cp(("arbitrary",), VMEM_LIMIT),
    )(*args)


def _s5_bwd(args, dy, nck, name):
    x = args[0]
    ng, r, w = x.shape

    def body(*refs):
        vals = [ref[...] for ref in refs[:18]]
        _, vjp = jax.vjp(lambda *v: _s5_group_fn(nck, *_s5_unpack(v)), *vals)
        grads = vjp(refs[18][...])
        for o_ref, gval in zip(refs[19:], grads):
            o_ref[...] = gval.astype(o_ref.dtype)

    specs = _s5_specs(r)
    return pl.pallas_call(
        body, name=name, grid=(ng,), in_specs=specs + [specs[0]], out_specs=specs,
        out_shape=[SDS(x.shape, BF16)] + [SDS(a.shape, F32) for a in args[1:]],
        compiler_params=_cp(("arbitrary",), VMEM_LIMIT),
    )(*args, dy)


def _mix_fn(yf, yb, xs, z, s5o, dvec, nw_ssd, nw_s5):
    hr = lax.broadcasted_iota(jnp.int32, (LANES, SSD_HEADS * SSD_HEAD_DIM), 0)
    hc = lax.broadcasted_iota(jnp.int32, (LANES, SSD_HEADS * SSD_HEAD_DIM), 1)
    expand = (hc // SSD_HEAD_DIM == hr).astype(F32)
    dch = jnp.sum(dvec * expand, axis=0, keepdims=True)
    y = (yf + yb + dch * xs) * (z * _sigmoid(z))
    return _rms(y, nw_ssd), _rms(s5o, nw_s5)


def _mix(y2, xs, z, s5o, dvec, nw_ssd, nw_s5, tm, name):
    n, c1 = xs.shape
    c2 = s5o.shape[1]

    def body(yf_ref, yb_ref, xs_ref, z_ref, s_ref, d_ref, n1_ref, n2_ref, o_ref):
        o1, o2 = _mix_fn(yf_ref[...], yb_ref[...], xs_ref[...], z_ref[...], s_ref[...], d_ref[...], n1_ref[...], n2_ref[...])
        o_ref[:, :c1] = o1.astype(BF16)
        o_ref[:, c1:] = o2.astype(BF16)

    row = lambda i: (i, 0)
    fix = lambda i: (0, 0)
    return pl.pallas_call(
        body, name=name, grid=(n // tm,),
        in_specs=[pl.BlockSpec((None, tm, c1), lambda i: (0, i, 0)), pl.BlockSpec((None, tm, c1), lambda i: (1, i, 0)),
                  pl.BlockSpec((tm, c1), row), pl.BlockSpec((tm, c1), row), pl.BlockSpec((tm, c2), row),
                  pl.BlockSpec((LANES, 1), fix), pl.BlockSpec((1, c1), fix), pl.BlockSpec((1, c2), fix)],
        out_specs=pl.BlockSpec((tm, c1 + c2), row), out_shape=SDS((n, c1 + c2), BF16),
        compiler_params=_cp(("arbitrary",), VMEM_LIMIT),
    )(y2, y2, xs, z, s5o, dvec, nw_ssd, nw_s5)


def _acc_rows(refs, vals, first):
    @pl.when(first)
    def _():
        for ref, v in zip(refs, vals):
            ref[...] = v

    @pl.when(jnp.logical_not(first))
    def _():
        for ref, v in zip(refs, vals):
            ref[...] += v


def _mix_bwd(y2, xs, z, s5o, dvec, nw_ssd, nw_s5, dmix, tm, name):
    n, c1 = xs.shape
    c2 = s5o.shape[1]

    def body(yf_ref, yb_ref, xs_ref, z_ref, s_ref, d_ref, n1_ref, n2_ref, dm_ref,
             dy_ref, dxs_ref, dz_ref, ds_ref, dd_ref, dn1_ref, dn2_ref):
        _, vjp = jax.vjp(_mix_fn, yf_ref[...], yb_ref[...], xs_ref[...], z_ref[...], s_ref[...], d_ref[...], n1_ref[...], n2_ref[...])
        dyf, _, dxs, dz, ds, dd, dn1, dn2 = vjp((dm_ref[:, :c1], dm_ref[:, c1:]))
        dy_ref[...] = dyf
        dxs_ref[...] = dxs
        dz_ref[...] = dz.astype(BF16)
        ds_ref[...] = ds
        _acc_rows((dd_ref, dn1_ref, dn2_ref), (dd, dn1, dn2), pl.program_id(0) == 0)

    row = lambda i: (i, 0)
    fix = lambda i: (0, 0)
    return pl.pallas_call(
        body, name=name, grid=(n // tm,),
        in_specs=[pl.BlockSpec((None, tm, c1), lambda i: (0, i, 0)), pl.BlockSpec((None, tm, c1), lambda i: (1, i, 0)),
                  pl.BlockSpec((tm, c1), row), pl.BlockSpec((tm, c1), row), pl.BlockSpec((tm, c2), row),
                  pl.BlockSpec((LANES, 1), fix), pl.BlockSpec((1, c1), fix), pl.BlockSpec((1, c2), fix),
                  pl.BlockSpec((tm, c1 + c2), row)],
        out_specs=[pl.BlockSpec((tm, c1), row), pl.BlockSpec((tm, c1), row), pl.BlockSpec((tm, c1), row), pl.BlockSpec((tm, c2), row),
                   pl.BlockSpec((LANES, 1), fix), pl.BlockSpec((1, c1), fix), pl.BlockSpec((1, c2), fix)],
        out_shape=[SDS((n, c1), F32), SDS((n, c1), F32), SDS((n, c1), BF16), SDS((n, c2), F32),
                   SDS((LANES, 1), F32), SDS((1, c1), F32), SDS((1, c2), F32)],
        compiler_params=_cp(("arbitrary",), VMEM_LIMIT),
    )(y2, y2, xs, z, s5o, dvec, nw_ssd, nw_s5, dmix)


def _final_loss(h2, nw, tgt, tm, name):
    n, d = h2.shape

    def loss_fn(h, w, t):
        e = _rms(h, w) - t
        return (0.5 / d) * jnp.sum(e * e)

    def body(h_ref, w_ref, t_ref, l_ref, dh_ref, dw_ref):
        loss, (dh, dw) = jax.value_and_grad(loss_fn, argnums=(0, 1))(h_ref[...], w_ref[...], t_ref[...])
        dh_ref[...] = dh
        _acc_rows((l_ref, dw_ref), (jnp.full((1, LANES), loss, F32), dw), pl.program_id(0) == 0)

    row = lambda i: (i, 0)
    fix = lambda i: (0, 0)
    return pl.pallas_call(
        body, name=name, grid=(n // tm,),
        in_specs=[pl.BlockSpec((tm, d), row), pl.BlockSpec((1, d), fix), pl.BlockSpec((tm, d), row)],
        out_specs=[pl.BlockSpec((1, LANES), fix), pl.BlockSpec((tm, d), row), pl.BlockSpec((1, d), fix)],
        out_shape=[SDS((1, LANES), F32), SDS((n, d), F32), SDS((1, d), F32)],
        compiler_params=_cp(("arbitrary",), VMEM_LIMIT),
    )(h2, nw, tgt)


def _norm_bwd(x, nw, dhn, dres, tm, name):
    n, d = x.shape

    def body(x_ref, w_ref, g_ref, r_ref, dx_ref, dw_ref):
        _, vjp = jax.vjp(_rms, x_ref[...], w_ref[...])
        dx, dw = vjp(g_ref[...])
        dx_ref[...] = r_ref[...] + dx
        _acc_rows((dw_ref,), (dw,), pl.program_id(0) == 0)

    row = lambda i: (i, 0)
    fix = lambda i: (0, 0)
    return pl.pallas_call(
        body, name=name, grid=(n // tm,),
        in_specs=[pl.BlockSpec((tm, d), row), pl.BlockSpec((1, d), fix), pl.BlockSpec((tm, d), row), pl.BlockSpec((tm, d), row)],
        out_specs=[pl.BlockSpec((tm, d), row), pl.BlockSpec((1, d), fix)],
        out_shape=[SDS((n, d), F32), SDS((1, d), F32)],
        compiler_params=_cp(("arbitrary",), VMEM_LIMIT),
    )(x, nw, dhn, dres)


def _row_tile(n, cap=512):
    for t in range(min(cap, n) // 8 * 8, 7, -8):
        if n % t == 0:
            return t
    return n


def _sum_lead(a, name):
    kk, n, c = a.shape
    tm = _row_tile(n)

    def body(a_ref, o_ref):
        acc = a_ref[0].astype(F32)
        for i in range(1, kk):
            acc = acc + a_ref[i].astype(F32)
        o_ref[...] = acc

    return pl.pallas_call(
        body, name=name, grid=(n // tm,),
        in_specs=[pl.BlockSpec((kk, tm, c), lambda i: (0, i, 0))],
        out_specs=pl.BlockSpec((tm, c), lambda i: (i, 0)), out_shape=SDS((n, c), F32),
        compiler_params=_cp(("arbitrary",), VMEM_LIMIT),
    )(a)


def _adamw(w, g, m, v, name):
    n, c = w.shape
    tm = _row_tile(n)

    def body(w_ref, g_ref, m_ref, v_ref, d_ref, nm_ref, nv_ref):
        gv = g_ref[...]
        mn = ADAM_B1 * m_ref[...] + (1.0 - ADAM_B1) * gv
        vn = ADAM_B2 * v_ref[...] + (1.0 - ADAM_B2) * jnp.square(gv)
        m_hat = mn / (1.0 - ADAM_B1 ** ADAM_STEP)
        v_hat = vn / (1.0 - ADAM_B2 ** ADAM_STEP)
        d_ref[...] = -ADAM_LR * (m_hat / (jnp.sqrt(v_hat) + ADAM_EPS) + ADAM_WD * w_ref[...])
        nm_ref[...] = mn
        nv_ref[...] = vn

    spec = pl.BlockSpec((tm, c), lambda i: (i, 0))
    return pl.pallas_call(
        body, name=name, grid=(n // tm,), in_specs=[spec] * 4, out_specs=[spec] * 3,
        out_shape=[SDS((n, c), F32)] * 3, compiler_params=_cp(("arbitrary",), VMEM_LIMIT),
    )(w, g, m, v)


ANY = pl.BlockSpec(memory_space=pl.ANY)


def _me():
    return lax.axis_index("x"), lax.axis_index("y"), lax.axis_index("c")


def _gather_xy(split, whole, name):
    ns, cnt = len(split), len(split) + len(whole)

    def body(*refs):
        src, dst = refs[:cnt], refs[cnt:2 * cnt]
        send, recv = refs[2 * cnt:]
        x, y, c = _me()
        mine = 2 * x + y
        chips = [(1 - x, y), (x, 1 - y), (1 - x, 1 - y)]

        def ici(a, j, slot):
            px, py = chips[j]
            if a < ns:
                s_ref, d_ref = src[a].at[c], dst[a].at[slot].at[c]
            else:
                s_ref, d_ref = src[a], dst[a].at[slot]
            return pltpu.make_async_remote_copy(s_ref, d_ref, send.at[3 * a + j], recv.at[3 * a + j],
                                                device_id=(px, py, c), device_id_type=MESH)

        def d2d(a, j, half):
            px, py = chips[j]
            ref = dst[a].at[2 * px + py].at[half]
            return pltpu.make_async_remote_copy(ref, ref, send.at[3 * cnt + 3 * a + j], recv.at[3 * cnt + 3 * a + j],
                                                device_id=(x, y, 1 - c), device_id_type=MESH)

        def own(a):
            return pltpu.make_async_remote_copy(src[a], dst[a].at[mine], send.at[nsem - cnt + a], recv.at[nsem - cnt + a],
                                                device_id=(x, y, 1 - c), device_id_type=MESH)

        started = []
        for a in range(cnt):
            cp = own(a)
            cp.start()
            started.append(cp)
            for j in range(3):
                cp = ici(a, j, mine)
                cp.start()
                started.append(cp)
        for a in range(cnt):
            for j, (px, py) in enumerate(chips):
                ici(a, j, 2 * px + py).wait_recv()
                if a < ns:
                    cp = d2d(a, j, c)
                    cp.start()
                    started.append(cp)
        for a in range(ns):
            for j in range(3):
                d2d(a, j, 1 - c).wait_recv()
        for a in range(cnt):
            own(a).wait_recv()
        for cp in started:
            cp.wait_send()

    nsem = 3 * cnt + 3 * ns + cnt
    return pl.pallas_call(
        body, name=name, in_specs=[ANY] * cnt, out_specs=[ANY] * cnt,
        out_shape=[SDS((4,) + s.shape, s.dtype) for s in split + whole],
        scratch_shapes=[pltpu.SemaphoreType.DMA((nsem,)), pltpu.SemaphoreType.DMA((nsem,))],
    )(*split, *whole)


HBM = pl.BlockSpec(memory_space=pltpu.HBM)
SEM = pl.BlockSpec(memory_space=pltpu.SEMAPHORE)
DATAFLOW = pltpu.SideEffectType.DATAFLOW_SIDE_EFFECTING


def _whole_copies(srcs, dsts, sends, recvs):
    x, y, c = _me()
    peers = [(1 - x, y, c), (x, 1 - y, c), (1 - x, 1 - y, c), (x, y, 1 - c)]
    return [pltpu.make_async_remote_copy(srcs[a], dsts[a].at[2 * x + y], sends[4 * a + j], recvs[4 * a + j],
                                         device_id=peer, device_id_type=MESH)
            for a in range(len(srcs)) for j, peer in enumerate(peers)]


def _gather_start(shards, after, name):
    cnt = len(shards)
    ncp = 4 * cnt

    def body(*refs):
        srcs, lands = refs[:cnt], refs[cnt:2 * cnt]
        outs = refs[2 * cnt + 1:]
        for cp in _whole_copies(srcs, lands, outs[:ncp], outs[ncp:2 * ncp]):
            cp.start()
        outs[-1][...] = jnp.zeros_like(outs[-1])

    lands = [lax.empty((4,) + s.shape, s.dtype) for s in shards]
    ops = [pltpu.with_memory_space_constraint(a, pltpu.HBM) for a in list(shards) + lands]
    res = pl.pallas_call(
        body, name=name, in_specs=[HBM] * (2 * cnt) + [ANY],
        out_shape=tuple([pltpu.SemaphoreType.DMA(())] * (2 * ncp) + [pltpu.HBM(a.shape, a.dtype) for a in ops] + [SDS((8, LANES), F32)]),
        out_specs=tuple([SEM] * (2 * ncp) + [HBM] * (2 * cnt) + [pl.BlockSpec(memory_space=pltpu.VMEM)]),
        input_output_aliases={i: 2 * ncp + i for i in range(2 * cnt)},
        compiler_params=pltpu.CompilerParams(has_side_effects=DATAFLOW),
    )(*ops, after)
    return res[:2 * ncp], res[2 * ncp:2 * ncp + cnt], res[2 * ncp + cnt:2 * ncp + 2 * cnt], res[-1]


def _gather_wait(handle, after, name):
    sems, srcs, lands, _ = handle
    cnt = len(srcs)
    ncp = 4 * cnt

    def body(*refs):
        sem_refs = refs[2 * cnt:2 * cnt + 2 * ncp]
        for cp in _whole_copies(refs[:cnt], refs[cnt:2 * cnt], sem_refs[:ncp], sem_refs[ncp:]):
            cp.wait_send()
            cp.wait_recv()

    res = pl.pallas_call(
        body, name=name, in_specs=[HBM] * (2 * cnt) + [SEM] * (2 * ncp) + [ANY],
        out_shape=tuple(pltpu.HBM(a.shape, a.dtype) for a in list(srcs) + list(lands)),
        out_specs=tuple([HBM] * (2 * cnt)), input_output_aliases={i: i for i in range(2 * cnt)},
        compiler_params=pltpu.CompilerParams(has_side_effects=DATAFLOW),
    )(*srcs, *lands, *sems, after)
    return list(res[cnt:])


def _swap_sibling(parts, pick, name):
    cnt = len(parts)

    def body(*refs):
        src, dst = refs[:cnt], refs[cnt:2 * cnt]
        send, recv = refs[2 * cnt:]
        x, y, c = _me()
        cps = []
        for a in range(cnt):
            cp = pltpu.make_async_remote_copy(src[a].at[1 - c] if pick else src[a], dst[a], send.at[a], recv.at[a],
                                              device_id=(x, y, 1 - c), device_id_type=MESH)
            cp.start()
            cps.append(cp)
        for cp in cps:
            cp.wait()

    return pl.pallas_call(
        body, name=name, in_specs=[ANY] * cnt, out_specs=[ANY] * cnt,
        out_shape=[SDS(p.shape[1:] if pick else p.shape, p.dtype) for p in parts],
        scratch_shapes=[pltpu.SemaphoreType.DMA((cnt,)), pltpu.SemaphoreType.DMA((cnt,))],
    )(*parts)


def _scatter_xy(parts, name):
    cnt = len(parts)

    def body(*refs):
        src, dst = refs[:cnt], refs[cnt:2 * cnt]
        send, recv, loc = refs[2 * cnt:]
        x, y, c = _me()
        mine = 2 * x + y
        chips = [(1 - x, y), (x, 1 - y), (1 - x, 1 - y)]
        local = []
        for a in range(cnt):
            cp = pltpu.make_async_copy(src[a].at[mine], dst[a].at[mine], loc.at[a])
            cp.start()
            local.append(cp)
        sends = []
        for a in range(cnt):
            for j, (px, py) in enumerate(chips):
                cp = pltpu.make_async_remote_copy(src[a].at[2 * px + py], dst[a].at[mine], send.at[3 * a + j], recv.at[3 * a + j],
                                                  device_id=(px, py, c), device_id_type=MESH)
                cp.start()
                sends.append(cp)
        for a in range(cnt):
            for j, (px, py) in enumerate(chips):
                pltpu.make_async_remote_copy(src[a].at[mine], dst[a].at[2 * px + py], send.at[3 * a + j], recv.at[3 * a + j],
                                             device_id=(px, py, c), device_id_type=MESH).wait_recv()
        for cp in sends:
            cp.wait_send()
        for cp in local:
            cp.wait()

    return pl.pallas_call(
        body, name=name, in_specs=[ANY] * cnt, out_specs=[ANY] * cnt,
        out_shape=[SDS(p.shape, p.dtype) for p in parts],
        scratch_shapes=[pltpu.SemaphoreType.DMA((3 * cnt,)), pltpu.SemaphoreType.DMA((3 * cnt,)), pltpu.SemaphoreType.DMA((cnt,))],
    )(*parts)


def _bcast_all(buf, name):
    def body(src, dst, send, recv, loc):
        x, y, c = _me()
        mine = 4 * x + 2 * y + c
        own = pltpu.make_async_copy(src, dst.at[mine], loc)
        own.start()
        sends = []
        for k in range(1, 8):
            px, py, pc = x ^ (k >> 2), y ^ ((k >> 1) & 1), c ^ (k & 1)
            cp = pltpu.make_async_remote_copy(src, dst.at[mine], send.at[k - 1], recv.at[k - 1],
                                              device_id=(px, py, pc), device_id_type=MESH)
            cp.start()
            sends.append(cp)
        for k in range(1, 8):
            px, py, pc = x ^ (k >> 2), y ^ ((k >> 1) & 1), c ^ (k & 1)
            pltpu.make_async_remote_copy(src, dst.at[4 * px + 2 * py + pc], send.at[k - 1], recv.at[k - 1],
                                         device_id=(px, py, pc), device_id_type=MESH).wait_recv()
        for cp in sends:
            cp.wait_send()
        own.wait()

    return pl.pallas_call(
        body, name=name, in_specs=[ANY], out_specs=ANY, out_shape=SDS((8,) + buf.shape, buf.dtype),
        scratch_shapes=[pltpu.SemaphoreType.DMA((7,)), pltpu.SemaphoreType.DMA((7,)), pltpu.SemaphoreType.DMA(())],
    )(buf)


def _add_half(parts, got, core, dtype, name):
    shp = got.shape
    a2, b2 = parts.reshape(2, -1, shp[-1]), got.reshape(-1, shp[-1])
    n, c = b2.shape
    tm = _row_tile(n, 256)

    def body(core_ref, a_ref, b_ref, o_ref):
        o_ref[...] = (a_ref[...] + b_ref[...]).astype(dtype)

    spec = pl.BlockSpec((tm, c), lambda i, core_ref: (i, 0))
    grid_spec = pltpu.PrefetchScalarGridSpec(
        num_scalar_prefetch=1, grid=(n // tm,),
        in_specs=[pl.BlockSpec((None, tm, c), lambda i, core_ref: (core_ref[0], i, 0)), spec], out_specs=spec)
    return pl.pallas_call(body, name=name, grid_spec=grid_spec, out_shape=SDS((n, c), dtype),
                          compiler_params=_cp(("arbitrary",), VMEM_LIMIT))(core.reshape(1), a2, b2).reshape(shp)


def _x_layout(u, name):
    n, c = u.shape
    t, ch = S5_T, S5_CH
    gb = LANES // ch
    rows = min(64, n // t)

    def body(u_ref, o_ref):
        for s in range(t):
            us = u_ref[pl.ds(s, rows, stride=t), :]
            for g in range(gb):
                o_ref[g, :, ch * s:ch * (s + 1)] = us[:, ch * g:ch * (g + 1)]

    return pl.pallas_call(
        body, name=name, grid=(n // (rows * t), c // LANES),
        in_specs=[pl.BlockSpec((rows * t, LANES), lambda i, j: (i, j))],
        out_specs=pl.BlockSpec((gb, rows, t * ch), lambda i, j: (j, i, 0)),
        out_shape=SDS((c // ch, n // t, t * ch), F32),
        compiler_params=_cp(("arbitrary", "arbitrary"), VMEM_LIMIT),
    )(u)


def _token_layout(xg, name):
    ng, r, w = xg.shape
    t, ch = S5_T, S5_CH
    gb = LANES // ch
    rows = min(64, r)

    def body(x_ref, o_ref):
        for s in range(t):
            parts = [x_ref[g, :, ch * s:ch * (s + 1)].astype(F32) for g in range(gb)]
            o_ref[pl.ds(s, rows, stride=t), :] = jnp.concatenate(parts, axis=1)

    return pl.pallas_call(
        body, name=name, grid=(r // rows, ng // gb),
        in_specs=[pl.BlockSpec((gb, rows, w), lambda i, j: (j, i, 0))],
        out_specs=pl.BlockSpec((rows * t, LANES), lambda i, j: (i, j)),
        out_shape=SDS((r * t, ng * ch), F32),
        compiler_params=_cp(("arbitrary", "arbitrary"), VMEM_LIMIT),
    )(xg)


def _pad_lanes(a, lanes=LANES):
    return jnp.pad(a, ((0, 0), (0, lanes - a.shape[1])))


def _local_step(x, tgt, p, bl, late):
    p = dict(p)
    n, d = x.shape
    sw = SSD_HEADS * SSD_HEAD_DIM
    gn = SSD_GROUPS * SSD_STATE
    tm = min(n, 512)
    tm_ffn = min(n, 256)
    nck = n // bl // S5_T
    s5w = S5_GROUPS * S5_CH

    w_in = p["w_in"]
    o1, o2, o3, o4 = sw, sw + sw, sw + sw + gn, sw + sw + 2 * gn
    w_z, w_xs, w_b, w_c = w_in[:, :o1], w_in[:, o1:o2], w_in[:, o2:o3], w_in[:, o3:o4]
    w_dt = jnp.concatenate([_pad_lanes(w_in[:, o4:o4 + SSD_HEADS]), _pad_lanes(w_in[:, o4 + SSD_HEADS:o4 + 2 * SSD_HEADS])], 1)
    w_u = w_in[:, o4 + 2 * SSD_HEADS:]
    in_ws = [w_z, w_xs, w_b, w_c, w_dt, w_u]
    cw, cb_ = p["ssd_conv_w"], p["ssd_conv_b"]
    conv_parts = [(cw[:, :sw], cb_[:, :sw]), (cw[:, sw:sw + gn], cb_[:, sw:sw + gn]), (cw[:, sw + gn:], cb_[:, sw + gn:])]
    alog2 = jnp.stack([_pad_lanes(p["ssd_a_log_fwd"]), _pad_lanes(p["ssd_a_log_bwd"])])
    dtb2 = jnp.stack([_pad_lanes(p["ssd_dt_bias_fwd"]), _pad_lanes(p["ssd_dt_bias_bwd"])])
    dvec = _pad_lanes(p["ssd_d"]).reshape(LANES, 1)

    hn, z, xs_pre, b_pre, c_pre, dtr, u = _norm_matmul(x, p["norm_mix_w"], in_ws, tm, "in_proj")
    pres = [xs_pre, b_pre, c_pre]
    acts = [_conv_silu(pre, w, b, bl, min(256, pre.shape[1]), f"ssd_conv_{i}") for i, (pre, (w, b)) in enumerate(zip(pres, conv_parts))]
    xs_a, b_a, c_a = acts
    y2, saved = _ssd_scan(xs_a, b_a, c_a, dtr, alog2, dtb2, bl, "ssd_scan")

    def col(a):
        return a.reshape(a.shape + (1,))

    s5_params = [
        col(p["s5_lambda_re_fwd"]), col(p["s5_lambda_im_fwd"]), p["s5_log_step_fwd"].reshape(S5_GROUPS, 1, 1), p["s5_c_re_fwd"], p["s5_c_im_fwd"],
        col(p["s5_lambda_re_bwd"]), col(p["s5_lambda_im_bwd"]), p["s5_log_step_bwd"].reshape(S5_GROUPS, 1, 1), p["s5_c_re_bwd"], p["s5_c_im_bwd"],
        p["s5_b_re"], p["s5_b_im"], col(p["s5_d"].reshape(S5_GROUPS, S5_CH)),
        p["s5_glu_w"][:, :, :S5_CH], p["s5_glu_w"][:, :, S5_CH:], col(p["s5_glu_b"][:, :S5_CH]), col(p["s5_glu_b"][:, S5_CH:]),
    ]
    s5_args = [_x_layout(u, "s5_u_blocks")] + s5_params
    s5o = _token_layout(_s5_fwd(s5_args, nck, "s5_fwd"), "s5_y_tokens")
    ymix = _mix(y2, xs_a, z, s5o, dvec, p["ssd_norm_w"], p["s5_norm_w"], tm, "mix")
    p["w_out"], p["w_up"], p["w_down"] = late(ymix)
    dff = p["w_down"].shape[0]
    h1 = _matmul_res(ymix, p["w_out"], x, tm, "out_proj")
    w_up = p["w_up"]
    hn2, up_v, up_g = _norm_matmul(h1, p["norm_ffn_w"], [w_up[:, :dff], w_up[:, dff:]], tm_ffn, "ffn_up")
    fw, fb = p["ffn_conv_w"], p["ffn_conv_b"]
    act = _conv_glu(up_v, up_g, fw[:, :dff], fw[:, dff:], fb[:, :dff], fb[:, dff:], bl, 256, "ffn_conv")
    h2 = _matmul_res(act, p["w_down"], h1, tm, "ffn_down")
    loss, dh2, g_nfw = _final_loss(h2, p["norm_final_w"].reshape(1, d), tgt, tm, "final_loss")

    g = {"norm_final_w": g_nfw.reshape(d)}
    g["w_down"] = _matmul_tn(act, dh2, tm, d, "ffn_down_dw")
    dact = _matmul_nt([dh2], [p["w_down"]], tm, "ffn_down_dx")
    dup_v, dup_g, dwv, dwg, dbv, dbg = _conv_glu_bwd(up_v, up_g, fw[:, :dff], fw[:, dff:], fb[:, :dff], fb[:, dff:], dact, bl, 256, "ffn_conv_bwd")
    g["ffn_conv_w"] = jnp.concatenate([dwv, dwg], 1)
    g["ffn_conv_b"] = jnp.concatenate([dbv, dbg], 1)
    g["w_up"] = jnp.concatenate([_matmul_tn(hn2, dup_v, tm, dff // 2, "ffn_up_dw_v"), _matmul_tn(hn2, dup_g, tm, dff // 2, "ffn_up_dw_g")], 1)
    dhn2 = _matmul_nt([dup_v, dup_g], [w_up[:, :dff], w_up[:, dff:]], tm_ffn, "ffn_up_dx")
    dh1, g["norm_ffn_w"] = _norm_bwd(h1, p["norm_ffn_w"], dhn2, dh2, tm, "ffn_norm_bwd")
    g["w_out"] = _matmul_tn(ymix, dh1, tm, d, "out_proj_dw")
    dmix = _matmul_nt([dh1], [p["w_out"]], tm, "out_proj_dx")
    dyssd, dxs_gate, dz, ds5o, g_d, g["ssd_norm_w"], g["s5_norm_w"] = _mix_bwd(
        y2, xs_a, z, s5o, dvec, p["ssd_norm_w"], p["s5_norm_w"], dmix, tm, "mix_bwd")
    g["ssd_d"] = g_d[:SSD_HEADS].reshape(1, SSD_HEADS)
    s5g = _s5_bwd(s5_args, _x_layout(ds5o, "s5_dy_blocks"), nck, "s5_bwd")
    du = _token_layout(s5g[0], "s5_du_tokens")
    (g["s5_lambda_re_fwd"], g["s5_lambda_im_fwd"], g["s5_log_step_fwd"], g["s5_c_re_fwd"], g["s5_c_im_fwd"],
     g["s5_lambda_re_bwd"], g["s5_lambda_im_bwd"], g["s5_log_step_bwd"], g["s5_c_re_bwd"], g["s5_c_im_bwd"],
     g["s5_b_re"], g["s5_b_im"], g_s5d, g_wv, g_wg, g_bv, g_bg) = s5g[1:]
    for k_ in ("s5_lambda_re_fwd", "s5_lambda_im_fwd", "s5_lambda_re_bwd", "s5_lambda_im_bwd"):
        g[k_] = g[k_].reshape(S5_GROUPS, S5_STATE)
    for k_ in ("s5_log_step_fwd", "s5_log_step_bwd"):
        g[k_] = g[k_].reshape(S5_GROUPS)
    g["s5_d"] = g_s5d.reshape(1, s5w)
    g["s5_glu_w"] = jnp.concatenate([g_wv, g_wg], 2)
    g["s5_glu_b"] = jnp.concatenate([g_bv.reshape(S5_GROUPS, S5_CH), g_bg.reshape(S5_GROUPS, S5_CH)], 1)
    dxs2, dbm2, dcm2, ddtr, dal2, ddb2 = _ssd_scan_bwd(xs_a, b_a, c_a, dtr, alog2, dtb2, saved, dyssd, bl, "ssd_scan_bwd")
    g["ssd_a_log_fwd"], g["ssd_a_log_bwd"] = dal2[0, :, :SSD_HEADS], dal2[1, :, :SSD_HEADS]
    g["ssd_dt_bias_fwd"], g["ssd_dt_bias_bwd"] = ddb2[0, :, :SSD_HEADS], ddb2[1, :, :SSD_HEADS]
    cots = [[(dxs2, 0), (dxs2, 1), (dxs_gate, None)], [(dbm2, 0), (dbm2, 1)], [(dcm2, 0), (dcm2, 1)]]
    dpres, dcw, dcb = [], [], []
    for i, (pre, (w, b), cot) in enumerate(zip(pres, conv_parts, cots)):
        dp, dw_, db_ = _conv_silu_bwd(pre, w, b, cot, bl, min(256, pre.shape[1]), f"ssd_conv_bwd_{i}")
        dpres.append(dp)
        dcw.append(dw_)
        dcb.append(db_)
    g["ssd_conv_w"] = jnp.concatenate(dcw, 1)
    g["ssd_conv_b"] = jnp.concatenate(dcb, 1)
    dprojs = [dz, dpres[0], dpres[1], dpres[2], ddtr, du]
    dws = [_matmul_tn(hn, dpj, tm, dpj.shape[1], f"in_proj_dw_{i}") for i, dpj in enumerate(dprojs)]
    dws[4] = jnp.concatenate([dws[4][:, :SSD_HEADS], dws[4][:, LANES:LANES + SSD_HEADS]], 1)
    g["w_in"] = jnp.concatenate(dws, 1)
    dhn = _matmul_nt(dprojs, in_ws, tm, "in_proj_dx")
    grad_x, g["norm_mix_w"] = _norm_bwd(x, p["norm_mix_w"], dhn, dh1, tm, "mix_norm_bwd")
    return loss, grad_x, g


_WEIGHTS = ['norm_mix_w', 'w_in', 'ssd_conv_w', 'ssd_conv_b', 'ssd_dt_bias_fwd', 'ssd_dt_bias_bwd', 'ssd_a_log_fwd', 'ssd_a_log_bwd',
            'ssd_d', 'ssd_norm_w', 's5_lambda_re_fwd', 's5_lambda_im_fwd', 's5_log_step_fwd', 's5_lambda_re_bwd', 's5_lambda_im_bwd',
            's5_log_step_bwd', 's5_b_re', 's5_b_im', 's5_c_re_fwd', 's5_c_im_fwd', 's5_c_re_bwd', 's5_c_im_bwd', 's5_d', 's5_glu_w',
            's5_glu_b', 's5_norm_w', 'w_out', 'norm_ffn_w', 'ffn_w_up', 'ffn_conv_w', 'ffn_conv_b', 'ffn_w_down', 'norm_final_w']
_BIG = ('w_in', 'w_out', 'ffn_w_up', 'ffn_w_down')
_CONV = ('ssd_conv_w', 'ffn_conv_w')


def _pack(arrs):
    flat = jnp.concatenate([a.reshape(-1) for a in arrs])
    rows = -(-flat.shape[0] // (64 * LANES)) * 64
    return jnp.pad(flat, (0, rows * LANES - flat.shape[0])).reshape(rows, LANES)


def _unpack(buf, shapes):
    flat = buf.reshape(-1)
    out, off = [], 0
    for shp in shapes:
        size = math.prod(shp)
        out.append(flat[off:off + size].reshape(shp))
        off += size
    return out


def kernel(x, norm_mix_w, w_in, ssd_conv_w, ssd_conv_b, ssd_dt_bias_fwd, ssd_dt_bias_bwd, ssd_a_log_fwd, ssd_a_log_bwd, ssd_d, ssd_norm_w, s5_lambda_re_fwd, s5_lambda_im_fwd, s5_log_step_fwd, s5_lambda_re_bwd, s5_lambda_im_bwd, s5_log_step_bwd, s5_b_re, s5_b_im, s5_c_re_fwd, s5_c_im_fwd, s5_c_re_bwd, s5_c_im_bwd, s5_d, s5_glu_w, s5_glu_b, s5_norm_w, w_out, norm_ffn_w, ffn_w_up, ffn_conv_w, ffn_conv_b, ffn_w_down, norm_final_w, loss_target, m_norm_mix_w, m_w_in, m_ssd_conv_w, m_ssd_conv_b, m_ssd_dt_bias_fwd, m_ssd_dt_bias_bwd, m_ssd_a_log_fwd, m_ssd_a_log_bwd, m_ssd_d, m_ssd_norm_w, m_s5_lambda_re_fwd, m_s5_lambda_im_fwd, m_s5_log_step_fwd, m_s5_lambda_re_bwd, m_s5_lambda_im_bwd, m_s5_log_step_bwd, m_s5_b_re, m_s5_b_im, m_s5_c_re_fwd, m_s5_c_im_fwd, m_s5_c_re_bwd, m_s5_c_im_bwd, m_s5_d, m_s5_glu_w, m_s5_glu_b, m_s5_norm_w, m_w_out, m_norm_ffn_w, m_ffn_w_up, m_ffn_conv_w, m_ffn_conv_b, m_ffn_w_down, m_norm_final_w, v_norm_mix_w, v_w_in, v_ssd_conv_w, v_ssd_conv_b, v_ssd_dt_bias_fwd, v_ssd_dt_bias_bwd, v_ssd_a_log_fwd, v_ssd_a_log_bwd, v_ssd_d, v_ssd_norm_w, v_s5_lambda_re_fwd, v_s5_lambda_im_fwd, v_s5_log_step_fwd, v_s5_lambda_re_bwd, v_s5_lambda_im_bwd, v_s5_log_step_bwd, v_s5_b_re, v_s5_b_im, v_s5_c_re_fwd, v_s5_c_im_fwd, v_s5_c_re_bwd, v_s5_c_im_bwd, v_s5_d, v_s5_glu_w, v_s5_glu_b, v_s5_norm_w, v_w_out, v_norm_ffn_w, v_ffn_w_up, v_ffn_conv_w, v_ffn_conv_b, v_ffn_w_down, v_norm_final_w):
    args = dict(locals())
    w = {k_: args[k_] for k_ in _WEIGHTS}
    m = {k_: args["m_" + k_] for k_ in _WEIGHTS}
    v = {k_: args["v_" + k_] for k_ in _WEIGHTS}
    bl, sl, d = x.shape
    chip = 2 * lax.axis_index("x") + lax.axis_index("y")
    core = lax.axis_index("c")

    first = w["w_in"][0].astype(BF16)
    g_in, g_scw, g_fcw = _gather_xy([first.reshape(2, first.shape[0] // 2, first.shape[1])], [w[k_][0] for k_ in _CONV], "gather_first")
    g_in = g_in.reshape((4,) + first.shape)
    handle = _gather_start([w[k_][0].astype(BF16) for k_ in _BIG[1:]], g_scw, "gather_rest_start")

    def cols(a):
        return jnp.moveaxis(a, 0, 1).reshape(a.shape[1], 4 * a.shape[2])

    p = {k_: (w[k_][0] if w[k_].ndim >= 3 else w[k_]) for k_ in _WEIGHTS if k_ not in _BIG + _CONV}
    p["norm_mix_w"] = p["norm_mix_w"] + handle[3][0, 0]
    p["w_in"] = cols(g_in)
    p["ssd_conv_w"], p["ffn_conv_w"] = cols(g_scw), cols(g_fcw)

    def late(after):
        g_out, g_up, g_down = _gather_wait(handle, after, "gather_rest_wait")
        return g_out.reshape(-1, g_out.shape[2]), cols(g_up), g_down.reshape(-1, g_down.shape[2])

    loss, grad_x, g = _local_step(x.reshape(bl * sl, d), loss_target.reshape(bl * sl, d), p, bl, late)
    g["ffn_w_up"], g["ffn_w_down"] = g.pop("w_up"), g.pop("w_down")

    def owner_major(a, k_):
        r, c = w[k_].shape[1:]
        if a.shape[0] == r:
            a = jnp.moveaxis(a.reshape(r, 4, c), 1, 0)
        else:
            a = a.reshape(4, r, c)
        return a.reshape(4, 2, r // 2, c)

    small = [k_ for k_ in _WEIGHTS if k_ not in _BIG]
    small_full_shapes = [g[k_].shape for k_ in small]
    buf = _pack([g[k_] for k_ in small] + [loss[0, :1]])
    parts = [jnp.moveaxis(owner_major(g[k_], k_), 1, 0) for k_ in _BIG]
    parts.append(jnp.moveaxis(buf.reshape(4, 2, -1, LANES), 1, 0))
    got = _swap_sibling(parts, True, "reduce_sibling")
    chip_sums = [_add_half(pt, gt, core, BF16 if i < len(_BIG) else F32, f"reduce_add_{i}") for i, (pt, gt) in enumerate(zip(parts, got))]
    from_chips = _scatter_xy(chip_sums, "reduce_chips")
    halves = [_sum_lead(a.reshape(4, -1, a.shape[-1]), f"reduce_sum_{i}") for i, a in enumerate(from_chips)]
    other = _swap_sibling(halves[:-1], False, "reduce_join")
    big_grad = {}
    for k_, own_half, sib_half in zip(_BIG, halves, other):
        south = core == 0
        full = jnp.stack([jnp.where(south, own_half, sib_half), jnp.where(south, sib_half, own_half)])
        big_grad[k_] = full.reshape((1,) + w[k_].shape[1:])
    tot = _bcast_all(halves[-1], "reduce_small").reshape(buf.shape)
    unp = _unpack(tot, small_full_shapes + [(1,)])
    small_grad = dict(zip(small, unp[:-1]))
    loss_out = unp[-1].reshape(())
    for k_ in _CONV:
        cshard = w[k_].shape[2]
        small_grad[k_] = lax.dynamic_slice_in_dim(small_grad[k_], chip * cshard, cshard, 1)

    grads, deltas, new_m, new_v = {}, {}, {}, {}
    for k_ in _BIG:
        shp = w[k_].shape
        grads[k_] = big_grad[k_]
        dl, nm, nv = _adamw(w[k_][0], big_grad[k_][0], m[k_][0], v[k_][0], f"adamw_{k_}")
        deltas[k_], new_m[k_], new_v[k_] = dl.reshape(shp), nm.reshape(shp), nv.reshape(shp)
    sw_ = _pack([w[k_] for k_ in small])
    sg_ = _pack([small_grad[k_] for k_ in small])
    sm_ = _pack([m[k_] for k_ in small])
    sv_ = _pack([v[k_] for k_ in small])
    dl, nm, nv = _adamw(sw_, sg_, sm_, sv_, "adamw_small")
    shapes = [w[k_].shape for k_ in small]
    for k_, a, b, c_ in zip(small, _unpack(dl, shapes), _unpack(nm, shapes), _unpack(nv, shapes)):
        deltas[k_], new_m[k_], new_v[k_] = a, b, c_
        grads[k_] = small_grad[k_].reshape(w[k_].shape)
    return (loss_out, grad_x.reshape(bl, sl, d), *[grads[k_] for k_ in _WEIGHTS], *[deltas[k_] for k_ in _WEIGHTS],
            *[new_m[k_] for k_ in _WEIGHTS], *[new_v[k_] for k_ in _WEIGHTS])
```

```python
import functools
import math

import jax
import jax.numpy as jnp
from jax import lax
from jax.experimental import pallas as pl
from jax.experimental.pallas import tpu as pltpu

F32 = jnp.float32
BF16 = jnp.bfloat16
HI = lax.Precision.HIGHEST
SDS = jax.ShapeDtypeStruct
MESH = pl.DeviceIdType.MESH

NN = (((1,), (0,)), ((), ()))
NT = (((1,), (1,)), ((), ()))
TN = (((0,), (0,)), ((), ()))

EPS = 1e-6
SSD_HEADS = 16
SSD_HEAD_DIM = 64
SSD_GROUPS = 4
SSD_STATE = 128
SSD_CHUNK = 128
SSD_CONV = 5
S5_GROUPS = 32
S5_CH = 16
S5_STATE = 64
S5_T = 16
LANES = 128
ADAM_LR, ADAM_B1, ADAM_B2, ADAM_EPS, ADAM_WD, ADAM_STEP = 0.001, 0.9, 0.999, 1e-08, 0.01, 10
V7X_VMEM_BYTES = 64 * 1024 * 1024
VMEM_LIMIT = V7X_VMEM_BYTES - 8 * 1024 * 1024


def _cp(sem, vmem=None):
    return pltpu.CompilerParams(dimension_semantics=sem, vmem_limit_bytes=vmem)


def _dot(a, b, dims=NN, precision=None):
    return lax.dot_general(a, b, dims, precision=precision, preferred_element_type=F32)


def _rms(x, w):
    return x * lax.rsqrt(jnp.mean(x * x, axis=-1, keepdims=True) + EPS) * w


def _sigmoid(x):
    return 1.0 / (1.0 + jnp.exp(-x))


def _softplus(x):
    return jnp.maximum(x, 0.0) + jnp.log1p(jnp.exp(-jnp.abs(x)))


@functools.partial(jax.custom_vjp, nondiff_argnums=(1, 2))
def _shift(x, k, seg):
    n = x.shape[0]
    r = lax.broadcasted_iota(jnp.int32, x.shape, 0)
    if seg != n:
        r = r & (seg - 1) if seg & (seg - 1) == 0 else r % seg
    y = pltpu.roll(x, k % n, 0)
    ok = (r >= k) if k > 0 else (r < seg + k)
    return jnp.where(ok, y, 0.0)


def _shift_fwd(x, k, seg):
    return _shift(x, k, seg), None


def _shift_bwd(k, seg, _, g):
    return (_shift(g, -k, seg),)


_shift.defvjp(_shift_fwd, _shift_bwd)


@functools.partial(jax.custom_vjp, nondiff_argnums=(1,))
def _lane_shift(x, k):
    if k == 0:
        return x
    n = x.shape[1]
    lane = lax.broadcasted_iota(jnp.int32, x.shape, 1)
    ok = (lane >= k) if k > 0 else (lane < n + k)
    return jnp.where(ok, pltpu.roll(x, k % n, 1), 0.0)


_lane_shift.defvjp(lambda x, k: (_lane_shift(x, k), None), lambda k, _, g: (_lane_shift(g, -k),))


@jax.custom_vjp
def _swap(z):
    return pltpu.roll(z, LANES // 2, 1)


_swap.defvjp(lambda z: (_swap(z), None), lambda _, g: (_swap(g),))


def _norm_matmul(x, nw, ws, tm, name):
    n, d = x.shape
    k = len(ws)

    def body(x_ref, nw_ref, *refs):
        hn = _rms(x_ref[...], nw_ref[...]).astype(BF16)
        refs[k][...] = hn
        for w_ref, o_ref in zip(refs[:k], refs[k + 1:]):
            o_ref[...] = _dot(hn, w_ref[...])

    row = lambda i: (i, 0)
    fix = lambda i: (0, 0)
    return pl.pallas_call(
        body, name=name, grid=(n // tm,),
        in_specs=[pl.BlockSpec((tm, d), row), pl.BlockSpec((1, d), fix)] + [pl.BlockSpec(w.shape, fix) for w in ws],
        out_specs=[pl.BlockSpec((tm, d), row)] + [pl.BlockSpec((tm, w.shape[1]), row) for w in ws],
        out_shape=[SDS((n, d), BF16)] + [SDS((n, w.shape[1]), F32) for w in ws],
        compiler_params=_cp(("arbitrary",), VMEM_LIMIT),
    )(x, nw, *ws)


def _matmul_res(a, w, res, tm, name):
    n, kd = a.shape
    m = w.shape[1]

    def body(a_ref, w_ref, r_ref, o_ref):
        o_ref[...] = r_ref[...] + _dot(a_ref[...], w_ref[...])

    return pl.pallas_call(
        body, name=name, grid=(n // tm,),
        in_specs=[pl.BlockSpec((tm, kd), lambda i: (i, 0)), pl.BlockSpec((kd, m), lambda i: (0, 0)),
                  pl.BlockSpec((tm, m), lambda i: (i, 0))],
        out_specs=pl.BlockSpec((tm, m), lambda i: (i, 0)),
        out_shape=SDS((n, m), F32),
        compiler_params=_cp(("arbitrary",), VMEM_LIMIT),
    )(a, w, res)


def _matmul_nt(gs, ws, tm, name):
    n = gs[0].shape[0]
    kd = ws[0].shape[0]
    cnt = len(gs)

    def body(*refs):
        acc = None
        for g_ref, w_ref in zip(refs[:cnt], refs[cnt:2 * cnt]):
            t = _dot(g_ref[...].astype(BF16), w_ref[...], NT)
            acc = t if acc is None else acc + t
        refs[2 * cnt][...] = acc

    return pl.pallas_call(
        body, name=name, grid=(n // tm,),
        in_specs=[pl.BlockSpec((tm, g.shape[1]), lambda i: (i, 0)) for g in gs]
        + [pl.BlockSpec(w.shape, lambda i: (0, 0)) for w in ws],
        out_specs=pl.BlockSpec((tm, kd), lambda i: (i, 0)),
        out_shape=SDS((n, kd), F32),
        compiler_params=_cp(("arbitrary",), VMEM_LIMIT),
    )(*gs, *ws)


def _matmul_tn(a, g, tm, cb, name):
    n, kd = a.shape
    m = g.shape[1]

    def body(a_ref, g_ref, o_ref):
        t = _dot(a_ref[...], g_ref[...].astype(BF16), TN)

        @pl.when(pl.program_id(1) == 0)
        def _():
            o_ref[...] = t

        @pl.when(pl.program_id(1) != 0)
        def _():
            o_ref[...] += t

    return pl.pallas_call(
        body, name=name, grid=(m // cb, n // tm),
        in_specs=[pl.BlockSpec((tm, kd), lambda j, i: (i, 0)), pl.BlockSpec((tm, cb), lambda j, i: (i, j))],
        out_specs=pl.BlockSpec((kd, cb), lambda j, i: (0, j)),
        out_shape=SDS((kd, m), F32),
        compiler_params=_cp(("arbitrary", "arbitrary"), VMEM_LIMIT),
    )(a, g)


def _dwconv(x, w, b):
    kw = w.shape[0]
    acc = b
    for k in range(kw):
        acc = acc + w[k:k + 1, :] * _shift(x, kw // 2 - k, x.shape[0])
    return acc


def _conv_silu_fn(x, w, b):
    y = _dwconv(x, w, b)
    return y * _sigmoid(y)


def _conv_glu_fn(v, g, wv, wg, bv, bg):
    cv = _dwconv(v, wv, bv)
    cg = _dwconv(g, wg, bg)
    return cg * _sigmoid(cg) * cv


def _conv_silu(x, w, b, bl, cb, name):
    n, c = x.shape
    sl = n // bl
    kw = w.shape[0]

    def body(x_ref, w_ref, b_ref, o_ref):
        o_ref[...] = _conv_silu_fn(x_ref[...], w_ref[...], b_ref[...])

    return pl.pallas_call(
        body, name=name, grid=(bl, c // cb),
        in_specs=[pl.BlockSpec((sl, cb), lambda s, j: (s, j)), pl.BlockSpec((kw, cb), lambda s, j: (0, j)),
                  pl.BlockSpec((1, cb), lambda s, j: (0, j))],
        out_specs=pl.BlockSpec((sl, cb), lambda s, j: (s, j)),
        out_shape=SDS((n, c), F32),
        compiler_params=_cp(("arbitrary", "arbitrary"), VMEM_LIMIT),
    )(x, w, b)


def _conv_silu_bwd(x, w, b, dys, bl, cb, name):
    n, c = x.shape
    sl = n // bl
    kw = w.shape[0]
    cnt = len(dys)

    def body(x_ref, w_ref, b_ref, *refs):
        dy = refs[0][...]
        for r in refs[1:cnt]:
            dy = dy + r[...]
        dx_ref, dw_ref, db_ref = refs[cnt:]
        _, vjp = jax.vjp(_conv_silu_fn, x_ref[...], w_ref[...], b_ref[...])
        dx, dw, db = vjp(dy)
        dx_ref[...] = dx.astype(BF16)

        @pl.when(pl.program_id(1) == 0)
        def _():
            dw_ref[...] = dw
            db_ref[...] = db

        @pl.when(pl.program_id(1) != 0)
        def _():
            dw_ref[...] += dw
            db_ref[...] += db

    dy_specs = []
    for arr, lead in dys:
        if lead is None:
            dy_specs.append(pl.BlockSpec((sl, cb), lambda j, s: (s, j)))
        else:
            dy_specs.append(pl.BlockSpec((None, sl, cb), functools.partial(lambda j, s, lead: (lead, s, j), lead=lead)))
    return pl.pallas_call(
        body, name=name, grid=(c // cb, bl),
        in_specs=[pl.BlockSpec((sl, cb), lambda j, s: (s, j)), pl.BlockSpec((kw, cb), lambda j, s: (0, j)),
                  pl.BlockSpec((1, cb), lambda j, s: (0, j))] + dy_specs,
        out_specs=[pl.BlockSpec((sl, cb), lambda j, s: (s, j)), pl.BlockSpec((kw, cb), lambda j, s: (0, j)),
                   pl.BlockSpec((1, cb), lambda j, s: (0, j))],
        out_shape=[SDS((n, c), BF16), SDS((kw, c), F32), SDS((1, c), F32)],
        compiler_params=_cp(("arbitrary", "arbitrary"), VMEM_LIMIT),
    )(x, w, b, *[a for a, _ in dys])


def _conv_glu(v, g, wv, wg, bv, bg, bl, cb, name):
    n, c = v.shape
    sl = n // bl
    kw = wv.shape[0]

    def body(v_ref, g_ref, wv_ref, wg_ref, bv_ref, bg_ref, o_ref):
        o_ref[...] = _conv_glu_fn(v_ref[...], g_ref[...], wv_ref[...], wg_ref[...], bv_ref[...], bg_ref[...]).astype(BF16)

    big = pl.BlockSpec((sl, cb), lambda s, j: (s, j))
    wsp = pl.BlockSpec((kw, cb), lambda s, j: (0, j))
    bsp = pl.BlockSpec((1, cb), lambda s, j: (0, j))
    return pl.pallas_call(
        body, name=name, grid=(bl, c // cb),
        in_specs=[big, big, wsp, wsp, bsp, bsp], out_specs=big, out_shape=SDS((n, c), BF16),
        compiler_params=_cp(("arbitrary", "arbitrary"), VMEM_LIMIT),
    )(v, g, wv, wg, bv, bg)


def _conv_glu_bwd(v, g, wv, wg, bv, bg, dact, bl, cb, name):
    n, c = v.shape
    sl = n // bl
    kw = wv.shape[0]

    def body(v_ref, g_ref, wv_ref, wg_ref, bv_ref, bg_ref, da_ref, dv_ref, dg_ref, dwv_ref, dwg_ref, dbv_ref, dbg_ref):
        _, vjp = jax.vjp(_conv_glu_fn, v_ref[...], g_ref[...], wv_ref[...], wg_ref[...], bv_ref[...], bg_ref[...])
        dv, dg, dwv, dwg, dbv, dbg = vjp(da_ref[...])
        dv_ref[...] = dv.astype(BF16)
        dg_ref[...] = dg.astype(BF16)

        @pl.when(pl.program_id(1) == 0)
        def _():
            dwv_ref[...] = dwv
            dwg_ref[...] = dwg
            dbv_ref[...] = dbv
            dbg_ref[...] = dbg

        @pl.when(pl.program_id(1) != 0)
        def _():
            dwv_ref[...] += dwv
            dwg_ref[...] += dwg
            dbv_ref[...] += dbv
            dbg_ref[...] += dbg

    big = pl.BlockSpec((sl, cb), lambda j, s: (s, j))
    wsp = pl.BlockSpec((kw, cb), lambda j, s: (0, j))
    bsp = pl.BlockSpec((1, cb), lambda j, s: (0, j))
    return pl.pallas_call(
        body, name=name, grid=(c // cb, bl),
        in_specs=[big, big, wsp, wsp, bsp, bsp, big],
        out_specs=[big, big, wsp, wsp, bsp, bsp],
        out_shape=[SDS((n, c), BF16), SDS((n, c), BF16), SDS((kw, c), F32), SDS((kw, c), F32), SDS((1, c), F32), SDS((1, c), F32)],
        compiler_params=_cp(("arbitrary", "arbitrary"), VMEM_LIMIT),
    )(v, g, wv, wg, bv, bg, dact)


_DIMS_T = {NN: (NT, TN, False, False), NT: (NN, TN, False, True), TN: (NT, NN, True, False)}


@functools.partial(jax.custom_vjp, nondiff_argnums=(2,))
def _bdot(a, b, dims):
    return _dot(a.astype(BF16), b.astype(BF16), dims)


def _bdot_fwd(a, b, dims):
    return _bdot(a, b, dims), (a, b)


def _bdot_bwd(dims, res, g):
    a, b = res
    da_dims, db_dims, a_swapped, b_swapped = _DIMS_T[dims]
    da = _bdot(b, g, da_dims) if a_swapped else _bdot(g, b, da_dims)
    db = _bdot(g, a, db_dims) if b_swapped else _bdot(a, g, db_dims)
    return da, db


_bdot.defvjp(_bdot_fwd, _bdot_bwd)


@functools.partial(jax.custom_vjp, nondiff_argnums=(1,))
def _expand_heads(v, width):
    return _split_dot(v, _head_matrix(width), NN)


def _head_matrix(width):
    hr = lax.broadcasted_iota(jnp.int32, (LANES, SSD_HEADS * width), 0)
    hc = lax.broadcasted_iota(jnp.int32, (LANES, SSD_HEADS * width), 1)
    return (hc // width == hr).astype(BF16)


def _split_dot(v, e, dims):
    hi = v.astype(BF16)
    lo = (v - hi.astype(F32)).astype(BF16)
    return _dot(hi, e, dims) + _dot(lo, e, dims)


_expand_heads.defvjp(lambda v, width: (_expand_heads(v, width), None),
                     lambda width, _, g: (_split_dot(g, _head_matrix(width), NT),))


def _ssd_chunk_fn(rev, xs, dtr, bms, cms, st, alog, dtb):
    q = dtr.shape[0]
    hd, per = SSD_HEAD_DIM, SSD_HEADS // SSD_GROUPS
    gw = per * hd
    r = lax.broadcasted_iota(jnp.int32, (q, q), 0)
    c = lax.broadcasted_iota(jnp.int32, (q, q), 1)
    sgn = 1 - 2 * rev
    tri = ((r - c) * sgn >= 0).astype(F32)
    tri_t = ((c - r) * sgn >= 0).astype(F32)
    r4 = lax.broadcasted_iota(jnp.int32, (q, per * q), 0)
    c4 = lax.broadcasted_iota(jnp.int32, (q, per * q), 1) % q
    mask4 = (r4 - c4) * sgn >= 0
    bdr = lax.broadcasted_iota(jnp.int32, (per * q, gw), 0) // q
    bdc = lax.broadcasted_iota(jnp.int32, (per * q, gw), 1) // hd
    diag = bdr == bdc
    dt = _softplus(dtr + dtb)
    dta = dt * (-jnp.exp(alog))
    cs = _dot(tri, dta, NN, HI)
    cs_t = _dot(dta, tri_t, TN, HI)
    tot = jnp.sum(dta, axis=0, keepdims=True)
    dt_x = _expand_heads(dt, hd)
    in_x = _expand_heads(jnp.exp(cs), hd)
    out_x = _expand_heads(jnp.exp(tot - cs), hd)
    ys, outs = [], []
    for g in range(SSD_GROUPS):
        bg, cg = bms[g], cms[g]
        heads = range(per * g, per * (g + 1))
        lanes = slice(gw * g, gw * (g + 1))
        scores = _bdot(cg, bg, NT)
        col = jnp.concatenate([jnp.broadcast_to(cs[:, h:h + 1], (q, q)) for h in heads], axis=1)
        row = jnp.concatenate([cs_t[h:h + 1, :] for h in heads], axis=1)
        seg = jnp.where(mask4, jnp.exp(jnp.where(mask4, col - row, 0.0)), 0.0)
        mcat = jnp.concatenate([scores] * per, axis=1) * seg
        xdt = xs[g] * dt_x[:, lanes]
        blocks = jnp.where(diag, jnp.concatenate([xdt] * per, axis=0), 0.0)
        y = _bdot(mcat, blocks, NN) + in_x[:, lanes] * _bdot(cg, st[g], NT)
        new = _bdot(xdt * out_x[:, lanes], bg, TN)
        keep = jnp.concatenate([jnp.exp(tot[:, h:h + 1]) * st[g][hd * j:hd * (j + 1), :] for j, h in enumerate(heads)], axis=0)
        ys.append(y)
        outs.append(keep + new)
    return ys, outs


def _ssd_scan(xs, bm, cm, dtr, alog2, dtb2, bl, name):
    n = xs.shape[0]
    q = SSD_CHUNK
    nc = n // bl // q
    hd, ns = SSD_HEAD_DIM, SSD_STATE
    gw = SSD_HEADS // SSD_GROUPS * hd

    def body(xs_ref, b_ref, c_ref, dt_ref, al_ref, db_ref, y_ref, sv_ref, st_ref):
        d, i = pl.program_id(0), pl.program_id(2)

        @pl.when(i == 0)
        def _():
            st_ref[...] = jnp.zeros(st_ref.shape, F32)

        st = [st_ref[gw * g:gw * (g + 1), :] for g in range(SSD_GROUPS)]
        sv_ref[...] = st_ref[...]
        xl = [xs_ref[:, gw * g:gw * (g + 1)] for g in range(SSD_GROUPS)]
        bms = [b_ref[:, ns * g:ns * (g + 1)] for g in range(SSD_GROUPS)]
        cms = [c_ref[:, ns * g:ns * (g + 1)] for g in range(SSD_GROUPS)]
        ys, outs = _ssd_chunk_fn(d, xl, dt_ref[...], bms, cms, st, al_ref[...], db_ref[...])
        for g in range(SSD_GROUPS):
            st_ref[gw * g:gw * (g + 1), :] = outs[g]
            y_ref[:, gw * g:gw * (g + 1)] = ys[g]

    def rowblk(d, s, i):
        return s * nc + i + d * (nc - 1 - 2 * i)

    return pl.pallas_call(
        body, name=name, grid=(2, bl, nc),
        in_specs=[pl.BlockSpec((q, SSD_HEADS * hd), lambda d, s, i: (rowblk(d, s, i), 0)),
                  pl.BlockSpec((q, SSD_GROUPS * ns), lambda d, s, i: (rowblk(d, s, i), 0)),
                  pl.BlockSpec((q, SSD_GROUPS * ns), lambda d, s, i: (rowblk(d, s, i), 0)),
                  pl.BlockSpec((q, LANES), lambda d, s, i: (rowblk(d, s, i), d)),
                  pl.BlockSpec((None, 1, LANES), lambda d, s, i: (d, 0, 0)),
                  pl.BlockSpec((None, 1, LANES), lambda d, s, i: (d, 0, 0))],
        out_specs=[pl.BlockSpec((None, q, SSD_HEADS * hd), lambda d, s, i: (d, rowblk(d, s, i), 0)),
                   pl.BlockSpec((None, None, SSD_HEADS * hd, ns), lambda d, s, i: (d, rowblk(d, s, i), 0, 0))],
        out_shape=[SDS((2, n, SSD_HEADS * hd), F32), SDS((2, n // q, SSD_HEADS * hd, ns), F32)],
        scratch_shapes=[pltpu.VMEM((SSD_HEADS * hd, ns), F32)],
        compiler_params=_cp(("arbitrary",) * 3, VMEM_LIMIT),
    )(xs, bm, cm, dtr, alog2, dtb2)


def _ssd_scan_bwd(xs, bm, cm, dtr, alog2, dtb2, saved, dy, bl, name):
    n = xs.shape[0]
    q = SSD_CHUNK
    nc = n // bl // q
    hd, ns = SSD_HEAD_DIM, SSD_STATE
    gw = SSD_HEADS // SSD_GROUPS * hd

    def body(xs_ref, b_ref, c_ref, dt_ref, al_ref, db_ref, sv_ref, dy_ref,
             dxs_ref, dbm_ref, dcm_ref, ddt_ref, dal_ref, ddb_ref, ds_ref):
        d, s, i = pl.program_id(0), pl.program_id(1), pl.program_id(2)

        @pl.when(i == 0)
        def _():
            ds_ref[...] = jnp.zeros(ds_ref.shape, F32)

        xl = [xs_ref[:, gw * g:gw * (g + 1)] for g in range(SSD_GROUPS)]
        bms = [b_ref[:, ns * g:ns * (g + 1)] for g in range(SSD_GROUPS)]
        cms = [c_ref[:, ns * g:ns * (g + 1)] for g in range(SSD_GROUPS)]
        st = [sv_ref[gw * g:gw * (g + 1), :] for g in range(SSD_GROUPS)]
        fn = functools.partial(_ssd_chunk_fn, d)
        _, vjp = jax.vjp(fn, xl, dt_ref[...], bms, cms, st, al_ref[...], db_ref[...])
        dys = [dy_ref[:, gw * g:gw * (g + 1)] for g in range(SSD_GROUPS)]
        dso = [ds_ref[gw * g:gw * (g + 1), :] for g in range(SSD_GROUPS)]
        dxl, ddt, dbg, dcg, dst, dal, ddb = vjp((dys, dso))
        for g in range(SSD_GROUPS):
            ds_ref[gw * g:gw * (g + 1), :] = dst[g]
            dxs_ref[:, gw * g:gw * (g + 1)] = dxl[g]
            dbm_ref[:, ns * g:ns * (g + 1)] = dbg[g]
            dcm_ref[:, ns * g:ns * (g + 1)] = dcg[g]
        ddt_ref[...] = ddt
        _acc_rows((dal_ref, ddb_ref), (dal, ddb), jnp.logical_and(s == 0, i == 0))

    def rowblk(d, s, i):
        return s * nc + (nc - 1 - i) + d * (2 * i - (nc - 1))

    row = lambda d, s, i: (rowblk(d, s, i), 0)
    drow = lambda d, s, i: (d, rowblk(d, s, i), 0)
    dfix = lambda d, s, i: (d, 0, 0)
    dcol = lambda d, s, i: (rowblk(d, s, i), d)
    return pl.pallas_call(
        body, name=name, grid=(2, bl, nc),
        in_specs=[pl.BlockSpec((q, SSD_HEADS * hd), row), pl.BlockSpec((q, SSD_GROUPS * ns), row),
                  pl.BlockSpec((q, SSD_GROUPS * ns), row), pl.BlockSpec((q, LANES), dcol),
                  pl.BlockSpec((None, 1, LANES), dfix), pl.BlockSpec((None, 1, LANES), dfix),
                  pl.BlockSpec((None, None, SSD_HEADS * hd, ns), lambda d, s, i: (d, rowblk(d, s, i), 0, 0)),
                  pl.BlockSpec((q, SSD_HEADS * hd), row)],
        out_specs=[pl.BlockSpec((None, q, SSD_HEADS * hd), drow), pl.BlockSpec((None, q, SSD_GROUPS * ns), drow),
                   pl.BlockSpec((None, q, SSD_GROUPS * ns), drow), pl.BlockSpec((q, LANES), dcol),
                   pl.BlockSpec((None, 1, LANES), dfix), pl.BlockSpec((None, 1, LANES), dfix)],
        out_shape=[SDS((2, n, SSD_HEADS * hd), F32), SDS((2, n, SSD_GROUPS * ns), F32), SDS((2, n, SSD_GROUPS * ns), F32),
                   SDS((n, 2 * LANES), F32), SDS((2, 1, LANES), F32), SDS((2, 1, LANES), F32)],
        scratch_shapes=[pltpu.VMEM((SSD_HEADS * hd, ns), F32)],
        compiler_params=_cp(("arbitrary",) * 3, VMEM_LIMIT),
    )(xs, bm, cm, dtr, alog2, dtb2, saved, dy)


def _s5_consts():
    t, ch, p = S5_T, S5_CH, S5_STATE
    lane = lax.broadcasted_iota(jnp.int32, (1, 2 * p), 1)
    pr = lax.broadcasted_iota(jnp.int32, (p, 2 * p), 0)
    pc = lax.broadcasted_iota(jnp.int32, (p, 2 * p), 1)
    cr = lax.broadcasted_iota(jnp.int32, (ch, t * ch), 0)
    cc = lax.broadcasted_iota(jnp.int32, (ch, t * ch), 1)
    return dict(
        left=lane < p,
        sg=jnp.where(lane < p, -1.0, 1.0).astype(F32),
        dup=(pc % p == pr).astype(F32),
        dup_l=(pc == pr).astype(F32),
        dup_r=(pc == pr + p).astype(F32),
        rep=(cc % ch == cr).astype(F32),
        rep0=(cc == cr).astype(F32),
    )


def _s5_mats(k, rev, lr, li, ls, bre, bim, cre, cim):
    t = S5_T
    step = jnp.exp(ls)
    lr2 = jnp.sum(lr * k["dup"], axis=0, keepdims=True)
    li2 = jnp.sum(li * k["dup"], axis=0, keepdims=True)

    def erow(d):
        ang = (d * step) * li2
        return jnp.exp((d * step) * lr2) * jnp.where(k["left"], jnp.cos(ang), jnp.sin(ang))

    es = [erow(d) for d in range(t + 1)]
    mag = jnp.exp(step * lr)
    ar, ai = mag * jnp.cos(step * li), mag * jnp.sin(step * li)
    den = lr * lr + li * li
    zr = ((ar - 1.0) * lr + ai * li) / den
    zi = (ai * lr - (ar - 1.0) * li) / den
    bbr = zr * bre - zi * bim
    bbi = zr * bim + zi * bre
    bt1 = _dot(bbr, k["dup"], TN, HI)
    bt2 = _dot(bbi, k["dup"], TN, HI)
    bst = _dot(bbr, k["dup_l"], TN, HI) - _dot(bbi, k["dup_r"], TN, HI)
    c1 = _dot(cre, k["dup"], NN, HI)
    c2 = _dot(cim, k["dup"], NN, HI)
    sg = k["sg"]
    ce = [e * c1 + sg * _swap(e) * c2 for e in es]
    lags = range(t - 1, -1, -1) if rev else range(t)
    kt = _dot(bst, jnp.concatenate([ce[d] for d in lags], axis=0), NT, HI)
    toep = jnp.concatenate([_lane_shift(kt, -S5_CH * (t - 1 - s) if rev else S5_CH * s) for s in range(t)], axis=0)
    w_out =jnp.concatenate([ce[(t - qq) if rev else (qq + 1)] * (-sg) for qq in range(t)], axis=0)
    w_st = jnp.concatenate(
        [(lambda e: e * bt1 + sg * _swap(e) * bt2)(es[s if rev else (t - 1 - s)]) for s in range(t)], axis=0)
    return toep, w_out, w_st, es[t]


def _cmul_row(k, e, z):
    es = _swap(e)
    return z * jnp.where(k["left"], e, es) + _swap(z) * (k["sg"] * jnp.where(k["left"], es, e))


def _s5_dir(k, rev, nck, x, mats):
    toep, w_out, w_st, a_t = mats
    acc = _dot(x, w_st)
    e = a_t
    kk = 1
    sign = -1 if rev else 1
    while kk < nck:
        acc = acc + _cmul_row(k, e, _shift(acc, sign * kk, nck))
        e = _cmul_row(k, e, e)
        kk *= 2
    prev = _shift(acc, sign, nck)
    return _dot(x, toep) + _dot(prev, w_out, NT)


def _s5_group_fn(nck, x, pf, pb, bre, bim, dcol, wv, wg, bv, bg):
    k = _s5_consts()
    t = S5_T
    y = x * jnp.sum(dcol * k["rep"], axis=0, keepdims=True)
    for rev, (lr, li, ls, cre, cim) in ((False, pf), (True, pb)):
        y = y + _s5_dir(k, rev, nck, x, _s5_mats(k, rev, lr, li, ls, bre, bim, cre, cim))
    gy = jax.nn.gelu(y)
    def kron_eye(w16):
        wide = _dot(w16, k["rep0"], NN, HI)
        return jnp.concatenate([_lane_shift(wide, S5_CH * qq) for qq in range(t)], axis=0)

    kv, kg = kron_eye(wv), kron_eye(wg)
    val =_dot(gy, kv) + jnp.sum(bv * k["rep"], axis=0, keepdims=True)
    gate = _dot(gy, kg) + jnp.sum(bg * k["rep"], axis=0, keepdims=True)
    return val * _sigmoid(gate)


def _s5_specs(r):
    p, ch = S5_STATE, S5_CH
    g3 = lambda i: (i, 0, 0)
    col = pl.BlockSpec((None, p, 1), g3)
    one = pl.BlockSpec((None, 1, 1), g3)
    cmat = pl.BlockSpec((None, ch, p), g3)
    bmat = pl.BlockSpec((None, p, ch), g3)
    ccol = pl.BlockSpec((None, ch, 1), g3)
    sq = pl.BlockSpec((None, ch, ch), g3)
    xs = pl.BlockSpec((None, r, S5_T * ch), g3)
    specs = [xs, col, col, one, cmat, cmat, col, col, one, cmat, cmat, bmat, bmat, ccol, sq, sq, ccol, ccol]
    return specs


def _s5_unpack(vals):
    x = vals[0]
    pf = tuple(vals[1:6])
    pb = tuple(vals[6:11])
    bre, bim, dcol, wv, wg, bv, bg = vals[11:18]
    return x, pf, pb, bre, bim, dcol, wv, wg, bv, bg


def _s5_fwd(args, nck, name):
    x = args[0]
    ng, r, w = x.shape

    def body(*refs):
        vals = [ref[...] for ref in refs[:18]]
        refs[18][...] = _s5_group_fn(nck, *_s5_unpack(vals))

    specs = _s5_specs(r)
    return pl.pallas_call(
        body, name=name, grid=(ng,), in_specs=specs, out_specs=specs[0], out_shape=SDS(x.shape, F32),
        compiler_params=_cp(("arbitrary",), VMEM_LIMIT),
    )(*args)


def _s5_bwd(args, dy, nck, name):
    x = args[0]
    ng, r, w = x.shape

    def body(*refs):
        vals = [ref[...] for ref in refs[:18]]
        _, vjp = jax.vjp(lambda *v: _s5_group_fn(nck, *_s5_unpack(v)), *vals)
        grads = vjp(refs[18][...])
        for o_ref, gval in zip(refs[19:], grads):
            o_ref[...] = gval.astype(o_ref.dtype)

    specs = _s5_specs(r)
    return pl.pallas_call(
        body, name=name, grid=(ng,), in_specs=specs + [specs[0]], out_specs=specs,
        out_shape=[SDS(x.shape, BF16)] + [SDS(a.shape, F32) for a in args[1:]],
        compiler_params=_cp(("arbitrary",), VMEM_LIMIT),
    )(*args, dy)


def _mix_fn(yf, yb, xs, z, s5o, dvec, nw_ssd, nw_s5):
    hr = lax.broadcasted_iota(jnp.int32, (LANES, SSD_HEADS * SSD_HEAD_DIM), 0)
    hc = lax.broadcasted_iota(jnp.int32, (LANES, SSD_HEADS * SSD_HEAD_DIM), 1)
    expand = (hc // SSD_HEAD_DIM == hr).astype(F32)
    dch = jnp.sum(dvec * expand, axis=0, keepdims=True)
    y = (yf + yb + dch * xs) * (z * _sigmoid(z))
    return _rms(y, nw_ssd), _rms(s5o, nw_s5)


def _mix(y2, xs, z, s5o, dvec, nw_ssd, nw_s5, tm, name):
    n, c1 = xs.shape
    c2 = s5o.shape[1]

    def body(yf_ref, yb_ref, xs_ref, z_ref, s_ref, d_ref, n1_ref, n2_ref, o_ref):
        o1, o2 = _mix_fn(yf_ref[...], yb_ref[...], xs_ref[...], z_ref[...], s_ref[...], d_ref[...], n1_ref[...], n2_ref[...])
        o_ref[:, :c1] = o1.astype(BF16)
        o_ref[:, c1:] = o2.astype(BF16)

    row = lambda i: (i, 0)
    fix = lambda i: (0, 0)
    return pl.pallas_call(
        body, name=name, grid=(n // tm,),
        in_specs=[pl.BlockSpec((None, tm, c1), lambda i: (0, i, 0)), pl.BlockSpec((None, tm, c1), lambda i: (1, i, 0)),
                  pl.BlockSpec((tm, c1), row), pl.BlockSpec((tm, c1), row), pl.BlockSpec((tm, c2), row),
                  pl.BlockSpec((LANES, 1), fix), pl.BlockSpec((1, c1), fix), pl.BlockSpec((1, c2), fix)],
        out_specs=pl.BlockSpec((tm, c1 + c2), row), out_shape=SDS((n, c1 + c2), BF16),
        compiler_params=_cp(("arbitrary",), VMEM_LIMIT),
    )(y2, y2, xs, z, s5o, dvec, nw_ssd, nw_s5)


def _acc_rows(refs, vals, first):
    @pl.when(first)
    def _():
        for ref, v in zip(refs, vals):
            ref[...] = v

    @pl.when(jnp.logical_not(first))
    def _():
        for ref, v in zip(refs, vals):
            ref[...] += v


def _mix_bwd(y2, xs, z, s5o, dvec, nw_ssd, nw_s5, dmix, tm, name):
    n, c1 = xs.shape
    c2 = s5o.shape[1]

    def body(yf_ref, yb_ref, xs_ref, z_ref, s_ref, d_ref, n1_ref, n2_ref, dm_ref,
             dy_ref, dxs_ref, dz_ref, ds_ref, dd_ref, dn1_ref, dn2_ref):
        _, vjp = jax.vjp(_mix_fn, yf_ref[...], yb_ref[...], xs_ref[...], z_ref[...], s_ref[...], d_ref[...], n1_ref[...], n2_ref[...])
        dyf, _, dxs, dz, ds, dd, dn1, dn2 = vjp((dm_ref[:, :c1], dm_ref[:, c1:]))
        dy_ref[...] = dyf
        dxs_ref[...] = dxs
        dz_ref[...] = dz.astype(BF16)
        ds_ref[...] = ds
        _acc_rows((dd_ref, dn1_ref, dn2_ref), (dd, dn1, dn2), pl.program_id(0) == 0)

    row = lambda i: (i, 0)
    fix = lambda i: (0, 0)
    return pl.pallas_call(
        body, name=name, grid=(n // tm,),
        in_specs=[pl.BlockSpec((None, tm, c1), lambda i: (0, i, 0)), pl.BlockSpec((None, tm, c1), lambda i: (1, i, 0)),
                  pl.BlockSpec((tm, c1), row), pl.BlockSpec((tm, c1), row), pl.BlockSpec((tm, c2), row),
                  pl.BlockSpec((LANES, 1), fix), pl.BlockSpec((1, c1), fix), pl.BlockSpec((1, c2), fix),
                  pl.BlockSpec((tm, c1 + c2), row)],
        out_specs=[pl.BlockSpec((tm, c1), row), pl.BlockSpec((tm, c1), row), pl.BlockSpec((tm, c1), row), pl.BlockSpec((tm, c2), row),
                   pl.BlockSpec((LANES, 1), fix), pl.BlockSpec((1, c1), fix), pl.BlockSpec((1, c2), fix)],
        out_shape=[SDS((n, c1), F32), SDS((n, c1), F32), SDS((n, c1), BF16), SDS((n, c2), F32),
                   SDS((LANES, 1), F32), SDS((1, c1), F32), SDS((1, c2), F32)],
        compiler_params=_cp(("arbitrary",), VMEM_LIMIT),
    )(y2, y2, xs, z, s5o, dvec, nw_ssd, nw_s5, dmix)


def _final_loss(h2, nw, tgt, tm, name):
    n, d = h2.shape

    def loss_fn(h, w, t):
        e = _rms(h, w) - t
        return (0.5 / d) * jnp.sum(e * e)

    def body(h_ref, w_ref, t_ref, l_ref, dh_ref, dw_ref):
        loss, (dh, dw) = jax.value_and_grad(loss_fn, argnums=(0, 1))(h_ref[...], w_ref[...], t_ref[...])
        dh_ref[...] = dh
        _acc_rows((l_ref, dw_ref), (jnp.full((1, LANES), loss, F32), dw), pl.program_id(0) == 0)

    row = lambda i: (i, 0)
    fix = lambda i: (0, 0)
    return pl.pallas_call(
        body, name=name, grid=(n // tm,),
        in_specs=[pl.BlockSpec((tm, d), row), pl.BlockSpec((1, d), fix), pl.BlockSpec((tm, d), row)],
        out_specs=[pl.BlockSpec((1, LANES), fix), pl.BlockSpec((tm, d), row), pl.BlockSpec((1, d), fix)],
        out_shape=[SDS((1, LANES), F32), SDS((n, d), F32), SDS((1, d), F32)],
        compiler_params=_cp(("arbitrary",), VMEM_LIMIT),
    )(h2, nw, tgt)


def _norm_bwd(x, nw, dhn, dres, tm, name):
    n, d = x.shape

    def body(x_ref, w_ref, g_ref, r_ref, dx_ref, dw_ref):
        _, vjp = jax.vjp(_rms, x_ref[...], w_ref[...])
        dx, dw = vjp(g_ref[...])
        dx_ref[...] = r_ref[...] + dx
        _acc_rows((dw_ref,), (dw,), pl.program_id(0) == 0)

    row = lambda i: (i, 0)
    fix = lambda i: (0, 0)
    return pl.pallas_call(
        body, name=name, grid=(n // tm,),
        in_specs=[pl.BlockSpec((tm, d), row), pl.BlockSpec((1, d), fix), pl.BlockSpec((tm, d), row), pl.BlockSpec((tm, d), row)],
        out_specs=[pl.BlockSpec((tm, d), row), pl.BlockSpec((1, d), fix)],
        out_shape=[SDS((n, d), F32), SDS((1, d), F32)],
        compiler_params=_cp(("arbitrary",), VMEM_LIMIT),
    )(x, nw, dhn, dres)


def _row_tile(n, cap=512):
    for t in range(min(cap, n) // 8 * 8, 7, -8):
        if n % t == 0:
            return t
    return n


def _sum_lead(a, name):
    kk, n, c = a.shape
    tm = _row_tile(n)

    def body(a_ref, o_ref):
        acc = a_ref[0].astype(F32)
        for i in range(1, kk):
            acc = acc + a_ref[i].astype(F32)
        o_ref[...] = acc

    return pl.pallas_call(
        body, name=name, grid=(n // tm,),
        in_specs=[pl.BlockSpec((kk, tm, c), lambda i: (0, i, 0))],
        out_specs=pl.BlockSpec((tm, c), lambda i: (i, 0)), out_shape=SDS((n, c), F32),
        compiler_params=_cp(("arbitrary",), VMEM_LIMIT),
    )(a)


def _adamw(w, g, m, v, name):
    n, c = w.shape
    tm = _row_tile(n)

    def body(w_ref, g_ref, m_ref, v_ref, d_ref, nm_ref, nv_ref):
        gv = g_ref[...]
        mn = ADAM_B1 * m_ref[...] + (1.0 - ADAM_B1) * gv
        vn = ADAM_B2 * v_ref[...] + (1.0 - ADAM_B2) * jnp.square(gv)
        m_hat = mn / (1.0 - ADAM_B1 ** ADAM_STEP)
        v_hat = vn / (1.0 - ADAM_B2 ** ADAM_STEP)
        d_ref[...] = -ADAM_LR * (m_hat / (jnp.sqrt(v_hat) + ADAM_EPS) + ADAM_WD * w_ref[...])
        nm_ref[...] = mn
        nv_ref[...] = vn

    spec = pl.BlockSpec((tm, c), lambda i: (i, 0))
    return pl.pallas_call(
        body, name=name, grid=(n // tm,), in_specs=[spec] * 4, out_specs=[spec] * 3,
        out_shape=[SDS((n, c), F32)] * 3, compiler_params=_cp(("arbitrary",), VMEM_LIMIT),
    )(w, g, m, v)


ANY = pl.BlockSpec(memory_space=pl.ANY)


def _me():
    return lax.axis_index("x"), lax.axis_index("y"), lax.axis_index("c")


def _gather_xy(split, whole, name):
    ns, cnt = len(split), len(split) + len(whole)

    def body(*refs):
        src, dst = refs[:cnt], refs[cnt:2 * cnt]
        send, recv = refs[2 * cnt:]
        x, y, c = _me()
        mine = 2 * x + y
        chips = [(1 - x, y), (x, 1 - y), (1 - x, 1 - y)]

        def ici(a, j, slot):
            px, py = chips[j]
            if a < ns:
                s_ref, d_ref = src[a].at[c], dst[a].at[slot].at[c]
            else:
                s_ref, d_ref = src[a], dst[a].at[slot]
            return pltpu.make_async_remote_copy(s_ref, d_ref, send.at[3 * a + j], recv.at[3 * a + j],
                                                device_id=(px, py, c), device_id_type=MESH)

        def d2d(a, j, half):
            px, py = chips[j]
            ref = dst[a].at[2 * px + py].at[half]
            return pltpu.make_async_remote_copy(ref, ref, send.at[3 * cnt + 3 * a + j], recv.at[3 * cnt + 3 * a + j],
                                                device_id=(x, y, 1 - c), device_id_type=MESH)

        def own(a):
            return pltpu.make_async_remote_copy(src[a], dst[a].at[mine], send.at[nsem - cnt + a], recv.at[nsem - cnt + a],
                                                device_id=(x, y, 1 - c), device_id_type=MESH)

        started = []
        for a in range(cnt):
            cp = own(a)
            cp.start()
            started.append(cp)
            for j in range(3):
                cp = ici(a, j, mine)
                cp.start()
                started.append(cp)
        for a in range(cnt):
            for j, (px, py) in enumerate(chips):
                ici(a, j, 2 * px + py).wait_recv()
                if a < ns:
                    cp = d2d(a, j, c)
                    cp.start()
                    started.append(cp)
        for a in range(ns):
            for j in range(3):
                d2d(a, j, 1 - c).wait_recv()
        for a in range(cnt):
            own(a).wait_recv()
        for cp in started:
            cp.wait_send()

    nsem = 3 * cnt + 3 * ns + cnt
    return pl.pallas_call(
        body, name=name, in_specs=[ANY] * cnt, out_specs=[ANY] * cnt,
        out_shape=[SDS((4,) + s.shape, s.dtype) for s in split + whole],
        scratch_shapes=[pltpu.SemaphoreType.DMA((nsem,)), pltpu.SemaphoreType.DMA((nsem,))],
    )(*split, *whole)


HBM = pl.BlockSpec(memory_space=pltpu.HBM)
SEM = pl.BlockSpec(memory_space=pltpu.SEMAPHORE)
DATAFLOW = pltpu.SideEffectType.DATAFLOW_SIDE_EFFECTING


def _whole_copies(srcs, dsts, sends, recvs):
    x, y, c = _me()
    peers = [(1 - x, y, c), (x, 1 - y, c), (1 - x, 1 - y, c), (x, y, 1 - c)]
    return [pltpu.make_async_remote_copy(srcs[a], dsts[a].at[2 * x + y], sends[4 * a + j], recvs[4 * a + j],
                                         device_id=peer, device_id_type=MESH)
            for a in range(len(srcs)) for j, peer in enumerate(peers)]


def _scatter_copies(srcs, dsts, sends, recvs):
    x, y, c = _me()
    chips = [(1 - x, y), (x, 1 - y), (1 - x, 1 - y)]
    return [pltpu.make_async_remote_copy(srcs[a].at[2 * px + py], dsts[a].at[2 * x + y], sends[3 * a + j], recvs[3 * a + j],
                                         device_id=(px, py, c), device_id_type=MESH)
            for a in range(len(srcs)) for j, (px, py) in enumerate(chips)]


def _copies_start(copies, per, shards, after, name):
    cnt = len(shards)
    ncp = per * cnt

    def body(*refs):
        srcs, lands = refs[:cnt], refs[cnt:2 * cnt]
        outs = refs[2 * cnt + 1:]
        for cp in copies(srcs, lands, outs[:ncp], outs[ncp:2 * ncp]):
            cp.start()
        outs[-1][...] = jnp.zeros_like(outs[-1])

    lands = [lax.empty((4,) + (s.shape if per == 4 else s.shape[1:]), s.dtype) for s in shards]
    ops = [pltpu.with_memory_space_constraint(a, pltpu.HBM) for a in list(shards) + lands]
    res = pl.pallas_call(
        body, name=name, in_specs=[HBM] * (2 * cnt) + [ANY],
        out_shape=tuple([pltpu.SemaphoreType.DMA(())] * (2 * ncp) + [pltpu.HBM(a.shape, a.dtype) for a in ops] + [SDS((8, LANES), F32)]),
        out_specs=tuple([SEM] * (2 * ncp) + [HBM] * (2 * cnt) + [pl.BlockSpec(memory_space=pltpu.VMEM)]),
        input_output_aliases={i: 2 * ncp + i for i in range(2 * cnt)},
        compiler_params=pltpu.CompilerParams(has_side_effects=DATAFLOW),
    )(*ops, after)
    return res[:2 * ncp], res[2 * ncp:2 * ncp + cnt], res[2 * ncp + cnt:2 * ncp + 2 * cnt], res[-1]


def _copies_wait(copies, per, handle, after, name):
    sems, srcs, lands, _ = handle
    cnt = len(srcs)
    ncp = per * cnt

    def body(*refs):
        sem_refs = refs[2 * cnt:2 * cnt + 2 * ncp]
        for cp in copies(refs[:cnt], refs[cnt:2 * cnt], sem_refs[:ncp], sem_refs[ncp:]):
            cp.wait_send()
            cp.wait_recv()

    res = pl.pallas_call(
        body, name=name, in_specs=[HBM] * (2 * cnt) + [SEM] * (2 * ncp) + [ANY],
        out_shape=tuple(pltpu.HBM(a.shape, a.dtype) for a in list(srcs) + list(lands)),
        out_specs=tuple([HBM] * (2 * cnt)), input_output_aliases={i: i for i in range(2 * cnt)},
        compiler_params=pltpu.CompilerParams(has_side_effects=DATAFLOW),
    )(*srcs, *lands, *sems, after)
    return list(res[cnt:])


def _swap_sibling(parts, pick, name):
    cnt = len(parts)

    def body(*refs):
        src, dst = refs[:cnt], refs[cnt:2 * cnt]
        send, recv = refs[2 * cnt:]
        x, y, c = _me()
        cps = []
        for a in range(cnt):
            cp = pltpu.make_async_remote_copy(src[a].at[1 - c] if pick else src[a], dst[a], send.at[a], recv.at[a],
                                              device_id=(x, y, 1 - c), device_id_type=MESH)
            cp.start()
            cps.append(cp)
        for cp in cps:
            cp.wait()

    return pl.pallas_call(
        body, name=name, in_specs=[ANY] * cnt, out_specs=[ANY] * cnt,
        out_shape=[SDS(p.shape[1:] if pick else p.shape, p.dtype) for p in parts],
        scratch_shapes=[pltpu.SemaphoreType.DMA((cnt,)), pltpu.SemaphoreType.DMA((cnt,))],
    )(*parts)


def _scatter_xy(parts, name):
    cnt = len(parts)

    def body(*refs):
        src, dst = refs[:cnt], refs[cnt:2 * cnt]
        send, recv, loc = refs[2 * cnt:]
        x, y, c = _me()
        mine = 2 * x + y
        chips = [(1 - x, y), (x, 1 - y), (1 - x, 1 - y)]
        local = []
        for a in range(cnt):
            cp = pltpu.make_async_copy(src[a].at[mine], dst[a].at[mine], loc.at[a])
            cp.start()
            local.append(cp)
        sends = []
        for a in range(cnt):
            for j, (px, py) in enumerate(chips):
                cp = pltpu.make_async_remote_copy(src[a].at[2 * px + py], dst[a].at[mine], send.at[3 * a + j], recv.at[3 * a + j],
                                                  device_id=(px, py, c), device_id_type=MESH)
                cp.start()
                sends.append(cp)
        for a in range(cnt):
            for j, (px, py) in enumerate(chips):
                pltpu.make_async_remote_copy(src[a].at[mine], dst[a].at[2 * px + py], send.at[3 * a + j], recv.at[3 * a + j],
                                             device_id=(px, py, c), device_id_type=MESH).wait_recv()
        for cp in sends:
            cp.wait_send()
        for cp in local:
            cp.wait()

    return pl.pallas_call(
        body, name=name, in_specs=[ANY] * cnt, out_specs=[ANY] * cnt,
        out_shape=[SDS(p.shape, p.dtype) for p in parts],
        scratch_shapes=[pltpu.SemaphoreType.DMA((3 * cnt,)), pltpu.SemaphoreType.DMA((3 * cnt,)), pltpu.SemaphoreType.DMA((cnt,))],
    )(*parts)


def _bcast_all(buf, name):
    def body(src, dst, send, recv, loc):
        x, y, c = _me()
        mine = 4 * x + 2 * y + c
        own = pltpu.make_async_copy(src, dst.at[mine], loc)
        own.start()
        sends = []
        for k in range(1, 8):
            px, py, pc = x ^ (k >> 2), y ^ ((k >> 1) & 1), c ^ (k & 1)
            cp = pltpu.make_async_remote_copy(src, dst.at[mine], send.at[k - 1], recv.at[k - 1],
                                              device_id=(px, py, pc), device_id_type=MESH)
            cp.start()
            sends.append(cp)
        for k in range(1, 8):
            px, py, pc = x ^ (k >> 2), y ^ ((k >> 1) & 1), c ^ (k & 1)
            pltpu.make_async_remote_copy(src, dst.at[4 * px + 2 * py + pc], send.at[k - 1], recv.at[k - 1],
                                         device_id=(px, py, pc), device_id_type=MESH).wait_recv()
        for cp in sends:
            cp.wait_send()
        own.wait()

    return pl.pallas_call(
        body, name=name, in_specs=[ANY], out_specs=ANY, out_shape=SDS((8,) + buf.shape, buf.dtype),
        scratch_shapes=[pltpu.SemaphoreType.DMA((7,)), pltpu.SemaphoreType.DMA((7,)), pltpu.SemaphoreType.DMA(())],
    )(buf)


def _sum_slots(recv, own, chip, name):
    kk, n, c = recv.shape
    tm = _row_tile(n)

    def body(chip_ref, r_ref, o_ref, out_ref):
        acc = None
        for j in range(kk):
            v = jnp.where(chip_ref[0] == j, o_ref[...], r_ref[j]).astype(F32)
            acc = v if acc is None else acc + v
        out_ref[...] = acc

    grid_spec = pltpu.PrefetchScalarGridSpec(
        num_scalar_prefetch=1, grid=(n // tm,),
        in_specs=[pl.BlockSpec((kk, tm, c), lambda i, chip_ref: (0, i, 0)),
                  pl.BlockSpec((None, tm, c), lambda i, chip_ref: (chip_ref[0], i, 0))],
        out_specs=pl.BlockSpec((tm, c), lambda i, chip_ref: (i, 0)))
    return pl.pallas_call(body, name=name, grid_spec=grid_spec, out_shape=SDS((n, c), F32),
                          compiler_params=_cp(("arbitrary",), VMEM_LIMIT))(chip.reshape(1), recv, own)


def _add_half(parts, got, core, dtype, name):
    shp = got.shape
    a2, b2 = parts.reshape(2, -1, shp[-1]), got.reshape(-1, shp[-1])
    n, c = b2.shape
    tm = _row_tile(n, 256)

    def body(core_ref, a_ref, b_ref, o_ref):
        o_ref[...] = (a_ref[...] + b_ref[...]).astype(dtype)

    spec = pl.BlockSpec((tm, c), lambda i, core_ref: (i, 0))
    grid_spec = pltpu.PrefetchScalarGridSpec(
        num_scalar_prefetch=1, grid=(n // tm,),
        in_specs=[pl.BlockSpec((None, tm, c), lambda i, core_ref: (core_ref[0], i, 0)), spec], out_specs=spec)
    return pl.pallas_call(body, name=name, grid_spec=grid_spec, out_shape=SDS((n, c), dtype),
                          compiler_params=_cp(("arbitrary",), VMEM_LIMIT))(core.reshape(1), a2, b2).reshape(shp)


def _x_layout(u, name):
    n, c = u.shape
    t, ch = S5_T, S5_CH
    gb = LANES // ch
    rows = min(64, n // t)

    def body(u_ref, o_ref):
        for s in range(t):
            us = u_ref[pl.ds(s, rows, stride=t), :]
            for g in range(gb):
                o_ref[g, :, ch * s:ch * (s + 1)] = us[:, ch * g:ch * (g + 1)]

    return pl.pallas_call(
        body, name=name, grid=(n // (rows * t), c // LANES),
        in_specs=[pl.BlockSpec((rows * t, LANES), lambda i, j: (i, j))],
        out_specs=pl.BlockSpec((gb, rows, t * ch), lambda i, j: (j, i, 0)),
        out_shape=SDS((c // ch, n // t, t * ch), F32),
        compiler_params=_cp(("arbitrary", "arbitrary"), VMEM_LIMIT),
    )(u)


def _token_layout(xg, name):
    ng, r, w = xg.shape
    t, ch = S5_T, S5_CH
    gb = LANES // ch
    rows = min(64, r)

    def body(x_ref, o_ref):
        for s in range(t):
            parts = [x_ref[g, :, ch * s:ch * (s + 1)].astype(F32) for g in range(gb)]
            o_ref[pl.ds(s, rows, stride=t), :] = jnp.concatenate(parts, axis=1)

    return pl.pallas_call(
        body, name=name, grid=(r // rows, ng // gb),
        in_specs=[pl.BlockSpec((gb, rows, w), lambda i, j: (j, i, 0))],
        out_specs=pl.BlockSpec((rows * t, LANES), lambda i, j: (i, j)),
        out_shape=SDS((r * t, ng * ch), F32),
        compiler_params=_cp(("arbitrary", "arbitrary"), VMEM_LIMIT),
    )(xg)


def _pad_lanes(a, lanes=LANES):
    return jnp.pad(a, ((0, 0), (0, lanes - a.shape[1])))


def _local_step(x, tgt, p, bl, late, early):
    p = dict(p)
    n, d = x.shape
    sw = SSD_HEADS * SSD_HEAD_DIM
    gn = SSD_GROUPS * SSD_STATE
    tm = min(n, 512)
    tm_ffn = min(n, 256)
    nck = n // bl // S5_T
    s5w = S5_GROUPS * S5_CH

    w_in = p["w_in"]
    o1, o2, o3, o4 = sw, sw + sw, sw + sw + gn, sw + sw + 2 * gn
    w_z, w_xs, w_b, w_c = w_in[:, :o1], w_in[:, o1:o2], w_in[:, o2:o3], w_in[:, o3:o4]
    w_dt = jnp.concatenate([_pad_lanes(w_in[:, o4:o4 + SSD_HEADS]), _pad_lanes(w_in[:, o4 + SSD_HEADS:o4 + 2 * SSD_HEADS])], 1)
    w_u = w_in[:, o4 + 2 * SSD_HEADS:]
    in_ws = [w_z, w_xs, w_b, w_c, w_dt, w_u]
    cw, cb_ = p["ssd_conv_w"], p["ssd_conv_b"]
    conv_parts = [(cw[:, :sw], cb_[:, :sw]), (cw[:, sw:sw + gn], cb_[:, sw:sw + gn]), (cw[:, sw + gn:], cb_[:, sw + gn:])]
    alog2 = jnp.stack([_pad_lanes(p["ssd_a_log_fwd"]), _pad_lanes(p["ssd_a_log_bwd"])])
    dtb2 = jnp.stack([_pad_lanes(p["ssd_dt_bias_fwd"]), _pad_lanes(p["ssd_dt_bias_bwd"])])
    dvec = _pad_lanes(p["ssd_d"]).reshape(LANES, 1)

    hn, z, xs_pre, b_pre, c_pre, dtr, u = _norm_matmul(x, p["norm_mix_w"], in_ws, tm, "in_proj")
    pres = [xs_pre, b_pre, c_pre]
    acts = [_conv_silu(pre, w, b, bl, min(256, pre.shape[1]), f"ssd_conv_{i}") for i, (pre, (w, b)) in enumerate(zip(pres, conv_parts))]
    xs_a, b_a, c_a = acts
    y2, saved = _ssd_scan(xs_a, b_a, c_a, dtr, alog2, dtb2, bl, "ssd_scan")

    def col(a):
        return a.reshape(a.shape + (1,))

    s5_params = [
        col(p["s5_lambda_re_fwd"]), col(p["s5_lambda_im_fwd"]), p["s5_log_step_fwd"].reshape(S5_GROUPS, 1, 1), p["s5_c_re_fwd"], p["s5_c_im_fwd"],
        col(p["s5_lambda_re_bwd"]), col(p["s5_lambda_im_bwd"]), p["s5_log_step_bwd"].reshape(S5_GROUPS, 1, 1), p["s5_c_re_bwd"], p["s5_c_im_bwd"],
        p["s5_b_re"], p["s5_b_im"], col(p["s5_d"].reshape(S5_GROUPS, S5_CH)),
        p["s5_glu_w"][:, :, :S5_CH], p["s5_glu_w"][:, :, S5_CH:], col(p["s5_glu_b"][:, :S5_CH]), col(p["s5_glu_b"][:, S5_CH:]),
    ]
    s5_args = [_x_layout(u, "s5_u_blocks")] + s5_params
    s5o = _token_layout(_s5_fwd(s5_args, nck, "s5_fwd"), "s5_y_tokens")
    ymix = _mix(y2, xs_a, z, s5o, dvec, p["ssd_norm_w"], p["s5_norm_w"], tm, "mix")
    p["w_out"], p["w_up"], p["w_down"] = late(ymix)
    dff = p["w_down"].shape[0]
    h1 = _matmul_res(ymix, p["w_out"], x, tm, "out_proj")
    w_up = p["w_up"]
    hn2, up_v, up_g = _norm_matmul(h1, p["norm_ffn_w"], [w_up[:, :dff], w_up[:, dff:]], tm_ffn, "ffn_up")
    fw, fb = p["ffn_conv_w"], p["ffn_conv_b"]
    act = _conv_glu(up_v, up_g, fw[:, :dff], fw[:, dff:], fb[:, :dff], fb[:, dff:], bl, 256, "ffn_conv")
    h2 = _matmul_res(act, p["w_down"], h1, tm, "ffn_down")
    loss, dh2, g_nfw = _final_loss(h2, p["norm_final_w"].reshape(1, d), tgt, tm, "final_loss")

    g = {"norm_final_w": g_nfw.reshape(d)}
    g["w_down"] = _matmul_tn(act, dh2, tm, d, "ffn_down_dw")
    dact = _matmul_nt([dh2], [p["w_down"]], tm, "ffn_down_dx")
    dup_v, dup_g, dwv, dwg, dbv, dbg = _conv_glu_bwd(up_v, up_g, fw[:, :dff], fw[:, dff:], fb[:, :dff], fb[:, dff:], dact, bl, 256, "ffn_conv_bwd")
    g["ffn_conv_w"] = jnp.concatenate([dwv, dwg], 1)
    g["ffn_conv_b"] = jnp.concatenate([dbv, dbg], 1)
    g["w_up"] = jnp.concatenate([_matmul_tn(hn2, dup_v, tm, dff // 2, "ffn_up_dw_v"), _matmul_tn(hn2, dup_g, tm, dff // 2, "ffn_up_dw_g")], 1)
    dhn2 = _matmul_nt([dup_v, dup_g], [w_up[:, :dff], w_up[:, dff:]], tm_ffn, "ffn_up_dx")
    g["ffn_reduce"] = early(g["w_up"], g["w_down"])
    behind = g["ffn_reduce"][1][3][0, 0]
    dh1, g["norm_ffn_w"] = _norm_bwd(h1, p["norm_ffn_w"] + behind, dhn2, dh2, tm, "ffn_norm_bwd")
    g["w_out"] = _matmul_tn(ymix, dh1, tm, d, "out_proj_dw")
    dmix = _matmul_nt([dh1], [p["w_out"]], tm, "out_proj_dx")
    dyssd, dxs_gate, dz, ds5o, g_d, g["ssd_norm_w"], g["s5_norm_w"] = _mix_bwd(
        y2, xs_a, z, s5o, dvec, p["ssd_norm_w"], p["s5_norm_w"], dmix, tm, "mix_bwd")
    g["ssd_d"] = g_d[:SSD_HEADS].reshape(1, SSD_HEADS)
    s5g = _s5_bwd(s5_args, _x_layout(ds5o, "s5_dy_blocks"), nck, "s5_bwd")
    du = _token_layout(s5g[0], "s5_du_tokens")
    (g["s5_lambda_re_fwd"], g["s5_lambda_im_fwd"], g["s5_log_step_fwd"], g["s5_c_re_fwd"], g["s5_c_im_fwd"],
     g["s5_lambda_re_bwd"], g["s5_lambda_im_bwd"], g["s5_log_step_bwd"], g["s5_c_re_bwd"], g["s5_c_im_bwd"],
     g["s5_b_re"], g["s5_b_im"], g_s5d, g_wv, g_wg, g_bv, g_bg) = s5g[1:]
    for k_ in ("s5_lambda_re_fwd", "s5_lambda_im_fwd", "s5_lambda_re_bwd", "s5_lambda_im_bwd"):
        g[k_] = g[k_].reshape(S5_GROUPS, S5_STATE)
    for k_ in ("s5_log_step_fwd", "s5_log_step_bwd"):
        g[k_] = g[k_].reshape(S5_GROUPS)
    g["s5_d"] = g_s5d.reshape(1, s5w)
    g["s5_glu_w"] = jnp.concatenate([g_wv, g_wg], 2)
    g["s5_glu_b"] = jnp.concatenate([g_bv.reshape(S5_GROUPS, S5_CH), g_bg.reshape(S5_GROUPS, S5_CH)], 1)
    dxs2, dbm2, dcm2, ddtr, dal2, ddb2 = _ssd_scan_bwd(xs_a, b_a, c_a, dtr, alog2, dtb2, saved, dyssd, bl, "ssd_scan_bwd")
    g["ssd_a_log_fwd"], g["ssd_a_log_bwd"] = dal2[0, :, :SSD_HEADS], dal2[1, :, :SSD_HEADS]
    g["ssd_dt_bias_fwd"], g["ssd_dt_bias_bwd"] = ddb2[0, :, :SSD_HEADS], ddb2[1, :, :SSD_HEADS]
    cots = [[(dxs2, 0), (dxs2, 1), (dxs_gate, None)], [(dbm2, 0), (dbm2, 1)], [(dcm2, 0), (dcm2, 1)]]
    dpres, dcw, dcb = [], [], []
    for i, (pre, (w, b), cot) in enumerate(zip(pres, conv_parts, cots)):
        dp, dw_, db_ = _conv_silu_bwd(pre, w, b, cot, bl, min(256, pre.shape[1]), f"ssd_conv_bwd_{i}")
        dpres.append(dp)
        dcw.append(dw_)
        dcb.append(db_)
    g["ssd_conv_w"] = jnp.concatenate(dcw, 1)
    g["ssd_conv_b"] = jnp.concatenate(dcb, 1)
    dprojs = [dz, dpres[0], dpres[1], dpres[2], ddtr, du]
    dws = [_matmul_tn(hn, dpj, tm, dpj.shape[1], f"in_proj_dw_{i}") for i, dpj in enumerate(dprojs)]
    dws[4] = jnp.concatenate([dws[4][:, :SSD_HEADS], dws[4][:, LANES:LANES + SSD_HEADS]], 1)
    g["w_in"] = jnp.concatenate(dws, 1)
    dhn = _matmul_nt(dprojs, in_ws, tm, "in_proj_dx")
    grad_x, g["norm_mix_w"] = _norm_bwd(x, p["norm_mix_w"], dhn, dh1, tm, "mix_norm_bwd")
    return loss, grad_x, g


_WEIGHTS = ['norm_mix_w', 'w_in', 'ssd_conv_w', 'ssd_conv_b', 'ssd_dt_bias_fwd', 'ssd_dt_bias_bwd', 'ssd_a_log_fwd', 'ssd_a_log_bwd',
            'ssd_d', 'ssd_norm_w', 's5_lambda_re_fwd', 's5_lambda_im_fwd', 's5_log_step_fwd', 's5_lambda_re_bwd', 's5_lambda_im_bwd',
            's5_log_step_bwd', 's5_b_re', 's5_b_im', 's5_c_re_fwd', 's5_c_im_fwd', 's5_c_re_bwd', 's5_c_im_bwd', 's5_d', 's5_glu_w',
            's5_glu_b', 's5_norm_w', 'w_out', 'norm_ffn_w', 'ffn_w_up', 'ffn_conv_w', 'ffn_conv_b', 'ffn_w_down', 'norm_final_w']
_BIG = ('w_in', 'w_out', 'ffn_w_up', 'ffn_w_down')
_CONV = ('ssd_conv_w', 'ffn_conv_w')


def _pack(arrs):
    flat = jnp.concatenate([a.reshape(-1) for a in arrs])
    rows = -(-flat.shape[0] // (64 * LANES)) * 64
    return jnp.pad(flat, (0, rows * LANES - flat.shape[0])).reshape(rows, LANES)


def _unpack(buf, shapes):
    flat = buf.reshape(-1)
    out, off = [], 0
    for shp in shapes:
        size = math.prod(shp)
        out.append(flat[off:off + size].reshape(shp))
        off += size
    return out


def kernel(x, norm_mix_w, w_in, ssd_conv_w, ssd_conv_b, ssd_dt_bias_fwd, ssd_dt_bias_bwd, ssd_a_log_fwd, ssd_a_log_bwd, ssd_d, ssd_norm_w, s5_lambda_re_fwd, s5_lambda_im_fwd, s5_log_step_fwd, s5_lambda_re_bwd, s5_lambda_im_bwd, s5_log_step_bwd, s5_b_re, s5_b_im, s5_c_re_fwd, s5_c_im_fwd, s5_c_re_bwd, s5_c_im_bwd, s5_d, s5_glu_w, s5_glu_b, s5_norm_w, w_out, norm_ffn_w, ffn_w_up, ffn_conv_w, ffn_conv_b, ffn_w_down, norm_final_w, loss_target, m_norm_mix_w, m_w_in, m_ssd_conv_w, m_ssd_conv_b, m_ssd_dt_bias_fwd, m_ssd_dt_bias_bwd, m_ssd_a_log_fwd, m_ssd_a_log_bwd, m_ssd_d, m_ssd_norm_w, m_s5_lambda_re_fwd, m_s5_lambda_im_fwd, m_s5_log_step_fwd, m_s5_lambda_re_bwd, m_s5_lambda_im_bwd, m_s5_log_step_bwd, m_s5_b_re, m_s5_b_im, m_s5_c_re_fwd, m_s5_c_im_fwd, m_s5_c_re_bwd, m_s5_c_im_bwd, m_s5_d, m_s5_glu_w, m_s5_glu_b, m_s5_norm_w, m_w_out, m_norm_ffn_w, m_ffn_w_up, m_ffn_conv_w, m_ffn_conv_b, m_ffn_w_down, m_norm_final_w, v_norm_mix_w, v_w_in, v_ssd_conv_w, v_ssd_conv_b, v_ssd_dt_bias_fwd, v_ssd_dt_bias_bwd, v_ssd_a_log_fwd, v_ssd_a_log_bwd, v_ssd_d, v_ssd_norm_w, v_s5_lambda_re_fwd, v_s5_lambda_im_fwd, v_s5_log_step_fwd, v_s5_lambda_re_bwd, v_s5_lambda_im_bwd, v_s5_log_step_bwd, v_s5_b_re, v_s5_b_im, v_s5_c_re_fwd, v_s5_c_im_fwd, v_s5_c_re_bwd, v_s5_c_im_bwd, v_s5_d, v_s5_glu_w, v_s5_glu_b, v_s5_norm_w, v_w_out, v_norm_ffn_w, v_ffn_w_up, v_ffn_conv_w, v_ffn_conv_b, v_ffn_w_down, v_norm_final_w):
    args = dict(locals())
    w = {k_: args[k_] for k_ in _WEIGHTS}
    m = {k_: args["m_" + k_] for k_ in _WEIGHTS}
    v = {k_: args["v_" + k_] for k_ in _WEIGHTS}
    bl, sl, d = x.shape
    chip = 2 * lax.axis_index("x") + lax.axis_index("y")
    core = lax.axis_index("c")

    first = w["w_in"][0].astype(BF16)
    g_in, g_scw, g_fcw = _gather_xy([first.reshape(2, first.shape[0] // 2, first.shape[1])], [w[k_][0] for k_ in _CONV], "gather_first")
    g_in = g_in.reshape((4,) + first.shape)
    handle = _copies_start(_whole_copies, 4, [w[k_][0].astype(BF16) for k_ in _BIG[1:]], g_scw, "gather_rest_start")

    def cols(a):
        return jnp.moveaxis(a, 0, 1).reshape(a.shape[1], 4 * a.shape[2])

    p = {k_: (w[k_][0] if w[k_].ndim >= 3 else w[k_]) for k_ in _WEIGHTS if k_ not in _BIG + _CONV}
    p["norm_mix_w"] = p["norm_mix_w"] + handle[3][0, 0]
    p["w_in"] = cols(g_in)
    p["ssd_conv_w"], p["ffn_conv_w"] = cols(g_scw), cols(g_fcw)

    def late(after):
        g_out, g_up, g_down = _copies_wait(_whole_copies, 4, handle, after, "gather_rest_wait")
        return g_out.reshape(-1, g_out.shape[2]), cols(g_up), g_down.reshape(-1, g_down.shape[2])

    def owner_major(a, k_):
        r, c = w[k_].shape[1:]
        if a.shape[0] == r:
            a = jnp.moveaxis(a.reshape(r, 4, c), 1, 0)
        else:
            a = a.reshape(4, r, c)
        return a.reshape(4, 2, r // 2, c)

    def early(g_up, g_down):
        parts = [jnp.moveaxis(owner_major(a, k_), 1, 0) for a, k_ in ((g_up, "ffn_w_up"), (g_down, "ffn_w_down"))]
        got = _swap_sibling(parts, True, "reduce_sibling_ffn")
        sums = [_add_half(pt, gt, core, BF16, f"reduce_add_ffn_{i}") for i, (pt, gt) in enumerate(zip(parts, got))]
        return sums, _copies_start(_scatter_copies, 3, sums, w["norm_ffn_w"], "reduce_chips_ffn_start")

    loss, grad_x, g = _local_step(x.reshape(bl * sl, d), loss_target.reshape(bl * sl, d), p, bl, late, early)
    ffn_sums, ffn_handle = g.pop("ffn_reduce")
    ffn_recv = _copies_wait(_scatter_copies, 3, ffn_handle, grad_x, "reduce_chips_ffn_wait")
    ffn_halves = [_sum_slots(rc, sm, chip, f"reduce_sum_ffn_{i}") for i, (rc, sm) in enumerate(zip(ffn_recv, ffn_sums))]

    small = [k_ for k_ in _WEIGHTS if k_ not in _BIG]
    small_full_shapes = [g[k_].shape for k_ in small]
    buf = _pack([g[k_] for k_ in small] + [loss[0, :1]])
    rest = _BIG[:2]
    parts = [jnp.moveaxis(owner_major(g[k_], k_), 1, 0) for k_ in rest]
    parts.append(jnp.moveaxis(buf.reshape(4, 2, -1, LANES), 1, 0))
    got = _swap_sibling(parts, True, "reduce_sibling")
    chip_sums = [_add_half(pt, gt, core, BF16 if i < len(rest) else F32, f"reduce_add_{i}") for i, (pt, gt) in enumerate(zip(parts, got))]
    from_chips = _scatter_xy(chip_sums, "reduce_chips")
    rest_halves = [_sum_lead(a.reshape(4, -1, a.shape[-1]), f"reduce_sum_{i}") for i, a in enumerate(from_chips)]
    halves = rest_halves[:-1] + ffn_halves + rest_halves[-1:]
    other = _swap_sibling(halves[:-1], False, "reduce_join")
    big_grad = {}
    for k_, own_half, sib_half in zip(_BIG, halves, other):
        south = core == 0
        full = jnp.stack([jnp.where(south, own_half, sib_half), jnp.where(south, sib_half, own_half)])
        big_grad[k_] = full.reshape((1,) + w[k_].shape[1:])
    tot = _bcast_all(halves[-1], "reduce_small").reshape(buf.shape)
    unp = _unpack(tot, small_full_shapes + [(1,)])
    small_grad = dict(zip(small, unp[:-1]))
    loss_out = unp[-1].reshape(())
    for k_ in _CONV:
        cshard = w[k_].shape[2]
        small_grad[k_] = lax.dynamic_slice_in_dim(small_grad[k_], chip * cshard, cshard, 1)

    grads, deltas, new_m, new_v = {}, {}, {}, {}
    for k_ in _BIG:
        shp = w[k_].shape
        grads[k_] = big_grad[k_]
        dl, nm, nv = _adamw(w[k_][0], big_grad[k_][0], m[k_][0], v[k_][0], f"adamw_{k_}")
        deltas[k_], new_m[k_], new_v[k_] = dl.reshape(shp), nm.reshape(shp), nv.reshape(shp)
    sw_ = _pack([w[k_] for k_ in small])
    sg_ = _pack([small_grad[k_] for k_ in small])
    sm_ = _pack([m[k_] for k_ in small])
    sv_ = _pack([v[k_] for k_ in small])
    dl, nm, nv = _adamw(sw_, sg_, sm_, sv_, "adamw_small")
    shapes = [w[k_].shape for k_ in small]
    for k_, a, b, c_ in zip(small, _unpack(dl, shapes), _unpack(nm, shapes), _unpack(nv, shapes)):
        deltas[k_], new_m[k_], new_v[k_] = a, b, c_
        grads[k_] = small_grad[k_].reshape(w[k_].shape)
    return (loss_out, grad_x.reshape(bl, sl, d), *[grads[k_] for k_ in _WEIGHTS], *[deltas[k_] for k_ in _WEIGHTS],
            *[new_m[k_] for k_ in _WEIGHTS], *[new_v[k_] for k_ in _WEIGHTS])
```

```python
import functools
import math

import jax
import jax.numpy as jnp
from jax import lax
from jax.experimental import pallas as pl
from jax.experimental.pallas import tpu as pltpu

F32 = jnp.float32
BF16 = jnp.bfloat16
HI = lax.Precision.HIGHEST
SDS = jax.ShapeDtypeStruct
MESH = pl.DeviceIdType.MESH

NN = (((1,), (0,)), ((), ()))
NT = (((1,), (1,)), ((), ()))
TN = (((0,), (0,)), ((), ()))

EPS = 1e-6
SSD_HEADS = 16
SSD_HEAD_DIM = 64
SSD_GROUPS = 4
SSD_STATE = 128
SSD_CHUNK = 128
S5_GROUPS = 32
S5_CH = 16
S5_STATE = 64
S5_T = 16
LANES = 128
ADAM_LR, ADAM_B1, ADAM_B2, ADAM_EPS, ADAM_WD, ADAM_STEP = 0.001, 0.9, 0.999, 1e-08, 0.01, 10
V7X_VMEM_BYTES = 64 * 1024 * 1024
VMEM_LIMIT = V7X_VMEM_BYTES - 8 * 1024 * 1024


def _cp(sem, vmem=None):
    return pltpu.CompilerParams(dimension_semantics=sem, vmem_limit_bytes=vmem)


def _dot(a, b, dims=NN, precision=None):
    return lax.dot_general(a, b, dims, precision=precision, preferred_element_type=F32)


def _rms(x, w):
    return x * lax.rsqrt(jnp.mean(x * x, axis=-1, keepdims=True) + EPS) * w


def _sigmoid(x):
    return 1.0 / (1.0 + jnp.exp(-x))


def _softplus(x):
    return jnp.maximum(x, 0.0) + jnp.log1p(jnp.exp(-jnp.abs(x)))


@functools.partial(jax.custom_vjp, nondiff_argnums=(1, 2))
def _shift(x, k, seg):
    n = x.shape[0]
    r = lax.broadcasted_iota(jnp.int32, x.shape, 0)
    if seg != n:
        r = r & (seg - 1) if seg & (seg - 1) == 0 else r % seg
    y = pltpu.roll(x, k % n, 0)
    ok = (r >= k) if k > 0 else (r < seg + k)
    return jnp.where(ok, y, 0.0)


def _shift_fwd(x, k, seg):
    return _shift(x, k, seg), None


def _shift_bwd(k, seg, _, g):
    return (_shift(g, -k, seg),)


_shift.defvjp(_shift_fwd, _shift_bwd)


@functools.partial(jax.custom_vjp, nondiff_argnums=(1,))
def _lane_shift(x, k):
    if k == 0:
        return x
    n = x.shape[1]
    lane = lax.broadcasted_iota(jnp.int32, x.shape, 1)
    ok = (lane >= k) if k > 0 else (lane < n + k)
    return jnp.where(ok, pltpu.roll(x, k % n, 1), 0.0)


_lane_shift.defvjp(lambda x, k: (_lane_shift(x, k), None), lambda k, _, g: (_lane_shift(g, -k),))


@jax.custom_vjp
def _swap(z):
    return pltpu.roll(z, LANES // 2, 1)


_swap.defvjp(lambda z: (_swap(z), None), lambda _, g: (_swap(g),))


def _norm_matmul(x, nw, ws, tm, name):
    n, d = x.shape
    k = len(ws)

    def body(x_ref, nw_ref, *refs):
        hn = _rms(x_ref[...], nw_ref[...]).astype(BF16)
        refs[k][...] = hn
        for w_ref, o_ref in zip(refs[:k], refs[k + 1:]):
            o_ref[...] = _dot(hn, w_ref[...])

    row = lambda i: (i, 0)
    fix = lambda i: (0, 0)
    return pl.pallas_call(
        body, name=name, grid=(n // tm,),
        in_specs=[pl.BlockSpec((tm, d), row), pl.BlockSpec((1, d), fix)] + [pl.BlockSpec(w.shape, fix) for w in ws],
        out_specs=[pl.BlockSpec((tm, d), row)] + [pl.BlockSpec((tm, w.shape[1]), row) for w in ws],
        out_shape=[SDS((n, d), BF16)] + [SDS((n, w.shape[1]), F32) for w in ws],
        compiler_params=_cp(("arbitrary",), VMEM_LIMIT),
    )(x, nw, *ws)


def _matmul_res(a, w, res, tm, name):
    n, kd = a.shape
    m = w.shape[1]

    def body(a_ref, w_ref, r_ref, o_ref):
        o_ref[...] = r_ref[...] + _dot(a_ref[...], w_ref[...])

    return pl.pallas_call(
        body, name=name, grid=(n // tm,),
        in_specs=[pl.BlockSpec((tm, kd), lambda i: (i, 0)), pl.BlockSpec((kd, m), lambda i: (0, 0)),
                  pl.BlockSpec((tm, m), lambda i: (i, 0))],
        out_specs=pl.BlockSpec((tm, m), lambda i: (i, 0)),
        out_shape=SDS((n, m), F32),
        compiler_params=_cp(("arbitrary",), VMEM_LIMIT),
    )(a, w, res)


def _matmul_nt(gs, ws, tm, name):
    n = gs[0].shape[0]
    kd = ws[0].shape[0]
    cnt = len(gs)

    def body(*refs):
        acc = None
        for g_ref, w_ref in zip(refs[:cnt], refs[cnt:2 * cnt]):
            t = _dot(g_ref[...].astype(BF16), w_ref[...], NT)
            acc = t if acc is None else acc + t
        refs[2 * cnt][...] = acc

    return pl.pallas_call(
        body, name=name, grid=(n // tm,),
        in_specs=[pl.BlockSpec((tm, g.shape[1]), lambda i: (i, 0)) for g in gs]
        + [pl.BlockSpec(w.shape, lambda i: (0, 0)) for w in ws],
        out_specs=pl.BlockSpec((tm, kd), lambda i: (i, 0)),
        out_shape=SDS((n, kd), F32),
        compiler_params=_cp(("arbitrary",), VMEM_LIMIT),
    )(*gs, *ws)


def _matmul_tn(a, g, tm, cb, name):
    n, kd = a.shape
    m = g.shape[1]

    def body(a_ref, g_ref, o_ref):
        t = _dot(a_ref[...], g_ref[...].astype(BF16), TN)

        @pl.when(pl.program_id(1) == 0)
        def _():
            o_ref[...] = t

        @pl.when(pl.program_id(1) != 0)
        def _():
            o_ref[...] += t

    return pl.pallas_call(
        body, name=name, grid=(m // cb, n // tm),
        in_specs=[pl.BlockSpec((tm, kd), lambda j, i: (i, 0)), pl.BlockSpec((tm, cb), lambda j, i: (i, j))],
        out_specs=pl.BlockSpec((kd, cb), lambda j, i: (0, j)),
        out_shape=SDS((kd, m), F32),
        compiler_params=_cp(("arbitrary", "arbitrary"), VMEM_LIMIT),
    )(a, g)


def _matmul_tn_multi(a, gs, tm, name):
    n, kd = a.shape
    cnt = len(gs)

    def body(a_ref, *refs):
        av = a_ref[...]
        first = pl.program_id(0) == 0
        for g_ref, o_ref in zip(refs[:cnt], refs[cnt:]):
            _acc_rows((o_ref,), (_dot(av, g_ref[...].astype(BF16), TN),), first)

    return pl.pallas_call(
        body, name=name, grid=(n // tm,),
        in_specs=[pl.BlockSpec((tm, kd), lambda i: (i, 0))] + [pl.BlockSpec((tm, g.shape[1]), lambda i: (i, 0)) for g in gs],
        out_specs=[pl.BlockSpec((kd, g.shape[1]), lambda i: (0, 0)) for g in gs],
        out_shape=[SDS((kd, g.shape[1]), F32) for g in gs],
        compiler_params=_cp(("arbitrary",), VMEM_LIMIT),
    )(a, *gs)


def _dwconv(x, w, b):
    kw = w.shape[0]
    acc = b
    for k in range(kw):
        acc = acc + w[k:k + 1, :] * _shift(x, kw // 2 - k, x.shape[0])
    return acc


def _conv_silu_fn(x, w, b):
    y = _dwconv(x, w, b)
    return y * _sigmoid(y)


def _conv_glu_fn(v, g, wv, wg, bv, bg):
    cv = _dwconv(v, wv, bv)
    cg = _dwconv(g, wg, bg)
    return cg * _sigmoid(cg) * cv


def _conv_silu(x, w, b, bl, cb, name):
    n, c = x.shape
    sl = n // bl
    kw = w.shape[0]

    def body(x_ref, w_ref, b_ref, o_ref):
        o_ref[...] = _conv_silu_fn(x_ref[...], w_ref[...], b_ref[...])

    return pl.pallas_call(
        body, name=name, grid=(bl, c // cb),
        in_specs=[pl.BlockSpec((sl, cb), lambda s, j: (s, j)), pl.BlockSpec((kw, cb), lambda s, j: (0, j)),
                  pl.BlockSpec((1, cb), lambda s, j: (0, j))],
        out_specs=pl.BlockSpec((sl, cb), lambda s, j: (s, j)),
        out_shape=SDS((n, c), F32),
        compiler_params=_cp(("arbitrary", "arbitrary"), VMEM_LIMIT),
    )(x, w, b)


def _conv_silu_bwd(x, w, b, dys, bl, cb, name):
    n, c = x.shape
    sl = n // bl
    kw = w.shape[0]
    cnt = len(dys)

    def body(x_ref, w_ref, b_ref, *refs):
        dy = refs[0][...]
        for r in refs[1:cnt]:
            dy = dy + r[...]
        dx_ref, dw_ref, db_ref = refs[cnt:]
        _, vjp = jax.vjp(_conv_silu_fn, x_ref[...], w_ref[...], b_ref[...])
        dx, dw, db = vjp(dy)
        dx_ref[...] = dx.astype(BF16)

        @pl.when(pl.program_id(1) == 0)
        def _():
            dw_ref[...] = dw
            db_ref[...] = db

        @pl.when(pl.program_id(1) != 0)
        def _():
            dw_ref[...] += dw
            db_ref[...] += db

    dy_specs = []
    for arr, lead in dys:
        if lead is None:
            dy_specs.append(pl.BlockSpec((sl, cb), lambda j, s: (s, j)))
        else:
            dy_specs.append(pl.BlockSpec((None, sl, cb), functools.partial(lambda j, s, lead: (lead, s, j), lead=lead)))
    return pl.pallas_call(
        body, name=name, grid=(c // cb, bl),
        in_specs=[pl.BlockSpec((sl, cb), lambda j, s: (s, j)), pl.BlockSpec((kw, cb), lambda j, s: (0, j)),
                  pl.BlockSpec((1, cb), lambda j, s: (0, j))] + dy_specs,
        out_specs=[pl.BlockSpec((sl, cb), lambda j, s: (s, j)), pl.BlockSpec((kw, cb), lambda j, s: (0, j)),
                   pl.BlockSpec((1, cb), lambda j, s: (0, j))],
        out_shape=[SDS((n, c), BF16), SDS((kw, c), F32), SDS((1, c), F32)],
        compiler_params=_cp(("arbitrary", "arbitrary"), VMEM_LIMIT),
    )(x, w, b, *[a for a, _ in dys])


def _conv_glu(v, g, wv, wg, bv, bg, bl, cb, name):
    n, c = v.shape
    sl = n // bl
    kw = wv.shape[0]

    def body(v_ref, g_ref, wv_ref, wg_ref, bv_ref, bg_ref, o_ref):
        o_ref[...] = _conv_glu_fn(v_ref[...], g_ref[...], wv_ref[...], wg_ref[...], bv_ref[...], bg_ref[...]).astype(BF16)

    big = pl.BlockSpec((sl, cb), lambda s, j: (s, j))
    wsp = pl.BlockSpec((kw, cb), lambda s, j: (0, j))
    bsp = pl.BlockSpec((1, cb), lambda s, j: (0, j))
    return pl.pallas_call(
        body, name=name, grid=(bl, c // cb),
        in_specs=[big, big, wsp, wsp, bsp, bsp], out_specs=big, out_shape=SDS((n, c), BF16),
        compiler_params=_cp(("arbitrary", "arbitrary"), VMEM_LIMIT),
    )(v, g, wv, wg, bv, bg)


def _conv_glu_bwd(v, g, wv, wg, bv, bg, dact, bl, cb, name):
    n, c = v.shape
    sl = n // bl
    kw = wv.shape[0]

    def body(v_ref, g_ref, wv_ref, wg_ref, bv_ref, bg_ref, da_ref, dv_ref, dg_ref, dwv_ref, dwg_ref, dbv_ref, dbg_ref):
        _, vjp = jax.vjp(_conv_glu_fn, v_ref[...], g_ref[...], wv_ref[...], wg_ref[...], bv_ref[...], bg_ref[...])
        dv, dg, dwv, dwg, dbv, dbg = vjp(da_ref[...])
        dv_ref[...] = dv.astype(BF16)
        dg_ref[...] = dg.astype(BF16)

        @pl.when(pl.program_id(1) == 0)
        def _():
            dwv_ref[...] = dwv
            dwg_ref[...] = dwg
            dbv_ref[...] = dbv
            dbg_ref[...] = dbg

        @pl.when(pl.program_id(1) != 0)
        def _():
            dwv_ref[...] += dwv
            dwg_ref[...] += dwg
            dbv_ref[...] += dbv
            dbg_ref[...] += dbg

    big = pl.BlockSpec((sl, cb), lambda j, s: (s, j))
    wsp = pl.BlockSpec((kw, cb), lambda j, s: (0, j))
    bsp = pl.BlockSpec((1, cb), lambda j, s: (0, j))
    return pl.pallas_call(
        body, name=name, grid=(c // cb, bl),
        in_specs=[big, big, wsp, wsp, bsp, bsp, big],
        out_specs=[big, big, wsp, wsp, bsp, bsp],
        out_shape=[SDS((n, c), BF16), SDS((n, c), BF16), SDS((kw, c), F32), SDS((kw, c), F32), SDS((1, c), F32), SDS((1, c), F32)],
        compiler_params=_cp(("arbitrary", "arbitrary"), VMEM_LIMIT),
    )(v, g, wv, wg, bv, bg, dact)


_DIMS_T = {NN: (NT, TN, False, False), NT: (NN, TN, False, True), TN: (NT, NN, True, False)}


@functools.partial(jax.custom_vjp, nondiff_argnums=(2,))
def _bdot(a, b, dims):
    return _dot(a.astype(BF16), b.astype(BF16), dims)


def _bdot_fwd(a, b, dims):
    return _bdot(a, b, dims), (a, b)


def _bdot_bwd(dims, res, g):
    a, b = res
    da_dims, db_dims, a_swapped, b_swapped = _DIMS_T[dims]
    da = _bdot(b, g, da_dims) if a_swapped else _bdot(g, b, da_dims)
    db = _bdot(g, a, db_dims) if b_swapped else _bdot(a, g, db_dims)
    return da, db


_bdot.defvjp(_bdot_fwd, _bdot_bwd)


@functools.partial(jax.custom_vjp, nondiff_argnums=(1,))
def _expand_heads(v, width):
    return _split_dot(v, _head_matrix(width), NN)


def _head_matrix(width):
    hr = lax.broadcasted_iota(jnp.int32, (LANES, SSD_HEADS * width), 0)
    hc = lax.broadcasted_iota(jnp.int32, (LANES, SSD_HEADS * width), 1)
    return (hc // width == hr).astype(BF16)


def _split_dot(v, e, dims):
    hi = v.astype(BF16)
    lo = (v - hi.astype(F32)).astype(BF16)
    return _dot(hi, e, dims) + _dot(lo, e, dims)


_expand_heads.defvjp(lambda v, width: (_expand_heads(v, width), None),
                     lambda width, _, g: (_split_dot(g, _head_matrix(width), NT),))


def _ssd_chunk_fn(rev, xs, dtr, bms, cms, st, alog, dtb):
    q = dtr.shape[0]
    hd, per = SSD_HEAD_DIM, SSD_HEADS // SSD_GROUPS
    gw = per * hd
    r = lax.broadcasted_iota(jnp.int32, (q, q), 0)
    c = lax.broadcasted_iota(jnp.int32, (q, q), 1)
    sgn = 1 - 2 * rev
    tri = ((r - c) * sgn >= 0).astype(F32)
    tri_t = ((c - r) * sgn >= 0).astype(F32)
    r4 = lax.broadcasted_iota(jnp.int32, (q, per * q), 0)
    c4 = lax.broadcasted_iota(jnp.int32, (q, per * q), 1) % q
    mask4 = (r4 - c4) * sgn >= 0
    bdr = lax.broadcasted_iota(jnp.int32, (per * q, gw), 0) // q
    bdc = lax.broadcasted_iota(jnp.int32, (per * q, gw), 1) // hd
    diag = bdr == bdc
    dt = _softplus(dtr + dtb)
    dta = dt * (-jnp.exp(alog))
    cs = _dot(tri, dta, NN, HI)
    cs_t = _dot(dta, tri_t, TN, HI)
    tot = jnp.sum(dta, axis=0, keepdims=True)
    dt_x = _expand_heads(dt, hd)
    in_x = _expand_heads(jnp.exp(cs), hd)
    out_x = _expand_heads(jnp.exp(tot - cs), hd)
    ys, outs = [], []
    for g in range(SSD_GROUPS):
        bg, cg = bms[g], cms[g]
        heads = range(per * g, per * (g + 1))
        lanes = slice(gw * g, gw * (g + 1))
        scores = _bdot(cg, bg, NT)
        col = jnp.concatenate([jnp.broadcast_to(cs[:, h:h + 1], (q, q)) for h in heads], axis=1)
        row = jnp.concatenate([cs_t[h:h + 1, :] for h in heads], axis=1)
        seg = jnp.where(mask4, jnp.exp(jnp.where(mask4, col - row, 0.0)), 0.0)
        mcat = jnp.concatenate([scores] * per, axis=1) * seg
        xdt = xs[g] * dt_x[:, lanes]
        blocks = jnp.where(diag, jnp.concatenate([xdt] * per, axis=0), 0.0)
        y = _bdot(mcat, blocks, NN) + in_x[:, lanes] * _bdot(cg, st[g], NT)
        new = _bdot(xdt * out_x[:, lanes], bg, TN)
        keep = jnp.concatenate([jnp.exp(tot[:, h:h + 1]) * st[g][hd * j:hd * (j + 1), :] for j, h in enumerate(heads)], axis=0)
        ys.append(y)
        outs.append(keep + new)
    return ys, outs


def _ssd_scan(xs, bm, cm, dtr, alog2, dtb2, bl, name):
    n = xs.shape[0]
    q = SSD_CHUNK
    nc = n // bl // q
    hd, ns = SSD_HEAD_DIM, SSD_STATE
    gw = SSD_HEADS // SSD_GROUPS * hd

    def body(xs_ref, b_ref, c_ref, dt_ref, al_ref, db_ref, y_ref, sv_ref, st_ref):
        d, i = pl.program_id(0), pl.program_id(2)

        @pl.when(i == 0)
        def _():
            st_ref[...] = jnp.zeros(st_ref.shape, F32)

        st = [st_ref[gw * g:gw * (g + 1), :] for g in range(SSD_GROUPS)]
        sv_ref[...] = st_ref[...]
        xl = [xs_ref[:, gw * g:gw * (g + 1)] for g in range(SSD_GROUPS)]
        bms = [b_ref[:, ns * g:ns * (g + 1)] for g in range(SSD_GROUPS)]
        cms = [c_ref[:, ns * g:ns * (g + 1)] for g in range(SSD_GROUPS)]
        ys, outs = _ssd_chunk_fn(d, xl, dt_ref[...], bms, cms, st, al_ref[...], db_ref[...])
        for g in range(SSD_GROUPS):
            st_ref[gw * g:gw * (g + 1), :] = outs[g]
            y_ref[:, gw * g:gw * (g + 1)] = ys[g]

    def rowblk(d, s, i):
        return s * nc + i + d * (nc - 1 - 2 * i)

    return pl.pallas_call(
        body, name=name, grid=(2, bl, nc),
        in_specs=[pl.BlockSpec((q, SSD_HEADS * hd), lambda d, s, i: (rowblk(d, s, i), 0)),
                  pl.BlockSpec((q, SSD_GROUPS * ns), lambda d, s, i: (rowblk(d, s, i), 0)),
                  pl.BlockSpec((q, SSD_GROUPS * ns), lambda d, s, i: (rowblk(d, s, i), 0)),
                  pl.BlockSpec((q, LANES), lambda d, s, i: (rowblk(d, s, i), d)),
                  pl.BlockSpec((None, 1, LANES), lambda d, s, i: (d, 0, 0)),
                  pl.BlockSpec((None, 1, LANES), lambda d, s, i: (d, 0, 0))],
        out_specs=[pl.BlockSpec((None, q, SSD_HEADS * hd), lambda d, s, i: (d, rowblk(d, s, i), 0)),
                   pl.BlockSpec((None, None, SSD_HEADS * hd, ns), lambda d, s, i: (d, rowblk(d, s, i), 0, 0))],
        out_shape=[SDS((2, n, SSD_HEADS * hd), F32), SDS((2, n // q, SSD_HEADS * hd, ns), F32)],
        scratch_shapes=[pltpu.VMEM((SSD_HEADS * hd, ns), F32)],
        compiler_params=_cp(("arbitrary",) * 3, VMEM_LIMIT),
    )(xs, bm, cm, dtr, alog2, dtb2)


def _ssd_scan_bwd(xs, bm, cm, dtr, alog2, dtb2, saved, dy, bl, name):
    n = xs.shape[0]
    q = SSD_CHUNK
    nc = n // bl // q
    hd, ns = SSD_HEAD_DIM, SSD_STATE
    gw = SSD_HEADS // SSD_GROUPS * hd

    def body(xs_ref, b_ref, c_ref, dt_ref, al_ref, db_ref, sv_ref, dy_ref,
             dxs_ref, dbm_ref, dcm_ref, ddt_ref, dal_ref, ddb_ref, ds_ref):
        d, s, i = pl.program_id(0), pl.program_id(1), pl.program_id(2)

        @pl.when(i == 0)
        def _():
            ds_ref[...] = jnp.zeros(ds_ref.shape, F32)

        xl = [xs_ref[:, gw * g:gw * (g + 1)] for g in range(SSD_GROUPS)]
        bms = [b_ref[:, ns * g:ns * (g + 1)] for g in range(SSD_GROUPS)]
        cms = [c_ref[:, ns * g:ns * (g + 1)] for g in range(SSD_GROUPS)]
        st = [sv_ref[gw * g:gw * (g + 1), :] for g in range(SSD_GROUPS)]
        fn = functools.partial(_ssd_chunk_fn, d)
        _, vjp = jax.vjp(fn, xl, dt_ref[...], bms, cms, st, al_ref[...], db_ref[...])
        dys = [dy_ref[:, gw * g:gw * (g + 1)] for g in range(SSD_GROUPS)]
        dso = [ds_ref[gw * g:gw * (g + 1), :] for g in range(SSD_GROUPS)]
        dxl, ddt, dbg, dcg, dst, dal, ddb = vjp((dys, dso))
        for g in range(SSD_GROUPS):
            ds_ref[gw * g:gw * (g + 1), :] = dst[g]
            dxs_ref[:, gw * g:gw * (g + 1)] = dxl[g]
            dbm_ref[:, ns * g:ns * (g + 1)] = dbg[g]
            dcm_ref[:, ns * g:ns * (g + 1)] = dcg[g]
        ddt_ref[...] = ddt
        _acc_rows((dal_ref, ddb_ref), (dal, ddb), jnp.logical_and(s == 0, i == 0))

    def rowblk(d, s, i):
        return s * nc + (nc - 1 - i) + d * (2 * i - (nc - 1))

    row = lambda d, s, i: (rowblk(d, s, i), 0)
    drow = lambda d, s, i: (d, rowblk(d, s, i), 0)
    dfix = lambda d, s, i: (d, 0, 0)
    dcol = lambda d, s, i: (rowblk(d, s, i), d)
    return pl.pallas_call(
        body, name=name, grid=(2, bl, nc),
        in_specs=[pl.BlockSpec((q, SSD_HEADS * hd), row), pl.BlockSpec((q, SSD_GROUPS * ns), row),
                  pl.BlockSpec((q, SSD_GROUPS * ns), row), pl.BlockSpec((q, LANES), dcol),
                  pl.BlockSpec((None, 1, LANES), dfix), pl.BlockSpec((None, 1, LANES), dfix),
                  pl.BlockSpec((None, None, SSD_HEADS * hd, ns), lambda d, s, i: (d, rowblk(d, s, i), 0, 0)),
                  pl.BlockSpec((q, SSD_HEADS * hd), row)],
        out_specs=[pl.BlockSpec((None, q, SSD_HEADS * hd), drow), pl.BlockSpec((None, q, SSD_GROUPS * ns), drow),
                   pl.BlockSpec((None, q, SSD_GROUPS * ns), drow), pl.BlockSpec((q, LANES), dcol),
                   pl.BlockSpec((None, 1, LANES), dfix), pl.BlockSpec((None, 1, LANES), dfix)],
        out_shape=[SDS((2, n, SSD_HEADS * hd), F32), SDS((2, n, SSD_GROUPS * ns), F32), SDS((2, n, SSD_GROUPS * ns), F32),
                   SDS((n, 2 * LANES), F32), SDS((2, 1, LANES), F32), SDS((2, 1, LANES), F32)],
        scratch_shapes=[pltpu.VMEM((SSD_HEADS * hd, ns), F32)],
        compiler_params=_cp(("arbitrary",) * 3, VMEM_LIMIT),
    )(xs, bm, cm, dtr, alog2, dtb2, saved, dy)


def _s5_consts():
    t, ch, p = S5_T, S5_CH, S5_STATE
    lane = lax.broadcasted_iota(jnp.int32, (1, 2 * p), 1)
    pr = lax.broadcasted_iota(jnp.int32, (p, 2 * p), 0)
    pc = lax.broadcasted_iota(jnp.int32, (p, 2 * p), 1)
    cr = lax.broadcasted_iota(jnp.int32, (ch, t * ch), 0)
    cc = lax.broadcasted_iota(jnp.int32, (ch, t * ch), 1)
    return dict(
        left=lane < p,
        sg=jnp.where(lane < p, -1.0, 1.0).astype(F32),
        dup=(pc % p == pr).astype(F32),
        dup_l=(pc == pr).astype(F32),
        dup_r=(pc == pr + p).astype(F32),
        rep=(cc % ch == cr).astype(F32),
        rep0=(cc == cr).astype(F32),
    )


def _s5_mats(k, rev, lr, li, ls, bre, bim, cre, cim):
    t = S5_T
    step = jnp.exp(ls)
    lr2 = jnp.sum(lr * k["dup"], axis=0, keepdims=True)
    li2 = jnp.sum(li * k["dup"], axis=0, keepdims=True)

    def erow(d):
        ang = (d * step) * li2
        return jnp.exp((d * step) * lr2) * jnp.where(k["left"], jnp.cos(ang), jnp.sin(ang))

    es = [erow(d) for d in range(t + 1)]
    mag = jnp.exp(step * lr)
    ar, ai = mag * jnp.cos(step * li), mag * jnp.sin(step * li)
    den = lr * lr + li * li
    zr = ((ar - 1.0) * lr + ai * li) / den
    zi = (ai * lr - (ar - 1.0) * li) / den
    bbr = zr * bre - zi * bim
    bbi = zr * bim + zi * bre
    bt1 = _dot(bbr, k["dup"], TN, HI)
    bt2 = _dot(bbi, k["dup"], TN, HI)
    bst = _dot(bbr, k["dup_l"], TN, HI) - _dot(bbi, k["dup_r"], TN, HI)
    c1 = _dot(cre, k["dup"], NN, HI)
    c2 = _dot(cim, k["dup"], NN, HI)
    sg = k["sg"]
    ce = [e * c1 + sg * _swap(e) * c2 for e in es]
    lags = range(t - 1, -1, -1) if rev else range(t)
    kt = _dot(bst, jnp.concatenate([ce[d] for d in lags], axis=0), NT, HI)
    toep = jnp.concatenate([_lane_shift(kt, -S5_CH * (t - 1 - s) if rev else S5_CH * s) for s in range(t)], axis=0)
    w_out =jnp.concatenate([ce[(t - qq) if rev else (qq + 1)] * (-sg) for qq in range(t)], axis=0)
    w_st = jnp.concatenate(
        [(lambda e: e * bt1 + sg * _swap(e) * bt2)(es[s if rev else (t - 1 - s)]) for s in range(t)], axis=0)
    return toep, w_out, w_st, es[t]


def _cmul_row(k, e, z):
    es = _swap(e)
    return z * jnp.where(k["left"], e, es) + _swap(z) * (k["sg"] * jnp.where(k["left"], es, e))


def _s5_dir(k, rev, nck, x, mats):
    toep, w_out, w_st, a_t = mats
    acc = _dot(x, w_st)
    e = a_t
    kk = 1
    sign = -1 if rev else 1
    while kk < nck:
        acc = acc + _cmul_row(k, e, _shift(acc, sign * kk, nck))
        e = _cmul_row(k, e, e)
        kk *= 2
    prev = _shift(acc, sign, nck)
    return _dot(x, toep) + _dot(prev, w_out, NT)


def _s5_group_fn(nck, x, pf, pb, bre, bim, dcol, wv, wg, bv, bg):
    k = _s5_consts()
    t = S5_T
    y = x * jnp.sum(dcol * k["rep"], axis=0, keepdims=True)
    for rev, (lr, li, ls, cre, cim) in ((False, pf), (True, pb)):
        y = y + _s5_dir(k, rev, nck, x, _s5_mats(k, rev, lr, li, ls, bre, bim, cre, cim))
    gy = jax.nn.gelu(y)
    def kron_eye(w16):
        wide = _dot(w16, k["rep0"], NN, HI)
        return jnp.concatenate([_lane_shift(wide, S5_CH * qq) for qq in range(t)], axis=0)

    kv, kg = kron_eye(wv), kron_eye(wg)
    val =_dot(gy, kv) + jnp.sum(bv * k["rep"], axis=0, keepdims=True)
    gate = _dot(gy, kg) + jnp.sum(bg * k["rep"], axis=0, keepdims=True)
    return val * _sigmoid(gate)


def _s5_specs(r):
    p, ch = S5_STATE, S5_CH
    g3 = lambda i: (i, 0, 0)
    col = pl.BlockSpec((None, p, 1), g3)
    one = pl.BlockSpec((None, 1, 1), g3)
    cmat = pl.BlockSpec((None, ch, p), g3)
    bmat = pl.BlockSpec((None, p, ch), g3)
    ccol = pl.BlockSpec((None, ch, 1), g3)
    sq = pl.BlockSpec((None, ch, ch), g3)
    xs = pl.BlockSpec((None, r, S5_T * ch), g3)
    specs = [xs, col, col, one, cmat, cmat, col, col, one, cmat, cmat, bmat, bmat, ccol, sq, sq, ccol, ccol]
    return specs


def _s5_unpack(vals):
    x = vals[0]
    pf = tuple(vals[1:6])
    pb = tuple(vals[6:11])
    bre, bim, dcol, wv, wg, bv, bg = vals[11:18]
    return x, pf, pb, bre, bim, dcol, wv, wg, bv, bg


def _s5_fwd(args, nck, name):
    x = args[0]
    ng, r, w = x.shape

    def body(*refs):
        vals = [ref[...] for ref in refs[:18]]
        refs[18][...] = _s5_group_fn(nck, *_s5_unpack(vals))

    specs = _s5_specs(r)
    return pl.pallas_call(
        body, name=name, grid=(ng,), in_specs=specs, out_specs=specs[0], out_shape=SDS(x.shape, F32),
        compiler_params=_cp(("arbitrary",), VMEM_LIMIT),
    )(*args)


def _s5_bwd(args, dy, nck, name):
    x = args[0]
    ng, r, w = x.shape

    def body(*refs):
        vals = [ref[...] for ref in refs[:18]]
        _, vjp = jax.vjp(lambda *v: _s5_group_fn(nck, *_s5_unpack(v)), *vals)
        grads = vjp(refs[18][...])
        for o_ref, gval in zip(refs[19:], grads):
            o_ref[...] = gval.astype(o_ref.dtype)

    specs = _s5_specs(r)
    return pl.pallas_call(
        body, name=name, grid=(ng,), in_specs=specs + [specs[0]], out_specs=specs,
        out_shape=[SDS(x.shape, BF16)] + [SDS(a.shape, F32) for a in args[1:]],
        compiler_params=_cp(("arbitrary",), VMEM_LIMIT),
    )(*args, dy)


def _mix_fn(yf, yb, xs, z, s5o, dvec, nw_ssd, nw_s5):
    hr = lax.broadcasted_iota(jnp.int32, (LANES, SSD_HEADS * SSD_HEAD_DIM), 0)
    hc = lax.broadcasted_iota(jnp.int32, (LANES, SSD_HEADS * SSD_HEAD_DIM), 1)
    expand = (hc // SSD_HEAD_DIM == hr).astype(F32)
    dch = jnp.sum(dvec * expand, axis=0, keepdims=True)
    y = (yf + yb + dch * xs) * (z * _sigmoid(z))
    return _rms(y, nw_ssd), _rms(s5o, nw_s5)


def _mix(y2, xs, z, s5o, dvec, nw_ssd, nw_s5, tm, name):
    n, c1 = xs.shape
    c2 = s5o.shape[1]

    def body(yf_ref, yb_ref, xs_ref, z_ref, s_ref, d_ref, n1_ref, n2_ref, o_ref):
        o1, o2 = _mix_fn(yf_ref[...], yb_ref[...], xs_ref[...], z_ref[...], s_ref[...], d_ref[...], n1_ref[...], n2_ref[...])
        o_ref[:, :c1] = o1.astype(BF16)
        o_ref[:, c1:] = o2.astype(BF16)

    row = lambda i: (i, 0)
    fix = lambda i: (0, 0)
    return pl.pallas_call(
        body, name=name, grid=(n // tm,),
        in_specs=[pl.BlockSpec((None, tm, c1), lambda i: (0, i, 0)), pl.BlockSpec((None, tm, c1), lambda i: (1, i, 0)),
                  pl.BlockSpec((tm, c1), row), pl.BlockSpec((tm, c1), row), pl.BlockSpec((tm, c2), row),
                  pl.BlockSpec((LANES, 1), fix), pl.BlockSpec((1, c1), fix), pl.BlockSpec((1, c2), fix)],
        out_specs=pl.BlockSpec((tm, c1 + c2), row), out_shape=SDS((n, c1 + c2), BF16),
        compiler_params=_cp(("arbitrary",), VMEM_LIMIT),
    )(y2, y2, xs, z, s5o, dvec, nw_ssd, nw_s5)


def _acc_rows(refs, vals, first):
    @pl.when(first)
    def _():
        for ref, v in zip(refs, vals):
            ref[...] = v

    @pl.when(jnp.logical_not(first))
    def _():
        for ref, v in zip(refs, vals):
            ref[...] += v


def _mix_bwd(y2, xs, z, s5o, dvec, nw_ssd, nw_s5, dmix, tm, name):
    n, c1 = xs.shape
    c2 = s5o.shape[1]

    def body(yf_ref, yb_ref, xs_ref, z_ref, s_ref, d_ref, n1_ref, n2_ref, dm_ref,
             dy_ref, dxs_ref, dz_ref, ds_ref, dd_ref, dn1_ref, dn2_ref):
        _, vjp = jax.vjp(_mix_fn, yf_ref[...], yb_ref[...], xs_ref[...], z_ref[...], s_ref[...], d_ref[...], n1_ref[...], n2_ref[...])
        dyf, _, dxs, dz, ds, dd, dn1, dn2 = vjp((dm_ref[:, :c1], dm_ref[:, c1:]))
        dy_ref[...] = dyf
        dxs_ref[...] = dxs
        dz_ref[...] = dz.astype(BF16)
        ds_ref[...] = ds
        _acc_rows((dd_ref, dn1_ref, dn2_ref), (dd, dn1, dn2), pl.program_id(0) == 0)

    row = lambda i: (i, 0)
    fix = lambda i: (0, 0)
    return pl.pallas_call(
        body, name=name, grid=(n // tm,),
        in_specs=[pl.BlockSpec((None, tm, c1), lambda i: (0, i, 0)), pl.BlockSpec((None, tm, c1), lambda i: (1, i, 0)),
                  pl.BlockSpec((tm, c1), row), pl.BlockSpec((tm, c1), row), pl.BlockSpec((tm, c2), row),
                  pl.BlockSpec((LANES, 1), fix), pl.BlockSpec((1, c1), fix), pl.BlockSpec((1, c2), fix),
                  pl.BlockSpec((tm, c1 + c2), row)],
        out_specs=[pl.BlockSpec((tm, c1), row), pl.BlockSpec((tm, c1), row), pl.BlockSpec((tm, c1), row), pl.BlockSpec((tm, c2), row),
                   pl.BlockSpec((LANES, 1), fix), pl.BlockSpec((1, c1), fix), pl.BlockSpec((1, c2), fix)],
        out_shape=[SDS((n, c1), F32), SDS((n, c1), F32), SDS((n, c1), BF16), SDS((n, c2), F32),
                   SDS((LANES, 1), F32), SDS((1, c1), F32), SDS((1, c2), F32)],
        compiler_params=_cp(("arbitrary",), VMEM_LIMIT),
    )(y2, y2, xs, z, s5o, dvec, nw_ssd, nw_s5, dmix)


def _final_loss(h2, nw, tgt, tm, name):
    n, d = h2.shape

    def loss_fn(h, w, t):
        e = _rms(h, w) - t
        return (0.5 / d) * jnp.sum(e * e)

    def body(h_ref, w_ref, t_ref, l_ref, dh_ref, dw_ref):
        loss, (dh, dw) = jax.value_and_grad(loss_fn, argnums=(0, 1))(h_ref[...], w_ref[...], t_ref[...])
        dh_ref[...] = dh
        _acc_rows((l_ref, dw_ref), (jnp.full((1, LANES), loss, F32), dw), pl.program_id(0) == 0)

    row = lambda i: (i, 0)
    fix = lambda i: (0, 0)
    return pl.pallas_call(
        body, name=name, grid=(n // tm,),
        in_specs=[pl.BlockSpec((tm, d), row), pl.BlockSpec((1, d), fix), pl.BlockSpec((tm, d), row)],
        out_specs=[pl.BlockSpec((1, LANES), fix), pl.BlockSpec((tm, d), row), pl.BlockSpec((1, d), fix)],
        out_shape=[SDS((1, LANES), F32), SDS((n, d), F32), SDS((1, d), F32)],
        compiler_params=_cp(("arbitrary",), VMEM_LIMIT),
    )(h2, nw, tgt)


def _norm_bwd(x, nw, dhn, dres, tm, name):
    n, d = x.shape

    def body(x_ref, w_ref, g_ref, r_ref, dx_ref, dw_ref):
        _, vjp = jax.vjp(_rms, x_ref[...], w_ref[...])
        dx, dw = vjp(g_ref[...])
        dx_ref[...] = r_ref[...] + dx
        _acc_rows((dw_ref,), (dw,), pl.program_id(0) == 0)

    row = lambda i: (i, 0)
    fix = lambda i: (0, 0)
    return pl.pallas_call(
        body, name=name, grid=(n // tm,),
        in_specs=[pl.BlockSpec((tm, d), row), pl.BlockSpec((1, d), fix), pl.BlockSpec((tm, d), row), pl.BlockSpec((tm, d), row)],
        out_specs=[pl.BlockSpec((tm, d), row), pl.BlockSpec((1, d), fix)],
        out_shape=[SDS((n, d), F32), SDS((1, d), F32)],
        compiler_params=_cp(("arbitrary",), VMEM_LIMIT),
    )(x, nw, dhn, dres)


def _row_tile(n, cap=512):
    for t in range(min(cap, n) // 8 * 8, 7, -8):
        if n % t == 0:
            return t
    return n


def _sum_lead(a, name):
    kk, n, c = a.shape
    tm = _row_tile(n)

    def body(a_ref, o_ref):
        acc = a_ref[0].astype(F32)
        for i in range(1, kk):
            acc = acc + a_ref[i].astype(F32)
        o_ref[...] = acc

    return pl.pallas_call(
        body, name=name, grid=(n // tm,),
        in_specs=[pl.BlockSpec((kk, tm, c), lambda i: (0, i, 0))],
        out_specs=pl.BlockSpec((tm, c), lambda i: (i, 0)), out_shape=SDS((n, c), F32),
        compiler_params=_cp(("arbitrary",), VMEM_LIMIT),
    )(a)


def _adamw(w, g, m, v, name):
    n, c = w.shape
    tm = _row_tile(n)

    def body(w_ref, g_ref, m_ref, v_ref, d_ref, nm_ref, nv_ref):
        gv = g_ref[...]
        mn = ADAM_B1 * m_ref[...] + (1.0 - ADAM_B1) * gv
        vn = ADAM_B2 * v_ref[...] + (1.0 - ADAM_B2) * jnp.square(gv)
        m_hat = mn / (1.0 - ADAM_B1 ** ADAM_STEP)
        v_hat = vn / (1.0 - ADAM_B2 ** ADAM_STEP)
        d_ref[...] = -ADAM_LR * (m_hat / (jnp.sqrt(v_hat) + ADAM_EPS) + ADAM_WD * w_ref[...])
        nm_ref[...] = mn
        nv_ref[...] = vn

    spec = pl.BlockSpec((tm, c), lambda i: (i, 0))
    return pl.pallas_call(
        body, name=name, grid=(n // tm,), in_specs=[spec] * 4, out_specs=[spec] * 3,
        out_shape=[SDS((n, c), F32)] * 3, compiler_params=_cp(("arbitrary",), VMEM_LIMIT),
    )(w, g, m, v)


ANY = pl.BlockSpec(memory_space=pl.ANY)


def _me():
    return lax.axis_index("x"), lax.axis_index("y"), lax.axis_index("c")


def _gather_xy(split, whole, name):
    ns, cnt = len(split), len(split) + len(whole)

    def body(*refs):
        src, dst = refs[:cnt], refs[cnt:2 * cnt]
        send, recv = refs[2 * cnt:]
        x, y, c = _me()
        mine = 2 * x + y
        chips = [(1 - x, y), (x, 1 - y), (1 - x, 1 - y)]

        def ici(a, j, slot):
            px, py = chips[j]
            if a < ns:
                s_ref, d_ref = src[a].at[c], dst[a].at[slot].at[c]
            else:
                s_ref, d_ref = src[a], dst[a].at[slot]
            return pltpu.make_async_remote_copy(s_ref, d_ref, send.at[3 * a + j], recv.at[3 * a + j],
                                                device_id=(px, py, c), device_id_type=MESH)

        def d2d(a, j, half):
            px, py = chips[j]
            ref = dst[a].at[2 * px + py].at[half]
            return pltpu.make_async_remote_copy(ref, ref, send.at[3 * cnt + 3 * a + j], recv.at[3 * cnt + 3 * a + j],
                                                device_id=(x, y, 1 - c), device_id_type=MESH)

        def own(a):
            return pltpu.make_async_remote_copy(src[a], dst[a].at[mine], send.at[nsem - cnt + a], recv.at[nsem - cnt + a],
                                                device_id=(x, y, 1 - c), device_id_type=MESH)

        started = []
        for a in range(cnt):
            cp = own(a)
            cp.start()
            started.append(cp)
            for j in range(3):
                cp = ici(a, j, mine)
                cp.start()
                started.append(cp)
        for a in range(cnt):
            for j, (px, py) in enumerate(chips):
                ici(a, j, 2 * px + py).wait_recv()
                if a < ns:
                    cp = d2d(a, j, c)
                    cp.start()
                    started.append(cp)
        for a in range(ns):
            for j in range(3):
                d2d(a, j, 1 - c).wait_recv()
        for a in range(cnt):
            own(a).wait_recv()
        for cp in started:
            cp.wait_send()

    nsem = 3 * cnt + 3 * ns + cnt
    return pl.pallas_call(
        body, name=name, in_specs=[ANY] * cnt, out_specs=[ANY] * cnt,
        out_shape=[SDS((4,) + s.shape, s.dtype) for s in split + whole],
        scratch_shapes=[pltpu.SemaphoreType.DMA((nsem,)), pltpu.SemaphoreType.DMA((nsem,))],
    )(*split, *whole)


HBM = pl.BlockSpec(memory_space=pltpu.HBM)
SEM = pl.BlockSpec(memory_space=pltpu.SEMAPHORE)
DATAFLOW = pltpu.SideEffectType.DATAFLOW_SIDE_EFFECTING


def _whole_copies(srcs, dsts, sends, recvs):
    x, y, c = _me()
    peers = [(1 - x, y, c), (x, 1 - y, c), (1 - x, 1 - y, c), (x, y, 1 - c)]
    return [pltpu.make_async_remote_copy(srcs[a], dsts[a].at[2 * x + y], sends[4 * a + j], recvs[4 * a + j],
                                         device_id=peer, device_id_type=MESH)
            for a in range(len(srcs)) for j, peer in enumerate(peers)]


def _scatter_copies(srcs, dsts, sends, recvs):
    x, y, c = _me()
    chips = [(1 - x, y), (x, 1 - y), (1 - x, 1 - y)]
    return [pltpu.make_async_remote_copy(srcs[a].at[2 * px + py], dsts[a].at[2 * x + y], sends[3 * a + j], recvs[3 * a + j],
                                         device_id=(px, py, c), device_id_type=MESH)
            for a in range(len(srcs)) for j, (px, py) in enumerate(chips)]


def _copies_start(copies, per, shards, after, name):
    cnt = len(shards)
    ncp = per * cnt

    def body(*refs):
        srcs, lands = refs[:cnt], refs[cnt:2 * cnt]
        outs = refs[2 * cnt + 1:]
        for cp in copies(srcs, lands, outs[:ncp], outs[ncp:2 * ncp]):
            cp.start()
        outs[-1][...] = jnp.zeros_like(outs[-1])

    lands = [lax.empty((4,) + (s.shape if per == 4 else s.shape[1:]), s.dtype) for s in shards]
    ops = [pltpu.with_memory_space_constraint(a, pltpu.HBM) for a in list(shards) + lands]
    res = pl.pallas_call(
        body, name=name, in_specs=[HBM] * (2 * cnt) + [ANY],
        out_shape=tuple([pltpu.SemaphoreType.DMA(())] * (2 * ncp) + [pltpu.HBM(a.shape, a.dtype) for a in ops] + [SDS((8, LANES), F32)]),
        out_specs=tuple([SEM] * (2 * ncp) + [HBM] * (2 * cnt) + [pl.BlockSpec(memory_space=pltpu.VMEM)]),
        input_output_aliases={i: 2 * ncp + i for i in range(2 * cnt)},
        compiler_params=pltpu.CompilerParams(has_side_effects=DATAFLOW),
    )(*ops, after)
    return res[:2 * ncp], res[2 * ncp:2 * ncp + cnt], res[2 * ncp + cnt:2 * ncp + 2 * cnt], res[-1]


def _copies_wait(copies, per, handle, after, name):
    sems, srcs, lands, _ = handle
    cnt = len(srcs)
    ncp = per * cnt

    def body(*refs):
        sem_refs = refs[2 * cnt:2 * cnt + 2 * ncp]
        for cp in copies(refs[:cnt], refs[cnt:2 * cnt], sem_refs[:ncp], sem_refs[ncp:]):
            cp.wait_send()
            cp.wait_recv()

    res = pl.pallas_call(
        body, name=name, in_specs=[HBM] * (2 * cnt) + [SEM] * (2 * ncp) + [ANY],
        out_shape=tuple(pltpu.HBM(a.shape, a.dtype) for a in list(srcs) + list(lands)),
        out_specs=tuple([HBM] * (2 * cnt)), input_output_aliases={i: i for i in range(2 * cnt)},
        compiler_params=pltpu.CompilerParams(has_side_effects=DATAFLOW),
    )(*srcs, *lands, *sems, after)
    return list(res[cnt:])


def _swap_sibling(parts, pick, name):
    cnt = len(parts)

    def body(*refs):
        src, dst = refs[:cnt], refs[cnt:2 * cnt]
        send, recv = refs[2 * cnt:]
        x, y, c = _me()
        cps = []
        for a in range(cnt):
            cp = pltpu.make_async_remote_copy(src[a].at[1 - c] if pick else src[a], dst[a], send.at[a], recv.at[a],
                                              device_id=(x, y, 1 - c), device_id_type=MESH)
            cp.start()
            cps.append(cp)
        for cp in cps:
            cp.wait()

    return pl.pallas_call(
        body, name=name, in_specs=[ANY] * cnt, out_specs=[ANY] * cnt,
        out_shape=[SDS(p.shape[1:] if pick else p.shape, p.dtype) for p in parts],
        scratch_shapes=[pltpu.SemaphoreType.DMA((cnt,)), pltpu.SemaphoreType.DMA((cnt,))],
    )(*parts)


def _scatter_xy(parts, name):
    cnt = len(parts)

    def body(*refs):
        src, dst = refs[:cnt], refs[cnt:2 * cnt]
        send, recv, loc = refs[2 * cnt:]
        x, y, c = _me()
        mine = 2 * x + y
        chips = [(1 - x, y), (x, 1 - y), (1 - x, 1 - y)]
        local = []
        for a in range(cnt):
            cp = pltpu.make_async_copy(src[a].at[mine], dst[a].at[mine], loc.at[a])
            cp.start()
            local.append(cp)
        sends = []
        for a in range(cnt):
            for j, (px, py) in enumerate(chips):
                cp = pltpu.make_async_remote_copy(src[a].at[2 * px + py], dst[a].at[mine], send.at[3 * a + j], recv.at[3 * a + j],
                                                  device_id=(px, py, c), device_id_type=MESH)
                cp.start()
                sends.append(cp)
        for a in range(cnt):
            for j, (px, py) in enumerate(chips):
                pltpu.make_async_remote_copy(src[a].at[mine], dst[a].at[2 * px + py], send.at[3 * a + j], recv.at[3 * a + j],
                                             device_id=(px, py, c), device_id_type=MESH).wait_recv()
        for cp in sends:
            cp.wait_send()
        for cp in local:
            cp.wait()

    return pl.pallas_call(
        body, name=name, in_specs=[ANY] * cnt, out_specs=[ANY] * cnt,
        out_shape=[SDS(p.shape, p.dtype) for p in parts],
        scratch_shapes=[pltpu.SemaphoreType.DMA((3 * cnt,)), pltpu.SemaphoreType.DMA((3 * cnt,)), pltpu.SemaphoreType.DMA((cnt,))],
    )(*parts)


def _bcast_all(buf, name):
    def body(src, dst, send, recv, loc):
        x, y, c = _me()
        mine = 4 * x + 2 * y + c
        own = pltpu.make_async_copy(src, dst.at[mine], loc)
        own.start()
        sends = []
        for k in range(1, 8):
            px, py, pc = x ^ (k >> 2), y ^ ((k >> 1) & 1), c ^ (k & 1)
            cp = pltpu.make_async_remote_copy(src, dst.at[mine], send.at[k - 1], recv.at[k - 1],
                                              device_id=(px, py, pc), device_id_type=MESH)
            cp.start()
            sends.append(cp)
        for k in range(1, 8):
            px, py, pc = x ^ (k >> 2), y ^ ((k >> 1) & 1), c ^ (k & 1)
            pltpu.make_async_remote_copy(src, dst.at[4 * px + 2 * py + pc], send.at[k - 1], recv.at[k - 1],
                                         device_id=(px, py, pc), device_id_type=MESH).wait_recv()
        for cp in sends:
            cp.wait_send()
        own.wait()

    return pl.pallas_call(
        body, name=name, in_specs=[ANY], out_specs=ANY, out_shape=SDS((8,) + buf.shape, buf.dtype),
        scratch_shapes=[pltpu.SemaphoreType.DMA((7,)), pltpu.SemaphoreType.DMA((7,)), pltpu.SemaphoreType.DMA(())],
    )(buf)


def _sum_slots(recv, own, chip, name):
    kk, n, c = recv.shape
    tm = _row_tile(n)

    def body(chip_ref, r_ref, o_ref, out_ref):
        acc = None
        for j in range(kk):
            v = jnp.where(chip_ref[0] == j, o_ref[...], r_ref[j]).astype(F32)
            acc = v if acc is None else acc + v
        out_ref[...] = acc

    grid_spec = pltpu.PrefetchScalarGridSpec(
        num_scalar_prefetch=1, grid=(n // tm,),
        in_specs=[pl.BlockSpec((kk, tm, c), lambda i, chip_ref: (0, i, 0)),
                  pl.BlockSpec((None, tm, c), lambda i, chip_ref: (chip_ref[0], i, 0))],
        out_specs=pl.BlockSpec((tm, c), lambda i, chip_ref: (i, 0)))
    return pl.pallas_call(body, name=name, grid_spec=grid_spec, out_shape=SDS((n, c), F32),
                          compiler_params=_cp(("arbitrary",), VMEM_LIMIT))(chip.reshape(1), recv, own)


def _add_half(parts, got, core, dtype, name):
    shp = got.shape
    a2, b2 = parts.reshape(2, -1, shp[-1]), got.reshape(-1, shp[-1])
    n, c = b2.shape
    tm = _row_tile(n, 256)

    def body(core_ref, a_ref, b_ref, o_ref):
        o_ref[...] = (a_ref[...] + b_ref[...]).astype(dtype)

    spec = pl.BlockSpec((tm, c), lambda i, core_ref: (i, 0))
    grid_spec = pltpu.PrefetchScalarGridSpec(
        num_scalar_prefetch=1, grid=(n // tm,),
        in_specs=[pl.BlockSpec((None, tm, c), lambda i, core_ref: (core_ref[0], i, 0)), spec], out_specs=spec)
    return pl.pallas_call(body, name=name, grid_spec=grid_spec, out_shape=SDS((n, c), dtype),
                          compiler_params=_cp(("arbitrary",), VMEM_LIMIT))(core.reshape(1), a2, b2).reshape(shp)


def _x_layout(u, name):
    n, c = u.shape
    t, ch = S5_T, S5_CH
    gb = LANES // ch
    rows = min(64, n // t)

    def body(u_ref, o_ref):
        for s in range(t):
            us = u_ref[pl.ds(s, rows, stride=t), :]
            for g in range(gb):
                o_ref[g, :, ch * s:ch * (s + 1)] = us[:, ch * g:ch * (g + 1)]

    return pl.pallas_call(
        body, name=name, grid=(n // (rows * t), c // LANES),
        in_specs=[pl.BlockSpec((rows * t, LANES), lambda i, j: (i, j))],
        out_specs=pl.BlockSpec((gb, rows, t * ch), lambda i, j: (j, i, 0)),
        out_shape=SDS((c // ch, n // t, t * ch), F32),
        compiler_params=_cp(("arbitrary", "arbitrary"), VMEM_LIMIT),
    )(u)


def _token_layout(xg, name):
    ng, r, w = xg.shape
    t, ch = S5_T, S5_CH
    gb = LANES // ch
    rows = min(64, r)

    def body(x_ref, o_ref):
        for s in range(t):
            parts = [x_ref[g, :, ch * s:ch * (s + 1)].astype(F32) for g in range(gb)]
            o_ref[pl.ds(s, rows, stride=t), :] = jnp.concatenate(parts, axis=1)

    return pl.pallas_call(
        body, name=name, grid=(r // rows, ng // gb),
        in_specs=[pl.BlockSpec((gb, rows, w), lambda i, j: (j, i, 0))],
        out_specs=pl.BlockSpec((rows * t, LANES), lambda i, j: (i, j)),
        out_shape=SDS((r * t, ng * ch), F32),
        compiler_params=_cp(("arbitrary", "arbitrary"), VMEM_LIMIT),
    )(xg)


def _pad_lanes(a, lanes=LANES):
    return jnp.pad(a, ((0, 0), (0, lanes - a.shape[1])))


def _local_step(x, tgt, p, bl, late, early):
    p = dict(p)
    n, d = x.shape
    sw = SSD_HEADS * SSD_HEAD_DIM
    gn = SSD_GROUPS * SSD_STATE
    tm = min(n, 512)
    tm_ffn = min(n, 256)
    nck = n // bl // S5_T
    s5w = S5_GROUPS * S5_CH

    w_in = p["w_in"]
    o1, o2, o3, o4 = sw, sw + sw, sw + sw + gn, sw + sw + 2 * gn
    w_z, w_xs, w_b, w_c = w_in[:, :o1], w_in[:, o1:o2], w_in[:, o2:o3], w_in[:, o3:o4]
    w_dt = jnp.concatenate([_pad_lanes(w_in[:, o4:o4 + SSD_HEADS]), _pad_lanes(w_in[:, o4 + SSD_HEADS:o4 + 2 * SSD_HEADS])], 1)
    w_u = w_in[:, o4 + 2 * SSD_HEADS:]
    in_ws = [w_z, w_xs, w_b, w_c, w_dt, w_u]
    cw, cb_ = p["ssd_conv_w"], p["ssd_conv_b"]
    conv_parts = [(cw[:, :sw], cb_[:, :sw]), (cw[:, sw:sw + gn], cb_[:, sw:sw + gn]), (cw[:, sw + gn:], cb_[:, sw + gn:])]
    alog2 = jnp.stack([_pad_lanes(p["ssd_a_log_fwd"]), _pad_lanes(p["ssd_a_log_bwd"])])
    dtb2 = jnp.stack([_pad_lanes(p["ssd_dt_bias_fwd"]), _pad_lanes(p["ssd_dt_bias_bwd"])])
    dvec = _pad_lanes(p["ssd_d"]).reshape(LANES, 1)

    hn, z, xs_pre, b_pre, c_pre, dtr, u = _norm_matmul(x, p["norm_mix_w"], in_ws, tm, "in_proj")
    pres = [xs_pre, b_pre, c_pre]
    acts = [_conv_silu(pre, w, b, bl, min(256, pre.shape[1]), f"ssd_conv_{i}") for i, (pre, (w, b)) in enumerate(zip(pres, conv_parts))]
    xs_a, b_a, c_a = acts
    y2, saved = _ssd_scan(xs_a, b_a, c_a, dtr, alog2, dtb2, bl, "ssd_scan")

    def col(a):
        return a.reshape(a.shape + (1,))

    s5_params = [
        col(p["s5_lambda_re_fwd"]), col(p["s5_lambda_im_fwd"]), p["s5_log_step_fwd"].reshape(S5_GROUPS, 1, 1), p["s5_c_re_fwd"], p["s5_c_im_fwd"],
        col(p["s5_lambda_re_bwd"]), col(p["s5_lambda_im_bwd"]), p["s5_log_step_bwd"].reshape(S5_GROUPS, 1, 1), p["s5_c_re_bwd"], p["s5_c_im_bwd"],
        p["s5_b_re"], p["s5_b_im"], col(p["s5_d"].reshape(S5_GROUPS, S5_CH)),
        p["s5_glu_w"][:, :, :S5_CH], p["s5_glu_w"][:, :, S5_CH:], col(p["s5_glu_b"][:, :S5_CH]), col(p["s5_glu_b"][:, S5_CH:]),
    ]
    s5_args = [_x_layout(u, "s5_u_blocks")] + s5_params
    s5o = _token_layout(_s5_fwd(s5_args, nck, "s5_fwd"), "s5_y_tokens")
    ymix = _mix(y2, xs_a, z, s5o, dvec, p["ssd_norm_w"], p["s5_norm_w"], tm, "mix")
    p["w_out"], p["w_up"], p["w_down"] = late(ymix)
    dff = p["w_down"].shape[0]
    h1 = _matmul_res(ymix, p["w_out"], x, tm, "out_proj")
    w_up = p["w_up"]
    hn2, up_v, up_g = _norm_matmul(h1, p["norm_ffn_w"], [w_up[:, :dff], w_up[:, dff:]], tm_ffn, "ffn_up")
    fw, fb = p["ffn_conv_w"], p["ffn_conv_b"]
    act = _conv_glu(up_v, up_g, fw[:, :dff], fw[:, dff:], fb[:, :dff], fb[:, dff:], bl, 256, "ffn_conv")
    h2 = _matmul_res(act, p["w_down"], h1, tm, "ffn_down")
    loss, dh2, g_nfw = _final_loss(h2, p["norm_final_w"].reshape(1, d), tgt, tm, "final_loss")

    g = {"norm_final_w": g_nfw.reshape(d)}
    g["w_down"] = _matmul_tn(act, dh2, tm, d, "ffn_down_dw")
    dact = _matmul_nt([dh2], [p["w_down"]], tm, "ffn_down_dx")
    dup_v, dup_g, dwv, dwg, dbv, dbg = _conv_glu_bwd(up_v, up_g, fw[:, :dff], fw[:, dff:], fb[:, :dff], fb[:, dff:], dact, bl, 256, "ffn_conv_bwd")
    g["ffn_conv_w"] = jnp.concatenate([dwv, dwg], 1)
    g["ffn_conv_b"] = jnp.concatenate([dbv, dbg], 1)
    g["w_up"] = jnp.concatenate([_matmul_tn(hn2, dup_v, tm, dff // 2, "ffn_up_dw_v"), _matmul_tn(hn2, dup_g, tm, dff // 2, "ffn_up_dw_g")], 1)
    dhn2 = _matmul_nt([dup_v, dup_g], [w_up[:, :dff], w_up[:, dff:]], tm_ffn, "ffn_up_dx")
    g["ffn_reduce"] = early(g["w_up"], g["w_down"])
    behind = g["ffn_reduce"][1][3][0, 0]
    dh1, g["norm_ffn_w"] = _norm_bwd(h1, p["norm_ffn_w"] + behind, dhn2, dh2, tm, "ffn_norm_bwd")
    g["w_out"] = _matmul_tn(ymix, dh1, tm, d, "out_proj_dw")
    dmix = _matmul_nt([dh1], [p["w_out"]], tm, "out_proj_dx")
    dyssd, dxs_gate, dz, ds5o, g_d, g["ssd_norm_w"], g["s5_norm_w"] = _mix_bwd(
        y2, xs_a, z, s5o, dvec, p["ssd_norm_w"], p["s5_norm_w"], dmix, tm, "mix_bwd")
    g["ssd_d"] = g_d[:SSD_HEADS].reshape(1, SSD_HEADS)
    s5g = _s5_bwd(s5_args, _x_layout(ds5o, "s5_dy_blocks"), nck, "s5_bwd")
    du = _token_layout(s5g[0], "s5_du_tokens")
    (g["s5_lambda_re_fwd"], g["s5_lambda_im_fwd"], g["s5_log_step_fwd"], g["s5_c_re_fwd"], g["s5_c_im_fwd"],
     g["s5_lambda_re_bwd"], g["s5_lambda_im_bwd"], g["s5_log_step_bwd"], g["s5_c_re_bwd"], g["s5_c_im_bwd"],
     g["s5_b_re"], g["s5_b_im"], g_s5d, g_wv, g_wg, g_bv, g_bg) = s5g[1:]
    for k_ in ("s5_lambda_re_fwd", "s5_lambda_im_fwd", "s5_lambda_re_bwd", "s5_lambda_im_bwd"):
        g[k_] = g[k_].reshape(S5_GROUPS, S5_STATE)
    for k_ in ("s5_log_step_fwd", "s5_log_step_bwd"):
        g[k_] = g[k_].reshape(S5_GROUPS)
    g["s5_d"] = g_s5d.reshape(1, s5w)
    g["s5_glu_w"] = jnp.concatenate([g_wv, g_wg], 2)
    g["s5_glu_b"] = jnp.concatenate([g_bv.reshape(S5_GROUPS, S5_CH), g_bg.reshape(S5_GROUPS, S5_CH)], 1)
    dxs2, dbm2, dcm2, ddtr, dal2, ddb2 = _ssd_scan_bwd(xs_a, b_a, c_a, dtr, alog2, dtb2, saved, dyssd, bl, "ssd_scan_bwd")
    g["ssd_a_log_fwd"], g["ssd_a_log_bwd"] = dal2[0, :, :SSD_HEADS], dal2[1, :, :SSD_HEADS]
    g["ssd_dt_bias_fwd"], g["ssd_dt_bias_bwd"] = ddb2[0, :, :SSD_HEADS], ddb2[1, :, :SSD_HEADS]
    cots = [[(dxs2, 0), (dxs2, 1), (dxs_gate, None)], [(dbm2, 0), (dbm2, 1)], [(dcm2, 0), (dcm2, 1)]]
    dpres, dcw, dcb = [], [], []
    for i, (pre, (w, b), cot) in enumerate(zip(pres, conv_parts, cots)):
        dp, dw_, db_ = _conv_silu_bwd(pre, w, b, cot, bl, min(256, pre.shape[1]), f"ssd_conv_bwd_{i}")
        dpres.append(dp)
        dcw.append(dw_)
        dcb.append(db_)
    g["ssd_conv_w"] = jnp.concatenate(dcw, 1)
    g["ssd_conv_b"] = jnp.concatenate(dcb, 1)
    dprojs = [dz, dpres[0], dpres[1], dpres[2], ddtr, du]
    dws = list(_matmul_tn_multi(hn, dprojs, tm_ffn, "in_proj_dw"))
    dws[4] = jnp.concatenate([dws[4][:, :SSD_HEADS], dws[4][:, LANES:LANES + SSD_HEADS]], 1)
    g["w_in"] = jnp.concatenate(dws, 1)
    dhn = _matmul_nt(dprojs, in_ws, tm, "in_proj_dx")
    grad_x, g["norm_mix_w"] = _norm_bwd(x, p["norm_mix_w"], dhn, dh1, tm, "mix_norm_bwd")
    return loss, grad_x, g


_WEIGHTS = ['norm_mix_w', 'w_in', 'ssd_conv_w', 'ssd_conv_b', 'ssd_dt_bias_fwd', 'ssd_dt_bias_bwd', 'ssd_a_log_fwd', 'ssd_a_log_bwd',
            'ssd_d', 'ssd_norm_w', 's5_lambda_re_fwd', 's5_lambda_im_fwd', 's5_log_step_fwd', 's5_lambda_re_bwd', 's5_lambda_im_bwd',
            's5_log_step_bwd', 's5_b_re', 's5_b_im', 's5_c_re_fwd', 's5_c_im_fwd', 's5_c_re_bwd', 's5_c_im_bwd', 's5_d', 's5_glu_w',
            's5_glu_b', 's5_norm_w', 'w_out', 'norm_ffn_w', 'ffn_w_up', 'ffn_conv_w', 'ffn_conv_b', 'ffn_w_down', 'norm_final_w']
_BIG = ('w_in', 'w_out', 'ffn_w_up', 'ffn_w_down')
_CONV = ('ssd_conv_w', 'ffn_conv_w')


def _pack(arrs):
    flat = jnp.concatenate([a.reshape(-1) for a in arrs])
    rows = -(-flat.shape[0] // (64 * LANES)) * 64
    return jnp.pad(flat, (0, rows * LANES - flat.shape[0])).reshape(rows, LANES)


def _unpack(buf, shapes):
    flat = buf.reshape(-1)
    out, off = [], 0
    for shp in shapes:
        size = math.prod(shp)
        out.append(flat[off:off + size].reshape(shp))
        off += size
    return out


def kernel(x, norm_mix_w, w_in, ssd_conv_w, ssd_conv_b, ssd_dt_bias_fwd, ssd_dt_bias_bwd, ssd_a_log_fwd, ssd_a_log_bwd, ssd_d, ssd_norm_w, s5_lambda_re_fwd, s5_lambda_im_fwd, s5_log_step_fwd, s5_lambda_re_bwd, s5_lambda_im_bwd, s5_log_step_bwd, s5_b_re, s5_b_im, s5_c_re_fwd, s5_c_im_fwd, s5_c_re_bwd, s5_c_im_bwd, s5_d, s5_glu_w, s5_glu_b, s5_norm_w, w_out, norm_ffn_w, ffn_w_up, ffn_conv_w, ffn_conv_b, ffn_w_down, norm_final_w, loss_target, m_norm_mix_w, m_w_in, m_ssd_conv_w, m_ssd_conv_b, m_ssd_dt_bias_fwd, m_ssd_dt_bias_bwd, m_ssd_a_log_fwd, m_ssd_a_log_bwd, m_ssd_d, m_ssd_norm_w, m_s5_lambda_re_fwd, m_s5_lambda_im_fwd, m_s5_log_step_fwd, m_s5_lambda_re_bwd, m_s5_lambda_im_bwd, m_s5_log_step_bwd, m_s5_b_re, m_s5_b_im, m_s5_c_re_fwd, m_s5_c_im_fwd, m_s5_c_re_bwd, m_s5_c_im_bwd, m_s5_d, m_s5_glu_w, m_s5_glu_b, m_s5_norm_w, m_w_out, m_norm_ffn_w, m_ffn_w_up, m_ffn_conv_w, m_ffn_conv_b, m_ffn_w_down, m_norm_final_w, v_norm_mix_w, v_w_in, v_ssd_conv_w, v_ssd_conv_b, v_ssd_dt_bias_fwd, v_ssd_dt_bias_bwd, v_ssd_a_log_fwd, v_ssd_a_log_bwd, v_ssd_d, v_ssd_norm_w, v_s5_lambda_re_fwd, v_s5_lambda_im_fwd, v_s5_log_step_fwd, v_s5_lambda_re_bwd, v_s5_lambda_im_bwd, v_s5_log_step_bwd, v_s5_b_re, v_s5_b_im, v_s5_c_re_fwd, v_s5_c_im_fwd, v_s5_c_re_bwd, v_s5_c_im_bwd, v_s5_d, v_s5_glu_w, v_s5_glu_b, v_s5_norm_w, v_w_out, v_norm_ffn_w, v_ffn_w_up, v_ffn_conv_w, v_ffn_conv_b, v_ffn_w_down, v_norm_final_w):
    args = dict(locals())
    w = {k_: args[k_] for k_ in _WEIGHTS}
    m = {k_: args["m_" + k_] for k_ in _WEIGHTS}
    v = {k_: args["v_" + k_] for k_ in _WEIGHTS}
    bl, sl, d = x.shape
    chip = 2 * lax.axis_index("x") + lax.axis_index("y")
    core = lax.axis_index("c")

    first = w["w_in"][0].astype(BF16)
    g_in, g_scw, g_fcw = _gather_xy([first.reshape(2, first.shape[0] // 2, first.shape[1])], [w[k_][0] for k_ in _CONV], "gather_first")
    g_in = g_in.reshape((4,) + first.shape)
    handle = _copies_start(_whole_copies, 4, [w[k_][0].astype(BF16) for k_ in _BIG[1:]], g_scw, "gather_rest_start")

    def cols(a):
        return jnp.moveaxis(a, 0, 1).reshape(a.shape[1], 4 * a.shape[2])

    p = {k_: (w[k_][0] if w[k_].ndim >= 3 else w[k_]) for k_ in _WEIGHTS if k_ not in _BIG + _CONV}
    p["norm_mix_w"] = p["norm_mix_w"] + handle[3][0, 0]
    p["w_in"] = cols(g_in)
    p["ssd_conv_w"], p["ffn_conv_w"] = cols(g_scw), cols(g_fcw)

    def late(after):
        g_out, g_up, g_down = _copies_wait(_whole_copies, 4, handle, after, "gather_rest_wait")
        return g_out.reshape(-1, g_out.shape[2]), cols(g_up), g_down.reshape(-1, g_down.shape[2])

    def owner_major(a, k_):
        r, c = w[k_].shape[1:]
        if a.shape[0] == r:
            a = jnp.moveaxis(a.reshape(r, 4, c), 1, 0)
        else:
            a = a.reshape(4, r, c)
        return a.reshape(4, 2, r // 2, c)

    def early(g_up, g_down):
        parts = [jnp.moveaxis(owner_major(a, k_), 1, 0) for a, k_ in ((g_up, "ffn_w_up"), (g_down, "ffn_w_down"))]
        got = _swap_sibling(parts, True, "reduce_sibling_ffn")
        sums = [_add_half(pt, gt, core, BF16, f"reduce_add_ffn_{i}") for i, (pt, gt) in enumerate(zip(parts, got))]
        return sums, _copies_start(_scatter_copies, 3, sums, w["norm_ffn_w"], "reduce_chips_ffn_start")

    loss, grad_x, g = _local_step(x.reshape(bl * sl, d), loss_target.reshape(bl * sl, d), p, bl, late, early)
    ffn_sums, ffn_handle = g.pop("ffn_reduce")
    ffn_recv = _copies_wait(_scatter_copies, 3, ffn_handle, grad_x, "reduce_chips_ffn_wait")
    ffn_halves = [_sum_slots(rc, sm, chip, f"reduce_sum_ffn_{i}") for i, (rc, sm) in enumerate(zip(ffn_recv, ffn_sums))]

    small = [k_ for k_ in _WEIGHTS if k_ not in _BIG]
    small_full_shapes = [g[k_].shape for k_ in small]
    buf = _pack([g[k_] for k_ in small] + [loss[0, :1]])
    rest = _BIG[:2]
    parts = [jnp.moveaxis(owner_major(g[k_], k_), 1, 0) for k_ in rest]
    parts.append(jnp.moveaxis(buf.reshape(4, 2, -1, LANES), 1, 0))
    got = _swap_sibling(parts, True, "reduce_sibling")
    chip_sums = [_add_half(pt, gt, core, BF16 if i < len(rest) else F32, f"reduce_add_{i}") for i, (pt, gt) in enumerate(zip(parts, got))]
    from_chips = _scatter_xy(chip_sums, "reduce_chips")
    rest_halves = [_sum_lead(a.reshape(4, -1, a.shape[-1]), f"reduce_sum_{i}") for i, a in enumerate(from_chips)]
    halves = rest_halves[:-1] + ffn_halves + rest_halves[-1:]
    other = _swap_sibling(halves[:-1], False, "reduce_join")
    big_grad = {}
    for k_, own_half, sib_half in zip(_BIG, halves, other):
        south = core == 0
        full = jnp.stack([jnp.where(south, own_half, sib_half), jnp.where(south, sib_half, own_half)])
        big_grad[k_] = full.reshape((1,) + w[k_].shape[1:])
    tot = _bcast_all(halves[-1], "reduce_small").reshape(buf.shape)
    unp = _unpack(tot, small_full_shapes + [(1,)])
    small_grad = dict(zip(small, unp[:-1]))
    loss_out = unp[-1].reshape(())
    for k_ in _CONV:
        cshard = w[k_].shape[2]
        small_grad[k_] = lax.dynamic_slice_in_dim(small_grad[k_], chip * cshard, cshard, 1)

    grads, deltas, new_m, new_v = {}, {}, {}, {}
    for k_ in _BIG:
        shp = w[k_].shape
        grads[k_] = big_grad[k_]
        dl, nm, nv = _adamw(w[k_][0], big_grad[k_][0], m[k_][0], v[k_][0], f"adamw_{k_}")
        deltas[k_], new_m[k_], new_v[k_] = dl.reshape(shp), nm.reshape(shp), nv.reshape(shp)
    sw_ = _pack([w[k_] for k_ in small])
    sg_ = _pack([small_grad[k_] for k_ in small])
    sm_ = _pack([m[k_] for k_ in small])
    sv_ = _pack([v[k_] for k_ in small])
    dl, nm, nv = _adamw(sw_, sg_, sm_, sv_, "adamw_small")
    shapes = [w[k_].shape for k_ in small]
    for k_, a, b, c_ in zip(small, _unpack(dl, shapes), _unpack(nm, shapes), _unpack(nv, shapes)):
        deltas[k_], new_m[k_], new_v[k_] = a, b, c_
        grads[k_] = small_grad[k_].reshape(w[k_].shape)
    return (loss_out, grad_x.reshape(bl, sl, d), *[grads[k_] for k_ in _WEIGHTS], *[deltas[k_] for k_ in _WEIGHTS],
            *[new_m[k_] for k_ in _WEIGHTS], *[new_v[k_] for k_ in _WEIGHTS])
```

```python
import functools
import math

import jax
import jax.numpy as jnp
from jax import lax
from jax.experimental import pallas as pl
from jax.experimental.pallas import tpu as pltpu

F32 = jnp.float32
BF16 = jnp.bfloat16
HI = lax.Precision.HIGHEST
SDS = jax.ShapeDtypeStruct
MESH = pl.DeviceIdType.MESH

NN = (((1,), (0,)), ((), ()))
NT = (((1,), (1,)), ((), ()))
TN = (((0,), (0,)), ((), ()))

EPS = 1e-6
SSD_HEADS = 16
SSD_HEAD_DIM = 64
SSD_GROUPS = 4
SSD_STATE = 128
SSD_CHUNK = 128
S5_GROUPS = 32
S5_CH = 16
S5_STATE = 64
S5_T = 16
LANES = 128
ADAM_LR, ADAM_B1, ADAM_B2, ADAM_EPS, ADAM_WD, ADAM_STEP = 0.001, 0.9, 0.999, 1e-08, 0.01, 10
V7X_VMEM_BYTES = 64 * 1024 * 1024
VMEM_LIMIT = V7X_VMEM_BYTES - 8 * 1024 * 1024


def _cp(sem, vmem=None):
    return pltpu.CompilerParams(dimension_semantics=sem, vmem_limit_bytes=vmem)


def _dot(a, b, dims=NN, precision=None):
    return lax.dot_general(a, b, dims, precision=precision, preferred_element_type=F32)


def _rms(x, w):
    return x * lax.rsqrt(jnp.mean(x * x, axis=-1, keepdims=True) + EPS) * w


def _sigmoid(x):
    return 1.0 / (1.0 + jnp.exp(-x))


def _softplus(x):
    return jnp.maximum(x, 0.0) + jnp.log1p(jnp.exp(-jnp.abs(x)))


@functools.partial(jax.custom_vjp, nondiff_argnums=(1, 2))
def _shift(x, k, seg):
    n = x.shape[0]
    r = lax.broadcasted_iota(jnp.int32, x.shape, 0)
    if seg != n:
        r = r & (seg - 1) if seg & (seg - 1) == 0 else r % seg
    y = pltpu.roll(x, k % n, 0)
    ok = (r >= k) if k > 0 else (r < seg + k)
    return jnp.where(ok, y, 0.0)


def _shift_fwd(x, k, seg):
    return _shift(x, k, seg), None


def _shift_bwd(k, seg, _, g):
    return (_shift(g, -k, seg),)


_shift.defvjp(_shift_fwd, _shift_bwd)


@functools.partial(jax.custom_vjp, nondiff_argnums=(1,))
def _lane_shift(x, k):
    if k == 0:
        return x
    n = x.shape[1]
    lane = lax.broadcasted_iota(jnp.int32, x.shape, 1)
    ok = (lane >= k) if k > 0 else (lane < n + k)
    return jnp.where(ok, pltpu.roll(x, k % n, 1), 0.0)


_lane_shift.defvjp(lambda x, k: (_lane_shift(x, k), None), lambda k, _, g: (_lane_shift(g, -k),))


@jax.custom_vjp
def _swap(z):
    return pltpu.roll(z, LANES // 2, 1)


_swap.defvjp(lambda z: (_swap(z), None), lambda _, g: (_swap(g),))


def _norm_matmul(x, nw, ws, tm, name):
    n, d = x.shape
    k = len(ws)

    def body(x_ref, nw_ref, *refs):
        hn = _rms(x_ref[...], nw_ref[...]).astype(BF16)
        refs[k][...] = hn
        for w_ref, o_ref in zip(refs[:k], refs[k + 1:]):
            o_ref[...] = _dot(hn, w_ref[...])

    row = lambda i: (i, 0)
    fix = lambda i: (0, 0)
    return pl.pallas_call(
        body, name=name, grid=(n // tm,),
        in_specs=[pl.BlockSpec((tm, d), row), pl.BlockSpec((1, d), fix)] + [pl.BlockSpec(w.shape, fix) for w in ws],
        out_specs=[pl.BlockSpec((tm, d), row)] + [pl.BlockSpec((tm, w.shape[1]), row) for w in ws],
        out_shape=[SDS((n, d), BF16)] + [SDS((n, w.shape[1]), F32) for w in ws],
        compiler_params=_cp(("arbitrary",), VMEM_LIMIT),
    )(x, nw, *ws)


def _matmul_res(a, w, res, tm, name):
    n, kd = a.shape
    m = w.shape[1]

    def body(a_ref, w_ref, r_ref, o_ref):
        o_ref[...] = r_ref[...] + _dot(a_ref[...], w_ref[...])

    return pl.pallas_call(
        body, name=name, grid=(n // tm,),
        in_specs=[pl.BlockSpec((tm, kd), lambda i: (i, 0)), pl.BlockSpec((kd, m), lambda i: (0, 0)),
                  pl.BlockSpec((tm, m), lambda i: (i, 0))],
        out_specs=pl.BlockSpec((tm, m), lambda i: (i, 0)),
        out_shape=SDS((n, m), F32),
        compiler_params=_cp(("arbitrary",), VMEM_LIMIT),
    )(a, w, res)


def _matmul_nt(gs, ws, tm, name):
    n = gs[0].shape[0]
    kd = ws[0].shape[0]
    cnt = len(gs)

    def body(*refs):
        acc = None
        for g_ref, w_ref in zip(refs[:cnt], refs[cnt:2 * cnt]):
            t = _dot(g_ref[...].astype(BF16), w_ref[...], NT)
            acc = t if acc is None else acc + t
        refs[2 * cnt][...] = acc

    return pl.pallas_call(
        body, name=name, grid=(n // tm,),
        in_specs=[pl.BlockSpec((tm, g.shape[1]), lambda i: (i, 0)) for g in gs]
        + [pl.BlockSpec(w.shape, lambda i: (0, 0)) for w in ws],
        out_specs=pl.BlockSpec((tm, kd), lambda i: (i, 0)),
        out_shape=SDS((n, kd), F32),
        compiler_params=_cp(("arbitrary",), VMEM_LIMIT),
    )(*gs, *ws)


def _matmul_tn(a, g, tm, cb, name):
    n, kd = a.shape
    m = g.shape[1]

    def body(a_ref, g_ref, o_ref):
        t = _dot(a_ref[...], g_ref[...].astype(BF16), TN)

        @pl.when(pl.program_id(1) == 0)
        def _():
            o_ref[...] = t

        @pl.when(pl.program_id(1) != 0)
        def _():
            o_ref[...] += t

    return pl.pallas_call(
        body, name=name, grid=(m // cb, n // tm),
        in_specs=[pl.BlockSpec((tm, kd), lambda j, i: (i, 0)), pl.BlockSpec((tm, cb), lambda j, i: (i, j))],
        out_specs=pl.BlockSpec((kd, cb), lambda j, i: (0, j)),
        out_shape=SDS((kd, m), F32),
        compiler_params=_cp(("arbitrary", "arbitrary"), VMEM_LIMIT),
    )(a, g)


def _dwconv(x, w, b):
    kw = w.shape[0]
    acc = b
    for k in range(kw):
        acc = acc + w[k:k + 1, :] * _shift(x, kw // 2 - k, x.shape[0])
    return acc


def _conv_silu_fn(x, w, b):
    y = _dwconv(x, w, b)
    return y * _sigmoid(y)


def _conv_glu_fn(v, g, wv, wg, bv, bg):
    cv = _dwconv(v, wv, bv)
    cg = _dwconv(g, wg, bg)
    return cg * _sigmoid(cg) * cv


def _conv_silu(x, w, b, bl, cb, name):
    n, c = x.shape
    sl = n // bl
    kw = w.shape[0]

    def body(x_ref, w_ref, b_ref, o_ref):
        o_ref[...] = _conv_silu_fn(x_ref[...], w_ref[...], b_ref[...])

    return pl.pallas_call(
        body, name=name, grid=(bl, c // cb),
        in_specs=[pl.BlockSpec((sl, cb), lambda s, j: (s, j)), pl.BlockSpec((kw, cb), lambda s, j: (0, j)),
                  pl.BlockSpec((1, cb), lambda s, j: (0, j))],
        out_specs=pl.BlockSpec((sl, cb), lambda s, j: (s, j)),
        out_shape=SDS((n, c), F32),
        compiler_params=_cp(("arbitrary", "arbitrary"), VMEM_LIMIT),
    )(x, w, b)


def _conv_silu_bwd(x, w, b, dys, bl, cb, name):
    n, c = x.shape
    sl = n // bl
    kw = w.shape[0]
    cnt = len(dys)

    def body(x_ref, w_ref, b_ref, *refs):
        dy = refs[0][...]
        for r in refs[1:cnt]:
            dy = dy + r[...]
        dx_ref, dw_ref, db_ref = refs[cnt:]
        _, vjp = jax.vjp(_conv_silu_fn, x_ref[...], w_ref[...], b_ref[...])
        dx, dw, db = vjp(dy)
        dx_ref[...] = dx.astype(BF16)

        @pl.when(pl.program_id(1) == 0)
        def _():
            dw_ref[...] = dw
            db_ref[...] = db

        @pl.when(pl.program_id(1) != 0)
        def _():
            dw_ref[...] += dw
            db_ref[...] += db

    dy_specs = []
    for arr, lead in dys:
        if lead is None:
            dy_specs.append(pl.BlockSpec((sl, cb), lambda j, s: (s, j)))
        else:
            dy_specs.append(pl.BlockSpec((None, sl, cb), functools.partial(lambda j, s, lead: (lead, s, j), lead=lead)))
    return pl.pallas_call(
        body, name=name, grid=(c // cb, bl),
        in_specs=[pl.BlockSpec((sl, cb), lambda j, s: (s, j)), pl.BlockSpec((kw, cb), lambda j, s: (0, j)),
                  pl.BlockSpec((1, cb), lambda j, s: (0, j))] + dy_specs,
        out_specs=[pl.BlockSpec((sl, cb), lambda j, s: (s, j)), pl.BlockSpec((kw, cb), lambda j, s: (0, j)),
                   pl.BlockSpec((1, cb), lambda j, s: (0, j))],
        out_shape=[SDS((n, c), BF16), SDS((kw, c), F32), SDS((1, c), F32)],
        compiler_params=_cp(("arbitrary", "arbitrary"), VMEM_LIMIT),
    )(x, w, b, *[a for a, _ in dys])


def _conv_glu(v, g, wv, wg, bv, bg, bl, cb, name):
    n, c = v.shape
    sl = n // bl
    kw = wv.shape[0]

    def body(v_ref, g_ref, wv_ref, wg_ref, bv_ref, bg_ref, o_ref):
        o_ref[...] = _conv_glu_fn(v_ref[...], g_ref[...], wv_ref[...], wg_ref[...], bv_ref[...], bg_ref[...]).astype(BF16)

    big = pl.BlockSpec((sl, cb), lambda s, j: (s, j))
    wsp = pl.BlockSpec((kw, cb), lambda s, j: (0, j))
    bsp = pl.BlockSpec((1, cb), lambda s, j: (0, j))
    return pl.pallas_call(
        body, name=name, grid=(bl, c // cb),
        in_specs=[big, big, wsp, wsp, bsp, bsp], out_specs=big, out_shape=SDS((n, c), BF16),
        compiler_params=_cp(("arbitrary", "arbitrary"), VMEM_LIMIT),
    )(v, g, wv, wg, bv, bg)


def _conv_glu_bwd(v, g, wv, wg, bv, bg, dact, bl, cb, name):
    n, c = v.shape
    sl = n // bl
    kw = wv.shape[0]

    def body(v_ref, g_ref, wv_ref, wg_ref, bv_ref, bg_ref, da_ref, dv_ref, dg_ref, dwv_ref, dwg_ref, dbv_ref, dbg_ref):
        _, vjp = jax.vjp(_conv_glu_fn, v_ref[...], g_ref[...], wv_ref[...], wg_ref[...], bv_ref[...], bg_ref[...])
        dv, dg, dwv, dwg, dbv, dbg = vjp(da_ref[...])
        dv_ref[...] = dv.astype(BF16)
        dg_ref[...] = dg.astype(BF16)

        @pl.when(pl.program_id(1) == 0)
        def _():
            dwv_ref[...] = dwv
            dwg_ref[...] = dwg
            dbv_ref[...] = dbv
            dbg_ref[...] = dbg

        @pl.when(pl.program_id(1) != 0)
        def _():
            dwv_ref[...] += dwv
            dwg_ref[...] += dwg
            dbv_ref[...] += dbv
            dbg_ref[...] += dbg

    big = pl.BlockSpec((sl, cb), lambda j, s: (s, j))
    wsp = pl.BlockSpec((kw, cb), lambda j, s: (0, j))
    bsp = pl.BlockSpec((1, cb), lambda j, s: (0, j))
    return pl.pallas_call(
        body, name=name, grid=(c // cb, bl),
        in_specs=[big, big, wsp, wsp, bsp, bsp, big],
        out_specs=[big, big, wsp, wsp, bsp, bsp],
        out_shape=[SDS((n, c), BF16), SDS((n, c), BF16), SDS((kw, c), F32), SDS((kw, c), F32), SDS((1, c), F32), SDS((1, c), F32)],
        compiler_params=_cp(("arbitrary", "arbitrary"), VMEM_LIMIT),
    )(v, g, wv, wg, bv, bg, dact)


_DIMS_T = {NN: (NT, TN, False, False), NT: (NN, TN, False, True), TN: (NT, NN, True, False)}


@functools.partial(jax.custom_vjp, nondiff_argnums=(2,))
def _bdot(a, b, dims):
    return _dot(a.astype(BF16), b.astype(BF16), dims)


def _bdot_fwd(a, b, dims):
    return _bdot(a, b, dims), (a, b)


def _bdot_bwd(dims, res, g):
    a, b = res
    da_dims, db_dims, a_swapped, b_swapped = _DIMS_T[dims]
    da = _bdot(b, g, da_dims) if a_swapped else _bdot(g, b, da_dims)
    db = _bdot(g, a, db_dims) if b_swapped else _bdot(a, g, db_dims)
    return da, db


_bdot.defvjp(_bdot_fwd, _bdot_bwd)


@functools.partial(jax.custom_vjp, nondiff_argnums=(1,))
def _expand_heads(v, width):
    return _split_dot(v, _head_matrix(width), NN)


def _head_matrix(width):
    hr = lax.broadcasted_iota(jnp.int32, (LANES, SSD_HEADS * width), 0)
    hc = lax.broadcasted_iota(jnp.int32, (LANES, SSD_HEADS * width), 1)
    return (hc // width == hr).astype(BF16)


def _split_dot(v, e, dims):
    hi = v.astype(BF16)
    lo = (v - hi.astype(F32)).astype(BF16)
    return _dot(hi, e, dims) + _dot(lo, e, dims)


_expand_heads.defvjp(lambda v, width: (_expand_heads(v, width), None),
                     lambda width, _, g: (_split_dot(g, _head_matrix(width), NT),))


def _ssd_chunk_fn(rev, xs, dtr, bms, cms, st, alog, dtb):
    q = dtr.shape[0]
    hd, per = SSD_HEAD_DIM, SSD_HEADS // SSD_GROUPS
    gw = per * hd
    r = lax.broadcasted_iota(jnp.int32, (q, q), 0)
    c = lax.broadcasted_iota(jnp.int32, (q, q), 1)
    sgn = 1 - 2 * rev
    tri = ((r - c) * sgn >= 0).astype(F32)
    tri_t = ((c - r) * sgn >= 0).astype(F32)
    r4 = lax.broadcasted_iota(jnp.int32, (q, per * q), 0)
    c4 = lax.broadcasted_iota(jnp.int32, (q, per * q), 1) % q
    mask4 = (r4 - c4) * sgn >= 0
    bdr = lax.broadcasted_iota(jnp.int32, (per * q, gw), 0) // q
    bdc = lax.broadcasted_iota(jnp.int32, (per * q, gw), 1) // hd
    diag = bdr == bdc
    dt = _softplus(dtr + dtb)
    dta = dt * (-jnp.exp(alog))
    cs = _dot(tri, dta, NN, HI)
    cs_t = _dot(dta, tri_t, TN, HI)
    tot = jnp.sum(dta, axis=0, keepdims=True)
    dt_x = _expand_heads(dt, hd)
    in_x = _expand_heads(jnp.exp(cs), hd)
    out_x = _expand_heads(jnp.exp(tot - cs), hd)
    ys, outs = [], []
    for g in range(SSD_GROUPS):
        bg, cg = bms[g], cms[g]
        heads = range(per * g, per * (g + 1))
        lanes = slice(gw * g, gw * (g + 1))
        scores = _bdot(cg, bg, NT)
        col = jnp.concatenate([jnp.broadcast_to(cs[:, h:h + 1], (q, q)) for h in heads], axis=1)
        row = jnp.concatenate([cs_t[h:h + 1, :] for h in heads], axis=1)
        seg = jnp.where(mask4, jnp.exp(jnp.where(mask4, col - row, 0.0)), 0.0)
        mcat = jnp.concatenate([scores] * per, axis=1) * seg
        xdt = xs[g] * dt_x[:, lanes]
        blocks = jnp.where(diag, jnp.concatenate([xdt] * per, axis=0), 0.0)
        y = _bdot(mcat, blocks, NN) + in_x[:, lanes] * _bdot(cg, st[g], NT)
        new = _bdot(xdt * out_x[:, lanes], bg, TN)
        keep = jnp.concatenate([jnp.exp(tot[:, h:h + 1]) * st[g][hd * j:hd * (j + 1), :] for j, h in enumerate(heads)], axis=0)
        ys.append(y)
        outs.append(keep + new)
    return ys, outs


def _ssd_scan(xs, bm, cm, dtr, alog2, dtb2, bl, name):
    n = xs.shape[0]
    q = SSD_CHUNK
    nc = n // bl // q
    hd, ns = SSD_HEAD_DIM, SSD_STATE
    gw = SSD_HEADS // SSD_GROUPS * hd

    def body(xs_ref, b_ref, c_ref, dt_ref, al_ref, db_ref, y_ref, sv_ref, st_ref):
        d, i = pl.program_id(0), pl.program_id(2)

        @pl.when(i == 0)
        def _():
            st_ref[...] = jnp.zeros(st_ref.shape, F32)

        st = [st_ref[gw * g:gw * (g + 1), :] for g in range(SSD_GROUPS)]
        sv_ref[...] = st_ref[...]
        xl = [xs_ref[:, gw * g:gw * (g + 1)] for g in range(SSD_GROUPS)]
        bms = [b_ref[:, ns * g:ns * (g + 1)] for g in range(SSD_GROUPS)]
        cms = [c_ref[:, ns * g:ns * (g + 1)] for g in range(SSD_GROUPS)]
        ys, outs = _ssd_chunk_fn(d, xl, dt_ref[...], bms, cms, st, al_ref[...], db_ref[...])
        for g in range(SSD_GROUPS):
            st_ref[gw * g:gw * (g + 1), :] = outs[g]
            y_ref[:, gw * g:gw * (g + 1)] = ys[g]

    def rowblk(d, s, i):
        return s * nc + i + d * (nc - 1 - 2 * i)

    return pl.pallas_call(
        body, name=name, grid=(2, bl, nc),
        in_specs=[pl.BlockSpec((q, SSD_HEADS * hd), lambda d, s, i: (rowblk(d, s, i), 0)),
                  pl.BlockSpec((q, SSD_GROUPS * ns), lambda d, s, i: (rowblk(d, s, i), 0)),
                  pl.BlockSpec((q, SSD_GROUPS * ns), lambda d, s, i: (rowblk(d, s, i), 0)),
                  pl.BlockSpec((q, LANES), lambda d, s, i: (rowblk(d, s, i), d)),
                  pl.BlockSpec((None, 1, LANES), lambda d, s, i: (d, 0, 0)),
                  pl.BlockSpec((None, 1, LANES), lambda d, s, i: (d, 0, 0))],
        out_specs=[pl.BlockSpec((None, q, SSD_HEADS * hd), lambda d, s, i: (d, rowblk(d, s, i), 0)),
                   pl.BlockSpec((None, None, SSD_HEADS * hd, ns), lambda d, s, i: (d, rowblk(d, s, i), 0, 0))],
        out_shape=[SDS((2, n, SSD_HEADS * hd), F32), SDS((2, n // q, SSD_HEADS * hd, ns), F32)],
        scratch_shapes=[pltpu.VMEM((SSD_HEADS * hd, ns), F32)],
        compiler_params=_cp(("arbitrary",) * 3, VMEM_LIMIT),
    )(xs, bm, cm, dtr, alog2, dtb2)


def _ssd_scan_bwd(xs, bm, cm, dtr, alog2, dtb2, saved, dy, bl, name):
    n = xs.shape[0]
    q = SSD_CHUNK
    nc = n // bl // q
    hd, ns = SSD_HEAD_DIM, SSD_STATE
    gw = SSD_HEADS // SSD_GROUPS * hd

    def body(xs_ref, b_ref, c_ref, dt_ref, al_ref, db_ref, sv_ref, dy_ref,
             dxs_ref, dbm_ref, dcm_ref, ddt_ref, dal_ref, ddb_ref, ds_ref):
        d, s, i = pl.program_id(0), pl.program_id(1), pl.program_id(2)

        @pl.when(i == 0)
        def _():
            ds_ref[...] = jnp.zeros(ds_ref.shape, F32)

        xl = [xs_ref[:, gw * g:gw * (g + 1)] for g in range(SSD_GROUPS)]
        bms = [b_ref[:, ns * g:ns * (g + 1)] for g in range(SSD_GROUPS)]
        cms = [c_ref[:, ns * g:ns * (g + 1)] for g in range(SSD_GROUPS)]
        st = [sv_ref[gw * g:gw * (g + 1), :] for g in range(SSD_GROUPS)]
        fn = functools.partial(_ssd_chunk_fn, d)
        _, vjp = jax.vjp(fn, xl, dt_ref[...], bms, cms, st, al_ref[...], db_ref[...])
        dys = [dy_ref[:, gw * g:gw * (g + 1)] for g in range(SSD_GROUPS)]
        dso = [ds_ref[gw * g:gw * (g + 1), :] for g in range(SSD_GROUPS)]
        dxl, ddt, dbg, dcg, dst, dal, ddb = vjp((dys, dso))
        for g in range(SSD_GROUPS):
            ds_ref[gw * g:gw * (g + 1), :] = dst[g]
            dxs_ref[:, gw * g:gw * (g + 1)] = dxl[g]
            dbm_ref[:, ns * g:ns * (g + 1)] = dbg[g]
            dcm_ref[:, ns * g:ns * (g + 1)] = dcg[g]
        ddt_ref[...] = ddt
        _acc_rows((dal_ref, ddb_ref), (dal, ddb), jnp.logical_and(s == 0, i == 0))

    def rowblk(d, s, i):
        return s * nc + (nc - 1 - i) + d * (2 * i - (nc - 1))

    row = lambda d, s, i: (rowblk(d, s, i), 0)
    drow = lambda d, s, i: (d, rowblk(d, s, i), 0)
    dfix = lambda d, s, i: (d, 0, 0)
    dcol = lambda d, s, i: (rowblk(d, s, i), d)
    return pl.pallas_call(
        body, name=name, grid=(2, bl, nc),
        in_specs=[pl.BlockSpec((q, SSD_HEADS * hd), row), pl.BlockSpec((q, SSD_GROUPS * ns), row),
                  pl.BlockSpec((q, SSD_GROUPS * ns), row), pl.BlockSpec((q, LANES), dcol),
                  pl.BlockSpec((None, 1, LANES), dfix), pl.BlockSpec((None, 1, LANES), dfix),
                  pl.BlockSpec((None, None, SSD_HEADS * hd, ns), lambda d, s, i: (d, rowblk(d, s, i), 0, 0)),
                  pl.BlockSpec((q, SSD_HEADS * hd), row)],
        out_specs=[pl.BlockSpec((None, q, SSD_HEADS * hd), drow), pl.BlockSpec((None, q, SSD_GROUPS * ns), drow),
                   pl.BlockSpec((None, q, SSD_GROUPS * ns), drow), pl.BlockSpec((q, LANES), dcol),
                   pl.BlockSpec((None, 1, LANES), dfix), pl.BlockSpec((None, 1, LANES), dfix)],
        out_shape=[SDS((2, n, SSD_HEADS * hd), F32), SDS((2, n, SSD_GROUPS * ns), F32), SDS((2, n, SSD_GROUPS * ns), F32),
                   SDS((n, 2 * LANES), F32), SDS((2, 1, LANES), F32), SDS((2, 1, LANES), F32)],
        scratch_shapes=[pltpu.VMEM((SSD_HEADS * hd, ns), F32)],
        compiler_params=_cp(("arbitrary",) * 3, VMEM_LIMIT),
    )(xs, bm, cm, dtr, alog2, dtb2, saved, dy)


def _s5_consts():
    t, ch, p = S5_T, S5_CH, S5_STATE
    lane = lax.broadcasted_iota(jnp.int32, (1, 2 * p), 1)
    pr = lax.broadcasted_iota(jnp.int32, (p, 2 * p), 0)
    pc = lax.broadcasted_iota(jnp.int32, (p, 2 * p), 1)
    cr = lax.broadcasted_iota(jnp.int32, (ch, t * ch), 0)
    cc = lax.broadcasted_iota(jnp.int32, (ch, t * ch), 1)
    return dict(
        left=lane < p,
        sg=jnp.where(lane < p, -1.0, 1.0).astype(F32),
        dup=(pc % p == pr).astype(F32),
        dup_l=(pc == pr).astype(F32),
        dup_r=(pc == pr + p).astype(F32),
        rep=(cc % ch == cr).astype(F32),
        rep0=(cc == cr).astype(F32),
    )


def _s5_mats(k, rev, lr, li, ls, bre, bim, cre, cim):
    t = S5_T
    step = jnp.exp(ls)
    lr2 = jnp.sum(lr * k["dup"], axis=0, keepdims=True)
    li2 = jnp.sum(li * k["dup"], axis=0, keepdims=True)

    def erow(d):
        ang = (d * step) * li2
        return jnp.exp((d * step) * lr2) * jnp.where(k["left"], jnp.cos(ang), jnp.sin(ang))

    es = [erow(d) for d in range(t + 1)]
    mag = jnp.exp(step * lr)
    ar, ai = mag * jnp.cos(step * li), mag * jnp.sin(step * li)
    den = lr * lr + li * li
    zr = ((ar - 1.0) * lr + ai * li) / den
    zi = (ai * lr - (ar - 1.0) * li) / den
    bbr = zr * bre - zi * bim
    bbi = zr * bim + zi * bre
    bt1 = _dot(bbr, k["dup"], TN, HI)
    bt2 = _dot(bbi, k["dup"], TN, HI)
    bst = _dot(bbr, k["dup_l"], TN, HI) - _dot(bbi, k["dup_r"], TN, HI)
    c1 = _dot(cre, k["dup"], NN, HI)
    c2 = _dot(cim, k["dup"], NN, HI)
    sg = k["sg"]
    ce = [e * c1 + sg * _swap(e) * c2 for e in es]
    lags = range(t - 1, -1, -1) if rev else range(t)
    kt = _dot(bst, jnp.concatenate([ce[d] for d in lags], axis=0), NT, HI)
    toep = jnp.concatenate([_lane_shift(kt, -S5_CH * (t - 1 - s) if rev else S5_CH * s) for s in range(t)], axis=0)
    w_out =jnp.concatenate([ce[(t - qq) if rev else (qq + 1)] * (-sg) for qq in range(t)], axis=0)
    w_st = jnp.concatenate(
        [(lambda e: e * bt1 + sg * _swap(e) * bt2)(es[s if rev else (t - 1 - s)]) for s in range(t)], axis=0)
    return toep, w_out, w_st, es[t]


def _cmul_row(k, e, z):
    es = _swap(e)
    return z * jnp.where(k["left"], e, es) + _swap(z) * (k["sg"] * jnp.where(k["left"], es, e))


def _s5_dir(k, rev, nck, x, mats):
    toep, w_out, w_st, a_t = mats
    acc = _dot(x, w_st)
    e = a_t
    kk = 1
    sign = -1 if rev else 1
    while kk < nck:
        acc = acc + _cmul_row(k, e, _shift(acc, sign * kk, nck))
        e = _cmul_row(k, e, e)
        kk *= 2
    prev = _shift(acc, sign, nck)
    return _dot(x, toep) + _dot(prev, w_out, NT)


def _s5_group_fn(nck, x, pf, pb, bre, bim, dcol, wv, wg, bv, bg):
    k = _s5_consts()
    t = S5_T
    y = x * jnp.sum(dcol * k["rep"], axis=0, keepdims=True)
    for rev, (lr, li, ls, cre, cim) in ((False, pf), (True, pb)):
        y = y + _s5_dir(k, rev, nck, x, _s5_mats(k, rev, lr, li, ls, bre, bim, cre, cim))
    gy = jax.nn.gelu(y)
    def kron_eye(w16):
        wide = _dot(w16, k["rep0"], NN, HI)
        return jnp.concatenate([_lane_shift(wide, S5_CH * qq) for qq in range(t)], axis=0)

    kv, kg = kron_eye(wv), kron_eye(wg)
    val =_dot(gy, kv) + jnp.sum(bv * k["rep"], axis=0, keepdims=True)
    gate = _dot(gy, kg) + jnp.sum(bg * k["rep"], axis=0, keepdims=True)
    return val * _sigmoid(gate)


def _s5_specs(r):
    p, ch = S5_STATE, S5_CH
    g3 = lambda i: (i, 0, 0)
    col = pl.BlockSpec((None, p, 1), g3)
    one = pl.BlockSpec((None, 1, 1), g3)
    cmat = pl.BlockSpec((None, ch, p), g3)
    bmat = pl.BlockSpec((None, p, ch), g3)
    ccol = pl.BlockSpec((None, ch, 1), g3)
    sq = pl.BlockSpec((None, ch, ch), g3)
    xs = pl.BlockSpec((None, r, S5_T * ch), g3)
    specs = [xs, col, col, one, cmat, cmat, col, col, one, cmat, cmat, bmat, bmat, ccol, sq, sq, ccol, ccol]
    return specs


def _s5_unpack(vals):
    x = vals[0]
    pf = tuple(vals[1:6])
    pb = tuple(vals[6:11])
    bre, bim, dcol, wv, wg, bv, bg = vals[11:18]
    return x, pf, pb, bre, bim, dcol, wv, wg, bv, bg


def _s5_fwd(args, nck, name):
    x = args[0]
    ng, r, w = x.shape

    def body(*refs):
        vals = [ref[...] for ref in refs[:18]]
        refs[18][...] = _s5_group_fn(nck, *_s5_unpack(vals))

    specs = _s5_specs(r)
    return pl.pallas_call(
        body, name=name, grid=(ng,), in_specs=specs, out_specs=specs[0], out_shape=SDS(x.shape, F32),
        compiler_params=_cp(("arbitrary",), VMEM_LIMIT),
    )(*args)


def _s5_bwd(args, dy, nck, name):
    x = args[0]
    ng, r, w = x.shape

    def body(*refs):
        vals = [ref[...] for ref in refs[:18]]
        _, vjp = jax.vjp(lambda *v: _s5_group_fn(nck, *_s5_unpack(v)), *vals)
        grads = vjp(refs[18][...])
        for o_ref, gval in zip(refs[19:], grads):
            o_ref[...] = gval.astype(o_ref.dtype)

    specs = _s5_specs(r)
    return pl.pallas_call(
        body, name=name, grid=(ng,), in_specs=specs + [specs[0]], out_specs=specs,
        out_shape=[SDS(x.shape, BF16)] + [SDS(a.shape, F32) for a in args[1:]],
        compiler_params=_cp(("arbitrary",), VMEM_LIMIT),
    )(*args, dy)


def _mix_fn(yf, yb, xs, z, s5o, dvec, nw_ssd, nw_s5):
    hr = lax.broadcasted_iota(jnp.int32, (LANES, SSD_HEADS * SSD_HEAD_DIM), 0)
    hc = lax.broadcasted_iota(jnp.int32, (LANES, SSD_HEADS * SSD_HEAD_DIM), 1)
    expand = (hc // SSD_HEAD_DIM == hr).astype(F32)
    dch = jnp.sum(dvec * expand, axis=0, keepdims=True)
    y = (yf + yb + dch * xs) * (z * _sigmoid(z))
    return _rms(y, nw_ssd), _rms(s5o, nw_s5)


def _mix(y2, xs, z, s5o, dvec, nw_ssd, nw_s5, tm, name):
    n, c1 = xs.shape
    c2 = s5o.shape[1]

    def body(yf_ref, yb_ref, xs_ref, z_ref, s_ref, d_ref, n1_ref, n2_ref, o_ref):
        o1, o2 = _mix_fn(yf_ref[...], yb_ref[...], xs_ref[...], z_ref[...], s_ref[...], d_ref[...], n1_ref[...], n2_ref[...])
        o_ref[:, :c1] = o1.astype(BF16)
        o_ref[:, c1:] = o2.astype(BF16)

    row = lambda i: (i, 0)
    fix = lambda i: (0, 0)
    return pl.pallas_call(
        body, name=name, grid=(n // tm,),
        in_specs=[pl.BlockSpec((None, tm, c1), lambda i: (0, i, 0)), pl.BlockSpec((None, tm, c1), lambda i: (1, i, 0)),
                  pl.BlockSpec((tm, c1), row), pl.BlockSpec((tm, c1), row), pl.BlockSpec((tm, c2), row),
                  pl.BlockSpec((LANES, 1), fix), pl.BlockSpec((1, c1), fix), pl.BlockSpec((1, c2), fix)],
        out_specs=pl.BlockSpec((tm, c1 + c2), row), out_shape=SDS((n, c1 + c2), BF16),
        compiler_params=_cp(("arbitrary",), VMEM_LIMIT),
    )(y2, y2, xs, z, s5o, dvec, nw_ssd, nw_s5)


def _acc_rows(refs, vals, first):
    @pl.when(first)
    def _():
        for ref, v in zip(refs, vals):
            ref[...] = v

    @pl.when(jnp.logical_not(first))
    def _():
        for ref, v in zip(refs, vals):
            ref[...] += v


def _mix_bwd(y2, xs, z, s5o, dvec, nw_ssd, nw_s5, dmix, tm, name):
    n, c1 = xs.shape
    c2 = s5o.shape[1]

    def body(yf_ref, yb_ref, xs_ref, z_ref, s_ref, d_ref, n1_ref, n2_ref, dm_ref,
             dy_ref, dxs_ref, dz_ref, ds_ref, dd_ref, dn1_ref, dn2_ref):
        _, vjp = jax.vjp(_mix_fn, yf_ref[...], yb_ref[...], xs_ref[...], z_ref[...], s_ref[...], d_ref[...], n1_ref[...], n2_ref[...])
        dyf, _, dxs, dz, ds, dd, dn1, dn2 = vjp((dm_ref[:, :c1], dm_ref[:, c1:]))
        dy_ref[...] = dyf
        dxs_ref[...] = dxs
        dz_ref[...] = dz.astype(BF16)
        ds_ref[...] = ds
        _acc_rows((dd_ref, dn1_ref, dn2_ref), (dd, dn1, dn2), pl.program_id(0) == 0)

    row = lambda i: (i, 0)
    fix = lambda i: (0, 0)
    return pl.pallas_call(
        body, name=name, grid=(n // tm,),
        in_specs=[pl.BlockSpec((None, tm, c1), lambda i: (0, i, 0)), pl.BlockSpec((None, tm, c1), lambda i: (1, i, 0)),
                  pl.BlockSpec((tm, c1), row), pl.BlockSpec((tm, c1), row), pl.BlockSpec((tm, c2), row),
                  pl.BlockSpec((LANES, 1), fix), pl.BlockSpec((1, c1), fix), pl.BlockSpec((1, c2), fix),
                  pl.BlockSpec((tm, c1 + c2), row)],
        out_specs=[pl.BlockSpec((tm, c1), row), pl.BlockSpec((tm, c1), row), pl.BlockSpec((tm, c1), row), pl.BlockSpec((tm, c2), row),
                   pl.BlockSpec((LANES, 1), fix), pl.BlockSpec((1, c1), fix), pl.BlockSpec((1, c2), fix)],
        out_shape=[SDS((n, c1), F32), SDS((n, c1), F32), SDS((n, c1), BF16), SDS((n, c2), F32),
                   SDS((LANES, 1), F32), SDS((1, c1), F32), SDS((1, c2), F32)],
        compiler_params=_cp(("arbitrary",), VMEM_LIMIT),
    )(y2, y2, xs, z, s5o, dvec, nw_ssd, nw_s5, dmix)


def _final_loss(h2, nw, tgt, tm, name):
    n, d = h2.shape

    def loss_fn(h, w, t):
        e = _rms(h, w) - t
        return (0.5 / d) * jnp.sum(e * e)

    def body(h_ref, w_ref, t_ref, l_ref, dh_ref, dw_ref):
        loss, (dh, dw) = jax.value_and_grad(loss_fn, argnums=(0, 1))(h_ref[...], w_ref[...], t_ref[...])
        dh_ref[...] = dh
        _acc_rows((l_ref, dw_ref), (jnp.full((1, LANES), loss, F32), dw), pl.program_id(0) == 0)

    row = lambda i: (i, 0)
    fix = lambda i: (0, 0)
    return pl.pallas_call(
        body, name=name, grid=(n // tm,),
        in_specs=[pl.BlockSpec((tm, d), row), pl.BlockSpec((1, d), fix), pl.BlockSpec((tm, d), row)],
        out_specs=[pl.BlockSpec((1, LANES), fix), pl.BlockSpec((tm, d), row), pl.BlockSpec((1, d), fix)],
        out_shape=[SDS((1, LANES), F32), SDS((n, d), F32), SDS((1, d), F32)],
        compiler_params=_cp(("arbitrary",), VMEM_LIMIT),
    )(h2, nw, tgt)


def _norm_bwd(x, nw, dhn, dres, tm, name):
    n, d = x.shape

    def body(x_ref, w_ref, g_ref, r_ref, dx_ref, dw_ref):
        _, vjp = jax.vjp(_rms, x_ref[...], w_ref[...])
        dx, dw = vjp(g_ref[...])
        dx_ref[...] = r_ref[...] + dx
        _acc_rows((dw_ref,), (dw,), pl.program_id(0) == 0)

    row = lambda i: (i, 0)
    fix = lambda i: (0, 0)
    return pl.pallas_call(
        body, name=name, grid=(n // tm,),
        in_specs=[pl.BlockSpec((tm, d), row), pl.BlockSpec((1, d), fix), pl.BlockSpec((tm, d), row), pl.BlockSpec((tm, d), row)],
        out_specs=[pl.BlockSpec((tm, d), row), pl.BlockSpec((1, d), fix)],
        out_shape=[SDS((n, d), F32), SDS((1, d), F32)],
        compiler_params=_cp(("arbitrary",), VMEM_LIMIT),
    )(x, nw, dhn, dres)


def _row_tile(n, cap=512):
    for t in range(min(cap, n) // 8 * 8, 7, -8):
        if n % t == 0:
            return t
    return n


def _sum_lead(a, name):
    kk, n, c = a.shape
    tm = _row_tile(n)

    def body(a_ref, o_ref):
        acc = a_ref[0].astype(F32)
        for i in range(1, kk):
            acc = acc + a_ref[i].astype(F32)
        o_ref[...] = acc

    return pl.pallas_call(
        body, name=name, grid=(n // tm,),
        in_specs=[pl.BlockSpec((kk, tm, c), lambda i: (0, i, 0))],
        out_specs=pl.BlockSpec((tm, c), lambda i: (i, 0)), out_shape=SDS((n, c), F32),
        compiler_params=_cp(("arbitrary",), VMEM_LIMIT),
    )(a)


def _adamw(w, g, m, v, name):
    n, c = w.shape
    tm = _row_tile(n)

    def body(w_ref, g_ref, m_ref, v_ref, d_ref, nm_ref, nv_ref):
        gv = g_ref[...]
        mn = ADAM_B1 * m_ref[...] + (1.0 - ADAM_B1) * gv
        vn = ADAM_B2 * v_ref[...] + (1.0 - ADAM_B2) * jnp.square(gv)
        m_hat = mn / (1.0 - ADAM_B1 ** ADAM_STEP)
        v_hat = vn / (1.0 - ADAM_B2 ** ADAM_STEP)
        d_ref[...] = -ADAM_LR * (m_hat / (jnp.sqrt(v_hat) + ADAM_EPS) + ADAM_WD * w_ref[...])
        nm_ref[...] = mn
        nv_ref[...] = vn

    spec = pl.BlockSpec((tm, c), lambda i: (i, 0))
    return pl.pallas_call(
        body, name=name, grid=(n // tm,), in_specs=[spec] * 4, out_specs=[spec] * 3,
        out_shape=[SDS((n, c), F32)] * 3, compiler_params=_cp(("arbitrary",), VMEM_LIMIT),
    )(w, g, m, v)


ANY = pl.BlockSpec(memory_space=pl.ANY)


def _me():
    return lax.axis_index("x"), lax.axis_index("y"), lax.axis_index("c")


def _gather_xy(split, whole, name):
    ns, cnt = len(split), len(split) + len(whole)

    def body(*refs):
        src, dst = refs[:cnt], refs[cnt:2 * cnt]
        send, recv = refs[2 * cnt:]
        x, y, c = _me()
        mine = 2 * x + y
        chips = [(1 - x, y), (x, 1 - y), (1 - x, 1 - y)]

        def ici(a, j, slot):
            px, py = chips[j]
            if a < ns:
                s_ref, d_ref = src[a].at[c], dst[a].at[slot].at[c]
            else:
                s_ref, d_ref = src[a], dst[a].at[slot]
            return pltpu.make_async_remote_copy(s_ref, d_ref, send.at[3 * a + j], recv.at[3 * a + j],
                                                device_id=(px, py, c), device_id_type=MESH)

        def d2d(a, j, half):
            px, py = chips[j]
            ref = dst[a].at[2 * px + py].at[half]
            return pltpu.make_async_remote_copy(ref, ref, send.at[3 * cnt + 3 * a + j], recv.at[3 * cnt + 3 * a + j],
                                                device_id=(x, y, 1 - c), device_id_type=MESH)

        def own(a):
            return pltpu.make_async_remote_copy(src[a], dst[a].at[mine], send.at[nsem - cnt + a], recv.at[nsem - cnt + a],
                                                device_id=(x, y, 1 - c), device_id_type=MESH)

        started = []
        for a in range(cnt):
            cp = own(a)
            cp.start()
            started.append(cp)
            for j in range(3):
                cp = ici(a, j, mine)
                cp.start()
                started.append(cp)
        for a in range(cnt):
            for j, (px, py) in enumerate(chips):
                ici(a, j, 2 * px + py).wait_recv()
                if a < ns:
                    cp = d2d(a, j, c)
                    cp.start()
                    started.append(cp)
        for a in range(ns):
            for j in range(3):
                d2d(a, j, 1 - c).wait_recv()
        for a in range(cnt):
            own(a).wait_recv()
        for cp in started:
            cp.wait_send()

    nsem = 3 * cnt + 3 * ns + cnt
    return pl.pallas_call(
        body, name=name, in_specs=[ANY] * cnt, out_specs=[ANY] * cnt,
        out_shape=[SDS((4,) + s.shape, s.dtype) for s in split + whole],
        scratch_shapes=[pltpu.SemaphoreType.DMA((nsem,)), pltpu.SemaphoreType.DMA((nsem,))],
    )(*split, *whole)


HBM = pl.BlockSpec(memory_space=pltpu.HBM)
SEM = pl.BlockSpec(memory_space=pltpu.SEMAPHORE)
DATAFLOW = pltpu.SideEffectType.DATAFLOW_SIDE_EFFECTING


def _whole_copies(srcs, dsts, sends, recvs):
    x, y, c = _me()
    peers = [(1 - x, y, c), (x, 1 - y, c), (1 - x, 1 - y, c), (x, y, 1 - c)]
    return [pltpu.make_async_remote_copy(srcs[a], dsts[a].at[2 * x + y], sends[4 * a + j], recvs[4 * a + j],
                                         device_id=peer, device_id_type=MESH)
            for a in range(len(srcs)) for j, peer in enumerate(peers)]


def _scatter_copies(srcs, dsts, sends, recvs):
    x, y, c = _me()
    chips = [(1 - x, y), (x, 1 - y), (1 - x, 1 - y)]
    return [pltpu.make_async_remote_copy(srcs[a].at[2 * px + py], dsts[a].at[2 * x + y], sends[3 * a + j], recvs[3 * a + j],
                                         device_id=(px, py, c), device_id_type=MESH)
            for a in range(len(srcs)) for j, (px, py) in enumerate(chips)]


def _copies_start(copies, per, shards, after, name):
    cnt = len(shards)
    ncp = per * cnt

    def body(*refs):
        srcs, lands = refs[:cnt], refs[cnt:2 * cnt]
        outs = refs[2 * cnt + 1:]
        for cp in copies(srcs, lands, outs[:ncp], outs[ncp:2 * ncp]):
            cp.start()
        outs[-1][...] = jnp.zeros_like(outs[-1])

    lands = [lax.empty((4,) + (s.shape if per == 4 else s.shape[1:]), s.dtype) for s in shards]
    ops = [pltpu.with_memory_space_constraint(a, pltpu.HBM) for a in list(shards) + lands]
    res = pl.pallas_call(
        body, name=name, in_specs=[HBM] * (2 * cnt) + [ANY],
        out_shape=tuple([pltpu.SemaphoreType.DMA(())] * (2 * ncp) + [pltpu.HBM(a.shape, a.dtype) for a in ops] + [SDS((8, LANES), F32)]),
        out_specs=tuple([SEM] * (2 * ncp) + [HBM] * (2 * cnt) + [pl.BlockSpec(memory_space=pltpu.VMEM)]),
        input_output_aliases={i: 2 * ncp + i for i in range(2 * cnt)},
        compiler_params=pltpu.CompilerParams(has_side_effects=DATAFLOW),
    )(*ops, after)
    return res[:2 * ncp], res[2 * ncp:2 * ncp + cnt], res[2 * ncp + cnt:2 * ncp + 2 * cnt], res[-1]


def _copies_wait(copies, per, handle, after, name):
    sems, srcs, lands, _ = handle
    cnt = len(srcs)
    ncp = per * cnt

    def body(*refs):
        sem_refs = refs[2 * cnt:2 * cnt + 2 * ncp]
        for cp in copies(refs[:cnt], refs[cnt:2 * cnt], sem_refs[:ncp], sem_refs[ncp:]):
            cp.wait_send()
            cp.wait_recv()

    res = pl.pallas_call(
        body, name=name, in_specs=[HBM] * (2 * cnt) + [SEM] * (2 * ncp) + [ANY],
        out_shape=tuple(pltpu.HBM(a.shape, a.dtype) for a in list(srcs) + list(lands)),
        out_specs=tuple([HBM] * (2 * cnt)), input_output_aliases={i: i for i in range(2 * cnt)},
        compiler_params=pltpu.CompilerParams(has_side_effects=DATAFLOW),
    )(*srcs, *lands, *sems, after)
    return list(res[cnt:])


def _swap_sibling(parts, pick, name):
    cnt = len(parts)

    def body(*refs):
        src, dst = refs[:cnt], refs[cnt:2 * cnt]
        send, recv = refs[2 * cnt:]
        x, y, c = _me()
        cps = []
        for a in range(cnt):
            cp = pltpu.make_async_remote_copy(src[a].at[1 - c] if pick else src[a], dst[a], send.at[a], recv.at[a],
                                              device_id=(x, y, 1 - c), device_id_type=MESH)
            cp.start()
            cps.append(cp)
        for cp in cps:
            cp.wait()

    return pl.pallas_call(
        body, name=name, in_specs=[ANY] * cnt, out_specs=[ANY] * cnt,
        out_shape=[SDS(p.shape[1:] if pick else p.shape, p.dtype) for p in parts],
        scratch_shapes=[pltpu.SemaphoreType.DMA((cnt,)), pltpu.SemaphoreType.DMA((cnt,))],
    )(*parts)


def _scatter_xy(parts, name):
    cnt = len(parts)

    def body(*refs):
        src, dst = refs[:cnt], refs[cnt:2 * cnt]
        send, recv, loc = refs[2 * cnt:]
        x, y, c = _me()
        mine = 2 * x + y
        chips = [(1 - x, y), (x, 1 - y), (1 - x, 1 - y)]
        local = []
        for a in range(cnt):
            cp = pltpu.make_async_copy(src[a].at[mine], dst[a].at[mine], loc.at[a])
            cp.start()
            local.append(cp)
        sends = []
        for a in range(cnt):
            for j, (px, py) in enumerate(chips):
                cp = pltpu.make_async_remote_copy(src[a].at[2 * px + py], dst[a].at[mine], send.at[3 * a + j], recv.at[3 * a + j],
                                                  device_id=(px, py, c), device_id_type=MESH)
                cp.start()
                sends.append(cp)
        for a in range(cnt):
            for j, (px, py) in enumerate(chips):
                pltpu.make_async_remote_copy(src[a].at[mine], dst[a].at[2 * px + py], send.at[3 * a + j], recv.at[3 * a + j],
                                             device_id=(px, py, c), device_id_type=MESH).wait_recv()
        for cp in sends:
            cp.wait_send()
        for cp in local:
            cp.wait()

    return pl.pallas_call(
        body, name=name, in_specs=[ANY] * cnt, out_specs=[ANY] * cnt,
        out_shape=[SDS(p.shape, p.dtype) for p in parts],
        scratch_shapes=[pltpu.SemaphoreType.DMA((3 * cnt,)), pltpu.SemaphoreType.DMA((3 * cnt,)), pltpu.SemaphoreType.DMA((cnt,))],
    )(*parts)


def _bcast_all(buf, name):
    def body(src, dst, send, recv, loc):
        x, y, c = _me()
        mine = 4 * x + 2 * y + c
        own = pltpu.make_async_copy(src, dst.at[mine], loc)
        own.start()
        sends = []
        for k in range(1, 8):
            px, py, pc = x ^ (k >> 2), y ^ ((k >> 1) & 1), c ^ (k & 1)
            cp = pltpu.make_async_remote_copy(src, dst.at[mine], send.at[k - 1], recv.at[k - 1],
                                              device_id=(px, py, pc), device_id_type=MESH)
            cp.start()
            sends.append(cp)
        for k in range(1, 8):
            px, py, pc = x ^ (k >> 2), y ^ ((k >> 1) & 1), c ^ (k & 1)
            pltpu.make_async_remote_copy(src, dst.at[4 * px + 2 * py + pc], send.at[k - 1], recv.at[k - 1],
                                         device_id=(px, py, pc), device_id_type=MESH).wait_recv()
        for cp in sends:
            cp.wait_send()
        own.wait()

    return pl.pallas_call(
        body, name=name, in_specs=[ANY], out_specs=ANY, out_shape=SDS((8,) + buf.shape, buf.dtype),
        scratch_shapes=[pltpu.SemaphoreType.DMA((7,)), pltpu.SemaphoreType.DMA((7,)), pltpu.SemaphoreType.DMA(())],
    )(buf)


def _sum_slots(recv, own, chip, name):
    kk, n, c = recv.shape
    tm = _row_tile(n)

    def body(chip_ref, r_ref, o_ref, out_ref):
        acc = None
        for j in range(kk):
            v = jnp.where(chip_ref[0] == j, o_ref[...], r_ref[j]).astype(F32)
            acc = v if acc is None else acc + v
        out_ref[...] = acc

    grid_spec = pltpu.PrefetchScalarGridSpec(
        num_scalar_prefetch=1, grid=(n // tm,),
        in_specs=[pl.BlockSpec((kk, tm, c), lambda i, chip_ref: (0, i, 0)),
                  pl.BlockSpec((None, tm, c), lambda i, chip_ref: (chip_ref[0], i, 0))],
        out_specs=pl.BlockSpec((tm, c), lambda i, chip_ref: (i, 0)))
    return pl.pallas_call(body, name=name, grid_spec=grid_spec, out_shape=SDS((n, c), F32),
                          compiler_params=_cp(("arbitrary",), VMEM_LIMIT))(chip.reshape(1), recv, own)


def _add_half(parts, got, core, dtype, name):
    shp = got.shape
    a2, b2 = parts.reshape(2, -1, shp[-1]), got.reshape(-1, shp[-1])
    n, c = b2.shape
    tm = _row_tile(n, 256)

    def body(core_ref, a_ref, b_ref, o_ref):
        o_ref[...] = (a_ref[...] + b_ref[...]).astype(dtype)

    spec = pl.BlockSpec((tm, c), lambda i, core_ref: (i, 0))
    grid_spec = pltpu.PrefetchScalarGridSpec(
        num_scalar_prefetch=1, grid=(n // tm,),
        in_specs=[pl.BlockSpec((None, tm, c), lambda i, core_ref: (core_ref[0], i, 0)), spec], out_specs=spec)
    return pl.pallas_call(body, name=name, grid_spec=grid_spec, out_shape=SDS((n, c), dtype),
                          compiler_params=_cp(("arbitrary",), VMEM_LIMIT))(core.reshape(1), a2, b2).reshape(shp)


def _x_layout(u, name):
    n, c = u.shape
    t, ch = S5_T, S5_CH
    gb = LANES // ch
    rows = min(64, n // t)

    def body(u_ref, o_ref):
        for s in range(t):
            us = u_ref[pl.ds(s, rows, stride=t), :]
            for g in range(gb):
                o_ref[g, :, ch * s:ch * (s + 1)] = us[:, ch * g:ch * (g + 1)]

    return pl.pallas_call(
        body, name=name, grid=(n // (rows * t), c // LANES),
        in_specs=[pl.BlockSpec((rows * t, LANES), lambda i, j: (i, j))],
        out_specs=pl.BlockSpec((gb, rows, t * ch), lambda i, j: (j, i, 0)),
        out_shape=SDS((c // ch, n // t, t * ch), F32),
        compiler_params=_cp(("arbitrary", "arbitrary"), VMEM_LIMIT),
    )(u)


def _token_layout(xg, name):
    ng, r, w = xg.shape
    t, ch = S5_T, S5_CH
    gb = LANES // ch
    rows = min(64, r)

    def body(x_ref, o_ref):
        for s in range(t):
            parts = [x_ref[g, :, ch * s:ch * (s + 1)].astype(F32) for g in range(gb)]
            o_ref[pl.ds(s, rows, stride=t), :] = jnp.concatenate(parts, axis=1)

    return pl.pallas_call(
        body, name=name, grid=(r // rows, ng // gb),
        in_specs=[pl.BlockSpec((gb, rows, w), lambda i, j: (j, i, 0))],
        out_specs=pl.BlockSpec((rows * t, LANES), lambda i, j: (i, j)),
        out_shape=SDS((r * t, ng * ch), F32),
        compiler_params=_cp(("arbitrary", "arbitrary"), VMEM_LIMIT),
    )(xg)


def _pad_lanes(a, lanes=LANES):
    return jnp.pad(a, ((0, 0), (0, lanes - a.shape[1])))


def _local_step(x, tgt, p, bl, late, early):
    p = dict(p)
    n, d = x.shape
    sw = SSD_HEADS * SSD_HEAD_DIM
    gn = SSD_GROUPS * SSD_STATE
    tm = min(n, 512)
    tm_ffn = min(n, 256)
    nck = n // bl // S5_T
    s5w = S5_GROUPS * S5_CH

    w_in = p["w_in"]
    o1, o2, o3, o4 = sw, sw + sw, sw + sw + gn, sw + sw + 2 * gn
    w_z, w_xs, w_b, w_c = w_in[:, :o1], w_in[:, o1:o2], w_in[:, o2:o3], w_in[:, o3:o4]
    w_dt = jnp.concatenate([_pad_lanes(w_in[:, o4:o4 + SSD_HEADS]), _pad_lanes(w_in[:, o4 + SSD_HEADS:o4 + 2 * SSD_HEADS])], 1)
    w_u = w_in[:, o4 + 2 * SSD_HEADS:]
    in_ws = [w_z, w_xs, w_b, w_c, w_dt, w_u]
    cw, cb_ = p["ssd_conv_w"], p["ssd_conv_b"]
    conv_parts = [(cw[:, :sw], cb_[:, :sw]), (cw[:, sw:sw + gn], cb_[:, sw:sw + gn]), (cw[:, sw + gn:], cb_[:, sw + gn:])]
    alog2 = jnp.stack([_pad_lanes(p["ssd_a_log_fwd"]), _pad_lanes(p["ssd_a_log_bwd"])])
    dtb2 = jnp.stack([_pad_lanes(p["ssd_dt_bias_fwd"]), _pad_lanes(p["ssd_dt_bias_bwd"])])
    dvec = _pad_lanes(p["ssd_d"]).reshape(LANES, 1)

    hn, z, xs_pre, b_pre, c_pre, dtr, u = _norm_matmul(x, p["norm_mix_w"], in_ws, tm, "in_proj")
    pres = [xs_pre, b_pre, c_pre]
    acts = [_conv_silu(pre, w, b, bl, min(256, pre.shape[1]), f"ssd_conv_{i}") for i, (pre, (w, b)) in enumerate(zip(pres, conv_parts))]
    xs_a, b_a, c_a = acts
    y2, saved = _ssd_scan(xs_a, b_a, c_a, dtr, alog2, dtb2, bl, "ssd_scan")

    def col(a):
        return a.reshape(a.shape + (1,))

    s5_params = [
        col(p["s5_lambda_re_fwd"]), col(p["s5_lambda_im_fwd"]), p["s5_log_step_fwd"].reshape(S5_GROUPS, 1, 1), p["s5_c_re_fwd"], p["s5_c_im_fwd"],
        col(p["s5_lambda_re_bwd"]), col(p["s5_lambda_im_bwd"]), p["s5_log_step_bwd"].reshape(S5_GROUPS, 1, 1), p["s5_c_re_bwd"], p["s5_c_im_bwd"],
        p["s5_b_re"], p["s5_b_im"], col(p["s5_d"].reshape(S5_GROUPS, S5_CH)),
        p["s5_glu_w"][:, :, :S5_CH], p["s5_glu_w"][:, :, S5_CH:], col(p["s5_glu_b"][:, :S5_CH]), col(p["s5_glu_b"][:, S5_CH:]),
    ]
    s5_args = [_x_layout(u, "s5_u_blocks")] + s5_params
    s5o = _token_layout(_s5_fwd(s5_args, nck, "s5_fwd"), "s5_y_tokens")
    ymix = _mix(y2, xs_a, z, s5o, dvec, p["ssd_norm_w"], p["s5_norm_w"], tm, "mix")
    p["w_out"], p["w_up"], p["w_down"] = late(ymix)
    dff = p["w_down"].shape[0]
    h1 = _matmul_res(ymix, p["w_out"], x, tm, "out_proj")
    w_up = p["w_up"]
    hn2, up_v, up_g = _norm_matmul(h1, p["norm_ffn_w"], [w_up[:, :dff], w_up[:, dff:]], tm_ffn, "ffn_up")
    fw, fb = p["ffn_conv_w"], p["ffn_conv_b"]
    act = _conv_glu(up_v, up_g, fw[:, :dff], fw[:, dff:], fb[:, :dff], fb[:, dff:], bl, 256, "ffn_conv")
    h2 = _matmul_res(act, p["w_down"], h1, tm, "ffn_down")
    loss, dh2, g_nfw = _final_loss(h2, p["norm_final_w"].reshape(1, d), tgt, tm, "final_loss")

    g = {"norm_final_w": g_nfw.reshape(d)}
    g["w_down"] = _matmul_tn(act, dh2, tm, d, "ffn_down_dw")
    dact = _matmul_nt([dh2], [p["w_down"]], tm, "ffn_down_dx")
    dup_v, dup_g, dwv, dwg, dbv, dbg = _conv_glu_bwd(up_v, up_g, fw[:, :dff], fw[:, dff:], fb[:, :dff], fb[:, dff:], dact, bl, 256, "ffn_conv_bwd")
    g["ffn_conv_w"] = jnp.concatenate([dwv, dwg], 1)
    g["ffn_conv_b"] = jnp.concatenate([dbv, dbg], 1)
    g["w_up"] = jnp.concatenate([_matmul_tn(hn2, dup_v, tm, dff // 2, "ffn_up_dw_v"), _matmul_tn(hn2, dup_g, tm, dff // 2, "ffn_up_dw_g")], 1)
    dhn2 = _matmul_nt([dup_v, dup_g], [w_up[:, :dff], w_up[:, dff:]], tm_ffn, "ffn_up_dx")
    dh1, g["norm_ffn_w"] = _norm_bwd(h1, p["norm_ffn_w"], dhn2, dh2, tm, "ffn_norm_bwd")
    g["w_out"] = _matmul_tn(ymix, dh1, tm, d, "out_proj_dw")
    g["early_reduce"] = early(g["w_out"], g["w_up"], g["w_down"])
    behind = g["early_reduce"][1][3][0, 0]
    dmix = _matmul_nt([dh1], [p["w_out"]], tm, "out_proj_dx")
    dyssd, dxs_gate, dz, ds5o, g_d, g["ssd_norm_w"], g["s5_norm_w"] = _mix_bwd(
        y2, xs_a, z, s5o, dvec, p["ssd_norm_w"] + behind, p["s5_norm_w"], dmix, tm, "mix_bwd")
    g["ssd_d"] = g_d[:SSD_HEADS].reshape(1, SSD_HEADS)
    s5g = _s5_bwd(s5_args, _x_layout(ds5o, "s5_dy_blocks"), nck, "s5_bwd")
    du = _token_layout(s5g[0], "s5_du_tokens")
    (g["s5_lambda_re_fwd"], g["s5_lambda_im_fwd"], g["s5_log_step_fwd"], g["s5_c_re_fwd"], g["s5_c_im_fwd"],
     g["s5_lambda_re_bwd"], g["s5_lambda_im_bwd"], g["s5_log_step_bwd"], g["s5_c_re_bwd"], g["s5_c_im_bwd"],
     g["s5_b_re"], g["s5_b_im"], g_s5d, g_wv, g_wg, g_bv, g_bg) = s5g[1:]
    for k_ in ("s5_lambda_re_fwd", "s5_lambda_im_fwd", "s5_lambda_re_bwd", "s5_lambda_im_bwd"):
        g[k_] = g[k_].reshape(S5_GROUPS, S5_STATE)
    for k_ in ("s5_log_step_fwd", "s5_log_step_bwd"):
        g[k_] = g[k_].reshape(S5_GROUPS)
    g["s5_d"] = g_s5d.reshape(1, s5w)
    g["s5_glu_w"] = jnp.concatenate([g_wv, g_wg], 2)
    g["s5_glu_b"] = jnp.concatenate([g_bv.reshape(S5_GROUPS, S5_CH), g_bg.reshape(S5_GROUPS, S5_CH)], 1)
    dxs2, dbm2, dcm2, ddtr, dal2, ddb2 = _ssd_scan_bwd(xs_a, b_a, c_a, dtr, alog2, dtb2, saved, dyssd, bl, "ssd_scan_bwd")
    g["ssd_a_log_fwd"], g["ssd_a_log_bwd"] = dal2[0, :, :SSD_HEADS], dal2[1, :, :SSD_HEADS]
    g["ssd_dt_bias_fwd"], g["ssd_dt_bias_bwd"] = ddb2[0, :, :SSD_HEADS], ddb2[1, :, :SSD_HEADS]
    cots = [[(dxs2, 0), (dxs2, 1), (dxs_gate, None)], [(dbm2, 0), (dbm2, 1)], [(dcm2, 0), (dcm2, 1)]]
    dpres, dcw, dcb = [], [], []
    for i, (pre, (w, b), cot) in enumerate(zip(pres, conv_parts, cots)):
        dp, dw_, db_ = _conv_silu_bwd(pre, w, b, cot, bl, min(256, pre.shape[1]), f"ssd_conv_bwd_{i}")
        dpres.append(dp)
        dcw.append(dw_)
        dcb.append(db_)
    g["ssd_conv_w"] = jnp.concatenate(dcw, 1)
    g["ssd_conv_b"] = jnp.concatenate(dcb, 1)
    dprojs = [dz, dpres[0], dpres[1], dpres[2], ddtr, du]
    dws = [_matmul_tn(hn, dpj, tm, dpj.shape[1], f"in_proj_dw_{i}") for i, dpj in enumerate(dprojs)]
    dws[4] = jnp.concatenate([dws[4][:, :SSD_HEADS], dws[4][:, LANES:LANES + SSD_HEADS]], 1)
    g["w_in"] = jnp.concatenate(dws, 1)
    dhn = _matmul_nt(dprojs, in_ws, tm, "in_proj_dx")
    grad_x, g["norm_mix_w"] = _norm_bwd(x, p["norm_mix_w"], dhn, dh1, tm, "mix_norm_bwd")
    return loss, grad_x, g


_WEIGHTS = ['norm_mix_w', 'w_in', 'ssd_conv_w', 'ssd_conv_b', 'ssd_dt_bias_fwd', 'ssd_dt_bias_bwd', 'ssd_a_log_fwd', 'ssd_a_log_bwd',
            'ssd_d', 'ssd_norm_w', 's5_lambda_re_fwd', 's5_lambda_im_fwd', 's5_log_step_fwd', 's5_lambda_re_bwd', 's5_lambda_im_bwd',
            's5_log_step_bwd', 's5_b_re', 's5_b_im', 's5_c_re_fwd', 's5_c_im_fwd', 's5_c_re_bwd', 's5_c_im_bwd', 's5_d', 's5_glu_w',
            's5_glu_b', 's5_norm_w', 'w_out', 'norm_ffn_w', 'ffn_w_up', 'ffn_conv_w', 'ffn_conv_b', 'ffn_w_down', 'norm_final_w']
_BIG = ('w_in', 'w_out', 'ffn_w_up', 'ffn_w_down')
_CONV = ('ssd_conv_w', 'ffn_conv_w')


def _pack(arrs):
    flat = jnp.concatenate([a.reshape(-1) for a in arrs])
    rows = -(-flat.shape[0] // (64 * LANES)) * 64
    return jnp.pad(flat, (0, rows * LANES - flat.shape[0])).reshape(rows, LANES)


def _unpack(buf, shapes):
    flat = buf.reshape(-1)
    out, off = [], 0
    for shp in shapes:
        size = math.prod(shp)
        out.append(flat[off:off + size].reshape(shp))
        off += size
    return out


def kernel(x, norm_mix_w, w_in, ssd_conv_w, ssd_conv_b, ssd_dt_bias_fwd, ssd_dt_bias_bwd, ssd_a_log_fwd, ssd_a_log_bwd, ssd_d, ssd_norm_w, s5_lambda_re_fwd, s5_lambda_im_fwd, s5_log_step_fwd, s5_lambda_re_bwd, s5_lambda_im_bwd, s5_log_step_bwd, s5_b_re, s5_b_im, s5_c_re_fwd, s5_c_im_fwd, s5_c_re_bwd, s5_c_im_bwd, s5_d, s5_glu_w, s5_glu_b, s5_norm_w, w_out, norm_ffn_w, ffn_w_up, ffn_conv_w, ffn_conv_b, ffn_w_down, norm_final_w, loss_target, m_norm_mix_w, m_w_in, m_ssd_conv_w, m_ssd_conv_b, m_ssd_dt_bias_fwd, m_ssd_dt_bias_bwd, m_ssd_a_log_fwd, m_ssd_a_log_bwd, m_ssd_d, m_ssd_norm_w, m_s5_lambda_re_fwd, m_s5_lambda_im_fwd, m_s5_log_step_fwd, m_s5_lambda_re_bwd, m_s5_lambda_im_bwd, m_s5_log_step_bwd, m_s5_b_re, m_s5_b_im, m_s5_c_re_fwd, m_s5_c_im_fwd, m_s5_c_re_bwd, m_s5_c_im_bwd, m_s5_d, m_s5_glu_w, m_s5_glu_b, m_s5_norm_w, m_w_out, m_norm_ffn_w, m_ffn_w_up, m_ffn_conv_w, m_ffn_conv_b, m_ffn_w_down, m_norm_final_w, v_norm_mix_w, v_w_in, v_ssd_conv_w, v_ssd_conv_b, v_ssd_dt_bias_fwd, v_ssd_dt_bias_bwd, v_ssd_a_log_fwd, v_ssd_a_log_bwd, v_ssd_d, v_ssd_norm_w, v_s5_lambda_re_fwd, v_s5_lambda_im_fwd, v_s5_log_step_fwd, v_s5_lambda_re_bwd, v_s5_lambda_im_bwd, v_s5_log_step_bwd, v_s5_b_re, v_s5_b_im, v_s5_c_re_fwd, v_s5_c_im_fwd, v_s5_c_re_bwd, v_s5_c_im_bwd, v_s5_d, v_s5_glu_w, v_s5_glu_b, v_s5_norm_w, v_w_out, v_norm_ffn_w, v_ffn_w_up, v_ffn_conv_w, v_ffn_conv_b, v_ffn_w_down, v_norm_final_w):
    args = dict(locals())
    w = {k_: args[k_] for k_ in _WEIGHTS}
    m = {k_: args["m_" + k_] for k_ in _WEIGHTS}
    v = {k_: args["v_" + k_] for k_ in _WEIGHTS}
    bl, sl, d = x.shape
    chip = 2 * lax.axis_index("x") + lax.axis_index("y")
    core = lax.axis_index("c")

    first = w["w_in"][0].astype(BF16)
    g_in, g_scw, g_fcw = _gather_xy([first.reshape(2, first.shape[0] // 2, first.shape[1])], [w[k_][0] for k_ in _CONV], "gather_first")
    g_in = g_in.reshape((4,) + first.shape)
    handle = _copies_start(_whole_copies, 4, [w[k_][0].astype(BF16) for k_ in _BIG[1:]], g_scw, "gather_rest_start")

    def cols(a):
        return jnp.moveaxis(a, 0, 1).reshape(a.shape[1], 4 * a.shape[2])

    p = {k_: (w[k_][0] if w[k_].ndim >= 3 else w[k_]) for k_ in _WEIGHTS if k_ not in _BIG + _CONV}
    p["norm_mix_w"] = p["norm_mix_w"] + handle[3][0, 0]
    p["w_in"] = cols(g_in)
    p["ssd_conv_w"], p["ffn_conv_w"] = cols(g_scw), cols(g_fcw)

    def late(after):
        g_out, g_up, g_down = _copies_wait(_whole_copies, 4, handle, after, "gather_rest_wait")
        return g_out.reshape(-1, g_out.shape[2]), cols(g_up), g_down.reshape(-1, g_down.shape[2])

    def owner_major(a, k_):
        r, c = w[k_].shape[1:]
        if a.shape[0] == r:
            a = jnp.moveaxis(a.reshape(r, 4, c), 1, 0)
        else:
            a = a.reshape(4, r, c)
        return a.reshape(4, 2, r // 2, c)

    def early(*grads):
        parts = [jnp.moveaxis(owner_major(a, k_), 1, 0) for a, k_ in zip(grads, _BIG[1:])]
        got = _swap_sibling(parts, True, "reduce_sibling_early")
        sums = [_add_half(pt, gt, core, BF16, f"reduce_add_early_{i}") for i, (pt, gt) in enumerate(zip(parts, got))]
        return sums, _copies_start(_scatter_copies, 3, sums, w["norm_ffn_w"], "reduce_chips_early_start")

    loss, grad_x, g = _local_step(x.reshape(bl * sl, d), loss_target.reshape(bl * sl, d), p, bl, late, early)
    early_sums, early_handle = g.pop("early_reduce")
    early_recv = _copies_wait(_scatter_copies, 3, early_handle, grad_x, "reduce_chips_early_wait")
    early_halves = [_sum_slots(rc, sm, chip, f"reduce_sum_early_{i}") for i, (rc, sm) in enumerate(zip(early_recv, early_sums))]

    small = [k_ for k_ in _WEIGHTS if k_ not in _BIG]
    small_full_shapes = [g[k_].shape for k_ in small]
    buf = _pack([g[k_] for k_ in small] + [loss[0, :1]])
    rest = _BIG[:1]
    parts =[jnp.moveaxis(owner_major(g[k_], k_), 1, 0) for k_ in rest]
    parts.append(jnp.moveaxis(buf.reshape(4, 2, -1, LANES), 1, 0))
    got = _swap_sibling(parts, True, "reduce_sibling")
    chip_sums = [_add_half(pt, gt, core, BF16 if i < len(rest) else F32, f"reduce_add_{i}") for i, (pt, gt) in enumerate(zip(parts, got))]
    from_chips = _scatter_xy(chip_sums, "reduce_chips")
    rest_halves = [_sum_lead(a.reshape(4, -1, a.shape[-1]), f"reduce_sum_{i}") for i, a in enumerate(from_chips)]
    halves = rest_halves[:-1] + early_halves + rest_halves[-1:]
    other = _swap_sibling(halves[:-1], False, "reduce_join")
    big_grad = {}
    for k_, own_half, sib_half in zip(_BIG, halves, other):
        south = core == 0
        full = jnp.stack([jnp.where(south, own_half, sib_half), jnp.where(south, sib_half, own_half)])
        big_grad[k_] = full.reshape((1,) + w[k_].shape[1:])
    tot = _bcast_all(halves[-1], "reduce_small").reshape(buf.shape)
    unp = _unpack(tot, small_full_shapes + [(1,)])
    small_grad = dict(zip(small, unp[:-1]))
    loss_out = unp[-1].reshape(())
    for k_ in _CONV:
        cshard = w[k_].shape[2]
        small_grad[k_] = lax.dynamic_slice_in_dim(small_grad[k_], chip * cshard, cshard, 1)

    grads, deltas, new_m, new_v = {}, {}, {}, {}
    for k_ in _BIG:
        shp = w[k_].shape
        grads[k_] = big_grad[k_]
        dl, nm, nv = _adamw(w[k_][0], big_grad[k_][0], m[k_][0], v[k_][0], f"adamw_{k_}")
        deltas[k_], new_m[k_], new_v[k_] = dl.reshape(shp), nm.reshape(shp), nv.reshape(shp)
    sw_ = _pack([w[k_] for k_ in small])
    sg_ = _pack([small_grad[k_] for k_ in small])
    sm_ = _pack([m[k_] for k_ in small])
    sv_ = _pack([v[k_] for k_ in small])
    dl, nm, nv = _adamw(sw_, sg_, sm_, sv_, "adamw_small")
    shapes = [w[k_].shape for k_ in small]
    for k_, a, b, c_ in zip(small, _unpack(dl, shapes), _unpack(nm, shapes), _unpack(nv, shapes)):
        deltas[k_], new_m[k_], new_v[k_] = a, b, c_
        grads[k_] = small_grad[k_].reshape(w[k_].shape)
    return (loss_out, grad_x.reshape(bl, sl, d), *[grads[k_] for k_ in _WEIGHTS], *[deltas[k_] for k_ in _WEIGHTS],
            *[new_m[k_] for k_ in _WEIGHTS], *[new_v[k_] for k_ in _WEIGHTS])
```

```python
import functools
import math

import jax
import jax.numpy as jnp
from jax import lax
from jax.experimental import pallas as pl
from jax.experimental.pallas import tpu as pltpu

F32 = jnp.float32
BF16 = jnp.bfloat16
HI = lax.Precision.HIGHEST
SDS = jax.ShapeDtypeStruct
MESH = pl.DeviceIdType.MESH

NN = (((1,), (0,)), ((), ()))
NT = (((1,), (1,)), ((), ()))
TN = (((0,), (0,)), ((), ()))

EPS = 1e-6
SSD_HEADS = 16
SSD_HEAD_DIM = 64
SSD_GROUPS = 4
SSD_STATE = 128
SSD_CHUNK = 128
S5_GROUPS = 32
S5_CH = 16
S5_STATE = 64
S5_T = 16
LANES = 128
ADAM_LR, ADAM_B1, ADAM_B2, ADAM_EPS, ADAM_WD, ADAM_STEP = 0.001, 0.9, 0.999, 1e-08, 0.01, 10
V7X_VMEM_BYTES = 64 * 1024 * 1024
VMEM_LIMIT = V7X_VMEM_BYTES - 8 * 1024 * 1024


def _cp(sem, vmem=None):
    return pltpu.CompilerParams(dimension_semantics=sem, vmem_limit_bytes=vmem)


def _dot(a, b, dims=NN, precision=None):
    return lax.dot_general(a, b, dims, precision=precision, preferred_element_type=F32)


def _rms(x, w):
    return x * lax.rsqrt(jnp.mean(x * x, axis=-1, keepdims=True) + EPS) * w


def _sigmoid(x):
    return 1.0 / (1.0 + jnp.exp(-x))


def _softplus(x):
    return jnp.maximum(x, 0.0) + jnp.log1p(jnp.exp(-jnp.abs(x)))


@functools.partial(jax.custom_vjp, nondiff_argnums=(1, 2))
def _shift(x, k, seg):
    n = x.shape[0]
    r = lax.broadcasted_iota(jnp.int32, x.shape, 0)
    if seg != n:
        r = r & (seg - 1) if seg & (seg - 1) == 0 else r % seg
    y = pltpu.roll(x, k % n, 0)
    ok = (r >= k) if k > 0 else (r < seg + k)
    return jnp.where(ok, y, 0.0)


def _shift_fwd(x, k, seg):
    return _shift(x, k, seg), None


def _shift_bwd(k, seg, _, g):
    return (_shift(g, -k, seg),)


_shift.defvjp(_shift_fwd, _shift_bwd)


@functools.partial(jax.custom_vjp, nondiff_argnums=(1,))
def _lane_shift(x, k):
    if k == 0:
        return x
    n = x.shape[1]
    lane = lax.broadcasted_iota(jnp.int32, x.shape, 1)
    ok = (lane >= k) if k > 0 else (lane < n + k)
    return jnp.where(ok, pltpu.roll(x, k % n, 1), 0.0)


_lane_shift.defvjp(lambda x, k: (_lane_shift(x, k), None), lambda k, _, g: (_lane_shift(g, -k),))


@jax.custom_vjp
def _swap(z):
    return pltpu.roll(z, LANES // 2, 1)


_swap.defvjp(lambda z: (_swap(z), None), lambda _, g: (_swap(g),))


def _norm_matmul(x, nw, ws, tm, name):
    n, d = x.shape
    k = len(ws)

    def body(x_ref, nw_ref, *refs):
        hn = _rms(x_ref[...], nw_ref[...]).astype(BF16)
        refs[k][...] = hn
        for w_ref, o_ref in zip(refs[:k], refs[k + 1:]):
            o_ref[...] = _dot(hn, w_ref[...])

    row = lambda i: (i, 0)
    fix = lambda i: (0, 0)
    return pl.pallas_call(
        body, name=name, grid=(n // tm,),
        in_specs=[pl.BlockSpec((tm, d), row), pl.BlockSpec((1, d), fix)] + [pl.BlockSpec(w.shape, fix) for w in ws],
        out_specs=[pl.BlockSpec((tm, d), row)] + [pl.BlockSpec((tm, w.shape[1]), row) for w in ws],
        out_shape=[SDS((n, d), BF16)] + [SDS((n, w.shape[1]), F32) for w in ws],
        compiler_params=_cp(("arbitrary",), VMEM_LIMIT),
    )(x, nw, *ws)


def _matmul_res(a, w, res, tm, name):
    n, kd = a.shape
    m = w.shape[1]

    def body(a_ref, w_ref, r_ref, o_ref):
        o_ref[...] = r_ref[...] + _dot(a_ref[...], w_ref[...])

    return pl.pallas_call(
        body, name=name, grid=(n // tm,),
        in_specs=[pl.BlockSpec((tm, kd), lambda i: (i, 0)), pl.BlockSpec((kd, m), lambda i: (0, 0)),
                  pl.BlockSpec((tm, m), lambda i: (i, 0))],
        out_specs=pl.BlockSpec((tm, m), lambda i: (i, 0)),
        out_shape=SDS((n, m), F32),
        compiler_params=_cp(("arbitrary",), VMEM_LIMIT),
    )(a, w, res)


def _matmul_nt(gs, ws, tm, name):
    n = gs[0].shape[0]
    kd = ws[0].shape[0]
    cnt = len(gs)

    def body(*refs):
        acc = None
        for g_ref, w_ref in zip(refs[:cnt], refs[cnt:2 * cnt]):
            t = _dot(g_ref[...].astype(BF16), w_ref[...], NT)
            acc = t if acc is None else acc + t
        refs[2 * cnt][...] = acc

    return pl.pallas_call(
        body, name=name, grid=(n // tm,),
        in_specs=[pl.BlockSpec((tm, g.shape[1]), lambda i: (i, 0)) for g in gs]
        + [pl.BlockSpec(w.shape, lambda i: (0, 0)) for w in ws],
        out_specs=pl.BlockSpec((tm, kd), lambda i: (i, 0)),
        out_shape=SDS((n, kd), F32),
        compiler_params=_cp(("arbitrary",), VMEM_LIMIT),
    )(*gs, *ws)


def _matmul_tn(a, g, tm, cb, name):
    n, kd = a.shape
    m = g.shape[1]

    def body(a_ref, g_ref, o_ref):
        t = _dot(a_ref[...], g_ref[...].astype(BF16), TN)

        @pl.when(pl.program_id(1) == 0)
        def _():
            o_ref[...] = t

        @pl.when(pl.program_id(1) != 0)
        def _():
            o_ref[...] += t

    return pl.pallas_call(
        body, name=name, grid=(m // cb, n // tm),
        in_specs=[pl.BlockSpec((tm, kd), lambda j, i: (i, 0)), pl.BlockSpec((tm, cb), lambda j, i: (i, j))],
        out_specs=pl.BlockSpec((kd, cb), lambda j, i: (0, j)),
        out_shape=SDS((kd, m), F32),
        compiler_params=_cp(("arbitrary", "arbitrary"), VMEM_LIMIT),
    )(a, g)


def _dwconv(x, w, b):
    kw = w.shape[0]
    acc = b
    for k in range(kw):
        acc = acc + w[k:k + 1, :] * _shift(x, kw // 2 - k, x.shape[0])
    return acc


def _conv_silu_fn(x, w, b):
    y = _dwconv(x, w, b)
    return y * _sigmoid(y)


def _conv_glu_fn(v, g, wv, wg, bv, bg):
    cv = _dwconv(v, wv, bv)
    cg = _dwconv(g, wg, bg)
    return cg * _sigmoid(cg) * cv


def _conv_silu(x, w, b, bl, cb, name):
    n, c = x.shape
    sl = n // bl
    kw = w.shape[0]

    def body(x_ref, w_ref, b_ref, o_ref):
        o_ref[...] = _conv_silu_fn(x_ref[...], w_ref[...], b_ref[...])

    return pl.pallas_call(
        body, name=name, grid=(bl, c // cb),
        in_specs=[pl.BlockSpec((sl, cb), lambda s, j: (s, j)), pl.BlockSpec((kw, cb), lambda s, j: (0, j)),
                  pl.BlockSpec((1, cb), lambda s, j: (0, j))],
        out_specs=pl.BlockSpec((sl, cb), lambda s, j: (s, j)),
        out_shape=SDS((n, c), F32),
        compiler_params=_cp(("arbitrary", "arbitrary"), VMEM_LIMIT),
    )(x, w, b)


def _conv_silu_bwd(x, w, b, dys, bl, cb, name):
    n, c = x.shape
    sl = n // bl
    kw = w.shape[0]
    cnt = len(dys)

    def body(x_ref, w_ref, b_ref, *refs):
        dy = refs[0][...]
        for r in refs[1:cnt]:
            dy = dy + r[...]
        dx_ref, dw_ref, db_ref = refs[cnt:]
        _, vjp = jax.vjp(_conv_silu_fn, x_ref[...], w_ref[...], b_ref[...])
        dx, dw, db = vjp(dy)
        dx_ref[...] = dx.astype(BF16)

        @pl.when(pl.program_id(1) == 0)
        def _():
            dw_ref[...] = dw
            db_ref[...] = db

        @pl.when(pl.program_id(1) != 0)
        def _():
            dw_ref[...] += dw
            db_ref[...] += db

    dy_specs = []
    for arr, lead in dys:
        if lead is None:
            dy_specs.append(pl.BlockSpec((sl, cb), lambda j, s: (s, j)))
        else:
            dy_specs.append(pl.BlockSpec((None, sl, cb), functools.partial(lambda j, s, lead: (lead, s, j), lead=lead)))
    return pl.pallas_call(
        body, name=name, grid=(c // cb, bl),
        in_specs=[pl.BlockSpec((sl, cb), lambda j, s: (s, j)), pl.BlockSpec((kw, cb), lambda j, s: (0, j)),
                  pl.BlockSpec((1, cb), lambda j, s: (0, j))] + dy_specs,
        out_specs=[pl.BlockSpec((sl, cb), lambda j, s: (s, j)), pl.BlockSpec((kw, cb), lambda j, s: (0, j)),
                   pl.BlockSpec((1, cb), lambda j, s: (0, j))],
        out_shape=[SDS((n, c), BF16), SDS((kw, c), F32), SDS((1, c), F32)],
        compiler_params=_cp(("arbitrary", "arbitrary"), VMEM_LIMIT),
    )(x, w, b, *[a for a, _ in dys])


def _conv_glu(v, g, wv, wg, bv, bg, bl, cb, name):
    n, c = v.shape
    sl = n // bl
    kw = wv.shape[0]

    def body(v_ref, g_ref, wv_ref, wg_ref, bv_ref, bg_ref, o_ref):
        o_ref[...] = _conv_glu_fn(v_ref[...], g_ref[...], wv_ref[...], wg_ref[...], bv_ref[...], bg_ref[...]).astype(BF16)

    big = pl.BlockSpec((sl, cb), lambda s, j: (s, j))
    wsp = pl.BlockSpec((kw, cb), lambda s, j: (0, j))
    bsp = pl.BlockSpec((1, cb), lambda s, j: (0, j))
    return pl.pallas_call(
        body, name=name, grid=(bl, c // cb),
        in_specs=[big, big, wsp, wsp, bsp, bsp], out_specs=big, out_shape=SDS((n, c), BF16),
        compiler_params=_cp(("arbitrary", "arbitrary"), VMEM_LIMIT),
    )(v, g, wv, wg, bv, bg)


def _conv_glu_bwd(v, g, wv, wg, bv, bg, dact, bl, cb, name):
    n, c = v.shape
    sl = n // bl
    kw = wv.shape[0]

    def body(v_ref, g_ref, wv_ref, wg_ref, bv_ref, bg_ref, da_ref, dv_ref, dg_ref, dwv_ref, dwg_ref, dbv_ref, dbg_ref):
        _, vjp = jax.vjp(_conv_glu_fn, v_ref[...], g_ref[...], wv_ref[...], wg_ref[...], bv_ref[...], bg_ref[...])
        dv, dg, dwv, dwg, dbv, dbg = vjp(da_ref[...])
        dv_ref[...] = dv.astype(BF16)
        dg_ref[...] = dg.astype(BF16)

        @pl.when(pl.program_id(1) == 0)
        def _():
            dwv_ref[...] = dwv
            dwg_ref[...] = dwg
            dbv_ref[...] = dbv
            dbg_ref[...] = dbg

        @pl.when(pl.program_id(1) != 0)
        def _():
            dwv_ref[...] += dwv
            dwg_ref[...] += dwg
            dbv_ref[...] += dbv
            dbg_ref[...] += dbg

    big = pl.BlockSpec((sl, cb), lambda j, s: (s, j))
    wsp = pl.BlockSpec((kw, cb), lambda j, s: (0, j))
    bsp = pl.BlockSpec((1, cb), lambda j, s: (0, j))
    return pl.pallas_call(
        body, name=name, grid=(c // cb, bl),
        in_specs=[big, big, wsp, wsp, bsp, bsp, big],
        out_specs=[big, big, wsp, wsp, bsp, bsp],
        out_shape=[SDS((n, c), BF16), SDS((n, c), BF16), SDS((kw, c), F32), SDS((kw, c), F32), SDS((1, c), F32), SDS((1, c), F32)],
        compiler_params=_cp(("arbitrary", "arbitrary"), VMEM_LIMIT),
    )(v, g, wv, wg, bv, bg, dact)


_DIMS_T = {NN: (NT, TN, False, False), NT: (NN, TN, False, True), TN: (NT, NN, True, False)}


@functools.partial(jax.custom_vjp, nondiff_argnums=(2,))
def _bdot(a, b, dims):
    return _dot(a.astype(BF16), b.astype(BF16), dims)


def _bdot_fwd(a, b, dims):
    return _bdot(a, b, dims), (a, b)


def _bdot_bwd(dims, res, g):
    a, b = res
    da_dims, db_dims, a_swapped, b_swapped = _DIMS_T[dims]
    da = _bdot(b, g, da_dims) if a_swapped else _bdot(g, b, da_dims)
    db = _bdot(g, a, db_dims) if b_swapped else _bdot(a, g, db_dims)
    return da, db


_bdot.defvjp(_bdot_fwd, _bdot_bwd)


@functools.partial(jax.custom_vjp, nondiff_argnums=(1,))
def _expand_heads(v, width):
    return _split_dot(v, _head_matrix(width), NN)


def _head_matrix(width):
    hr = lax.broadcasted_iota(jnp.int32, (LANES, SSD_HEADS * width), 0)
    hc = lax.broadcasted_iota(jnp.int32, (LANES, SSD_HEADS * width), 1)
    return (hc // width == hr).astype(BF16)


def _split_dot(v, e, dims):
    hi = v.astype(BF16)
    lo = (v - hi.astype(F32)).astype(BF16)
    return _dot(hi, e, dims) + _dot(lo, e, dims)


_expand_heads.defvjp(lambda v, width: (_expand_heads(v, width), None),
                     lambda width, _, g: (_split_dot(g, _head_matrix(width), NT),))


def _ssd_chunk_fn(rev, xs, dtr, bms, cms, st, alog, dtb):
    q = dtr.shape[0]
    hd, per = SSD_HEAD_DIM, SSD_HEADS // SSD_GROUPS
    gw = per * hd
    r = lax.broadcasted_iota(jnp.int32, (q, q), 0)
    c = lax.broadcasted_iota(jnp.int32, (q, q), 1)
    sgn = 1 - 2 * rev
    tri = ((r - c) * sgn >= 0).astype(F32)
    r4 = lax.broadcasted_iota(jnp.int32, (q, per * q), 0)
    c4 = lax.broadcasted_iota(jnp.int32, (q, per * q), 1) % q
    mask4 = (r4 - c4) * sgn >= 0
    bdr = lax.broadcasted_iota(jnp.int32, (per * q, gw), 0) // q
    bdc = lax.broadcasted_iota(jnp.int32, (per * q, gw), 1) // hd
    diag = bdr == bdc
    dt = _softplus(dtr + dtb)
    dta = dt * (-jnp.exp(alog))
    cs = _dot(tri, dta, NN, HI)
    cs_t = cs.T
    tot = jnp.sum(dta, axis=0, keepdims=True)
    dt_x = _expand_heads(dt, hd)
    in_x = _expand_heads(jnp.exp(cs), hd)
    out_x = _expand_heads(jnp.exp(tot - cs), hd)
    ys, outs = [], []
    for g in range(SSD_GROUPS):
        bg, cg = bms[g], cms[g]
        heads = range(per * g, per * (g + 1))
        lanes = slice(gw * g, gw * (g + 1))
        scores = _bdot(cg, bg, NT)
        col = jnp.concatenate([jnp.broadcast_to(cs[:, h:h + 1], (q, q)) for h in heads], axis=1)
        row = jnp.concatenate([cs_t[h:h + 1, :] for h in heads], axis=1)
        seg = jnp.where(mask4, jnp.exp(jnp.where(mask4, col - row, 0.0)), 0.0)
        mcat = jnp.concatenate([scores] * per, axis=1) * seg
        xdt = xs[g] * dt_x[:, lanes]
        blocks = jnp.where(diag, jnp.concatenate([xdt] * per, axis=0), 0.0)
        y = _bdot(mcat, blocks, NN) + in_x[:, lanes] * _bdot(cg, st[g], NT)
        new = _bdot(xdt * out_x[:, lanes], bg, TN)
        keep = jnp.concatenate([jnp.exp(tot[:, h:h + 1]) * st[g][hd * j:hd * (j + 1), :] for j, h in enumerate(heads)], axis=0)
        ys.append(y)
        outs.append(keep + new)
    return ys, outs


def _ssd_scan(xs, bm, cm, dtr, alog2, dtb2, bl, name):
    n = xs.shape[0]
    q = SSD_CHUNK
    nc = n // bl // q
    hd, ns = SSD_HEAD_DIM, SSD_STATE
    gw = SSD_HEADS // SSD_GROUPS * hd

    def body(xs_ref, b_ref, c_ref, dt_ref, al_ref, db_ref, y_ref, sv_ref, st_ref):
        d, i = pl.program_id(0), pl.program_id(2)

        @pl.when(i == 0)
        def _():
            st_ref[...] = jnp.zeros(st_ref.shape, F32)

        st = [st_ref[gw * g:gw * (g + 1), :] for g in range(SSD_GROUPS)]
        sv_ref[...] = st_ref[...]
        xl = [xs_ref[:, gw * g:gw * (g + 1)] for g in range(SSD_GROUPS)]
        bms = [b_ref[:, ns * g:ns * (g + 1)] for g in range(SSD_GROUPS)]
        cms = [c_ref[:, ns * g:ns * (g + 1)] for g in range(SSD_GROUPS)]
        ys, outs = _ssd_chunk_fn(d, xl, dt_ref[...], bms, cms, st, al_ref[...], db_ref[...])
        for g in range(SSD_GROUPS):
            st_ref[gw * g:gw * (g + 1), :] = outs[g]
            y_ref[:, gw * g:gw * (g + 1)] = ys[g]

    def rowblk(d, s, i):
        return s * nc + i + d * (nc - 1 - 2 * i)

    return pl.pallas_call(
        body, name=name, grid=(2, bl, nc),
        in_specs=[pl.BlockSpec((q, SSD_HEADS * hd), lambda d, s, i: (rowblk(d, s, i), 0)),
                  pl.BlockSpec((q, SSD_GROUPS * ns), lambda d, s, i: (rowblk(d, s, i), 0)),
                  pl.BlockSpec((q, SSD_GROUPS * ns), lambda d, s, i: (rowblk(d, s, i), 0)),
                  pl.BlockSpec((q, LANES), lambda d, s, i: (rowblk(d, s, i), d)),
                  pl.BlockSpec((None, 1, LANES), lambda d, s, i: (d, 0, 0)),
                  pl.BlockSpec((None, 1, LANES), lambda d, s, i: (d, 0, 0))],
        out_specs=[pl.BlockSpec((None, q, SSD_HEADS * hd), lambda d, s, i: (d, rowblk(d, s, i), 0)),
                   pl.BlockSpec((None, None, SSD_HEADS * hd, ns), lambda d, s, i: (d, rowblk(d, s, i), 0, 0))],
        out_shape=[SDS((2, n, SSD_HEADS * hd), F32), SDS((2, n // q, SSD_HEADS * hd, ns), F32)],
        scratch_shapes=[pltpu.VMEM((SSD_HEADS * hd, ns), F32)],
        compiler_params=_cp(("arbitrary",) * 3, VMEM_LIMIT),
    )(xs, bm, cm, dtr, alog2, dtb2)


def _ssd_scan_bwd(xs, bm, cm, dtr, alog2, dtb2, saved, dy, bl, name):
    n = xs.shape[0]
    q = SSD_CHUNK
    nc = n // bl // q
    hd, ns = SSD_HEAD_DIM, SSD_STATE
    gw = SSD_HEADS // SSD_GROUPS * hd

    def body(xs_ref, b_ref, c_ref, dt_ref, al_ref, db_ref, sv_ref, dy_ref,
             dxs_ref, dbm_ref, dcm_ref, ddt_ref, dal_ref, ddb_ref, ds_ref):
        d, s, i = pl.program_id(0), pl.program_id(1), pl.program_id(2)

        @pl.when(i == 0)
        def _():
            ds_ref[...] = jnp.zeros(ds_ref.shape, F32)

        xl = [xs_ref[:, gw * g:gw * (g + 1)] for g in range(SSD_GROUPS)]
        bms = [b_ref[:, ns * g:ns * (g + 1)] for g in range(SSD_GROUPS)]
        cms = [c_ref[:, ns * g:ns * (g + 1)] for g in range(SSD_GROUPS)]
        st = [sv_ref[gw * g:gw * (g + 1), :] for g in range(SSD_GROUPS)]
        fn = functools.partial(_ssd_chunk_fn, d)
        _, vjp = jax.vjp(fn, xl, dt_ref[...], bms, cms, st, al_ref[...], db_ref[...])
        dys = [dy_ref[:, gw * g:gw * (g + 1)] for g in range(SSD_GROUPS)]
        dso = [ds_ref[gw * g:gw * (g + 1), :] for g in range(SSD_GROUPS)]
        dxl, ddt, dbg, dcg, dst, dal, ddb = vjp((dys, dso))
        for g in range(SSD_GROUPS):
            ds_ref[gw * g:gw * (g + 1), :] = dst[g]
            dxs_ref[:, gw * g:gw * (g + 1)] = dxl[g]
            dbm_ref[:, ns * g:ns * (g + 1)] = dbg[g]
            dcm_ref[:, ns * g:ns * (g + 1)] = dcg[g]
        ddt_ref[...] = ddt
        _acc_rows((dal_ref, ddb_ref), (dal, ddb), jnp.logical_and(s == 0, i == 0))

    def rowblk(d, s, i):
        return s * nc + (nc - 1 - i) + d * (2 * i - (nc - 1))

    row = lambda d, s, i: (rowblk(d, s, i), 0)
    drow = lambda d, s, i: (d, rowblk(d, s, i), 0)
    dfix = lambda d, s, i: (d, 0, 0)
    dcol = lambda d, s, i: (rowblk(d, s, i), d)
    return pl.pallas_call(
        body, name=name, grid=(2, bl, nc),
        in_specs=[pl.BlockSpec((q, SSD_HEADS * hd), row), pl.BlockSpec((q, SSD_GROUPS * ns), row),
                  pl.BlockSpec((q, SSD_GROUPS * ns), row), pl.BlockSpec((q, LANES), dcol),
                  pl.BlockSpec((None, 1, LANES), dfix), pl.BlockSpec((None, 1, LANES), dfix),
                  pl.BlockSpec((None, None, SSD_HEADS * hd, ns), lambda d, s, i: (d, rowblk(d, s, i), 0, 0)),
                  pl.BlockSpec((q, SSD_HEADS * hd), row)],
        out_specs=[pl.BlockSpec((None, q, SSD_HEADS * hd), drow), pl.BlockSpec((None, q, SSD_GROUPS * ns), drow),
                   pl.BlockSpec((None, q, SSD_GROUPS * ns), drow), pl.BlockSpec((q, LANES), dcol),
                   pl.BlockSpec((None, 1, LANES), dfix), pl.BlockSpec((None, 1, LANES), dfix)],
        out_shape=[SDS((2, n, SSD_HEADS * hd), F32), SDS((2, n, SSD_GROUPS * ns), F32), SDS((2, n, SSD_GROUPS * ns), F32),
                   SDS((n, 2 * LANES), F32), SDS((2, 1, LANES), F32), SDS((2, 1, LANES), F32)],
        scratch_shapes=[pltpu.VMEM((SSD_HEADS * hd, ns), F32)],
        compiler_params=_cp(("arbitrary",) * 3, VMEM_LIMIT),
    )(xs, bm, cm, dtr, alog2, dtb2, saved, dy)


def _s5_consts():
    t, ch, p = S5_T, S5_CH, S5_STATE
    lane = lax.broadcasted_iota(jnp.int32, (1, 2 * p), 1)
    pr = lax.broadcasted_iota(jnp.int32, (p, 2 * p), 0)
    pc = lax.broadcasted_iota(jnp.int32, (p, 2 * p), 1)
    cr = lax.broadcasted_iota(jnp.int32, (ch, t * ch), 0)
    cc = lax.broadcasted_iota(jnp.int32, (ch, t * ch), 1)
    return dict(
        left=lane < p,
        sg=jnp.where(lane < p, -1.0, 1.0).astype(F32),
        dup=(pc % p == pr).astype(F32),
        dup_l=(pc == pr).astype(F32),
        dup_r=(pc == pr + p).astype(F32),
        rep=(cc % ch == cr).astype(F32),
        rep0=(cc == cr).astype(F32),
    )


def _s5_mats(k, rev, lr, li, ls, bre, bim, cre, cim):
    t = S5_T
    step = jnp.exp(ls)
    lr2 = jnp.sum(lr * k["dup"], axis=0, keepdims=True)
    li2 = jnp.sum(li * k["dup"], axis=0, keepdims=True)

    def erow(d):
        ang = (d * step) * li2
        return jnp.exp((d * step) * lr2) * jnp.where(k["left"], jnp.cos(ang), jnp.sin(ang))

    es = [erow(d) for d in range(t + 1)]
    mag = jnp.exp(step * lr)
    ar, ai = mag * jnp.cos(step * li), mag * jnp.sin(step * li)
    den = lr * lr + li * li
    zr = ((ar - 1.0) * lr + ai * li) / den
    zi = (ai * lr - (ar - 1.0) * li) / den
    bbr = zr * bre - zi * bim
    bbi = zr * bim + zi * bre
    bt1 = _dot(bbr, k["dup"], TN, HI)
    bt2 = _dot(bbi, k["dup"], TN, HI)
    bst = _dot(bbr, k["dup_l"], TN, HI) - _dot(bbi, k["dup_r"], TN, HI)
    c1 = _dot(cre, k["dup"], NN, HI)
    c2 = _dot(cim, k["dup"], NN, HI)
    sg = k["sg"]
    ce = [e * c1 + sg * _swap(e) * c2 for e in es]
    lags = range(t - 1, -1, -1) if rev else range(t)
    kt = _dot(bst, jnp.concatenate([ce[d] for d in lags], axis=0), NT, HI)
    toep = jnp.concatenate([_lane_shift(kt, -S5_CH * (t - 1 - s) if rev else S5_CH * s) for s in range(t)], axis=0)
    w_out =jnp.concatenate([ce[(t - qq) if rev else (qq + 1)] * (-sg) for qq in range(t)], axis=0)
    w_st = jnp.concatenate(
        [(lambda e: e * bt1 + sg * _swap(e) * bt2)(es[s if rev else (t - 1 - s)]) for s in range(t)], axis=0)
    return toep, w_out, w_st, es[t]


def _cmul_row(k, e, z):
    es = _swap(e)
    return z * jnp.where(k["left"], e, es) + _swap(z) * (k["sg"] * jnp.where(k["left"], es, e))


def _s5_dir(k, rev, nck, x, mats):
    toep, w_out, w_st, a_t = mats
    acc = _dot(x, w_st)
    e = a_t
    kk = 1
    sign = -1 if rev else 1
    while kk < nck:
        acc = acc + _cmul_row(k, e, _shift(acc, sign * kk, nck))
        e = _cmul_row(k, e, e)
        kk *= 2
    prev = _shift(acc, sign, nck)
    return _dot(x, toep) + _dot(prev, w_out, NT)


def _s5_group_fn(nck, x, pf, pb, bre, bim, dcol, wv, wg, bv, bg):
    k = _s5_consts()
    t = S5_T
    y = x * jnp.sum(dcol * k["rep"], axis=0, keepdims=True)
    for rev, (lr, li, ls, cre, cim) in ((False, pf), (True, pb)):
        y = y + _s5_dir(k, rev, nck, x, _s5_mats(k, rev, lr, li, ls, bre, bim, cre, cim))
    gy = jax.nn.gelu(y)
    def kron_eye(w16):
        wide = _dot(w16, k["rep0"], NN, HI)
        return jnp.concatenate([_lane_shift(wide, S5_CH * qq) for qq in range(t)], axis=0)

    kv, kg = kron_eye(wv), kron_eye(wg)
    val =_dot(gy, kv) + jnp.sum(bv * k["rep"], axis=0, keepdims=True)
    gate = _dot(gy, kg) + jnp.sum(bg * k["rep"], axis=0, keepdims=True)
    return val * _sigmoid(gate)


def _s5_specs(r):
    p, ch = S5_STATE, S5_CH
    g3 = lambda i: (i, 0, 0)
    col = pl.BlockSpec((None, p, 1), g3)
    one = pl.BlockSpec((None, 1, 1), g3)
    cmat = pl.BlockSpec((None, ch, p), g3)
    bmat = pl.BlockSpec((None, p, ch), g3)
    ccol = pl.BlockSpec((None, ch, 1), g3)
    sq = pl.BlockSpec((None, ch, ch), g3)
    xs = pl.BlockSpec((None, r, S5_T * ch), g3)
    specs = [xs, col, col, one, cmat, cmat, col, col, one, cmat, cmat, bmat, bmat, ccol, sq, sq, ccol, ccol]
    return specs


def _s5_unpack(vals):
    x = vals[0]
    pf = tuple(vals[1:6])
    pb = tuple(vals[6:11])
    bre, bim, dcol, wv, wg, bv, bg = vals[11:18]
    return x, pf, pb, bre, bim, dcol, wv, wg, bv, bg


def _s5_fwd(args, nck, name):
    x = args[0]
    ng, r, w = x.shape

    def body(*refs):
        vals = [ref[...] for ref in refs[:18]]
        refs[18][...] = _s5_group_fn(nck, *_s5_unpack(vals))

    specs = _s5_specs(r)
    return pl.pallas_call(
        body, name=name, grid=(ng,), in_specs=specs, out_specs=specs[0], out_shape=SDS(x.shape, F32),
        compiler_params=_cp(("arbitrary",), VMEM_LIMIT),
    )(*args)


def _s5_bwd(args, dy, nck, name):
    x = args[0]
    ng, r, w = x.shape

    def body(*refs):
        vals = [ref[...] for ref in refs[:18]]
        _, vjp = jax.vjp(lambda *v: _s5_group_fn(nck, *_s5_unpack(v)), *vals)
        grads = vjp(refs[18][...])
        for o_ref, gval in zip(refs[19:], grads):
            o_ref[...] = gval.astype(o_ref.dtype)

    specs = _s5_specs(r)
    return pl.pallas_call(
        body, name=name, grid=(ng,), in_specs=specs + [specs[0]], out_specs=specs,
        out_shape=[SDS(x.shape, BF16)] + [SDS(a.shape, F32) for a in args[1:]],
        compiler_params=_cp(("arbitrary",), VMEM_LIMIT),
    )(*args, dy)


def _mix_fn(yf, yb, xs, z, s5o, dvec, nw_ssd, nw_s5):
    hr = lax.broadcasted_iota(jnp.int32, (LANES, SSD_HEADS * SSD_HEAD_DIM), 0)
    hc = lax.broadcasted_iota(jnp.int32, (LANES, SSD_HEADS * SSD_HEAD_DIM), 1)
    expand = (hc // SSD_HEAD_DIM == hr).astype(F32)
    dch = jnp.sum(dvec * expand, axis=0, keepdims=True)
    y = (yf + yb + dch * xs) * (z * _sigmoid(z))
    return _rms(y, nw_ssd), _rms(s5o, nw_s5)


def _mix(y2, xs, z, s5o, dvec, nw_ssd, nw_s5, tm, name):
    n, c1 = xs.shape
    c2 = s5o.shape[1]

    def body(yf_ref, yb_ref, xs_ref, z_ref, s_ref, d_ref, n1_ref, n2_ref, o_ref):
        o1, o2 = _mix_fn(yf_ref[...], yb_ref[...], xs_ref[...], z_ref[...], s_ref[...], d_ref[...], n1_ref[...], n2_ref[...])
        o_ref[:, :c1] = o1.astype(BF16)
        o_ref[:, c1:] = o2.astype(BF16)

    row = lambda i: (i, 0)
    fix = lambda i: (0, 0)
    return pl.pallas_call(
        body, name=name, grid=(n // tm,),
        in_specs=[pl.BlockSpec((None, tm, c1), lambda i: (0, i, 0)), pl.BlockSpec((None, tm, c1), lambda i: (1, i, 0)),
                  pl.BlockSpec((tm, c1), row), pl.BlockSpec((tm, c1), row), pl.BlockSpec((tm, c2), row),
                  pl.BlockSpec((LANES, 1), fix), pl.BlockSpec((1, c1), fix), pl.BlockSpec((1, c2), fix)],
        out_specs=pl.BlockSpec((tm, c1 + c2), row), out_shape=SDS((n, c1 + c2), BF16),
        compiler_params=_cp(("arbitrary",), VMEM_LIMIT),
    )(y2, y2, xs, z, s5o, dvec, nw_ssd, nw_s5)


def _acc_rows(refs, vals, first):
    @pl.when(first)
    def _():
        for ref, v in zip(refs, vals):
            ref[...] = v

    @pl.when(jnp.logical_not(first))
    def _():
        for ref, v in zip(refs, vals):
            ref[...] += v


def _mix_bwd(y2, xs, z, s5o, dvec, nw_ssd, nw_s5, dmix, tm, name):
    n, c1 = xs.shape
    c2 = s5o.shape[1]

    def body(yf_ref, yb_ref, xs_ref, z_ref, s_ref, d_ref, n1_ref, n2_ref, dm_ref,
             dy_ref, dxs_ref, dz_ref, ds_ref, dd_ref, dn1_ref, dn2_ref):
        _, vjp = jax.vjp(_mix_fn, yf_ref[...], yb_ref[...], xs_ref[...], z_ref[...], s_ref[...], d_ref[...], n1_ref[...], n2_ref[...])
        dyf, _, dxs, dz, ds, dd, dn1, dn2 = vjp((dm_ref[:, :c1], dm_ref[:, c1:]))
        dy_ref[...] = dyf
        dxs_ref[...] = dxs
        dz_ref[...] = dz.astype(BF16)
        ds_ref[...] = ds
        _acc_rows((dd_ref, dn1_ref, dn2_ref), (dd, dn1, dn2), pl.program_id(0) == 0)

    row = lambda i: (i, 0)
    fix = lambda i: (0, 0)
    return pl.pallas_call(
        body, name=name, grid=(n // tm,),
        in_specs=[pl.BlockSpec((None, tm, c1), lambda i: (0, i, 0)), pl.BlockSpec((None, tm, c1), lambda i: (1, i, 0)),
                  pl.BlockSpec((tm, c1), row), pl.BlockSpec((tm, c1), row), pl.BlockSpec((tm, c2), row),
                  pl.BlockSpec((LANES, 1), fix), pl.BlockSpec((1, c1), fix), pl.BlockSpec((1, c2), fix),
                  pl.BlockSpec((tm, c1 + c2), row)],
        out_specs=[pl.BlockSpec((tm, c1), row), pl.BlockSpec((tm, c1), row), pl.BlockSpec((tm, c1), row), pl.BlockSpec((tm, c2), row),
                   pl.BlockSpec((LANES, 1), fix), pl.BlockSpec((1, c1), fix), pl.BlockSpec((1, c2), fix)],
        out_shape=[SDS((n, c1), F32), SDS((n, c1), F32), SDS((n, c1), BF16), SDS((n, c2), F32),
                   SDS((LANES, 1), F32), SDS((1, c1), F32), SDS((1, c2), F32)],
        compiler_params=_cp(("arbitrary",), VMEM_LIMIT),
    )(y2, y2, xs, z, s5o, dvec, nw_ssd, nw_s5, dmix)


def _final_loss(h2, nw, tgt, tm, name):
    n, d = h2.shape

    def loss_fn(h, w, t):
        e = _rms(h, w) - t
        return (0.5 / d) * jnp.sum(e * e)

    def body(h_ref, w_ref, t_ref, l_ref, dh_ref, dw_ref):
        loss, (dh, dw) = jax.value_and_grad(loss_fn, argnums=(0, 1))(h_ref[...], w_ref[...], t_ref[...])
        dh_ref[...] = dh
        _acc_rows((l_ref, dw_ref), (jnp.full((1, LANES), loss, F32), dw), pl.program_id(0) == 0)

    row = lambda i: (i, 0)
    fix = lambda i: (0, 0)
    return pl.pallas_call(
        body, name=name, grid=(n // tm,),
        in_specs=[pl.BlockSpec((tm, d), row), pl.BlockSpec((1, d), fix), pl.BlockSpec((tm, d), row)],
        out_specs=[pl.BlockSpec((1, LANES), fix), pl.BlockSpec((tm, d), row), pl.BlockSpec((1, d), fix)],
        out_shape=[SDS((1, LANES), F32), SDS((n, d), F32), SDS((1, d), F32)],
        compiler_params=_cp(("arbitrary",), VMEM_LIMIT),
    )(h2, nw, tgt)


def _norm_bwd(x, nw, dhn, dres, tm, name):
    n, d = x.shape

    def body(x_ref, w_ref, g_ref, r_ref, dx_ref, dw_ref):
        _, vjp = jax.vjp(_rms, x_ref[...], w_ref[...])
        dx, dw = vjp(g_ref[...])
        dx_ref[...] = r_ref[...] + dx
        _acc_rows((dw_ref,), (dw,), pl.program_id(0) == 0)

    row = lambda i: (i, 0)
    fix = lambda i: (0, 0)
    return pl.pallas_call(
        body, name=name, grid=(n // tm,),
        in_specs=[pl.BlockSpec((tm, d), row), pl.BlockSpec((1, d), fix), pl.BlockSpec((tm, d), row), pl.BlockSpec((tm, d), row)],
        out_specs=[pl.BlockSpec((tm, d), row), pl.BlockSpec((1, d), fix)],
        out_shape=[SDS((n, d), F32), SDS((1, d), F32)],
        compiler_params=_cp(("arbitrary",), VMEM_LIMIT),
    )(x, nw, dhn, dres)


def _row_tile(n, cap=512):
    for t in range(min(cap, n) // 8 * 8, 7, -8):
        if n % t == 0:
            return t
    return n


def _sum_lead(a, name):
    kk, n, c = a.shape
    tm = _row_tile(n)

    def body(a_ref, o_ref):
        acc = a_ref[0].astype(F32)
        for i in range(1, kk):
            acc = acc + a_ref[i].astype(F32)
        o_ref[...] = acc

    return pl.pallas_call(
        body, name=name, grid=(n // tm,),
        in_specs=[pl.BlockSpec((kk, tm, c), lambda i: (0, i, 0))],
        out_specs=pl.BlockSpec((tm, c), lambda i: (i, 0)), out_shape=SDS((n, c), F32),
        compiler_params=_cp(("arbitrary",), VMEM_LIMIT),
    )(a)


def _adamw(w, g, m, v, name):
    n, c = w.shape
    tm = _row_tile(n)

    def body(w_ref, g_ref, m_ref, v_ref, d_ref, nm_ref, nv_ref):
        gv = g_ref[...]
        mn = ADAM_B1 * m_ref[...] + (1.0 - ADAM_B1) * gv
        vn = ADAM_B2 * v_ref[...] + (1.0 - ADAM_B2) * jnp.square(gv)
        m_hat = mn / (1.0 - ADAM_B1 ** ADAM_STEP)
        v_hat = vn / (1.0 - ADAM_B2 ** ADAM_STEP)
        d_ref[...] = -ADAM_LR * (m_hat / (jnp.sqrt(v_hat) + ADAM_EPS) + ADAM_WD * w_ref[...])
        nm_ref[...] = mn
        nv_ref[...] = vn

    spec = pl.BlockSpec((tm, c), lambda i: (i, 0))
    return pl.pallas_call(
        body, name=name, grid=(n // tm,), in_specs=[spec] * 4, out_specs=[spec] * 3,
        out_shape=[SDS((n, c), F32)] * 3, compiler_params=_cp(("arbitrary",), VMEM_LIMIT),
    )(w, g, m, v)


ANY = pl.BlockSpec(memory_space=pl.ANY)


def _me():
    return lax.axis_index("x"), lax.axis_index("y"), lax.axis_index("c")


def _gather_xy(split, whole, name):
    ns, cnt = len(split), len(split) + len(whole)

    def body(*refs):
        src, dst = refs[:cnt], refs[cnt:2 * cnt]
        send, recv = refs[2 * cnt:]
        x, y, c = _me()
        mine = 2 * x + y
        chips = [(1 - x, y), (x, 1 - y), (1 - x, 1 - y)]

        def ici(a, j, slot):
            px, py = chips[j]
            if a < ns:
                s_ref, d_ref = src[a].at[c], dst[a].at[slot].at[c]
            else:
                s_ref, d_ref = src[a], dst[a].at[slot]
            return pltpu.make_async_remote_copy(s_ref, d_ref, send.at[3 * a + j], recv.at[3 * a + j],
                                                device_id=(px, py, c), device_id_type=MESH)

        def d2d(a, j, half):
            px, py = chips[j]
            ref = dst[a].at[2 * px + py].at[half]
            return pltpu.make_async_remote_copy(ref, ref, send.at[3 * cnt + 3 * a + j], recv.at[3 * cnt + 3 * a + j],
                                                device_id=(x, y, 1 - c), device_id_type=MESH)

        def own(a):
            return pltpu.make_async_remote_copy(src[a], dst[a].at[mine], send.at[nsem - cnt + a], recv.at[nsem - cnt + a],
                                                device_id=(x, y, 1 - c), device_id_type=MESH)

        started = []
        for a in range(cnt):
            cp = own(a)
            cp.start()
            started.append(cp)
            for j in range(3):
                cp = ici(a, j, mine)
                cp.start()
                started.append(cp)
        for a in range(cnt):
            for j, (px, py) in enumerate(chips):
                ici(a, j, 2 * px + py).wait_recv()
                if a < ns:
                    cp = d2d(a, j, c)
                    cp.start()
                    started.append(cp)
        for a in range(ns):
            for j in range(3):
                d2d(a, j, 1 - c).wait_recv()
        for a in range(cnt):
            own(a).wait_recv()
        for cp in started:
            cp.wait_send()

    nsem = 3 * cnt + 3 * ns + cnt
    return pl.pallas_call(
        body, name=name, in_specs=[ANY] * cnt, out_specs=[ANY] * cnt,
        out_shape=[SDS((4,) + s.shape, s.dtype) for s in split + whole],
        scratch_shapes=[pltpu.SemaphoreType.DMA((nsem,)), pltpu.SemaphoreType.DMA((nsem,))],
    )(*split, *whole)


HBM = pl.BlockSpec(memory_space=pltpu.HBM)
SEM = pl.BlockSpec(memory_space=pltpu.SEMAPHORE)
DATAFLOW = pltpu.SideEffectType.DATAFLOW_SIDE_EFFECTING


def _whole_copies(srcs, dsts, sends, recvs):
    x, y, c = _me()
    peers = [(1 - x, y, c), (x, 1 - y, c), (1 - x, 1 - y, c), (x, y, 1 - c)]
    return [pltpu.make_async_remote_copy(srcs[a], dsts[a].at[2 * x + y], sends[4 * a + j], recvs[4 * a + j],
                                         device_id=peer, device_id_type=MESH)
            for a in range(len(srcs)) for j, peer in enumerate(peers)]


def _scatter_copies(srcs, dsts, sends, recvs):
    x, y, c = _me()
    chips = [(1 - x, y), (x, 1 - y), (1 - x, 1 - y)]
    return [pltpu.make_async_remote_copy(srcs[a].at[2 * px + py], dsts[a].at[2 * x + y], sends[3 * a + j], recvs[3 * a + j],
                                         device_id=(px, py, c), device_id_type=MESH)
            for a in range(len(srcs)) for j, (px, py) in enumerate(chips)]


def _copies_start(copies, per, shards, after, name):
    cnt = len(shards)
    ncp = per * cnt

    def body(*refs):
        srcs, lands = refs[:cnt], refs[cnt:2 * cnt]
        outs = refs[2 * cnt + 1:]
        for cp in copies(srcs, lands, outs[:ncp], outs[ncp:2 * ncp]):
            cp.start()
        outs[-1][...] = jnp.zeros_like(outs[-1])

    lands = [lax.empty((4,) + (s.shape if per == 4 else s.shape[1:]), s.dtype) for s in shards]
    ops = [pltpu.with_memory_space_constraint(a, pltpu.HBM) for a in list(shards) + lands]
    res = pl.pallas_call(
        body, name=name, in_specs=[HBM] * (2 * cnt) + [ANY],
        out_shape=tuple([pltpu.SemaphoreType.DMA(())] * (2 * ncp) + [pltpu.HBM(a.shape, a.dtype) for a in ops] + [SDS((8, LANES), F32)]),
        out_specs=tuple([SEM] * (2 * ncp) + [HBM] * (2 * cnt) + [pl.BlockSpec(memory_space=pltpu.VMEM)]),
        input_output_aliases={i: 2 * ncp + i for i in range(2 * cnt)},
        compiler_params=pltpu.CompilerParams(has_side_effects=DATAFLOW),
    )(*ops, after)
    return res[:2 * ncp], res[2 * ncp:2 * ncp + cnt], res[2 * ncp + cnt:2 * ncp + 2 * cnt], res[-1]


def _copies_wait(copies, per, handle, after, name):
    sems, srcs, lands, _ = handle
    cnt = len(srcs)
    ncp = per * cnt

    def body(*refs):
        sem_refs = refs[2 * cnt:2 * cnt + 2 * ncp]
        for cp in copies(refs[:cnt], refs[cnt:2 * cnt], sem_refs[:ncp], sem_refs[ncp:]):
            cp.wait_send()
            cp.wait_recv()

    res = pl.pallas_call(
        body, name=name, in_specs=[HBM] * (2 * cnt) + [SEM] * (2 * ncp) + [ANY],
        out_shape=tuple(pltpu.HBM(a.shape, a.dtype) for a in list(srcs) + list(lands)),
        out_specs=tuple([HBM] * (2 * cnt)), input_output_aliases={i: i for i in range(2 * cnt)},
        compiler_params=pltpu.CompilerParams(has_side_effects=DATAFLOW),
    )(*srcs, *lands, *sems, after)
    return list(res[cnt:])


def _swap_sibling(parts, pick, name):
    cnt = len(parts)

    def body(*refs):
        src, dst = refs[:cnt], refs[cnt:2 * cnt]
        send, recv = refs[2 * cnt:]
        x, y, c = _me()
        cps = []
        for a in range(cnt):
            cp = pltpu.make_async_remote_copy(src[a].at[1 - c] if pick else src[a], dst[a], send.at[a], recv.at[a],
                                              device_id=(x, y, 1 - c), device_id_type=MESH)
            cp.start()
            cps.append(cp)
        for cp in cps:
            cp.wait()

    return pl.pallas_call(
        body, name=name, in_specs=[ANY] * cnt, out_specs=[ANY] * cnt,
        out_shape=[SDS(p.shape[1:] if pick else p.shape, p.dtype) for p in parts],
        scratch_shapes=[pltpu.SemaphoreType.DMA((cnt,)), pltpu.SemaphoreType.DMA((cnt,))],
    )(*parts)


def _scatter_xy(parts, name):
    cnt = len(parts)

    def body(*refs):
        src, dst = refs[:cnt], refs[cnt:2 * cnt]
        send, recv, loc = refs[2 * cnt:]
        x, y, c = _me()
        mine = 2 * x + y
        chips = [(1 - x, y), (x, 1 - y), (1 - x, 1 - y)]
        local = []
        for a in range(cnt):
            cp = pltpu.make_async_copy(src[a].at[mine], dst[a].at[mine], loc.at[a])
            cp.start()
            local.append(cp)
        sends = []
        for a in range(cnt):
            for j, (px, py) in enumerate(chips):
                cp = pltpu.make_async_remote_copy(src[a].at[2 * px + py], dst[a].at[mine], send.at[3 * a + j], recv.at[3 * a + j],
                                                  device_id=(px, py, c), device_id_type=MESH)
                cp.start()
                sends.append(cp)
        for a in range(cnt):
            for j, (px, py) in enumerate(chips):
                pltpu.make_async_remote_copy(src[a].at[mine], dst[a].at[2 * px + py], send.at[3 * a + j], recv.at[3 * a + j],
                                             device_id=(px, py, c), device_id_type=MESH).wait_recv()
        for cp in sends:
            cp.wait_send()
        for cp in local:
            cp.wait()

    return pl.pallas_call(
        body, name=name, in_specs=[ANY] * cnt, out_specs=[ANY] * cnt,
        out_shape=[SDS(p.shape, p.dtype) for p in parts],
        scratch_shapes=[pltpu.SemaphoreType.DMA((3 * cnt,)), pltpu.SemaphoreType.DMA((3 * cnt,)), pltpu.SemaphoreType.DMA((cnt,))],
    )(*parts)


def _bcast_all(buf, name):
    def body(src, dst, send, recv, loc):
        x, y, c = _me()
        mine = 4 * x + 2 * y + c
        own = pltpu.make_async_copy(src, dst.at[mine], loc)
        own.start()
        sends = []
        for k in range(1, 8):
            px, py, pc = x ^ (k >> 2), y ^ ((k >> 1) & 1), c ^ (k & 1)
            cp = pltpu.make_async_remote_copy(src, dst.at[mine], send.at[k - 1], recv.at[k - 1],
                                              device_id=(px, py, pc), device_id_type=MESH)
            cp.start()
            sends.append(cp)
        for k in range(1, 8):
            px, py, pc = x ^ (k >> 2), y ^ ((k >> 1) & 1), c ^ (k & 1)
            pltpu.make_async_remote_copy(src, dst.at[4 * px + 2 * py + pc], send.at[k - 1], recv.at[k - 1],
                                         device_id=(px, py, pc), device_id_type=MESH).wait_recv()
        for cp in sends:
            cp.wait_send()
        own.wait()

    return pl.pallas_call(
        body, name=name, in_specs=[ANY], out_specs=ANY, out_shape=SDS((8,) + buf.shape, buf.dtype),
        scratch_shapes=[pltpu.SemaphoreType.DMA((7,)), pltpu.SemaphoreType.DMA((7,)), pltpu.SemaphoreType.DMA(())],
    )(buf)


def _sum_slots(recv, own, chip, name):
    kk, n, c = recv.shape
    tm = _row_tile(n)

    def body(chip_ref, r_ref, o_ref, out_ref):
        acc = None
        for j in range(kk):
            v = jnp.where(chip_ref[0] == j, o_ref[...], r_ref[j]).astype(F32)
            acc = v if acc is None else acc + v
        out_ref[...] = acc

    grid_spec = pltpu.PrefetchScalarGridSpec(
        num_scalar_prefetch=1, grid=(n // tm,),
        in_specs=[pl.BlockSpec((kk, tm, c), lambda i, chip_ref: (0, i, 0)),
                  pl.BlockSpec((None, tm, c), lambda i, chip_ref: (chip_ref[0], i, 0))],
        out_specs=pl.BlockSpec((tm, c), lambda i, chip_ref: (i, 0)))
    return pl.pallas_call(body, name=name, grid_spec=grid_spec, out_shape=SDS((n, c), F32),
                          compiler_params=_cp(("arbitrary",), VMEM_LIMIT))(chip.reshape(1), recv, own)


def _add_half(parts, got, core, dtype, name):
    shp = got.shape
    a2, b2 = parts.reshape(2, -1, shp[-1]), got.reshape(-1, shp[-1])
    n, c = b2.shape
    tm = _row_tile(n, 256)

    def body(core_ref, a_ref, b_ref, o_ref):
        o_ref[...] = (a_ref[...] + b_ref[...]).astype(dtype)

    spec = pl.BlockSpec((tm, c), lambda i, core_ref: (i, 0))
    grid_spec = pltpu.PrefetchScalarGridSpec(
        num_scalar_prefetch=1, grid=(n // tm,),
        in_specs=[pl.BlockSpec((None, tm, c), lambda i, core_ref: (core_ref[0], i, 0)), spec], out_specs=spec)
    return pl.pallas_call(body, name=name, grid_spec=grid_spec, out_shape=SDS((n, c), dtype),
                          compiler_params=_cp(("arbitrary",), VMEM_LIMIT))(core.reshape(1), a2, b2).reshape(shp)


def _x_layout(u, name):
    n, c = u.shape
    t, ch = S5_T, S5_CH
    gb = LANES // ch
    rows = min(64, n // t)

    def body(u_ref, o_ref):
        for s in range(t):
            us = u_ref[pl.ds(s, rows, stride=t), :]
            for g in range(gb):
                o_ref[g, :, ch * s:ch * (s + 1)] = us[:, ch * g:ch * (g + 1)]

    return pl.pallas_call(
        body, name=name, grid=(n // (rows * t), c // LANES),
        in_specs=[pl.BlockSpec((rows * t, LANES), lambda i, j: (i, j))],
        out_specs=pl.BlockSpec((gb, rows, t * ch), lambda i, j: (j, i, 0)),
        out_shape=SDS((c // ch, n // t, t * ch), F32),
        compiler_params=_cp(("arbitrary", "arbitrary"), VMEM_LIMIT),
    )(u)


def _token_layout(xg, name):
    ng, r, w = xg.shape
    t, ch = S5_T, S5_CH
    gb = LANES // ch
    rows = min(64, r)

    def body(x_ref, o_ref):
        for s in range(t):
            parts = [x_ref[g, :, ch * s:ch * (s + 1)].astype(F32) for g in range(gb)]
            o_ref[pl.ds(s, rows, stride=t), :] = jnp.concatenate(parts, axis=1)

    return pl.pallas_call(
        body, name=name, grid=(r // rows, ng // gb),
        in_specs=[pl.BlockSpec((gb, rows, w), lambda i, j: (j, i, 0))],
        out_specs=pl.BlockSpec((rows * t, LANES), lambda i, j: (i, j)),
        out_shape=SDS((r * t, ng * ch), F32),
        compiler_params=_cp(("arbitrary", "arbitrary"), VMEM_LIMIT),
    )(xg)


def _pad_lanes(a, lanes=LANES):
    return jnp.pad(a, ((0, 0), (0, lanes - a.shape[1])))


def _local_step(x, tgt, p, bl, late, early):
    p = dict(p)
    n, d = x.shape
    sw = SSD_HEADS * SSD_HEAD_DIM
    gn = SSD_GROUPS * SSD_STATE
    tm = min(n, 512)
    tm_ffn = min(n, 256)
    nck = n // bl // S5_T
    s5w = S5_GROUPS * S5_CH

    w_in = p["w_in"]
    o1, o2, o3, o4 = sw, sw + sw, sw + sw + gn, sw + sw + 2 * gn
    w_z, w_xs, w_b, w_c = w_in[:, :o1], w_in[:, o1:o2], w_in[:, o2:o3], w_in[:, o3:o4]
    w_dt = jnp.concatenate([_pad_lanes(w_in[:, o4:o4 + SSD_HEADS]), _pad_lanes(w_in[:, o4 + SSD_HEADS:o4 + 2 * SSD_HEADS])], 1)
    w_u = w_in[:, o4 + 2 * SSD_HEADS:]
    in_ws = [w_z, w_xs, w_b, w_c, w_dt, w_u]
    cw, cb_ = p["ssd_conv_w"], p["ssd_conv_b"]
    conv_parts = [(cw[:, :sw], cb_[:, :sw]), (cw[:, sw:sw + gn], cb_[:, sw:sw + gn]), (cw[:, sw + gn:], cb_[:, sw + gn:])]
    alog2 = jnp.stack([_pad_lanes(p["ssd_a_log_fwd"]), _pad_lanes(p["ssd_a_log_bwd"])])
    dtb2 = jnp.stack([_pad_lanes(p["ssd_dt_bias_fwd"]), _pad_lanes(p["ssd_dt_bias_bwd"])])
    dvec = _pad_lanes(p["ssd_d"]).reshape(LANES, 1)

    hn, z, xs_pre, b_pre, c_pre, dtr, u = _norm_matmul(x, p["norm_mix_w"], in_ws, tm, "in_proj")
    pres = [xs_pre, b_pre, c_pre]
    acts = [_conv_silu(pre, w, b, bl, min(256, pre.shape[1]), f"ssd_conv_{i}") for i, (pre, (w, b)) in enumerate(zip(pres, conv_parts))]
    xs_a, b_a, c_a = acts
    y2, saved = _ssd_scan(xs_a, b_a, c_a, dtr, alog2, dtb2, bl, "ssd_scan")

    def col(a):
        return a.reshape(a.shape + (1,))

    s5_params = [
        col(p["s5_lambda_re_fwd"]), col(p["s5_lambda_im_fwd"]), p["s5_log_step_fwd"].reshape(S5_GROUPS, 1, 1), p["s5_c_re_fwd"], p["s5_c_im_fwd"],
        col(p["s5_lambda_re_bwd"]), col(p["s5_lambda_im_bwd"]), p["s5_log_step_bwd"].reshape(S5_GROUPS, 1, 1), p["s5_c_re_bwd"], p["s5_c_im_bwd"],
        p["s5_b_re"], p["s5_b_im"], col(p["s5_d"].reshape(S5_GROUPS, S5_CH)),
        p["s5_glu_w"][:, :, :S5_CH], p["s5_glu_w"][:, :, S5_CH:], col(p["s5_glu_b"][:, :S5_CH]), col(p["s5_glu_b"][:, S5_CH:]),
    ]
    s5_args = [_x_layout(u, "s5_u_blocks")] + s5_params
    s5o = _token_layout(_s5_fwd(s5_args, nck, "s5_fwd"), "s5_y_tokens")
    ymix = _mix(y2, xs_a, z, s5o, dvec, p["ssd_norm_w"], p["s5_norm_w"], tm, "mix")
    p["w_out"], p["w_up"], p["w_down"] = late(ymix)
    dff = p["w_down"].shape[0]
    h1 = _matmul_res(ymix, p["w_out"], x, tm, "out_proj")
    w_up = p["w_up"]
    hn2, up_v, up_g = _norm_matmul(h1, p["norm_ffn_w"], [w_up[:, :dff], w_up[:, dff:]], tm_ffn, "ffn_up")
    fw, fb = p["ffn_conv_w"], p["ffn_conv_b"]
    act = _conv_glu(up_v, up_g, fw[:, :dff], fw[:, dff:], fb[:, :dff], fb[:, dff:], bl, 256, "ffn_conv")
    h2 = _matmul_res(act, p["w_down"], h1, tm, "ffn_down")
    loss, dh2, g_nfw = _final_loss(h2, p["norm_final_w"].reshape(1, d), tgt, tm, "final_loss")

    g = {"norm_final_w": g_nfw.reshape(d)}
    g["w_down"] = _matmul_tn(act, dh2, tm, d, "ffn_down_dw")
    dact = _matmul_nt([dh2], [p["w_down"]], tm, "ffn_down_dx")
    dup_v, dup_g, dwv, dwg, dbv, dbg = _conv_glu_bwd(up_v, up_g, fw[:, :dff], fw[:, dff:], fb[:, :dff], fb[:, dff:], dact, bl, 256, "ffn_conv_bwd")
    g["ffn_conv_w"] = jnp.concatenate([dwv, dwg], 1)
    g["ffn_conv_b"] = jnp.concatenate([dbv, dbg], 1)
    g["w_up"] = jnp.concatenate([_matmul_tn(hn2, dup_v, tm, dff // 2, "ffn_up_dw_v"), _matmul_tn(hn2, dup_g, tm, dff // 2, "ffn_up_dw_g")], 1)
    dhn2 = _matmul_nt([dup_v, dup_g], [w_up[:, :dff], w_up[:, dff:]], tm_ffn, "ffn_up_dx")
    g["ffn_reduce"] = early(g["w_up"], g["w_down"])
    behind = g["ffn_reduce"][1][3][0, 0]
    dh1, g["norm_ffn_w"] = _norm_bwd(h1, p["norm_ffn_w"] + behind, dhn2, dh2, tm, "ffn_norm_bwd")
    g["w_out"] = _matmul_tn(ymix, dh1, tm, d, "out_proj_dw")
    dmix = _matmul_nt([dh1], [p["w_out"]], tm, "out_proj_dx")
    dyssd, dxs_gate, dz, ds5o, g_d, g["ssd_norm_w"], g["s5_norm_w"] = _mix_bwd(
        y2, xs_a, z, s5o, dvec, p["ssd_norm_w"], p["s5_norm_w"], dmix, tm, "mix_bwd")
    g["ssd_d"] = g_d[:SSD_HEADS].reshape(1, SSD_HEADS)
    s5g = _s5_bwd(s5_args, _x_layout(ds5o, "s5_dy_blocks"), nck, "s5_bwd")
    du = _token_layout(s5g[0], "s5_du_tokens")
    (g["s5_lambda_re_fwd"], g["s5_lambda_im_fwd"], g["s5_log_step_fwd"], g["s5_c_re_fwd"], g["s5_c_im_fwd"],
     g["s5_lambda_re_bwd"], g["s5_lambda_im_bwd"], g["s5_log_step_bwd"], g["s5_c_re_bwd"], g["s5_c_im_bwd"],
     g["s5_b_re"], g["s5_b_im"], g_s5d, g_wv, g_wg, g_bv, g_bg) = s5g[1:]
    for k_ in ("s5_lambda_re_fwd", "s5_lambda_im_fwd", "s5_lambda_re_bwd", "s5_lambda_im_bwd"):
        g[k_] = g[k_].reshape(S5_GROUPS, S5_STATE)
    for k_ in ("s5_log_step_fwd", "s5_log_step_bwd"):
        g[k_] = g[k_].reshape(S5_GROUPS)
    g["s5_d"] = g_s5d.reshape(1, s5w)
    g["s5_glu_w"] = jnp.concatenate([g_wv, g_wg], 2)
    g["s5_glu_b"] = jnp.concatenate([g_bv.reshape(S5_GROUPS, S5_CH), g_bg.reshape(S5_GROUPS, S5_CH)], 1)
    dxs2, dbm2, dcm2, ddtr, dal2, ddb2 = _ssd_scan_bwd(xs_a, b_a, c_a, dtr, alog2, dtb2, saved, dyssd, bl, "ssd_scan_bwd")
    g["ssd_a_log_fwd"], g["ssd_a_log_bwd"] = dal2[0, :, :SSD_HEADS], dal2[1, :, :SSD_HEADS]
    g["ssd_dt_bias_fwd"], g["ssd_dt_bias_bwd"] = ddb2[0, :, :SSD_HEADS], ddb2[1, :, :SSD_HEADS]
    cots = [[(dxs2, 0), (dxs2, 1), (dxs_gate, None)], [(dbm2, 0), (dbm2, 1)], [(dcm2, 0), (dcm2, 1)]]
    dpres, dcw, dcb = [], [], []
    for i, (pre, (w, b), cot) in enumerate(zip(pres, conv_parts, cots)):
        dp, dw_, db_ = _conv_silu_bwd(pre, w, b, cot, bl, min(256, pre.shape[1]), f"ssd_conv_bwd_{i}")
        dpres.append(dp)
        dcw.append(dw_)
        dcb.append(db_)
    g["ssd_conv_w"] = jnp.concatenate(dcw, 1)
    g["ssd_conv_b"] = jnp.concatenate(dcb, 1)
    dprojs = [dz, dpres[0], dpres[1], dpres[2], ddtr, du]
    dws = [_matmul_tn(hn, dpj, tm, dpj.shape[1], f"in_proj_dw_{i}") for i, dpj in enumerate(dprojs)]
    dws[4] = jnp.concatenate([dws[4][:, :SSD_HEADS], dws[4][:, LANES:LANES + SSD_HEADS]], 1)
    g["w_in"] = jnp.concatenate(dws, 1)
    dhn = _matmul_nt(dprojs, in_ws, tm, "in_proj_dx")
    grad_x, g["norm_mix_w"] = _norm_bwd(x, p["norm_mix_w"], dhn, dh1, tm, "mix_norm_bwd")
    return loss, grad_x, g


_WEIGHTS = ['norm_mix_w', 'w_in', 'ssd_conv_w', 'ssd_conv_b', 'ssd_dt_bias_fwd', 'ssd_dt_bias_bwd', 'ssd_a_log_fwd', 'ssd_a_log_bwd',
            'ssd_d', 'ssd_norm_w', 's5_lambda_re_fwd', 's5_lambda_im_fwd', 's5_log_step_fwd', 's5_lambda_re_bwd', 's5_lambda_im_bwd',
            's5_log_step_bwd', 's5_b_re', 's5_b_im', 's5_c_re_fwd', 's5_c_im_fwd', 's5_c_re_bwd', 's5_c_im_bwd', 's5_d', 's5_glu_w',
            's5_glu_b', 's5_norm_w', 'w_out', 'norm_ffn_w', 'ffn_w_up', 'ffn_conv_w', 'ffn_conv_b', 'ffn_w_down', 'norm_final_w']
_BIG = ('w_in', 'w_out', 'ffn_w_up', 'ffn_w_down')
_CONV = ('ssd_conv_w', 'ffn_conv_w')


def _pack(arrs):
    flat = jnp.concatenate([a.reshape(-1) for a in arrs])
    rows = -(-flat.shape[0] // (64 * LANES)) * 64
    return jnp.pad(flat, (0, rows * LANES - flat.shape[0])).reshape(rows, LANES)


def _unpack(buf, shapes):
    flat = buf.reshape(-1)
    out, off = [], 0
    for shp in shapes:
        size = math.prod(shp)
        out.append(flat[off:off + size].reshape(shp))
        off += size
    return out


def kernel(x, norm_mix_w, w_in, ssd_conv_w, ssd_conv_b, ssd_dt_bias_fwd, ssd_dt_bias_bwd, ssd_a_log_fwd, ssd_a_log_bwd, ssd_d, ssd_norm_w, s5_lambda_re_fwd, s5_lambda_im_fwd, s5_log_step_fwd, s5_lambda_re_bwd, s5_lambda_im_bwd, s5_log_step_bwd, s5_b_re, s5_b_im, s5_c_re_fwd, s5_c_im_fwd, s5_c_re_bwd, s5_c_im_bwd, s5_d, s5_glu_w, s5_glu_b, s5_norm_w, w_out, norm_ffn_w, ffn_w_up, ffn_conv_w, ffn_conv_b, ffn_w_down, norm_final_w, loss_target, m_norm_mix_w, m_w_in, m_ssd_conv_w, m_ssd_conv_b, m_ssd_dt_bias_fwd, m_ssd_dt_bias_bwd, m_ssd_a_log_fwd, m_ssd_a_log_bwd, m_ssd_d, m_ssd_norm_w, m_s5_lambda_re_fwd, m_s5_lambda_im_fwd, m_s5_log_step_fwd, m_s5_lambda_re_bwd, m_s5_lambda_im_bwd, m_s5_log_step_bwd, m_s5_b_re, m_s5_b_im, m_s5_c_re_fwd, m_s5_c_im_fwd, m_s5_c_re_bwd, m_s5_c_im_bwd, m_s5_d, m_s5_glu_w, m_s5_glu_b, m_s5_norm_w, m_w_out, m_norm_ffn_w, m_ffn_w_up, m_ffn_conv_w, m_ffn_conv_b, m_ffn_w_down, m_norm_final_w, v_norm_mix_w, v_w_in, v_ssd_conv_w, v_ssd_conv_b, v_ssd_dt_bias_fwd, v_ssd_dt_bias_bwd, v_ssd_a_log_fwd, v_ssd_a_log_bwd, v_ssd_d, v_ssd_norm_w, v_s5_lambda_re_fwd, v_s5_lambda_im_fwd, v_s5_log_step_fwd, v_s5_lambda_re_bwd, v_s5_lambda_im_bwd, v_s5_log_step_bwd, v_s5_b_re, v_s5_b_im, v_s5_c_re_fwd, v_s5_c_im_fwd, v_s5_c_re_bwd, v_s5_c_im_bwd, v_s5_d, v_s5_glu_w, v_s5_glu_b, v_s5_norm_w, v_w_out, v_norm_ffn_w, v_ffn_w_up, v_ffn_conv_w, v_ffn_conv_b, v_ffn_w_down, v_norm_final_w):
    args = dict(locals())
    w = {k_: args[k_] for k_ in _WEIGHTS}
    m = {k_: args["m_" + k_] for k_ in _WEIGHTS}
    v = {k_: args["v_" + k_] for k_ in _WEIGHTS}
    bl, sl, d = x.shape
    chip = 2 * lax.axis_index("x") + lax.axis_index("y")
    core = lax.axis_index("c")

    first = w["w_in"][0].astype(BF16)
    g_in, g_scw, g_fcw = _gather_xy([first.reshape(2, first.shape[0] // 2, first.shape[1])], [w[k_][0] for k_ in _CONV], "gather_first")
    g_in = g_in.reshape((4,) + first.shape)
    handle = _copies_start(_whole_copies, 4, [w[k_][0].astype(BF16) for k_ in _BIG[1:]], g_scw, "gather_rest_start")

    def cols(a):
        return jnp.moveaxis(a, 0, 1).reshape(a.shape[1], 4 * a.shape[2])

    p = {k_: (w[k_][0] if w[k_].ndim >= 3 else w[k_]) for k_ in _WEIGHTS if k_ not in _BIG + _CONV}
    p["norm_mix_w"] = p["norm_mix_w"] + handle[3][0, 0]
    p["w_in"] = cols(g_in)
    p["ssd_conv_w"], p["ffn_conv_w"] = cols(g_scw), cols(g_fcw)

    def late(after):
        g_out, g_up, g_down = _copies_wait(_whole_copies, 4, handle, after, "gather_rest_wait")
        return g_out.reshape(-1, g_out.shape[2]), cols(g_up), g_down.reshape(-1, g_down.shape[2])

    def owner_major(a, k_):
        r, c = w[k_].shape[1:]
        if a.shape[0] == r:
            a = jnp.moveaxis(a.reshape(r, 4, c), 1, 0)
        else:
            a = a.reshape(4, r, c)
        return a.reshape(4, 2, r // 2, c)

    def early(g_up, g_down):
        parts = [jnp.moveaxis(owner_major(a, k_), 1, 0) for a, k_ in ((g_up, "ffn_w_up"), (g_down, "ffn_w_down"))]
        got = _swap_sibling(parts, True, "reduce_sibling_ffn")
        sums = [_add_half(pt, gt, core, BF16, f"reduce_add_ffn_{i}") for i, (pt, gt) in enumerate(zip(parts, got))]
        return sums, _copies_start(_scatter_copies, 3, sums, w["norm_ffn_w"], "reduce_chips_ffn_start")

    loss, grad_x, g = _local_step(x.reshape(bl * sl, d), loss_target.reshape(bl * sl, d), p, bl, late, early)
    ffn_sums, ffn_handle = g.pop("ffn_reduce")
    ffn_recv = _copies_wait(_scatter_copies, 3, ffn_handle, grad_x, "reduce_chips_ffn_wait")
    ffn_halves = [_sum_slots(rc, sm, chip, f"reduce_sum_ffn_{i}") for i, (rc, sm) in enumerate(zip(ffn_recv, ffn_sums))]

    small = [k_ for k_ in _WEIGHTS if k_ not in _BIG]
    small_full_shapes = [g[k_].shape for k_ in small]
    buf = _pack([g[k_] for k_ in small] + [loss[0, :1]])
    rest = _BIG[:2]
    parts = [jnp.moveaxis(owner_major(g[k_], k_), 1, 0) for k_ in rest]
    parts.append(jnp.moveaxis(buf.reshape(4, 2, -1, LANES), 1, 0))
    got = _swap_sibling(parts, True, "reduce_sibling")
    chip_sums = [_add_half(pt, gt, core, BF16 if i < len(rest) else F32, f"reduce_add_{i}") for i, (pt, gt) in enumerate(zip(parts, got))]
    from_chips = _scatter_xy(chip_sums, "reduce_chips")
    rest_halves = [_sum_lead(a.reshape(4, -1, a.shape[-1]), f"reduce_sum_{i}") for i, a in enumerate(from_chips)]
    halves = rest_halves[:-1] + ffn_halves + rest_halves[-1:]
    other = _swap_sibling(halves[:-1], False, "reduce_join")
    big_grad = {}
    for k_, own_half, sib_half in zip(_BIG, halves, other):
        south = core == 0
        full = jnp.stack([jnp.where(south, own_half, sib_half), jnp.where(south, sib_half, own_half)])
        big_grad[k_] = full.reshape((1,) + w[k_].shape[1:])
    tot = _bcast_all(halves[-1], "reduce_small").reshape(buf.shape)
    unp = _unpack(tot, small_full_shapes + [(1,)])
    small_grad = dict(zip(small, unp[:-1]))
    loss_out = unp[-1].reshape(())
    for k_ in _CONV:
        cshard = w[k_].shape[2]
        small_grad[k_] = lax.dynamic_slice_in_dim(small_grad[k_], chip * cshard, cshard, 1)

    grads, deltas, new_m, new_v = {}, {}, {}, {}
    for k_ in _BIG:
        shp = w[k_].shape
        grads[k_] = big_grad[k_]
        dl, nm, nv = _adamw(w[k_][0], big_grad[k_][0], m[k_][0], v[k_][0], f"adamw_{k_}")
        deltas[k_], new_m[k_], new_v[k_] = dl.reshape(shp), nm.reshape(shp), nv.reshape(shp)
    sw_ = _pack([w[k_] for k_ in small])
    sg_ = _pack([small_grad[k_] for k_ in small])
    sm_ = _pack([m[k_] for k_ in small])
    sv_ = _pack([v[k_] for k_ in small])
    dl, nm, nv = _adamw(sw_, sg_, sm_, sv_, "adamw_small")
    shapes = [w[k_].shape for k_ in small]
    for k_, a, b, c_ in zip(small, _unpack(dl, shapes), _unpack(nm, shapes), _unpack(nv, shapes)):
        deltas[k_], new_m[k_], new_v[k_] = a, b, c_
        grads[k_] = small_grad[k_].reshape(w[k_].shape)
    return (loss_out, grad_x.reshape(bl, sl, d), *[grads[k_] for k_ in _WEIGHTS], *[deltas[k_] for k_ in _WEIGHTS],
            *[new_m[k_] for k_ in _WEIGHTS], *[new_v[k_] for k_ in _WEIGHTS])
```

```python
import functools
import math

import jax
import jax.numpy as jnp
from jax import lax
from jax.experimental import pallas as pl
from jax.experimental.pallas import tpu as pltpu

F32 = jnp.float32
BF16 = jnp.bfloat16
HI = lax.Precision.HIGHEST
SDS = jax.ShapeDtypeStruct
MESH = pl.DeviceIdType.MESH

NN = (((1,), (0,)), ((), ()))
NT = (((1,), (1,)), ((), ()))
TN = (((0,), (0,)), ((), ()))

EPS = 1e-6
SSD_HEADS = 16
SSD_HEAD_DIM = 64
SSD_GROUPS = 4
SSD_STATE = 128
SSD_CHUNK = 128
S5_GROUPS = 32
S5_CH = 16
S5_STATE = 64
S5_T = 16
LANES = 128
ADAM_LR, ADAM_B1, ADAM_B2, ADAM_EPS, ADAM_WD, ADAM_STEP = 0.001, 0.9, 0.999, 1e-08, 0.01, 10
V7X_VMEM_BYTES = 64 * 1024 * 1024
VMEM_LIMIT = V7X_VMEM_BYTES - 8 * 1024 * 1024


def _cp(sem, vmem=None):
    return pltpu.CompilerParams(dimension_semantics=sem, vmem_limit_bytes=vmem)


def _dot(a, b, dims=NN, precision=None):
    return lax.dot_general(a, b, dims, precision=precision, preferred_element_type=F32)


def _rms(x, w):
    return x * lax.rsqrt(jnp.mean(x * x, axis=-1, keepdims=True) + EPS) * w


def _sigmoid(x):
    return 1.0 / (1.0 + jnp.exp(-x))


def _softplus(x):
    return jnp.maximum(x, 0.0) + jnp.log1p(jnp.exp(-jnp.abs(x)))


@functools.partial(jax.custom_vjp, nondiff_argnums=(1, 2))
def _shift(x, k, seg):
    n = x.shape[0]
    r = lax.broadcasted_iota(jnp.int32, x.shape, 0)
    if seg != n:
        r = r & (seg - 1) if seg & (seg - 1) == 0 else r % seg
    y = pltpu.roll(x, k % n, 0)
    ok = (r >= k) if k > 0 else (r < seg + k)
    return jnp.where(ok, y, 0.0)


def _shift_fwd(x, k, seg):
    return _shift(x, k, seg), None


def _shift_bwd(k, seg, _, g):
    return (_shift(g, -k, seg),)


_shift.defvjp(_shift_fwd, _shift_bwd)


@functools.partial(jax.custom_vjp, nondiff_argnums=(1,))
def _lane_shift(x, k):
    if k == 0:
        return x
    n = x.shape[1]
    lane = lax.broadcasted_iota(jnp.int32, x.shape, 1)
    ok = (lane >= k) if k > 0 else (lane < n + k)
    return jnp.where(ok, pltpu.roll(x, k % n, 1), 0.0)


_lane_shift.defvjp(lambda x, k: (_lane_shift(x, k), None), lambda k, _, g: (_lane_shift(g, -k),))


@jax.custom_vjp
def _swap(z):
    return pltpu.roll(z, LANES // 2, 1)


_swap.defvjp(lambda z: (_swap(z), None), lambda _, g: (_swap(g),))


def _norm_matmul(x, nw, ws, tm, name):
    n, d = x.shape
    k = len(ws)

    def body(x_ref, nw_ref, *refs):
        hn = _rms(x_ref[...], nw_ref[...]).astype(BF16)
        refs[k][...] = hn
        for w_ref, o_ref in zip(refs[:k], refs[k + 1:]):
            o_ref[...] = _dot(hn, w_ref[...])

    row = lambda i: (i, 0)
    fix = lambda i: (0, 0)
    return pl.pallas_call(
        body, name=name, grid=(n // tm,),
        in_specs=[pl.BlockSpec((tm, d), row), pl.BlockSpec((1, d), fix)] + [pl.BlockSpec(w.shape, fix) for w in ws],
        out_specs=[pl.BlockSpec((tm, d), row)] + [pl.BlockSpec((tm, w.shape[1]), row) for w in ws],
        out_shape=[SDS((n, d), BF16)] + [SDS((n, w.shape[1]), F32) for w in ws],
        compiler_params=_cp(("arbitrary",), VMEM_LIMIT),
    )(x, nw, *ws)


def _matmul_res(a, w, res, tm, name):
    n, kd = a.shape
    m = w.shape[1]

    def body(a_ref, w_ref, r_ref, o_ref):
        o_ref[...] = r_ref[...] + _dot(a_ref[...], w_ref[...])

    return pl.pallas_call(
        body, name=name, grid=(n // tm,),
        in_specs=[pl.BlockSpec((tm, kd), lambda i: (i, 0)), pl.BlockSpec((kd, m), lambda i: (0, 0)),
                  pl.BlockSpec((tm, m), lambda i: (i, 0))],
        out_specs=pl.BlockSpec((tm, m), lambda i: (i, 0)),
        out_shape=SDS((n, m), F32),
        compiler_params=_cp(("arbitrary",), VMEM_LIMIT),
    )(a, w, res)


def _matmul_nt(gs, ws, tm, name):
    n = gs[0].shape[0]
    kd = ws[0].shape[0]
    cnt = len(gs)

    def body(*refs):
        acc = None
        for g_ref, w_ref in zip(refs[:cnt], refs[cnt:2 * cnt]):
            t = _dot(g_ref[...].astype(BF16), w_ref[...], NT)
            acc = t if acc is None else acc + t
        refs[2 * cnt][...] = acc

    return pl.pallas_call(
        body, name=name, grid=(n // tm,),
        in_specs=[pl.BlockSpec((tm, g.shape[1]), lambda i: (i, 0)) for g in gs]
        + [pl.BlockSpec(w.shape, lambda i: (0, 0)) for w in ws],
        out_specs=pl.BlockSpec((tm, kd), lambda i: (i, 0)),
        out_shape=SDS((n, kd), F32),
        compiler_params=_cp(("arbitrary",), VMEM_LIMIT),
    )(*gs, *ws)


def _matmul_tn(a, g, tm, cb, name):
    n, kd = a.shape
    m = g.shape[1]

    def body(a_ref, g_ref, o_ref):
        t = _dot(a_ref[...], g_ref[...].astype(BF16), TN)

        @pl.when(pl.program_id(1) == 0)
        def _():
            o_ref[...] = t

        @pl.when(pl.program_id(1) != 0)
        def _():
            o_ref[...] += t

    return pl.pallas_call(
        body, name=name, grid=(m // cb, n // tm),
        in_specs=[pl.BlockSpec((tm, kd), lambda j, i: (i, 0)), pl.BlockSpec((tm, cb), lambda j, i: (i, j))],
        out_specs=pl.BlockSpec((kd, cb), lambda j, i: (0, j)),
        out_shape=SDS((kd, m), F32),
        compiler_params=_cp(("arbitrary", "arbitrary"), VMEM_LIMIT),
    )(a, g)


def _dwconv(x, w, b):
    kw = w.shape[0]
    acc = b
    for k in range(kw):
        acc = acc + w[k:k + 1, :] * _shift(x, kw // 2 - k, x.shape[0])
    return acc


def _conv_silu_fn(x, w, b):
    y = _dwconv(x, w, b)
    return y * _sigmoid(y)


def _conv_glu_fn(v, g, wv, wg, bv, bg):
    cv = _dwconv(v, wv, bv)
    cg = _dwconv(g, wg, bg)
    return cg * _sigmoid(cg) * cv


def _conv_silu(x, w, b, bl, cb, name):
    n, c = x.shape
    sl = n // bl
    kw = w.shape[0]

    def body(x_ref, w_ref, b_ref, o_ref):
        o_ref[...] = _conv_silu_fn(x_ref[...], w_ref[...], b_ref[...])

    return pl.pallas_call(
        body, name=name, grid=(bl, c // cb),
        in_specs=[pl.BlockSpec((sl, cb), lambda s, j: (s, j)), pl.BlockSpec((kw, cb), lambda s, j: (0, j)),
                  pl.BlockSpec((1, cb), lambda s, j: (0, j))],
        out_specs=pl.BlockSpec((sl, cb), lambda s, j: (s, j)),
        out_shape=SDS((n, c), F32),
        compiler_params=_cp(("arbitrary", "arbitrary"), VMEM_LIMIT),
    )(x, w, b)


def _conv_silu_bwd(x, w, b, dys, bl, cb, name):
    n, c = x.shape
    sl = n // bl
    kw = w.shape[0]
    cnt = len(dys)

    def body(x_ref, w_ref, b_ref, *refs):
        dy = refs[0][...]
        for r in refs[1:cnt]:
            dy = dy + r[...]
        dx_ref, dw_ref, db_ref = refs[cnt:]
        _, vjp = jax.vjp(_conv_silu_fn, x_ref[...], w_ref[...], b_ref[...])
        dx, dw, db = vjp(dy)
        dx_ref[...] = dx.astype(BF16)

        @pl.when(pl.program_id(1) == 0)
        def _():
            dw_ref[...] = dw
            db_ref[...] = db

        @pl.when(pl.program_id(1) != 0)
        def _():
            dw_ref[...] += dw
            db_ref[...] += db

    dy_specs = []
    for arr, lead in dys:
        if lead is None:
            dy_specs.append(pl.BlockSpec((sl, cb), lambda j, s: (s, j)))
        else:
            dy_specs.append(pl.BlockSpec((None, sl, cb), functools.partial(lambda j, s, lead: (lead, s, j), lead=lead)))
    return pl.pallas_call(
        body, name=name, grid=(c // cb, bl),
        in_specs=[pl.BlockSpec((sl, cb), lambda j, s: (s, j)), pl.BlockSpec((kw, cb), lambda j, s: (0, j)),
                  pl.BlockSpec((1, cb), lambda j, s: (0, j))] + dy_specs,
        out_specs=[pl.BlockSpec((sl, cb), lambda j, s: (s, j)), pl.BlockSpec((kw, cb), lambda j, s: (0, j)),
                   pl.BlockSpec((1, cb), lambda j, s: (0, j))],
        out_shape=[SDS((n, c), BF16), SDS((kw, c), F32), SDS((1, c), F32)],
        compiler_params=_cp(("arbitrary", "arbitrary"), VMEM_LIMIT),
    )(x, w, b, *[a for a, _ in dys])


def _conv_glu(v, g, wv, wg, bv, bg, bl, cb, name):
    n, c = v.shape
    sl = n // bl
    kw = wv.shape[0]

    def body(v_ref, g_ref, wv_ref, wg_ref, bv_ref, bg_ref, o_ref):
        o_ref[...] = _conv_glu_fn(v_ref[...], g_ref[...], wv_ref[...], wg_ref[...], bv_ref[...], bg_ref[...]).astype(BF16)

    big = pl.BlockSpec((sl, cb), lambda s, j: (s, j))
    wsp = pl.BlockSpec((kw, cb), lambda s, j: (0, j))
    bsp = pl.BlockSpec((1, cb), lambda s, j: (0, j))
    return pl.pallas_call(
        body, name=name, grid=(bl, c // cb),
        in_specs=[big, big, wsp, wsp, bsp, bsp], out_specs=big, out_shape=SDS((n, c), BF16),
        compiler_params=_cp(("arbitrary", "arbitrary"), VMEM_LIMIT),
    )(v, g, wv, wg, bv, bg)


def _conv_glu_bwd(v, g, wv, wg, bv, bg, dact, bl, cb, name):
    n, c = v.shape
    sl = n // bl
    kw = wv.shape[0]

    def body(v_ref, g_ref, wv_ref, wg_ref, bv_ref, bg_ref, da_ref, dv_ref, dg_ref, dwv_ref, dwg_ref, dbv_ref, dbg_ref):
        _, vjp = jax.vjp(_conv_glu_fn, v_ref[...], g_ref[...], wv_ref[...], wg_ref[...], bv_ref[...], bg_ref[...])
        dv, dg, dwv, dwg, dbv, dbg = vjp(da_ref[...])
        dv_ref[...] = dv.astype(BF16)
        dg_ref[...] = dg.astype(BF16)

        @pl.when(pl.program_id(1) == 0)
        def _():
            dwv_ref[...] = dwv
            dwg_ref[...] = dwg
            dbv_ref[...] = dbv
            dbg_ref[...] = dbg

        @pl.when(pl.program_id(1) != 0)
        def _():
            dwv_ref[...] += dwv
            dwg_ref[...] += dwg
            dbv_ref[...] += dbv
            dbg_ref[...] += dbg

    big = pl.BlockSpec((sl, cb), lambda j, s: (s, j))
    wsp = pl.BlockSpec((kw, cb), lambda j, s: (0, j))
    bsp = pl.BlockSpec((1, cb), lambda j, s: (0, j))
    return pl.pallas_call(
        body, name=name, grid=(c // cb, bl),
        in_specs=[big, big, wsp, wsp, bsp, bsp, big],
        out_specs=[big, big, wsp, wsp, bsp, bsp],
        out_shape=[SDS((n, c), BF16), SDS((n, c), BF16), SDS((kw, c), F32), SDS((kw, c), F32), SDS((1, c), F32), SDS((1, c), F32)],
        compiler_params=_cp(("arbitrary", "arbitrary"), VMEM_LIMIT),
    )(v, g, wv, wg, bv, bg, dact)


_DIMS_T = {NN: (NT, TN, False, False), NT: (NN, TN, False, True), TN: (NT, NN, True, False)}


@functools.partial(jax.custom_vjp, nondiff_argnums=(2,))
def _bdot(a, b, dims):
    return _dot(a.astype(BF16), b.astype(BF16), dims)


def _bdot_fwd(a, b, dims):
    return _bdot(a, b, dims), (a, b)


def _bdot_bwd(dims, res, g):
    a, b = res
    da_dims, db_dims, a_swapped, b_swapped = _DIMS_T[dims]
    da = _bdot(b, g, da_dims) if a_swapped else _bdot(g, b, da_dims)
    db = _bdot(g, a, db_dims) if b_swapped else _bdot(a, g, db_dims)
    return da, db


_bdot.defvjp(_bdot_fwd, _bdot_bwd)


@functools.partial(jax.custom_vjp, nondiff_argnums=(1,))
def _expand_heads(v, width):
    return _split_dot(v, _head_matrix(width), NN)


def _head_matrix(width):
    hr = lax.broadcasted_iota(jnp.int32, (LANES, SSD_HEADS * width), 0)
    hc = lax.broadcasted_iota(jnp.int32, (LANES, SSD_HEADS * width), 1)
    return (hc // width == hr).astype(BF16)


def _split_dot(v, e, dims):
    hi = v.astype(BF16)
    lo = (v - hi.astype(F32)).astype(BF16)
    return _dot(hi, e, dims) + _dot(lo, e, dims)


_expand_heads.defvjp(lambda v, width: (_expand_heads(v, width), None),
                     lambda width, _, g: (_split_dot(g, _head_matrix(width), NT),))


def _ssd_chunk_fn(rev, xs, dtr, bms, cms, st, alog, dtb):
    q = dtr.shape[0]
    hd, per = SSD_HEAD_DIM, SSD_HEADS // SSD_GROUPS
    gw = per * hd
    r = lax.broadcasted_iota(jnp.int32, (q, q), 0)
    c = lax.broadcasted_iota(jnp.int32, (q, q), 1)
    sgn = 1 - 2 * rev
    tri = ((r - c) * sgn >= 0).astype(F32)
    r4 = lax.broadcasted_iota(jnp.int32, (q, per * q), 0)
    c4 = lax.broadcasted_iota(jnp.int32, (q, per * q), 1) % q
    mask4 = (r4 - c4) * sgn >= 0
    bdr = lax.broadcasted_iota(jnp.int32, (per * q, gw), 0) // q
    bdc = lax.broadcasted_iota(jnp.int32, (per * q, gw), 1) // hd
    diag = bdr == bdc
    dt = _softplus(dtr + dtb)
    dta = dt * (-jnp.exp(alog))
    cs = _dot(tri, dta, NN, HI)
    cs_t = cs.T
    tot = jnp.sum(dta, axis=0, keepdims=True)
    dt_x = _expand_heads(dt, hd)
    in_x = _expand_heads(jnp.exp(cs), hd)
    out_x = _expand_heads(jnp.exp(tot - cs), hd)
    ys, outs = [], []
    for g in range(SSD_GROUPS):
        bg, cg = bms[g], cms[g]
        heads = range(per * g, per * (g + 1))
        lanes = slice(gw * g, gw * (g + 1))
        scores = _bdot(cg, bg, NT)
        col = jnp.concatenate([jnp.broadcast_to(cs[:, h:h + 1], (q, q)) for h in heads], axis=1)
        row = jnp.concatenate([cs_t[h:h + 1, :] for h in heads], axis=1)
        seg = jnp.where(mask4, jnp.exp(jnp.where(mask4, col - row, 0.0)), 0.0)
        mcat = jnp.concatenate([scores] * per, axis=1) * seg
        xdt = xs[g] * dt_x[:, lanes]
        blocks = jnp.where(diag, jnp.concatenate([xdt] * per, axis=0), 0.0)
        y = _bdot(mcat, blocks, NN) + in_x[:, lanes] * _bdot(cg, st[g], NT)
        new = _bdot(xdt * out_x[:, lanes], bg, TN)
        keep = jnp.concatenate([jnp.exp(tot[:, h:h + 1]) * st[g][hd * j:hd * (j + 1), :] for j, h in enumerate(heads)], axis=0)
        ys.append(y)
        outs.append(keep + new)
    return ys, outs


def _ssd_scan(xs, bm, cm, dtr, alog2, dtb2, bl, name):
    n = xs.shape[0]
    q = SSD_CHUNK
    nc = n // bl // q
    hd, ns = SSD_HEAD_DIM, SSD_STATE
    gw = SSD_HEADS // SSD_GROUPS * hd

    def body(xs_ref, b_ref, c_ref, dt_ref, al_ref, db_ref, y_ref, sv_ref, st_ref):
        d, i = pl.program_id(0), pl.program_id(2)

        @pl.when(i == 0)
        def _():
            st_ref[...] = jnp.zeros(st_ref.shape, F32)

        st = [st_ref[gw * g:gw * (g + 1), :] for g in range(SSD_GROUPS)]
        sv_ref[...] = st_ref[...]
        xl = [xs_ref[:, gw * g:gw * (g + 1)] for g in range(SSD_GROUPS)]
        bms = [b_ref[:, ns * g:ns * (g + 1)] for g in range(SSD_GROUPS)]
        cms = [c_ref[:, ns * g:ns * (g + 1)] for g in range(SSD_GROUPS)]
        ys, outs = _ssd_chunk_fn(d, xl, dt_ref[...], bms, cms, st, al_ref[...], db_ref[...])
        for g in range(SSD_GROUPS):
            st_ref[gw * g:gw * (g + 1), :] = outs[g]
            y_ref[:, gw * g:gw * (g + 1)] = ys[g]

    def rowblk(d, s, i):
        return s * nc + i + d * (nc - 1 - 2 * i)

    return pl.pallas_call(
        body, name=name, grid=(2, bl, nc),
        in_specs=[pl.BlockSpec((q, SSD_HEADS * hd), lambda d, s, i: (rowblk(d, s, i), 0)),
                  pl.BlockSpec((q, SSD_GROUPS * ns), lambda d, s, i: (rowblk(d, s, i), 0)),
                  pl.BlockSpec((q, SSD_GROUPS * ns), lambda d, s, i: (rowblk(d, s, i), 0)),
                  pl.BlockSpec((q, LANES), lambda d, s, i: (rowblk(d, s, i), d)),
                  pl.BlockSpec((None, 1, LANES), lambda d, s, i: (d, 0, 0)),
                  pl.BlockSpec((None, 1, LANES), lambda d, s, i: (d, 0, 0))],
        out_specs=[pl.BlockSpec((None, q, SSD_HEADS * hd), lambda d, s, i: (d, rowblk(d, s, i), 0)),
                   pl.BlockSpec((None, None, SSD_HEADS * hd, ns), lambda d, s, i: (d, rowblk(d, s, i), 0, 0))],
        out_shape=[SDS((2, n, SSD_HEADS * hd), F32), SDS((2, n // q, SSD_HEADS * hd, ns), F32)],
        scratch_shapes=[pltpu.VMEM((SSD_HEADS * hd, ns), F32)],
        compiler_params=_cp(("arbitrary",) * 3, VMEM_LIMIT),
    )(xs, bm, cm, dtr, alog2, dtb2)


def _ssd_scan_bwd(xs, bm, cm, dtr, alog2, dtb2, saved, dy, bl, name):
    n = xs.shape[0]
    q = SSD_CHUNK
    nc = n // bl // q
    hd, ns = SSD_HEAD_DIM, SSD_STATE
    gw = SSD_HEADS // SSD_GROUPS * hd

    def body(xs_ref, b_ref, c_ref, dt_ref, al_ref, db_ref, sv_ref, dy_ref,
             dxs_ref, dbm_ref, dcm_ref, ddt_ref, dal_ref, ddb_ref, ds_ref):
        d, s, i = pl.program_id(0), pl.program_id(1), pl.program_id(2)

        @pl.when(i == 0)
        def _():
            ds_ref[...] = jnp.zeros(ds_ref.shape, F32)

        xl = [xs_ref[:, gw * g:gw * (g + 1)] for g in range(SSD_GROUPS)]
        bms = [b_ref[:, ns * g:ns * (g + 1)] for g in range(SSD_GROUPS)]
        cms = [c_ref[:, ns * g:ns * (g + 1)] for g in range(SSD_GROUPS)]
        st = [sv_ref[gw * g:gw * (g + 1), :] for g in range(SSD_GROUPS)]
        fn = functools.partial(_ssd_chunk_fn, d)
        _, vjp = jax.vjp(fn, xl, dt_ref[...], bms, cms, st, al_ref[...], db_ref[...])
        dys = [dy_ref[:, gw * g:gw * (g + 1)] for g in range(SSD_GROUPS)]
        dso = [ds_ref[gw * g:gw * (g + 1), :] for g in range(SSD_GROUPS)]
        dxl, ddt, dbg, dcg, dst, dal, ddb = vjp((dys, dso))
        for g in range(SSD_GROUPS):
            ds_ref[gw * g:gw * (g + 1), :] = dst[g]
            dxs_ref[:, gw * g:gw * (g + 1)] = dxl[g]
            dbm_ref[:, ns * g:ns * (g + 1)] = dbg[g]
            dcm_ref[:, ns * g:ns * (g + 1)] = dcg[g]
        ddt_ref[...] = ddt
        _acc_rows((dal_ref, ddb_ref), (dal, ddb), jnp.logical_and(s == 0, i == 0))

    def rowblk(d, s, i):
        return s * nc + (nc - 1 - i) + d * (2 * i - (nc - 1))

    row = lambda d, s, i: (rowblk(d, s, i), 0)
    drow = lambda d, s, i: (d, rowblk(d, s, i), 0)
    dfix = lambda d, s, i: (d, 0, 0)
    dcol = lambda d, s, i: (rowblk(d, s, i), d)
    return pl.pallas_call(
        body, name=name, grid=(2, bl, nc),
        in_specs=[pl.BlockSpec((q, SSD_HEADS * hd), row), pl.BlockSpec((q, SSD_GROUPS * ns), row),
                  pl.BlockSpec((q, SSD_GROUPS * ns), row), pl.BlockSpec((q, LANES), dcol),
                  pl.BlockSpec((None, 1, LANES), dfix), pl.BlockSpec((None, 1, LANES), dfix),
                  pl.BlockSpec((None, None, SSD_HEADS * hd, ns), lambda d, s, i: (d, rowblk(d, s, i), 0, 0)),
                  pl.BlockSpec((q, SSD_HEADS * hd), row)],
        out_specs=[pl.BlockSpec((None, q, SSD_HEADS * hd), drow), pl.BlockSpec((None, q, SSD_GROUPS * ns), drow),
                   pl.BlockSpec((None, q, SSD_GROUPS * ns), drow), pl.BlockSpec((q, LANES), dcol),
                   pl.BlockSpec((None, 1, LANES), dfix), pl.BlockSpec((None, 1, LANES), dfix)],
        out_shape=[SDS((2, n, SSD_HEADS * hd), F32), SDS((2, n, SSD_GROUPS * ns), F32), SDS((2, n, SSD_GROUPS * ns), F32),
                   SDS((n, 2 * LANES), F32), SDS((2, 1, LANES), F32), SDS((2, 1, LANES), F32)],
        scratch_shapes=[pltpu.VMEM((SSD_HEADS * hd, ns), F32)],
        compiler_params=_cp(("arbitrary",) * 3, VMEM_LIMIT),
    )(xs, bm, cm, dtr, alog2, dtb2, saved, dy)


def _s5_consts():
    t, ch, p = S5_T, S5_CH, S5_STATE
    lane = lax.broadcasted_iota(jnp.int32, (1, 2 * p), 1)
    pr = lax.broadcasted_iota(jnp.int32, (p, 2 * p), 0)
    pc = lax.broadcasted_iota(jnp.int32, (p, 2 * p), 1)
    cr = lax.broadcasted_iota(jnp.int32, (ch, t * ch), 0)
    cc = lax.broadcasted_iota(jnp.int32, (ch, t * ch), 1)
    return dict(
        left=lane < p,
        sg=jnp.where(lane < p, -1.0, 1.0).astype(F32),
        dup=(pc % p == pr).astype(F32),
        dup_l=(pc == pr).astype(F32),
        dup_r=(pc == pr + p).astype(F32),
        rep=(cc % ch == cr).astype(F32),
        rep0=(cc == cr).astype(F32),
    )


def _s5_mats(k, rev, lr, li, ls, bre, bim, cre, cim):
    t = S5_T
    step = jnp.exp(ls)
    lr2 = jnp.sum(lr * k["dup"], axis=0, keepdims=True)
    li2 = jnp.sum(li * k["dup"], axis=0, keepdims=True)

    def erow(d):
        ang = (d * step) * li2
        return jnp.exp((d * step) * lr2) * jnp.where(k["left"], jnp.cos(ang), jnp.sin(ang))

    es = [erow(d) for d in range(t + 1)]
    mag = jnp.exp(step * lr)
    ar, ai = mag * jnp.cos(step * li), mag * jnp.sin(step * li)
    den = lr * lr + li * li
    zr = ((ar - 1.0) * lr + ai * li) / den
    zi = (ai * lr - (ar - 1.0) * li) / den
    bbr = zr * bre - zi * bim
    bbi = zr * bim + zi * bre
    bt1 = _dot(bbr, k["dup"], TN, HI)
    bt2 = _dot(bbi, k["dup"], TN, HI)
    bst = _dot(bbr, k["dup_l"], TN, HI) - _dot(bbi, k["dup_r"], TN, HI)
    c1 = _dot(cre, k["dup"], NN, HI)
    c2 = _dot(cim, k["dup"], NN, HI)
    sg = k["sg"]
    ce = [e * c1 + sg * _swap(e) * c2 for e in es]
    lags = range(t - 1, -1, -1) if rev else range(t)
    kt = _dot(bst, jnp.concatenate([ce[d] for d in lags], axis=0), NT, HI)
    toep = jnp.concatenate([_lane_shift(kt, -S5_CH * (t - 1 - s) if rev else S5_CH * s) for s in range(t)], axis=0)
    w_out =jnp.concatenate([ce[(t - qq) if rev else (qq + 1)] * (-sg) for qq in range(t)], axis=0)
    w_st = jnp.concatenate(
        [(lambda e: e * bt1 + sg * _swap(e) * bt2)(es[s if rev else (t - 1 - s)]) for s in range(t)], axis=0)
    return toep, w_out, w_st, es[t]


def _cmul_row(k, e, z):
    es = _swap(e)
    return z * jnp.where(k["left"], e, es) + _swap(z) * (k["sg"] * jnp.where(k["left"], es, e))


def _s5_dir(k, rev, nck, x, mats):
    toep, w_out, w_st, a_t = mats
    acc = _bdot(x, w_st, NN)
    e = a_t
    kk = 1
    sign = -1 if rev else 1
    while kk < nck:
        acc = acc + _cmul_row(k, e, _shift(acc, sign * kk, nck))
        e = _cmul_row(k, e, e)
        kk *= 2
    prev = _shift(acc, sign, nck)
    return _bdot(x, toep, NN) + _bdot(prev, w_out, NT)


def _s5_group_fn(nck, x, pf, pb, bre, bim, dcol, wv, wg, bv, bg):
    k = _s5_consts()
    t = S5_T
    y = x * jnp.sum(dcol * k["rep"], axis=0, keepdims=True)
    for rev, (lr, li, ls, cre, cim) in ((False, pf), (True, pb)):
        y = y + _s5_dir(k, rev, nck, x, _s5_mats(k, rev, lr, li, ls, bre, bim, cre, cim))
    gy = jax.nn.gelu(y)
    def kron_eye(w16):
        wide = _dot(w16, k["rep0"], NN, HI)
        return jnp.concatenate([_lane_shift(wide, S5_CH * qq) for qq in range(t)], axis=0)

    kv, kg = kron_eye(wv), kron_eye(wg)
    val = _bdot(gy, kv, NN) + jnp.sum(bv * k["rep"], axis=0, keepdims=True)
    gate = _bdot(gy, kg, NN) + jnp.sum(bg * k["rep"], axis=0, keepdims=True)
    return val * _sigmoid(gate)


def _s5_specs(r):
    p, ch = S5_STATE, S5_CH
    g3 = lambda i: (i, 0, 0)
    col = pl.BlockSpec((None, p, 1), g3)
    one = pl.BlockSpec((None, 1, 1), g3)
    cmat = pl.BlockSpec((None, ch, p), g3)
    bmat = pl.BlockSpec((None, p, ch), g3)
    ccol = pl.BlockSpec((None, ch, 1), g3)
    sq = pl.BlockSpec((None, ch, ch), g3)
    xs = pl.BlockSpec((None, r, S5_T * ch), g3)
    specs = [xs, col, col, one, cmat, cmat, col, col, one, cmat, cmat, bmat, bmat, ccol, sq, sq, ccol, ccol]
    return specs


def _s5_unpack(vals):
    x = vals[0]
    pf = tuple(vals[1:6])
    pb = tuple(vals[6:11])
    bre, bim, dcol, wv, wg, bv, bg = vals[11:18]
    return x, pf, pb, bre, bim, dcol, wv, wg, bv, bg


def _s5_fwd(args, nck, name):
    x = args[0]
    ng, r, w = x.shape

    def body(*refs):
        vals = [ref[...] for ref in refs[:18]]
        refs[18][...] = _s5_group_fn(nck, *_s5_unpack(vals))

    specs = _s5_specs(r)
    return pl.pallas_call(
        body, name=name, grid=(ng,), in_specs=specs, out_specs=specs[0], out_shape=SDS(x.shape, F32),
        compiler_params=_cp(("arbitrary",), VMEM_LIMIT),
    )(*args)


def _s5_bwd(args, dy, nck, name):
    x = args[0]
    ng, r, w = x.shape

    def body(*refs):
        vals = [ref[...] for ref in refs[:18]]
        _, vjp = jax.vjp(lambda *v: _s5_group_fn(nck, *_s5_unpack(v)), *vals)
        grads = vjp(refs[18][...])
        for o_ref, gval in zip(refs[19:], grads):
            o_ref[...] = gval.astype(o_ref.dtype)

    specs = _s5_specs(r)
    return pl.pallas_call(
        body, name=name, grid=(ng,), in_specs=specs + [specs[0]], out_specs=specs,
        out_shape=[SDS(x.shape, BF16)] + [SDS(a.shape, F32) for a in args[1:]],
        compiler_params=_cp(("arbitrary",), VMEM_LIMIT),
    )(*args, dy)


def _mix_fn(yf, yb, xs, z, s5o, dvec, nw_ssd, nw_s5):
    hr = lax.broadcasted_iota(jnp.int32, (LANES, SSD_HEADS * SSD_HEAD_DIM), 0)
    hc = lax.broadcasted_iota(jnp.int32, (LANES, SSD_HEADS * SSD_HEAD_DIM), 1)
    expand = (hc // SSD_HEAD_DIM == hr).astype(F32)
    dch = jnp.sum(dvec * expand, axis=0, keepdims=True)
    y = (yf + yb + dch * xs) * (z * _sigmoid(z))
    return _rms(y, nw_ssd), _rms(s5o, nw_s5)


def _mix(y2, xs, z, s5o, dvec, nw_ssd, nw_s5, tm, name):
    n, c1 = xs.shape
    c2 = s5o.shape[1]

    def body(yf_ref, yb_ref, xs_ref, z_ref, s_ref, d_ref, n1_ref, n2_ref, o_ref):
        o1, o2 = _mix_fn(yf_ref[...], yb_ref[...], xs_ref[...], z_ref[...], s_ref[...], d_ref[...], n1_ref[...], n2_ref[...])
        o_ref[:, :c1] = o1.astype(BF16)
        o_ref[:, c1:] = o2.astype(BF16)

    row = lambda i: (i, 0)
    fix = lambda i: (0, 0)
    return pl.pallas_call(
        body, name=name, grid=(n // tm,),
        in_specs=[pl.BlockSpec((None, tm, c1), lambda i: (0, i, 0)), pl.BlockSpec((None, tm, c1), lambda i: (1, i, 0)),
                  pl.BlockSpec((tm, c1), row), pl.BlockSpec((tm, c1), row), pl.BlockSpec((tm, c2), row),
                  pl.BlockSpec((LANES, 1), fix), pl.BlockSpec((1, c1), fix), pl.BlockSpec((1, c2), fix)],
        out_specs=pl.BlockSpec((tm, c1 + c2), row), out_shape=SDS((n, c1 + c2), BF16),
        compiler_params=_cp(("arbitrary",), VMEM_LIMIT),
    )(y2, y2, xs, z, s5o, dvec, nw_ssd, nw_s5)


def _acc_rows(refs, vals, first):
    @pl.when(first)
    def _():
        for ref, v in zip(refs, vals):
            ref[...] = v

    @pl.when(jnp.logical_not(first))
    def _():
        for ref, v in zip(refs, vals):
            ref[...] += v


def _mix_bwd(y2, xs, z, s5o, dvec, nw_ssd, nw_s5, dmix, tm, name):
    n, c1 = xs.shape
    c2 = s5o.shape[1]

    def body(yf_ref, yb_ref, xs_ref, z_ref, s_ref, d_ref, n1_ref, n2_ref, dm_ref,
             dy_ref, dxs_ref, dz_ref, ds_ref, dd_ref, dn1_ref, dn2_ref):
        _, vjp = jax.vjp(_mix_fn, yf_ref[...], yb_ref[...], xs_ref[...], z_ref[...], s_ref[...], d_ref[...], n1_ref[...], n2_ref[...])
        dyf, _, dxs, dz, ds, dd, dn1, dn2 = vjp((dm_ref[:, :c1], dm_ref[:, c1:]))
        dy_ref[...] = dyf
        dxs_ref[...] = dxs
        dz_ref[...] = dz.astype(BF16)
        ds_ref[...] = ds
        _acc_rows((dd_ref, dn1_ref, dn2_ref), (dd, dn1, dn2), pl.program_id(0) == 0)

    row = lambda i: (i, 0)
    fix = lambda i: (0, 0)
    return pl.pallas_call(
        body, name=name, grid=(n // tm,),
        in_specs=[pl.BlockSpec((None, tm, c1), lambda i: (0, i, 0)), pl.BlockSpec((None, tm, c1), lambda i: (1, i, 0)),
                  pl.BlockSpec((tm, c1), row), pl.BlockSpec((tm, c1), row), pl.BlockSpec((tm, c2), row),
                  pl.BlockSpec((LANES, 1), fix), pl.BlockSpec((1, c1), fix), pl.BlockSpec((1, c2), fix),
                  pl.BlockSpec((tm, c1 + c2), row)],
        out_specs=[pl.BlockSpec((tm, c1), row), pl.BlockSpec((tm, c1), row), pl.BlockSpec((tm, c1), row), pl.BlockSpec((tm, c2), row),
                   pl.BlockSpec((LANES, 1), fix), pl.BlockSpec((1, c1), fix), pl.BlockSpec((1, c2), fix)],
        out_shape=[SDS((n, c1), F32), SDS((n, c1), F32), SDS((n, c1), BF16), SDS((n, c2), F32),
                   SDS((LANES, 1), F32), SDS((1, c1), F32), SDS((1, c2), F32)],
        compiler_params=_cp(("arbitrary",), VMEM_LIMIT),
    )(y2, y2, xs, z, s5o, dvec, nw_ssd, nw_s5, dmix)


def _final_loss(h2, nw, tgt, tm, name):
    n, d = h2.shape

    def loss_fn(h, w, t):
        e = _rms(h, w) - t
        return (0.5 / d) * jnp.sum(e * e)

    def body(h_ref, w_ref, t_ref, l_ref, dh_ref, dw_ref):
        loss, (dh, dw) = jax.value_and_grad(loss_fn, argnums=(0, 1))(h_ref[...], w_ref[...], t_ref[...])
        dh_ref[...] = dh
        _acc_rows((l_ref, dw_ref), (jnp.full((1, LANES), loss, F32), dw), pl.program_id(0) == 0)

    row = lambda i: (i, 0)
    fix = lambda i: (0, 0)
    return pl.pallas_call(
        body, name=name, grid=(n // tm,),
        in_specs=[pl.BlockSpec((tm, d), row), pl.BlockSpec((1, d), fix), pl.BlockSpec((tm, d), row)],
        out_specs=[pl.BlockSpec((1, LANES), fix), pl.BlockSpec((tm, d), row), pl.BlockSpec((1, d), fix)],
        out_shape=[SDS((1, LANES), F32), SDS((n, d), F32), SDS((1, d), F32)],
        compiler_params=_cp(("arbitrary",), VMEM_LIMIT),
    )(h2, nw, tgt)


def _norm_bwd(x, nw, dhn, dres, tm, name):
    n, d = x.shape

    def body(x_ref, w_ref, g_ref, r_ref, dx_ref, dw_ref):
        _, vjp = jax.vjp(_rms, x_ref[...], w_ref[...])
        dx, dw = vjp(g_ref[...])
        dx_ref[...] = r_ref[...] + dx
        _acc_rows((dw_ref,), (dw,), pl.program_id(0) == 0)

    row = lambda i: (i, 0)
    fix = lambda i: (0, 0)
    return pl.pallas_call(
        body, name=name, grid=(n // tm,),
        in_specs=[pl.BlockSpec((tm, d), row), pl.BlockSpec((1, d), fix), pl.BlockSpec((tm, d), row), pl.BlockSpec((tm, d), row)],
        out_specs=[pl.BlockSpec((tm, d), row), pl.BlockSpec((1, d), fix)],
        out_shape=[SDS((n, d), F32), SDS((1, d), F32)],
        compiler_params=_cp(("arbitrary",), VMEM_LIMIT),
    )(x, nw, dhn, dres)


def _row_tile(n, cap=512):
    for t in range(min(cap, n) // 8 * 8, 7, -8):
        if n % t == 0:
            return t
    return n


def _sum_lead(a, name):
    kk, n, c = a.shape
    tm = _row_tile(n)

    def body(a_ref, o_ref):
        acc = a_ref[0].astype(F32)
        for i in range(1, kk):
            acc = acc + a_ref[i].astype(F32)
        o_ref[...] = acc

    return pl.pallas_call(
        body, name=name, grid=(n // tm,),
        in_specs=[pl.BlockSpec((kk, tm, c), lambda i: (0, i, 0))],
        out_specs=pl.BlockSpec((tm, c), lambda i: (i, 0)), out_shape=SDS((n, c), F32),
        compiler_params=_cp(("arbitrary",), VMEM_LIMIT),
    )(a)


def _adamw(w, g, m, v, name):
    n, c = w.shape
    tm = _row_tile(n)

    def body(w_ref, g_ref, m_ref, v_ref, d_ref, nm_ref, nv_ref):
        gv = g_ref[...]
        mn = ADAM_B1 * m_ref[...] + (1.0 - ADAM_B1) * gv
        vn = ADAM_B2 * v_ref[...] + (1.0 - ADAM_B2) * jnp.square(gv)
        m_hat = mn / (1.0 - ADAM_B1 ** ADAM_STEP)
        v_hat = vn / (1.0 - ADAM_B2 ** ADAM_STEP)
        d_ref[...] = -ADAM_LR * (m_hat / (jnp.sqrt(v_hat) + ADAM_EPS) + ADAM_WD * w_ref[...])
        nm_ref[...] = mn
        nv_ref[...] = vn

    spec = pl.BlockSpec((tm, c), lambda i: (i, 0))
    return pl.pallas_call(
        body, name=name, grid=(n // tm,), in_specs=[spec] * 4, out_specs=[spec] * 3,
        out_shape=[SDS((n, c), F32)] * 3, compiler_params=_cp(("arbitrary",), VMEM_LIMIT),
    )(w, g, m, v)


ANY = pl.BlockSpec(memory_space=pl.ANY)


def _me():
    return lax.axis_index("x"), lax.axis_index("y"), lax.axis_index("c")


def _gather_xy(split, whole, name):
    ns, cnt = len(split), len(split) + len(whole)

    def body(*refs):
        src, dst = refs[:cnt], refs[cnt:2 * cnt]
        send, recv = refs[2 * cnt:]
        x, y, c = _me()
        mine = 2 * x + y
        chips = [(1 - x, y), (x, 1 - y), (1 - x, 1 - y)]

        def ici(a, j, slot):
            px, py = chips[j]
            if a < ns:
                s_ref, d_ref = src[a].at[c], dst[a].at[slot].at[c]
            else:
                s_ref, d_ref = src[a], dst[a].at[slot]
            return pltpu.make_async_remote_copy(s_ref, d_ref, send.at[3 * a + j], recv.at[3 * a + j],
                                                device_id=(px, py, c), device_id_type=MESH)

        def d2d(a, j, half):
            px, py = chips[j]
            ref = dst[a].at[2 * px + py].at[half]
            return pltpu.make_async_remote_copy(ref, ref, send.at[3 * cnt + 3 * a + j], recv.at[3 * cnt + 3 * a + j],
                                                device_id=(x, y, 1 - c), device_id_type=MESH)

        def own(a):
            return pltpu.make_async_remote_copy(src[a], dst[a].at[mine], send.at[nsem - cnt + a], recv.at[nsem - cnt + a],
                                                device_id=(x, y, 1 - c), device_id_type=MESH)

        started = []
        for a in range(cnt):
            cp = own(a)
            cp.start()
            started.append(cp)
            for j in range(3):
                cp = ici(a, j, mine)
                cp.start()
                started.append(cp)
        for a in range(cnt):
            for j, (px, py) in enumerate(chips):
                ici(a, j, 2 * px + py).wait_recv()
                if a < ns:
                    cp = d2d(a, j, c)
                    cp.start()
                    started.append(cp)
        for a in range(ns):
            for j in range(3):
                d2d(a, j, 1 - c).wait_recv()
        for a in range(cnt):
            own(a).wait_recv()
        for cp in started:
            cp.wait_send()

    nsem = 3 * cnt + 3 * ns + cnt
    return pl.pallas_call(
        body, name=name, in_specs=[ANY] * cnt, out_specs=[ANY] * cnt,
        out_shape=[SDS((4,) + s.shape, s.dtype) for s in split + whole],
        scratch_shapes=[pltpu.SemaphoreType.DMA((nsem,)), pltpu.SemaphoreType.DMA((nsem,))],
    )(*split, *whole)


HBM = pl.BlockSpec(memory_space=pltpu.HBM)
SEM = pl.BlockSpec(memory_space=pltpu.SEMAPHORE)
DATAFLOW = pltpu.SideEffectType.DATAFLOW_SIDE_EFFECTING


def _whole_copies(srcs, dsts, sends, recvs):
    x, y, c = _me()
    peers = [(1 - x, y, c), (x, 1 - y, c), (1 - x, 1 - y, c), (x, y, 1 - c)]
    return [pltpu.make_async_remote_copy(srcs[a], dsts[a].at[2 * x + y], sends[4 * a + j], recvs[4 * a + j],
                                         device_id=peer, device_id_type=MESH)
            for a in range(len(srcs)) for j, peer in enumerate(peers)]


def _scatter_copies(srcs, dsts, sends, recvs):
    x, y, c = _me()
    chips = [(1 - x, y), (x, 1 - y), (1 - x, 1 - y)]
    return [pltpu.make_async_remote_copy(srcs[a].at[2 * px + py], dsts[a].at[2 * x + y], sends[3 * a + j], recvs[3 * a + j],
                                         device_id=(px, py, c), device_id_type=MESH)
            for a in range(len(srcs)) for j, (px, py) in enumerate(chips)]


def _copies_start(copies, per, shards, after, name):
    cnt = len(shards)
    ncp = per * cnt

    def body(*refs):
        srcs, lands = refs[:cnt], refs[cnt:2 * cnt]
        outs = refs[2 * cnt + 1:]
        for cp in copies(srcs, lands, outs[:ncp], outs[ncp:2 * ncp]):
            cp.start()
        outs[-1][...] = jnp.zeros_like(outs[-1])

    lands = [lax.empty((4,) + (s.shape if per == 4 else s.shape[1:]), s.dtype) for s in shards]
    ops = [pltpu.with_memory_space_constraint(a, pltpu.HBM) for a in list(shards) + lands]
    res = pl.pallas_call(
        body, name=name, in_specs=[HBM] * (2 * cnt) + [ANY],
        out_shape=tuple([pltpu.SemaphoreType.DMA(())] * (2 * ncp) + [pltpu.HBM(a.shape, a.dtype) for a in ops] + [SDS((8, LANES), F32)]),
        out_specs=tuple([SEM] * (2 * ncp) + [HBM] * (2 * cnt) + [pl.BlockSpec(memory_space=pltpu.VMEM)]),
        input_output_aliases={i: 2 * ncp + i for i in range(2 * cnt)},
        compiler_params=pltpu.CompilerParams(has_side_effects=DATAFLOW),
    )(*ops, after)
    return res[:2 * ncp], res[2 * ncp:2 * ncp + cnt], res[2 * ncp + cnt:2 * ncp + 2 * cnt], res[-1]


def _copies_wait(copies, per, handle, after, name):
    sems, srcs, lands, _ = handle
    cnt = len(srcs)
    ncp = per * cnt

    def body(*refs):
        sem_refs = refs[2 * cnt:2 * cnt + 2 * ncp]
        for cp in copies(refs[:cnt], refs[cnt:2 * cnt], sem_refs[:ncp], sem_refs[ncp:]):
            cp.wait_send()
            cp.wait_recv()

    res = pl.pallas_call(
        body, name=name, in_specs=[HBM] * (2 * cnt) + [SEM] * (2 * ncp) + [ANY],
        out_shape=tuple(pltpu.HBM(a.shape, a.dtype) for a in list(srcs) + list(lands)),
        out_specs=tuple([HBM] * (2 * cnt)), input_output_aliases={i: i for i in range(2 * cnt)},
        compiler_params=pltpu.CompilerParams(has_side_effects=DATAFLOW),
    )(*srcs, *lands, *sems, after)
    return list(res[cnt:])


def _swap_sibling(parts, pick, name):
    cnt = len(parts)

    def body(*refs):
        src, dst = refs[:cnt], refs[cnt:2 * cnt]
        send, recv = refs[2 * cnt:]
        x, y, c = _me()
        cps = []
        for a in range(cnt):
            cp = pltpu.make_async_remote_copy(src[a].at[1 - c] if pick else src[a], dst[a], send.at[a], recv.at[a],
                                              device_id=(x, y, 1 - c), device_id_type=MESH)
            cp.start()
            cps.append(cp)
        for cp in cps:
            cp.wait()

    return pl.pallas_call(
        body, name=name, in_specs=[ANY] * cnt, out_specs=[ANY] * cnt,
        out_shape=[SDS(p.shape[1:] if pick else p.shape, p.dtype) for p in parts],
        scratch_shapes=[pltpu.SemaphoreType.DMA((cnt,)), pltpu.SemaphoreType.DMA((cnt,))],
    )(*parts)


def _scatter_xy(parts, name):
    cnt = len(parts)

    def body(*refs):
        src, dst = refs[:cnt], refs[cnt:2 * cnt]
        send, recv, loc = refs[2 * cnt:]
        x, y, c = _me()
        mine = 2 * x + y
        chips = [(1 - x, y), (x, 1 - y), (1 - x, 1 - y)]
        local = []
        for a in range(cnt):
            cp = pltpu.make_async_copy(src[a].at[mine], dst[a].at[mine], loc.at[a])
            cp.start()
            local.append(cp)
        sends = []
        for a in range(cnt):
            for j, (px, py) in enumerate(chips):
                cp = pltpu.make_async_remote_copy(src[a].at[2 * px + py], dst[a].at[mine], send.at[3 * a + j], recv.at[3 * a + j],
                                                  device_id=(px, py, c), device_id_type=MESH)
                cp.start()
                sends.append(cp)
        for a in range(cnt):
            for j, (px, py) in enumerate(chips):
                pltpu.make_async_remote_copy(src[a].at[mine], dst[a].at[2 * px + py], send.at[3 * a + j], recv.at[3 * a + j],
                                             device_id=(px, py, c), device_id_type=MESH).wait_recv()
        for cp in sends:
            cp.wait_send()
        for cp in local:
            cp.wait()

    return pl.pallas_call(
        body, name=name, in_specs=[ANY] * cnt, out_specs=[ANY] * cnt,
        out_shape=[SDS(p.shape, p.dtype) for p in parts],
        scratch_shapes=[pltpu.SemaphoreType.DMA((3 * cnt,)), pltpu.SemaphoreType.DMA((3 * cnt,)), pltpu.SemaphoreType.DMA((cnt,))],
    )(*parts)


def _bcast_all(buf, name):
    def body(src, dst, send, recv, loc):
        x, y, c = _me()
        mine = 4 * x + 2 * y + c
        own = pltpu.make_async_copy(src, dst.at[mine], loc)
        own.start()
        sends = []
        for k in range(1, 8):
            px, py, pc = x ^ (k >> 2), y ^ ((k >> 1) & 1), c ^ (k & 1)
            cp = pltpu.make_async_remote_copy(src, dst.at[mine], send.at[k - 1], recv.at[k - 1],
                                              device_id=(px, py, pc), device_id_type=MESH)
            cp.start()
            sends.append(cp)
        for k in range(1, 8):
            px, py, pc = x ^ (k >> 2), y ^ ((k >> 1) & 1), c ^ (k & 1)
            pltpu.make_async_remote_copy(src, dst.at[4 * px + 2 * py + pc], send.at[k - 1], recv.at[k - 1],
                                         device_id=(px, py, pc), device_id_type=MESH).wait_recv()
        for cp in sends:
            cp.wait_send()
        own.wait()

    return pl.pallas_call(
        body, name=name, in_specs=[ANY], out_specs=ANY, out_shape=SDS((8,) + buf.shape, buf.dtype),
        scratch_shapes=[pltpu.SemaphoreType.DMA((7,)), pltpu.SemaphoreType.DMA((7,)), pltpu.SemaphoreType.DMA(())],
    )(buf)


def _sum_slots(recv, own, chip, name):
    kk, n, c = recv.shape
    tm = _row_tile(n)

    def body(chip_ref, r_ref, o_ref, out_ref):
        acc = None
        for j in range(kk):
            v = jnp.where(chip_ref[0] == j, o_ref[...], r_ref[j]).astype(F32)
            acc = v if acc is None else acc + v
        out_ref[...] = acc

    grid_spec = pltpu.PrefetchScalarGridSpec(
        num_scalar_prefetch=1, grid=(n // tm,),
        in_specs=[pl.BlockSpec((kk, tm, c), lambda i, chip_ref: (0, i, 0)),
                  pl.BlockSpec((None, tm, c), lambda i, chip_ref: (chip_ref[0], i, 0))],
        out_specs=pl.BlockSpec((tm, c), lambda i, chip_ref: (i, 0)))
    return pl.pallas_call(body, name=name, grid_spec=grid_spec, out_shape=SDS((n, c), F32),
                          compiler_params=_cp(("arbitrary",), VMEM_LIMIT))(chip.reshape(1), recv, own)


def _add_half(parts, got, core, dtype, name):
    shp = got.shape
    a2, b2 = parts.reshape(2, -1, shp[-1]), got.reshape(-1, shp[-1])
    n, c = b2.shape
    tm = _row_tile(n, 256)

    def body(core_ref, a_ref, b_ref, o_ref):
        o_ref[...] = (a_ref[...] + b_ref[...]).astype(dtype)

    spec = pl.BlockSpec((tm, c), lambda i, core_ref: (i, 0))
    grid_spec = pltpu.PrefetchScalarGridSpec(
        num_scalar_prefetch=1, grid=(n // tm,),
        in_specs=[pl.BlockSpec((None, tm, c), lambda i, core_ref: (core_ref[0], i, 0)), spec], out_specs=spec)
    return pl.pallas_call(body, name=name, grid_spec=grid_spec, out_shape=SDS((n, c), dtype),
                          compiler_params=_cp(("arbitrary",), VMEM_LIMIT))(core.reshape(1), a2, b2).reshape(shp)


def _x_layout(u, name):
    n, c = u.shape
    t, ch = S5_T, S5_CH
    gb = LANES // ch
    rows = min(64, n // t)

    def body(u_ref, o_ref):
        for s in range(t):
            us = u_ref[pl.ds(s, rows, stride=t), :]
            for g in range(gb):
                o_ref[g, :, ch * s:ch * (s + 1)] = us[:, ch * g:ch * (g + 1)]

    return pl.pallas_call(
        body, name=name, grid=(n // (rows * t), c // LANES),
        in_specs=[pl.BlockSpec((rows * t, LANES), lambda i, j: (i, j))],
        out_specs=pl.BlockSpec((gb, rows, t * ch), lambda i, j: (j, i, 0)),
        out_shape=SDS((c // ch, n // t, t * ch), F32),
        compiler_params=_cp(("arbitrary", "arbitrary"), VMEM_LIMIT),
    )(u)


def _token_layout(xg, name):
    ng, r, w = xg.shape
    t, ch = S5_T, S5_CH
    gb = LANES // ch
    rows = min(64, r)

    def body(x_ref, o_ref):
        for s in range(t):
            parts = [x_ref[g, :, ch * s:ch * (s + 1)].astype(F32) for g in range(gb)]
            o_ref[pl.ds(s, rows, stride=t), :] = jnp.concatenate(parts, axis=1)

    return pl.pallas_call(
        body, name=name, grid=(r // rows, ng // gb),
        in_specs=[pl.BlockSpec((gb, rows, w), lambda i, j: (j, i, 0))],
        out_specs=pl.BlockSpec((rows * t, LANES), lambda i, j: (i, j)),
        out_shape=SDS((r * t, ng * ch), F32),
        compiler_params=_cp(("arbitrary", "arbitrary"), VMEM_LIMIT),
    )(xg)


def _pad_lanes(a, lanes=LANES):
    return jnp.pad(a, ((0, 0), (0, lanes - a.shape[1])))


def _local_step(x, tgt, p, bl, late, early):
    p = dict(p)
    n, d = x.shape
    sw = SSD_HEADS * SSD_HEAD_DIM
    gn = SSD_GROUPS * SSD_STATE
    tm = min(n, 512)
    tm_ffn = min(n, 256)
    nck = n // bl // S5_T
    s5w = S5_GROUPS * S5_CH

    w_in = p["w_in"]
    o1, o2, o3, o4 = sw, sw + sw, sw + sw + gn, sw + sw + 2 * gn
    w_z, w_xs, w_b, w_c = w_in[:, :o1], w_in[:, o1:o2], w_in[:, o2:o3], w_in[:, o3:o4]
    w_dt = jnp.concatenate([_pad_lanes(w_in[:, o4:o4 + SSD_HEADS]), _pad_lanes(w_in[:, o4 + SSD_HEADS:o4 + 2 * SSD_HEADS])], 1)
    w_u = w_in[:, o4 + 2 * SSD_HEADS:]
    in_ws = [w_z, w_xs, w_b, w_c, w_dt, w_u]
    cw, cb_ = p["ssd_conv_w"], p["ssd_conv_b"]
    conv_parts = [(cw[:, :sw], cb_[:, :sw]), (cw[:, sw:sw + gn], cb_[:, sw:sw + gn]), (cw[:, sw + gn:], cb_[:, sw + gn:])]
    alog2 = jnp.stack([_pad_lanes(p["ssd_a_log_fwd"]), _pad_lanes(p["ssd_a_log_bwd"])])
    dtb2 = jnp.stack([_pad_lanes(p["ssd_dt_bias_fwd"]), _pad_lanes(p["ssd_dt_bias_bwd"])])
    dvec = _pad_lanes(p["ssd_d"]).reshape(LANES, 1)

    hn, z, xs_pre, b_pre, c_pre, dtr, u = _norm_matmul(x, p["norm_mix_w"], in_ws, tm, "in_proj")
    pres = [xs_pre, b_pre, c_pre]
    acts = [_conv_silu(pre, w, b, bl, min(256, pre.shape[1]), f"ssd_conv_{i}") for i, (pre, (w, b)) in enumerate(zip(pres, conv_parts))]
    xs_a, b_a, c_a = acts
    y2, saved = _ssd_scan(xs_a, b_a, c_a, dtr, alog2, dtb2, bl, "ssd_scan")

    def col(a):
        return a.reshape(a.shape + (1,))

    s5_params = [
        col(p["s5_lambda_re_fwd"]), col(p["s5_lambda_im_fwd"]), p["s5_log_step_fwd"].reshape(S5_GROUPS, 1, 1), p["s5_c_re_fwd"], p["s5_c_im_fwd"],
        col(p["s5_lambda_re_bwd"]), col(p["s5_lambda_im_bwd"]), p["s5_log_step_bwd"].reshape(S5_GROUPS, 1, 1), p["s5_c_re_bwd"], p["s5_c_im_bwd"],
        p["s5_b_re"], p["s5_b_im"], col(p["s5_d"].reshape(S5_GROUPS, S5_CH)),
        p["s5_glu_w"][:, :, :S5_CH], p["s5_glu_w"][:, :, S5_CH:], col(p["s5_glu_b"][:, :S5_CH]), col(p["s5_glu_b"][:, S5_CH:]),
    ]
    s5_args = [_x_layout(u, "s5_u_blocks")] + s5_params
    s5o = _token_layout(_s5_fwd(s5_args, nck, "s5_fwd"), "s5_y_tokens")
    ymix = _mix(y2, xs_a, z, s5o, dvec, p["ssd_norm_w"], p["s5_norm_w"], tm, "mix")
    p["w_out"], p["w_up"], p["w_down"] = late(ymix)
    dff = p["w_down"].shape[0]
    h1 = _matmul_res(ymix, p["w_out"], x, tm, "out_proj")
    w_up = p["w_up"]
    hn2, up_v, up_g = _norm_matmul(h1, p["norm_ffn_w"], [w_up[:, :dff], w_up[:, dff:]], tm_ffn, "ffn_up")
    fw, fb = p["ffn_conv_w"], p["ffn_conv_b"]
    act = _conv_glu(up_v, up_g, fw[:, :dff], fw[:, dff:], fb[:, :dff], fb[:, dff:], bl, 256, "ffn_conv")
    h2 = _matmul_res(act, p["w_down"], h1, tm, "ffn_down")
    loss, dh2, g_nfw = _final_loss(h2, p["norm_final_w"].reshape(1, d), tgt, tm, "final_loss")

    g = {"norm_final_w": g_nfw.reshape(d)}
    g["w_down"] = _matmul_tn(act, dh2, tm, d, "ffn_down_dw")
    dact = _matmul_nt([dh2], [p["w_down"]], tm, "ffn_down_dx")
    dup_v, dup_g, dwv, dwg, dbv, dbg = _conv_glu_bwd(up_v, up_g, fw[:, :dff], fw[:, dff:], fb[:, :dff], fb[:, dff:], dact, bl, 256, "ffn_conv_bwd")
    g["ffn_conv_w"] = jnp.concatenate([dwv, dwg], 1)
    g["ffn_conv_b"] = jnp.concatenate([dbv, dbg], 1)
    g["w_up"] = jnp.concatenate([_matmul_tn(hn2, dup_v, tm, dff // 2, "ffn_up_dw_v"), _matmul_tn(hn2, dup_g, tm, dff // 2, "ffn_up_dw_g")], 1)
    dhn2 = _matmul_nt([dup_v, dup_g], [w_up[:, :dff], w_up[:, dff:]], tm_ffn, "ffn_up_dx")
    g["ffn_reduce"] = early(g["w_up"], g["w_down"])
    behind = g["ffn_reduce"][1][3][0, 0]
    dh1, g["norm_ffn_w"] = _norm_bwd(h1, p["norm_ffn_w"] + behind, dhn2, dh2, tm, "ffn_norm_bwd")
    g["w_out"] = _matmul_tn(ymix, dh1, tm, d, "out_proj_dw")
    dmix = _matmul_nt([dh1], [p["w_out"]], tm, "out_proj_dx")
    dyssd, dxs_gate, dz, ds5o, g_d, g["ssd_norm_w"], g["s5_norm_w"] = _mix_bwd(
        y2, xs_a, z, s5o, dvec, p["ssd_norm_w"], p["s5_norm_w"], dmix, tm, "mix_bwd")
    g["ssd_d"] = g_d[:SSD_HEADS].reshape(1, SSD_HEADS)
    s5g = _s5_bwd(s5_args, _x_layout(ds5o, "s5_dy_blocks"), nck, "s5_bwd")
    du = _token_layout(s5g[0], "s5_du_tokens")
    (g["s5_lambda_re_fwd"], g["s5_lambda_im_fwd"], g["s5_log_step_fwd"], g["s5_c_re_fwd"], g["s5_c_im_fwd"],
     g["s5_lambda_re_bwd"], g["s5_lambda_im_bwd"], g["s5_log_step_bwd"], g["s5_c_re_bwd"], g["s5_c_im_bwd"],
     g["s5_b_re"], g["s5_b_im"], g_s5d, g_wv, g_wg, g_bv, g_bg) = s5g[1:]
    for k_ in ("s5_lambda_re_fwd", "s5_lambda_im_fwd", "s5_lambda_re_bwd", "s5_lambda_im_bwd"):
        g[k_] = g[k_].reshape(S5_GROUPS, S5_STATE)
    for k_ in ("s5_log_step_fwd", "s5_log_step_bwd"):
        g[k_] = g[k_].reshape(S5_GROUPS)
    g["s5_d"] = g_s5d.reshape(1, s5w)
    g["s5_glu_w"] = jnp.concatenate([g_wv, g_wg], 2)
    g["s5_glu_b"] = jnp.concatenate([g_bv.reshape(S5_GROUPS, S5_CH), g_bg.reshape(S5_GROUPS, S5_CH)], 1)
    dxs2, dbm2, dcm2, ddtr, dal2, ddb2 = _ssd_scan_bwd(xs_a, b_a, c_a, dtr, alog2, dtb2, saved, dyssd, bl, "ssd_scan_bwd")
    g["ssd_a_log_fwd"], g["ssd_a_log_bwd"] = dal2[0, :, :SSD_HEADS], dal2[1, :, :SSD_HEADS]
    g["ssd_dt_bias_fwd"], g["ssd_dt_bias_bwd"] = ddb2[0, :, :SSD_HEADS], ddb2[1, :, :SSD_HEADS]
    cots = [[(dxs2, 0), (dxs2, 1), (dxs_gate, None)], [(dbm2, 0), (dbm2, 1)], [(dcm2, 0), (dcm2, 1)]]
    dpres, dcw, dcb = [], [], []
    for i, (pre, (w, b), cot) in enumerate(zip(pres, conv_parts, cots)):
        dp, dw_, db_ = _conv_silu_bwd(pre, w, b, cot, bl, min(256, pre.shape[1]), f"ssd_conv_bwd_{i}")
        dpres.append(dp)
        dcw.append(dw_)
        dcb.append(db_)
    g["ssd_conv_w"] = jnp.concatenate(dcw, 1)
    g["ssd_conv_b"] = jnp.concatenate(dcb, 1)
    dprojs = [dz, dpres[0], dpres[1], dpres[2], ddtr, du]
    dws = [_matmul_tn(hn, dpj, tm, dpj.shape[1], f"in_proj_dw_{i}") for i, dpj in enumerate(dprojs)]
    dws[4] = jnp.concatenate([dws[4][:, :SSD_HEADS], dws[4][:, LANES:LANES + SSD_HEADS]], 1)
    g["w_in"] = jnp.concatenate(dws, 1)
    dhn = _matmul_nt(dprojs, in_ws, tm, "in_proj_dx")
    grad_x, g["norm_mix_w"] = _norm_bwd(x, p["norm_mix_w"], dhn, dh1, tm, "mix_norm_bwd")
    return loss, grad_x, g


_WEIGHTS = ['norm_mix_w', 'w_in', 'ssd_conv_w', 'ssd_conv_b', 'ssd_dt_bias_fwd', 'ssd_dt_bias_bwd', 'ssd_a_log_fwd', 'ssd_a_log_bwd',
            'ssd_d', 'ssd_norm_w', 's5_lambda_re_fwd', 's5_lambda_im_fwd', 's5_log_step_fwd', 's5_lambda_re_bwd', 's5_lambda_im_bwd',
            's5_log_step_bwd', 's5_b_re', 's5_b_im', 's5_c_re_fwd', 's5_c_im_fwd', 's5_c_re_bwd', 's5_c_im_bwd', 's5_d', 's5_glu_w',
            's5_glu_b', 's5_norm_w', 'w_out', 'norm_ffn_w', 'ffn_w_up', 'ffn_conv_w', 'ffn_conv_b', 'ffn_w_down', 'norm_final_w']
_BIG = ('w_in', 'w_out', 'ffn_w_up', 'ffn_w_down')
_CONV = ('ssd_conv_w', 'ffn_conv_w')


def _pack(arrs):
    flat = jnp.concatenate([a.reshape(-1) for a in arrs])
    rows = -(-flat.shape[0] // (64 * LANES)) * 64
    return jnp.pad(flat, (0, rows * LANES - flat.shape[0])).reshape(rows, LANES)


def _unpack(buf, shapes):
    flat = buf.reshape(-1)
    out, off = [], 0
    for shp in shapes:
        size = math.prod(shp)
        out.append(flat[off:off + size].reshape(shp))
        off += size
    return out


def kernel(x, norm_mix_w, w_in, ssd_conv_w, ssd_conv_b, ssd_dt_bias_fwd, ssd_dt_bias_bwd, ssd_a_log_fwd, ssd_a_log_bwd, ssd_d, ssd_norm_w, s5_lambda_re_fwd, s5_lambda_im_fwd, s5_log_step_fwd, s5_lambda_re_bwd, s5_lambda_im_bwd, s5_log_step_bwd, s5_b_re, s5_b_im, s5_c_re_fwd, s5_c_im_fwd, s5_c_re_bwd, s5_c_im_bwd, s5_d, s5_glu_w, s5_glu_b, s5_norm_w, w_out, norm_ffn_w, ffn_w_up, ffn_conv_w, ffn_conv_b, ffn_w_down, norm_final_w, loss_target, m_norm_mix_w, m_w_in, m_ssd_conv_w, m_ssd_conv_b, m_ssd_dt_bias_fwd, m_ssd_dt_bias_bwd, m_ssd_a_log_fwd, m_ssd_a_log_bwd, m_ssd_d, m_ssd_norm_w, m_s5_lambda_re_fwd, m_s5_lambda_im_fwd, m_s5_log_step_fwd, m_s5_lambda_re_bwd, m_s5_lambda_im_bwd, m_s5_log_step_bwd, m_s5_b_re, m_s5_b_im, m_s5_c_re_fwd, m_s5_c_im_fwd, m_s5_c_re_bwd, m_s5_c_im_bwd, m_s5_d, m_s5_glu_w, m_s5_glu_b, m_s5_norm_w, m_w_out, m_norm_ffn_w, m_ffn_w_up, m_ffn_conv_w, m_ffn_conv_b, m_ffn_w_down, m_norm_final_w, v_norm_mix_w, v_w_in, v_ssd_conv_w, v_ssd_conv_b, v_ssd_dt_bias_fwd, v_ssd_dt_bias_bwd, v_ssd_a_log_fwd, v_ssd_a_log_bwd, v_ssd_d, v_ssd_norm_w, v_s5_lambda_re_fwd, v_s5_lambda_im_fwd, v_s5_log_step_fwd, v_s5_lambda_re_bwd, v_s5_lambda_im_bwd, v_s5_log_step_bwd, v_s5_b_re, v_s5_b_im, v_s5_c_re_fwd, v_s5_c_im_fwd, v_s5_c_re_bwd, v_s5_c_im_bwd, v_s5_d, v_s5_glu_w, v_s5_glu_b, v_s5_norm_w, v_w_out, v_norm_ffn_w, v_ffn_w_up, v_ffn_conv_w, v_ffn_conv_b, v_ffn_w_down, v_norm_final_w):
    args = dict(locals())
    w = {k_: args[k_] for k_ in _WEIGHTS}
    m = {k_: args["m_" + k_] for k_ in _WEIGHTS}
    v = {k_: args["v_" + k_] for k_ in _WEIGHTS}
    bl, sl, d = x.shape
    chip = 2 * lax.axis_index("x") + lax.axis_index("y")
    core = lax.axis_index("c")

    first = w["w_in"][0].astype(BF16)
    g_in, g_scw, g_fcw = _gather_xy([first.reshape(2, first.shape[0] // 2, first.shape[1])], [w[k_][0] for k_ in _CONV], "gather_first")
    g_in = g_in.reshape((4,) + first.shape)
    handle = _copies_start(_whole_copies, 4, [w[k_][0].astype(BF16) for k_ in _BIG[1:]], g_scw, "gather_rest_start")

    def cols(a):
        return jnp.moveaxis(a, 0, 1).reshape(a.shape[1], 4 * a.shape[2])

    p = {k_: (w[k_][0] if w[k_].ndim >= 3 else w[k_]) for k_ in _WEIGHTS if k_ not in _BIG + _CONV}
    p["norm_mix_w"] = p["norm_mix_w"] + handle[3][0, 0]
    p["w_in"] = cols(g_in)
    p["ssd_conv_w"], p["ffn_conv_w"] = cols(g_scw), cols(g_fcw)

    def late(after):
        g_out, g_up, g_down = _copies_wait(_whole_copies, 4, handle, after, "gather_rest_wait")
        return g_out.reshape(-1, g_out.shape[2]), cols(g_up), g_down.reshape(-1, g_down.shape[2])

    def owner_major(a, k_):
        r, c = w[k_].shape[1:]
        if a.shape[0] == r:
            a = jnp.moveaxis(a.reshape(r, 4, c), 1, 0)
        else:
            a = a.reshape(4, r, c)
        return a.reshape(4, 2, r // 2, c)

    def early(g_up, g_down):
        parts = [jnp.moveaxis(owner_major(a, k_), 1, 0) for a, k_ in ((g_up, "ffn_w_up"), (g_down, "ffn_w_down"))]
        got = _swap_sibling(parts, True, "reduce_sibling_ffn")
        sums = [_add_half(pt, gt, core, BF16, f"reduce_add_ffn_{i}") for i, (pt, gt) in enumerate(zip(parts, got))]
        return sums, _copies_start(_scatter_copies, 3, sums, w["norm_ffn_w"], "reduce_chips_ffn_start")

    loss, grad_x, g = _local_step(x.reshape(bl * sl, d), loss_target.reshape(bl * sl, d), p, bl, late, early)
    ffn_sums, ffn_handle = g.pop("ffn_reduce")
    ffn_recv = _copies_wait(_scatter_copies, 3, ffn_handle, grad_x, "reduce_chips_ffn_wait")
    ffn_halves = [_sum_slots(rc, sm, chip, f"reduce_sum_ffn_{i}") for i, (rc, sm) in enumerate(zip(ffn_recv, ffn_sums))]

    small = [k_ for k_ in _WEIGHTS if k_ not in _BIG]
    small_full_shapes = [g[k_].shape for k_ in small]
    buf = _pack([g[k_] for k_ in small] + [loss[0, :1]])
    rest = _BIG[:2]
    parts = [jnp.moveaxis(owner_major(g[k_], k_), 1, 0) for k_ in rest]
    parts.append(jnp.moveaxis(buf.reshape(4, 2, -1, LANES), 1, 0))
    got = _swap_sibling(parts, True, "reduce_sibling")
    chip_sums = [_add_half(pt, gt, core, BF16 if i < len(rest) else F32, f"reduce_add_{i}") for i, (pt, gt) in enumerate(zip(parts, got))]
    from_chips = _scatter_xy(chip_sums, "reduce_chips")
    rest_halves = [_sum_lead(a.reshape(4, -1, a.shape[-1]), f"reduce_sum_{i}") for i, a in enumerate(from_chips)]
    halves = rest_halves[:-1] + ffn_halves + rest_halves[-1:]
    other = _swap_sibling(halves[:-1], False, "reduce_join")
    big_grad = {}
    for k_, own_half, sib_half in zip(_BIG, halves, other):
        south = core == 0
        full = jnp.stack([jnp.where(south, own_half, sib_half), jnp.where(south, sib_half, own_half)])
        big_grad[k_] = full.reshape((1,) + w[k_].shape[1:])
    tot = _bcast_all(halves[-1], "reduce_small").reshape(buf.shape)
    unp = _unpack(tot, small_full_shapes + [(1,)])
    small_grad = dict(zip(small, unp[:-1]))
    loss_out = unp[-1].reshape(())
    for k_ in _CONV:
        cshard = w[k_].shape[2]
        small_grad[k_] = lax.dynamic_slice_in_dim(small_grad[k_], chip * cshard, cshard, 1)

    grads, deltas, new_m, new_v = {}, {}, {}, {}
    for k_ in _BIG:
        shp = w[k_].shape
        grads[k_] = big_grad[k_]
        dl, nm, nv = _adamw(w[k_][0], big_grad[k_][0], m[k_][0], v[k_][0], f"adamw_{k_}")
        deltas[k_], new_m[k_], new_v[k_] = dl.reshape(shp), nm.reshape(shp), nv.reshape(shp)
    sw_ = _pack([w[k_] for k_ in small])
    sg_ = _pack([small_grad[k_] for k_ in small])
    sm_ = _pack([m[k_] for k_ in small])
    sv_ = _pack([v[k_] for k_ in small])
    dl, nm, nv = _adamw(sw_, sg_, sm_, sv_, "adamw_small")
    shapes = [w[k_].shape for k_ in small]
    for k_, a, b, c_ in zip(small, _unpack(dl, shapes), _unpack(nm, shapes), _unpack(nv, shapes)):
        deltas[k_], new_m[k_], new_v[k_] = a, b, c_
        grads[k_] = small_grad[k_].reshape(w[k_].shape)
    return (loss_out, grad_x.reshape(bl, sl, d), *[grads[k_] for k_ in _WEIGHTS], *[deltas[k_] for k_ in _WEIGHTS],
            *[new_m[k_] for k_ in _WEIGHTS], *[new_v[k_] for k_ in _WEIGHTS])
```
